```python
import math
import jax, jax.numpy as jnp
from jax import lax
import numpy as np

D_MODEL = 2048
BATCH = 8
SEQ = 4096
DEPTH = 2

N_META = 16
N_A = DEPTH // 2
N_B = DEPTH - N_A
HG_KDIM = 128
HG_HEADS = D_MODEL // HG_KDIM
HG_FDIM = HG_HEADS * HG_KDIM
HG_VDIM = D_MODEL // HG_HEADS
CHUNK = 64
FOX_HEADS = 16
FOX_HDIM = D_MODEL // FOX_HEADS
Q_BLOCK = 128
EPS = 1e-6
MASK_VALUE = -1e30

kernel_name = "yoco_hgrn2_fox_meta_hybrid"


def rms_norm(x, g):
    xf = x.astype(jnp.float32)
    y = xf * lax.rsqrt(jnp.mean(xf * xf, axis=-1, keepdims=True) + EPS)
    return (y * g.astype(jnp.float32)).astype(x.dtype)


def hgrn2_mix(h, g_norm, w_in, g_out, w_out, lb):
    bsz, L, _ = h.shape
    u = rms_norm(h, g_norm) @ w_in
    q = u[..., :HG_FDIM]
    f = u[..., HG_FDIM:2 * HG_FDIM]
    i = u[..., 2 * HG_FDIM:2 * HG_FDIM + D_MODEL]
    z = u[..., 2 * HG_FDIM + D_MODEL:]
    q = jax.nn.silu(q).astype(jnp.float32)
    fg = lb + (1.0 - lb) * jax.nn.sigmoid(f.astype(jnp.float32))
    logf = jnp.log(fg)
    k = 1.0 - fg
    v = i.astype(jnp.float32)
    n_pad = CHUNK - N_META
    padf = lambda a: jnp.pad(a, ((0, 0), (n_pad, 0), (0, 0)))
    q, logf, k, v = padf(q), padf(logf), padf(k), padf(v)
    Lp = L + n_pad
    nc = Lp // CHUNK
    to_chunks = lambda a, d: a.reshape(bsz, nc, CHUNK, HG_HEADS, d).transpose(0, 3, 1, 2, 4)
    q = to_chunks(q, HG_KDIM)
    logf = to_chunks(logf, HG_KDIM)
    k = to_chunks(k, HG_KDIM)
    v = to_chunks(v, HG_VDIM)
    b = jnp.cumsum(logf, axis=3)
    b_last = b[..., CHUNK - 1:CHUNK, :]
    b_mid = b[..., CHUNK // 2 - 1:CHUNK // 2, :]
    q_intra = q * jnp.exp(b - b_mid)
    k_intra = k * jnp.exp(b_mid - b)
    causal = jnp.tril(jnp.ones((CHUNK, CHUNK), dtype=bool))
    A = jnp.einsum('bhntd,bhnsd->bhnts', q_intra, k_intra)
    A = jnp.where(causal, A, 0.0)
    o_intra = jnp.einsum('bhnts,bhnse->bhnte', A, v)
    dS = jnp.einsum('bhnsd,bhnse->bhnde', k * jnp.exp(b_last - b), v)
    decay = jnp.exp(b_last[..., 0, :])

    def step(S, inp):
        dec, ds = inp
        return dec[..., None] * S + ds, S

    S0 = jnp.zeros((bsz, HG_HEADS, HG_KDIM, HG_VDIM), jnp.float32)
    _, S_prev = lax.scan(step, S0, (jnp.moveaxis(decay, 2, 0), jnp.moveaxis(dS, 2, 0)))
    S_prev = jnp.moveaxis(S_prev, 0, 2)
    o_inter = jnp.einsum('bhntd,bhnde->bhnte', q * jnp.exp(b), S_prev)
    o = (o_intra + o_inter).transpose(0, 2, 3, 1, 4).reshape(bsz, Lp, HG_HEADS, HG_VDIM)
    o = o[:, n_pad:].astype(h.dtype)
    o = rms_norm(o, g_out.reshape(HG_HEADS, HG_VDIM)).reshape(bsz, L, D_MODEL)
    return (o * jax.nn.silu(z)) @ w_out


def fox_shared_kv(h, g_kv, w_kv, b_f, g_k):
    bsz, L, _ = h.shape
    u = rms_norm(h, g_kv) @ w_kv
    k = u[..., :D_MODEL].reshape(bsz, L, FOX_HEADS, FOX_HDIM)
    v = u[..., D_MODEL:2 * D_MODEL].reshape(bsz, L, FOX_HEADS, FOX_HDIM)
    fl = u[..., 2 * D_MODEL:]
    k = rms_norm(k, g_k)
    logf = jax.nn.log_sigmoid(fl.astype(jnp.float32) + b_f.astype(jnp.float32))
    n_pad = Q_BLOCK - N_META
    k = jnp.pad(k, ((0, 0), (n_pad, 0), (0, 0), (0, 0)))
    v = jnp.pad(v, ((0, 0), (n_pad, 0), (0, 0), (0, 0)))
    logf = jnp.pad(logf, ((0, 0), (n_pad, 0), (0, 0)))
    F = jnp.cumsum(logf, axis=1).transpose(0, 2, 1)
    valid = jnp.arange(L + n_pad) >= n_pad
    return k, v, F, valid


def fox_mix(h, g_norm, w_in, g_q, w_out, k, v, F, valid):
    bsz, L, _ = h.shape
    u = rms_norm(h, g_norm) @ w_in
    q = rms_norm(u[..., :D_MODEL].reshape(bsz, L, FOX_HEADS, FOX_HDIM), g_q)
    z = u[..., D_MODEL:]
    n_pad = Q_BLOCK - N_META
    q = jnp.pad(q, ((0, 0), (n_pad, 0), (0, 0), (0, 0)))
    Lp = L + n_pad
    scale = FOX_HDIM ** -0.5
    outs = []
    for blk in range(Lp // Q_BLOCK):
        s0 = blk * Q_BLOCK
        s1 = s0 + Q_BLOCK
        logits = jnp.einsum('bqhd,bkhd->bhqk', q[:, s0:s1], k[:, :s1]).astype(jnp.float32) * scale
        logits = logits + F[:, :, s0:s1, None] - F[:, :, None, :s1]
        qi = jnp.arange(s0, s1)[:, None]
        ki = jnp.arange(s1)[None, :]
        mask = (ki <= qi) & valid[None, :s1]
        logits = jnp.where(mask, logits, MASK_VALUE)
        p = jax.nn.softmax(logits, axis=-1).astype(v.dtype)
        outs.append(jnp.einsum('bhqk,bkhd->bqhd', p, v[:, :s1]))
    o = jnp.concatenate(outs, axis=1)[:, n_pad:].reshape(bsz, L, D_MODEL)
    return (o * jax.nn.silu(z)) @ w_out


def _fwd_setup_inputs(seed: int = 0) -> dict:
    key = jax.random.key(seed)
    ks = jax.random.split(key, 16)
    f32 = jnp.float32
    nrm = lambda k, shape, s: jax.random.normal(k, shape, f32) * s
    gain = lambda k, shape: 1.0 + 0.02 * jax.random.normal(k, shape, f32)
    sd = D_MODEL ** -0.5
    return {
        "x": nrm(ks[0], (BATCH, SEQ, D_MODEL), 1.0),
        "meta": nrm(ks[1], (N_META, D_MODEL), 1.0),
        "gamma_lb": nrm(ks[2], (N_A + 1, HG_FDIM), 0.1),
        "a_norm": gain(ks[3], (N_A, D_MODEL)),
        "a_w_in": nrm(ks[4], (N_A, D_MODEL, 2 * HG_FDIM + 2 * D_MODEL), sd),
        "a_out_norm": gain(ks[5], (N_A, D_MODEL)),
        "a_w_out": nrm(ks[6], (N_A, D_MODEL, D_MODEL), sd),
        "kv_norm": gain(ks[7], (D_MODEL,)),
        "kv_w": nrm(ks[8], (D_MODEL, 2 * D_MODEL + FOX_HEADS), sd),
        "fox_b_f": 3.0 + 0.1 * jax.random.normal(ks[9], (FOX_HEADS,), f32),
        "fox_k_norm": gain(ks[10], (FOX_HEADS, FOX_HDIM)),
        "b_norm": gain(ks[11], (N_B, D_MODEL)),
        "b_w_in": nrm(ks[12], (N_B, D_MODEL, 2 * D_MODEL), sd),
        "b_q_norm": gain(ks[13], (N_B, FOX_HEADS, FOX_HDIM)),
        "b_w_out": nrm(ks[14], (N_B, D_MODEL, D_MODEL), sd),
    }


def _fwd_reference(x, meta, gamma_lb, a_norm, a_w_in, a_out_norm, a_w_out, kv_norm, kv_w,
              fox_b_f, fox_k_norm, b_norm, b_w_in, b_q_norm, b_w_out):
    bsz = x.shape[0]
    h = jnp.concatenate(
        [jnp.broadcast_to(meta[None].astype(x.dtype), (bsz, N_META, D_MODEL)), x], axis=1)
    lbs = jnp.cumsum(jax.nn.softmax(gamma_lb.astype(jnp.float32), axis=0), axis=0)
    shared = None
    for layer in range(DEPTH):
        if layer < N_A:
            h = h + hgrn2_mix(h, a_norm[layer], a_w_in[layer], a_out_norm[layer],
                              a_w_out[layer], lbs[layer])
        else:
            if layer == N_A:
                shared = fox_shared_kv(h, kv_norm, kv_w, fox_b_f, fox_k_norm)
            j = layer - N_A
            k_s, v_s, F_s, valid_s = shared
            h = h + fox_mix(h, b_norm[j], b_w_in[j], b_q_norm[j], b_w_out[j],
                            k_s, v_s, F_s, valid_s)
    return h[:, N_META:]


import jax as _jax
import jax.numpy as _jnp

TWIN_FORMAT = 'train_step'
FWD_PARAMS = ['x', 'meta', 'gamma_lb', 'a_norm', 'a_w_in', 'a_out_norm', 'a_w_out', 'kv_norm', 'kv_w', 'fox_b_f', 'fox_k_norm', 'b_norm', 'b_w_in', 'b_q_norm', 'b_w_out']
TWIN_WEIGHTS = ['meta', 'gamma_lb', 'a_norm', 'a_w_in', 'a_out_norm', 'a_w_out', 'kv_norm', 'kv_w', 'fox_b_f', 'fox_k_norm', 'b_norm', 'b_w_in', 'b_q_norm', 'b_w_out']
TWIN_DIFF_INPUT = 'x'
TWIN_INPUTS = ['x', 'meta', 'gamma_lb', 'a_norm', 'a_w_in', 'a_out_norm', 'a_w_out', 'kv_norm', 'kv_w', 'fox_b_f', 'fox_k_norm', 'b_norm', 'b_w_in', 'b_q_norm', 'b_w_out', 'loss_target', 'm_meta', 'm_gamma_lb', 'm_a_norm', 'm_a_w_in', 'm_a_out_norm', 'm_a_w_out', 'm_kv_norm', 'm_kv_w', 'm_fox_b_f', 'm_fox_k_norm', 'm_b_norm', 'm_b_w_in', 'm_b_q_norm', 'm_b_w_out', 'v_meta', 'v_gamma_lb', 'v_a_norm', 'v_a_w_in', 'v_a_out_norm', 'v_a_w_out', 'v_kv_norm', 'v_kv_w', 'v_fox_b_f', 'v_fox_k_norm', 'v_b_norm', 'v_b_w_in', 'v_b_q_norm', 'v_b_w_out']
TWIN_OUTPUTS = ['loss', 'grad_x', 'grad_meta', 'grad_gamma_lb', 'grad_a_norm', 'grad_a_w_in', 'grad_a_out_norm', 'grad_a_w_out', 'grad_kv_norm', 'grad_kv_w', 'grad_fox_b_f', 'grad_fox_k_norm', 'grad_b_norm', 'grad_b_w_in', 'grad_b_q_norm', 'grad_b_w_out', 'delta_meta', 'delta_gamma_lb', 'delta_a_norm', 'delta_a_w_in', 'delta_a_out_norm', 'delta_a_w_out', 'delta_kv_norm', 'delta_kv_w', 'delta_fox_b_f', 'delta_fox_k_norm', 'delta_b_norm', 'delta_b_w_in', 'delta_b_q_norm', 'delta_b_w_out', 'new_m_meta', 'new_m_gamma_lb', 'new_m_a_norm', 'new_m_a_w_in', 'new_m_a_out_norm', 'new_m_a_w_out', 'new_m_kv_norm', 'new_m_kv_w', 'new_m_fox_b_f', 'new_m_fox_k_norm', 'new_m_b_norm', 'new_m_b_w_in', 'new_m_b_q_norm', 'new_m_b_w_out', 'new_v_meta', 'new_v_gamma_lb', 'new_v_a_norm', 'new_v_a_w_in', 'new_v_a_out_norm', 'new_v_a_w_out', 'new_v_kv_norm', 'new_v_kv_w', 'new_v_fox_b_f', 'new_v_fox_k_norm', 'new_v_b_norm', 'new_v_b_w_in', 'new_v_b_q_norm', 'new_v_b_w_out']
TWIN_LEAF_KINDS = {'loss': 'loss', 'grad_x': 'grad_x', 'grad_meta': 'grad_w', 'grad_gamma_lb': 'grad_w', 'grad_a_norm': 'grad_w', 'grad_a_w_in': 'grad_w', 'grad_a_out_norm': 'grad_w', 'grad_a_w_out': 'grad_w', 'grad_kv_norm': 'grad_w', 'grad_kv_w': 'grad_w', 'grad_fox_b_f': 'grad_w', 'grad_fox_k_norm': 'grad_w', 'grad_b_norm': 'grad_w', 'grad_b_w_in': 'grad_w', 'grad_b_q_norm': 'grad_w', 'grad_b_w_out': 'grad_w', 'delta_meta': 'delta_w', 'delta_gamma_lb': 'delta_w', 'delta_a_norm': 'delta_w', 'delta_a_w_in': 'delta_w', 'delta_a_out_norm': 'delta_w', 'delta_a_w_out': 'delta_w', 'delta_kv_norm': 'delta_w', 'delta_kv_w': 'delta_w', 'delta_fox_b_f': 'delta_w', 'delta_fox_k_norm': 'delta_w', 'delta_b_norm': 'delta_w', 'delta_b_w_in': 'delta_w', 'delta_b_q_norm': 'delta_w', 'delta_b_w_out': 'delta_w', 'new_m_meta': 'new_m', 'new_m_gamma_lb': 'new_m', 'new_m_a_norm': 'new_m', 'new_m_a_w_in': 'new_m', 'new_m_a_out_norm': 'new_m', 'new_m_a_w_out': 'new_m', 'new_m_kv_norm': 'new_m', 'new_m_kv_w': 'new_m', 'new_m_fox_b_f': 'new_m', 'new_m_fox_k_norm': 'new_m', 'new_m_b_norm': 'new_m', 'new_m_b_w_in': 'new_m', 'new_m_b_q_norm': 'new_m', 'new_m_b_w_out': 'new_m', 'new_v_meta': 'new_v', 'new_v_gamma_lb': 'new_v', 'new_v_a_norm': 'new_v', 'new_v_a_w_in': 'new_v', 'new_v_a_out_norm': 'new_v', 'new_v_a_w_out': 'new_v', 'new_v_kv_norm': 'new_v', 'new_v_kv_w': 'new_v', 'new_v_fox_b_f': 'new_v', 'new_v_fox_k_norm': 'new_v', 'new_v_b_norm': 'new_v', 'new_v_b_w_in': 'new_v', 'new_v_b_q_norm': 'new_v', 'new_v_b_w_out': 'new_v'}


def _forward(args):
    return _fwd_reference(*[args[k] for k in FWD_PARAMS])


def _output_shape():
    def fwd():
        inp = _fwd_setup_inputs(0)
        return _fwd_reference(*[inp[k] for k in FWD_PARAMS])
    out = _jax.eval_shape(fwd)
    return out.shape, out.dtype

N_MICROBATCH = 1
ADAM_LR = 0.001
ADAM_B1 = 0.9
ADAM_B2 = 0.999
ADAM_EPS = 1e-08
ADAM_WD = 0.01
ADAM_STEP = 10
PER_EXAMPLE_BATCH_AXIS = {'x': 0, 'loss_target': 0}
SHARED_INPUTS = []
_WEIGHT_DTYPES = {'meta': _jnp.float32, 'gamma_lb': _jnp.float32, 'a_norm': _jnp.float32, 'a_w_in': _jnp.float32, 'a_out_norm': _jnp.float32, 'a_w_out': _jnp.float32, 'kv_norm': _jnp.float32, 'kv_w': _jnp.float32, 'fox_b_f': _jnp.float32, 'fox_k_norm': _jnp.float32, 'b_norm': _jnp.float32, 'b_w_in': _jnp.float32, 'b_q_norm': _jnp.float32, 'b_w_out': _jnp.float32}
MOMENT_SCALE = {'meta': 3.718313e-03, 'gamma_lb': 1.222312e-02, 'a_norm': 6.687945e+00, 'a_w_in': 1.023051e-01, 'a_out_norm': 5.692635e+00, 'a_w_out': 1.310074e-01, 'kv_norm': 6.296161e-01, 'kv_w': 3.571712e-02, 'fox_b_f': 2.250496e+01, 'fox_k_norm': 2.485235e-01, 'b_norm': 5.445968e-01, 'b_w_in': 2.574843e-02, 'b_q_norm': 2.484956e-01, 'b_w_out': 2.504018e-02}


def _to_microbatches(a, axis):
    t = _jnp.moveaxis(a, axis, 0)
    t = t.reshape((N_MICROBATCH, t.shape[0] // N_MICROBATCH) + t.shape[1:])
    return _jnp.moveaxis(t, 1, axis + 1)


def setup_inputs(seed: int = 0) -> dict:
    inp = _fwd_setup_inputs(seed)
    key = _jax.random.fold_in(_jax.random.key(seed), 7919)
    shape, _ = _output_shape()
    out = dict(inp)
    out["loss_target"] = _jax.random.normal(_jax.random.fold_in(key, 0), shape, _jnp.float32)
    for i, name in enumerate(TWIN_WEIGHTS):
        w = inp[name].astype(_jnp.float32)
        if MOMENT_SCALE is None:
            s = _jnp.sqrt(_jnp.mean(_jnp.square(w)) + 1e-30)
        else:
            s = MOMENT_SCALE[name]
        km, kv = _jax.random.split(_jax.random.fold_in(key, i + 1))
        out[name] = w
        out["m_" + name] = s * _jax.random.normal(km, w.shape, _jnp.float32)
        out["v_" + name] = (s * s) * _jax.random.uniform(kv, w.shape, _jnp.float32, 0.5, 1.5)
    if N_MICROBATCH > 1:
        for name, axis in PER_EXAMPLE_BATCH_AXIS.items():
            out[name] = _to_microbatches(out[name], axis)
    return {'x': out['x'], 'meta': out['meta'], 'gamma_lb': out['gamma_lb'], 'a_norm': out['a_norm'], 'a_w_in': out['a_w_in'], 'a_out_norm': out['a_out_norm'], 'a_w_out': out['a_w_out'], 'kv_norm': out['kv_norm'], 'kv_w': out['kv_w'], 'fox_b_f': out['fox_b_f'], 'fox_k_norm': out['fox_k_norm'], 'b_norm': out['b_norm'], 'b_w_in': out['b_w_in'], 'b_q_norm': out['b_q_norm'], 'b_w_out': out['b_w_out'], 'loss_target': out['loss_target'], 'm_meta': out['m_meta'], 'm_gamma_lb': out['m_gamma_lb'], 'm_a_norm': out['m_a_norm'], 'm_a_w_in': out['m_a_w_in'], 'm_a_out_norm': out['m_a_out_norm'], 'm_a_w_out': out['m_a_w_out'], 'm_kv_norm': out['m_kv_norm'], 'm_kv_w': out['m_kv_w'], 'm_fox_b_f': out['m_fox_b_f'], 'm_fox_k_norm': out['m_fox_k_norm'], 'm_b_norm': out['m_b_norm'], 'm_b_w_in': out['m_b_w_in'], 'm_b_q_norm': out['m_b_q_norm'], 'm_b_w_out': out['m_b_w_out'], 'v_meta': out['v_meta'], 'v_gamma_lb': out['v_gamma_lb'], 'v_a_norm': out['v_a_norm'], 'v_a_w_in': out['v_a_w_in'], 'v_a_out_norm': out['v_a_out_norm'], 'v_a_w_out': out['v_a_w_out'], 'v_kv_norm': out['v_kv_norm'], 'v_kv_w': out['v_kv_w'], 'v_fox_b_f': out['v_fox_b_f'], 'v_fox_k_norm': out['v_fox_k_norm'], 'v_b_norm': out['v_b_norm'], 'v_b_w_in': out['v_b_w_in'], 'v_b_q_norm': out['v_b_q_norm'], 'v_b_w_out': out['v_b_w_out']}


def _loss(weights, diff, rest, loss_target):
    with _jax.named_scope("forward"):
        args = {**rest, TWIN_DIFF_INPUT: diff, **{k: w.astype(_WEIGHT_DTYPES[k]) for k, w in weights.items()}}
        y = _forward(args)
    with _jax.named_scope("loss_head"):
        err = _jnp.square(y.astype(_jnp.float32) - loss_target)
        return 0.5 * _jnp.sum(_jnp.mean(err, axis=-1)) if err.ndim else 0.5 * err


def _adamw(w, g, m, v):
    m = ADAM_B1 * m + (1.0 - ADAM_B1) * g
    v = ADAM_B2 * v + (1.0 - ADAM_B2) * _jnp.square(g)
    m_hat = m / (1.0 - ADAM_B1 ** ADAM_STEP)
    v_hat = v / (1.0 - ADAM_B2 ** ADAM_STEP)
    delta = -ADAM_LR * (m_hat / (_jnp.sqrt(v_hat) + ADAM_EPS) + ADAM_WD * w)
    return delta, m, v


def reference(x, meta, gamma_lb, a_norm, a_w_in, a_out_norm, a_w_out, kv_norm, kv_w, fox_b_f, fox_k_norm, b_norm, b_w_in, b_q_norm, b_w_out, loss_target, m_meta, m_gamma_lb, m_a_norm, m_a_w_in, m_a_out_norm, m_a_w_out, m_kv_norm, m_kv_w, m_fox_b_f, m_fox_k_norm, m_b_norm, m_b_w_in, m_b_q_norm, m_b_w_out, v_meta, v_gamma_lb, v_a_norm, v_a_w_in, v_a_out_norm, v_a_w_out, v_kv_norm, v_kv_w, v_fox_b_f, v_fox_k_norm, v_b_norm, v_b_w_in, v_b_q_norm, v_b_w_out):
    given = dict(x=x, meta=meta, gamma_lb=gamma_lb, a_norm=a_norm, a_w_in=a_w_in, a_out_norm=a_out_norm, a_w_out=a_w_out, kv_norm=kv_norm, kv_w=kv_w, fox_b_f=fox_b_f, fox_k_norm=fox_k_norm, b_norm=b_norm, b_w_in=b_w_in, b_q_norm=b_q_norm, b_w_out=b_w_out, loss_target=loss_target, m_meta=m_meta, m_gamma_lb=m_gamma_lb, m_a_norm=m_a_norm, m_a_w_in=m_a_w_in, m_a_out_norm=m_a_out_norm, m_a_w_out=m_a_w_out, m_kv_norm=m_kv_norm, m_kv_w=m_kv_w, m_fox_b_f=m_fox_b_f, m_fox_k_norm=m_fox_k_norm, m_b_norm=m_b_norm, m_b_w_in=m_b_w_in, m_b_q_norm=m_b_q_norm, m_b_w_out=m_b_w_out, v_meta=v_meta, v_gamma_lb=v_gamma_lb, v_a_norm=v_a_norm, v_a_w_in=v_a_w_in, v_a_out_norm=v_a_out_norm, v_a_w_out=v_a_w_out, v_kv_norm=v_kv_norm, v_kv_w=v_kv_w, v_fox_b_f=v_fox_b_f, v_fox_k_norm=v_fox_k_norm, v_b_norm=v_b_norm, v_b_w_in=v_b_w_in, v_b_q_norm=v_b_q_norm, v_b_w_out=v_b_w_out)
    weights = {n: given[n] for n in TWIN_WEIGHTS}
    shared = {n: given[n] for n in SHARED_INPUTS}
    per_example = {n: given[n] for n in ['x']}
    grad_fn = _jax.value_and_grad(_loss, argnums=(0, 1))

    def one_microbatch(ex, loss_target):
        ex = dict(ex)
        diff = ex.pop(TWIN_DIFF_INPUT)
        return grad_fn(weights, diff, {**shared, **ex}, loss_target)

    if N_MICROBATCH == 1:
        loss, (grad_w, grad_x) = one_microbatch(per_example, given["loss_target"])
    else:
        def body(carry, xs):
            loss_sum, grad_sum = carry
            l_k, (gw_k, gx_k) = one_microbatch(xs[0], xs[1])
            with _jax.named_scope("update"):
                return (loss_sum + l_k, _jax.tree.map(_jnp.add, grad_sum, gw_k)), gx_k

        init = (_jnp.zeros((), _jnp.float32), _jax.tree.map(_jnp.zeros_like, weights))
        (loss, grad_w), grad_x = _jax.lax.scan(body, init, (per_example, given["loss_target"]))
    with _jax.named_scope("update"):
        delta_w, new_m, new_v = {}, {}, {}
        for n in TWIN_WEIGHTS:
            delta_w[n], new_m[n], new_v[n] = _adamw(weights[n], grad_w[n], given["m_" + n], given["v_" + n])
    return (loss, grad_x, *[grad_w[n] for n in TWIN_WEIGHTS], *[delta_w[n] for n in TWIN_WEIGHTS],
            *[new_m[n] for n in TWIN_WEIGHTS], *[new_v[n] for n in TWIN_WEIGHTS])
```

```python
import math

import jax
import jax.numpy as jnp
from jax import lax
from jax.experimental import pallas as pl
from jax.experimental.pallas import tpu as pltpu

HEAD = 128
CHUNK = 64
N_META = 16
ROW_PAD = 128 - N_META
EPS = 1e-6
MASK_VALUE = -1e30
ADAM_LR = 0.001
ADAM_B1 = 0.9
ADAM_B2 = 0.999
ADAM_EPS = 1e-08
ADAM_WD = 0.01
ADAM_STEP = 10
N_DEV = 8
N_CHIP = 4
MIB = 1024 * 1024
AXES = ("x", "y", "c")
MESH = pl.DeviceIdType.MESH

F32 = jnp.float32
MXU = jnp.bfloat16
HI = lax.Precision.HIGHEST

NN = (((1,), (0,)), ((), ()))
NT = (((1,), (1,)), ((), ()))
TN = (((0,), (0,)), ((), ()))


def _dot(a, b, dims=NN):
    return lax.dot_general(a.astype(MXU), b.astype(MXU), dims, preferred_element_type=F32)


def _dot_exact(a, b):
    return lax.dot_general(a, b, NN, precision=HI, preferred_element_type=F32)


def _sigmoid(x):
    return 1.0 / (1.0 + jnp.exp(-x))


def _tile(dim, target):
    best = None
    t = 128
    while t <= min(dim, target):
        if dim % t == 0:
            best = t
        t += 128
    return best if best is not None else dim


def _cparams(semantics, vmem_mib):
    return pltpu.CompilerParams(dimension_semantics=semantics, vmem_limit_bytes=vmem_mib * MIB)


def _mat_spec(arr, br, bc, rc_of_grid):
    if arr.ndim == 2:
        return pl.BlockSpec((br, bc), rc_of_grid)
    assert arr.shape[2] % bc == 0, (arr.shape, bc)
    per = arr.shape[2] // bc

    def idx(*g):
        r, c = rc_of_grid(*g)
        return (c // per, r, c % per)

    return pl.BlockSpec((None, br, bc), idx)


def _mat_shape(arr):
    return (arr.shape[0], arr.shape[1]) if arr.ndim == 2 else (arr.shape[1], arr.shape[0] * arr.shape[2])


def _matmul(a, b, dims, out_dtype, name, *, add=None, out_parts=1, tm=512, tn=512, tk=512, vmem_mib=48):
    ar, ac = _mat_shape(a)
    br_, bc_ = _mat_shape(b)
    if dims == "nn":
        m, k, n = ar, ac, bc_
        assert br_ == k
    elif dims == "nt":
        m, k, n = ar, ac, br_
        assert bc_ == k
    else:
        m, k, n = ac, ar, bc_
        assert br_ == k
    m_unit, n_unit, k_unit = m, n, k
    if a.ndim == 3:
        if dims == "tn":
            m_unit = math.gcd(m_unit, a.shape[2])
        else:
            k_unit = math.gcd(k_unit, a.shape[2])
    if b.ndim == 3:
        if dims == "nt":
            k_unit = math.gcd(k_unit, b.shape[2])
        else:
            n_unit = math.gcd(n_unit, b.shape[2])
    if out_parts > 1:
        n_unit = math.gcd(n_unit, n // out_parts)
    tm, tn, tk = _tile(m_unit, tm), _tile(n_unit, tn), _tile(k_unit, tk)
    gm, gn, gk = m // tm, n // tn, k // tk
    assert gm * tm == m and gn * tn == n and gk * tk == k, (name, m, n, k, tm, tn, tk)

    if dims == "nn":
        a_spec = _mat_spec(a, tm, tk, lambda i, j, kk: (i, kk))
        b_spec = _mat_spec(b, tk, tn, lambda i, j, kk: (kk, j))
        dn = NN
    elif dims == "nt":
        a_spec = _mat_spec(a, tm, tk, lambda i, j, kk: (i, kk))
        b_spec = _mat_spec(b, tn, tk, lambda i, j, kk: (j, kk))
        dn = NT
    else:
        a_spec = _mat_spec(a, tk, tm, lambda i, j, kk: (kk, i))
        b_spec = _mat_spec(b, tk, tn, lambda i, j, kk: (kk, j))
        dn = TN

    if out_parts > 1:
        per = (n // out_parts) // tn
        out_shape = jax.ShapeDtypeStruct((out_parts, m, n // out_parts), out_dtype)
        o_spec = pl.BlockSpec((None, tm, tn), lambda i, j, kk: (j // per, i, j % per))
    else:
        out_shape = jax.ShapeDtypeStruct((m, n), out_dtype)
        o_spec = pl.BlockSpec((tm, tn), lambda i, j, kk: (i, j))

    in_specs = [a_spec, b_spec]
    args = [a, b]
    if add is not None:
        in_specs.append(pl.BlockSpec((tm, tn), lambda i, j, kk: (i, j)))
        args.append(add)

    def body(*refs):
        if add is not None:
            a_ref, b_ref, add_ref, o_ref, acc_ref = refs
        else:
            a_ref, b_ref, o_ref, acc_ref = refs
            add_ref = None
        kk = pl.program_id(2)
        part = lax.dot_general(a_ref[...].astype(MXU), b_ref[...].astype(MXU), dn, preferred_element_type=F32)

        def finish(total):
            if add_ref is not None:
                total = total + add_ref[...]
            o_ref[...] = total.astype(o_ref.dtype)

        if gk == 1:
            finish(part)
        else:
            @pl.when(kk == 0)
            def _():
                acc_ref[...] = part

            @pl.when(jnp.logical_and(kk > 0, kk < gk - 1))
            def _():
                acc_ref[...] += part

            @pl.when(kk == gk - 1)
            def _():
                finish(acc_ref[...] + part)

    return pl.pallas_call(
        body,
        name=name,
        grid=(gm, gn, gk),
        in_specs=in_specs,
        out_specs=o_spec,
        out_shape=out_shape,
        scratch_shapes=[pltpu.VMEM((tm, tn) if gk > 1 else (8, 128), F32)],
        compiler_params=_cparams(("parallel", "parallel", "arbitrary"), vmem_mib),
    )(*args)


def _rms_fwd(h, gains, name, tm=384):
    lp, d = h.shape
    tm = _tile(lp, tm)
    n = len(gains)

    def body(*refs):
        h_ref = refs[0]
        g_refs = refs[1:1 + n]
        o_refs = refs[1 + n:]
        x = h_ref[...]
        y = x * lax.rsqrt(jnp.mean(x * x, axis=-1, keepdims=True) + EPS)
        for g_ref, o_ref in zip(g_refs, o_refs):
            o_ref[...] = (y * g_ref[...]).astype(o_ref.dtype)

    row = pl.BlockSpec((tm, d), lambda i: (i, 0))
    vec = pl.BlockSpec((1, d), lambda i: (0, 0))
    return pl.pallas_call(
        body,
        name=name,
        grid=(lp // tm,),
        in_specs=[row] + [vec] * n,
        out_specs=[row] * n,
        out_shape=[jax.ShapeDtypeStruct((lp, d), MXU)] * n,
        compiler_params=_cparams(("parallel",), 40),
    )(h, *gains)


def _rms_bwd(h, gains, dys, res, name, tm=384):
    lp, d = h.shape
    tm = _tile(lp, tm)
    n = len(gains)

    def body(*refs):
        h_ref, res_ref = refs[0], refs[1]
        g_refs = refs[2:2 + n]
        dy_refs = refs[2 + n:2 + 2 * n]
        dh_ref = refs[2 + 2 * n]
        dg_refs = refs[3 + 2 * n:]
        i = pl.program_id(0)
        x = h_ref[...]
        rstd = lax.rsqrt(jnp.mean(x * x, axis=-1, keepdims=True) + EPS)
        xhat = x * rstd
        dh = res_ref[...]
        for g_ref, dy_ref, dg_ref in zip(g_refs, dy_refs, dg_refs):
            dy = dy_ref[...]
            gdy = dy * g_ref[...]
            dh = dh + rstd * (gdy - xhat * jnp.mean(gdy * xhat, axis=-1, keepdims=True))
            part = jnp.sum(dy * xhat, axis=0, keepdims=True)

            @pl.when(i == 0)
            def _():
                dg_ref[...] = part

            @pl.when(i > 0)
            def _():
                dg_ref[...] += part

        dh_ref[...] = dh

    row = pl.BlockSpec((tm, d), lambda i: (i, 0))
    vec = pl.BlockSpec((1, d), lambda i: (0, 0))
    outs = pl.pallas_call(
        body,
        name=name,
        grid=(lp // tm,),
        in_specs=[row, row] + [vec] * n + [row] * n,
        out_specs=[row] + [vec] * n,
        out_shape=[jax.ShapeDtypeStruct((lp, d), F32)] + [jax.ShapeDtypeStruct((1, d), F32)] * n,
        compiler_params=_cparams(("arbitrary",), 56),
    )(h, res, *gains, *dys)
    return outs[0], list(outs[1:])


def _lb_fwd(gamma):
    def body(g_ref, lb_ref):
        g = g_ref[...]
        e = jnp.exp(g - jnp.max(g, axis=0, keepdims=True))
        lb_ref[...] = (e / jnp.sum(e, axis=0, keepdims=True))[0:1, :]

    return pl.pallas_call(body, name="lb_fwd", out_shape=jax.ShapeDtypeStruct((1, gamma.shape[1]), F32))(gamma)


def _lb_bwd(gamma, dlb):
    def body(g_ref, dlb_ref, dg_ref):
        g = g_ref[...]
        e = jnp.exp(g - jnp.max(g, axis=0, keepdims=True))
        s = e / jnp.sum(e, axis=0, keepdims=True)
        rows = lax.broadcasted_iota(jnp.int32, g.shape, 0)
        ds = jnp.where(rows == 0, dlb_ref[...], 0.0)
        dg_ref[...] = s * (ds - jnp.sum(s * ds, axis=0, keepdims=True))

    return pl.pallas_call(body, name="lb_bwd", out_shape=jax.ShapeDtypeStruct(gamma.shape, F32))(gamma, dlb)


def _tri(n, lower):
    r = lax.broadcasted_iota(jnp.int32, (n, n), 0)
    c = lax.broadcasted_iota(jnp.int32, (n, n), 1)
    return jnp.where((r >= c) if lower else (r <= c), 1.0, 0.0).astype(F32)


def _causal(n):
    r = lax.broadcasted_iota(jnp.int32, (n, n), 0)
    c = lax.broadcasted_iota(jnp.int32, (n, n), 1)
    return r >= c


def _hgrn2_chunk(u_ref, lb, c):
    sl = pl.ds(pl.multiple_of(c * CHUNK, CHUNK), CHUNK)
    valid = (c * CHUNK + lax.broadcasted_iota(jnp.int32, (CHUNK, HEAD), 0)) >= ROW_PAD
    uq = u_ref[0, sl, :]
    uf = u_ref[1, sl, :]
    sq = _sigmoid(uq)
    sf = _sigmoid(uf)
    q = jnp.where(valid, uq * sq, 0.0)
    fg = lb + (1.0 - lb) * sf
    logf = jnp.where(valid, jnp.log(fg), 0.0)
    k = jnp.where(valid, 1.0 - fg, 0.0)
    v = jnp.where(valid, u_ref[2, sl, :], 0.0)
    b = _dot_exact(_tri(CHUNK, True), logf)
    b_last = b[CHUNK - 1:CHUNK, :]
    b_mid = b[CHUNK // 2 - 1:CHUNK // 2, :]
    e_qi = jnp.exp(b - b_mid)
    e_ki = jnp.exp(b_mid - b)
    e_kd = jnp.exp(b_last - b)
    e_qe = jnp.exp(b)
    return dict(sl=sl, valid=valid, uq=uq, sq=sq, sf=sf, fg=fg, v=v, e_qi=e_qi, e_ki=e_ki, e_kd=e_kd, e_qe=e_qe,
                qi=q * e_qi, ki=k * e_ki, kd=k * e_kd, qe=q * e_qe, decay=jnp.exp(b_last))


def _hgrn2_fwd(u4, lb, g_out, name="hgrn2_fwd"):
    _, lp, d = u4.shape
    nh, nc = d // HEAD, lp // CHUNK

    def body(u_ref, lb_ref, g_ref, o_ref, og_ref):
        lb_v = lb_ref[...]
        g = g_ref[...]

        def step(c, st):
            x = _hgrn2_chunk(u_ref, lb_v, c)
            a = jnp.where(_causal(CHUNK), _dot(x["qi"], x["ki"], NT), 0.0)
            o = _dot(a, x["v"]) + _dot(x["qe"], st, NT)
            o_ref[x["sl"], :] = o
            on = o * lax.rsqrt(jnp.mean(o * o, axis=-1, keepdims=True) + EPS) * g
            z = u_ref[3, x["sl"], :]
            og_ref[x["sl"], :] = (on * (z * _sigmoid(z))).astype(og_ref.dtype)
            return x["decay"] * st + _dot(x["v"], x["kd"], TN)

        lax.fori_loop(0, nc, step, jnp.zeros((HEAD, HEAD), F32))

    slab = pl.BlockSpec((lp, HEAD), lambda h: (0, h))
    vec = pl.BlockSpec((1, HEAD), lambda h: (0, h))
    return pl.pallas_call(
        body,
        name=name,
        grid=(nh,),
        in_specs=[pl.BlockSpec((4, lp, HEAD), lambda h: (0, 0, h)), vec, vec],
        out_specs=[slab, slab],
        out_shape=[jax.ShapeDtypeStruct((lp, d), F32), jax.ShapeDtypeStruct((lp, d), MXU)],
        compiler_params=_cparams(("parallel",), 48),
    )(u4, lb, g_out)


def _hgrn2_bwd(u4, o, d_og, lb, g_out, name="hgrn2_bwd"):
    _, lp, d = u4.shape
    nh, nc = d // HEAD, lp // CHUNK

    def body(u_ref, o_ref, dog_ref, lb_ref, g_ref, du_ref, dlb_ref, dg_ref, st_ref, do_ref):
        lb_v = lb_ref[...]
        g = g_ref[...]

        def fwd_step(c, carry):
            st, dg_acc = carry
            x = _hgrn2_chunk(u_ref, lb_v, c)
            sl = x["sl"]
            st_ref[c] = st
            ov = o_ref[sl, :]
            rstd = lax.rsqrt(jnp.mean(ov * ov, axis=-1, keepdims=True) + EPS)
            on = ov * rstd
            z = u_ref[3, sl, :]
            sz = _sigmoid(z)
            dog = dog_ref[sl, :]
            dy = dog * (z * sz)
            dz = dog * (on * g) * (sz * (1.0 + z * (1.0 - sz)))
            du_ref[3, sl, :] = dz.astype(du_ref.dtype)
            gdy = dy * g
            do = rstd * (gdy - on * jnp.mean(gdy * on, axis=-1, keepdims=True))
            do_ref[sl, :] = jnp.where(x["valid"], do, 0.0)
            dg_acc = dg_acc + jnp.sum(dy * on, axis=0, keepdims=True)
            return x["decay"] * st + _dot(x["v"], x["kd"], TN), dg_acc

        _, dg_tot = lax.fori_loop(0, nc, fwd_step, (jnp.zeros((HEAD, HEAD), F32), jnp.zeros((1, HEAD), F32)))
        dg_ref[...] = dg_tot

        def bwd_step(i, carry):
            gt, dlb_acc = carry
            c = nc - 1 - i
            x = _hgrn2_chunk(u_ref, lb_v, c)
            sl = x["sl"]
            do = do_ref[sl, :]
            st = st_ref[c]
            v, qi, ki, kd, qe = x["v"], x["qi"], x["ki"], x["kd"], x["qe"]
            a = jnp.where(_causal(CHUNK), _dot(qi, ki, NT), 0.0)
            d_qe = _dot(do, st)
            d_kd = _dot(v, gt)
            dv = _dot(kd, gt, NT) + _dot(a, do, TN)
            d_decay = jnp.sum(gt * st, axis=0, keepdims=True)
            d_a = jnp.where(_causal(CHUNK), _dot(do, v, NT), 0.0)
            d_qi = _dot(d_a, ki)
            d_ki = _dot(d_a, qi, TN)
            t_qi, t_ki, t_qe, t_kd = d_qi * qi, d_ki * ki, d_qe * qe, d_kd * kd
            rows = lax.broadcasted_iota(jnp.int32, (CHUNK, HEAD), 0)
            d_mid = jnp.sum(t_ki - t_qi, axis=0, keepdims=True)
            d_last = jnp.sum(t_kd, axis=0, keepdims=True) + d_decay * x["decay"]
            db = (t_qi - t_ki + t_qe - t_kd
                  + jnp.where(rows == CHUNK // 2 - 1, d_mid, 0.0) + jnp.where(rows == CHUNK - 1, d_last, 0.0))
            dlogf = _dot_exact(_tri(CHUNK, False), db)
            dq = d_qi * x["e_qi"] + d_qe * x["e_qe"]
            dk = d_ki * x["e_ki"] + d_kd * x["e_kd"]
            valid, sq, sf, uq = x["valid"], x["sq"], x["sf"], x["uq"]
            dfg = jnp.where(valid, dlogf / x["fg"] - dk, 0.0)
            du_ref[0, sl, :] = jnp.where(valid, dq * (sq * (1.0 + uq * (1.0 - sq))), 0.0).astype(du_ref.dtype)
            du_ref[1, sl, :] = (dfg * (1.0 - lb_v) * (sf * (1.0 - sf))).astype(du_ref.dtype)
            du_ref[2, sl, :] = jnp.where(valid, dv, 0.0).astype(du_ref.dtype)
            dlb_acc = dlb_acc + jnp.sum(dfg * (1.0 - sf), axis=0, keepdims=True)
            return x["decay"] * gt + _dot(do, qe, TN), dlb_acc

        _, dlb_tot = lax.fori_loop(0, nc, bwd_step, (jnp.zeros((HEAD, HEAD), F32), jnp.zeros((1, HEAD), F32)))
        dlb_ref[...] = dlb_tot

    slab = pl.BlockSpec((lp, HEAD), lambda h: (0, h))
    vec = pl.BlockSpec((1, HEAD), lambda h: (0, h))
    quad = pl.BlockSpec((4, lp, HEAD), lambda h: (0, 0, h))
    return pl.pallas_call(
        body,
        name=name,
        grid=(nh,),
        in_specs=[quad, slab, slab, vec, vec],
        out_specs=[quad, vec, vec],
        out_shape=[jax.ShapeDtypeStruct((4, lp, d), MXU), jax.ShapeDtypeStruct((1, d), F32),
                   jax.ShapeDtypeStruct((1, d), F32)],
        scratch_shapes=[pltpu.VMEM((nc, HEAD, HEAD), F32), pltpu.VMEM((lp, HEAD), F32)],
        compiler_params=_cparams(("parallel",), 58),
    )(u4, o, d_og, lb, g_out)


def _head_rms_fwd(src, part, g, name, tm=384):
    _, lp, d = src.shape
    tm = _tile(lp, tm)

    def body(x_ref, g_ref, o_ref):
        for h in range(d // HEAD):
            hs = slice(h * HEAD, (h + 1) * HEAD)
            x = x_ref[:, hs]
            y = x * lax.rsqrt(jnp.mean(x * x, axis=-1, keepdims=True) + EPS)
            o_ref[:, hs] = (y * g_ref[:, hs]).astype(o_ref.dtype)

    return pl.pallas_call(
        body,
        name=name,
        grid=(lp // tm,),
        in_specs=[pl.BlockSpec((None, tm, d), lambda i: (part, i, 0)), pl.BlockSpec((1, d), lambda i: (0, 0))],
        out_specs=pl.BlockSpec((tm, d), lambda i: (i, 0)),
        out_shape=jax.ShapeDtypeStruct((lp, d), MXU),
        compiler_params=_cparams(("parallel",), 32),
    )(src, g)


def _head_rms_bwd_tile(x, g, dy):
    rstd = lax.rsqrt(jnp.mean(x * x, axis=-1, keepdims=True) + EPS)
    xhat = x * rstd
    gdy = dy * g
    dx = rstd * (gdy - xhat * jnp.mean(gdy * xhat, axis=-1, keepdims=True))
    return dx, jnp.sum(dy * xhat, axis=0, keepdims=True)


def _fgate_fwd(ufl, b_f):
    lp = ufl.shape[0]
    nb = lp // 128

    def body(u_ref, b_ref, f_ref):
        def step(i, carry):
            sl = pl.ds(pl.multiple_of(i * 128, 128), 128)
            valid = (i * 128 + lax.broadcasted_iota(jnp.int32, (128, 128), 0)) >= ROW_PAD
            x = u_ref[sl, :] + b_ref[...]
            logf = jnp.where(valid, jnp.minimum(x, 0.0) - jnp.log(1.0 + jnp.exp(-jnp.abs(x))), 0.0)
            f = _dot_exact(_tri(128, True), logf) + carry
            f_ref[sl, :] = f
            return f[127:128, :]

        lax.fori_loop(0, nb, step, jnp.zeros((1, 128), F32))

    return pl.pallas_call(body, name="fgate_fwd", out_shape=jax.ShapeDtypeStruct((lp, 128), F32))(ufl, b_f)


def _fgate_bwd(ufl, b_f, d_f):
    lp = ufl.shape[0]
    nb = lp // 128

    def body(u_ref, b_ref, df_ref, du_ref, db_ref):
        def step(j, carry):
            later, db_acc = carry
            i = nb - 1 - j
            sl = pl.ds(pl.multiple_of(i * 128, 128), 128)
            valid = (i * 128 + lax.broadcasted_iota(jnp.int32, (128, 128), 0)) >= ROW_PAD
            x = u_ref[sl, :] + b_ref[...]
            df = df_ref[sl, :]
            dlogf = _dot_exact(_tri(128, False), df) + later
            dx = jnp.where(valid, dlogf * _sigmoid(-x), 0.0)
            du_ref[sl, :] = dx.astype(du_ref.dtype)
            return later + jnp.sum(df, axis=0, keepdims=True), db_acc + jnp.sum(dx, axis=0, keepdims=True)

        _, db_tot = lax.fori_loop(0, nb, step, (jnp.zeros((1, 128), F32), jnp.zeros((1, 128), F32)))
        db_ref[...] = db_tot

    return pl.pallas_call(
        body, name="fgate_bwd",
        out_shape=[jax.ShapeDtypeStruct((lp, 128), MXU), jax.ShapeDtypeStruct((1, 128), F32)],
    )(ufl, b_f, d_f)


def _attn_mask(qb, kb, t):
    row = qb * t + lax.broadcasted_iota(jnp.int32, (t, t), 0)
    col = kb * t + lax.broadcasted_iota(jnp.int32, (t, t), 1)
    return jnp.logical_and(col <= row, col >= ROW_PAD)


def _fox_fwd(qn, kn, ukv, ub, fq, fk, name="fox_fwd", t=384):
    lp, d = qn.shape
    t = _tile(lp, t)
    nh, nq = d // HEAD, lp // t
    scale = HEAD ** -0.5

    def body(q_ref, k_ref, v_ref, z_ref, fq_ref, fk_ref, o_ref, olo_ref, og_ref, lse_ref):
        qb = pl.program_id(1)
        q = q_ref[...]
        fqv = fq_ref[...]

        def step(kb, carry):
            m, l, acc, acc_lo = carry
            ks = pl.ds(pl.multiple_of(kb * t, t), t)
            s = _dot(q, k_ref[ks, :], NT) * scale + fqv - fk_ref[:, ks]
            s = jnp.where(_attn_mask(qb, kb, t), s, MASK_VALUE)
            m_new = jnp.maximum(m, jnp.max(s, axis=-1, keepdims=True))
            alpha = jnp.exp(m - m_new)
            p = jnp.exp(s - m_new)
            l = alpha * l + jnp.sum(p, axis=-1, keepdims=True)
            p_hi = p.astype(MXU)
            v = v_ref[ks, :]
            acc = alpha * acc + _dot(p_hi, v)
            acc_lo = alpha * acc_lo + _dot(p - p_hi.astype(F32), v)
            return m_new, l, acc, acc_lo

        init = (jnp.full((t, 1), MASK_VALUE, F32), jnp.zeros((t, 1), F32), jnp.zeros((t, HEAD), F32),
                jnp.zeros((t, HEAD), F32))
        m, l, acc, acc_lo = lax.fori_loop(0, qb + 1, step, init)
        o = acc / l
        o_ref[...] = o
        olo_ref[...] = acc_lo / l
        z = z_ref[...]
        og_ref[...] = (o * (z * _sigmoid(z))).astype(og_ref.dtype)
        lse_ref[...] = m + jnp.log(l)

    tile = pl.BlockSpec((t, HEAD), lambda h, i: (i, h))
    return pl.pallas_call(
        body,
        name=name,
        grid=(nh, nq),
        in_specs=[tile,
                  pl.BlockSpec((lp, HEAD), lambda h, i: (0, h)),
                  pl.BlockSpec((None, lp, HEAD), lambda h, i: (1, 0, h)),
                  pl.BlockSpec((None, t, HEAD), lambda h, i: (1, i, h)),
                  pl.BlockSpec((None, t, 1), lambda h, i: (h, i, 0)),
                  pl.BlockSpec((None, 1, lp), lambda h, i: (h, 0, 0))],
        out_specs=[tile, tile, tile, pl.BlockSpec((None, t, 1), lambda h, i: (h, i, 0))],
        out_shape=[jax.ShapeDtypeStruct((lp, d), F32), jax.ShapeDtypeStruct((lp, d), F32),
                   jax.ShapeDtypeStruct((lp, d), MXU), jax.ShapeDtypeStruct((nh, lp, 1), F32)],
        compiler_params=_cparams(("parallel", "arbitrary"), 40),
    )(qn, kn, ukv, ub, fq, fk)


def _fox_bwd(qn, kn, ukv, do, c1, fk, delta, name="fox_bwd", t=384):
    lp, d = qn.shape
    t = _tile(lp, t)
    nh, nk = d // HEAD, lp // t
    scale = HEAD ** -0.5

    def body(q_ref, do_ref, k_ref, v_ref, c1_ref, fk_ref, dl_ref, dq_ref, dk_ref, dv_ref, dfk_ref):
        kb = pl.program_id(1)

        @pl.when(kb == 0)
        def _():
            dq_ref[...] = jnp.zeros_like(dq_ref)

        k = k_ref[...]
        v = v_ref[...]
        fkv = fk_ref[...]
        dk_ref[...] = jnp.zeros_like(dk_ref)
        dv_ref[...] = jnp.zeros_like(dv_ref)

        def step(qb, dfk_acc):
            qs = pl.ds(pl.multiple_of(qb * t, t), t)
            q = q_ref[qs, :]
            dob = do_ref[qs, :]
            s = _dot(q, k, NT) * scale + c1_ref[qs, :] - fkv
            p = jnp.where(_attn_mask(qb, kb, t), jnp.exp(s), 0.0)
            dv_ref[...] += _dot(p, dob, TN)
            dp = _dot(dob, v, NT)
            ds = p * (dp - dl_ref[qs, :])
            dsb = (ds * scale).astype(MXU)
            dq_ref[qs, :] += _dot(dsb, k)
            dk_ref[...] += _dot(dsb, q, TN)
            return dfk_acc + jnp.sum(ds, axis=0, keepdims=True)

        dfk = lax.fori_loop(kb, nk, step, jnp.zeros((1, t), F32))
        dfk_ref[...] = -dfk

    whole = pl.BlockSpec((lp, HEAD), lambda h, j: (0, h))
    tile = pl.BlockSpec((t, HEAD), lambda h, j: (j, h))
    col = pl.BlockSpec((None, lp, 1), lambda h, j: (h, 0, 0))
    return pl.pallas_call(
        body,
        name=name,
        grid=(nh, nk),
        in_specs=[whole, whole, tile,
                  pl.BlockSpec((None, t, HEAD), lambda h, j: (1, j, h)),
                  col,
                  pl.BlockSpec((None, 1, t), lambda h, j: (h, 0, j)),
                  col],
        out_specs=[whole, tile, tile, pl.BlockSpec((None, 1, t), lambda h, j: (h, 0, j))],
        out_shape=[jax.ShapeDtypeStruct((lp, d), F32)] * 3 + [jax.ShapeDtypeStruct((nh, 1, lp), F32)],
        compiler_params=_cparams(("parallel", "arbitrary"), 48),
    )(qn, do, kn, ukv, c1, fk, delta)


def _fox_do(d_og, o, o_lo, ub, name="fox_do", tm=384):
    lp, d = o.shape
    tm = _tile(lp, tm)
    nh = d // HEAD

    def body(dog_ref, o_ref, olo_ref, z_ref, do_ref, dl_ref):
        lane = lax.broadcasted_iota(jnp.int32, (tm, 128), 1)
        delta = jnp.zeros((tm, 128), F32)
        for h in range(nh):
            hs = slice(h * HEAD, (h + 1) * HEAD)
            z = z_ref[:, hs]
            do = (dog_ref[:, hs] * (z * _sigmoid(z))).astype(do_ref.dtype)
            do_ref[:, hs] = do
            row_sum = jnp.sum(do.astype(F32) * (o_ref[:, hs] + olo_ref[:, hs]), axis=-1, keepdims=True)
            delta = jnp.where(lane == h, row_sum, delta)
        dl_ref[...] = delta

    row = pl.BlockSpec((tm, d), lambda i: (i, 0))
    return pl.pallas_call(
        body,
        name=name,
        grid=(lp // tm,),
        in_specs=[row, row, row, pl.BlockSpec((None, tm, d), lambda i: (1, i, 0))],
        out_specs=[row, pl.BlockSpec((tm, 128), lambda i: (i, 0))],
        out_shape=[jax.ShapeDtypeStruct((lp, d), MXU), jax.ShapeDtypeStruct((lp, 128), F32)],
        compiler_params=_cparams(("parallel",), 48),
    )(d_og, o, o_lo, ub)


def _fox_q_bwd(ub, d_og, o, dqn, g_q, name="fox_q_bwd", tm=384):
    _, lp, d = ub.shape
    tm = _tile(lp, tm)
    nh = d // HEAD

    def body(ub_ref, dog_ref, o_ref, dqn_ref, g_ref, dub_ref, dg_ref):
        i = pl.program_id(0)
        for h in range(nh):
            hs = slice(h * HEAD, (h + 1) * HEAD)
            z = ub_ref[1, :, hs]
            sz = _sigmoid(z)
            dub_ref[1, :, hs] = (dog_ref[:, hs] * o_ref[:, hs] * (sz * (1.0 + z * (1.0 - sz)))).astype(dub_ref.dtype)
            dx, dg = _head_rms_bwd_tile(ub_ref[0, :, hs], g_ref[:, hs], dqn_ref[:, hs])
            dub_ref[0, :, hs] = dx.astype(dub_ref.dtype)

            @pl.when(i == 0)
            def _():
                dg_ref[:, hs] = dg

            @pl.when(i > 0)
            def _():
                dg_ref[:, hs] += dg

    row = pl.BlockSpec((tm, d), lambda i: (i, 0))
    pair = pl.BlockSpec((2, tm, d), lambda i: (0, i, 0))
    vec = pl.BlockSpec((1, d), lambda i: (0, 0))
    return pl.pallas_call(
        body,
        name=name,
        grid=(lp // tm,),
        in_specs=[pair, row, row, row, vec],
        out_specs=[pair, vec],
        out_shape=[jax.ShapeDtypeStruct((2, lp, d), MXU), jax.ShapeDtypeStruct((1, d), F32)],
        compiler_params=_cparams(("arbitrary",), 56),
    )(ub, d_og, o, dqn, g_q)


def _fox_k_bwd(ukv, dkn, dv, g_k, name="fox_k_bwd", tm=384):
    _, lp, d = ukv.shape
    tm = _tile(lp, tm)
    nh = d // HEAD

    def body(ukv_ref, dkn_ref, dv_ref, g_ref, du_ref, dg_ref):
        i = pl.program_id(0)
        du_ref[1] = dv_ref[...].astype(du_ref.dtype)
        for h in range(nh):
            hs = slice(h * HEAD, (h + 1) * HEAD)
            dx, dg = _head_rms_bwd_tile(ukv_ref[:, hs], g_ref[:, hs], dkn_ref[:, hs])
            du_ref[0, :, hs] = dx.astype(du_ref.dtype)

            @pl.when(i == 0)
            def _():
                dg_ref[:, hs] = dg

            @pl.when(i > 0)
            def _():
                dg_ref[:, hs] += dg

    row = pl.BlockSpec((tm, d), lambda i: (i, 0))
    vec = pl.BlockSpec((1, d), lambda i: (0, 0))
    return pl.pallas_call(
        body,
        name=name,
        grid=(lp // tm,),
        in_specs=[pl.BlockSpec((None, tm, d), lambda i: (0, i, 0)), row, row, vec],
        out_specs=[pl.BlockSpec((2, tm, d), lambda i: (0, i, 0)), vec],
        out_shape=[jax.ShapeDtypeStruct((2, lp, d), MXU), jax.ShapeDtypeStruct((1, d), F32)],
        compiler_params=_cparams(("arbitrary",), 48),
    )(ukv, dkn, dv, g_k)


def _loss(h, target, name="loss_head"):
    lp, d = h.shape
    nb = lp // 128

    def body(h_ref, t_ref, loss_ref, dh_ref, acc_ref):
        i = pl.program_id(0)

        @pl.when(i == 0)
        def _():
            acc_ref[...] = jnp.zeros_like(acc_ref)
            dh_ref[...] = jnp.zeros_like(dh_ref)

        @pl.when(i > 0)
        def _():
            err = h_ref[...] - t_ref[...]
            dh_ref[...] = err * (1.0 / d)
            acc_ref[...] += jnp.sum(jnp.sum(err * err, axis=-1, keepdims=True) * (1.0 / d), axis=0, keepdims=True)

        @pl.when(i == nb - 1)
        def _():
            loss_ref[...] = 0.5 * acc_ref[...]

    return pl.pallas_call(
        body,
        name=name,
        grid=(nb,),
        in_specs=[pl.BlockSpec((128, d), lambda i: (i, 0)),
                  pl.BlockSpec((128, d), lambda i: (jnp.maximum(i - 1, 0), 0))],
        out_specs=[pl.BlockSpec((1, 1), lambda i: (0, 0)), pl.BlockSpec((128, d), lambda i: (i, 0))],
        out_shape=[jax.ShapeDtypeStruct((1, 1), F32), jax.ShapeDtypeStruct((lp, d), F32)],
        scratch_shapes=[pltpu.VMEM((1, 1), F32)],
        compiler_params=_cparams(("arbitrary",), 32),
    )(h, target)


def _adam_math(w, g, m, v):
    m = ADAM_B1 * m + (1.0 - ADAM_B1) * g
    v = ADAM_B2 * v + (1.0 - ADAM_B2) * (g * g)
    m_hat = m / (1.0 - ADAM_B1 ** ADAM_STEP)
    v_hat = v / (1.0 - ADAM_B2 ** ADAM_STEP)
    delta = -ADAM_LR * (m_hat / (jnp.sqrt(v_hat) + ADAM_EPS) + ADAM_WD * w)
    return delta, m, v


def _adamw(parts, w, m, v, name, tm=256):
    n, r, c = parts.shape
    tm = tm if r % tm == 0 else r

    def body(p_ref, w_ref, m_ref, v_ref, g_ref, d_ref, nm_ref, nv_ref):
        g = p_ref[0]
        for j in range(1, n):
            g = g + p_ref[j]
        g_ref[...] = g
        d_ref[...], nm_ref[...], nv_ref[...] = _adam_math(w_ref[...], g, m_ref[...], v_ref[...])

    row = pl.BlockSpec((tm, c), lambda i: (i, 0))
    return pl.pallas_call(
        body,
        name=name,
        grid=(r // tm,),
        in_specs=[pl.BlockSpec((n, tm, c), lambda i: (0, i, 0)), row, row, row],
        out_specs=[row] * 4,
        out_shape=[jax.ShapeDtypeStruct((r, c), F32)] * 4,
        compiler_params=_cparams(("parallel",), 48),
    )(parts, w, m, v)


def _place():
    x, y, c = lax.axis_index("x"), lax.axis_index("y"), lax.axis_index("c")
    return x, y, c


def _other_chips(x, y):
    return [(1 - x, y), (x, 1 - y), (1 - x, 1 - y)]


def _any_spec():
    return pl.BlockSpec(memory_space=pl.ANY)


def _all_gather(blocks, name):
    n = len(blocks)

    def body(*refs):
        in_refs, out_refs = refs[:n], refs[n:2 * n]
        send_sems, recv_sems, local_sems = refs[2 * n:]
        x, y, c = _place()
        me, sibling = (x, y, c), (x, y, 1 - c)
        chips = _other_chips(x, y)

        def slot(p):
            return 4 * p[0] + 2 * p[1] + p[2]

        def copy(t, k, block, to, src=None):
            dst = out_refs[t].at[slot(block)]
            return pltpu.make_async_remote_copy(
                src_ref=dst if src is None else src, dst_ref=dst,
                send_sem=send_sems.at[7 * t + k], recv_sem=recv_sems.at[7 * t + k],
                device_id=to, device_id_type=MESH)

        started = []
        for t in range(n):
            mine = pltpu.make_async_copy(in_refs[t], out_refs[t].at[slot(me)], local_sems.at[t])
            mine.start()
            started.append(mine)
        sends = []
        for t in range(n):
            first = [copy(t, 0, me, sibling, src=in_refs[t])]
            first += [copy(t, 1 + j, me, (*chip, c), src=in_refs[t]) for j, chip in enumerate(chips)]
            for cp in first:
                cp.start()
            sends += first
        for t in range(n):
            for j, chip in enumerate(chips):
                copy(t, 1 + j, (*chip, c), me).wait_recv()
                passed = copy(t, 4 + j, (*chip, c), sibling)
                passed.start()
                sends.append(passed)
        for t in range(n):
            copy(t, 0, sibling, me).wait_recv()
            for j, chip in enumerate(chips):
                copy(t, 4 + j, (*chip, 1 - c), me).wait_recv()
        for cp in sends:
            cp.wait_send()
        for mine in started:
            mine.wait()

    return pl.pallas_call(
        body,
        name=name,
        in_specs=[_any_spec()] * n,
        out_specs=[_any_spec()] * n,
        out_shape=[jax.ShapeDtypeStruct((N_DEV,) + b.shape, b.dtype) for b in blocks],
        scratch_shapes=[pltpu.SemaphoreType.DMA((7 * n,)), pltpu.SemaphoreType.DMA((7 * n,)),
                        pltpu.SemaphoreType.DMA((n,))],
    )(*blocks)


def _to_sibling(grads, name):
    n = len(grads)

    def body(*refs):
        in_refs, out_refs = refs[:n], refs[n:2 * n]
        send_sems, recv_sems = refs[2 * n:]
        x, y, c = _place()
        sends = []
        for t in range(n):
            for chip in range(N_CHIP):
                cp = pltpu.make_async_remote_copy(
                    src_ref=in_refs[t].at[2 * chip + (1 - c)], dst_ref=out_refs[t].at[chip],
                    send_sem=send_sems.at[4 * t + chip], recv_sem=recv_sems.at[4 * t + chip],
                    device_id=(x, y, 1 - c), device_id_type=MESH)
                cp.start()
                sends.append(cp)
        for cp in sends:
            cp.wait()

    return pl.pallas_call(
        body,
        name=name,
        in_specs=[_any_spec()] * n,
        out_specs=[_any_spec()] * n,
        out_shape=[jax.ShapeDtypeStruct((N_CHIP,) + g.shape[1:], g.dtype) for g in grads],
        scratch_shapes=[pltpu.SemaphoreType.DMA((4 * n,)), pltpu.SemaphoreType.DMA((4 * n,))],
    )(*grads)


def _pair_sum(g8, got, name, tm=256):
    _, r, c = g8.shape
    tm = tm if r % tm == 0 else r
    core = lax.axis_index("c")

    def body(core_ref, mine_ref, got_ref, o_ref):
        south_first = core_ref[0] == 0
        a, b = mine_ref[...], got_ref[...]
        o_ref[...] = jnp.where(south_first, a, b) + jnp.where(south_first, b, a)

    return pl.pallas_call(
        body,
        name=name,
        grid_spec=pltpu.PrefetchScalarGridSpec(
            num_scalar_prefetch=1,
            grid=(N_CHIP, r // tm),
            in_specs=[pl.BlockSpec((None, tm, c), lambda j, i, core_ref: (2 * j + core_ref[0], i, 0)),
                      pl.BlockSpec((None, tm, c), lambda j, i, core_ref: (j, i, 0))],
            out_specs=pl.BlockSpec((None, tm, c), lambda j, i, core_ref: (j, i, 0)),
        ),
        out_shape=jax.ShapeDtypeStruct((N_CHIP, r, c), F32),
        compiler_params=_cparams(("parallel", "parallel"), 32),
    )(jnp.reshape(core, (1,)).astype(jnp.int32), g8, got)


def _to_chips(partials, name):
    n = len(partials)

    def body(*refs):
        in_refs, out_refs = refs[:n], refs[n:2 * n]
        send_sems, recv_sems, local_sems = refs[2 * n:]
        x, y, c = _place()
        my_chip = 2 * x + y
        chips = _other_chips(x, y)
        pending = []
        for t in range(n):
            mine = pltpu.make_async_copy(in_refs[t].at[my_chip], out_refs[t].at[my_chip], local_sems.at[t])
            mine.start()
            pending.append(mine)
        sends = []
        for t in range(n):
            for j, chip in enumerate(chips):
                cp = pltpu.make_async_remote_copy(
                    src_ref=in_refs[t].at[2 * chip[0] + chip[1]], dst_ref=out_refs[t].at[my_chip],
                    send_sem=send_sems.at[3 * t + j], recv_sem=recv_sems.at[3 * t + j],
                    device_id=(*chip, c), device_id_type=MESH)
                cp.start()
                sends.append((t, j, chip, cp))
        for t, j, chip, cp in sends:
            cp.wait_send()
            pltpu.make_async_remote_copy(
                src_ref=in_refs[t].at[my_chip], dst_ref=out_refs[t].at[2 * chip[0] + chip[1]],
                send_sem=send_sems.at[3 * t + j], recv_sem=recv_sems.at[3 * t + j],
                device_id=(*chip, c), device_id_type=MESH).wait_recv()
        for mine in pending:
            mine.wait()

    return pl.pallas_call(
        body,
        name=name,
        in_specs=[_any_spec()] * n,
        out_specs=[_any_spec()] * n,
        out_shape=[jax.ShapeDtypeStruct(p.shape, p.dtype) for p in partials],
        scratch_shapes=[pltpu.SemaphoreType.DMA((3 * n,)), pltpu.SemaphoreType.DMA((3 * n,)),
                        pltpu.SemaphoreType.DMA((n,))],
    )(*partials)


def _all_reduce_small(s, name="small_all_reduce"):
    r, c = s.shape

    def body(s_ref, o_ref, buf_ref, send_sems, recv_sems):
        x, y, c_ = _place()
        me = 4 * x + 2 * y + c_
        buf_ref[me] = s_ref[...]

        def copy(k, slot, peer):
            return pltpu.make_async_remote_copy(
                src_ref=s_ref, dst_ref=buf_ref.at[slot],
                send_sem=send_sems.at[k - 1], recv_sem=recv_sems.at[k - 1],
                device_id=peer, device_id_type=MESH)

        peers = []
        for k in range(1, N_DEV):
            peer = (x ^ ((k >> 2) & 1), y ^ ((k >> 1) & 1), c_ ^ (k & 1))
            peers.append(peer)
            copy(k, me, peer).start()
        for k, peer in zip(range(1, N_DEV), peers):
            cp = copy(k, 4 * peer[0] + 2 * peer[1] + peer[2], peer)
            cp.wait_send()
            cp.wait_recv()
        total = buf_ref[0]
        for j in range(1, N_DEV):
            total = total + buf_ref[j]
        o_ref[...] = total

    return pl.pallas_call(
        body,
        name=name,
        out_shape=jax.ShapeDtypeStruct((r, c), F32),
        in_specs=[pl.BlockSpec(memory_space=pltpu.VMEM)],
        out_specs=pl.BlockSpec(memory_space=pltpu.VMEM),
        scratch_shapes=[pltpu.VMEM((N_DEV, r, c), F32), pltpu.SemaphoreType.DMA((N_DEV - 1,)),
                        pltpu.SemaphoreType.DMA((N_DEV - 1,))],
    )(s)


def _heads_to_cols(a, nh):
    return jnp.transpose(a[:, :nh])[:, :, None]


def _local_step(x, target, meta, gamma, a_norm, wa_in, a_out_norm, wa_out, kv_norm, wkv2, wfl, b_f, g_k,
                b_norm, wb_in, g_q, wb_out):
    d = x.shape[1]
    nh = d // HEAD
    h0 = jnp.concatenate([jnp.zeros((ROW_PAD, d), F32), meta, x], axis=0)
    big = dict(tm=1408, tn=512, tk=2048)

    lb = _lb_fwd(gamma)
    (hn_a,) = _rms_fwd(h0, [a_norm], "rms_a")
    u4 = _matmul(hn_a, wa_in, "nn", F32, "a_in", out_parts=4, **big)
    o_a, og_a = _hgrn2_fwd(u4, lb, a_out_norm)
    h1 = _matmul(og_a, wa_out, "nn", F32, "a_out", add=h0, **big)
    hk, hb = _rms_fwd(h1, [kv_norm, b_norm], "rms_kv_b")
    ukv = _matmul(hk, wkv2, "nn", F32, "kv_in", out_parts=2, **big)
    ufl = _matmul(hk, wfl, "nn", F32, "kv_f", **big)
    ub = _matmul(hb, wb_in, "nn", F32, "b_in", out_parts=2, **big)
    kn = _head_rms_fwd(ukv, 0, g_k, "k_norm")
    qn = _head_rms_fwd(ub, 0, g_q, "q_norm")
    f_cum = _fgate_fwd(ufl, b_f)
    fq = _heads_to_cols(f_cum, nh)
    fk = jnp.transpose(fq, (0, 2, 1))
    o_b, o_lo, og_b, lse = _fox_fwd(qn, kn, ukv, ub, fq, fk)
    h2 = _matmul(og_b, wb_out, "nn", F32, "b_out", add=h1, **big)
    loss, dh2 = _loss(h2, target)

    dx_t = dict(tm=1408, tn=512, tk=1024)
    dw_t = dict(tm=1024, tn=1024, tk=1408, vmem_mib=56)
    d_ogb = _matmul(dh2, wb_out, "nt", F32, "b_out_dx", **dx_t)
    dwb_out = _matmul(og_b, dh2, "tn", F32, "b_out_dw", **dw_t)
    do_b, delta = _fox_do(d_ogb, o_b, o_lo, ub)
    dqn, dkn, dv, dfk = _fox_bwd(qn, kn, ukv, do_b, fq - lse, fk, _heads_to_cols(delta, nh))
    dub, dg_q = _fox_q_bwd(ub, d_ogb, o_b, dqn, g_q)
    dukv, dg_k = _fox_k_bwd(ukv, dkn, dv, g_k)
    d_f = jnp.pad(jnp.transpose(dfk[:, 0, :]), ((0, 0), (0, 128 - nh)))
    dufl, db_f = _fgate_bwd(ufl, b_f, d_f)
    d_hb = _matmul(dub, wb_in, "nt", F32, "b_in_dx", **dx_t)
    dwb_in = _matmul(hb, dub, "tn", F32, "b_in_dw", out_parts=N_DEV, **dw_t)
    d_hk = _matmul(dukv, wkv2, "nt", F32, "kv_dx", **dx_t)
    d_hk = _matmul(dufl, wfl, "nt", F32, "kv_f_dx", add=d_hk, **dx_t)
    dwkv2 = _matmul(hk, dukv, "tn", F32, "kv_dw", **dw_t)
    dwfl = _matmul(hk, dufl, "tn", F32, "kv_f_dw", **dw_t)
    dh1, (dg_kv, dg_b) = _rms_bwd(h1, [kv_norm, b_norm], [d_hk, d_hb], dh2, "rms_kv_b_bwd")
    d_oga = _matmul(dh1, wa_out, "nt", F32, "a_out_dx", **dx_t)
    dwa_out = _matmul(og_a, dh1, "tn", F32, "a_out_dw", **dw_t)
    du4, dlb, dg_aout = _hgrn2_bwd(u4, o_a, d_oga, lb, a_out_norm)
    d_hna = _matmul(du4, wa_in, "nt", F32, "a_in_dx", **dx_t)
    dwa_in = _matmul(hn_a, du4, "tn", F32, "a_in_dw", out_parts=N_DEV, **dw_t)
    dh0, (dg_a,) = _rms_bwd(h0, [a_norm], [d_hna], dh1, "rms_a_bwd")
    dgamma = _lb_bwd(gamma, dlb)

    grads = dict(meta=dh0[ROW_PAD:ROW_PAD + N_META], gamma=dgamma, a_norm=dg_a, wa_in=dwa_in, a_out_norm=dg_aout,
                 wa_out=dwa_out, kv_norm=dg_kv, wkv2=dwkv2, wfl=dwfl, b_f=db_f, g_k=dg_k, b_norm=dg_b,
                 wb_in=dwb_in, g_q=dg_q, wb_out=dwb_out)
    return loss, dh0[ROW_PAD + N_META:], grads


def _reduce_scatter_adamw(g8s, ws, ms, vs, names):
    got = _to_sibling(g8s, "grads_to_sibling")
    partials = [_pair_sum(g8, g, "pair_sum_" + nm) for g8, g, nm in zip(g8s, got, names)]
    parts = _to_chips(partials, "grads_to_chips")
    return [_adamw(p, w, m, v, "adamw_" + nm) for p, w, m, v, nm in zip(parts, ws, ms, vs, names)]


def kernel(x, meta, gamma_lb, a_norm, a_w_in, a_out_norm, a_w_out, kv_norm, kv_w, fox_b_f, fox_k_norm, b_norm, b_w_in, b_q_norm, b_w_out, loss_target, m_meta, m_gamma_lb, m_a_norm, m_a_w_in, m_a_out_norm, m_a_w_out, m_kv_norm, m_kv_w, m_fox_b_f, m_fox_k_norm, m_b_norm, m_b_w_in, m_b_q_norm, m_b_w_out, v_meta, v_gamma_lb, v_a_norm, v_a_w_in, v_a_out_norm, v_a_w_out, v_kv_norm, v_kv_w, v_fox_b_f, v_fox_k_norm, v_b_norm, v_b_w_in, v_b_q_norm, v_b_w_out):
    d = x.shape[-1]
    nh = d // HEAD
    cols = d // N_DEV
    me = 4 * lax.axis_index("x") + 2 * lax.axis_index("y") + lax.axis_index("c")

    sharded_small = jnp.concatenate([meta, gamma_lb, a_norm, a_out_norm, jnp.zeros((4, cols), F32)], axis=0)
    g_a_in, g_a_out, g_kv, g_b_in, g_b_out, g_small = _all_gather(
        [a_w_in[0].astype(MXU), a_w_out[0].astype(MXU), kv_w.astype(MXU), b_w_in[0].astype(MXU),
         b_w_out[0].astype(MXU), sharded_small], "gather_weights")
    wa_out = jnp.reshape(g_a_out, (d, d))
    wb_out = jnp.reshape(g_b_out, (d, d))
    wkv = jnp.reshape(jnp.transpose(g_kv, (1, 0, 2)), (d, -1))
    wkv2 = wkv[:, :2 * d]
    wfl = jnp.pad(wkv[:, 2 * d:], ((0, 0), (0, 128 - nh)))
    small = jnp.reshape(jnp.transpose(g_small, (1, 0, 2)), (-1, d))
    meta_f, gamma_f, a_norm_f, a_out_norm_f = small[:16], small[16:18], small[18:19], small[19:20]
    b_f = jnp.pad(jnp.reshape(fox_b_f, (1, nh)), ((0, 0), (0, 128 - nh)))
    g_k = jnp.reshape(fox_k_norm, (1, d))
    g_q = jnp.reshape(b_q_norm, (1, d))
    kv_norm_r = jnp.reshape(kv_norm, (1, d))

    loss, grad_x, g = _local_step(x[0], loss_target[0], meta_f, gamma_f, a_norm_f, g_a_in, a_out_norm_f, wa_out,
                                  kv_norm_r, wkv2, wfl, b_f, g_k, b_norm, g_b_in, g_q, wb_out)
    loss = lax.psum(loss[0, 0], AXES)

    dwkv = jnp.concatenate([g["wkv2"], g["wfl"][:, :nh]], axis=1)
    dwkv8 = jnp.transpose(jnp.reshape(dwkv, (d, N_DEV, -1)), (1, 0, 2))
    big = _reduce_scatter_adamw(
        [g["wa_in"], jnp.reshape(g["wa_out"], (N_DEV, cols, d)), dwkv8, g["wb_in"], jnp.reshape(g["wb_out"], (N_DEV, cols, d))],
        [a_w_in[0], a_w_out[0], kv_w, b_w_in[0], b_w_out[0]],
        [m_a_w_in[0], m_a_w_out[0], m_kv_w, m_b_w_in[0], m_b_w_out[0]],
        [v_a_w_in[0], v_a_w_out[0], v_kv_w, v_b_w_in[0], v_b_w_out[0]],
        ["a_w_in", "a_w_out", "kv_w", "b_w_in", "b_w_out"])
    r_a_in, r_a_out, r_kv, r_b_in, r_b_out = big

    packed = jnp.concatenate(
        [g["meta"], g["gamma"], g["a_norm"], g["a_out_norm"], g["kv_norm"], g["b_norm"], g["g_k"], g["g_q"],
         jnp.pad(g["b_f"], ((0, 0), (0, d - 128))), jnp.zeros((7, d), F32)], axis=0)
    tot = _all_reduce_small(packed)
    mine = lax.dynamic_slice_in_dim(tot[:20], me * cols, cols, axis=1)
    gs = [mine[:16], mine[16:18], mine[18:19], mine[19:20], tot[20], tot[21:22], jnp.reshape(tot[22], (nh, HEAD)),
          jnp.reshape(tot[23], (1, nh, HEAD)), tot[24, :nh]]
    small_w = [meta, gamma_lb, a_norm, a_out_norm, kv_norm, b_norm, fox_k_norm, b_q_norm, fox_b_f]
    small_m = [m_meta, m_gamma_lb, m_a_norm, m_a_out_norm, m_kv_norm, m_b_norm, m_fox_k_norm, m_b_q_norm, m_fox_b_f]
    small_v = [v_meta, v_gamma_lb, v_a_norm, v_a_out_norm, v_kv_norm, v_b_norm, v_fox_k_norm, v_b_q_norm, v_fox_b_f]

    def pack(arrs):
        rows = [jnp.reshape(a, (-1, cols)) for a in arrs[:8]]
        rows.append(jnp.pad(jnp.reshape(arrs[8], (1, nh)), ((0, 0), (0, cols - nh))))
        n_rows = sum(r.shape[0] for r in rows)
        return jnp.concatenate(rows + [jnp.zeros((-n_rows % 8, cols), F32)], axis=0)

    _, sd, sm, sv = _adamw(pack(gs)[None], pack(small_w), pack(small_m), pack(small_v), "adamw_small")

    def unpack(p):
        out, at = [], 0
        for a in small_w[:8]:
            n = a.size // cols
            out.append(jnp.reshape(p[at:at + n], a.shape))
            at += n
        out.append(p[at, :nh])
        return out

    d_s, m_s, v_s = unpack(sd), unpack(sm), unpack(sv)

    def ordered(s, a_in, a_out, kv, b_in, b_out):
        return [s[0], s[1], s[2], a_in[None], s[3], a_out[None], s[4], kv, s[8], s[6], s[5], b_in[None], s[7], b_out[None]]

    outs = []
    for i, s in enumerate([gs, d_s, m_s, v_s]):
        outs += ordered(s, r_a_in[i], r_a_out[i], r_kv[i], r_b_in[i], r_b_out[i])
    return (loss, grad_x[None], *outs)
```

```python
import math

import jax
import jax.numpy as jnp
from jax import lax
from jax.experimental import pallas as pl
from jax.experimental.pallas import tpu as pltpu

HEAD = 128
CHUNK = 64
N_META = 16
ROW_PAD = 128 - N_META
EPS = 1e-6
MASK_VALUE = -1e30
ADAM_LR = 0.001
ADAM_B1 = 0.9
ADAM_B2 = 0.999
ADAM_EPS = 1e-08
ADAM_WD = 0.01
ADAM_STEP = 10
N_DEV = 8
N_CHIP = 4
MIB = 1024 * 1024
AXES = ("x", "y", "c")
MESH = pl.DeviceIdType.MESH

F32 = jnp.float32
MXU = jnp.bfloat16
PAYLOAD = jnp.bfloat16
HI = lax.Precision.HIGHEST

NN = (((1,), (0,)), ((), ()))
NT = (((1,), (1,)), ((), ()))
TN = (((0,), (0,)), ((), ()))


def _dot(a, b, dims=NN):
    return lax.dot_general(a.astype(MXU), b.astype(MXU), dims, preferred_element_type=F32)


def _dot_exact(a, b):
    return lax.dot_general(a, b, NN, precision=HI, preferred_element_type=F32)


def _sigmoid(x):
    return 1.0 / (1.0 + jnp.exp(-x))


def _tile(dim, target):
    best = None
    t = 128
    while t <= min(dim, target):
        if dim % t == 0:
            best = t
        t += 128
    return best if best is not None else dim


def _cparams(semantics, vmem_mib):
    return pltpu.CompilerParams(dimension_semantics=semantics, vmem_limit_bytes=vmem_mib * MIB)


def _mat_spec(arr, br, bc, rc_of_grid):
    if arr.ndim == 2:
        return pl.BlockSpec((br, bc), rc_of_grid)
    assert arr.shape[2] % bc == 0, (arr.shape, bc)
    per = arr.shape[2] // bc

    def idx(*g):
        r, c = rc_of_grid(*g)
        return (c // per, r, c % per)

    return pl.BlockSpec((None, br, bc), idx)


def _mat_shape(arr):
    return (arr.shape[0], arr.shape[1]) if arr.ndim == 2 else (arr.shape[1], arr.shape[0] * arr.shape[2])


def _matmul(a, b, dims, out_dtype, name, *, add=None, out_parts=1, tm=512, tn=512, tk=512, vmem_mib=48):
    ar, ac = _mat_shape(a)
    br_, bc_ = _mat_shape(b)
    if dims == "nn":
        m, k, n = ar, ac, bc_
        assert br_ == k
    elif dims == "nt":
        m, k, n = ar, ac, br_
        assert bc_ == k
    else:
        m, k, n = ac, ar, bc_
        assert br_ == k
    m_unit, n_unit, k_unit = m, n, k
    if a.ndim == 3:
        if dims == "tn":
            m_unit = math.gcd(m_unit, a.shape[2])
        else:
            k_unit = math.gcd(k_unit, a.shape[2])
    if b.ndim == 3:
        if dims == "nt":
            k_unit = math.gcd(k_unit, b.shape[2])
        else:
            n_unit = math.gcd(n_unit, b.shape[2])
    if out_parts > 1:
        n_unit = math.gcd(n_unit, n // out_parts)
    tm, tn, tk = _tile(m_unit, tm), _tile(n_unit, tn), _tile(k_unit, tk)
    gm, gn, gk = m // tm, n // tn, k // tk
    assert gm * tm == m and gn * tn == n and gk * tk == k, (name, m, n, k, tm, tn, tk)

    if dims == "nn":
        a_spec = _mat_spec(a, tm, tk, lambda i, j, kk: (i, kk))
        b_spec = _mat_spec(b, tk, tn, lambda i, j, kk: (kk, j))
        dn = NN
    elif dims == "nt":
        a_spec = _mat_spec(a, tm, tk, lambda i, j, kk: (i, kk))
        b_spec = _mat_spec(b, tn, tk, lambda i, j, kk: (j, kk))
        dn = NT
    else:
        a_spec = _mat_spec(a, tk, tm, lambda i, j, kk: (kk, i))
        b_spec = _mat_spec(b, tk, tn, lambda i, j, kk: (kk, j))
        dn = TN

    if out_parts > 1:
        per = (n // out_parts) // tn
        out_shape = jax.ShapeDtypeStruct((out_parts, m, n // out_parts), out_dtype)
        o_spec = pl.BlockSpec((None, tm, tn), lambda i, j, kk: (j // per, i, j % per))
    else:
        out_shape = jax.ShapeDtypeStruct((m, n), out_dtype)
        o_spec = pl.BlockSpec((tm, tn), lambda i, j, kk: (i, j))

    in_specs = [a_spec, b_spec]
    args = [a, b]
    if add is not None:
        in_specs.append(pl.BlockSpec((tm, tn), lambda i, j, kk: (i, j)))
        args.append(add)

    def body(*refs):
        if add is not None:
            a_ref, b_ref, add_ref, o_ref, acc_ref = refs
        else:
            a_ref, b_ref, o_ref, acc_ref = refs
            add_ref = None
        kk = pl.program_id(2)
        part = lax.dot_general(a_ref[...].astype(MXU), b_ref[...].astype(MXU), dn, preferred_element_type=F32)

        def finish(total):
            if add_ref is not None:
                total = total + add_ref[...]
            o_ref[...] = total.astype(o_ref.dtype)

        if gk == 1:
            finish(part)
        else:
            @pl.when(kk == 0)
            def _():
                acc_ref[...] = part

            @pl.when(jnp.logical_and(kk > 0, kk < gk - 1))
            def _():
                acc_ref[...] += part

            @pl.when(kk == gk - 1)
            def _():
                finish(acc_ref[...] + part)

    return pl.pallas_call(
        body,
        name=name,
        grid=(gm, gn, gk),
        in_specs=in_specs,
        out_specs=o_spec,
        out_shape=out_shape,
        scratch_shapes=[pltpu.VMEM((tm, tn) if gk > 1 else (8, 128), F32)],
        compiler_params=_cparams(("parallel", "parallel", "arbitrary"), vmem_mib),
    )(*args)


def _rms_fwd(h, gains, name, tm=384):
    lp, d = h.shape
    tm = _tile(lp, tm)
    n = len(gains)

    def body(*refs):
        h_ref = refs[0]
        g_refs = refs[1:1 + n]
        o_refs = refs[1 + n:]
        x = h_ref[...]
        y = x * lax.rsqrt(jnp.mean(x * x, axis=-1, keepdims=True) + EPS)
        for g_ref, o_ref in zip(g_refs, o_refs):
            o_ref[...] = (y * g_ref[...]).astype(o_ref.dtype)

    row = pl.BlockSpec((tm, d), lambda i: (i, 0))
    vec = pl.BlockSpec((1, d), lambda i: (0, 0))
    return pl.pallas_call(
        body,
        name=name,
        grid=(lp // tm,),
        in_specs=[row] + [vec] * n,
        out_specs=[row] * n,
        out_shape=[jax.ShapeDtypeStruct((lp, d), MXU)] * n,
        compiler_params=_cparams(("parallel",), 40),
    )(h, *gains)


def _rms_bwd(h, gains, dys, res, name, tm=384):
    lp, d = h.shape
    tm = _tile(lp, tm)
    n = len(gains)

    def body(*refs):
        h_ref, res_ref = refs[0], refs[1]
        g_refs = refs[2:2 + n]
        dy_refs = refs[2 + n:2 + 2 * n]
        dh_ref = refs[2 + 2 * n]
        dg_refs = refs[3 + 2 * n:]
        i = pl.program_id(0)
        x = h_ref[...]
        rstd = lax.rsqrt(jnp.mean(x * x, axis=-1, keepdims=True) + EPS)
        xhat = x * rstd
        dh = res_ref[...]
        for g_ref, dy_ref, dg_ref in zip(g_refs, dy_refs, dg_refs):
            dy = dy_ref[...]
            gdy = dy * g_ref[...]
            dh = dh + rstd * (gdy - xhat * jnp.mean(gdy * xhat, axis=-1, keepdims=True))
            part = jnp.sum(dy * xhat, axis=0, keepdims=True)

            @pl.when(i == 0)
            def _():
                dg_ref[...] = part

            @pl.when(i > 0)
            def _():
                dg_ref[...] += part

        dh_ref[...] = dh

    row = pl.BlockSpec((tm, d), lambda i: (i, 0))
    vec = pl.BlockSpec((1, d), lambda i: (0, 0))
    outs = pl.pallas_call(
        body,
        name=name,
        grid=(lp // tm,),
        in_specs=[row, row] + [vec] * n + [row] * n,
        out_specs=[row] + [vec] * n,
        out_shape=[jax.ShapeDtypeStruct((lp, d), F32)] + [jax.ShapeDtypeStruct((1, d), F32)] * n,
        compiler_params=_cparams(("arbitrary",), 56),
    )(h, res, *gains, *dys)
    return outs[0], list(outs[1:])


def _lb_fwd(gamma):
    def body(g_ref, lb_ref):
        g = g_ref[...]
        e = jnp.exp(g - jnp.max(g, axis=0, keepdims=True))
        lb_ref[...] = (e / jnp.sum(e, axis=0, keepdims=True))[0:1, :]

    return pl.pallas_call(body, name="lb_fwd", out_shape=jax.ShapeDtypeStruct((1, gamma.shape[1]), F32))(gamma)


def _lb_bwd(gamma, dlb):
    def body(g_ref, dlb_ref, dg_ref):
        g = g_ref[...]
        e = jnp.exp(g - jnp.max(g, axis=0, keepdims=True))
        s = e / jnp.sum(e, axis=0, keepdims=True)
        rows = lax.broadcasted_iota(jnp.int32, g.shape, 0)
        ds = jnp.where(rows == 0, dlb_ref[...], 0.0)
        dg_ref[...] = s * (ds - jnp.sum(s * ds, axis=0, keepdims=True))

    return pl.pallas_call(body, name="lb_bwd", out_shape=jax.ShapeDtypeStruct(gamma.shape, F32))(gamma, dlb)


def _tri(n, lower):
    r = lax.broadcasted_iota(jnp.int32, (n, n), 0)
    c = lax.broadcasted_iota(jnp.int32, (n, n), 1)
    return jnp.where((r >= c) if lower else (r <= c), 1.0, 0.0).astype(F32)


def _group(nc):
    return max(u for u in (1, 2, 3) if nc % u == 0)


def _running_sum(tri, x):
    hi = x.astype(MXU)
    rest = x - hi.astype(F32)
    mid = rest.astype(MXU)
    lo = (rest - mid.astype(F32)).astype(MXU)
    return _dot(tri, hi) + _dot(tri, mid) + _dot(tri, lo)


def _causal(n):
    r = lax.broadcasted_iota(jnp.int32, (n, n), 0)
    c = lax.broadcasted_iota(jnp.int32, (n, n), 1)
    return r >= c


def _chunk_gates(u_ref, lb, c):
    sl = pl.ds(pl.multiple_of(c * CHUNK, CHUNK), CHUNK)
    valid = (c * CHUNK + lax.broadcasted_iota(jnp.int32, (CHUNK, HEAD), 0)) >= ROW_PAD
    uq = u_ref[0, sl, :]
    uf = u_ref[1, sl, :]
    sq = _sigmoid(uq)
    sf = _sigmoid(uf)
    fg = lb + (1.0 - lb) * sf
    return dict(sl=sl, valid=valid, uq=uq, sq=sq, sf=sf, fg=fg, q=jnp.where(valid, uq * sq, 0.0),
                logf=jnp.where(valid, jnp.log(fg), 0.0), k=jnp.where(valid, 1.0 - fg, 0.0),
                v=jnp.where(valid, u_ref[2, sl, :], 0.0))


def _chunk_decays(x, b):
    b_last = b[CHUNK - 1:CHUNK, :]
    b_mid = b[CHUNK // 2 - 1:CHUNK // 2, :]
    e_qi = jnp.exp(b - b_mid)
    e_ki = jnp.exp(b_mid - b)
    e_kd = jnp.exp(b_last - b)
    e_qe = jnp.exp(b)
    q, k = x["q"], x["k"]
    return dict(x, e_qi=e_qi, e_ki=e_ki, e_kd=e_kd, e_qe=e_qe, qi=q * e_qi, ki=k * e_ki, kd=k * e_kd, qe=q * e_qe,
                decay=jnp.exp(b_last))


def _chunks(u_ref, lb, tri_lower, cs):
    gates = [_chunk_gates(u_ref, lb, c) for c in cs]
    sums = [_running_sum(tri_lower, x["logf"]) for x in gates]
    return [_chunk_decays(x, b) for x, b in zip(gates, sums)]


def _hgrn2_fwd(u4, lb, g_out, name="hgrn2_fwd"):
    _, lp, d = u4.shape
    nh, nc = d // HEAD, lp // CHUNK
    per = _group(nc)

    def body(u_ref, lb_ref, g_ref, o_ref, og_ref):
        lb_v = lb_ref[...]
        g = g_ref[...]

        tri_lower = _tri(CHUNK, True).astype(MXU)
        causal = _causal(CHUNK)

        def step(i, st):
            xs = _chunks(u_ref, lb_v, tri_lower, [i * per + u for u in range(per)])
            scores = [_dot(x["qi"], x["ki"], NT) for x in xs]
            updates = [_dot(x["v"], x["kd"], TN) for x in xs]
            states = []
            for x, upd in zip(xs, updates):
                states.append(st)
                st = x["decay"] * st + upd
            outs = [_dot(jnp.where(causal, a, 0.0), x["v"]) + _dot(x["qe"], s, NT)
                    for x, a, s in zip(xs, scores, states)]
            for x, o in zip(xs, outs):
                o_ref[x["sl"], :] = o
                on = o * lax.rsqrt(jnp.mean(o * o, axis=-1, keepdims=True) + EPS) * g
                z = u_ref[3, x["sl"], :]
                og_ref[x["sl"], :] = (on * (z * _sigmoid(z))).astype(og_ref.dtype)
            return st

        lax.fori_loop(0, nc // per, step, jnp.zeros((HEAD, HEAD), F32))

    slab = pl.BlockSpec((lp, HEAD), lambda h: (0, h))
    vec = pl.BlockSpec((1, HEAD), lambda h: (0, h))
    return pl.pallas_call(
        body,
        name=name,
        grid=(nh,),
        in_specs=[pl.BlockSpec((4, lp, HEAD), lambda h: (0, 0, h)), vec, vec],
        out_specs=[slab, slab],
        out_shape=[jax.ShapeDtypeStruct((lp, d), F32), jax.ShapeDtypeStruct((lp, d), MXU)],
        compiler_params=_cparams(("parallel",), 48),
    )(u4, lb, g_out)


def _hgrn2_bwd(u4, o, d_og, lb, g_out, name="hgrn2_bwd"):
    _, lp, d = u4.shape
    nh, nc = d // HEAD, lp // CHUNK
    per = _group(nc)

    def body(u_ref, o_ref, dog_ref, lb_ref, g_ref, du_ref, dlb_ref, dg_ref, st_ref, do_ref):
        lb_v = lb_ref[...]
        g = g_ref[...]

        tri_lower = _tri(CHUNK, True).astype(MXU)
        tri_upper = _tri(CHUNK, False).astype(MXU)
        causal = _causal(CHUNK)

        def fwd_step(i, carry):
            st, dg_acc = carry
            cs = [i * per + u for u in range(per)]
            xs = _chunks(u_ref, lb_v, tri_lower, cs)
            updates = [_dot(x["v"], x["kd"], TN) for x in xs]
            for c, x, upd in zip(cs, xs, updates):
                st_ref[c] = st
                st = x["decay"] * st + upd
            for x in xs:
                sl = x["sl"]
                ov = o_ref[sl, :]
                rstd = lax.rsqrt(jnp.mean(ov * ov, axis=-1, keepdims=True) + EPS)
                on = ov * rstd
                z = u_ref[3, sl, :]
                sz = _sigmoid(z)
                dog = dog_ref[sl, :]
                dy = dog * (z * sz)
                dz = dog * (on * g) * (sz * (1.0 + z * (1.0 - sz)))
                du_ref[3, sl, :] = dz.astype(du_ref.dtype)
                gdy = dy * g
                do = rstd * (gdy - on * jnp.mean(gdy * on, axis=-1, keepdims=True))
                do_ref[sl, :] = jnp.where(x["valid"], do, 0.0)
                dg_acc = dg_acc + jnp.sum(dy * on, axis=0, keepdims=True)
            return st, dg_acc

        _, dg_tot = lax.fori_loop(0, nc // per, fwd_step, (jnp.zeros((HEAD, HEAD), F32), jnp.zeros((1, HEAD), F32)))
        dg_ref[...] = dg_tot

        def bwd_step(i, carry):
            gt, dlb_acc = carry
            cs = [nc - 1 - (i * per + u) for u in range(per)]
            xs = _chunks(u_ref, lb_v, tri_lower, cs)
            dos = [do_ref[x["sl"], :] for x in xs]
            sts = [st_ref[c] for c in cs]
            scores = [jnp.where(causal, _dot(x["qi"], x["ki"], NT), 0.0) for x in xs]
            d_scores = [jnp.where(causal, _dot(do, x["v"], NT), 0.0) for x, do in zip(xs, dos)]
            d_qes = [_dot(do, st) for do, st in zip(dos, sts)]
            g_updates = [_dot(do, x["qe"], TN) for x, do in zip(xs, dos)]
            gts = []
            for x, upd in zip(xs, g_updates):
                gts.append(gt)
                gt = x["decay"] * gt + upd
            d_kds = [_dot(x["v"], g_) for x, g_ in zip(xs, gts)]
            dvs = [_dot(x["kd"], g_, NT) + _dot(a, do, TN) for x, g_, a, do in zip(xs, gts, scores, dos)]
            d_qis = [_dot(d_a, x["ki"]) for x, d_a in zip(xs, d_scores)]
            d_kis = [_dot(d_a, x["qi"], TN) for x, d_a in zip(xs, d_scores)]
            rows = lax.broadcasted_iota(jnp.int32, (CHUNK, HEAD), 0)
            dbs = []
            for x, g_, st, d_qi, d_ki, d_qe, d_kd in zip(xs, gts, sts, d_qis, d_kis, d_qes, d_kds):
                t_qi, t_ki, t_qe, t_kd = d_qi * x["qi"], d_ki * x["ki"], d_qe * x["qe"], d_kd * x["kd"]
                d_decay = jnp.sum(g_ * st, axis=0, keepdims=True)
                d_mid = jnp.sum(t_ki - t_qi, axis=0, keepdims=True)
                d_last = jnp.sum(t_kd, axis=0, keepdims=True) + d_decay * x["decay"]
                dbs.append(t_qi - t_ki + t_qe - t_kd + jnp.where(rows == CHUNK // 2 - 1, d_mid, 0.0)
                           + jnp.where(rows == CHUNK - 1, d_last, 0.0))
            dlogfs = [_running_sum(tri_upper, db) for db in dbs]
            for x, dlogf, dv, d_qi, d_ki, d_qe, d_kd in zip(xs, dlogfs, dvs, d_qis, d_kis, d_qes, d_kds):
                sl = x["sl"]
                dq = d_qi * x["e_qi"] + d_qe * x["e_qe"]
                dk = d_ki * x["e_ki"] + d_kd * x["e_kd"]
                valid, sq, sf, uq = x["valid"], x["sq"], x["sf"], x["uq"]
                dfg = jnp.where(valid, dlogf / x["fg"] - dk, 0.0)
                du_ref[0, sl, :] = jnp.where(valid, dq * (sq * (1.0 + uq * (1.0 - sq))), 0.0).astype(du_ref.dtype)
                du_ref[1, sl, :] = (dfg * (1.0 - lb_v) * (sf * (1.0 - sf))).astype(du_ref.dtype)
                du_ref[2, sl, :] = jnp.where(valid, dv, 0.0).astype(du_ref.dtype)
                dlb_acc = dlb_acc + jnp.sum(dfg * (1.0 - sf), axis=0, keepdims=True)
            return gt, dlb_acc

        _, dlb_tot = lax.fori_loop(0, nc // per, bwd_step, (jnp.zeros((HEAD, HEAD), F32), jnp.zeros((1, HEAD), F32)))
        dlb_ref[...] = dlb_tot

    slab = pl.BlockSpec((lp, HEAD), lambda h: (0, h))
    vec = pl.BlockSpec((1, HEAD), lambda h: (0, h))
    quad = pl.BlockSpec((4, lp, HEAD), lambda h: (0, 0, h))
    return pl.pallas_call(
        body,
        name=name,
        grid=(nh,),
        in_specs=[quad, slab, slab, vec, vec],
        out_specs=[quad, vec, vec],
        out_shape=[jax.ShapeDtypeStruct((4, lp, d), MXU), jax.ShapeDtypeStruct((1, d), F32),
                   jax.ShapeDtypeStruct((1, d), F32)],
        scratch_shapes=[pltpu.VMEM((nc, HEAD, HEAD), F32), pltpu.VMEM((lp, HEAD), F32)],
        compiler_params=_cparams(("parallel",), 58),
    )(u4, o, d_og, lb, g_out)


WIDE = 2 * HEAD
INV_SCALE = HEAD ** 0.5


def _split3(x):
    hi = x.astype(MXU).astype(F32)
    rest = x - hi
    mid = rest.astype(MXU).astype(F32)
    return hi, mid, (rest - mid).astype(MXU).astype(F32)


def _extra_cols(rows, first, second):
    lane = lax.broadcasted_iota(jnp.int32, (rows, HEAD), 1)
    out = jnp.where(lane < 6, 1.0, 0.0).astype(F32)
    for base, terms in ((0, first), (3, second)):
        if terms is not None:
            for j, term in enumerate(terms):
                out = jnp.where(lane == base + j, term, out)
    return out


def _head_col(a, h):
    lane = lax.broadcasted_iota(jnp.int32, a.shape, 1)
    return jnp.sum(jnp.where(lane == h, a, 0.0), axis=-1, keepdims=True)


def _attn_operands(ub, ukv, g_q, g_k, f_cum, name="attn_operands", tm=384):
    _, lp, d = ub.shape
    tm = _tile(lp, tm)
    nh = d // HEAD

    def body(q_ref, k_ref, v_ref, gq_ref, gk_ref, f_ref, qa_ref, ka_ref, va_ref):
        i = pl.program_id(0)
        f = f_ref[...]
        is_pad = (i * tm + lax.broadcasted_iota(jnp.int32, (tm, 1), 0)) < ROW_PAD
        ones_only = _extra_cols(tm, None, (0.0, 0.0, 0.0)).astype(MXU)
        for h in range(nh):
            hs = slice(h * HEAD, (h + 1) * HEAD)
            lo, hi = h * WIDE, h * WIDE + HEAD
            f_h = _head_col(f, h)
            for x_ref, g_ref, o_ref in ((q_ref, gq_ref, qa_ref), (k_ref, gk_ref, ka_ref)):
                x = x_ref[:, hs]
                y = x * lax.rsqrt(jnp.mean(x * x, axis=-1, keepdims=True) + EPS)
                o_ref[:, lo:hi] = (y * g_ref[:, hs]).astype(o_ref.dtype)
            qa_ref[:, hi:hi + HEAD] = _extra_cols(tm, _split3(f_h * INV_SCALE), None).astype(MXU)
            f_key = jnp.where(is_pad, -MASK_VALUE, f_h)
            ka_ref[:, hi:hi + HEAD] = _extra_cols(tm, None, _split3(-f_key * INV_SCALE)).astype(MXU)
            va_ref[:, lo:hi] = v_ref[:, hs].astype(MXU)
            va_ref[:, hi:hi + HEAD] = ones_only

    row = pl.BlockSpec((tm, d), lambda i: (i, 0))
    wide = pl.BlockSpec((tm, nh * WIDE), lambda i: (i, 0))
    vec = pl.BlockSpec((1, d), lambda i: (0, 0))
    return pl.pallas_call(
        body,
        name=name,
        grid=(lp // tm,),
        in_specs=[pl.BlockSpec((None, tm, d), lambda i: (0, i, 0)), pl.BlockSpec((None, tm, d), lambda i: (0, i, 0)),
                  pl.BlockSpec((None, tm, d), lambda i: (1, i, 0)), vec, vec, pl.BlockSpec((tm, 128), lambda i: (i, 0))],
        out_specs=[wide, wide, wide],
        out_shape=[jax.ShapeDtypeStruct((lp, nh * WIDE), MXU)] * 3,
        compiler_params=_cparams(("parallel",), 56),
    )(ub, ukv, ukv, g_q, g_k, f_cum)


def _head_rms_bwd_tile(x, g, dy):
    rstd = lax.rsqrt(jnp.mean(x * x, axis=-1, keepdims=True) + EPS)
    xhat = x * rstd
    gdy = dy * g
    dx = rstd * (gdy - xhat * jnp.mean(gdy * xhat, axis=-1, keepdims=True))
    return dx, jnp.sum(dy * xhat, axis=0, keepdims=True)


def _fgate_fwd(ufl, b_f):
    lp = ufl.shape[0]
    nb = lp // 128

    def body(u_ref, b_ref, f_ref):
        def step(i, carry):
            sl = pl.ds(pl.multiple_of(i * 128, 128), 128)
            valid = (i * 128 + lax.broadcasted_iota(jnp.int32, (128, 128), 0)) >= ROW_PAD
            x = u_ref[sl, :] + b_ref[...]
            logf = jnp.where(valid, jnp.minimum(x, 0.0) - jnp.log(1.0 + jnp.exp(-jnp.abs(x))), 0.0)
            f = _dot_exact(_tri(128, True), logf) + carry
            f_ref[sl, :] = f
            return f[127:128, :]

        lax.fori_loop(0, nb, step, jnp.zeros((1, 128), F32))

    return pl.pallas_call(body, name="fgate_fwd", out_shape=jax.ShapeDtypeStruct((lp, 128), F32))(ufl, b_f)


def _fgate_bwd(ufl, b_f, d_f):
    lp = ufl.shape[0]
    nb = lp // 128

    def body(u_ref, b_ref, df_ref, du_ref, db_ref):
        def step(j, carry):
            later, db_acc = carry
            i = nb - 1 - j
            sl = pl.ds(pl.multiple_of(i * 128, 128), 128)
            valid = (i * 128 + lax.broadcasted_iota(jnp.int32, (128, 128), 0)) >= ROW_PAD
            x = u_ref[sl, :] + b_ref[...]
            df = df_ref[sl, :]
            dlogf = _dot_exact(_tri(128, False), df) + later
            dx = jnp.where(valid, dlogf * _sigmoid(-x), 0.0)
            du_ref[sl, :] = dx.astype(du_ref.dtype)
            return later + jnp.sum(df, axis=0, keepdims=True), db_acc + jnp.sum(dx, axis=0, keepdims=True)

        _, db_tot = lax.fori_loop(0, nb, step, (jnp.zeros((1, 128), F32), jnp.zeros((1, 128), F32)))
        db_ref[...] = db_tot

    return pl.pallas_call(
        body, name="fgate_bwd",
        out_shape=[jax.ShapeDtypeStruct((lp, 128), MXU), jax.ShapeDtypeStruct((1, 128), F32)],
    )(ufl, b_f, d_f)


STRIP = 32
PAIR = 2


def _strip_causal(r, t):
    row = r + lax.broadcasted_iota(jnp.int32, (STRIP, t), 0)
    col = lax.broadcasted_iota(jnp.int32, (STRIP, t), 1)
    return col <= row


def _fox_fwd(qa, ka, va, ub, name="fox_fwd", t=384):
    lp = qa.shape[0]
    d = ub.shape[2]
    t = _tile(lp, t)
    nh, nq = d // HEAD, lp // t
    scale = HEAD ** -0.5

    assert nh % PAIR == 0
    heads = range(PAIR)

    def body(q_ref, k_ref, v_ref, z_ref, o_ref, olo_ref, og_ref, qb_ref,
             s_ref, p_ref, m_ref, a_ref, l_ref, acc_ref):
        qb = pl.program_id(1)
        m_ref[...] = jnp.full((PAIR, t, 1), MASK_VALUE, F32)
        l_ref[...] = jnp.zeros((PAIR, t, 128), F32)
        acc_ref[...] = jnp.zeros((PAIR, 2 * t, HEAD), F32)

        def step(kb, diagonal):
            ks = pl.ds(pl.multiple_of(kb * t, t), t)
            for j in heads:
                ws = slice(j * WIDE, (j + 1) * WIDE)
                s_ref[j] = _dot(q_ref[:, ws], k_ref[ks, ws], NT)
            for j in heads:
                for r in range(0, t, STRIP):
                    rs = slice(r, r + STRIP)
                    x = s_ref[j, rs, :] * scale
                    if diagonal:
                        x = jnp.where(_strip_causal(r, t), x, MASK_VALUE)
                    m_old = m_ref[j, rs, :]
                    m_new = jnp.maximum(m_old, jnp.max(x, axis=-1, keepdims=True))
                    alpha = jnp.exp(m_old - m_new)
                    p = jnp.exp(x - m_new)
                    m_ref[j, rs, :] = m_new
                    a_ref[j, rs, :] = alpha
                    l_ref[j, rs, :] = alpha * l_ref[j, rs, :] + sum(p[:, c:c + 128] for c in range(0, t, 128))
                    p_hi = p.astype(MXU)
                    p_ref[j, rs, :] = p_hi
                    p_ref[j, t + r:t + r + STRIP, :] = (p - p_hi.astype(F32)).astype(MXU)
            for j in heads:
                pv = _dot(p_ref[j], v_ref[ks, j * WIDE:j * WIDE + HEAD])
                alpha = a_ref[j]
                acc_ref[j, 0:t, :] = alpha * acc_ref[j, 0:t, :] + pv[0:t]
                acc_ref[j, t:2 * t, :] = alpha * acc_ref[j, t:2 * t, :] + pv[t:2 * t]

        def off_diagonal(kb, carry):
            step(kb, False)
            return carry

        lax.fori_loop(0, qb, off_diagonal, 0)
        step(qb, True)
        is_pad = (qb * t + lax.broadcasted_iota(jnp.int32, (t, 1), 0)) < ROW_PAD
        for j in heads:
            hs = slice(j * HEAD, (j + 1) * HEAD)
            l = jnp.sum(l_ref[j], axis=-1, keepdims=True)
            o = acc_ref[j, 0:t, :] / l
            o_ref[:, hs] = o
            olo_ref[:, hs] = acc_ref[j, t:2 * t, :] / l
            z = z_ref[:, hs]
            og_ref[:, hs] = (o * (z * _sigmoid(z))).astype(og_ref.dtype)
            extra = q_ref[:, j * WIDE + HEAD:(j + 1) * WIDE].astype(F32)
            f_scaled = extra[:, 0:1] + extra[:, 1:2] + extra[:, 2:3]
            log_term = jnp.where(is_pad, MASK_VALUE * INV_SCALE, f_scaled - (m_ref[j] + jnp.log(l)) * INV_SCALE)
            qb_ref[:, j * WIDE:j * WIDE + HEAD] = q_ref[:, j * WIDE:j * WIDE + HEAD]
            qb_ref[:, j * WIDE + HEAD:(j + 1) * WIDE] = _extra_cols(t, _split3(log_term), None).astype(qb_ref.dtype)

    scratch = [pltpu.VMEM((PAIR, t, t), F32), pltpu.VMEM((PAIR, 2 * t, t), MXU), pltpu.VMEM((PAIR, t, 1), F32),
               pltpu.VMEM((PAIR, t, 1), F32), pltpu.VMEM((PAIR, t, 128), F32), pltpu.VMEM((PAIR, 2 * t, HEAD), F32)]
    tile = pl.BlockSpec((t, PAIR * HEAD), lambda g, i: (i, g))
    wide_tile = pl.BlockSpec((t, PAIR * WIDE), lambda g, i: (i, g))
    wide_all = pl.BlockSpec((lp, PAIR * WIDE), lambda g, i: (0, g))
    return pl.pallas_call(
        body,
        name=name,
        grid=(nh // PAIR, nq),
        in_specs=[wide_tile, wide_all, wide_all, pl.BlockSpec((None, t, PAIR * HEAD), lambda g, i: (1, i, g))],
        out_specs=[tile, tile, tile, wide_tile],
        out_shape=[jax.ShapeDtypeStruct((lp, d), F32), jax.ShapeDtypeStruct((lp, d), F32),
                   jax.ShapeDtypeStruct((lp, d), MXU), jax.ShapeDtypeStruct((lp, nh * WIDE), MXU)],
        scratch_shapes=scratch,
        compiler_params=_cparams(("parallel", "arbitrary"), 48),
    )(qa, ka, va, ub)


def _fox_bwd(qb, ka, va, doa, d, name="fox_bwd", t=384):
    lp = qb.shape[0]
    t = _tile(lp, t)
    nh, nk = d // HEAD, lp // t
    scale = HEAD ** -0.5

    assert nh % PAIR == 0
    heads = range(PAIR)

    def body(q_ref, do_ref, k_ref, v_ref, dq_ref, dk_ref, dv_ref, dfk_ref,
             s_ref, dp_ref, p_ref, ds_ref, col_ref):
        kb = pl.program_id(1)

        @pl.when(kb == 0)
        def _():
            dq_ref[...] = jnp.zeros_like(dq_ref)

        dk_ref[...] = jnp.zeros_like(dk_ref)
        dv_ref[...] = jnp.zeros_like(dv_ref)
        col_ref[...] = jnp.zeros_like(col_ref)

        def step(qb, diagonal):
            qs = pl.ds(pl.multiple_of(qb * t, t), t)
            for j in heads:
                ws = slice(j * WIDE, (j + 1) * WIDE)
                s_ref[j] = _dot(q_ref[qs, ws], k_ref[:, ws], NT)
                dp_ref[j] = _dot(do_ref[qs, ws], v_ref[:, ws], NT)
            for j in heads:
                for r in range(0, t, STRIP):
                    rs = slice(r, r + STRIP)
                    x = s_ref[j, rs, :] * scale
                    if diagonal:
                        x = jnp.where(_strip_causal(r, t), x, MASK_VALUE)
                    p = jnp.exp(x)
                    ds = p * dp_ref[j, rs, :]
                    p_ref[j, rs, :] = p.astype(MXU)
                    ds_ref[j, rs, :] = (ds * scale).astype(MXU)
                    col_ref[j] += ds
            for j in heads:
                hs = slice(j * HEAD, (j + 1) * HEAD)
                narrow = slice(j * WIDE, j * WIDE + HEAD)
                dsb = ds_ref[j]
                dv_ref[:, hs] += _dot(p_ref[j], do_ref[qs, narrow], TN)
                dq_ref[qs, hs] += _dot(dsb, k_ref[:, narrow])
                dk_ref[:, hs] += _dot(dsb, q_ref[qs, narrow], TN)

        def off_diagonal(qb, carry):
            step(qb, False)
            return carry

        step(kb, True)
        lax.fori_loop(kb + 1, nk, off_diagonal, 0)
        for j in heads:
            dfk_ref[j] = -jnp.sum(col_ref[j], axis=0, keepdims=True)

    scratch = [pltpu.VMEM((PAIR, t, t), F32), pltpu.VMEM((PAIR, t, t), F32), pltpu.VMEM((PAIR, t, t), MXU),
               pltpu.VMEM((PAIR, t, t), MXU), pltpu.VMEM((PAIR, STRIP, t), F32)]
    whole = pl.BlockSpec((lp, PAIR * HEAD), lambda g, j: (0, g))
    tile = pl.BlockSpec((t, PAIR * HEAD), lambda g, j: (j, g))
    wide_all = pl.BlockSpec((lp, PAIR * WIDE), lambda g, j: (0, g))
    wide_tile = pl.BlockSpec((t, PAIR * WIDE), lambda g, j: (j, g))
    return pl.pallas_call(
        body,
        name=name,
        grid=(nh // PAIR, nk),
        in_specs=[wide_all, wide_all, wide_tile, wide_tile],
        out_specs=[whole, tile, tile, pl.BlockSpec((PAIR, 1, t), lambda g, j: (g, 0, j))],
        out_shape=[jax.ShapeDtypeStruct((lp, d), F32)] * 3 + [jax.ShapeDtypeStruct((nh, 1, lp), F32)],
        scratch_shapes=scratch,
        compiler_params=_cparams(("parallel", "arbitrary"), 48),
    )(qb, doa, ka, va)


def _fox_do(d_og, o, o_lo, ub, name="fox_do", tm=384):
    lp, d = o.shape
    tm = _tile(lp, tm)
    nh = d // HEAD

    def body(dog_ref, o_ref, olo_ref, z_ref, doa_ref):
        for h in range(nh):
            hs = slice(h * HEAD, (h + 1) * HEAD)
            z = z_ref[:, hs]
            do = (dog_ref[:, hs] * (z * _sigmoid(z))).astype(doa_ref.dtype)
            delta = jnp.sum(do.astype(F32) * (o_ref[:, hs] + olo_ref[:, hs]), axis=-1, keepdims=True)
            doa_ref[:, h * WIDE:h * WIDE + HEAD] = do
            doa_ref[:, h * WIDE + HEAD:(h + 1) * WIDE] = _extra_cols(tm, _split3(-delta), (0.0, 0.0, 0.0)).astype(
                doa_ref.dtype)

    row = pl.BlockSpec((tm, d), lambda i: (i, 0))
    return pl.pallas_call(
        body,
        name=name,
        grid=(lp // tm,),
        in_specs=[row, row, row, pl.BlockSpec((None, tm, d), lambda i: (1, i, 0))],
        out_specs=pl.BlockSpec((tm, nh * WIDE), lambda i: (i, 0)),
        out_shape=jax.ShapeDtypeStruct((lp, nh * WIDE), MXU),
        compiler_params=_cparams(("parallel",), 56),
    )(d_og, o, o_lo, ub)


def _fox_q_bwd(ub, d_og, o, dqn, g_q, name="fox_q_bwd", tm=384):
    _, lp, d = ub.shape
    tm = _tile(lp, tm)
    nh = d // HEAD

    def body(ub_ref, dog_ref, o_ref, dqn_ref, g_ref, dub_ref, dg_ref):
        i = pl.program_id(0)
        for h in range(nh):
            hs = slice(h * HEAD, (h + 1) * HEAD)
            z = ub_ref[1, :, hs]
            sz = _sigmoid(z)
            dub_ref[1, :, hs] = (dog_ref[:, hs] * o_ref[:, hs] * (sz * (1.0 + z * (1.0 - sz)))).astype(dub_ref.dtype)
            dx, dg = _head_rms_bwd_tile(ub_ref[0, :, hs], g_ref[:, hs], dqn_ref[:, hs])
            dub_ref[0, :, hs] = dx.astype(dub_ref.dtype)

            @pl.when(i == 0)
            def _():
                dg_ref[:, hs] = dg

            @pl.when(i > 0)
            def _():
                dg_ref[:, hs] += dg

    row = pl.BlockSpec((tm, d), lambda i: (i, 0))
    pair = pl.BlockSpec((2, tm, d), lambda i: (0, i, 0))
    vec = pl.BlockSpec((1, d), lambda i: (0, 0))
    return pl.pallas_call(
        body,
        name=name,
        grid=(lp // tm,),
        in_specs=[pair, row, row, row, vec],
        out_specs=[pair, vec],
        out_shape=[jax.ShapeDtypeStruct((2, lp, d), MXU), jax.ShapeDtypeStruct((1, d), F32)],
        compiler_params=_cparams(("arbitrary",), 56),
    )(ub, d_og, o, dqn, g_q)


def _fox_k_bwd(ukv, dkn, dv, g_k, name="fox_k_bwd", tm=384):
    _, lp, d = ukv.shape
    tm = _tile(lp, tm)
    nh = d // HEAD

    def body(ukv_ref, dkn_ref, dv_ref, g_ref, du_ref, dg_ref):
        i = pl.program_id(0)
        du_ref[1] = dv_ref[...].astype(du_ref.dtype)
        for h in range(nh):
            hs = slice(h * HEAD, (h + 1) * HEAD)
            dx, dg = _head_rms_bwd_tile(ukv_ref[:, hs], g_ref[:, hs], dkn_ref[:, hs])
            du_ref[0, :, hs] = dx.astype(du_ref.dtype)

            @pl.when(i == 0)
            def _():
                dg_ref[:, hs] = dg

            @pl.when(i > 0)
            def _():
                dg_ref[:, hs] += dg

    row = pl.BlockSpec((tm, d), lambda i: (i, 0))
    vec = pl.BlockSpec((1, d), lambda i: (0, 0))
    return pl.pallas_call(
        body,
        name=name,
        grid=(lp // tm,),
        in_specs=[pl.BlockSpec((None, tm, d), lambda i: (0, i, 0)), row, row, vec],
        out_specs=[pl.BlockSpec((2, tm, d), lambda i: (0, i, 0)), vec],
        out_shape=[jax.ShapeDtypeStruct((2, lp, d), MXU), jax.ShapeDtypeStruct((1, d), F32)],
        compiler_params=_cparams(("arbitrary",), 48),
    )(ukv, dkn, dv, g_k)


def _loss(h, target, name="loss_head"):
    lp, d = h.shape
    nb = lp // 128

    def body(h_ref, t_ref, loss_ref, dh_ref, acc_ref):
        i = pl.program_id(0)

        @pl.when(i == 0)
        def _():
            acc_ref[...] = jnp.zeros_like(acc_ref)
            dh_ref[...] = jnp.zeros_like(dh_ref)

        @pl.when(i > 0)
        def _():
            err = h_ref[...] - t_ref[...]
            dh_ref[...] = err * (1.0 / d)
            acc_ref[...] += jnp.sum(jnp.sum(err * err, axis=-1, keepdims=True) * (1.0 / d), axis=0, keepdims=True)

        @pl.when(i == nb - 1)
        def _():
            loss_ref[...] = 0.5 * acc_ref[...]

    return pl.pallas_call(
        body,
        name=name,
        grid=(nb,),
        in_specs=[pl.BlockSpec((128, d), lambda i: (i, 0)),
                  pl.BlockSpec((128, d), lambda i: (jnp.maximum(i - 1, 0), 0))],
        out_specs=[pl.BlockSpec((1, 1), lambda i: (0, 0)), pl.BlockSpec((128, d), lambda i: (i, 0))],
        out_shape=[jax.ShapeDtypeStruct((1, 1), F32), jax.ShapeDtypeStruct((lp, d), F32)],
        scratch_shapes=[pltpu.VMEM((1, 1), F32)],
        compiler_params=_cparams(("arbitrary",), 32),
    )(h, target)


def _adam_math(w, g, m, v):
    m = ADAM_B1 * m + (1.0 - ADAM_B1) * g
    v = ADAM_B2 * v + (1.0 - ADAM_B2) * (g * g)
    m_hat = m / (1.0 - ADAM_B1 ** ADAM_STEP)
    v_hat = v / (1.0 - ADAM_B2 ** ADAM_STEP)
    delta = -ADAM_LR * (m_hat / (jnp.sqrt(v_hat) + ADAM_EPS) + ADAM_WD * w)
    return delta, m, v


def _adamw(parts, w, m, v, name, tm=256):
    n, r, c = parts.shape
    tm = tm if r % tm == 0 else r

    def body(p_ref, w_ref, m_ref, v_ref, g_ref, d_ref, nm_ref, nv_ref):
        g = p_ref[0].astype(F32)
        for j in range(1, n):
            g = g + p_ref[j].astype(F32)
        g_ref[...] = g
        d_ref[...], nm_ref[...], nv_ref[...] = _adam_math(w_ref[...], g, m_ref[...], v_ref[...])

    row = pl.BlockSpec((tm, c), lambda i: (i, 0))
    return pl.pallas_call(
        body,
        name=name,
        grid=(r // tm,),
        in_specs=[pl.BlockSpec((n, tm, c), lambda i: (0, i, 0)), row, row, row],
        out_specs=[row] * 4,
        out_shape=[jax.ShapeDtypeStruct((r, c), F32)] * 4,
        compiler_params=_cparams(("parallel",), 48),
    )(parts, w, m, v)


def _place():
    x, y, c = lax.axis_index("x"), lax.axis_index("y"), lax.axis_index("c")
    return x, y, c


def _other_chips(x, y):
    return [(1 - x, y), (x, 1 - y), (1 - x, 1 - y)]


def _any_spec():
    return pl.BlockSpec(memory_space=pl.ANY)


def _all_gather(blocks, name):
    n = len(blocks)

    def body(*refs):
        in_refs, out_refs = refs[:n], refs[n:2 * n]
        send_sems, recv_sems, local_sems = refs[2 * n:]
        x, y, c = _place()
        me, sibling = (x, y, c), (x, y, 1 - c)
        chips = _other_chips(x, y)

        def slot(p):
            return 4 * p[0] + 2 * p[1] + p[2]

        def copy(t, k, block, to, src=None):
            dst = out_refs[t].at[slot(block)]
            return pltpu.make_async_remote_copy(
                src_ref=dst if src is None else src, dst_ref=dst,
                send_sem=send_sems.at[7 * t + k], recv_sem=recv_sems.at[7 * t + k],
                device_id=to, device_id_type=MESH)

        started = []
        for t in range(n):
            mine = pltpu.make_async_copy(in_refs[t], out_refs[t].at[slot(me)], local_sems.at[t])
            mine.start()
            started.append(mine)
        sends = []
        for t in range(n):
            first = [copy(t, 0, me, sibling, src=in_refs[t])]
            first += [copy(t, 1 + j, me, (*chip, c), src=in_refs[t]) for j, chip in enumerate(chips)]
            for cp in first:
                cp.start()
            sends += first
        for t in range(n):
            for j, chip in enumerate(chips):
                copy(t, 1 + j, (*chip, c), me).wait_recv()
                passed = copy(t, 4 + j, (*chip, c), sibling)
                passed.start()
                sends.append(passed)
        for t in range(n):
            copy(t, 0, sibling, me).wait_recv()
            for j, chip in enumerate(chips):
                copy(t, 4 + j, (*chip, 1 - c), me).wait_recv()
        for cp in sends:
            cp.wait_send()
        for mine in started:
            mine.wait()

    return pl.pallas_call(
        body,
        name=name,
        in_specs=[_any_spec()] * n,
        out_specs=[_any_spec()] * n,
        out_shape=[jax.ShapeDtypeStruct((N_DEV,) + b.shape, b.dtype) for b in blocks],
        scratch_shapes=[pltpu.SemaphoreType.DMA((7 * n,)), pltpu.SemaphoreType.DMA((7 * n,)),
                        pltpu.SemaphoreType.DMA((n,))],
    )(*blocks)


def _to_sibling(grads, name):
    n = len(grads)

    def body(*refs):
        in_refs, out_refs = refs[:n], refs[n:2 * n]
        send_sems, recv_sems = refs[2 * n:]
        x, y, c = _place()
        sends = []
        for t in range(n):
            for chip in range(N_CHIP):
                cp = pltpu.make_async_remote_copy(
                    src_ref=in_refs[t].at[2 * chip + (1 - c)], dst_ref=out_refs[t].at[chip],
                    send_sem=send_sems.at[4 * t + chip], recv_sem=recv_sems.at[4 * t + chip],
                    device_id=(x, y, 1 - c), device_id_type=MESH)
                cp.start()
                sends.append(cp)
        for cp in sends:
            cp.wait()

    return pl.pallas_call(
        body,
        name=name,
        in_specs=[_any_spec()] * n,
        out_specs=[_any_spec()] * n,
        out_shape=[jax.ShapeDtypeStruct((N_CHIP,) + g.shape[1:], g.dtype) for g in grads],
        scratch_shapes=[pltpu.SemaphoreType.DMA((4 * n,)), pltpu.SemaphoreType.DMA((4 * n,))],
    )(*grads)


def _pair_sum(g8, got, name, tm=256):
    _, r, c = g8.shape
    tm = tm if r % tm == 0 else r
    core = lax.axis_index("c")

    def body(core_ref, mine_ref, got_ref, o_ref):
        south_first = core_ref[0] == 0
        a, b = mine_ref[...], got_ref[...]
        o_ref[...] = (jnp.where(south_first, a, b) + jnp.where(south_first, b, a)).astype(o_ref.dtype)

    return pl.pallas_call(
        body,
        name=name,
        grid_spec=pltpu.PrefetchScalarGridSpec(
            num_scalar_prefetch=1,
            grid=(N_CHIP, r // tm),
            in_specs=[pl.BlockSpec((None, tm, c), lambda j, i, core_ref: (2 * j + core_ref[0], i, 0)),
                      pl.BlockSpec((None, tm, c), lambda j, i, core_ref: (j, i, 0))],
            out_specs=pl.BlockSpec((None, tm, c), lambda j, i, core_ref: (j, i, 0)),
        ),
        out_shape=jax.ShapeDtypeStruct((N_CHIP, r, c), PAYLOAD),
        compiler_params=_cparams(("parallel", "parallel"), 32),
    )(jnp.reshape(core, (1,)).astype(jnp.int32), g8, got)


def _to_chips(partials, name):
    n = len(partials)

    def body(*refs):
        in_refs, out_refs = refs[:n], refs[n:2 * n]
        send_sems, recv_sems, local_sems = refs[2 * n:]
        x, y, c = _place()
        my_chip = 2 * x + y
        chips = _other_chips(x, y)
        pending = []
        for t in range(n):
            mine = pltpu.make_async_copy(in_refs[t].at[my_chip], out_refs[t].at[my_chip], local_sems.at[t])
            mine.start()
            pending.append(mine)
        sends = []
        for t in range(n):
            for j, chip in enumerate(chips):
                cp = pltpu.make_async_remote_copy(
                    src_ref=in_refs[t].at[2 * chip[0] + chip[1]], dst_ref=out_refs[t].at[my_chip],
                    send_sem=send_sems.at[3 * t + j], recv_sem=recv_sems.at[3 * t + j],
                    device_id=(*chip, c), device_id_type=MESH)
                cp.start()
                sends.append((t, j, chip, cp))
        for t, j, chip, cp in sends:
            cp.wait_send()
            pltpu.make_async_remote_copy(
                src_ref=in_refs[t].at[my_chip], dst_ref=out_refs[t].at[2 * chip[0] + chip[1]],
                send_sem=send_sems.at[3 * t + j], recv_sem=recv_sems.at[3 * t + j],
                device_id=(*chip, c), device_id_type=MESH).wait_recv()
        for mine in pending:
            mine.wait()

    return pl.pallas_call(
        body,
        name=name,
        in_specs=[_any_spec()] * n,
        out_specs=[_any_spec()] * n,
        out_shape=[jax.ShapeDtypeStruct(p.shape, p.dtype) for p in partials],
        scratch_shapes=[pltpu.SemaphoreType.DMA((3 * n,)), pltpu.SemaphoreType.DMA((3 * n,)),
                        pltpu.SemaphoreType.DMA((n,))],
    )(*partials)


def _all_reduce_small(s, name="small_all_reduce"):
    r, c = s.shape

    def body(s_ref, o_ref, buf_ref, send_sems, recv_sems):
        x, y, c_ = _place()
        me = 4 * x + 2 * y + c_
        buf_ref[me] = s_ref[...]

        def copy(k, slot, peer):
            return pltpu.make_async_remote_copy(
                src_ref=s_ref, dst_ref=buf_ref.at[slot],
                send_sem=send_sems.at[k - 1], recv_sem=recv_sems.at[k - 1],
                device_id=peer, device_id_type=MESH)

        peers = []
        for k in range(1, N_DEV):
            peer = (x ^ ((k >> 2) & 1), y ^ ((k >> 1) & 1), c_ ^ (k & 1))
            peers.append(peer)
            copy(k, me, peer).start()
        for k, peer in zip(range(1, N_DEV), peers):
            cp = copy(k, 4 * peer[0] + 2 * peer[1] + peer[2], peer)
            cp.wait_send()
            cp.wait_recv()
        total = buf_ref[0]
        for j in range(1, N_DEV):
            total = total + buf_ref[j]
        o_ref[...] = total

    return pl.pallas_call(
        body,
        name=name,
        out_shape=jax.ShapeDtypeStruct((r, c), F32),
        in_specs=[pl.BlockSpec(memory_space=pltpu.VMEM)],
        out_specs=pl.BlockSpec(memory_space=pltpu.VMEM),
        scratch_shapes=[pltpu.VMEM((N_DEV, r, c), F32), pltpu.SemaphoreType.DMA((N_DEV - 1,)),
                        pltpu.SemaphoreType.DMA((N_DEV - 1,))],
    )(s)


def _local_step(x, target, meta, gamma, a_norm, wa_in, a_out_norm, wa_out, kv_norm, wkv2, wfl, b_f, g_k,
                b_norm, wb_in, g_q, wb_out):
    d = x.shape[1]
    nh = d // HEAD
    h0 = jnp.concatenate([jnp.zeros((ROW_PAD, d), F32), meta, x], axis=0)
    big = dict(tm=1408, tn=512, tk=2048)

    lb = _lb_fwd(gamma)
    (hn_a,) = _rms_fwd(h0, [a_norm], "rms_a")
    u4 = _matmul(hn_a, wa_in, "nn", F32, "a_in", out_parts=4, **big)
    o_a, og_a = _hgrn2_fwd(u4, lb, a_out_norm)
    h1 = _matmul(og_a, wa_out, "nn", F32, "a_out", add=h0, **big)
    hk, hb = _rms_fwd(h1, [kv_norm, b_norm], "rms_kv_b")
    ukv = _matmul(hk, wkv2, "nn", F32, "kv_in", out_parts=2, **big)
    ufl = _matmul(hk, wfl, "nn", F32, "kv_f", **big)
    ub = _matmul(hb, wb_in, "nn", F32, "b_in", out_parts=2, **big)
    f_cum = _fgate_fwd(ufl, b_f)
    qa, ka, va = _attn_operands(ub, ukv, g_q, g_k, f_cum)
    o_b, o_lo, og_b, qb = _fox_fwd(qa, ka, va, ub)
    h2 = _matmul(og_b, wb_out, "nn", F32, "b_out", add=h1, **big)
    loss, dh2 = _loss(h2, target)

    dx_t = dict(tm=1408, tn=512, tk=1024)
    dw_t = dict(tm=1024, tn=1024, tk=1408, vmem_mib=56)
    d_ogb = _matmul(dh2, wb_out, "nt", F32, "b_out_dx", **dx_t)
    dwb_out = _matmul(og_b, dh2, "tn", F32, "b_out_dw", **dw_t)
    doa = _fox_do(d_ogb, o_b, o_lo, ub)
    dqn, dkn, dv, dfk = _fox_bwd(qb, ka, va, doa, d)
    dub, dg_q = _fox_q_bwd(ub, d_ogb, o_b, dqn, g_q)
    dukv, dg_k = _fox_k_bwd(ukv, dkn, dv, g_k)
    d_f = jnp.pad(jnp.transpose(dfk[:, 0, :]), ((0, 0), (0, 128 - nh)))
    dufl, db_f = _fgate_bwd(ufl, b_f, d_f)
    d_hb = _matmul(dub, wb_in, "nt", F32, "b_in_dx", **dx_t)
    dwb_in = _matmul(hb, dub, "tn", F32, "b_in_dw", out_parts=N_DEV, **dw_t)
    d_hk = _matmul(dukv, wkv2, "nt", F32, "kv_dx", **dx_t)
    d_hk = _matmul(dufl, wfl, "nt", F32, "kv_f_dx", add=d_hk, **dx_t)
    dwkv2 = _matmul(hk, dukv, "tn", F32, "kv_dw", **dw_t)
    dwfl = _matmul(hk, dufl, "tn", F32, "kv_f_dw", **dw_t)
    dh1, (dg_kv, dg_b) = _rms_bwd(h1, [kv_norm, b_norm], [d_hk, d_hb], dh2, "rms_kv_b_bwd")
    d_oga = _matmul(dh1, wa_out, "nt", F32, "a_out_dx", **dx_t)
    dwa_out = _matmul(og_a, dh1, "tn", F32, "a_out_dw", **dw_t)
    du4, dlb, dg_aout = _hgrn2_bwd(u4, o_a, d_oga, lb, a_out_norm)
    d_hna = _matmul(du4, wa_in, "nt", F32, "a_in_dx", **dx_t)
    dwa_in = _matmul(hn_a, du4, "tn", F32, "a_in_dw", out_parts=N_DEV, **dw_t)
    dh0, (dg_a,) = _rms_bwd(h0, [a_norm], [d_hna], dh1, "rms_a_bwd")
    dgamma = _lb_bwd(gamma, dlb)

    grads = dict(meta=dh0[ROW_PAD:ROW_PAD + N_META], gamma=dgamma, a_norm=dg_a, wa_in=dwa_in, a_out_norm=dg_aout,
                 wa_out=dwa_out, kv_norm=dg_kv, wkv2=dwkv2, wfl=dwfl, b_f=db_f, g_k=dg_k, b_norm=dg_b,
                 wb_in=dwb_in, g_q=dg_q, wb_out=dwb_out)
    return loss, dh0[ROW_PAD + N_META:], grads


def _reduce_scatter_adamw(g8s, ws, ms, vs, names):
    got = _to_sibling(g8s, "grads_to_sibling")
    partials = [_pair_sum(g8, g, "pair_sum_" + nm) for g8, g, nm in zip(g8s, got, names)]
    parts = _to_chips(partials, "grads_to_chips")
    return [_adamw(p, w, m, v, "adamw_" + nm) for p, w, m, v, nm in zip(parts, ws, ms, vs, names)]


def kernel(x, meta, gamma_lb, a_norm, a_w_in, a_out_norm, a_w_out, kv_norm, kv_w, fox_b_f, fox_k_norm, b_norm, b_w_in, b_q_norm, b_w_out, loss_target, m_meta, m_gamma_lb, m_a_norm, m_a_w_in, m_a_out_norm, m_a_w_out, m_kv_norm, m_kv_w, m_fox_b_f, m_fox_k_norm, m_b_norm, m_b_w_in, m_b_q_norm, m_b_w_out, v_meta, v_gamma_lb, v_a_norm, v_a_w_in, v_a_out_norm, v_a_w_out, v_kv_norm, v_kv_w, v_fox_b_f, v_fox_k_norm, v_b_norm, v_b_w_in, v_b_q_norm, v_b_w_out):
    d = x.shape[-1]
    nh = d // HEAD
    cols = d // N_DEV
    me = 4 * lax.axis_index("x") + 2 * lax.axis_index("y") + lax.axis_index("c")

    sharded_small = jnp.concatenate([meta, gamma_lb, a_norm, a_out_norm, jnp.zeros((4, cols), F32)], axis=0)
    g_a_in, g_a_out, g_kv, g_b_in, g_b_out, g_small = _all_gather(
        [a_w_in[0].astype(MXU), a_w_out[0].astype(MXU), kv_w.astype(MXU), b_w_in[0].astype(MXU),
         b_w_out[0].astype(MXU), sharded_small], "gather_weights")
    wa_out = jnp.reshape(g_a_out, (d, d))
    wb_out = jnp.reshape(g_b_out, (d, d))
    wkv = jnp.reshape(jnp.transpose(g_kv, (1, 0, 2)), (d, -1))
    wkv2 = wkv[:, :2 * d]
    wfl = jnp.pad(wkv[:, 2 * d:], ((0, 0), (0, 128 - nh)))
    small = jnp.reshape(jnp.transpose(g_small, (1, 0, 2)), (-1, d))
    meta_f, gamma_f, a_norm_f, a_out_norm_f = small[:16], small[16:18], small[18:19], small[19:20]
    b_f = jnp.pad(jnp.reshape(fox_b_f, (1, nh)), ((0, 0), (0, 128 - nh)))
    g_k = jnp.reshape(fox_k_norm, (1, d))
    g_q = jnp.reshape(b_q_norm, (1, d))
    kv_norm_r = jnp.reshape(kv_norm, (1, d))

    loss, grad_x, g = _local_step(x[0], loss_target[0], meta_f, gamma_f, a_norm_f, g_a_in, a_out_norm_f, wa_out,
                                  kv_norm_r, wkv2, wfl, b_f, g_k, b_norm, g_b_in, g_q, wb_out)
    loss = lax.psum(loss[0, 0], AXES)

    dwkv = jnp.concatenate([g["wkv2"], g["wfl"][:, :nh]], axis=1)
    dwkv8 = jnp.transpose(jnp.reshape(dwkv, (d, N_DEV, -1)), (1, 0, 2))
    big = _reduce_scatter_adamw(
        [g["wa_in"], jnp.reshape(g["wa_out"], (N_DEV, cols, d)), dwkv8, g["wb_in"], jnp.reshape(g["wb_out"], (N_DEV, cols, d))],
        [a_w_in[0], a_w_out[0], kv_w, b_w_in[0], b_w_out[0]],
        [m_a_w_in[0], m_a_w_out[0], m_kv_w, m_b_w_in[0], m_b_w_out[0]],
        [v_a_w_in[0], v_a_w_out[0], v_kv_w, v_b_w_in[0], v_b_w_out[0]],
        ["a_w_in", "a_w_out", "kv_w", "b_w_in", "b_w_out"])
    r_a_in, r_a_out, r_kv, r_b_in, r_b_out = big

    packed = jnp.concatenate(
        [g["meta"], g["gamma"], g["a_norm"], g["a_out_norm"], g["kv_norm"], g["b_norm"], g["g_k"], g["g_q"],
         jnp.pad(g["b_f"], ((0, 0), (0, d - 128))), jnp.zeros((7, d), F32)], axis=0)
    tot = _all_reduce_small(packed)
    mine = lax.dynamic_slice_in_dim(tot[:20], me * cols, cols, axis=1)
    gs = [mine[:16], mine[16:18], mine[18:19], mine[19:20], tot[20], tot[21:22], jnp.reshape(tot[22], (nh, HEAD)),
          jnp.reshape(tot[23], (1, nh, HEAD)), tot[24, :nh]]
    small_w = [meta, gamma_lb, a_norm, a_out_norm, kv_norm, b_norm, fox_k_norm, b_q_norm, fox_b_f]
    small_m = [m_meta, m_gamma_lb, m_a_norm, m_a_out_norm, m_kv_norm, m_b_norm, m_fox_k_norm, m_b_q_norm, m_fox_b_f]
    small_v = [v_meta, v_gamma_lb, v_a_norm, v_a_out_norm, v_kv_norm, v_b_norm, v_fox_k_norm, v_b_q_norm, v_fox_b_f]

    def pack(arrs):
        rows = [jnp.reshape(a, (-1, cols)) for a in arrs[:8]]
        rows.append(jnp.pad(jnp.reshape(arrs[8], (1, nh)), ((0, 0), (0, cols - nh))))
        n_rows = sum(r.shape[0] for r in rows)
        return jnp.concatenate(rows + [jnp.zeros((-n_rows % 8, cols), F32)], axis=0)

    _, sd, sm, sv = _adamw(pack(gs)[None], pack(small_w), pack(small_m), pack(small_v), "adamw_small")

    def unpack(p):
        out, at = [], 0
        for a in small_w[:8]:
            n = a.size // cols
            out.append(jnp.reshape(p[at:at + n], a.shape))
            at += n
        out.append(p[at, :nh])
        return out

    d_s, m_s, v_s = unpack(sd), unpack(sm), unpack(sv)

    def ordered(s, a_in, a_out, kv, b_in, b_out):
        return [s[0], s[1], s[2], a_in[None], s[3], a_out[None], s[4], kv, s[8], s[6], s[5], b_in[None], s[7], b_out[None]]

    outs = []
    for i, s in enumerate([gs, d_s, m_s, v_s]):
        outs += ordered(s, r_a_in[i], r_a_out[i], r_kv[i], r_b_in[i], r_b_out[i])
    return (loss, grad_x[None], *outs)
```

```python
import math

import jax
import jax.numpy as jnp
from jax import lax
from jax.experimental import pallas as pl
from jax.experimental.pallas import tpu as pltpu

HEAD = 128
CHUNK = 64
N_META = 16
ROW_PAD = 128 - N_META
EPS = 1e-6
MASK_VALUE = -1e30
ADAM_LR = 0.001
ADAM_B1 = 0.9
ADAM_B2 = 0.999
ADAM_EPS = 1e-08
ADAM_WD = 0.01
ADAM_STEP = 10
N_DEV = 8
N_CHIP = 4
MIB = 1024 * 1024
AXES = ("x", "y", "c")
MESH = pl.DeviceIdType.MESH

F32 = jnp.float32
MXU = jnp.bfloat16
PAYLOAD = jnp.bfloat16
HI = lax.Precision.HIGHEST

NN = (((1,), (0,)), ((), ()))
NT = (((1,), (1,)), ((), ()))
TN = (((0,), (0,)), ((), ()))


def _dot(a, b, dims=NN):
    return lax.dot_general(a.astype(MXU), b.astype(MXU), dims, preferred_element_type=F32)


def _dot_exact(a, b):
    return lax.dot_general(a, b, NN, precision=HI, preferred_element_type=F32)


def _sigmoid(x):
    return 1.0 / (1.0 + jnp.exp(-x))


def _tile(dim, target):
    best = None
    t = 128
    while t <= min(dim, target):
        if dim % t == 0:
            best = t
        t += 128
    return best if best is not None else dim


def _cparams(semantics, vmem_mib):
    return pltpu.CompilerParams(dimension_semantics=semantics, vmem_limit_bytes=vmem_mib * MIB)


def _mat_spec(arr, br, bc, rc_of_grid):
    if arr.ndim == 2:
        return pl.BlockSpec((br, bc), rc_of_grid)
    assert arr.shape[2] % bc == 0, (arr.shape, bc)
    per = arr.shape[2] // bc

    def idx(*g):
        r, c = rc_of_grid(*g)
        return (c // per, r, c % per)

    return pl.BlockSpec((None, br, bc), idx)


def _mat_shape(arr):
    return (arr.shape[0], arr.shape[1]) if arr.ndim == 2 else (arr.shape[1], arr.shape[0] * arr.shape[2])


def _matmul(a, b, dims, out_dtype, name, *, add=None, out_parts=1, tm=512, tn=512, tk=512, vmem_mib=48, sides=()):
    ar, ac = _mat_shape(a)
    br_, bc_ = _mat_shape(b)
    if dims == "nn":
        m, k, n = ar, ac, bc_
        assert br_ == k
    elif dims == "nt":
        m, k, n = ar, ac, br_
        assert bc_ == k
    else:
        m, k, n = ac, ar, bc_
        assert br_ == k
    m_unit, n_unit, k_unit = m, n, k
    if a.ndim == 3:
        if dims == "tn":
            m_unit = math.gcd(m_unit, a.shape[2])
        else:
            k_unit = math.gcd(k_unit, a.shape[2])
    if b.ndim == 3:
        if dims == "nt":
            k_unit = math.gcd(k_unit, b.shape[2])
        else:
            n_unit = math.gcd(n_unit, b.shape[2])
    if out_parts > 1:
        n_unit = math.gcd(n_unit, n // out_parts)
    tm, tn, tk = _tile(m_unit, tm), _tile(n_unit, tn), _tile(k_unit, tk)
    gm, gn, gk = m // tm, n // tn, k // tk
    assert gm * tm == m and gn * tn == n and gk * tk == k, (name, m, n, k, tm, tn, tk)

    if dims == "nn":
        a_spec = _mat_spec(a, tm, tk, lambda i, j, kk: (i, kk))
        b_spec = _mat_spec(b, tk, tn, lambda i, j, kk: (kk, j))
        dn = NN
    elif dims == "nt":
        a_spec = _mat_spec(a, tm, tk, lambda i, j, kk: (i, kk))
        b_spec = _mat_spec(b, tn, tk, lambda i, j, kk: (j, kk))
        dn = NT
    else:
        a_spec = _mat_spec(a, tk, tm, lambda i, j, kk: (kk, i))
        b_spec = _mat_spec(b, tk, tn, lambda i, j, kk: (kk, j))
        dn = TN

    if out_parts > 1:
        per = (n // out_parts) // tn
        out_shape = jax.ShapeDtypeStruct((out_parts, m, n // out_parts), out_dtype)
        o_spec = pl.BlockSpec((None, tm, tn), lambda i, j, kk: (j // per, i, j % per))
    else:
        out_shape = jax.ShapeDtypeStruct((m, n), out_dtype)
        o_spec = pl.BlockSpec((tm, tn), lambda i, j, kk: (i, j))

    in_specs = [a_spec, b_spec]
    args = [a, b]
    if add is not None:
        in_specs.append(pl.BlockSpec((tm, tn), lambda i, j, kk: (i, j)))
        args.append(add)

    def body(*refs):
        if add is not None:
            a_ref, b_ref, add_ref, o_ref, acc_ref = refs
        else:
            a_ref, b_ref, o_ref, acc_ref = refs
            add_ref = None
        kk = pl.program_id(2)
        part = lax.dot_general(a_ref[...].astype(MXU), b_ref[...].astype(MXU), dn, preferred_element_type=F32)

        def finish(total):
            if add_ref is not None:
                total = total + add_ref[...]
            o_ref[...] = total.astype(o_ref.dtype)

        if gk == 1:
            finish(part)
        else:
            @pl.when(kk == 0)
            def _():
                acc_ref[...] = part

            @pl.when(jnp.logical_and(kk > 0, kk < gk - 1))
            def _():
                acc_ref[...] += part

            @pl.when(kk == gk - 1)
            def _():
                finish(acc_ref[...] + part)

    (out,), side_results = _pallas(
        body,
        name=name,
        args=args,
        grid=(gm, gn, gk),
        in_specs=in_specs,
        out_specs=[o_spec],
        out_shape=[out_shape],
        scratch_shapes=[pltpu.VMEM((tm, tn) if gk > 1 else (8, 128), F32)],
        semantics=("parallel", "parallel", "arbitrary"),
        vmem_mib=vmem_mib,
        sides=sides,
    )
    return (out, side_results) if sides else out


def _rms_fwd(h, gains, name, tm=384):
    lp, d = h.shape
    tm = _tile(lp, tm)
    n = len(gains)

    def body(*refs):
        h_ref = refs[0]
        g_refs = refs[1:1 + n]
        o_refs = refs[1 + n:]
        x = h_ref[...]
        y = x * lax.rsqrt(jnp.mean(x * x, axis=-1, keepdims=True) + EPS)
        for g_ref, o_ref in zip(g_refs, o_refs):
            o_ref[...] = (y * g_ref[...]).astype(o_ref.dtype)

    row = pl.BlockSpec((tm, d), lambda i: (i, 0))
    vec = pl.BlockSpec((1, d), lambda i: (0, 0))
    return pl.pallas_call(
        body,
        name=name,
        grid=(lp // tm,),
        in_specs=[row] + [vec] * n,
        out_specs=[row] * n,
        out_shape=[jax.ShapeDtypeStruct((lp, d), MXU)] * n,
        compiler_params=_cparams(("parallel",), 40),
    )(h, *gains)


def _rms_bwd(h, gains, dys, res, name, tm=384):
    lp, d = h.shape
    tm = _tile(lp, tm)
    n = len(gains)

    def body(*refs):
        h_ref, res_ref = refs[0], refs[1]
        g_refs = refs[2:2 + n]
        dy_refs = refs[2 + n:2 + 2 * n]
        dh_ref = refs[2 + 2 * n]
        dg_refs = refs[3 + 2 * n:]
        i = pl.program_id(0)
        x = h_ref[...]
        rstd = lax.rsqrt(jnp.mean(x * x, axis=-1, keepdims=True) + EPS)
        xhat = x * rstd
        dh = res_ref[...]
        for g_ref, dy_ref, dg_ref in zip(g_refs, dy_refs, dg_refs):
            dy = dy_ref[...]
            gdy = dy * g_ref[...]
            dh = dh + rstd * (gdy - xhat * jnp.mean(gdy * xhat, axis=-1, keepdims=True))
            part = jnp.sum(dy * xhat, axis=0, keepdims=True)

            @pl.when(i == 0)
            def _():
                dg_ref[...] = part

            @pl.when(i > 0)
            def _():
                dg_ref[...] += part

        dh_ref[...] = dh

    row = pl.BlockSpec((tm, d), lambda i: (i, 0))
    vec = pl.BlockSpec((1, d), lambda i: (0, 0))
    outs = pl.pallas_call(
        body,
        name=name,
        grid=(lp // tm,),
        in_specs=[row, row] + [vec] * n + [row] * n,
        out_specs=[row] + [vec] * n,
        out_shape=[jax.ShapeDtypeStruct((lp, d), F32)] + [jax.ShapeDtypeStruct((1, d), F32)] * n,
        compiler_params=_cparams(("arbitrary",), 56),
    )(h, res, *gains, *dys)
    return outs[0], list(outs[1:])


def _lb_fwd(gamma):
    def body(g_ref, lb_ref):
        g = g_ref[...]
        e = jnp.exp(g - jnp.max(g, axis=0, keepdims=True))
        lb_ref[...] = (e / jnp.sum(e, axis=0, keepdims=True))[0:1, :]

    return pl.pallas_call(body, name="lb_fwd", out_shape=jax.ShapeDtypeStruct((1, gamma.shape[1]), F32))(gamma)


def _lb_bwd(gamma, dlb):
    def body(g_ref, dlb_ref, dg_ref):
        g = g_ref[...]
        e = jnp.exp(g - jnp.max(g, axis=0, keepdims=True))
        s = e / jnp.sum(e, axis=0, keepdims=True)
        rows = lax.broadcasted_iota(jnp.int32, g.shape, 0)
        ds = jnp.where(rows == 0, dlb_ref[...], 0.0)
        dg_ref[...] = s * (ds - jnp.sum(s * ds, axis=0, keepdims=True))

    return pl.pallas_call(body, name="lb_bwd", out_shape=jax.ShapeDtypeStruct(gamma.shape, F32))(gamma, dlb)


def _tri(n, lower):
    r = lax.broadcasted_iota(jnp.int32, (n, n), 0)
    c = lax.broadcasted_iota(jnp.int32, (n, n), 1)
    return jnp.where((r >= c) if lower else (r <= c), 1.0, 0.0).astype(F32)


def _group(nc):
    return max(u for u in (1, 2, 3, 6) if nc % u == 0)


def _running_sum(tri, x):
    hi = x.astype(MXU)
    rest = x - hi.astype(F32)
    mid = rest.astype(MXU)
    lo = (rest - mid.astype(F32)).astype(MXU)
    return _dot(tri, hi) + _dot(tri, mid) + _dot(tri, lo)


def _causal(n):
    r = lax.broadcasted_iota(jnp.int32, (n, n), 0)
    c = lax.broadcasted_iota(jnp.int32, (n, n), 1)
    return r >= c


def _chunk_gates(u_ref, lb, c):
    sl = pl.ds(pl.multiple_of(c * CHUNK, CHUNK), CHUNK)
    valid = (c * CHUNK + lax.broadcasted_iota(jnp.int32, (CHUNK, HEAD), 0)) >= ROW_PAD
    uq = u_ref[0, sl, :]
    uf = u_ref[1, sl, :]
    sq = _sigmoid(uq)
    sf = _sigmoid(uf)
    fg = lb + (1.0 - lb) * sf
    return dict(sl=sl, valid=valid, uq=uq, sq=sq, sf=sf, fg=fg, q=jnp.where(valid, uq * sq, 0.0),
                logf=jnp.where(valid, jnp.log(fg), 0.0), k=jnp.where(valid, 1.0 - fg, 0.0),
                v=jnp.where(valid, u_ref[2, sl, :], 0.0))


def _chunk_decays(x, b):
    b_last = b[CHUNK - 1:CHUNK, :]
    b_mid = b[CHUNK // 2 - 1:CHUNK // 2, :]
    e_qi = jnp.exp(b - b_mid)
    e_ki = jnp.exp(b_mid - b)
    e_kd = jnp.exp(b_last - b)
    e_qe = jnp.exp(b)
    q, k = x["q"], x["k"]
    return dict(x, e_qi=e_qi, e_ki=e_ki, e_kd=e_kd, e_qe=e_qe, qi=q * e_qi, ki=k * e_ki, kd=k * e_kd, qe=q * e_qe,
                decay=jnp.exp(b_last))


def _chunks(u_ref, lb, tri_lower, cs):
    gates = [_chunk_gates(u_ref, lb, c) for c in cs]
    sums = [_running_sum(tri_lower, x["logf"]) for x in gates]
    return [_chunk_decays(x, b) for x, b in zip(gates, sums)]


def _hgrn2_fwd(u4, lb, g_out, name="hgrn2_fwd", sides=()):
    _, lp, d = u4.shape
    nh, nc = d // HEAD, lp // CHUNK
    per = _group(nc)

    def body(u_ref, lb_ref, g_ref, o_ref, og_ref):
        lb_v = lb_ref[...]
        g = g_ref[...]

        tri_lower = _tri(CHUNK, True).astype(MXU)
        causal = _causal(CHUNK)

        def step(i, st):
            xs = _chunks(u_ref, lb_v, tri_lower, [i * per + u for u in range(per)])
            scores = [_dot(x["qi"], x["ki"], NT) for x in xs]
            updates = [_dot(x["v"], x["kd"], TN) for x in xs]
            states = []
            for x, upd in zip(xs, updates):
                states.append(st)
                st = x["decay"] * st + upd
            outs = [_dot(jnp.where(causal, a, 0.0), x["v"]) + _dot(x["qe"], s, NT)
                    for x, a, s in zip(xs, scores, states)]
            for x, o in zip(xs, outs):
                o_ref[x["sl"], :] = o
                on = o * lax.rsqrt(jnp.mean(o * o, axis=-1, keepdims=True) + EPS) * g
                z = u_ref[3, x["sl"], :]
                og_ref[x["sl"], :] = (on * (z * _sigmoid(z))).astype(og_ref.dtype)
            return st

        lax.fori_loop(0, nc // per, step, jnp.zeros((HEAD, HEAD), F32))

    slab = pl.BlockSpec((lp, HEAD), lambda h: (0, h))
    vec = pl.BlockSpec((1, HEAD), lambda h: (0, h))
    outs, side_results = _pallas(
        body,
        name=name,
        args=(u4, lb, g_out),
        grid=(nh,),
        in_specs=[pl.BlockSpec((4, lp, HEAD), lambda h: (0, 0, h)), vec, vec],
        out_specs=[slab, slab],
        out_shape=[jax.ShapeDtypeStruct((lp, d), F32), jax.ShapeDtypeStruct((lp, d), MXU)],
        semantics=("parallel",),
        vmem_mib=48,
        sides=sides,
    )
    return (*outs, side_results) if sides else tuple(outs)


def _hgrn2_bwd(u4, o, d_og, lb, g_out, name="hgrn2_bwd", sides=()):
    _, lp, d = u4.shape
    nh, nc = d // HEAD, lp // CHUNK
    per = _group(nc)

    def body(u_ref, o_ref, dog_ref, lb_ref, g_ref, du_ref, dlb_ref, dg_ref, st_ref, do_ref):
        lb_v = lb_ref[...]
        g = g_ref[...]

        tri_lower = _tri(CHUNK, True).astype(MXU)
        tri_upper = _tri(CHUNK, False).astype(MXU)
        causal = _causal(CHUNK)

        def fwd_step(i, carry):
            st, dg_acc = carry
            cs = [i * per + u for u in range(per)]
            xs = _chunks(u_ref, lb_v, tri_lower, cs)
            updates = [_dot(x["v"], x["kd"], TN) for x in xs]
            for c, x, upd in zip(cs, xs, updates):
                st_ref[c] = st
                st = x["decay"] * st + upd
            for x in xs:
                sl = x["sl"]
                ov = o_ref[sl, :]
                rstd = lax.rsqrt(jnp.mean(ov * ov, axis=-1, keepdims=True) + EPS)
                on = ov * rstd
                z = u_ref[3, sl, :]
                sz = _sigmoid(z)
                dog = dog_ref[sl, :]
                dy = dog * (z * sz)
                dz = dog * (on * g) * (sz * (1.0 + z * (1.0 - sz)))
                du_ref[3, sl, :] = dz.astype(du_ref.dtype)
                gdy = dy * g
                do = rstd * (gdy - on * jnp.mean(gdy * on, axis=-1, keepdims=True))
                do_ref[sl, :] = jnp.where(x["valid"], do, 0.0)
                dg_acc = dg_acc + jnp.sum(dy * on, axis=0, keepdims=True)
            return st, dg_acc

        _, dg_tot = lax.fori_loop(0, nc // per, fwd_step, (jnp.zeros((HEAD, HEAD), F32), jnp.zeros((1, HEAD), F32)))
        dg_ref[...] = dg_tot

        def bwd_step(i, carry):
            gt, dlb_acc = carry
            cs = [nc - 1 - (i * per + u) for u in range(per)]
            xs = _chunks(u_ref, lb_v, tri_lower, cs)
            dos = [do_ref[x["sl"], :] for x in xs]
            sts = [st_ref[c] for c in cs]
            scores = [jnp.where(causal, _dot(x["qi"], x["ki"], NT), 0.0) for x in xs]
            d_scores = [jnp.where(causal, _dot(do, x["v"], NT), 0.0) for x, do in zip(xs, dos)]
            d_qes = [_dot(do, st) for do, st in zip(dos, sts)]
            g_updates = [_dot(do, x["qe"], TN) for x, do in zip(xs, dos)]
            gts = []
            for x, upd in zip(xs, g_updates):
                gts.append(gt)
                gt = x["decay"] * gt + upd
            d_kds = [_dot(x["v"], g_) for x, g_ in zip(xs, gts)]
            dvs = [_dot(x["kd"], g_, NT) + _dot(a, do, TN) for x, g_, a, do in zip(xs, gts, scores, dos)]
            d_qis = [_dot(d_a, x["ki"]) for x, d_a in zip(xs, d_scores)]
            d_kis = [_dot(d_a, x["qi"], TN) for x, d_a in zip(xs, d_scores)]
            rows = lax.broadcasted_iota(jnp.int32, (CHUNK, HEAD), 0)
            dbs = []
            for x, g_, st, d_qi, d_ki, d_qe, d_kd in zip(xs, gts, sts, d_qis, d_kis, d_qes, d_kds):
                t_qi, t_ki, t_qe, t_kd = d_qi * x["qi"], d_ki * x["ki"], d_qe * x["qe"], d_kd * x["kd"]
                d_decay = jnp.sum(g_ * st, axis=0, keepdims=True)
                d_mid = jnp.sum(t_ki - t_qi, axis=0, keepdims=True)
                d_last = jnp.sum(t_kd, axis=0, keepdims=True) + d_decay * x["decay"]
                dbs.append(t_qi - t_ki + t_qe - t_kd + jnp.where(rows == CHUNK // 2 - 1, d_mid, 0.0)
                           + jnp.where(rows == CHUNK - 1, d_last, 0.0))
            dlogfs = [_running_sum(tri_upper, db) for db in dbs]
            for x, dlogf, dv, d_qi, d_ki, d_qe, d_kd in zip(xs, dlogfs, dvs, d_qis, d_kis, d_qes, d_kds):
                sl = x["sl"]
                dq = d_qi * x["e_qi"] + d_qe * x["e_qe"]
                dk = d_ki * x["e_ki"] + d_kd * x["e_kd"]
                valid, sq, sf, uq = x["valid"], x["sq"], x["sf"], x["uq"]
                dfg = jnp.where(valid, dlogf / x["fg"] - dk, 0.0)
                du_ref[0, sl, :] = jnp.where(valid, dq * (sq * (1.0 + uq * (1.0 - sq))), 0.0).astype(du_ref.dtype)
                du_ref[1, sl, :] = (dfg * (1.0 - lb_v) * (sf * (1.0 - sf))).astype(du_ref.dtype)
                du_ref[2, sl, :] = jnp.where(valid, dv, 0.0).astype(du_ref.dtype)
                dlb_acc = dlb_acc + jnp.sum(dfg * (1.0 - sf), axis=0, keepdims=True)
            return gt, dlb_acc

        _, dlb_tot = lax.fori_loop(0, nc // per, bwd_step, (jnp.zeros((HEAD, HEAD), F32), jnp.zeros((1, HEAD), F32)))
        dlb_ref[...] = dlb_tot

    slab = pl.BlockSpec((lp, HEAD), lambda h: (0, h))
    vec = pl.BlockSpec((1, HEAD), lambda h: (0, h))
    quad = pl.BlockSpec((4, lp, HEAD), lambda h: (0, 0, h))
    outs, side_results = _pallas(
        body,
        name=name,
        args=(u4, o, d_og, lb, g_out),
        grid=(nh,),
        in_specs=[quad, slab, slab, vec, vec],
        out_specs=[quad, vec, vec],
        out_shape=[jax.ShapeDtypeStruct((4, lp, d), MXU), jax.ShapeDtypeStruct((1, d), F32),
                   jax.ShapeDtypeStruct((1, d), F32)],
        scratch_shapes=[pltpu.VMEM((nc, HEAD, HEAD), F32), pltpu.VMEM((lp, HEAD), F32)],
        semantics=("parallel",),
        vmem_mib=58,
        sides=sides,
    )
    return (*outs, side_results) if sides else tuple(outs)


WIDE = 2 * HEAD
INV_SCALE = HEAD ** 0.5


def _split3(x):
    hi = x.astype(MXU).astype(F32)
    rest = x - hi
    mid = rest.astype(MXU).astype(F32)
    return hi, mid, (rest - mid).astype(MXU).astype(F32)


def _extra_cols(rows, first, second):
    lane = lax.broadcasted_iota(jnp.int32, (rows, HEAD), 1)
    out = jnp.where(lane < 6, 1.0, 0.0).astype(F32)
    for base, terms in ((0, first), (3, second)):
        if terms is not None:
            for j, term in enumerate(terms):
                out = jnp.where(lane == base + j, term, out)
    return out


def _head_col(a, h):
    lane = lax.broadcasted_iota(jnp.int32, a.shape, 1)
    return jnp.sum(jnp.where(lane == h, a, 0.0), axis=-1, keepdims=True)


def _attn_operands(ub, ukv, g_q, g_k, f_cum, name="attn_operands", tm=384):
    _, lp, d = ub.shape
    tm = _tile(lp, tm)
    nh = d // HEAD

    def body(q_ref, k_ref, v_ref, gq_ref, gk_ref, f_ref, qa_ref, ka_ref, va_ref):
        i = pl.program_id(0)
        f = f_ref[...]
        is_pad = (i * tm + lax.broadcasted_iota(jnp.int32, (tm, 1), 0)) < ROW_PAD
        ones_only = _extra_cols(tm, None, (0.0, 0.0, 0.0)).astype(MXU)
        for h in range(nh):
            hs = slice(h * HEAD, (h + 1) * HEAD)
            lo, hi = h * WIDE, h * WIDE + HEAD
            f_h = _head_col(f, h)
            for x_ref, g_ref, o_ref in ((q_ref, gq_ref, qa_ref), (k_ref, gk_ref, ka_ref)):
                x = x_ref[:, hs]
                y = x * lax.rsqrt(jnp.mean(x * x, axis=-1, keepdims=True) + EPS)
                o_ref[:, lo:hi] = (y * g_ref[:, hs]).astype(o_ref.dtype)
            qa_ref[:, hi:hi + HEAD] = _extra_cols(tm, _split3(f_h * INV_SCALE), None).astype(MXU)
            f_key = jnp.where(is_pad, -MASK_VALUE, f_h)
            ka_ref[:, hi:hi + HEAD] = _extra_cols(tm, None, _split3(-f_key * INV_SCALE)).astype(MXU)
            va_ref[:, lo:hi] = v_ref[:, hs].astype(MXU)
            va_ref[:, hi:hi + HEAD] = ones_only

    row = pl.BlockSpec((tm, d), lambda i: (i, 0))
    wide = pl.BlockSpec((tm, nh * WIDE), lambda i: (i, 0))
    vec = pl.BlockSpec((1, d), lambda i: (0, 0))
    return pl.pallas_call(
        body,
        name=name,
        grid=(lp // tm,),
        in_specs=[pl.BlockSpec((None, tm, d), lambda i: (0, i, 0)), pl.BlockSpec((None, tm, d), lambda i: (0, i, 0)),
                  pl.BlockSpec((None, tm, d), lambda i: (1, i, 0)), vec, vec, pl.BlockSpec((tm, 128), lambda i: (i, 0))],
        out_specs=[wide, wide, wide],
        out_shape=[jax.ShapeDtypeStruct((lp, nh * WIDE), MXU)] * 3,
        compiler_params=_cparams(("parallel",), 56),
    )(ub, ukv, ukv, g_q, g_k, f_cum)


def _head_rms_bwd_tile(x, g, dy):
    rstd = lax.rsqrt(jnp.mean(x * x, axis=-1, keepdims=True) + EPS)
    xhat = x * rstd
    gdy = dy * g
    dx = rstd * (gdy - xhat * jnp.mean(gdy * xhat, axis=-1, keepdims=True))
    return dx, jnp.sum(dy * xhat, axis=0, keepdims=True)


def _fgate_fwd(ufl, b_f):
    lp = ufl.shape[0]
    nb = lp // 128

    def body(u_ref, b_ref, f_ref):
        def step(i, carry):
            sl = pl.ds(pl.multiple_of(i * 128, 128), 128)
            valid = (i * 128 + lax.broadcasted_iota(jnp.int32, (128, 128), 0)) >= ROW_PAD
            x = u_ref[sl, :] + b_ref[...]
            logf = jnp.where(valid, jnp.minimum(x, 0.0) - jnp.log(1.0 + jnp.exp(-jnp.abs(x))), 0.0)
            f = _dot_exact(_tri(128, True), logf) + carry
            f_ref[sl, :] = f
            return f[127:128, :]

        lax.fori_loop(0, nb, step, jnp.zeros((1, 128), F32))

    return pl.pallas_call(body, name="fgate_fwd", out_shape=jax.ShapeDtypeStruct((lp, 128), F32))(ufl, b_f)


def _fgate_bwd(ufl, b_f, d_f):
    lp = ufl.shape[0]
    nb = lp // 128

    def body(u_ref, b_ref, df_ref, du_ref, db_ref):
        def step(j, carry):
            later, db_acc = carry
            i = nb - 1 - j
            sl = pl.ds(pl.multiple_of(i * 128, 128), 128)
            valid = (i * 128 + lax.broadcasted_iota(jnp.int32, (128, 128), 0)) >= ROW_PAD
            x = u_ref[sl, :] + b_ref[...]
            df = df_ref[sl, :]
            dlogf = _dot_exact(_tri(128, False), df) + later
            dx = jnp.where(valid, dlogf * _sigmoid(-x), 0.0)
            du_ref[sl, :] = dx.astype(du_ref.dtype)
            return later + jnp.sum(df, axis=0, keepdims=True), db_acc + jnp.sum(dx, axis=0, keepdims=True)

        _, db_tot = lax.fori_loop(0, nb, step, (jnp.zeros((1, 128), F32), jnp.zeros((1, 128), F32)))
        db_ref[...] = db_tot

    return pl.pallas_call(
        body, name="fgate_bwd",
        out_shape=[jax.ShapeDtypeStruct((lp, 128), MXU), jax.ShapeDtypeStruct((1, 128), F32)],
    )(ufl, b_f, d_f)


STRIP = 32
PAIR = 2


def _strip_causal(r, t):
    row = r + lax.broadcasted_iota(jnp.int32, (STRIP, t), 0)
    col = lax.broadcasted_iota(jnp.int32, (STRIP, t), 1)
    return col <= row


def _fox_fwd(qa, ka, va, ub, name="fox_fwd", t=384):
    lp = qa.shape[0]
    d = ub.shape[2]
    t = _tile(lp, t)
    nh, nq = d // HEAD, lp // t
    scale = HEAD ** -0.5

    assert nh % PAIR == 0
    heads = range(PAIR)

    def body(q_ref, k_ref, v_ref, z_ref, o_ref, olo_ref, og_ref, qb_ref,
             s_ref, p_ref, m_ref, a_ref, l_ref, acc_ref):
        qb = pl.program_id(1)
        m_ref[...] = jnp.full((PAIR, t, 1), MASK_VALUE, F32)
        l_ref[...] = jnp.zeros((PAIR, t, 128), F32)
        acc_ref[...] = jnp.zeros((PAIR, 2 * t, HEAD), F32)

        def step(kb, diagonal):
            ks = pl.ds(pl.multiple_of(kb * t, t), t)
            for j in heads:
                ws = slice(j * WIDE, (j + 1) * WIDE)
                s_ref[j] = _dot(q_ref[:, ws], k_ref[ks, ws], NT)
            for j in heads:
                for r in range(0, t, STRIP):
                    rs = slice(r, r + STRIP)
                    x = s_ref[j, rs, :] * scale
                    if diagonal:
                        x = jnp.where(_strip_causal(r, t), x, MASK_VALUE)
                    m_old = m_ref[j, rs, :]
                    m_new = jnp.maximum(m_old, jnp.max(x, axis=-1, keepdims=True))
                    alpha = jnp.exp(m_old - m_new)
                    p = jnp.exp(x - m_new)
                    m_ref[j, rs, :] = m_new
                    a_ref[j, rs, :] = alpha
                    l_ref[j, rs, :] = alpha * l_ref[j, rs, :] + sum(p[:, c:c + 128] for c in range(0, t, 128))
                    p_hi = p.astype(MXU)
                    p_ref[j, rs, :] = p_hi
                    p_ref[j, t + r:t + r + STRIP, :] = (p - p_hi.astype(F32)).astype(MXU)
            for j in heads:
                pv = _dot(p_ref[j], v_ref[ks, j * WIDE:j * WIDE + HEAD])
                alpha = a_ref[j]
                acc_ref[j, 0:t, :] = alpha * acc_ref[j, 0:t, :] + pv[0:t]
                acc_ref[j, t:2 * t, :] = alpha * acc_ref[j, t:2 * t, :] + pv[t:2 * t]

        def off_diagonal(kb, carry):
            step(kb, False)
            return carry

        lax.fori_loop(0, qb, off_diagonal, 0)
        step(qb, True)
        is_pad = (qb * t + lax.broadcasted_iota(jnp.int32, (t, 1), 0)) < ROW_PAD
        for j in heads:
            hs = slice(j * HEAD, (j + 1) * HEAD)
            l = jnp.sum(l_ref[j], axis=-1, keepdims=True)
            o = acc_ref[j, 0:t, :] / l
            o_ref[:, hs] = o
            olo_ref[:, hs] = acc_ref[j, t:2 * t, :] / l
            z = z_ref[:, hs]
            og_ref[:, hs] = (o * (z * _sigmoid(z))).astype(og_ref.dtype)
            extra = q_ref[:, j * WIDE + HEAD:(j + 1) * WIDE].astype(F32)
            f_scaled = extra[:, 0:1] + extra[:, 1:2] + extra[:, 2:3]
            log_term = jnp.where(is_pad, MASK_VALUE * INV_SCALE, f_scaled - (m_ref[j] + jnp.log(l)) * INV_SCALE)
            qb_ref[:, j * WIDE:j * WIDE + HEAD] = q_ref[:, j * WIDE:j * WIDE + HEAD]
            qb_ref[:, j * WIDE + HEAD:(j + 1) * WIDE] = _extra_cols(t, _split3(log_term), None).astype(qb_ref.dtype)

    scratch = [pltpu.VMEM((PAIR, t, t), F32), pltpu.VMEM((PAIR, 2 * t, t), MXU), pltpu.VMEM((PAIR, t, 1), F32),
               pltpu.VMEM((PAIR, t, 1), F32), pltpu.VMEM((PAIR, t, 128), F32), pltpu.VMEM((PAIR, 2 * t, HEAD), F32)]
    tile = pl.BlockSpec((t, PAIR * HEAD), lambda g, i: (i, g))
    wide_tile = pl.BlockSpec((t, PAIR * WIDE), lambda g, i: (i, g))
    wide_all = pl.BlockSpec((lp, PAIR * WIDE), lambda g, i: (0, g))
    return pl.pallas_call(
        body,
        name=name,
        grid=(nh // PAIR, nq),
        in_specs=[wide_tile, wide_all, wide_all, pl.BlockSpec((None, t, PAIR * HEAD), lambda g, i: (1, i, g))],
        out_specs=[tile, tile, tile, wide_tile],
        out_shape=[jax.ShapeDtypeStruct((lp, d), F32), jax.ShapeDtypeStruct((lp, d), F32),
                   jax.ShapeDtypeStruct((lp, d), MXU), jax.ShapeDtypeStruct((lp, nh * WIDE), MXU)],
        scratch_shapes=scratch,
        compiler_params=_cparams(("parallel", "arbitrary"), 48),
    )(qa, ka, va, ub)


def _fox_bwd(qb, ka, va, doa, d, name="fox_bwd", t=384, sides=()):
    lp = qb.shape[0]
    t = _tile(lp, t)
    nh, nk = d // HEAD, lp // t
    scale = HEAD ** -0.5

    assert nh % PAIR == 0
    heads = range(PAIR)

    def body(q_ref, do_ref, k_ref, v_ref, dq_ref, dk_ref, dv_ref, dfk_ref,
             s_ref, dp_ref, p_ref, ds_ref, col_ref):
        kb = pl.program_id(1)

        @pl.when(kb == 0)
        def _():
            dq_ref[...] = jnp.zeros_like(dq_ref)

        dk_ref[...] = jnp.zeros_like(dk_ref)
        dv_ref[...] = jnp.zeros_like(dv_ref)
        col_ref[...] = jnp.zeros_like(col_ref)

        def step(qb, diagonal):
            qs = pl.ds(pl.multiple_of(qb * t, t), t)
            for j in heads:
                ws = slice(j * WIDE, (j + 1) * WIDE)
                s_ref[j] = _dot(q_ref[qs, ws], k_ref[:, ws], NT)
                dp_ref[j] = _dot(do_ref[qs, ws], v_ref[:, ws], NT)
            for j in heads:
                for r in range(0, t, STRIP):
                    rs = slice(r, r + STRIP)
                    x = s_ref[j, rs, :] * scale
                    if diagonal:
                        x = jnp.where(_strip_causal(r, t), x, MASK_VALUE)
                    p = jnp.exp(x)
                    ds = p * dp_ref[j, rs, :]
                    p_ref[j, rs, :] = p.astype(MXU)
                    ds_ref[j, rs, :] = (ds * scale).astype(MXU)
                    col_ref[j] += ds
            for j in heads:
                hs = slice(j * HEAD, (j + 1) * HEAD)
                narrow = slice(j * WIDE, j * WIDE + HEAD)
                dsb = ds_ref[j]
                dv_ref[:, hs] += _dot(p_ref[j], do_ref[qs, narrow], TN)
                dq_ref[qs, hs] += _dot(dsb, k_ref[:, narrow])
                dk_ref[:, hs] += _dot(dsb, q_ref[qs, narrow], TN)

        def off_diagonal(qb, carry):
            step(qb, False)
            return carry

        step(kb, True)
        lax.fori_loop(kb + 1, nk, off_diagonal, 0)
        for j in heads:
            dfk_ref[j] = -jnp.sum(col_ref[j], axis=0, keepdims=True)

    scratch = [pltpu.VMEM((PAIR, t, t), F32), pltpu.VMEM((PAIR, t, t), F32), pltpu.VMEM((PAIR, t, t), MXU),
               pltpu.VMEM((PAIR, t, t), MXU), pltpu.VMEM((PAIR, STRIP, t), F32)]
    whole = pl.BlockSpec((lp, PAIR * HEAD), lambda g, j: (0, g))
    tile = pl.BlockSpec((t, PAIR * HEAD), lambda g, j: (j, g))
    wide_all = pl.BlockSpec((lp, PAIR * WIDE), lambda g, j: (0, g))
    wide_tile = pl.BlockSpec((t, PAIR * WIDE), lambda g, j: (j, g))
    outs, side_results = _pallas(
        body,
        name=name,
        args=(qb, doa, ka, va),
        grid=(nh // PAIR, nk),
        in_specs=[wide_all, wide_all, wide_tile, wide_tile],
        out_specs=[whole, tile, tile, pl.BlockSpec((PAIR, 1, t), lambda g, j: (g, 0, j))],
        out_shape=[jax.ShapeDtypeStruct((lp, d), F32)] * 3 + [jax.ShapeDtypeStruct((nh, 1, lp), F32)],
        scratch_shapes=scratch,
        semantics=("parallel", "arbitrary"),
        vmem_mib=48,
        sides=sides,
    )
    return (*outs, side_results) if sides else tuple(outs)


def _fox_do(d_og, o, o_lo, ub, name="fox_do", tm=384):
    lp, d = o.shape
    tm = _tile(lp, tm)
    nh = d // HEAD

    def body(dog_ref, o_ref, olo_ref, z_ref, doa_ref):
        for h in range(nh):
            hs = slice(h * HEAD, (h + 1) * HEAD)
            z = z_ref[:, hs]
            do = (dog_ref[:, hs] * (z * _sigmoid(z))).astype(doa_ref.dtype)
            delta = jnp.sum(do.astype(F32) * (o_ref[:, hs] + olo_ref[:, hs]), axis=-1, keepdims=True)
            doa_ref[:, h * WIDE:h * WIDE + HEAD] = do
            doa_ref[:, h * WIDE + HEAD:(h + 1) * WIDE] = _extra_cols(tm, _split3(-delta), (0.0, 0.0, 0.0)).astype(
                doa_ref.dtype)

    row = pl.BlockSpec((tm, d), lambda i: (i, 0))
    return pl.pallas_call(
        body,
        name=name,
        grid=(lp // tm,),
        in_specs=[row, row, row, pl.BlockSpec((None, tm, d), lambda i: (1, i, 0))],
        out_specs=pl.BlockSpec((tm, nh * WIDE), lambda i: (i, 0)),
        out_shape=jax.ShapeDtypeStruct((lp, nh * WIDE), MXU),
        compiler_params=_cparams(("parallel",), 56),
    )(d_og, o, o_lo, ub)


def _fox_q_bwd(ub, d_og, o, dqn, g_q, name="fox_q_bwd", tm=384):
    _, lp, d = ub.shape
    tm = _tile(lp, tm)
    nh = d // HEAD

    def body(ub_ref, dog_ref, o_ref, dqn_ref, g_ref, dub_ref, dg_ref):
        i = pl.program_id(0)
        for h in range(nh):
            hs = slice(h * HEAD, (h + 1) * HEAD)
            z = ub_ref[1, :, hs]
            sz = _sigmoid(z)
            dub_ref[1, :, hs] = (dog_ref[:, hs] * o_ref[:, hs] * (sz * (1.0 + z * (1.0 - sz)))).astype(dub_ref.dtype)
            dx, dg = _head_rms_bwd_tile(ub_ref[0, :, hs], g_ref[:, hs], dqn_ref[:, hs])
            dub_ref[0, :, hs] = dx.astype(dub_ref.dtype)

            @pl.when(i == 0)
            def _():
                dg_ref[:, hs] = dg

            @pl.when(i > 0)
            def _():
                dg_ref[:, hs] += dg

    row = pl.BlockSpec((tm, d), lambda i: (i, 0))
    pair = pl.BlockSpec((2, tm, d), lambda i: (0, i, 0))
    vec = pl.BlockSpec((1, d), lambda i: (0, 0))
    return pl.pallas_call(
        body,
        name=name,
        grid=(lp // tm,),
        in_specs=[pair, row, row, row, vec],
        out_specs=[pair, vec],
        out_shape=[jax.ShapeDtypeStruct((2, lp, d), MXU), jax.ShapeDtypeStruct((1, d), F32)],
        compiler_params=_cparams(("arbitrary",), 56),
    )(ub, d_og, o, dqn, g_q)


def _fox_k_bwd(ukv, dkn, dv, g_k, name="fox_k_bwd", tm=384):
    _, lp, d = ukv.shape
    tm = _tile(lp, tm)
    nh = d // HEAD

    def body(ukv_ref, dkn_ref, dv_ref, g_ref, du_ref, dg_ref):
        i = pl.program_id(0)
        du_ref[1] = dv_ref[...].astype(du_ref.dtype)
        for h in range(nh):
            hs = slice(h * HEAD, (h + 1) * HEAD)
            dx, dg = _head_rms_bwd_tile(ukv_ref[:, hs], g_ref[:, hs], dkn_ref[:, hs])
            du_ref[0, :, hs] = dx.astype(du_ref.dtype)

            @pl.when(i == 0)
            def _():
                dg_ref[:, hs] = dg

            @pl.when(i > 0)
            def _():
                dg_ref[:, hs] += dg

    row = pl.BlockSpec((tm, d), lambda i: (i, 0))
    vec = pl.BlockSpec((1, d), lambda i: (0, 0))
    return pl.pallas_call(
        body,
        name=name,
        grid=(lp // tm,),
        in_specs=[pl.BlockSpec((None, tm, d), lambda i: (0, i, 0)), row, row, vec],
        out_specs=[pl.BlockSpec((2, tm, d), lambda i: (0, i, 0)), vec],
        out_shape=[jax.ShapeDtypeStruct((2, lp, d), MXU), jax.ShapeDtypeStruct((1, d), F32)],
        compiler_params=_cparams(("arbitrary",), 48),
    )(ukv, dkn, dv, g_k)


def _loss(h, target, name="loss_head"):
    lp, d = h.shape
    nb = lp // 128

    def body(h_ref, t_ref, loss_ref, dh_ref, acc_ref):
        i = pl.program_id(0)

        @pl.when(i == 0)
        def _():
            acc_ref[...] = jnp.zeros_like(acc_ref)
            dh_ref[...] = jnp.zeros_like(dh_ref)

        @pl.when(i > 0)
        def _():
            err = h_ref[...] - t_ref[...]
            dh_ref[...] = err * (1.0 / d)
            acc_ref[...] += jnp.sum(jnp.sum(err * err, axis=-1, keepdims=True) * (1.0 / d), axis=0, keepdims=True)

        @pl.when(i == nb - 1)
        def _():
            loss_ref[...] = 0.5 * acc_ref[...]

    return pl.pallas_call(
        body,
        name=name,
        grid=(nb,),
        in_specs=[pl.BlockSpec((128, d), lambda i: (i, 0)),
                  pl.BlockSpec((128, d), lambda i: (jnp.maximum(i - 1, 0), 0))],
        out_specs=[pl.BlockSpec((1, 1), lambda i: (0, 0)), pl.BlockSpec((128, d), lambda i: (i, 0))],
        out_shape=[jax.ShapeDtypeStruct((1, 1), F32), jax.ShapeDtypeStruct((lp, d), F32)],
        scratch_shapes=[pltpu.VMEM((1, 1), F32)],
        compiler_params=_cparams(("arbitrary",), 32),
    )(h, target)


def _adam_math(w, g, m, v):
    m = ADAM_B1 * m + (1.0 - ADAM_B1) * g
    v = ADAM_B2 * v + (1.0 - ADAM_B2) * (g * g)
    m_hat = m / (1.0 - ADAM_B1 ** ADAM_STEP)
    v_hat = v / (1.0 - ADAM_B2 ** ADAM_STEP)
    delta = -ADAM_LR * (m_hat / (jnp.sqrt(v_hat) + ADAM_EPS) + ADAM_WD * w)
    return delta, m, v


def _adamw(parts, w, m, v, name, tm=256):
    n, r, c = parts.shape
    tm = tm if r % tm == 0 else r

    def body(p_ref, w_ref, m_ref, v_ref, g_ref, d_ref, nm_ref, nv_ref):
        g = p_ref[0].astype(F32)
        for j in range(1, n):
            g = g + p_ref[j].astype(F32)
        g_ref[...] = g
        d_ref[...], nm_ref[...], nv_ref[...] = _adam_math(w_ref[...], g, m_ref[...], v_ref[...])

    row = pl.BlockSpec((tm, c), lambda i: (i, 0))
    return pl.pallas_call(
        body,
        name=name,
        grid=(r // tm,),
        in_specs=[pl.BlockSpec((n, tm, c), lambda i: (0, i, 0)), row, row, row],
        out_specs=[row] * 4,
        out_shape=[jax.ShapeDtypeStruct((r, c), F32)] * 4,
        compiler_params=_cparams(("parallel",), 48),
    )(parts, w, m, v)


def _place():
    x, y, c = lax.axis_index("x"), lax.axis_index("y"), lax.axis_index("c")
    return x, y, c


def _other_chips(x, y):
    return [(1 - x, y), (x, 1 - y), (1 - x, 1 - y)]


def _any_spec():
    return pl.BlockSpec(memory_space=pl.ANY)


class _Side:
    def __init__(self, ins, outs, sems, start, finish, aliases=None):
        self.ins, self.outs, self.sems = list(ins), list(outs), sems
        self.start, self.finish = start, finish
        self.aliases = dict(aliases or {})


def _pallas(body, *, name, out_shape, args=(), grid=(), in_specs=(), out_specs=(), scratch_shapes=(),
            semantics=(), vmem_mib=None, sides=()):
    n_in, n_out, n_scr = len(args), len(out_shape), len(scratch_shapes)
    side_ins = [a for s in sides for a in s.ins]
    side_outs = [o for s in sides for o in s.outs]
    side_sems = [pltpu.SemaphoreType.DMA((max(k, 1),)) for s in sides for k in s.sems]
    aliases, in_at, out_at = {}, n_in, n_out
    for s in sides:
        aliases.update({in_at + i: out_at + o for i, o in s.aliases.items()})
        in_at, out_at = in_at + len(s.ins), out_at + len(s.outs)

    def wrapped(*refs):
        at = [0]

        def take(k):
            got = refs[at[0]:at[0] + k]
            at[0] += k
            return got

        main_in = take(n_in)
        s_in = [take(len(s.ins)) for s in sides]
        main_out = take(n_out)
        s_out = [take(len(s.outs)) for s in sides]
        main_scr = take(n_scr)
        s_sem = [take(3) for s in sides]

        def run(stage):
            for s, i_, o_, m_ in zip(sides, s_in, s_out, s_sem):
                getattr(s, stage)(i_, o_, *m_)

        first = last = None
        for k, g in enumerate(grid):
            i = pl.program_id(k)
            first = (i == 0) if first is None else jnp.logical_and(first, i == 0)
            last = (i == g - 1) if last is None else jnp.logical_and(last, i == g - 1)
        if sides:
            run("start") if first is None else pl.when(first)(lambda: run("start"))
        body(*main_in, *main_out, *main_scr)
        if sides:
            run("finish") if last is None else pl.when(last)(lambda: run("finish"))

    kw = {}
    if grid:
        kw["grid"] = grid
    if semantics or vmem_mib:
        sem = tuple("arbitrary" for _ in grid) if sides else tuple(semantics)
        kw["compiler_params"] = pltpu.CompilerParams(
            dimension_semantics=sem or None, vmem_limit_bytes=vmem_mib * MIB if vmem_mib else None)
    res = pl.pallas_call(
        wrapped,
        name=name,
        in_specs=list(in_specs) + [_any_spec()] * len(side_ins),
        out_specs=list(out_specs) + [_any_spec()] * len(side_outs),
        out_shape=list(out_shape) + side_outs,
        scratch_shapes=list(scratch_shapes) + side_sems,
        input_output_aliases=aliases,
        **kw,
    )(*args, *side_ins)
    main, rest, per_side = list(res[:n_out]), list(res[n_out:]), []
    for s in sides:
        per_side.append(rest[:len(s.outs)])
        rest = rest[len(s.outs):]
    return main, per_side


def _slot(p):
    return 4 * p[0] + 2 * p[1] + p[2]


def _sibling_side(grads):
    n = len(grads)

    def copies(ins, outs, send_sems, recv_sems):
        x, y, c = _place()
        return [pltpu.make_async_remote_copy(
            src_ref=ins[t].at[2 * chip + (1 - c)], dst_ref=outs[t].at[chip],
            send_sem=send_sems.at[4 * t + chip], recv_sem=recv_sems.at[4 * t + chip],
            device_id=(x, y, 1 - c), device_id_type=MESH) for t in range(n) for chip in range(N_CHIP)]

    def start(ins, outs, send_sems, recv_sems, local_sems):
        for cp in copies(ins, outs, send_sems, recv_sems):
            cp.start()

    def finish(ins, outs, send_sems, recv_sems, local_sems):
        for cp in copies(ins, outs, send_sems, recv_sems):
            cp.wait()

    outs = [jax.ShapeDtypeStruct((N_CHIP,) + g.shape[1:], g.dtype) for g in grads]
    return _Side(grads, outs, (4 * n, 4 * n, 0), start, finish)


def _chips_side(partials):
    n = len(partials)

    def copies(ins, outs, send_sems, recv_sems, local_sems):
        x, y, c = _place()
        my_chip = 2 * x + y
        local = [pltpu.make_async_copy(ins[t].at[my_chip], outs[t].at[my_chip], local_sems.at[t]) for t in range(n)]
        sends, recvs = [], []
        for t in range(n):
            for j, chip in enumerate(_other_chips(x, y)):
                their = 2 * chip[0] + chip[1]
                sems = dict(send_sem=send_sems.at[3 * t + j], recv_sem=recv_sems.at[3 * t + j],
                            device_id=(*chip, c), device_id_type=MESH)
                sends.append(pltpu.make_async_remote_copy(src_ref=ins[t].at[their], dst_ref=outs[t].at[my_chip], **sems))
                recvs.append(pltpu.make_async_remote_copy(src_ref=ins[t].at[my_chip], dst_ref=outs[t].at[their], **sems))
        return local, sends, recvs

    def start(ins, outs, send_sems, recv_sems, local_sems):
        local, sends, _ = copies(ins, outs, send_sems, recv_sems, local_sems)
        for cp in local + sends:
            cp.start()

    def finish(ins, outs, send_sems, recv_sems, local_sems):
        local, sends, recvs = copies(ins, outs, send_sems, recv_sems, local_sems)
        for cp in sends:
            cp.wait_send()
        for cp in recvs:
            cp.wait_recv()
        for cp in local:
            cp.wait()

    outs = [jax.ShapeDtypeStruct(p.shape, p.dtype) for p in partials]
    return _Side(partials, outs, (3 * n, 3 * n, n), start, finish)


def _gather_own_side(blocks):
    n = len(blocks)

    def copies(ins, outs, send_sems, recv_sems, local_sems):
        x, y, c = _place()
        me = (x, y, c)
        peers = [(x, y, 1 - c)] + [(*chip, c) for chip in _other_chips(x, y)]
        local = [pltpu.make_async_copy(ins[t], outs[t].at[_slot(me)], local_sems.at[t]) for t in range(n)]
        sends, recvs = [], []
        for t in range(n):
            for k, peer in enumerate(peers):
                sems = dict(send_sem=send_sems.at[4 * t + k], recv_sem=recv_sems.at[4 * t + k],
                            device_id=peer, device_id_type=MESH)
                sends.append(pltpu.make_async_remote_copy(src_ref=ins[t], dst_ref=outs[t].at[_slot(me)], **sems))
                recvs.append(pltpu.make_async_remote_copy(src_ref=ins[t], dst_ref=outs[t].at[_slot(peer)], **sems))
        return local, sends, recvs

    def start(ins, outs, send_sems, recv_sems, local_sems):
        local, sends, _ = copies(ins, outs, send_sems, recv_sems, local_sems)
        for cp in local + sends:
            cp.start()

    def finish(ins, outs, send_sems, recv_sems, local_sems):
        local, sends, recvs = copies(ins, outs, send_sems, recv_sems, local_sems)
        for cp in sends:
            cp.wait_send()
        for cp in recvs:
            cp.wait_recv()
        for cp in local:
            cp.wait()

    outs = [jax.ShapeDtypeStruct((N_DEV,) + b.shape, b.dtype) for b in blocks]
    return _Side(blocks, outs, (4 * n, 4 * n, n), start, finish)


def _gather_pass_side(gathered):
    n = len(gathered)

    def copies(outs, send_sems, recv_sems):
        x, y, c = _place()
        sends, recvs = [], []
        for t in range(n):
            for j, chip in enumerate(_other_chips(x, y)):
                sems = dict(send_sem=send_sems.at[3 * t + j], recv_sem=recv_sems.at[3 * t + j],
                            device_id=(x, y, 1 - c), device_id_type=MESH)
                mine, theirs = outs[t].at[_slot((*chip, c))], outs[t].at[_slot((*chip, 1 - c))]
                sends.append(pltpu.make_async_remote_copy(src_ref=mine, dst_ref=mine, **sems))
                recvs.append(pltpu.make_async_remote_copy(src_ref=mine, dst_ref=theirs, **sems))
        return sends, recvs

    def start(ins, outs, send_sems, recv_sems, local_sems):
        for cp in copies(outs, send_sems, recv_sems)[0]:
            cp.start()

    def finish(ins, outs, send_sems, recv_sems, local_sems):
        sends, recvs = copies(outs, send_sems, recv_sems)
        for cp in sends:
            cp.wait_send()
        for cp in recvs:
            cp.wait_recv()

    outs = [jax.ShapeDtypeStruct(g.shape, g.dtype) for g in gathered]
    return _Side(gathered, outs, (3 * n, 3 * n, 0), start, finish, aliases={t: t for t in range(n)})


def _alone(side, name):
    return _pallas(lambda: None, name=name, out_shape=[], sides=[side])[1][0]


def _all_gather(blocks, name):
    n = len(blocks)

    def body(*refs):
        in_refs, out_refs = refs[:n], refs[n:2 * n]
        send_sems, recv_sems, local_sems = refs[2 * n:]
        x, y, c = _place()
        me, sibling = (x, y, c), (x, y, 1 - c)
        chips = _other_chips(x, y)

        def slot(p):
            return 4 * p[0] + 2 * p[1] + p[2]

        def copy(t, k, block, to, src=None):
            dst = out_refs[t].at[slot(block)]
            return pltpu.make_async_remote_copy(
                src_ref=dst if src is None else src, dst_ref=dst,
                send_sem=send_sems.at[7 * t + k], recv_sem=recv_sems.at[7 * t + k],
                device_id=to, device_id_type=MESH)

        started = []
        for t in range(n):
            mine = pltpu.make_async_copy(in_refs[t], out_refs[t].at[slot(me)], local_sems.at[t])
            mine.start()
            started.append(mine)
        sends = []
        for t in range(n):
            first = [copy(t, 0, me, sibling, src=in_refs[t])]
            first += [copy(t, 1 + j, me, (*chip, c), src=in_refs[t]) for j, chip in enumerate(chips)]
            for cp in first:
                cp.start()
            sends += first
        for t in range(n):
            for j, chip in enumerate(chips):
                copy(t, 1 + j, (*chip, c), me).wait_recv()
                passed = copy(t, 4 + j, (*chip, c), sibling)
                passed.start()
                sends.append(passed)
        for t in range(n):
            copy(t, 0, sibling, me).wait_recv()
            for j, chip in enumerate(chips):
                copy(t, 4 + j, (*chip, 1 - c), me).wait_recv()
        for cp in sends:
            cp.wait_send()
        for mine in started:
            mine.wait()

    return pl.pallas_call(
        body,
        name=name,
        in_specs=[_any_spec()] * n,
        out_specs=[_any_spec()] * n,
        out_shape=[jax.ShapeDtypeStruct((N_DEV,) + b.shape, b.dtype) for b in blocks],
        scratch_shapes=[pltpu.SemaphoreType.DMA((7 * n,)), pltpu.SemaphoreType.DMA((7 * n,)),
                        pltpu.SemaphoreType.DMA((n,))],
    )(*blocks)


def _pair_sum(g8, got, name, tm=256):
    _, r, c = g8.shape
    tm = tm if r % tm == 0 else r
    core = lax.axis_index("c")

    def body(core_ref, mine_ref, got_ref, o_ref):
        south_first = core_ref[0] == 0
        a, b = mine_ref[...], got_ref[...]
        o_ref[...] = (jnp.where(south_first, a, b) + jnp.where(south_first, b, a)).astype(o_ref.dtype)

    return pl.pallas_call(
        body,
        name=name,
        grid_spec=pltpu.PrefetchScalarGridSpec(
            num_scalar_prefetch=1,
            grid=(N_CHIP, r // tm),
            in_specs=[pl.BlockSpec((None, tm, c), lambda j, i, core_ref: (2 * j + core_ref[0], i, 0)),
                      pl.BlockSpec((None, tm, c), lambda j, i, core_ref: (j, i, 0))],
            out_specs=pl.BlockSpec((None, tm, c), lambda j, i, core_ref: (j, i, 0)),
        ),
        out_shape=jax.ShapeDtypeStruct((N_CHIP, r, c), PAYLOAD),
        compiler_params=_cparams(("parallel", "parallel"), 32),
    )(jnp.reshape(core, (1,)).astype(jnp.int32), g8, got)


def _all_reduce_small(s, name="small_all_reduce"):
    r, c = s.shape

    def body(s_ref, o_ref, buf_ref, send_sems, recv_sems):
        x, y, c_ = _place()
        me = 4 * x + 2 * y + c_
        buf_ref[me] = s_ref[...]

        def copy(k, slot, peer):
            return pltpu.make_async_remote_copy(
                src_ref=s_ref, dst_ref=buf_ref.at[slot],
                send_sem=send_sems.at[k - 1], recv_sem=recv_sems.at[k - 1],
                device_id=peer, device_id_type=MESH)

        peers = []
        for k in range(1, N_DEV):
            peer = (x ^ ((k >> 2) & 1), y ^ ((k >> 1) & 1), c_ ^ (k & 1))
            peers.append(peer)
            copy(k, me, peer).start()
        for k, peer in zip(range(1, N_DEV), peers):
            cp = copy(k, 4 * peer[0] + 2 * peer[1] + peer[2], peer)
            cp.wait_send()
            cp.wait_recv()
        total = buf_ref[0]
        for j in range(1, N_DEV):
            total = total + buf_ref[j]
        o_ref[...] = total

    return pl.pallas_call(
        body,
        name=name,
        out_shape=jax.ShapeDtypeStruct((r, c), F32),
        in_specs=[pl.BlockSpec(memory_space=pltpu.VMEM)],
        out_specs=pl.BlockSpec(memory_space=pltpu.VMEM),
        scratch_shapes=[pltpu.VMEM((N_DEV, r, c), F32), pltpu.SemaphoreType.DMA((N_DEV - 1,)),
                        pltpu.SemaphoreType.DMA((N_DEV - 1,))],
    )(s)


def _late_weights(g_kv, g_b_in, g_b_out):
    d = g_kv.shape[1]
    nh = d // HEAD
    wkv = jnp.reshape(jnp.transpose(g_kv, (1, 0, 2)), (d, -1))
    return wkv[:, :2 * d], jnp.pad(wkv[:, 2 * d:], ((0, 0), (0, 128 - nh))), g_b_in, jnp.reshape(g_b_out, (d, d))


def _kv_grad_blocks(dwkv2, dwfl):
    d = dwkv2.shape[0]
    dwkv = jnp.concatenate([dwkv2, dwfl[:, :d // HEAD]], axis=1)
    return jnp.transpose(jnp.reshape(dwkv, (d, N_DEV, -1)), (1, 0, 2))


def _local_step(x, target, meta, gamma, a_norm, wa_in, a_out_norm, wa_out, kv_norm, late, b_f, g_k, b_norm, g_q,
                dist):
    d = x.shape[1]
    nh = d // HEAD
    cols = d // N_DEV
    h0 = jnp.concatenate([jnp.zeros((ROW_PAD, d), F32), meta, x], axis=0)
    big = dict(tm=1408, tn=512, tk=2048)

    lb = _lb_fwd(gamma)
    (hn_a,) = _rms_fwd(h0, [a_norm], "rms_a")
    u4 = _matmul(hn_a, wa_in, "nn", F32, "a_in", out_parts=4, **big)
    if dist:
        o_a, og_a, (gathered,) = _hgrn2_fwd(u4, lb, a_out_norm, sides=[_gather_own_side(list(late))])
        h1, (gathered,) = _matmul(og_a, wa_out, "nn", F32, "a_out", add=h0, sides=[_gather_pass_side(gathered)], **big)
        wkv2, wfl, wb_in, wb_out = _late_weights(*gathered)
    else:
        o_a, og_a = _hgrn2_fwd(u4, lb, a_out_norm)
        h1 = _matmul(og_a, wa_out, "nn", F32, "a_out", add=h0, **big)
        wkv2, wfl, wb_in, wb_out = late
    hk, hb = _rms_fwd(h1, [kv_norm, b_norm], "rms_kv_b")
    ukv = _matmul(hk, wkv2, "nn", F32, "kv_in", out_parts=2, **big)
    ufl = _matmul(hk, wfl, "nn", F32, "kv_f", **big)
    ub = _matmul(hb, wb_in, "nn", F32, "b_in", out_parts=2, **big)
    f_cum = _fgate_fwd(ufl, b_f)
    qa, ka, va = _attn_operands(ub, ukv, g_q, g_k, f_cum)
    o_b, o_lo, og_b, qb = _fox_fwd(qa, ka, va, ub)
    h2 = _matmul(og_b, wb_out, "nn", F32, "b_out", add=h1, **big)
    loss, dh2 = _loss(h2, target)

    dx_t = dict(tm=1408, tn=512, tk=1024)
    dw_t = dict(tm=1024, tn=1024, tk=1408, vmem_mib=56)
    def to_sibling(g8):
        return [_sibling_side([g8])] if dist else []

    def to_chips(partial):
        return [_chips_side([partial])] if dist else []

    def unpack(res, n_sides):
        if not dist:
            return res, [None] * n_sides
        *main, side_results = res
        return (main[0] if len(main) == 1 else tuple(main)), [r[0] for r in side_results]

    dwb_out = _matmul(og_b, dh2, "tn", F32, "b_out_dw", **dw_t)
    g8_b_out = jnp.reshape(dwb_out, (N_DEV, cols, d))
    d_ogb, (got,) = unpack(_matmul(dh2, wb_out, "nt", F32, "b_out_dx", sides=to_sibling(g8_b_out), **dx_t), 1)
    p_b_out = _pair_sum(g8_b_out, got, "pair_sum_b_w_out") if dist else None
    doa = _fox_do(d_ogb, o_b, o_lo, ub)
    (dqn, dkn, dv, dfk), (r_b_out,) = unpack(_fox_bwd(qb, ka, va, doa, d, sides=to_chips(p_b_out)), 1)
    dub, dg_q = _fox_q_bwd(ub, d_ogb, o_b, dqn, g_q)
    dukv, dg_k = _fox_k_bwd(ukv, dkn, dv, g_k)
    d_f = jnp.pad(jnp.transpose(dfk[:, 0, :]), ((0, 0), (0, 128 - nh)))
    dufl, db_f = _fgate_bwd(ufl, b_f, d_f)
    dwb_in = _matmul(hb, dub, "tn", F32, "b_in_dw", out_parts=N_DEV, **dw_t)
    d_hb, (got,) = unpack(_matmul(dub, wb_in, "nt", F32, "b_in_dx", sides=to_sibling(dwb_in), **dx_t), 1)
    p_b_in = _pair_sum(dwb_in, got, "pair_sum_b_w_in") if dist else None
    d_hk, (r_b_in,) = unpack(_matmul(dukv, wkv2, "nt", F32, "kv_dx", sides=to_chips(p_b_in), **dx_t), 1)
    d_hk = _matmul(dufl, wfl, "nt", F32, "kv_f_dx", add=d_hk, **dx_t)
    dwkv2 = _matmul(hk, dukv, "tn", F32, "kv_dw", **dw_t)
    dwfl = _matmul(hk, dufl, "tn", F32, "kv_f_dw", **dw_t)
    g8_kv = _kv_grad_blocks(dwkv2, dwfl) if dist else None
    dh1, (dg_kv, dg_b) = _rms_bwd(h1, [kv_norm, b_norm], [d_hk, d_hb], dh2, "rms_kv_b_bwd")
    d_oga, (got,) = unpack(_matmul(dh1, wa_out, "nt", F32, "a_out_dx", sides=to_sibling(g8_kv), **dx_t), 1)
    p_kv = _pair_sum(g8_kv, got, "pair_sum_kv_w") if dist else None
    dwa_out = _matmul(og_a, dh1, "tn", F32, "a_out_dw", **dw_t)
    g8_a_out = jnp.reshape(dwa_out, (N_DEV, cols, d))
    (du4, dlb, dg_aout), (r_kv, got) = unpack(
        _hgrn2_bwd(u4, o_a, d_oga, lb, a_out_norm, sides=to_chips(p_kv) + to_sibling(g8_a_out)), 2)
    p_a_out = _pair_sum(g8_a_out, got, "pair_sum_a_w_out") if dist else None
    dwa_in, (r_a_out,) = unpack(
        _matmul(hn_a, du4, "tn", F32, "a_in_dw", out_parts=N_DEV, sides=to_chips(p_a_out), **dw_t), 1)
    d_hna, (got,) = unpack(_matmul(du4, wa_in, "nt", F32, "a_in_dx", sides=to_sibling(dwa_in), **dx_t), 1)
    p_a_in = _pair_sum(dwa_in, got, "pair_sum_a_w_in") if dist else None
    dh0, (dg_a,) = _rms_bwd(h0, [a_norm], [d_hna], dh1, "rms_a_bwd")
    dgamma = _lb_bwd(gamma, dlb)

    grads = dict(meta=dh0[ROW_PAD:ROW_PAD + N_META], gamma=dgamma, a_norm=dg_a, a_out_norm=dg_aout, kv_norm=dg_kv,
                 b_f=db_f, g_k=dg_k, b_norm=dg_b, g_q=dg_q)
    if dist:
        (r_a_in,) = _alone(_chips_side([p_a_in]), "grads_to_chips_a_w_in")
        grads.update(wa_in=r_a_in, wa_out=r_a_out, wkv=r_kv, wb_in=r_b_in, wb_out=r_b_out)
    else:
        grads.update(wa_in=dwa_in, wa_out=dwa_out, wkv2=dwkv2, wfl=dwfl, wb_in=dwb_in, wb_out=dwb_out)
    return loss, dh0[ROW_PAD + N_META:], grads


def kernel(x, meta, gamma_lb, a_norm, a_w_in, a_out_norm, a_w_out, kv_norm, kv_w, fox_b_f, fox_k_norm, b_norm, b_w_in, b_q_norm, b_w_out, loss_target, m_meta, m_gamma_lb, m_a_norm, m_a_w_in, m_a_out_norm, m_a_w_out, m_kv_norm, m_kv_w, m_fox_b_f, m_fox_k_norm, m_b_norm, m_b_w_in, m_b_q_norm, m_b_w_out, v_meta, v_gamma_lb, v_a_norm, v_a_w_in, v_a_out_norm, v_a_w_out, v_kv_norm, v_kv_w, v_fox_b_f, v_fox_k_norm, v_b_norm, v_b_w_in, v_b_q_norm, v_b_w_out):
    d = x.shape[-1]
    nh = d // HEAD
    cols = d // N_DEV
    me = 4 * lax.axis_index("x") + 2 * lax.axis_index("y") + lax.axis_index("c")

    sharded_small = jnp.concatenate([meta, gamma_lb, a_norm, a_out_norm, jnp.zeros((4, cols), F32)], axis=0)
    g_a_in, g_a_out, g_small = _all_gather(
        [a_w_in[0].astype(MXU), a_w_out[0].astype(MXU), sharded_small], "gather_weights")
    wa_out = jnp.reshape(g_a_out, (d, d))
    late = (kv_w.astype(MXU), b_w_in[0].astype(MXU), b_w_out[0].astype(MXU))
    small = jnp.reshape(jnp.transpose(g_small, (1, 0, 2)), (-1, d))
    meta_f, gamma_f, a_norm_f, a_out_norm_f = small[:16], small[16:18], small[18:19], small[19:20]
    b_f = jnp.pad(jnp.reshape(fox_b_f, (1, nh)), ((0, 0), (0, 128 - nh)))
    g_k = jnp.reshape(fox_k_norm, (1, d))
    g_q = jnp.reshape(b_q_norm, (1, d))
    kv_norm_r = jnp.reshape(kv_norm, (1, d))

    loss, grad_x, g = _local_step(x[0], loss_target[0], meta_f, gamma_f, a_norm_f, g_a_in, a_out_norm_f, wa_out,
                                  kv_norm_r, late, b_f, g_k, b_norm, g_q, dist=True)
    loss = lax.psum(loss[0, 0], AXES)

    r_a_in = _adamw(g["wa_in"], a_w_in[0], m_a_w_in[0], v_a_w_in[0], "adamw_a_w_in")
    r_a_out = _adamw(g["wa_out"], a_w_out[0], m_a_w_out[0], v_a_w_out[0], "adamw_a_w_out")
    r_kv = _adamw(g["wkv"], kv_w, m_kv_w, v_kv_w, "adamw_kv_w")
    r_b_in = _adamw(g["wb_in"], b_w_in[0], m_b_w_in[0], v_b_w_in[0], "adamw_b_w_in")
    r_b_out = _adamw(g["wb_out"], b_w_out[0], m_b_w_out[0], v_b_w_out[0], "adamw_b_w_out")

    packed = jnp.concatenate(
        [g["meta"], g["gamma"], g["a_norm"], g["a_out_norm"], g["kv_norm"], g["b_norm"], g["g_k"], g["g_q"],
         jnp.pad(g["b_f"], ((0, 0), (0, d - 128))), jnp.zeros((7, d), F32)], axis=0)
    tot = _all_reduce_small(packed)
    mine = lax.dynamic_slice_in_dim(tot[:20], me * cols, cols, axis=1)
    gs = [mine[:16], mine[16:18], mine[18:19], mine[19:20], tot[20], tot[21:22], jnp.reshape(tot[22], (nh, HEAD)),
          jnp.reshape(tot[23], (1, nh, HEAD)), tot[24, :nh]]
    small_w = [meta, gamma_lb, a_norm, a_out_norm, kv_norm, b_norm, fox_k_norm, b_q_norm, fox_b_f]
    small_m = [m_meta, m_gamma_lb, m_a_norm, m_a_out_norm, m_kv_norm, m_b_norm, m_fox_k_norm, m_b_q_norm, m_fox_b_f]
    small_v = [v_meta, v_gamma_lb, v_a_norm, v_a_out_norm, v_kv_norm, v_b_norm, v_fox_k_norm, v_b_q_norm, v_fox_b_f]

    def pack(arrs):
        rows = [jnp.reshape(a, (-1, cols)) for a in arrs[:8]]
        rows.append(jnp.pad(jnp.reshape(arrs[8], (1, nh)), ((0, 0), (0, cols - nh))))
        n_rows = sum(r.shape[0] for r in rows)
        return jnp.concatenate(rows + [jnp.zeros((-n_rows % 8, cols), F32)], axis=0)

    _, sd, sm, sv = _adamw(pack(gs)[None], pack(small_w), pack(small_m), pack(small_v), "adamw_small")

    def unpack(p):
        out, at = [], 0
        for a in small_w[:8]:
            n = a.size // cols
            out.append(jnp.reshape(p[at:at + n], a.shape))
            at += n
        out.append(p[at, :nh])
        return out

    d_s, m_s, v_s = unpack(sd), unpack(sm), unpack(sv)

    def ordered(s, a_in, a_out, kv, b_in, b_out):
        return [s[0], s[1], s[2], a_in[None], s[3], a_out[None], s[4], kv, s[8], s[6], s[5], b_in[None], s[7], b_out[None]]

    outs = []
    for i, s in enumerate([gs, d_s, m_s, v_s]):
        outs += ordered(s, r_a_in[i], r_a_out[i], r_kv[i], r_b_in[i], r_b_out[i])
    return (loss, grad_x[None], *outs)
```

```python
import math

import jax
import jax.numpy as jnp
from jax import lax
from jax.experimental import pallas as pl
from jax.experimental.pallas import tpu as pltpu

HEAD = 128
CHUNK = 64
N_META = 16
ROW_PAD = 128 - N_META
EPS = 1e-6
MASK_VALUE = -1e30
ADAM_LR = 0.001
ADAM_B1 = 0.9
ADAM_B2 = 0.999
ADAM_EPS = 1e-08
ADAM_WD = 0.01
ADAM_STEP = 10
N_DEV = 8
N_CHIP = 4
MIB = 1024 * 1024
AXES = ("x", "y", "c")
MESH = pl.DeviceIdType.MESH

F32 = jnp.float32
MXU = jnp.bfloat16
PAYLOAD = jnp.bfloat16
HI = lax.Precision.HIGHEST

NN = (((1,), (0,)), ((), ()))
NT = (((1,), (1,)), ((), ()))
TN = (((0,), (0,)), ((), ()))


def _dot(a, b, dims=NN):
    return lax.dot_general(a.astype(MXU), b.astype(MXU), dims, preferred_element_type=F32)


def _dot_exact(a, b):
    return lax.dot_general(a, b, NN, precision=HI, preferred_element_type=F32)


def _sigmoid(x):
    return 1.0 / (1.0 + jnp.exp(-x))


def _tile(dim, target, unit=128):
    best = None
    t = unit
    while t <= min(dim, target):
        if dim % t == 0:
            best = t
        t += unit
    return best if best is not None else dim


def _cparams(semantics, vmem_mib):
    return pltpu.CompilerParams(dimension_semantics=semantics, vmem_limit_bytes=vmem_mib * MIB)


def _mat_spec(arr, br, bc, rc_of_grid):
    if arr.ndim == 2:
        return pl.BlockSpec((br, bc), rc_of_grid)
    assert arr.shape[2] % bc == 0, (arr.shape, bc)
    per = arr.shape[2] // bc

    def idx(*g):
        r, c = rc_of_grid(*g)
        return (c // per, r, c % per)

    return pl.BlockSpec((None, br, bc), idx)


def _mat_shape(arr):
    return (arr.shape[0], arr.shape[1]) if arr.ndim == 2 else (arr.shape[1], arr.shape[0] * arr.shape[2])


def _matmul(a, b, dims, out_dtype, name, *, add=None, out_parts=1, tm=512, tn=512, tk=512, vmem_mib=48, sides=(),
            k_whole=False):
    ar, ac = _mat_shape(a)
    br_, bc_ = _mat_shape(b)
    if dims == "nn":
        m, k, n = ar, ac, bc_
        assert br_ == k
    elif dims == "nt":
        m, k, n = ar, ac, br_
        assert bc_ == k
    else:
        m, k, n = ac, ar, bc_
        assert br_ == k
    m_unit, n_unit, k_unit = m, n, k
    if a.ndim == 3:
        if dims == "tn":
            m_unit = math.gcd(m_unit, a.shape[2])
        else:
            k_unit = math.gcd(k_unit, a.shape[2])
    if b.ndim == 3:
        if dims == "nt":
            k_unit = math.gcd(k_unit, b.shape[2])
        else:
            n_unit = math.gcd(n_unit, b.shape[2])
    if out_parts > 1:
        n_unit = math.gcd(n_unit, n // out_parts)
    tm = _tile(m_unit, tm, 128 if dims == "tn" else 64)
    tn, tk = _tile(n_unit, tn), _tile(k_unit, tk)
    whole_k = dims == "nt" and tk < k and k_whole
    if whole_k:
        k_chunk, tk = tk, k
    gm, gn, gk = m // tm, n // tn, k // tk
    assert gm * tm == m and gn * tn == n and gk * tk == k, (name, m, n, k, tm, tn, tk)

    def chunk_of(ref, arr, c):
        if arr.ndim == 2:
            return ref[:, c * k_chunk:(c + 1) * k_chunk]
        per = arr.shape[2] // k_chunk
        return ref[c // per, :, (c % per) * k_chunk:(c % per + 1) * k_chunk]

    def all_cols(arr, rows, row_of_grid):
        if arr.ndim == 2:
            return pl.BlockSpec((rows, arr.shape[1]), lambda i, j, kk: (row_of_grid(i, j), 0))
        return pl.BlockSpec((arr.shape[0], rows, arr.shape[2]), lambda i, j, kk: (0, row_of_grid(i, j), 0))

    if dims == "nn":
        a_spec = _mat_spec(a, tm, tk, lambda i, j, kk: (i, kk))
        b_spec = _mat_spec(b, tk, tn, lambda i, j, kk: (kk, j))
        dn = NN
    elif whole_k:
        a_spec = all_cols(a, tm, lambda i, j: i)
        b_spec = all_cols(b, tn, lambda i, j: j)
        dn = NT
    elif dims == "nt":
        a_spec = _mat_spec(a, tm, tk, lambda i, j, kk: (i, kk))
        b_spec = _mat_spec(b, tn, tk, lambda i, j, kk: (j, kk))
        dn = NT
    else:
        a_spec = _mat_spec(a, tk, tm, lambda i, j, kk: (kk, i))
        b_spec = _mat_spec(b, tk, tn, lambda i, j, kk: (kk, j))
        dn = TN

    if out_parts > 1:
        per = (n // out_parts) // tn
        out_shape = jax.ShapeDtypeStruct((out_parts, m, n // out_parts), out_dtype)
        o_spec = pl.BlockSpec((None, tm, tn), lambda i, j, kk: (j // per, i, j % per))
    else:
        out_shape = jax.ShapeDtypeStruct((m, n), out_dtype)
        o_spec = pl.BlockSpec((tm, tn), lambda i, j, kk: (i, j))

    in_specs = [a_spec, b_spec]
    args = [a, b]
    if add is not None:
        in_specs.append(pl.BlockSpec((tm, tn), lambda i, j, kk: (i, j)))
        args.append(add)

    def body(*refs):
        if add is not None:
            a_ref, b_ref, add_ref, o_ref, acc_ref = refs
        else:
            a_ref, b_ref, o_ref, acc_ref = refs
            add_ref = None
        kk = pl.program_id(2)
        if whole_k:
            part = sum(lax.dot_general(chunk_of(a_ref, a, c).astype(MXU), chunk_of(b_ref, b, c).astype(MXU), dn,
                                       preferred_element_type=F32) for c in range(k // k_chunk))
        else:
            part = lax.dot_general(a_ref[...].astype(MXU), b_ref[...].astype(MXU), dn, preferred_element_type=F32)

        def finish(total):
            if add_ref is not None:
                total = total + add_ref[...]
            o_ref[...] = total.astype(o_ref.dtype)

        if gk == 1:
            finish(part)
        else:
            @pl.when(kk == 0)
            def _():
                acc_ref[...] = part

            @pl.when(jnp.logical_and(kk > 0, kk < gk - 1))
            def _():
                acc_ref[...] += part

            @pl.when(kk == gk - 1)
            def _():
                finish(acc_ref[...] + part)

    (out,), side_results = _pallas(
        body,
        name=name,
        args=args,
        grid=(gm, gn, gk),
        in_specs=in_specs,
        out_specs=[o_spec],
        out_shape=[out_shape],
        scratch_shapes=[pltpu.VMEM((tm, tn) if gk > 1 else (8, 128), F32)],
        semantics=("parallel", "parallel", "arbitrary"),
        vmem_mib=vmem_mib,
        sides=sides,
    )
    return (out, side_results) if sides else out


def _rms_fwd(h, gains, name, tm=384):
    lp, d = h.shape
    tm = _tile(lp, tm)
    n = len(gains)

    def body(*refs):
        h_ref = refs[0]
        g_refs = refs[1:1 + n]
        o_refs = refs[1 + n:]
        x = h_ref[...]
        y = x * lax.rsqrt(jnp.mean(x * x, axis=-1, keepdims=True) + EPS)
        for g_ref, o_ref in zip(g_refs, o_refs):
            o_ref[...] = (y * g_ref[...]).astype(o_ref.dtype)

    row = pl.BlockSpec((tm, d), lambda i: (i, 0))
    vec = pl.BlockSpec((1, d), lambda i: (0, 0))
    return pl.pallas_call(
        body,
        name=name,
        grid=(lp // tm,),
        in_specs=[row] + [vec] * n,
        out_specs=[row] * n,
        out_shape=[jax.ShapeDtypeStruct((lp, d), MXU)] * n,
        compiler_params=_cparams(("parallel",), 40),
    )(h, *gains)


def _rms_bwd(h, gains, dys, res, name, tm=384):
    lp, d = h.shape
    tm = _tile(lp, tm)
    n = len(gains)

    def body(*refs):
        h_ref, res_ref = refs[0], refs[1]
        g_refs = refs[2:2 + n]
        dy_refs = refs[2 + n:2 + 2 * n]
        dh_ref = refs[2 + 2 * n]
        dg_refs = refs[3 + 2 * n:]
        i = pl.program_id(0)
        x = h_ref[...]
        rstd = lax.rsqrt(jnp.mean(x * x, axis=-1, keepdims=True) + EPS)
        xhat = x * rstd
        dh = res_ref[...]
        for g_ref, dy_ref, dg_ref in zip(g_refs, dy_refs, dg_refs):
            dy = dy_ref[...]
            gdy = dy * g_ref[...]
            dh = dh + rstd * (gdy - xhat * jnp.mean(gdy * xhat, axis=-1, keepdims=True))
            part = jnp.sum(dy * xhat, axis=0, keepdims=True)

            @pl.when(i == 0)
            def _():
                dg_ref[...] = part

            @pl.when(i > 0)
            def _():
                dg_ref[...] += part

        dh_ref[...] = dh

    row = pl.BlockSpec((tm, d), lambda i: (i, 0))
    vec = pl.BlockSpec((1, d), lambda i: (0, 0))
    outs = pl.pallas_call(
        body,
        name=name,
        grid=(lp // tm,),
        in_specs=[row, row] + [vec] * n + [row] * n,
        out_specs=[row] + [vec] * n,
        out_shape=[jax.ShapeDtypeStruct((lp, d), F32)] + [jax.ShapeDtypeStruct((1, d), F32)] * n,
        compiler_params=_cparams(("arbitrary",), 56),
    )(h, res, *gains, *dys)
    return outs[0], list(outs[1:])


def _lb_fwd(gamma):
    def body(g_ref, lb_ref):
        g = g_ref[...]
        e = jnp.exp(g - jnp.max(g, axis=0, keepdims=True))
        lb_ref[...] = (e / jnp.sum(e, axis=0, keepdims=True))[0:1, :]

    return pl.pallas_call(body, name="lb_fwd", out_shape=jax.ShapeDtypeStruct((1, gamma.shape[1]), F32))(gamma)


def _lb_bwd(gamma, dlb):
    def body(g_ref, dlb_ref, dg_ref):
        g = g_ref[...]
        e = jnp.exp(g - jnp.max(g, axis=0, keepdims=True))
        s = e / jnp.sum(e, axis=0, keepdims=True)
        rows = lax.broadcasted_iota(jnp.int32, g.shape, 0)
        ds = jnp.where(rows == 0, dlb_ref[...], 0.0)
        dg_ref[...] = s * (ds - jnp.sum(s * ds, axis=0, keepdims=True))

    return pl.pallas_call(body, name="lb_bwd", out_shape=jax.ShapeDtypeStruct(gamma.shape, F32))(gamma, dlb)


def _tri(n, lower):
    r = lax.broadcasted_iota(jnp.int32, (n, n), 0)
    c = lax.broadcasted_iota(jnp.int32, (n, n), 1)
    return jnp.where((r >= c) if lower else (r <= c), 1.0, 0.0).astype(F32)


def _group(nc):
    return max(u for u in (1, 2, 3, 6) if nc % u == 0)


def _running_sum(tri, x):
    hi = x.astype(MXU)
    rest = x - hi.astype(F32)
    mid = rest.astype(MXU)
    lo = (rest - mid.astype(F32)).astype(MXU)
    return _dot(tri, hi) + _dot(tri, mid) + _dot(tri, lo)


def _causal(n):
    r = lax.broadcasted_iota(jnp.int32, (n, n), 0)
    c = lax.broadcasted_iota(jnp.int32, (n, n), 1)
    return r >= c


def _chunk_gates(u_ref, lb, c):
    sl = pl.ds(pl.multiple_of(c * CHUNK, CHUNK), CHUNK)
    valid = (c * CHUNK + lax.broadcasted_iota(jnp.int32, (CHUNK, HEAD), 0)) >= ROW_PAD
    uq = u_ref[0, sl, :]
    uf = u_ref[1, sl, :]
    sq = _sigmoid(uq)
    sf = _sigmoid(uf)
    fg = lb + (1.0 - lb) * sf
    return dict(sl=sl, valid=valid, uq=uq, sq=sq, sf=sf, fg=fg, q=jnp.where(valid, uq * sq, 0.0),
                logf=jnp.where(valid, jnp.log(fg), 0.0), k=jnp.where(valid, 1.0 - fg, 0.0),
                v=jnp.where(valid, u_ref[2, sl, :], 0.0))


def _chunk_decays(x, b):
    b_last = b[CHUNK - 1:CHUNK, :]
    b_mid = b[CHUNK // 2 - 1:CHUNK // 2, :]
    e_qi = jnp.exp(b - b_mid)
    e_ki = jnp.exp(b_mid - b)
    e_kd = jnp.exp(b_last - b)
    e_qe = jnp.exp(b)
    q, k = x["q"], x["k"]
    return dict(x, e_qi=e_qi, e_ki=e_ki, e_kd=e_kd, e_qe=e_qe, qi=q * e_qi, ki=k * e_ki, kd=k * e_kd, qe=q * e_qe,
                decay=jnp.exp(b_last))


def _chunks(u_ref, lb, tri_lower, cs):
    gates = [_chunk_gates(u_ref, lb, c) for c in cs]
    sums = [_running_sum(tri_lower, x["logf"]) for x in gates]
    return [_chunk_decays(x, b) for x, b in zip(gates, sums)]


def _hgrn2_fwd(u4, lb, g_out, name="hgrn2_fwd", sides=()):
    _, lp, d = u4.shape
    nh, nc = d // HEAD, lp // CHUNK
    per = _group(nc)

    def body(u_ref, lb_ref, g_ref, o_ref, og_ref):
        lb_v = lb_ref[...]
        g = g_ref[...]

        tri_lower = _tri(CHUNK, True).astype(MXU)
        causal = _causal(CHUNK)

        def step(i, st):
            xs = _chunks(u_ref, lb_v, tri_lower, [i * per + u for u in range(per)])
            scores = [_dot(x["qi"], x["ki"], NT) for x in xs]
            updates = [_dot(x["v"], x["kd"], TN) for x in xs]
            states = []
            for x, upd in zip(xs, updates):
                states.append(st)
                st = x["decay"] * st + upd
            outs = [_dot(jnp.where(causal, a, 0.0), x["v"]) + _dot(x["qe"], s, NT)
                    for x, a, s in zip(xs, scores, states)]
            for x, o in zip(xs, outs):
                o_ref[x["sl"], :] = o
                on = o * lax.rsqrt(jnp.mean(o * o, axis=-1, keepdims=True) + EPS) * g
                z = u_ref[3, x["sl"], :]
                og_ref[x["sl"], :] = (on * (z * _sigmoid(z))).astype(og_ref.dtype)
            return st

        lax.fori_loop(0, nc // per, step, jnp.zeros((HEAD, HEAD), F32))

    slab = pl.BlockSpec((lp, HEAD), lambda h: (0, h))
    vec = pl.BlockSpec((1, HEAD), lambda h: (0, h))
    outs, side_results = _pallas(
        body,
        name=name,
        args=(u4, lb, g_out),
        grid=(nh,),
        in_specs=[pl.BlockSpec((4, lp, HEAD), lambda h: (0, 0, h)), vec, vec],
        out_specs=[slab, slab],
        out_shape=[jax.ShapeDtypeStruct((lp, d), F32), jax.ShapeDtypeStruct((lp, d), MXU)],
        semantics=("parallel",),
        vmem_mib=48,
        sides=sides,
    )
    return (*outs, side_results) if sides else tuple(outs)


def _hgrn2_bwd(u4, o, d_og, lb, g_out, name="hgrn2_bwd", sides=()):
    _, lp, d = u4.shape
    nh, nc = d // HEAD, lp // CHUNK
    per = _group(nc)

    def body(u_ref, o_ref, dog_ref, lb_ref, g_ref, du_ref, dlb_ref, dg_ref, st_ref, do_ref):
        lb_v = lb_ref[...]
        g = g_ref[...]

        tri_lower = _tri(CHUNK, True).astype(MXU)
        tri_upper = _tri(CHUNK, False).astype(MXU)
        causal = _causal(CHUNK)

        def fwd_step(i, carry):
            st, dg_acc = carry
            cs = [i * per + u for u in range(per)]
            xs = _chunks(u_ref, lb_v, tri_lower, cs)
            updates = [_dot(x["v"], x["kd"], TN) for x in xs]
            for c, x, upd in zip(cs, xs, updates):
                st_ref[c] = st
                st = x["decay"] * st + upd
            for x in xs:
                sl = x["sl"]
                ov = o_ref[sl, :]
                rstd = lax.rsqrt(jnp.mean(ov * ov, axis=-1, keepdims=True) + EPS)
                on = ov * rstd
                z = u_ref[3, sl, :]
                sz = _sigmoid(z)
                dog = dog_ref[sl, :]
                dy = dog * (z * sz)
                dz = dog * (on * g) * (sz * (1.0 + z * (1.0 - sz)))
                du_ref[3, sl, :] = dz.astype(du_ref.dtype)
                gdy = dy * g
                do = rstd * (gdy - on * jnp.mean(gdy * on, axis=-1, keepdims=True))
                do_ref[sl, :] = jnp.where(x["valid"], do, 0.0)
                dg_acc = dg_acc + jnp.sum(dy * on, axis=0, keepdims=True)
            return st, dg_acc

        _, dg_tot = lax.fori_loop(0, nc // per, fwd_step, (jnp.zeros((HEAD, HEAD), F32), jnp.zeros((1, HEAD), F32)))
        dg_ref[...] = dg_tot

        def bwd_step(i, carry):
            gt, dlb_acc = carry
            cs = [nc - 1 - (i * per + u) for u in range(per)]
            xs = _chunks(u_ref, lb_v, tri_lower, cs)
            dos = [do_ref[x["sl"], :] for x in xs]
            sts = [st_ref[c] for c in cs]
            scores = [jnp.where(causal, _dot(x["qi"], x["ki"], NT), 0.0) for x in xs]
            d_scores = [jnp.where(causal, _dot(do, x["v"], NT), 0.0) for x, do in zip(xs, dos)]
            d_qes = [_dot(do, st) for do, st in zip(dos, sts)]
            g_updates = [_dot(do, x["qe"], TN) for x, do in zip(xs, dos)]
            gts = []
            for x, upd in zip(xs, g_updates):
                gts.append(gt)
                gt = x["decay"] * gt + upd
            d_kds = [_dot(x["v"], g_) for x, g_ in zip(xs, gts)]
            dvs = [_dot(x["kd"], g_, NT) + _dot(a, do, TN) for x, g_, a, do in zip(xs, gts, scores, dos)]
            d_qis = [_dot(d_a, x["ki"]) for x, d_a in zip(xs, d_scores)]
            d_kis = [_dot(d_a, x["qi"], TN) for x, d_a in zip(xs, d_scores)]
            rows = lax.broadcasted_iota(jnp.int32, (CHUNK, HEAD), 0)
            dbs = []
            for x, g_, st, d_qi, d_ki, d_qe, d_kd in zip(xs, gts, sts, d_qis, d_kis, d_qes, d_kds):
                t_qi, t_ki, t_qe, t_kd = d_qi * x["qi"], d_ki * x["ki"], d_qe * x["qe"], d_kd * x["kd"]
                d_decay = jnp.sum(g_ * st, axis=0, keepdims=True)
                d_mid = jnp.sum(t_ki - t_qi, axis=0, keepdims=True)
                d_last = jnp.sum(t_kd, axis=0, keepdims=True) + d_decay * x["decay"]
                dbs.append(t_qi - t_ki + t_qe - t_kd + jnp.where(rows == CHUNK // 2 - 1, d_mid, 0.0)
                           + jnp.where(rows == CHUNK - 1, d_last, 0.0))
            dlogfs = [_running_sum(tri_upper, db) for db in dbs]
            for x, dlogf, dv, d_qi, d_ki, d_qe, d_kd in zip(xs, dlogfs, dvs, d_qis, d_kis, d_qes, d_kds):
                sl = x["sl"]
                dq = d_qi * x["e_qi"] + d_qe * x["e_qe"]
                dk = d_ki * x["e_ki"] + d_kd * x["e_kd"]
                valid, sq, sf, uq = x["valid"], x["sq"], x["sf"], x["uq"]
                dfg = jnp.where(valid, dlogf / x["fg"] - dk, 0.0)
                du_ref[0, sl, :] = jnp.where(valid, dq * (sq * (1.0 + uq * (1.0 - sq))), 0.0).astype(du_ref.dtype)
                du_ref[1, sl, :] = (dfg * (1.0 - lb_v) * (sf * (1.0 - sf))).astype(du_ref.dtype)
                du_ref[2, sl, :] = jnp.where(valid, dv, 0.0).astype(du_ref.dtype)
                dlb_acc = dlb_acc + jnp.sum(dfg * (1.0 - sf), axis=0, keepdims=True)
            return gt, dlb_acc

        _, dlb_tot = lax.fori_loop(0, nc // per, bwd_step, (jnp.zeros((HEAD, HEAD), F32), jnp.zeros((1, HEAD), F32)))
        dlb_ref[...] = dlb_tot

    slab = pl.BlockSpec((lp, HEAD), lambda h: (0, h))
    vec = pl.BlockSpec((1, HEAD), lambda h: (0, h))
    quad = pl.BlockSpec((4, lp, HEAD), lambda h: (0, 0, h))
    outs, side_results = _pallas(
        body,
        name=name,
        args=(u4, o, d_og, lb, g_out),
        grid=(nh,),
        in_specs=[quad, slab, slab, vec, vec],
        out_specs=[quad, vec, vec],
        out_shape=[jax.ShapeDtypeStruct((4, lp, d), MXU), jax.ShapeDtypeStruct((1, d), F32),
                   jax.ShapeDtypeStruct((1, d), F32)],
        scratch_shapes=[pltpu.VMEM((nc, HEAD, HEAD), F32), pltpu.VMEM((lp, HEAD), F32)],
        semantics=("parallel",),
        vmem_mib=58,
        sides=sides,
    )
    return (*outs, side_results) if sides else tuple(outs)


WIDE = 2 * HEAD
INV_SCALE = HEAD ** 0.5


def _split3(x):
    hi = x.astype(MXU).astype(F32)
    rest = x - hi
    mid = rest.astype(MXU).astype(F32)
    return hi, mid, (rest - mid).astype(MXU).astype(F32)


def _extra_cols(rows, first, second):
    lane = lax.broadcasted_iota(jnp.int32, (rows, HEAD), 1)
    out = jnp.where(lane < 6, 1.0, 0.0).astype(F32)
    for base, terms in ((0, first), (3, second)):
        if terms is not None:
            for j, term in enumerate(terms):
                out = jnp.where(lane == base + j, term, out)
    return out


def _head_col(a, h):
    lane = lax.broadcasted_iota(jnp.int32, a.shape, 1)
    return jnp.sum(jnp.where(lane == h, a, 0.0), axis=-1, keepdims=True)


def _attn_operands(ub, ukv, g_q, g_k, f_cum, name="attn_operands", tm=384):
    _, lp, d = ub.shape
    tm = _tile(lp, tm)
    nh = d // HEAD

    def body(q_ref, k_ref, v_ref, gq_ref, gk_ref, f_ref, qa_ref, ka_ref, va_ref):
        i = pl.program_id(0)
        f = f_ref[...]
        is_pad = (i * tm + lax.broadcasted_iota(jnp.int32, (tm, 1), 0)) < ROW_PAD
        ones_only = _extra_cols(tm, None, (0.0, 0.0, 0.0)).astype(MXU)
        for h in range(nh):
            hs = slice(h * HEAD, (h + 1) * HEAD)
            lo, hi = h * WIDE, h * WIDE + HEAD
            f_h = _head_col(f, h)
            for x_ref, g_ref, o_ref in ((q_ref, gq_ref, qa_ref), (k_ref, gk_ref, ka_ref)):
                x = x_ref[:, hs]
                y = x * lax.rsqrt(jnp.mean(x * x, axis=-1, keepdims=True) + EPS)
                o_ref[:, lo:hi] = (y * g_ref[:, hs]).astype(o_ref.dtype)
            qa_ref[:, hi:hi + HEAD] = _extra_cols(tm, _split3(f_h * INV_SCALE), None).astype(MXU)
            f_key = jnp.where(is_pad, -MASK_VALUE, f_h)
            ka_ref[:, hi:hi + HEAD] = _extra_cols(tm, None, _split3(-f_key * INV_SCALE)).astype(MXU)
            va_ref[:, lo:hi] = v_ref[:, hs].astype(MXU)
            va_ref[:, hi:hi + HEAD] = ones_only

    row = pl.BlockSpec((tm, d), lambda i: (i, 0))
    wide = pl.BlockSpec((tm, nh * WIDE), lambda i: (i, 0))
    vec = pl.BlockSpec((1, d), lambda i: (0, 0))
    return pl.pallas_call(
        body,
        name=name,
        grid=(lp // tm,),
        in_specs=[pl.BlockSpec((None, tm, d), lambda i: (0, i, 0)), pl.BlockSpec((None, tm, d), lambda i: (0, i, 0)),
                  pl.BlockSpec((None, tm, d), lambda i: (1, i, 0)), vec, vec, pl.BlockSpec((tm, 128), lambda i: (i, 0))],
        out_specs=[wide, wide, wide],
        out_shape=[jax.ShapeDtypeStruct((lp, nh * WIDE), MXU)] * 3,
        compiler_params=_cparams(("parallel",), 56),
    )(ub, ukv, ukv, g_q, g_k, f_cum)


def _head_rms_bwd_tile(x, g, dy):
    rstd = lax.rsqrt(jnp.mean(x * x, axis=-1, keepdims=True) + EPS)
    xhat = x * rstd
    gdy = dy * g
    dx = rstd * (gdy - xhat * jnp.mean(gdy * xhat, axis=-1, keepdims=True))
    return dx, jnp.sum(dy * xhat, axis=0, keepdims=True)


def _fgate_fwd(ufl, b_f):
    lp = ufl.shape[0]
    nb = lp // 128

    def body(u_ref, b_ref, f_ref):
        def step(i, carry):
            sl = pl.ds(pl.multiple_of(i * 128, 128), 128)
            valid = (i * 128 + lax.broadcasted_iota(jnp.int32, (128, 128), 0)) >= ROW_PAD
            x = u_ref[sl, :] + b_ref[...]
            logf = jnp.where(valid, jnp.minimum(x, 0.0) - jnp.log(1.0 + jnp.exp(-jnp.abs(x))), 0.0)
            f = _dot_exact(_tri(128, True), logf) + carry
            f_ref[sl, :] = f
            return f[127:128, :]

        lax.fori_loop(0, nb, step, jnp.zeros((1, 128), F32))

    return pl.pallas_call(body, name="fgate_fwd", out_shape=jax.ShapeDtypeStruct((lp, 128), F32))(ufl, b_f)


def _fgate_bwd(ufl, b_f, d_f):
    lp = ufl.shape[0]
    nb = lp // 128

    def body(u_ref, b_ref, df_ref, du_ref, db_ref):
        def step(j, carry):
            later, db_acc = carry
            i = nb - 1 - j
            sl = pl.ds(pl.multiple_of(i * 128, 128), 128)
            valid = (i * 128 + lax.broadcasted_iota(jnp.int32, (128, 128), 0)) >= ROW_PAD
            x = u_ref[sl, :] + b_ref[...]
            df = df_ref[sl, :]
            dlogf = _dot_exact(_tri(128, False), df) + later
            dx = jnp.where(valid, dlogf * _sigmoid(-x), 0.0)
            du_ref[sl, :] = dx.astype(du_ref.dtype)
            return later + jnp.sum(df, axis=0, keepdims=True), db_acc + jnp.sum(dx, axis=0, keepdims=True)

        _, db_tot = lax.fori_loop(0, nb, step, (jnp.zeros((1, 128), F32), jnp.zeros((1, 128), F32)))
        db_ref[...] = db_tot

    return pl.pallas_call(
        body, name="fgate_bwd",
        out_shape=[jax.ShapeDtypeStruct((lp, 128), MXU), jax.ShapeDtypeStruct((1, 128), F32)],
    )(ufl, b_f, d_f)


STRIP = 32
PAIR = 2


def _strip_causal(r, t):
    row = r + lax.broadcasted_iota(jnp.int32, (STRIP, t), 0)
    col = lax.broadcasted_iota(jnp.int32, (STRIP, t), 1)
    return col <= row


def _fox_fwd(qa, ka, va, ub, name="fox_fwd", t=384):
    lp = qa.shape[0]
    d = ub.shape[2]
    t = _tile(lp, t)
    nh, nq = d // HEAD, lp // t
    scale = HEAD ** -0.5

    assert nh % PAIR == 0
    heads = range(PAIR)

    def body(q_ref, k_ref, v_ref, z_ref, o_ref, olo_ref, og_ref, qb_ref,
             s_ref, p_ref, m_ref, a_ref, l_ref, acc_ref):
        qb = pl.program_id(1)
        m_ref[...] = jnp.full((PAIR, t, 1), MASK_VALUE, F32)
        l_ref[...] = jnp.zeros((PAIR, t, 128), F32)
        acc_ref[...] = jnp.zeros((PAIR, 2 * t, HEAD), F32)
        p_ref[1] = jnp.zeros((PAIR, 2 * t, t), MXU)
        a_ref[1] = jnp.ones((PAIR, t, 1), F32)

        def scores(kb, buf):
            ks = pl.ds(pl.multiple_of(kb * t, t), t)
            for j in heads:
                ws = slice(j * WIDE, (j + 1) * WIDE)
                s_ref[buf, j] = _dot(q_ref[:, ws], k_ref[ks, ws], NT)

        def weighted_sum(kb, buf):
            ks = pl.ds(pl.multiple_of(kb * t, t), t)
            for j in heads:
                pv = _dot(p_ref[buf, j], v_ref[ks, j * WIDE:j * WIDE + HEAD])
                alpha = a_ref[buf, j]
                acc_ref[j, 0:t, :] = alpha * acc_ref[j, 0:t, :] + pv[0:t]
                acc_ref[j, t:2 * t, :] = alpha * acc_ref[j, t:2 * t, :] + pv[t:2 * t]

        def softmax_update(buf, diagonal):
            for j in heads:
                for r in range(0, t, STRIP):
                    rs = slice(r, r + STRIP)
                    x = s_ref[buf, j, rs, :] * scale
                    if diagonal:
                        x = jnp.where(_strip_causal(r, t), x, MASK_VALUE)
                    m_old = m_ref[j, rs, :]
                    m_new = jnp.maximum(m_old, jnp.max(x, axis=-1, keepdims=True))
                    alpha = jnp.exp(m_old - m_new)
                    p = jnp.exp(x - m_new)
                    m_ref[j, rs, :] = m_new
                    a_ref[buf, j, rs, :] = alpha
                    l_ref[j, rs, :] = alpha * l_ref[j, rs, :] + sum(p[:, c:c + 128] for c in range(0, t, 128))
                    p_hi = p.astype(MXU)
                    p_ref[buf, j, rs, :] = p_hi
                    p_ref[buf, j, t + r:t + r + STRIP, :] = (p - p_hi.astype(F32)).astype(MXU)

        def off_diagonal(kb, cur):
            weighted_sum(jnp.maximum(kb - 1, 0), 1 - cur)
            scores(kb + 1, 1 - cur)
            softmax_update(cur, False)

        def diagonal(cur):
            weighted_sum(jnp.maximum(qb - 1, 0), 1 - cur)
            softmax_update(cur, True)
            weighted_sum(qb, cur)

        def two_blocks(i, carry):
            off_diagonal(2 * i, 0)
            off_diagonal(2 * i + 1, 1)
            return carry

        scores(0, 0)
        lax.fori_loop(0, qb // 2, two_blocks, 0)

        @pl.when(lax.rem(qb, 2) == 0)
        def _():
            diagonal(0)

        @pl.when(lax.rem(qb, 2) == 1)
        def _():
            off_diagonal(qb - 1, 0)
            diagonal(1)

        is_pad = (qb * t + lax.broadcasted_iota(jnp.int32, (t, 1), 0)) < ROW_PAD
        for j in heads:
            hs = slice(j * HEAD, (j + 1) * HEAD)
            l = jnp.sum(l_ref[j], axis=-1, keepdims=True)
            o = acc_ref[j, 0:t, :] / l
            o_ref[:, hs] = o
            olo_ref[:, hs] = acc_ref[j, t:2 * t, :] / l
            z = z_ref[:, hs]
            og_ref[:, hs] = (o * (z * _sigmoid(z))).astype(og_ref.dtype)
            extra = q_ref[:, j * WIDE + HEAD:(j + 1) * WIDE].astype(F32)
            f_scaled = extra[:, 0:1] + extra[:, 1:2] + extra[:, 2:3]
            log_term = jnp.where(is_pad, MASK_VALUE * INV_SCALE, f_scaled - (m_ref[j] + jnp.log(l)) * INV_SCALE)
            qb_ref[:, j * WIDE:j * WIDE + HEAD] = q_ref[:, j * WIDE:j * WIDE + HEAD]
            qb_ref[:, j * WIDE + HEAD:(j + 1) * WIDE] = _extra_cols(t, _split3(log_term), None).astype(qb_ref.dtype)

    scratch = [pltpu.VMEM((2, PAIR, t, t), F32), pltpu.VMEM((2, PAIR, 2 * t, t), MXU), pltpu.VMEM((PAIR, t, 1), F32),
               pltpu.VMEM((2, PAIR, t, 1), F32), pltpu.VMEM((PAIR, t, 128), F32), pltpu.VMEM((PAIR, 2 * t, HEAD), F32)]
    tile = pl.BlockSpec((t, PAIR * HEAD), lambda g, i: (i, g))
    wide_tile = pl.BlockSpec((t, PAIR * WIDE), lambda g, i: (i, g))
    wide_all = pl.BlockSpec((lp, PAIR * WIDE), lambda g, i: (0, g))
    return pl.pallas_call(
        body,
        name=name,
        grid=(nh // PAIR, nq),
        in_specs=[wide_tile, wide_all, wide_all, pl.BlockSpec((None, t, PAIR * HEAD), lambda g, i: (1, i, g))],
        out_specs=[tile, tile, tile, wide_tile],
        out_shape=[jax.ShapeDtypeStruct((lp, d), F32), jax.ShapeDtypeStruct((lp, d), F32),
                   jax.ShapeDtypeStruct((lp, d), MXU), jax.ShapeDtypeStruct((lp, nh * WIDE), MXU)],
        scratch_shapes=scratch,
        compiler_params=_cparams(("parallel", "arbitrary"), 48),
    )(qa, ka, va, ub)


def _fox_bwd(qb, ka, va, doa, d, name="fox_bwd", t=384, sides=()):
    lp = qb.shape[0]
    t = _tile(lp, t)
    nh, nk = d // HEAD, lp // t
    scale = HEAD ** -0.5

    assert nh % PAIR == 0
    heads = range(PAIR)

    def body(q_ref, do_ref, k_ref, v_ref, dq_ref, dk_ref, dv_ref, dfk_ref,
             s_ref, dp_ref, p_ref, ds_ref, col_ref):
        kb = pl.program_id(1)

        @pl.when(kb == 0)
        def _():
            dq_ref[...] = jnp.zeros_like(dq_ref)

        dk_ref[...] = jnp.zeros_like(dk_ref)
        dv_ref[...] = jnp.zeros_like(dv_ref)
        col_ref[...] = jnp.zeros_like(col_ref)

        def step(qb, diagonal):
            qs = pl.ds(pl.multiple_of(qb * t, t), t)
            for j in heads:
                ws = slice(j * WIDE, (j + 1) * WIDE)
                s_ref[j] = _dot(q_ref[qs, ws], k_ref[:, ws], NT)
                dp_ref[j] = _dot(do_ref[qs, ws], v_ref[:, ws], NT)
            for j in heads:
                for r in range(0, t, STRIP):
                    rs = slice(r, r + STRIP)
                    x = s_ref[j, rs, :] * scale
                    if diagonal:
                        x = jnp.where(_strip_causal(r, t), x, MASK_VALUE)
                    p = jnp.exp(x)
                    ds = p * dp_ref[j, rs, :]
                    p_ref[j, rs, :] = p.astype(MXU)
                    ds_ref[j, rs, :] = (ds * scale).astype(MXU)
                    col_ref[j] += ds
            for j in heads:
                hs = slice(j * HEAD, (j + 1) * HEAD)
                narrow = slice(j * WIDE, j * WIDE + HEAD)
                dsb = ds_ref[j]
                dv_ref[:, hs] += _dot(p_ref[j], do_ref[qs, narrow], TN)
                dq_ref[qs, hs] += _dot(dsb, k_ref[:, narrow])
                dk_ref[:, hs] += _dot(dsb, q_ref[qs, narrow], TN)

        def off_diagonal(qb, carry):
            step(qb, False)
            return carry

        step(kb, True)
        lax.fori_loop(kb + 1, nk, off_diagonal, 0)
        for j in heads:
            dfk_ref[j] = -jnp.sum(col_ref[j], axis=0, keepdims=True)

    scratch = [pltpu.VMEM((PAIR, t, t), F32), pltpu.VMEM((PAIR, t, t), F32), pltpu.VMEM((PAIR, t, t), MXU),
               pltpu.VMEM((PAIR, t, t), MXU), pltpu.VMEM((PAIR, STRIP, t), F32)]
    whole = pl.BlockSpec((lp, PAIR * HEAD), lambda g, j: (0, g))
    tile = pl.BlockSpec((t, PAIR * HEAD), lambda g, j: (j, g))
    wide_all = pl.BlockSpec((lp, PAIR * WIDE), lambda g, j: (0, g))
    wide_tile = pl.BlockSpec((t, PAIR * WIDE), lambda g, j: (j, g))
    outs, side_results = _pallas(
        body,
        name=name,
        args=(qb, doa, ka, va),
        grid=(nh // PAIR, nk),
        in_specs=[wide_all, wide_all, wide_tile, wide_tile],
        out_specs=[whole, tile, tile, pl.BlockSpec((PAIR, 1, t), lambda g, j: (g, 0, j))],
        out_shape=[jax.ShapeDtypeStruct((lp, d), F32)] * 3 + [jax.ShapeDtypeStruct((nh, 1, lp), F32)],
        scratch_shapes=scratch,
        semantics=("parallel", "arbitrary"),
        vmem_mib=48,
        sides=sides,
    )
    return (*outs, side_results) if sides else tuple(outs)


def _fox_do(d_og, o, o_lo, ub, name="fox_do", tm=384):
    lp, d = o.shape
    tm = _tile(lp, tm)
    nh = d // HEAD

    def body(dog_ref, o_ref, olo_ref, z_ref, doa_ref):
        for h in range(nh):
            hs = slice(h * HEAD, (h + 1) * HEAD)
            z = z_ref[:, hs]
            do = (dog_ref[:, hs] * (z * _sigmoid(z))).astype(doa_ref.dtype)
            delta = jnp.sum(do.astype(F32) * (o_ref[:, hs] + olo_ref[:, hs]), axis=-1, keepdims=True)
            doa_ref[:, h * WIDE:h * WIDE + HEAD] = do
            doa_ref[:, h * WIDE + HEAD:(h + 1) * WIDE] = _extra_cols(tm, _split3(-delta), (0.0, 0.0, 0.0)).astype(
                doa_ref.dtype)

    row = pl.BlockSpec((tm, d), lambda i: (i, 0))
    return pl.pallas_call(
        body,
        name=name,
        grid=(lp // tm,),
        in_specs=[row, row, row, pl.BlockSpec((None, tm, d), lambda i: (1, i, 0))],
        out_specs=pl.BlockSpec((tm, nh * WIDE), lambda i: (i, 0)),
        out_shape=jax.ShapeDtypeStruct((lp, nh * WIDE), MXU),
        compiler_params=_cparams(("parallel",), 56),
    )(d_og, o, o_lo, ub)


def _fox_q_bwd(ub, d_og, o, dqn, g_q, name="fox_q_bwd", tm=384):
    _, lp, d = ub.shape
    tm = _tile(lp, tm)
    nh = d // HEAD

    def body(ub_ref, dog_ref, o_ref, dqn_ref, g_ref, dub_ref, dg_ref):
        i = pl.program_id(0)
        for h in range(nh):
            hs = slice(h * HEAD, (h + 1) * HEAD)
            z = ub_ref[1, :, hs]
            sz = _sigmoid(z)
            dub_ref[1, :, hs] = (dog_ref[:, hs] * o_ref[:, hs] * (sz * (1.0 + z * (1.0 - sz)))).astype(dub_ref.dtype)
            dx, dg = _head_rms_bwd_tile(ub_ref[0, :, hs], g_ref[:, hs], dqn_ref[:, hs])
            dub_ref[0, :, hs] = dx.astype(dub_ref.dtype)

            @pl.when(i == 0)
            def _():
                dg_ref[:, hs] = dg

            @pl.when(i > 0)
            def _():
                dg_ref[:, hs] += dg

    row = pl.BlockSpec((tm, d), lambda i: (i, 0))
    pair = pl.BlockSpec((2, tm, d), lambda i: (0, i, 0))
    vec = pl.BlockSpec((1, d), lambda i: (0, 0))
    return pl.pallas_call(
        body,
        name=name,
        grid=(lp // tm,),
        in_specs=[pair, row, row, row, vec],
        out_specs=[pair, vec],
        out_shape=[jax.ShapeDtypeStruct((2, lp, d), MXU), jax.ShapeDtypeStruct((1, d), F32)],
        compiler_params=_cparams(("arbitrary",), 56),
    )(ub, d_og, o, dqn, g_q)


def _fox_k_bwd(ukv, dkn, dv, g_k, name="fox_k_bwd", tm=384):
    _, lp, d = ukv.shape
    tm = _tile(lp, tm)
    nh = d // HEAD

    def body(ukv_ref, dkn_ref, dv_ref, g_ref, du_ref, dg_ref):
        i = pl.program_id(0)
        du_ref[1] = dv_ref[...].astype(du_ref.dtype)
        for h in range(nh):
            hs = slice(h * HEAD, (h + 1) * HEAD)
            dx, dg = _head_rms_bwd_tile(ukv_ref[:, hs], g_ref[:, hs], dkn_ref[:, hs])
            du_ref[0, :, hs] = dx.astype(du_ref.dtype)

            @pl.when(i == 0)
            def _():
                dg_ref[:, hs] = dg

            @pl.when(i > 0)
            def _():
                dg_ref[:, hs] += dg

    row = pl.BlockSpec((tm, d), lambda i: (i, 0))
    vec = pl.BlockSpec((1, d), lambda i: (0, 0))
    return pl.pallas_call(
        body,
        name=name,
        grid=(lp // tm,),
        in_specs=[pl.BlockSpec((None, tm, d), lambda i: (0, i, 0)), row, row, vec],
        out_specs=[pl.BlockSpec((2, tm, d), lambda i: (0, i, 0)), vec],
        out_shape=[jax.ShapeDtypeStruct((2, lp, d), MXU), jax.ShapeDtypeStruct((1, d), F32)],
        compiler_params=_cparams(("arbitrary",), 48),
    )(ukv, dkn, dv, g_k)


def _loss(h, target, name="loss_head"):
    lp, d = h.shape
    nb = lp // 128

    def body(h_ref, t_ref, loss_ref, dh_ref, acc_ref):
        i = pl.program_id(0)

        @pl.when(i == 0)
        def _():
            acc_ref[...] = jnp.zeros_like(acc_ref)
            dh_ref[...] = jnp.zeros_like(dh_ref)

        @pl.when(i > 0)
        def _():
            err = h_ref[...] - t_ref[...]
            dh_ref[...] = err * (1.0 / d)
            acc_ref[...] += jnp.sum(jnp.sum(err * err, axis=-1, keepdims=True) * (1.0 / d), axis=0, keepdims=True)

        @pl.when(i == nb - 1)
        def _():
            loss_ref[...] = 0.5 * acc_ref[...]

    return pl.pallas_call(
        body,
        name=name,
        grid=(nb,),
        in_specs=[pl.BlockSpec((128, d), lambda i: (i, 0)),
                  pl.BlockSpec((128, d), lambda i: (jnp.maximum(i - 1, 0), 0))],
        out_specs=[pl.BlockSpec((1, 1), lambda i: (0, 0)), pl.BlockSpec((128, d), lambda i: (i, 0))],
        out_shape=[jax.ShapeDtypeStruct((1, 1), F32), jax.ShapeDtypeStruct((lp, d), F32)],
        scratch_shapes=[pltpu.VMEM((1, 1), F32)],
        compiler_params=_cparams(("arbitrary",), 32),
    )(h, target)


def _adam_math(w, g, m, v):
    m = ADAM_B1 * m + (1.0 - ADAM_B1) * g
    v = ADAM_B2 * v + (1.0 - ADAM_B2) * (g * g)
    m_hat = m / (1.0 - ADAM_B1 ** ADAM_STEP)
    v_hat = v / (1.0 - ADAM_B2 ** ADAM_STEP)
    delta = -ADAM_LR * (m_hat / (jnp.sqrt(v_hat) + ADAM_EPS) + ADAM_WD * w)
    return delta, m, v


def _adamw(parts, w, m, v, name, tm=256):
    n, r, c = parts.shape
    tm = tm if r % tm == 0 else r

    def body(p_ref, w_ref, m_ref, v_ref, g_ref, d_ref, nm_ref, nv_ref):
        g = p_ref[0].astype(F32)
        for j in range(1, n):
            g = g + p_ref[j].astype(F32)
        g_ref[...] = g
        d_ref[...], nm_ref[...], nv_ref[...] = _adam_math(w_ref[...], g, m_ref[...], v_ref[...])

    row = pl.BlockSpec((tm, c), lambda i: (i, 0))
    return pl.pallas_call(
        body,
        name=name,
        grid=(r // tm,),
        in_specs=[pl.BlockSpec((n, tm, c), lambda i: (0, i, 0)), row, row, row],
        out_specs=[row] * 4,
        out_shape=[jax.ShapeDtypeStruct((r, c), F32)] * 4,
        compiler_params=_cparams(("parallel",), 48),
    )(parts, w, m, v)


def _place():
    x, y, c = lax.axis_index("x"), lax.axis_index("y"), lax.axis_index("c")
    return x, y, c


def _other_chips(x, y):
    return [(1 - x, y), (x, 1 - y), (1 - x, 1 - y)]


def _any_spec():
    return pl.BlockSpec(memory_space=pl.ANY)


class _Side:
    def __init__(self, ins, outs, sems, start, finish, aliases=None):
        self.ins, self.outs, self.sems = list(ins), list(outs), sems
        self.start, self.finish = start, finish
        self.aliases = dict(aliases or {})


def _pallas(body, *, name, out_shape, args=(), grid=(), in_specs=(), out_specs=(), scratch_shapes=(),
            semantics=(), vmem_mib=None, sides=()):
    n_in, n_out, n_scr = len(args), len(out_shape), len(scratch_shapes)
    side_ins = [a for s in sides for a in s.ins]
    side_outs = [o for s in sides for o in s.outs]
    side_sems = [pltpu.SemaphoreType.DMA((max(k, 1),)) for s in sides for k in s.sems]
    aliases, in_at, out_at = {}, n_in, n_out
    for s in sides:
        aliases.update({in_at + i: out_at + o for i, o in s.aliases.items()})
        in_at, out_at = in_at + len(s.ins), out_at + len(s.outs)

    def wrapped(*refs):
        at = [0]

        def take(k):
            got = refs[at[0]:at[0] + k]
            at[0] += k
            return got

        main_in = take(n_in)
        s_in = [take(len(s.ins)) for s in sides]
        main_out = take(n_out)
        s_out = [take(len(s.outs)) for s in sides]
        main_scr = take(n_scr)
        s_sem = [take(3) for s in sides]

        def run(stage):
            for s, i_, o_, m_ in zip(sides, s_in, s_out, s_sem):
                getattr(s, stage)(i_, o_, *m_)

        first = last = None
        for k, g in enumerate(grid):
            i = pl.program_id(k)
            first = (i == 0) if first is None else jnp.logical_and(first, i == 0)
            last = (i == g - 1) if last is None else jnp.logical_and(last, i == g - 1)
        if sides:
            run("start") if first is None else pl.when(first)(lambda: run("start"))
        body(*main_in, *main_out, *main_scr)
        if sides:
            run("finish") if last is None else pl.when(last)(lambda: run("finish"))

    kw = {}
    if grid:
        kw["grid"] = grid
    if semantics or vmem_mib:
        sem = tuple("arbitrary" for _ in grid) if sides else tuple(semantics)
        kw["compiler_params"] = pltpu.CompilerParams(
            dimension_semantics=sem or None, vmem_limit_bytes=vmem_mib * MIB if vmem_mib else None)
    res = pl.pallas_call(
        wrapped,
        name=name,
        in_specs=list(in_specs) + [_any_spec()] * len(side_ins),
        out_specs=list(out_specs) + [_any_spec()] * len(side_outs),
        out_shape=list(out_shape) + side_outs,
        scratch_shapes=list(scratch_shapes) + side_sems,
        input_output_aliases=aliases,
        **kw,
    )(*args, *side_ins)
    main, rest, per_side = list(res[:n_out]), list(res[n_out:]), []
    for s in sides:
        per_side.append(rest[:len(s.outs)])
        rest = rest[len(s.outs):]
    return main, per_side


def _slot(p):
    return 4 * p[0] + 2 * p[1] + p[2]


def _sibling_side(grads):
    n = len(grads)

    def copies(ins, outs, send_sems, recv_sems):
        x, y, c = _place()
        return [pltpu.make_async_remote_copy(
            src_ref=ins[t].at[2 * chip + (1 - c)], dst_ref=outs[t].at[chip],
            send_sem=send_sems.at[4 * t + chip], recv_sem=recv_sems.at[4 * t + chip],
            device_id=(x, y, 1 - c), device_id_type=MESH) for t in range(n) for chip in range(N_CHIP)]

    def start(ins, outs, send_sems, recv_sems, local_sems):
        for cp in copies(ins, outs, send_sems, recv_sems):
            cp.start()

    def finish(ins, outs, send_sems, recv_sems, local_sems):
        for cp in copies(ins, outs, send_sems, recv_sems):
            cp.wait()

    outs = [jax.ShapeDtypeStruct((N_CHIP,) + g.shape[1:], g.dtype) for g in grads]
    return _Side(grads, outs, (4 * n, 4 * n, 0), start, finish)


def _chips_side(partials):
    n = len(partials)

    def copies(ins, outs, send_sems, recv_sems, local_sems):
        x, y, c = _place()
        my_chip = 2 * x + y
        local = [pltpu.make_async_copy(ins[t].at[my_chip], outs[t].at[my_chip], local_sems.at[t]) for t in range(n)]
        sends, recvs = [], []
        for t in range(n):
            for j, chip in enumerate(_other_chips(x, y)):
                their = 2 * chip[0] + chip[1]
                sems = dict(send_sem=send_sems.at[3 * t + j], recv_sem=recv_sems.at[3 * t + j],
                            device_id=(*chip, c), device_id_type=MESH)
                sends.append(pltpu.make_async_remote_copy(src_ref=ins[t].at[their], dst_ref=outs[t].at[my_chip], **sems))
                recvs.append(pltpu.make_async_remote_copy(src_ref=ins[t].at[my_chip], dst_ref=outs[t].at[their], **sems))
        return local, sends, recvs

    def start(ins, outs, send_sems, recv_sems, local_sems):
        local, sends, _ = copies(ins, outs, send_sems, recv_sems, local_sems)
        for cp in local + sends:
            cp.start()

    def finish(ins, outs, send_sems, recv_sems, local_sems):
        local, sends, recvs = copies(ins, outs, send_sems, recv_sems, local_sems)
        for cp in sends:
            cp.wait_send()
        for cp in recvs:
            cp.wait_recv()
        for cp in local:
            cp.wait()

    outs = [jax.ShapeDtypeStruct(p.shape, p.dtype) for p in partials]
    return _Side(partials, outs, (3 * n, 3 * n, n), start, finish)


def _gather_own_side(blocks):
    n = len(blocks)

    def copies(ins, outs, send_sems, recv_sems, local_sems):
        x, y, c = _place()
        me = (x, y, c)
        peers = [(x, y, 1 - c)] + [(*chip, c) for chip in _other_chips(x, y)]
        local = [pltpu.make_async_copy(ins[t], outs[t].at[_slot(me)], local_sems.at[t]) for t in range(n)]
        sends, recvs = [], []
        for t in range(n):
            for k, peer in enumerate(peers):
                sems = dict(send_sem=send_sems.at[4 * t + k], recv_sem=recv_sems.at[4 * t + k],
                            device_id=peer, device_id_type=MESH)
                sends.append(pltpu.make_async_remote_copy(src_ref=ins[t], dst_ref=outs[t].at[_slot(me)], **sems))
                recvs.append(pltpu.make_async_remote_copy(src_ref=ins[t], dst_ref=outs[t].at[_slot(peer)], **sems))
        return local, sends, recvs

    def start(ins, outs, send_sems, recv_sems, local_sems):
        local, sends, _ = copies(ins, outs, send_sems, recv_sems, local_sems)
        for cp in local + sends:
            cp.start()

    def finish(ins, outs, send_sems, recv_sems, local_sems):
        local, sends, recvs = copies(ins, outs, send_sems, recv_sems, local_sems)
        for cp in sends:
            cp.wait_send()
        for cp in recvs:
            cp.wait_recv()
        for cp in local:
            cp.wait()

    outs = [jax.ShapeDtypeStruct((N_DEV,) + b.shape, b.dtype) for b in blocks]
    return _Side(blocks, outs, (4 * n, 4 * n, n), start, finish)


def _gather_pass_side(gathered):
    n = len(gathered)

    def copies(outs, send_sems, recv_sems):
        x, y, c = _place()
        sends, recvs = [], []
        for t in range(n):
            for j, chip in enumerate(_other_chips(x, y)):
                sems = dict(send_sem=send_sems.at[3 * t + j], recv_sem=recv_sems.at[3 * t + j],
                            device_id=(x, y, 1 - c), device_id_type=MESH)
                mine, theirs = outs[t].at[_slot((*chip, c))], outs[t].at[_slot((*chip, 1 - c))]
                sends.append(pltpu.make_async_remote_copy(src_ref=mine, dst_ref=mine, **sems))
                recvs.append(pltpu.make_async_remote_copy(src_ref=mine, dst_ref=theirs, **sems))
        return sends, recvs

    def start(ins, outs, send_sems, recv_sems, local_sems):
        for cp in copies(outs, send_sems, recv_sems)[0]:
            cp.start()

    def finish(ins, outs, send_sems, recv_sems, local_sems):
        sends, recvs = copies(outs, send_sems, recv_sems)
        for cp in sends:
            cp.wait_send()
        for cp in recvs:
            cp.wait_recv()

    outs = [jax.ShapeDtypeStruct(g.shape, g.dtype) for g in gathered]
    return _Side(gathered, outs, (3 * n, 3 * n, 0), start, finish, aliases={t: t for t in range(n)})


def _alone(side, name):
    return _pallas(lambda: None, name=name, out_shape=[], sides=[side])[1][0]


def _all_gather(blocks, name):
    n = len(blocks)

    def body(*refs):
        in_refs, out_refs = refs[:n], refs[n:2 * n]
        send_sems, recv_sems, local_sems = refs[2 * n:]
        x, y, c = _place()
        me, sibling = (x, y, c), (x, y, 1 - c)
        chips = _other_chips(x, y)

        def slot(p):
            return 4 * p[0] + 2 * p[1] + p[2]

        def copy(t, k, block, to, src=None):
            dst = out_refs[t].at[slot(block)]
            return pltpu.make_async_remote_copy(
                src_ref=dst if src is None else src, dst_ref=dst,
                send_sem=send_sems.at[7 * t + k], recv_sem=recv_sems.at[7 * t + k],
                device_id=to, device_id_type=MESH)

        started = []
        for t in range(n):
            mine = pltpu.make_async_copy(in_refs[t], out_refs[t].at[slot(me)], local_sems.at[t])
            mine.start()
            started.append(mine)
        sends = []
        for t in range(n):
            first = [copy(t, 0, me, sibling, src=in_refs[t])]
            first += [copy(t, 1 + j, me, (*chip, c), src=in_refs[t]) for j, chip in enumerate(chips)]
            for cp in first:
                cp.start()
            sends += first
        for t in range(n):
            for j, chip in enumerate(chips):
                copy(t, 1 + j, (*chip, c), me).wait_recv()
                passed = copy(t, 4 + j, (*chip, c), sibling)
                passed.start()
                sends.append(passed)
        for t in range(n):
            copy(t, 0, sibling, me).wait_recv()
            for j, chip in enumerate(chips):
                copy(t, 4 + j, (*chip, 1 - c), me).wait_recv()
        for cp in sends:
            cp.wait_send()
        for mine in started:
            mine.wait()

    return pl.pallas_call(
        body,
        name=name,
        in_specs=[_any_spec()] * n,
        out_specs=[_any_spec()] * n,
        out_shape=[jax.ShapeDtypeStruct((N_DEV,) + b.shape, b.dtype) for b in blocks],
        scratch_shapes=[pltpu.SemaphoreType.DMA((7 * n,)), pltpu.SemaphoreType.DMA((7 * n,)),
                        pltpu.SemaphoreType.DMA((n,))],
    )(*blocks)


def _pair_sum(g8, got, name, tm=256):
    _, r, c = g8.shape
    tm = tm if r % tm == 0 else r
    core = lax.axis_index("c")

    def body(core_ref, mine_ref, got_ref, o_ref):
        south_first = core_ref[0] == 0
        a, b = mine_ref[...], got_ref[...]
        o_ref[...] = (jnp.where(south_first, a, b) + jnp.where(south_first, b, a)).astype(o_ref.dtype)

    return pl.pallas_call(
        body,
        name=name,
        grid_spec=pltpu.PrefetchScalarGridSpec(
            num_scalar_prefetch=1,
            grid=(N_CHIP, r // tm),
            in_specs=[pl.BlockSpec((None, tm, c), lambda j, i, core_ref: (2 * j + core_ref[0], i, 0)),
                      pl.BlockSpec((None, tm, c), lambda j, i, core_ref: (j, i, 0))],
            out_specs=pl.BlockSpec((None, tm, c), lambda j, i, core_ref: (j, i, 0)),
        ),
        out_shape=jax.ShapeDtypeStruct((N_CHIP, r, c), PAYLOAD),
        compiler_params=_cparams(("parallel", "parallel"), 32),
    )(jnp.reshape(core, (1,)).astype(jnp.int32), g8, got)


def _all_reduce_small(s, name="small_all_reduce"):
    r, c = s.shape

    def body(s_ref, o_ref, buf_ref, send_sems, recv_sems):
        x, y, c_ = _place()
        me = 4 * x + 2 * y + c_
        buf_ref[me] = s_ref[...]

        def copy(k, slot, peer):
            return pltpu.make_async_remote_copy(
                src_ref=s_ref, dst_ref=buf_ref.at[slot],
                send_sem=send_sems.at[k - 1], recv_sem=recv_sems.at[k - 1],
                device_id=peer, device_id_type=MESH)

        peers = []
        for k in range(1, N_DEV):
            peer = (x ^ ((k >> 2) & 1), y ^ ((k >> 1) & 1), c_ ^ (k & 1))
            peers.append(peer)
            copy(k, me, peer).start()
        for k, peer in zip(range(1, N_DEV), peers):
            cp = copy(k, 4 * peer[0] + 2 * peer[1] + peer[2], peer)
            cp.wait_send()
            cp.wait_recv()
        total = buf_ref[0]
        for j in range(1, N_DEV):
            total = total + buf_ref[j]
        o_ref[...] = total

    return pl.pallas_call(
        body,
        name=name,
        out_shape=jax.ShapeDtypeStruct((r, c), F32),
        in_specs=[pl.BlockSpec(memory_space=pltpu.VMEM)],
        out_specs=pl.BlockSpec(memory_space=pltpu.VMEM),
        scratch_shapes=[pltpu.VMEM((N_DEV, r, c), F32), pltpu.SemaphoreType.DMA((N_DEV - 1,)),
                        pltpu.SemaphoreType.DMA((N_DEV - 1,))],
    )(s)


def _late_weights(g_kv, g_b_in, g_b_out):
    d = g_kv.shape[1]
    nh = d // HEAD
    wkv = jnp.reshape(jnp.transpose(g_kv, (1, 0, 2)), (d, -1))
    return wkv[:, :2 * d], jnp.pad(wkv[:, 2 * d:], ((0, 0), (0, 128 - nh))), g_b_in, jnp.reshape(g_b_out, (d, d))


def _kv_grad_blocks(dwkv2, dwfl):
    d = dwkv2.shape[0]
    dwkv = jnp.concatenate([dwkv2, dwfl[:, :d // HEAD]], axis=1)
    return jnp.transpose(jnp.reshape(dwkv, (d, N_DEV, -1)), (1, 0, 2))


def _local_step(x, target, meta, gamma, a_norm, wa_in, a_out_norm, wa_out, kv_norm, late, b_f, g_k, b_norm, g_q,
                dist):
    d = x.shape[1]
    nh = d // HEAD
    cols = d // N_DEV
    h0 = jnp.concatenate([jnp.zeros((ROW_PAD, d), F32), meta, x], axis=0)
    lp = h0.shape[0]
    big = dict(tm=1408, tn=512, tk=2048)

    lb = _lb_fwd(gamma)
    (hn_a,) = _rms_fwd(h0, [a_norm], "rms_a")
    u4 = _matmul(hn_a, wa_in, "nn", F32, "a_in", out_parts=4, **big)
    if dist:
        o_a, og_a, (gathered,) = _hgrn2_fwd(u4, lb, a_out_norm, sides=[_gather_own_side(list(late))])
        h1, (gathered,) = _matmul(og_a, wa_out, "nn", F32, "a_out", add=h0, sides=[_gather_pass_side(gathered)], **big)
        wkv2, wfl, wb_in, wb_out = _late_weights(*gathered)
    else:
        o_a, og_a = _hgrn2_fwd(u4, lb, a_out_norm)
        h1 = _matmul(og_a, wa_out, "nn", F32, "a_out", add=h0, **big)
        wkv2, wfl, wb_in, wb_out = late
    hk, hb = _rms_fwd(h1, [kv_norm, b_norm], "rms_kv_b")
    ukv = _matmul(hk, wkv2, "nn", F32, "kv_in", out_parts=2, **big)
    ufl = _matmul(hk, wfl, "nn", F32, "kv_f", **big)
    ub = _matmul(hb, wb_in, "nn", F32, "b_in", out_parts=2, **big)
    f_cum = _fgate_fwd(ufl, b_f)
    qa, ka, va = _attn_operands(ub, ukv, g_q, g_k, f_cum)
    o_b, o_lo, og_b, qb = _fox_fwd(qa, ka, va, ub)
    h2 = _matmul(og_b, wb_out, "nn", F32, "b_out", add=h1, **big)
    loss, dh2 = _loss(h2, target)

    dx_t = dict(tm=1408, tn=512, tk=2048, vmem_mib=56)
    dx_parts_t = dict(tm=704, tn=512, tk=2048, vmem_mib=58, k_whole=True)
    dw_t = dict(tm=512, tn=1024, tk=lp, vmem_mib=58)
    dw_f32_t = dict(tm=1024, tn=512, tk=lp, vmem_mib=58)
    def to_sibling(g8):
        return [_sibling_side([g8])] if dist else []

    def to_chips(partial):
        return [_chips_side([partial])] if dist else []

    def unpack(res, n_sides):
        if not dist:
            return res, [None] * n_sides
        *main, side_results = res
        return (main[0] if len(main) == 1 else tuple(main)), [r[0] for r in side_results]

    dwb_out = _matmul(og_b, dh2, "tn", F32, "b_out_dw", **dw_f32_t)
    g8_b_out = jnp.reshape(dwb_out, (N_DEV, cols, d))
    d_ogb, (got,) = unpack(_matmul(dh2, wb_out, "nt", F32, "b_out_dx", sides=to_sibling(g8_b_out), **dx_t), 1)
    p_b_out = _pair_sum(g8_b_out, got, "pair_sum_b_w_out") if dist else None
    doa = _fox_do(d_ogb, o_b, o_lo, ub)
    (dqn, dkn, dv, dfk), (r_b_out,) = unpack(_fox_bwd(qb, ka, va, doa, d, sides=to_chips(p_b_out)), 1)
    dub, dg_q = _fox_q_bwd(ub, d_ogb, o_b, dqn, g_q)
    dukv, dg_k = _fox_k_bwd(ukv, dkn, dv, g_k)
    d_f = jnp.pad(jnp.transpose(dfk[:, 0, :]), ((0, 0), (0, 128 - nh)))
    dufl, db_f = _fgate_bwd(ufl, b_f, d_f)
    dwb_in = _matmul(hb, dub, "tn", F32, "b_in_dw", out_parts=N_DEV, **dw_t)
    d_hb, (got,) = unpack(_matmul(dub, wb_in, "nt", F32, "b_in_dx", sides=to_sibling(dwb_in), **dx_parts_t), 1)
    p_b_in = _pair_sum(dwb_in, got, "pair_sum_b_w_in") if dist else None
    d_hk, (r_b_in,) = unpack(_matmul(dukv, wkv2, "nt", F32, "kv_dx", sides=to_chips(p_b_in), **dx_parts_t), 1)
    d_hk = _matmul(dufl, wfl, "nt", F32, "kv_f_dx", add=d_hk, **dx_t)
    dwkv2 = _matmul(hk, dukv, "tn", F32, "kv_dw", **dw_t)
    dwfl = _matmul(hk, dufl, "tn", F32, "kv_f_dw", **dw_t)
    g8_kv = _kv_grad_blocks(dwkv2, dwfl) if dist else None
    dh1, (dg_kv, dg_b) = _rms_bwd(h1, [kv_norm, b_norm], [d_hk, d_hb], dh2, "rms_kv_b_bwd")
    d_oga, (got,) = unpack(_matmul(dh1, wa_out, "nt", F32, "a_out_dx", sides=to_sibling(g8_kv), **dx_t), 1)
    p_kv = _pair_sum(g8_kv, got, "pair_sum_kv_w") if dist else None
    dwa_out = _matmul(og_a, dh1, "tn", F32, "a_out_dw", **dw_f32_t)
    g8_a_out = jnp.reshape(dwa_out, (N_DEV, cols, d))
    (du4, dlb, dg_aout), (r_kv, got) = unpack(
        _hgrn2_bwd(u4, o_a, d_oga, lb, a_out_norm, sides=to_chips(p_kv) + to_sibling(g8_a_out)), 2)
    p_a_out = _pair_sum(g8_a_out, got, "pair_sum_a_w_out") if dist else None
    dwa_in, (r_a_out,) = unpack(
        _matmul(hn_a, du4, "tn", F32, "a_in_dw", out_parts=N_DEV, sides=to_chips(p_a_out), **dw_t), 1)
    d_hna, (got,) = unpack(_matmul(du4, wa_in, "nt", F32, "a_in_dx", sides=to_sibling(dwa_in), **dx_parts_t), 1)
    p_a_in = _pair_sum(dwa_in, got, "pair_sum_a_w_in") if dist else None
    dh0, (dg_a,) = _rms_bwd(h0, [a_norm], [d_hna], dh1, "rms_a_bwd")
    dgamma = _lb_bwd(gamma, dlb)

    grads = dict(meta=dh0[ROW_PAD:ROW_PAD + N_META], gamma=dgamma, a_norm=dg_a, a_out_norm=dg_aout, kv_norm=dg_kv,
                 b_f=db_f, g_k=dg_k, b_norm=dg_b, g_q=dg_q)
    if dist:
        (r_a_in,) = _alone(_chips_side([p_a_in]), "grads_to_chips_a_w_in")
        grads.update(wa_in=r_a_in, wa_out=r_a_out, wkv=r_kv, wb_in=r_b_in, wb_out=r_b_out)
    else:
        grads.update(wa_in=dwa_in, wa_out=dwa_out, wkv2=dwkv2, wfl=dwfl, wb_in=dwb_in, wb_out=dwb_out)
    return loss, dh0[ROW_PAD + N_META:], grads


def kernel(x, meta, gamma_lb, a_norm, a_w_in, a_out_norm, a_w_out, kv_norm, kv_w, fox_b_f, fox_k_norm, b_norm, b_w_in, b_q_norm, b_w_out, loss_target, m_meta, m_gamma_lb, m_a_norm, m_a_w_in, m_a_out_norm, m_a_w_out, m_kv_norm, m_kv_w, m_fox_b_f, m_fox_k_norm, m_b_norm, m_b_w_in, m_b_q_norm, m_b_w_out, v_meta, v_gamma_lb, v_a_norm, v_a_w_in, v_a_out_norm, v_a_w_out, v_kv_norm, v_kv_w, v_fox_b_f, v_fox_k_norm, v_b_norm, v_b_w_in, v_b_q_norm, v_b_w_out):
    d = x.shape[-1]
    nh = d // HEAD
    cols = d // N_DEV
    me = 4 * lax.axis_index("x") + 2 * lax.axis_index("y") + lax.axis_index("c")

    sharded_small = jnp.concatenate([meta, gamma_lb, a_norm, a_out_norm, jnp.zeros((4, cols), F32)], axis=0)
    g_a_in, g_a_out, g_small = _all_gather(
        [a_w_in[0].astype(MXU), a_w_out[0].astype(MXU), sharded_small], "gather_weights")
    wa_out = jnp.reshape(g_a_out, (d, d))
    late = (kv_w.astype(MXU), b_w_in[0].astype(MXU), b_w_out[0].astype(MXU))
    small = jnp.reshape(jnp.transpose(g_small, (1, 0, 2)), (-1, d))
    meta_f, gamma_f, a_norm_f, a_out_norm_f = small[:16], small[16:18], small[18:19], small[19:20]
    b_f = jnp.pad(jnp.reshape(fox_b_f, (1, nh)), ((0, 0), (0, 128 - nh)))
    g_k = jnp.reshape(fox_k_norm, (1, d))
    g_q = jnp.reshape(b_q_norm, (1, d))
    kv_norm_r = jnp.reshape(kv_norm, (1, d))

    loss, grad_x, g = _local_step(x[0], loss_target[0], meta_f, gamma_f, a_norm_f, g_a_in, a_out_norm_f, wa_out,
                                  kv_norm_r, late, b_f, g_k, b_norm, g_q, dist=True)
    loss = lax.psum(loss[0, 0], AXES)

    r_a_in = _adamw(g["wa_in"], a_w_in[0], m_a_w_in[0], v_a_w_in[0], "adamw_a_w_in")
    r_a_out = _adamw(g["wa_out"], a_w_out[0], m_a_w_out[0], v_a_w_out[0], "adamw_a_w_out")
    r_kv = _adamw(g["wkv"], kv_w, m_kv_w, v_kv_w, "adamw_kv_w")
    r_b_in = _adamw(g["wb_in"], b_w_in[0], m_b_w_in[0], v_b_w_in[0], "adamw_b_w_in")
    r_b_out = _adamw(g["wb_out"], b_w_out[0], m_b_w_out[0], v_b_w_out[0], "adamw_b_w_out")

    packed = jnp.concatenate(
        [g["meta"], g["gamma"], g["a_norm"], g["a_out_norm"], g["kv_norm"], g["b_norm"], g["g_k"], g["g_q"],
         jnp.pad(g["b_f"], ((0, 0), (0, d - 128))), jnp.zeros((7, d), F32)], axis=0)
    tot = _all_reduce_small(packed)
    mine = lax.dynamic_slice_in_dim(tot[:20], me * cols, cols, axis=1)
    gs = [mine[:16], mine[16:18], mine[18:19], mine[19:20], tot[20], tot[21:22], jnp.reshape(tot[22], (nh, HEAD)),
          jnp.reshape(tot[23], (1, nh, HEAD)), tot[24, :nh]]
    small_w = [meta, gamma_lb, a_norm, a_out_norm, kv_norm, b_norm, fox_k_norm, b_q_norm, fox_b_f]
    small_m = [m_meta, m_gamma_lb, m_a_norm, m_a_out_norm, m_kv_norm, m_b_norm, m_fox_k_norm, m_b_q_norm, m_fox_b_f]
    small_v = [v_meta, v_gamma_lb, v_a_norm, v_a_out_norm, v_kv_norm, v_b_norm, v_fox_k_norm, v_b_q_norm, v_fox_b_f]

    def pack(arrs):
        rows = [jnp.reshape(a, (-1, cols)) for a in arrs[:8]]
        rows.append(jnp.pad(jnp.reshape(arrs[8], (1, nh)), ((0, 0), (0, cols - nh))))
        n_rows = sum(r.shape[0] for r in rows)
        return jnp.concatenate(rows + [jnp.zeros((-n_rows % 8, cols), F32)], axis=0)

    _, sd, sm, sv = _adamw(pack(gs)[None], pack(small_w), pack(small_m), pack(small_v), "adamw_small")

    def unpack(p):
        out, at = [], 0
        for a in small_w[:8]:
            n = a.size // cols
            out.append(jnp.reshape(p[at:at + n], a.shape))
            at += n
        out.append(p[at, :nh])
        return out

    d_s, m_s, v_s = unpack(sd), unpack(sm), unpack(sv)

    def ordered(s, a_in, a_out, kv, b_in, b_out):
        return [s[0], s[1], s[2], a_in[None], s[3], a_out[None], s[4], kv, s[8], s[6], s[5], b_in[None], s[7], b_out[None]]

    outs = []
    for i, s in enumerate([gs, d_s, m_s, v_s]):
        outs += ordered(s, r_a_in[i], r_a_out[i], r_kv[i], r_b_in[i], r_b_out[i])
    return (loss, grad_x[None], *outs)
```

```python
import math

import jax
import jax.numpy as jnp
from jax import lax
from jax.experimental import pallas as pl
from jax.experimental.pallas import tpu as pltpu

HEAD = 128
CHUNK = 64
N_META = 16
ROW_PAD = 128 - N_META
EPS = 1e-6
MASK_VALUE = -1e30
ADAM_LR = 0.001
ADAM_B1 = 0.9
ADAM_B2 = 0.999
ADAM_EPS = 1e-08
ADAM_WD = 0.01
ADAM_STEP = 10
N_DEV = 8
N_CHIP = 4
MIB = 1024 * 1024
AXES = ("x", "y", "c")
MESH = pl.DeviceIdType.MESH

F32 = jnp.float32
MXU = jnp.bfloat16
PAYLOAD = jnp.bfloat16
HI = lax.Precision.HIGHEST

NN = (((1,), (0,)), ((), ()))
NT = (((1,), (1,)), ((), ()))
TN = (((0,), (0,)), ((), ()))


def _dot(a, b, dims=NN):
    return lax.dot_general(a.astype(MXU), b.astype(MXU), dims, preferred_element_type=F32)


def _dot_exact(a, b):
    return lax.dot_general(a, b, NN, precision=HI, preferred_element_type=F32)


def _sigmoid(x):
    return 1.0 / (1.0 + jnp.exp(-x))


def _tile(dim, target, unit=128):
    best = None
    t = unit
    while t <= min(dim, target):
        if dim % t == 0:
            best = t
        t += unit
    return best if best is not None else dim


def _cparams(semantics, vmem_mib):
    return pltpu.CompilerParams(dimension_semantics=semantics, vmem_limit_bytes=vmem_mib * MIB)


def _mat_spec(arr, br, bc, rc_of_grid):
    if arr.ndim == 2:
        return pl.BlockSpec((br, bc), rc_of_grid)
    assert arr.shape[2] % bc == 0, (arr.shape, bc)
    per = arr.shape[2] // bc

    def idx(*g):
        r, c = rc_of_grid(*g)
        return (c // per, r, c % per)

    return pl.BlockSpec((None, br, bc), idx)


def _mat_shape(arr):
    return (arr.shape[0], arr.shape[1]) if arr.ndim == 2 else (arr.shape[1], arr.shape[0] * arr.shape[2])


def _matmul(a, b, dims, out_dtype, name, *, add=None, out_parts=1, tm=512, tn=512, tk=512, vmem_mib=48, sides=(),
            k_whole=False, rows=None, into=None):
    ar, ac = _mat_shape(a)
    br_, bc_ = _mat_shape(b)
    if dims == "nn":
        m, k, n = ar, ac, bc_
        assert br_ == k
    elif dims == "nt":
        m, k, n = ar, ac, br_
        assert bc_ == k
    else:
        m, k, n = ac, ar, bc_
        assert br_ == k
    m_unit, n_unit, k_unit = m, n, k
    if a.ndim == 3:
        if dims == "tn":
            m_unit = math.gcd(m_unit, a.shape[2])
        else:
            k_unit = math.gcd(k_unit, a.shape[2])
    if b.ndim == 3:
        if dims == "nt":
            k_unit = math.gcd(k_unit, b.shape[2])
        else:
            n_unit = math.gcd(n_unit, b.shape[2])
    if out_parts > 1:
        n_unit = math.gcd(n_unit, n // out_parts)
    tm = _tile(m_unit, tm, 128 if dims == "tn" else 64)
    tn, tk = _tile(n_unit, tn), _tile(k_unit, tk)
    whole_k = dims == "nt" and tk < k and k_whole
    if whole_k:
        k_chunk, tk = tk, k
    gm, gn, gk = m // tm, n // tn, k // tk
    assert gm * tm == m and gn * tn == n and gk * tk == k, (name, m, n, k, tm, tn, tk)

    def chunk_of(ref, arr, c):
        if arr.ndim == 2:
            return ref[:, c * k_chunk:(c + 1) * k_chunk]
        per = arr.shape[2] // k_chunk
        return ref[c // per, :, (c % per) * k_chunk:(c % per + 1) * k_chunk]

    def all_cols(arr, rows, row_of_grid):
        if arr.ndim == 2:
            return pl.BlockSpec((rows, arr.shape[1]), lambda i, j, kk: (row_of_grid(i, j), 0))
        return pl.BlockSpec((arr.shape[0], rows, arr.shape[2]), lambda i, j, kk: (0, row_of_grid(i, j), 0))

    if dims == "nn":
        a_spec = _mat_spec(a, tm, tk, lambda i, j, kk: (i, kk))
        b_spec = _mat_spec(b, tk, tn, lambda i, j, kk: (kk, j))
        dn = NN
    elif whole_k:
        a_spec = all_cols(a, tm, lambda i, j: i)
        b_spec = all_cols(b, tn, lambda i, j: j)
        dn = NT
    elif dims == "nt":
        a_spec = _mat_spec(a, tm, tk, lambda i, j, kk: (i, kk))
        b_spec = _mat_spec(b, tn, tk, lambda i, j, kk: (j, kk))
        dn = NT
    else:
        a_spec = _mat_spec(a, tk, tm, lambda i, j, kk: (kk, i))
        b_spec = _mat_spec(b, tk, tn, lambda i, j, kk: (kk, j))
        dn = TN

    if out_parts > 1:
        per = (n // out_parts) // tn
        out_shape = jax.ShapeDtypeStruct((out_parts, m, n // out_parts), out_dtype)
        o_spec = pl.BlockSpec((None, tm, tn), lambda i, j, kk: (j // per, i, j % per))
    else:
        out_shape = jax.ShapeDtypeStruct((m, n), out_dtype)
        o_spec = pl.BlockSpec((tm, tn), lambda i, j, kk: (i, j))

    in_specs = [a_spec, b_spec]
    args = [a, b]
    if add is not None:
        in_specs.append(pl.BlockSpec((tm, tn), lambda i, j, kk: (i, j)))
        args.append(add)
    if rows is not None:
        first_tile, gm = rows

        def shifted(spec):
            return pl.BlockSpec(spec.block_shape, lambda i, j, kk: spec.index_map(i + first_tile, j, kk))

        in_specs = [shifted(s) for s in in_specs]
        o_spec = shifted(o_spec)
    aliases = {}
    if into is not None:
        aliases[len(args)] = 0
        in_specs.append(_any_spec())
        args.append(into)

    def body(*refs):
        a_ref, b_ref, *others, o_ref, acc_ref = refs
        add_ref = others[0] if add is not None else None
        kk = pl.program_id(2)
        if whole_k:
            part = sum(lax.dot_general(chunk_of(a_ref, a, c).astype(MXU), chunk_of(b_ref, b, c).astype(MXU), dn,
                                       preferred_element_type=F32) for c in range(k // k_chunk))
        else:
            part = lax.dot_general(a_ref[...].astype(MXU), b_ref[...].astype(MXU), dn, preferred_element_type=F32)

        def finish(total):
            if add_ref is not None:
                total = total + add_ref[...]
            o_ref[...] = total.astype(o_ref.dtype)

        if gk == 1:
            finish(part)
        else:
            @pl.when(kk == 0)
            def _():
                acc_ref[...] = part

            @pl.when(jnp.logical_and(kk > 0, kk < gk - 1))
            def _():
                acc_ref[...] += part

            @pl.when(kk == gk - 1)
            def _():
                finish(acc_ref[...] + part)

    (out,), side_results = _pallas(
        body,
        name=name,
        args=args,
        grid=(gm, gn, gk),
        in_specs=in_specs,
        out_specs=[o_spec],
        out_shape=[out_shape],
        scratch_shapes=[pltpu.VMEM((tm, tn) if gk > 1 else (8, 128), F32)],
        semantics=("parallel", "parallel", "arbitrary"),
        vmem_mib=vmem_mib,
        sides=sides,
        aliases=aliases,
    )
    return (out, side_results) if sides else out


def _rms_fwd(h, gains, name, tm=384):
    lp, d = h.shape
    tm = _tile(lp, tm)
    n = len(gains)

    def body(*refs):
        h_ref = refs[0]
        g_refs = refs[1:1 + n]
        o_refs = refs[1 + n:]
        x = h_ref[...]
        y = x * lax.rsqrt(jnp.mean(x * x, axis=-1, keepdims=True) + EPS)
        for g_ref, o_ref in zip(g_refs, o_refs):
            o_ref[...] = (y * g_ref[...]).astype(o_ref.dtype)

    row = pl.BlockSpec((tm, d), lambda i: (i, 0))
    vec = pl.BlockSpec((1, d), lambda i: (0, 0))
    return pl.pallas_call(
        body,
        name=name,
        grid=(lp // tm,),
        in_specs=[row] + [vec] * n,
        out_specs=[row] * n,
        out_shape=[jax.ShapeDtypeStruct((lp, d), MXU)] * n,
        compiler_params=_cparams(("parallel",), 40),
    )(h, *gains)


def _rms_bwd(h, gains, dys, res, name, tm=384):
    lp, d = h.shape
    tm = _tile(lp, tm)
    n = len(gains)

    def body(*refs):
        h_ref, res_ref = refs[0], refs[1]
        g_refs = refs[2:2 + n]
        dy_refs = refs[2 + n:2 + 2 * n]
        dh_ref = refs[2 + 2 * n]
        dg_refs = refs[3 + 2 * n:]
        i = pl.program_id(0)
        x = h_ref[...]
        rstd = lax.rsqrt(jnp.mean(x * x, axis=-1, keepdims=True) + EPS)
        xhat = x * rstd
        dh = res_ref[...]
        for g_ref, dy_ref, dg_ref in zip(g_refs, dy_refs, dg_refs):
            dy = dy_ref[...]
            gdy = dy * g_ref[...]
            dh = dh + rstd * (gdy - xhat * jnp.mean(gdy * xhat, axis=-1, keepdims=True))
            part = jnp.sum(dy * xhat, axis=0, keepdims=True)

            @pl.when(i == 0)
            def _():
                dg_ref[...] = part

            @pl.when(i > 0)
            def _():
                dg_ref[...] += part

        dh_ref[...] = dh

    row = pl.BlockSpec((tm, d), lambda i: (i, 0))
    vec = pl.BlockSpec((1, d), lambda i: (0, 0))
    outs = pl.pallas_call(
        body,
        name=name,
        grid=(lp // tm,),
        in_specs=[row, row] + [vec] * n + [row] * n,
        out_specs=[row] + [vec] * n,
        out_shape=[jax.ShapeDtypeStruct((lp, d), F32)] + [jax.ShapeDtypeStruct((1, d), F32)] * n,
        compiler_params=_cparams(("arbitrary",), 56),
    )(h, res, *gains, *dys)
    return outs[0], list(outs[1:])


def _lb_fwd(gamma):
    def body(g_ref, lb_ref):
        g = g_ref[...]
        e = jnp.exp(g - jnp.max(g, axis=0, keepdims=True))
        lb_ref[...] = (e / jnp.sum(e, axis=0, keepdims=True))[0:1, :]

    return pl.pallas_call(body, name="lb_fwd", out_shape=jax.ShapeDtypeStruct((1, gamma.shape[1]), F32))(gamma)


def _lb_bwd(gamma, dlb):
    def body(g_ref, dlb_ref, dg_ref):
        g = g_ref[...]
        e = jnp.exp(g - jnp.max(g, axis=0, keepdims=True))
        s = e / jnp.sum(e, axis=0, keepdims=True)
        rows = lax.broadcasted_iota(jnp.int32, g.shape, 0)
        ds = jnp.where(rows == 0, dlb_ref[...], 0.0)
        dg_ref[...] = s * (ds - jnp.sum(s * ds, axis=0, keepdims=True))

    return pl.pallas_call(body, name="lb_bwd", out_shape=jax.ShapeDtypeStruct(gamma.shape, F32))(gamma, dlb)


def _tri(n, lower):
    r = lax.broadcasted_iota(jnp.int32, (n, n), 0)
    c = lax.broadcasted_iota(jnp.int32, (n, n), 1)
    return jnp.where((r >= c) if lower else (r <= c), 1.0, 0.0).astype(F32)


def _group(nc):
    return max(u for u in (1, 2, 3, 6, 11) if nc % u == 0)


def _running_sum(tri, x):
    hi = x.astype(MXU)
    rest = x - hi.astype(F32)
    mid = rest.astype(MXU)
    lo = (rest - mid.astype(F32)).astype(MXU)
    return _dot(tri, hi) + _dot(tri, mid) + _dot(tri, lo)


def _causal(n):
    r = lax.broadcasted_iota(jnp.int32, (n, n), 0)
    c = lax.broadcasted_iota(jnp.int32, (n, n), 1)
    return r >= c


def _chunk_gates(u_ref, lb, c):
    sl = pl.ds(pl.multiple_of(c * CHUNK, CHUNK), CHUNK)
    valid = (c * CHUNK + lax.broadcasted_iota(jnp.int32, (CHUNK, HEAD), 0)) >= ROW_PAD
    uq = u_ref[0, sl, :]
    uf = u_ref[1, sl, :]
    sq = _sigmoid(uq)
    sf = _sigmoid(uf)
    fg = lb + (1.0 - lb) * sf
    return dict(sl=sl, valid=valid, uq=uq, sq=sq, sf=sf, fg=fg, q=jnp.where(valid, uq * sq, 0.0),
                logf=jnp.where(valid, jnp.log(fg), 0.0), k=jnp.where(valid, 1.0 - fg, 0.0),
                v=jnp.where(valid, u_ref[2, sl, :], 0.0))


def _chunk_decays(x, b):
    b_last = b[CHUNK - 1:CHUNK, :]
    b_mid = b[CHUNK // 2 - 1:CHUNK // 2, :]
    e_qi = jnp.exp(b - b_mid)
    e_ki = jnp.exp(b_mid - b)
    e_kd = jnp.exp(b_last - b)
    e_qe = jnp.exp(b)
    q, k = x["q"], x["k"]
    return dict(x, e_qi=e_qi, e_ki=e_ki, e_kd=e_kd, e_qe=e_qe, qi=q * e_qi, ki=k * e_ki, kd=k * e_kd, qe=q * e_qe,
                decay=jnp.exp(b_last))


def _chunks(u_ref, lb, tri_lower, cs):
    gates = [_chunk_gates(u_ref, lb, c) for c in cs]
    sums = [_running_sum(tri_lower, x["logf"]) for x in gates]
    return [_chunk_decays(x, b) for x, b in zip(gates, sums)]


def _hgrn2_fwd(u4, lb, g_out, name="hgrn2_fwd", sides=()):
    _, lp, d = u4.shape
    nh, nc = d // HEAD, lp // CHUNK
    per = _group(nc)

    def body(u_ref, lb_ref, g_ref, o_ref, og_ref):
        lb_v = lb_ref[...]
        g = g_ref[...]

        tri_lower = _tri(CHUNK, True).astype(MXU)
        causal = _causal(CHUNK)

        def step(i, st):
            xs = _chunks(u_ref, lb_v, tri_lower, [i * per + u for u in range(per)])
            scores = [_dot(x["qi"], x["ki"], NT) for x in xs]
            updates = [_dot(x["v"], x["kd"], TN) for x in xs]
            states = []
            for x, upd in zip(xs, updates):
                states.append(st)
                st = x["decay"] * st + upd
            outs = [_dot(jnp.where(causal, a, 0.0), x["v"]) + _dot(x["qe"], s, NT)
                    for x, a, s in zip(xs, scores, states)]
            for x, o in zip(xs, outs):
                o_ref[x["sl"], :] = o
                on = o * lax.rsqrt(jnp.mean(o * o, axis=-1, keepdims=True) + EPS) * g
                z = u_ref[3, x["sl"], :]
                og_ref[x["sl"], :] = (on * (z * _sigmoid(z))).astype(og_ref.dtype)
            return st

        lax.fori_loop(0, nc // per, step, jnp.zeros((HEAD, HEAD), F32))

    slab = pl.BlockSpec((lp, HEAD), lambda h: (0, h))
    vec = pl.BlockSpec((1, HEAD), lambda h: (0, h))
    outs, side_results = _pallas(
        body,
        name=name,
        args=(u4, lb, g_out),
        grid=(nh,),
        in_specs=[pl.BlockSpec((4, lp, HEAD), lambda h: (0, 0, h)), vec, vec],
        out_specs=[slab, slab],
        out_shape=[jax.ShapeDtypeStruct((lp, d), F32), jax.ShapeDtypeStruct((lp, d), MXU)],
        semantics=("parallel",),
        vmem_mib=48,
        sides=sides,
    )
    return (*outs, side_results) if sides else tuple(outs)


def _hgrn2_bwd(u4, o, d_og, lb, g_out, name="hgrn2_bwd", sides=()):
    _, lp, d = u4.shape
    nh, nc = d // HEAD, lp // CHUNK
    per = _group(nc)

    def body(u_ref, o_ref, dog_ref, lb_ref, g_ref, du_ref, dlb_ref, dg_ref, st_ref, do_ref):
        lb_v = lb_ref[...]
        g = g_ref[...]

        tri_lower = _tri(CHUNK, True).astype(MXU)
        tri_upper = _tri(CHUNK, False).astype(MXU)
        causal = _causal(CHUNK)

        def fwd_step(i, carry):
            st, dg_acc = carry
            cs = [i * per + u for u in range(per)]
            xs = _chunks(u_ref, lb_v, tri_lower, cs)
            updates = [_dot(x["v"], x["kd"], TN) for x in xs]
            for c, x, upd in zip(cs, xs, updates):
                st_ref[c] = st
                st = x["decay"] * st + upd
            for x in xs:
                sl = x["sl"]
                ov = o_ref[sl, :]
                rstd = lax.rsqrt(jnp.mean(ov * ov, axis=-1, keepdims=True) + EPS)
                on = ov * rstd
                z = u_ref[3, sl, :]
                sz = _sigmoid(z)
                dog = dog_ref[sl, :]
                dy = dog * (z * sz)
                dz = dog * (on * g) * (sz * (1.0 + z * (1.0 - sz)))
                du_ref[3, sl, :] = dz.astype(du_ref.dtype)
                gdy = dy * g
                do = rstd * (gdy - on * jnp.mean(gdy * on, axis=-1, keepdims=True))
                do_ref[sl, :] = jnp.where(x["valid"], do, 0.0)
                dg_acc = dg_acc + jnp.sum(dy * on, axis=0, keepdims=True)
            return st, dg_acc

        _, dg_tot = lax.fori_loop(0, nc // per, fwd_step, (jnp.zeros((HEAD, HEAD), F32), jnp.zeros((1, HEAD), F32)))
        dg_ref[...] = dg_tot

        def bwd_step(i, carry):
            gt, dlb_acc = carry
            cs = [nc - 1 - (i * per + u) for u in range(per)]
            xs = _chunks(u_ref, lb_v, tri_lower, cs)
            dos = [do_ref[x["sl"], :] for x in xs]
            sts = [st_ref[c] for c in cs]
            scores = [jnp.where(causal, _dot(x["qi"], x["ki"], NT), 0.0) for x in xs]
            d_scores = [jnp.where(causal, _dot(do, x["v"], NT), 0.0) for x, do in zip(xs, dos)]
            d_qes = [_dot(do, st) for do, st in zip(dos, sts)]
            g_updates = [_dot(do, x["qe"], TN) for x, do in zip(xs, dos)]
            gts = []
            for x, upd in zip(xs, g_updates):
                gts.append(gt)
                gt = x["decay"] * gt + upd
            d_kds = [_dot(x["v"], g_) for x, g_ in zip(xs, gts)]
            dvs = [_dot(x["kd"], g_, NT) + _dot(a, do, TN) for x, g_, a, do in zip(xs, gts, scores, dos)]
            d_qis = [_dot(d_a, x["ki"]) for x, d_a in zip(xs, d_scores)]
            d_kis = [_dot(d_a, x["qi"], TN) for x, d_a in zip(xs, d_scores)]
            rows = lax.broadcasted_iota(jnp.int32, (CHUNK, HEAD), 0)
            dbs = []
            for x, g_, st, d_qi, d_ki, d_qe, d_kd in zip(xs, gts, sts, d_qis, d_kis, d_qes, d_kds):
                t_qi, t_ki, t_qe, t_kd = d_qi * x["qi"], d_ki * x["ki"], d_qe * x["qe"], d_kd * x["kd"]
                d_decay = jnp.sum(g_ * st, axis=0, keepdims=True)
                d_mid = jnp.sum(t_ki - t_qi, axis=0, keepdims=True)
                d_last = jnp.sum(t_kd, axis=0, keepdims=True) + d_decay * x["decay"]
                dbs.append(t_qi - t_ki + t_qe - t_kd + jnp.where(rows == CHUNK // 2 - 1, d_mid, 0.0)
                           + jnp.where(rows == CHUNK - 1, d_last, 0.0))
            dlogfs = [_running_sum(tri_upper, db) for db in dbs]
            for x, dlogf, dv, d_qi, d_ki, d_qe, d_kd in zip(xs, dlogfs, dvs, d_qis, d_kis, d_qes, d_kds):
                sl = x["sl"]
                dq = d_qi * x["e_qi"] + d_qe * x["e_qe"]
                dk = d_ki * x["e_ki"] + d_kd * x["e_kd"]
                valid, sq, sf, uq = x["valid"], x["sq"], x["sf"], x["uq"]
                dfg = jnp.where(valid, dlogf / x["fg"] - dk, 0.0)
                du_ref[0, sl, :] = jnp.where(valid, dq * (sq * (1.0 + uq * (1.0 - sq))), 0.0).astype(du_ref.dtype)
                du_ref[1, sl, :] = (dfg * (1.0 - lb_v) * (sf * (1.0 - sf))).astype(du_ref.dtype)
                du_ref[2, sl, :] = jnp.where(valid, dv, 0.0).astype(du_ref.dtype)
                dlb_acc = dlb_acc + jnp.sum(dfg * (1.0 - sf), axis=0, keepdims=True)
            return gt, dlb_acc

        _, dlb_tot = lax.fori_loop(0, nc // per, bwd_step, (jnp.zeros((HEAD, HEAD), F32), jnp.zeros((1, HEAD), F32)))
        dlb_ref[...] = dlb_tot

    slab = pl.BlockSpec((lp, HEAD), lambda h: (0, h))
    vec = pl.BlockSpec((1, HEAD), lambda h: (0, h))
    quad = pl.BlockSpec((4, lp, HEAD), lambda h: (0, 0, h))
    outs, side_results = _pallas(
        body,
        name=name,
        args=(u4, o, d_og, lb, g_out),
        grid=(nh,),
        in_specs=[quad, slab, slab, vec, vec],
        out_specs=[quad, vec, vec],
        out_shape=[jax.ShapeDtypeStruct((4, lp, d), MXU), jax.ShapeDtypeStruct((1, d), F32),
                   jax.ShapeDtypeStruct((1, d), F32)],
        scratch_shapes=[pltpu.VMEM((nc, HEAD, HEAD), F32), pltpu.VMEM((lp, HEAD), F32)],
        semantics=("parallel",),
        vmem_mib=58,
        sides=sides,
    )
    return (*outs, side_results) if sides else tuple(outs)


WIDE = 2 * HEAD
INV_SCALE = HEAD ** 0.5


def _split3(x):
    hi = x.astype(MXU).astype(F32)
    rest = x - hi
    mid = rest.astype(MXU).astype(F32)
    return hi, mid, (rest - mid).astype(MXU).astype(F32)


def _extra_cols(rows, first, second):
    lane = lax.broadcasted_iota(jnp.int32, (rows, HEAD), 1)
    out = jnp.where(lane < 6, 1.0, 0.0).astype(F32)
    for base, terms in ((0, first), (3, second)):
        if terms is not None:
            for j, term in enumerate(terms):
                out = jnp.where(lane == base + j, term, out)
    return out


def _head_col(a, h):
    lane = lax.broadcasted_iota(jnp.int32, a.shape, 1)
    return jnp.sum(jnp.where(lane == h, a, 0.0), axis=-1, keepdims=True)


def _attn_operands(ub, ukv, g_q, g_k, f_cum, name="attn_operands", tm=384):
    _, lp, d = ub.shape
    tm = _tile(lp, tm)
    nh = d // HEAD

    def body(q_ref, k_ref, v_ref, gq_ref, gk_ref, f_ref, qa_ref, ka_ref, va_ref):
        i = pl.program_id(0)
        f = f_ref[...]
        is_pad = (i * tm + lax.broadcasted_iota(jnp.int32, (tm, 1), 0)) < ROW_PAD
        ones_only = _extra_cols(tm, None, (0.0, 0.0, 0.0)).astype(MXU)
        for h in range(nh):
            hs = slice(h * HEAD, (h + 1) * HEAD)
            lo, hi = h * WIDE, h * WIDE + HEAD
            f_h = _head_col(f, h)
            for x_ref, g_ref, o_ref in ((q_ref, gq_ref, qa_ref), (k_ref, gk_ref, ka_ref)):
                x = x_ref[:, hs]
                y = x * lax.rsqrt(jnp.mean(x * x, axis=-1, keepdims=True) + EPS)
                o_ref[:, lo:hi] = (y * g_ref[:, hs]).astype(o_ref.dtype)
            qa_ref[:, hi:hi + HEAD] = _extra_cols(tm, _split3(f_h * INV_SCALE), None).astype(MXU)
            f_key = jnp.where(is_pad, -MASK_VALUE, f_h)
            ka_ref[:, hi:hi + HEAD] = _extra_cols(tm, None, _split3(-f_key * INV_SCALE)).astype(MXU)
            va_ref[:, lo:hi] = v_ref[:, hs].astype(MXU)
            va_ref[:, hi:hi + HEAD] = ones_only

    row = pl.BlockSpec((tm, d), lambda i: (i, 0))
    wide = pl.BlockSpec((tm, nh * WIDE), lambda i: (i, 0))
    vec = pl.BlockSpec((1, d), lambda i: (0, 0))
    return pl.pallas_call(
        body,
        name=name,
        grid=(lp // tm,),
        in_specs=[pl.BlockSpec((None, tm, d), lambda i: (0, i, 0)), pl.BlockSpec((None, tm, d), lambda i: (0, i, 0)),
                  pl.BlockSpec((None, tm, d), lambda i: (1, i, 0)), vec, vec, pl.BlockSpec((tm, 128), lambda i: (i, 0))],
        out_specs=[wide, wide, wide],
        out_shape=[jax.ShapeDtypeStruct((lp, nh * WIDE), MXU)] * 3,
        compiler_params=_cparams(("parallel",), 56),
    )(ub, ukv, ukv, g_q, g_k, f_cum)


def _head_rms_bwd_tile(x, g, dy):
    rstd = lax.rsqrt(jnp.mean(x * x, axis=-1, keepdims=True) + EPS)
    xhat = x * rstd
    gdy = dy * g
    dx = rstd * (gdy - xhat * jnp.mean(gdy * xhat, axis=-1, keepdims=True))
    return dx, jnp.sum(dy * xhat, axis=0, keepdims=True)


def _fgate_fwd(ufl, b_f):
    lp = ufl.shape[0]
    nb = lp // 128

    def body(u_ref, b_ref, f_ref):
        def step(i, carry):
            sl = pl.ds(pl.multiple_of(i * 128, 128), 128)
            valid = (i * 128 + lax.broadcasted_iota(jnp.int32, (128, 128), 0)) >= ROW_PAD
            x = u_ref[sl, :] + b_ref[...]
            logf = jnp.where(valid, jnp.minimum(x, 0.0) - jnp.log(1.0 + jnp.exp(-jnp.abs(x))), 0.0)
            f = _dot_exact(_tri(128, True), logf) + carry
            f_ref[sl, :] = f
            return f[127:128, :]

        lax.fori_loop(0, nb, step, jnp.zeros((1, 128), F32))

    return pl.pallas_call(body, name="fgate_fwd", out_shape=jax.ShapeDtypeStruct((lp, 128), F32))(ufl, b_f)


def _fgate_bwd(ufl, b_f, d_f):
    lp = ufl.shape[0]
    nb = lp // 128

    def body(u_ref, b_ref, df_ref, du_ref, db_ref):
        def step(j, carry):
            later, db_acc = carry
            i = nb - 1 - j
            sl = pl.ds(pl.multiple_of(i * 128, 128), 128)
            valid = (i * 128 + lax.broadcasted_iota(jnp.int32, (128, 128), 0)) >= ROW_PAD
            x = u_ref[sl, :] + b_ref[...]
            df = df_ref[sl, :]
            dlogf = _dot_exact(_tri(128, False), df) + later
            dx = jnp.where(valid, dlogf * _sigmoid(-x), 0.0)
            du_ref[sl, :] = dx.astype(du_ref.dtype)
            return later + jnp.sum(df, axis=0, keepdims=True), db_acc + jnp.sum(dx, axis=0, keepdims=True)

        _, db_tot = lax.fori_loop(0, nb, step, (jnp.zeros((1, 128), F32), jnp.zeros((1, 128), F32)))
        db_ref[...] = db_tot

    return pl.pallas_call(
        body, name="fgate_bwd",
        out_shape=[jax.ShapeDtypeStruct((lp, 128), MXU), jax.ShapeDtypeStruct((1, 128), F32)],
    )(ufl, b_f, d_f)


STRIP = 32
PAIR = 2


def _strip_causal(r, t):
    row = r + lax.broadcasted_iota(jnp.int32, (STRIP, t), 0)
    col = lax.broadcasted_iota(jnp.int32, (STRIP, t), 1)
    return col <= row


def _fox_fwd(qa, ka, va, ub, name="fox_fwd", t=384):
    lp = qa.shape[0]
    d = ub.shape[2]
    t = _tile(lp, t)
    nh, nq = d // HEAD, lp // t
    scale = HEAD ** -0.5

    assert nh % PAIR == 0
    heads = range(PAIR)

    def body(q_ref, k_ref, v_ref, z_ref, o_ref, olo_ref, og_ref, qb_ref,
             s_ref, p_ref, m_ref, a_ref, l_ref, acc_ref):
        qb = pl.program_id(1)
        m_ref[...] = jnp.full((PAIR, t, 1), MASK_VALUE, F32)
        l_ref[...] = jnp.zeros((PAIR, t, 128), F32)
        acc_ref[...] = jnp.zeros((PAIR, 2 * t, HEAD), F32)
        p_ref[1] = jnp.zeros((PAIR, 2 * t, t), MXU)
        a_ref[1] = jnp.ones((PAIR, t, 1), F32)

        def scores(kb, buf):
            ks = pl.ds(pl.multiple_of(kb * t, t), t)
            for j in heads:
                ws = slice(j * WIDE, (j + 1) * WIDE)
                s_ref[buf, j] = _dot(q_ref[:, ws], k_ref[ks, ws], NT)

        def weighted_sum(kb, buf):
            ks = pl.ds(pl.multiple_of(kb * t, t), t)
            for j in heads:
                pv = _dot(p_ref[buf, j], v_ref[ks, j * WIDE:j * WIDE + HEAD])
                alpha = a_ref[buf, j]
                acc_ref[j, 0:t, :] = alpha * acc_ref[j, 0:t, :] + pv[0:t]
                acc_ref[j, t:2 * t, :] = alpha * acc_ref[j, t:2 * t, :] + pv[t:2 * t]

        def softmax_update(buf, diagonal):
            for j in heads:
                for r in range(0, t, STRIP):
                    rs = slice(r, r + STRIP)
                    x = s_ref[buf, j, rs, :] * scale
                    if diagonal:
                        x = jnp.where(_strip_causal(r, t), x, MASK_VALUE)
                    m_old = m_ref[j, rs, :]
                    m_new = jnp.maximum(m_old, jnp.max(x, axis=-1, keepdims=True))
                    alpha = jnp.exp(m_old - m_new)
                    p = jnp.exp(x - m_new)
                    m_ref[j, rs, :] = m_new
                    a_ref[buf, j, rs, :] = alpha
                    l_ref[j, rs, :] = alpha * l_ref[j, rs, :] + sum(p[:, c:c + 128] for c in range(0, t, 128))
                    p_hi = p.astype(MXU)
                    p_ref[buf, j, rs, :] = p_hi
                    p_ref[buf, j, t + r:t + r + STRIP, :] = (p - p_hi.astype(F32)).astype(MXU)

        def off_diagonal(kb, cur):
            weighted_sum(jnp.maximum(kb - 1, 0), 1 - cur)
            scores(kb + 1, 1 - cur)
            softmax_update(cur, False)

        def diagonal(cur):
            weighted_sum(jnp.maximum(qb - 1, 0), 1 - cur)
            softmax_update(cur, True)
            weighted_sum(qb, cur)

        def two_blocks(i, carry):
            off_diagonal(2 * i, 0)
            off_diagonal(2 * i + 1, 1)
            return carry

        scores(0, 0)
        lax.fori_loop(0, qb // 2, two_blocks, 0)

        @pl.when(lax.rem(qb, 2) == 0)
        def _():
            diagonal(0)

        @pl.when(lax.rem(qb, 2) == 1)
        def _():
            off_diagonal(qb - 1, 0)
            diagonal(1)

        is_pad = (qb * t + lax.broadcasted_iota(jnp.int32, (t, 1), 0)) < ROW_PAD
        for j in heads:
            hs = slice(j * HEAD, (j + 1) * HEAD)
            l = jnp.sum(l_ref[j], axis=-1, keepdims=True)
            o = acc_ref[j, 0:t, :] / l
            o_ref[:, hs] = o
            olo_ref[:, hs] = acc_ref[j, t:2 * t, :] / l
            z = z_ref[:, hs]
            og_ref[:, hs] = (o * (z * _sigmoid(z))).astype(og_ref.dtype)
            extra = q_ref[:, j * WIDE + HEAD:(j + 1) * WIDE].astype(F32)
            f_scaled = extra[:, 0:1] + extra[:, 1:2] + extra[:, 2:3]
            log_term = jnp.where(is_pad, MASK_VALUE * INV_SCALE, f_scaled - (m_ref[j] + jnp.log(l)) * INV_SCALE)
            qb_ref[:, j * WIDE:j * WIDE + HEAD] = q_ref[:, j * WIDE:j * WIDE + HEAD]
            qb_ref[:, j * WIDE + HEAD:(j + 1) * WIDE] = _extra_cols(t, _split3(log_term), None).astype(qb_ref.dtype)

    scratch = [pltpu.VMEM((2, PAIR, t, t), F32), pltpu.VMEM((2, PAIR, 2 * t, t), MXU), pltpu.VMEM((PAIR, t, 1), F32),
               pltpu.VMEM((2, PAIR, t, 1), F32), pltpu.VMEM((PAIR, t, 128), F32), pltpu.VMEM((PAIR, 2 * t, HEAD), F32)]
    tile = pl.BlockSpec((t, PAIR * HEAD), lambda g, i: (i, g))
    wide_tile = pl.BlockSpec((t, PAIR * WIDE), lambda g, i: (i, g))
    wide_all = pl.BlockSpec((lp, PAIR * WIDE), lambda g, i: (0, g))
    return pl.pallas_call(
        body,
        name=name,
        grid=(nh // PAIR, nq),
        in_specs=[wide_tile, wide_all, wide_all, pl.BlockSpec((None, t, PAIR * HEAD), lambda g, i: (1, i, g))],
        out_specs=[tile, tile, tile, wide_tile],
        out_shape=[jax.ShapeDtypeStruct((lp, d), F32), jax.ShapeDtypeStruct((lp, d), F32),
                   jax.ShapeDtypeStruct((lp, d), MXU), jax.ShapeDtypeStruct((lp, nh * WIDE), MXU)],
        scratch_shapes=scratch,
        compiler_params=_cparams(("parallel", "arbitrary"), 48),
    )(qa, ka, va, ub)


def _fox_bwd(qb, ka, va, doa, d, name="fox_bwd", t=384, sides=()):
    lp = qb.shape[0]
    t = _tile(lp, t)
    nh, nk = d // HEAD, lp // t
    scale = HEAD ** -0.5

    assert nh % PAIR == 0
    heads = range(PAIR)

    def body(q_ref, do_ref, k_ref, v_ref, dq_ref, dk_ref, dv_ref, dfk_ref,
             s_ref, dp_ref, p_ref, ds_ref, col_ref):
        kb = pl.program_id(1)

        @pl.when(kb == 0)
        def _():
            dq_ref[...] = jnp.zeros_like(dq_ref)

        dk_ref[...] = jnp.zeros_like(dk_ref)
        dv_ref[...] = jnp.zeros_like(dv_ref)
        col_ref[...] = jnp.zeros_like(col_ref)
        p_ref[1] = jnp.zeros((PAIR, t, t), MXU)
        ds_ref[1] = jnp.zeros((PAIR, t, t), MXU)
        last = nk - 1 - kb

        def scores(i, buf):
            qs = pl.ds(pl.multiple_of((kb + i) * t, t), t)
            for j in heads:
                ws = slice(j * WIDE, (j + 1) * WIDE)
                s_ref[buf, j] = _dot(q_ref[qs, ws], k_ref[:, ws], NT)
                dp_ref[buf, j] = _dot(do_ref[qs, ws], v_ref[:, ws], NT)

        def gradients(i, buf):
            qs = pl.ds(pl.multiple_of((kb + jnp.maximum(i, 0)) * t, t), t)
            for j in heads:
                hs = slice(j * HEAD, (j + 1) * HEAD)
                narrow = slice(j * WIDE, j * WIDE + HEAD)
                dsb = ds_ref[buf, j]
                dv_ref[:, hs] += _dot(p_ref[buf, j], do_ref[qs, narrow], TN)
                dq_ref[qs, hs] += _dot(dsb, k_ref[:, narrow])
                dk_ref[:, hs] += _dot(dsb, q_ref[qs, narrow], TN)

        def elementwise(buf, diagonal):
            for j in heads:
                for r in range(0, t, STRIP):
                    rs = slice(r, r + STRIP)
                    x = s_ref[buf, j, rs, :] * scale
                    if diagonal:
                        x = jnp.where(_strip_causal(r, t), x, MASK_VALUE)
                    p = jnp.exp(x)
                    ds = p * dp_ref[buf, j, rs, :]
                    p_ref[buf, j, rs, :] = p.astype(MXU)
                    ds_ref[buf, j, rs, :] = (ds * scale).astype(MXU)
                    col_ref[j] += ds

        def step(i, cur, diagonal=False, more=True):
            gradients(i - 1, 1 - cur)
            if more:
                scores(i + 1, 1 - cur)
            elementwise(cur, diagonal)

        def two_blocks(n, carry):
            step(1 + 2 * n, 1)
            step(2 + 2 * n, 0)
            return carry

        scores(0, 0)

        @pl.when(last == 0)
        def _():
            step(0, 0, diagonal=True, more=False)
            gradients(0, 0)

        @pl.when(last > 0)
        def _():
            step(0, 0, diagonal=True)
            lax.fori_loop(0, (last - 1) // 2, two_blocks, 0)

            @pl.when(lax.rem(last, 2) == 1)
            def _():
                step(last, 1, more=False)
                gradients(last, 1)

            @pl.when(lax.rem(last, 2) == 0)
            def _():
                step(last - 1, 1)
                step(last, 0, more=False)
                gradients(last, 0)

        for j in heads:
            dfk_ref[j] = -jnp.sum(col_ref[j], axis=0, keepdims=True)

    scratch = [pltpu.VMEM((2, PAIR, t, t), F32), pltpu.VMEM((2, PAIR, t, t), F32), pltpu.VMEM((2, PAIR, t, t), MXU),
               pltpu.VMEM((2, PAIR, t, t), MXU), pltpu.VMEM((PAIR, STRIP, t), F32)]
    whole = pl.BlockSpec((lp, PAIR * HEAD), lambda g, j: (0, g))
    tile = pl.BlockSpec((t, PAIR * HEAD), lambda g, j: (j, g))
    wide_all = pl.BlockSpec((lp, PAIR * WIDE), lambda g, j: (0, g))
    wide_tile = pl.BlockSpec((t, PAIR * WIDE), lambda g, j: (j, g))
    outs, side_results = _pallas(
        body,
        name=name,
        args=(qb, doa, ka, va),
        grid=(nh // PAIR, nk),
        in_specs=[wide_all, wide_all, wide_tile, wide_tile],
        out_specs=[whole, tile, tile, pl.BlockSpec((PAIR, 1, t), lambda g, j: (g, 0, j))],
        out_shape=[jax.ShapeDtypeStruct((lp, d), F32)] * 3 + [jax.ShapeDtypeStruct((nh, 1, lp), F32)],
        scratch_shapes=scratch,
        semantics=("parallel", "arbitrary"),
        vmem_mib=48,
        sides=sides,
    )
    return (*outs, side_results) if sides else tuple(outs)


def _fox_do(d_og, o, o_lo, ub, name="fox_do", tm=384):
    lp, d = o.shape
    tm = _tile(lp, tm)
    nh = d // HEAD

    def body(dog_ref, o_ref, olo_ref, z_ref, doa_ref):
        for h in range(nh):
            hs = slice(h * HEAD, (h + 1) * HEAD)
            z = z_ref[:, hs]
            do = (dog_ref[:, hs] * (z * _sigmoid(z))).astype(doa_ref.dtype)
            delta = jnp.sum(do.astype(F32) * (o_ref[:, hs] + olo_ref[:, hs]), axis=-1, keepdims=True)
            doa_ref[:, h * WIDE:h * WIDE + HEAD] = do
            doa_ref[:, h * WIDE + HEAD:(h + 1) * WIDE] = _extra_cols(tm, _split3(-delta), (0.0, 0.0, 0.0)).astype(
                doa_ref.dtype)

    row = pl.BlockSpec((tm, d), lambda i: (i, 0))
    return pl.pallas_call(
        body,
        name=name,
        grid=(lp // tm,),
        in_specs=[row, row, row, pl.BlockSpec((None, tm, d), lambda i: (1, i, 0))],
        out_specs=pl.BlockSpec((tm, nh * WIDE), lambda i: (i, 0)),
        out_shape=jax.ShapeDtypeStruct((lp, nh * WIDE), MXU),
        compiler_params=_cparams(("parallel",), 56),
    )(d_og, o, o_lo, ub)


def _fox_q_bwd(ub, d_og, o, dqn, g_q, name="fox_q_bwd", tm=384):
    _, lp, d = ub.shape
    tm = _tile(lp, tm)
    nh = d // HEAD

    def body(ub_ref, dog_ref, o_ref, dqn_ref, g_ref, dub_ref, dg_ref):
        i = pl.program_id(0)
        for h in range(nh):
            hs = slice(h * HEAD, (h + 1) * HEAD)
            z = ub_ref[1, :, hs]
            sz = _sigmoid(z)
            dub_ref[1, :, hs] = (dog_ref[:, hs] * o_ref[:, hs] * (sz * (1.0 + z * (1.0 - sz)))).astype(dub_ref.dtype)
            dx, dg = _head_rms_bwd_tile(ub_ref[0, :, hs], g_ref[:, hs], dqn_ref[:, hs])
            dub_ref[0, :, hs] = dx.astype(dub_ref.dtype)

            @pl.when(i == 0)
            def _():
                dg_ref[:, hs] = dg

            @pl.when(i > 0)
            def _():
                dg_ref[:, hs] += dg

    row = pl.BlockSpec((tm, d), lambda i: (i, 0))
    pair = pl.BlockSpec((2, tm, d), lambda i: (0, i, 0))
    vec = pl.BlockSpec((1, d), lambda i: (0, 0))
    return pl.pallas_call(
        body,
        name=name,
        grid=(lp // tm,),
        in_specs=[pair, row, row, row, vec],
        out_specs=[pair, vec],
        out_shape=[jax.ShapeDtypeStruct((2, lp, d), MXU), jax.ShapeDtypeStruct((1, d), F32)],
        compiler_params=_cparams(("arbitrary",), 56),
    )(ub, d_og, o, dqn, g_q)


def _fox_k_bwd(ukv, dkn, dv, g_k, name="fox_k_bwd", tm=384):
    _, lp, d = ukv.shape
    tm = _tile(lp, tm)
    nh = d // HEAD

    def body(ukv_ref, dkn_ref, dv_ref, g_ref, du_ref, dg_ref):
        i = pl.program_id(0)
        du_ref[1] = dv_ref[...].astype(du_ref.dtype)
        for h in range(nh):
            hs = slice(h * HEAD, (h + 1) * HEAD)
            dx, dg = _head_rms_bwd_tile(ukv_ref[:, hs], g_ref[:, hs], dkn_ref[:, hs])
            du_ref[0, :, hs] = dx.astype(du_ref.dtype)

            @pl.when(i == 0)
            def _():
                dg_ref[:, hs] = dg

            @pl.when(i > 0)
            def _():
                dg_ref[:, hs] += dg

    row = pl.BlockSpec((tm, d), lambda i: (i, 0))
    vec = pl.BlockSpec((1, d), lambda i: (0, 0))
    return pl.pallas_call(
        body,
        name=name,
        grid=(lp // tm,),
        in_specs=[pl.BlockSpec((None, tm, d), lambda i: (0, i, 0)), row, row, vec],
        out_specs=[pl.BlockSpec((2, tm, d), lambda i: (0, i, 0)), vec],
        out_shape=[jax.ShapeDtypeStruct((2, lp, d), MXU), jax.ShapeDtypeStruct((1, d), F32)],
        compiler_params=_cparams(("arbitrary",), 48),
    )(ukv, dkn, dv, g_k)


def _loss(h, target, name="loss_head"):
    lp, d = h.shape
    nb = lp // 128

    def body(h_ref, t_ref, loss_ref, dh_ref, acc_ref):
        i = pl.program_id(0)

        @pl.when(i == 0)
        def _():
            acc_ref[...] = jnp.zeros_like(acc_ref)
            dh_ref[...] = jnp.zeros_like(dh_ref)

        @pl.when(i > 0)
        def _():
            err = h_ref[...] - t_ref[...]
            dh_ref[...] = err * (1.0 / d)
            acc_ref[...] += jnp.sum(jnp.sum(err * err, axis=-1, keepdims=True) * (1.0 / d), axis=0, keepdims=True)

        @pl.when(i == nb - 1)
        def _():
            loss_ref[...] = 0.5 * acc_ref[...]

    return pl.pallas_call(
        body,
        name=name,
        grid=(nb,),
        in_specs=[pl.BlockSpec((128, d), lambda i: (i, 0)),
                  pl.BlockSpec((128, d), lambda i: (jnp.maximum(i - 1, 0), 0))],
        out_specs=[pl.BlockSpec((1, 1), lambda i: (0, 0)), pl.BlockSpec((128, d), lambda i: (i, 0))],
        out_shape=[jax.ShapeDtypeStruct((1, 1), F32), jax.ShapeDtypeStruct((lp, d), F32)],
        scratch_shapes=[pltpu.VMEM((1, 1), F32)],
        compiler_params=_cparams(("arbitrary",), 32),
    )(h, target)


def _adam_math(w, g, m, v):
    m = ADAM_B1 * m + (1.0 - ADAM_B1) * g
    v = ADAM_B2 * v + (1.0 - ADAM_B2) * (g * g)
    m_hat = m / (1.0 - ADAM_B1 ** ADAM_STEP)
    v_hat = v / (1.0 - ADAM_B2 ** ADAM_STEP)
    delta = -ADAM_LR * (m_hat / (jnp.sqrt(v_hat) + ADAM_EPS) + ADAM_WD * w)
    return delta, m, v


def _adamw(parts, w, m, v, name, tm=512):
    n, r, c = parts.shape
    tm = tm if r % tm == 0 else r

    def body(p_ref, w_ref, m_ref, v_ref, g_ref, d_ref, nm_ref, nv_ref):
        g = p_ref[0].astype(F32)
        for j in range(1, n):
            g = g + p_ref[j].astype(F32)
        g_ref[...] = g
        d_ref[...], nm_ref[...], nv_ref[...] = _adam_math(w_ref[...], g, m_ref[...], v_ref[...])

    row = pl.BlockSpec((tm, c), lambda i: (i, 0))
    return pl.pallas_call(
        body,
        name=name,
        grid=(r // tm,),
        in_specs=[pl.BlockSpec((n, tm, c), lambda i: (0, i, 0)), row, row, row],
        out_specs=[row] * 4,
        out_shape=[jax.ShapeDtypeStruct((r, c), F32)] * 4,
        compiler_params=_cparams(("parallel",), 48),
    )(parts, w, m, v)


def _place():
    x, y, c = lax.axis_index("x"), lax.axis_index("y"), lax.axis_index("c")
    return x, y, c


def _other_chips(x, y):
    return [(1 - x, y), (x, 1 - y), (1 - x, 1 - y)]


def _any_spec():
    return pl.BlockSpec(memory_space=pl.ANY)


class _Side:
    def __init__(self, ins, outs, sems, start, finish, aliases=None):
        self.ins, self.outs, self.sems = list(ins), list(outs), sems
        self.start, self.finish = start, finish
        self.aliases = dict(aliases or {})


def _pallas(body, *, name, out_shape, args=(), grid=(), in_specs=(), out_specs=(), scratch_shapes=(),
            semantics=(), vmem_mib=None, sides=(), aliases=None):
    n_in, n_out, n_scr = len(args), len(out_shape), len(scratch_shapes)
    side_ins = [a for s in sides for a in s.ins]
    side_outs = [o for s in sides for o in s.outs]
    side_sems = [pltpu.SemaphoreType.DMA((max(k, 1),)) for s in sides for k in s.sems]
    aliases, in_at, out_at = dict(aliases or {}), n_in, n_out
    for s in sides:
        aliases.update({in_at + i: out_at + o for i, o in s.aliases.items()})
        in_at, out_at = in_at + len(s.ins), out_at + len(s.outs)

    def wrapped(*refs):
        at = [0]

        def take(k):
            got = refs[at[0]:at[0] + k]
            at[0] += k
            return got

        main_in = take(n_in)
        s_in = [take(len(s.ins)) for s in sides]
        main_out = take(n_out)
        s_out = [take(len(s.outs)) for s in sides]
        main_scr = take(n_scr)
        s_sem = [take(3) for s in sides]

        def run(stage):
            for s, i_, o_, m_ in zip(sides, s_in, s_out, s_sem):
                getattr(s, stage)(i_, o_, *m_)

        first = last = None
        for k, g in enumerate(grid):
            i = pl.program_id(k)
            first = (i == 0) if first is None else jnp.logical_and(first, i == 0)
            last = (i == g - 1) if last is None else jnp.logical_and(last, i == g - 1)
        if sides:
            run("start") if first is None else pl.when(first)(lambda: run("start"))
        body(*main_in, *main_out, *main_scr)
        if sides:
            run("finish") if last is None else pl.when(last)(lambda: run("finish"))

    kw = {}
    if grid:
        kw["grid"] = grid
    if semantics or vmem_mib:
        sem = tuple("arbitrary" for _ in grid) if sides else tuple(semantics)
        kw["compiler_params"] = pltpu.CompilerParams(
            dimension_semantics=sem or None, vmem_limit_bytes=vmem_mib * MIB if vmem_mib else None)
    res = pl.pallas_call(
        wrapped,
        name=name,
        in_specs=list(in_specs) + [_any_spec()] * len(side_ins),
        out_specs=list(out_specs) + [_any_spec()] * len(side_outs),
        out_shape=list(out_shape) + side_outs,
        scratch_shapes=list(scratch_shapes) + side_sems,
        input_output_aliases=aliases,
        **kw,
    )(*args, *side_ins)
    main, rest, per_side = list(res[:n_out]), list(res[n_out:]), []
    for s in sides:
        per_side.append(rest[:len(s.outs)])
        rest = rest[len(s.outs):]
    return main, per_side


def _slot(p):
    return 4 * p[0] + 2 * p[1] + p[2]


def _sibling_side(grads):
    n = len(grads)

    def copies(ins, outs, send_sems, recv_sems):
        x, y, c = _place()
        return [pltpu.make_async_remote_copy(
            src_ref=ins[t].at[2 * chip + (1 - c)], dst_ref=outs[t].at[chip],
            send_sem=send_sems.at[4 * t + chip], recv_sem=recv_sems.at[4 * t + chip],
            device_id=(x, y, 1 - c), device_id_type=MESH) for t in range(n) for chip in range(N_CHIP)]

    def start(ins, outs, send_sems, recv_sems, local_sems):
        for cp in copies(ins, outs, send_sems, recv_sems):
            cp.start()

    def finish(ins, outs, send_sems, recv_sems, local_sems):
        for cp in copies(ins, outs, send_sems, recv_sems):
            cp.wait()

    outs = [jax.ShapeDtypeStruct((N_CHIP,) + g.shape[1:], g.dtype) for g in grads]
    return _Side(grads, outs, (4 * n, 4 * n, 0), start, finish)


def _chips_side(partials):
    n = len(partials)

    def copies(ins, outs, send_sems, recv_sems, local_sems):
        x, y, c = _place()
        my_chip = 2 * x + y
        local = [pltpu.make_async_copy(ins[t].at[my_chip], outs[t].at[my_chip], local_sems.at[t]) for t in range(n)]
        sends, recvs = [], []
        for t in range(n):
            for j, chip in enumerate(_other_chips(x, y)):
                their = 2 * chip[0] + chip[1]
                sems = dict(send_sem=send_sems.at[3 * t + j], recv_sem=recv_sems.at[3 * t + j],
                            device_id=(*chip, c), device_id_type=MESH)
                sends.append(pltpu.make_async_remote_copy(src_ref=ins[t].at[their], dst_ref=outs[t].at[my_chip], **sems))
                recvs.append(pltpu.make_async_remote_copy(src_ref=ins[t].at[my_chip], dst_ref=outs[t].at[their], **sems))
        return local, sends, recvs

    def start(ins, outs, send_sems, recv_sems, local_sems):
        local, sends, _ = copies(ins, outs, send_sems, recv_sems, local_sems)
        for cp in local + sends:
            cp.start()

    def finish(ins, outs, send_sems, recv_sems, local_sems):
        local, sends, recvs = copies(ins, outs, send_sems, recv_sems, local_sems)
        for cp in sends:
            cp.wait_send()
        for cp in recvs:
            cp.wait_recv()
        for cp in local:
            cp.wait()

    outs = [jax.ShapeDtypeStruct(p.shape, p.dtype) for p in partials]
    return _Side(partials, outs, (3 * n, 3 * n, n), start, finish)


def _gather_own_side(blocks):
    n = len(blocks)

    def copies(ins, outs, send_sems, recv_sems, local_sems):
        x, y, c = _place()
        me = (x, y, c)
        peers = [(x, y, 1 - c)] + [(*chip, c) for chip in _other_chips(x, y)]
        local = [pltpu.make_async_copy(ins[t], outs[t].at[_slot(me)], local_sems.at[t]) for t in range(n)]
        sends, recvs = [], []
        for t in range(n):
            for k, peer in enumerate(peers):
                sems = dict(send_sem=send_sems.at[4 * t + k], recv_sem=recv_sems.at[4 * t + k],
                            device_id=peer, device_id_type=MESH)
                sends.append(pltpu.make_async_remote_copy(src_ref=ins[t], dst_ref=outs[t].at[_slot(me)], **sems))
                recvs.append(pltpu.make_async_remote_copy(src_ref=ins[t], dst_ref=outs[t].at[_slot(peer)], **sems))
        return local, sends, recvs

    def start(ins, outs, send_sems, recv_sems, local_sems):
        local, sends, _ = copies(ins, outs, send_sems, recv_sems, local_sems)
        for cp in local + sends:
            cp.start()

    def finish(ins, outs, send_sems, recv_sems, local_sems):
        local, sends, recvs = copies(ins, outs, send_sems, recv_sems, local_sems)
        for cp in sends:
            cp.wait_send()
        for cp in recvs:
            cp.wait_recv()
        for cp in local:
            cp.wait()

    outs = [jax.ShapeDtypeStruct((N_DEV,) + b.shape, b.dtype) for b in blocks]
    return _Side(blocks, outs, (4 * n, 4 * n, n), start, finish)


def _gather_pass_side(gathered):
    n = len(gathered)

    def copies(outs, send_sems, recv_sems):
        x, y, c = _place()
        sends, recvs = [], []
        for t in range(n):
            for j, chip in enumerate(_other_chips(x, y)):
                sems = dict(send_sem=send_sems.at[3 * t + j], recv_sem=recv_sems.at[3 * t + j],
                            device_id=(x, y, 1 - c), device_id_type=MESH)
                mine, theirs = outs[t].at[_slot((*chip, c))], outs[t].at[_slot((*chip, 1 - c))]
                sends.append(pltpu.make_async_remote_copy(src_ref=mine, dst_ref=mine, **sems))
                recvs.append(pltpu.make_async_remote_copy(src_ref=mine, dst_ref=theirs, **sems))
        return sends, recvs

    def start(ins, outs, send_sems, recv_sems, local_sems):
        for cp in copies(outs, send_sems, recv_sems)[0]:
            cp.start()

    def finish(ins, outs, send_sems, recv_sems, local_sems):
        sends, recvs = copies(outs, send_sems, recv_sems)
        for cp in sends:
            cp.wait_send()
        for cp in recvs:
            cp.wait_recv()

    outs = [jax.ShapeDtypeStruct(g.shape, g.dtype) for g in gathered]
    return _Side(gathered, outs, (3 * n, 3 * n, 0), start, finish, aliases={t: t for t in range(n)})


def _alone(side, name):
    return _pallas(lambda: None, name=name, out_shape=[], sides=[side])[1][0]


def _all_gather(blocks, name):
    n = len(blocks)

    def body(*refs):
        in_refs, out_refs = refs[:n], refs[n:2 * n]
        send_sems, recv_sems, local_sems = refs[2 * n:]
        x, y, c = _place()
        me, sibling = (x, y, c), (x, y, 1 - c)
        chips = _other_chips(x, y)

        def slot(p):
            return 4 * p[0] + 2 * p[1] + p[2]

        def copy(t, k, block, to, src=None):
            dst = out_refs[t].at[slot(block)]
            return pltpu.make_async_remote_copy(
                src_ref=dst if src is None else src, dst_ref=dst,
                send_sem=send_sems.at[7 * t + k], recv_sem=recv_sems.at[7 * t + k],
                device_id=to, device_id_type=MESH)

        started = []
        for t in range(n):
            mine = pltpu.make_async_copy(in_refs[t], out_refs[t].at[slot(me)], local_sems.at[t])
            mine.start()
            started.append(mine)
        sends = []
        for t in range(n):
            first = [copy(t, 0, me, sibling, src=in_refs[t])]
            first += [copy(t, 1 + j, me, (*chip, c), src=in_refs[t]) for j, chip in enumerate(chips)]
            for cp in first:
                cp.start()
            sends += first
        for t in range(n):
            for j, chip in enumerate(chips):
                copy(t, 1 + j, (*chip, c), me).wait_recv()
                passed = copy(t, 4 + j, (*chip, c), sibling)
                passed.start()
                sends.append(passed)
        for t in range(n):
            copy(t, 0, sibling, me).wait_recv()
            for j, chip in enumerate(chips):
                copy(t, 4 + j, (*chip, 1 - c), me).wait_recv()
        for cp in sends:
            cp.wait_send()
        for mine in started:
            mine.wait()

    return pl.pallas_call(
        body,
        name=name,
        in_specs=[_any_spec()] * n,
        out_specs=[_any_spec()] * n,
        out_shape=[jax.ShapeDtypeStruct((N_DEV,) + b.shape, b.dtype) for b in blocks],
        scratch_shapes=[pltpu.SemaphoreType.DMA((7 * n,)), pltpu.SemaphoreType.DMA((7 * n,)),
                        pltpu.SemaphoreType.DMA((n,))],
    )(*blocks)


def _pair_sum(g8, got, name, tm=1024):
    _, r, c = g8.shape
    tm = tm if r % tm == 0 else r
    core = lax.axis_index("c")

    def body(core_ref, mine_ref, got_ref, o_ref):
        south_first = core_ref[0] == 0
        a, b = mine_ref[...], got_ref[...]
        o_ref[...] = (jnp.where(south_first, a, b) + jnp.where(south_first, b, a)).astype(o_ref.dtype)

    return pl.pallas_call(
        body,
        name=name,
        grid_spec=pltpu.PrefetchScalarGridSpec(
            num_scalar_prefetch=1,
            grid=(N_CHIP, r // tm),
            in_specs=[pl.BlockSpec((None, tm, c), lambda j, i, core_ref: (2 * j + core_ref[0], i, 0)),
                      pl.BlockSpec((None, tm, c), lambda j, i, core_ref: (j, i, 0))],
            out_specs=pl.BlockSpec((None, tm, c), lambda j, i, core_ref: (j, i, 0)),
        ),
        out_shape=jax.ShapeDtypeStruct((N_CHIP, r, c), PAYLOAD),
        compiler_params=_cparams(("parallel", "parallel"), 32),
    )(jnp.reshape(core, (1,)).astype(jnp.int32), g8, got)


def _all_reduce_small(s, name="small_all_reduce"):
    r, c = s.shape

    def body(s_ref, o_ref, buf_ref, send_sems, recv_sems):
        x, y, c_ = _place()
        me = 4 * x + 2 * y + c_
        buf_ref[me] = s_ref[...]

        def copy(k, slot, peer):
            return pltpu.make_async_remote_copy(
                src_ref=s_ref, dst_ref=buf_ref.at[slot],
                send_sem=send_sems.at[k - 1], recv_sem=recv_sems.at[k - 1],
                device_id=peer, device_id_type=MESH)

        peers = []
        for k in range(1, N_DEV):
            peer = (x ^ ((k >> 2) & 1), y ^ ((k >> 1) & 1), c_ ^ (k & 1))
            peers.append(peer)
            copy(k, me, peer).start()
        for k, peer in zip(range(1, N_DEV), peers):
            cp = copy(k, 4 * peer[0] + 2 * peer[1] + peer[2], peer)
            cp.wait_send()
            cp.wait_recv()
        total = buf_ref[0]
        for j in range(1, N_DEV):
            total = total + buf_ref[j]
        o_ref[...] = total

    return pl.pallas_call(
        body,
        name=name,
        out_shape=jax.ShapeDtypeStruct((r, c), F32),
        in_specs=[pl.BlockSpec(memory_space=pltpu.VMEM)],
        out_specs=pl.BlockSpec(memory_space=pltpu.VMEM),
        scratch_shapes=[pltpu.VMEM((N_DEV, r, c), F32), pltpu.SemaphoreType.DMA((N_DEV - 1,)),
                        pltpu.SemaphoreType.DMA((N_DEV - 1,))],
    )(s)


def _late_weights(g_kv, g_b_in, g_b_out):
    d = g_kv.shape[1]
    nh = d // HEAD
    wkv = jnp.reshape(jnp.transpose(g_kv, (1, 0, 2)), (d, -1))
    return wkv[:, :2 * d], jnp.pad(wkv[:, 2 * d:], ((0, 0), (0, 128 - nh))), g_b_in, jnp.reshape(g_b_out, (d, d))


def _kv_grad_blocks(dwkv2, dwfl):
    d = dwkv2.shape[0]
    dwkv = jnp.concatenate([dwkv2, dwfl[:, :d // HEAD]], axis=1)
    return jnp.transpose(jnp.reshape(dwkv, (d, N_DEV, -1)), (1, 0, 2))


def _local_step(x, target, meta, gamma, a_norm, wa_in, a_out_norm, wa_out, kv_norm, late, b_f, g_k, b_norm, g_q,
                dist):
    d = x.shape[1]
    nh = d // HEAD
    cols = d // N_DEV
    h0 = jnp.concatenate([jnp.zeros((ROW_PAD, d), F32), meta, x], axis=0)
    lp = h0.shape[0]
    big = dict(tm=1408, tn=512, tk=2048)

    lb = _lb_fwd(gamma)
    (hn_a,) = _rms_fwd(h0, [a_norm], "rms_a")
    if dist:
        a_out_blk, kv_blk, b_in_blk, b_out_blk = late
        u4, (first,) = _matmul(hn_a, wa_in, "nn", F32, "a_in", out_parts=4,
                               sides=[_gather_own_side([a_out_blk, b_in_blk, b_out_blk])], **big)
        o_a, og_a, ((g_a_out, g_b_in, g_b_out), (g_kv,)) = _hgrn2_fwd(
            u4, lb, a_out_norm, sides=[_gather_pass_side(first), _gather_own_side([kv_blk])])
        wa_out = jnp.reshape(g_a_out, (d, d))
        h1, ((g_kv,),) = _matmul(og_a, wa_out, "nn", F32, "a_out", add=h0, sides=[_gather_pass_side([g_kv])], **big)
        wkv2, wfl, wb_in, wb_out = _late_weights(g_kv, g_b_in, g_b_out)
    else:
        u4 = _matmul(hn_a, wa_in, "nn", F32, "a_in", out_parts=4, **big)
        o_a, og_a = _hgrn2_fwd(u4, lb, a_out_norm)
        h1 = _matmul(og_a, wa_out, "nn", F32, "a_out", add=h0, **big)
        wkv2, wfl, wb_in, wb_out = late
    hk, hb = _rms_fwd(h1, [kv_norm, b_norm], "rms_kv_b")
    ukv = _matmul(hk, wkv2, "nn", F32, "kv_in", out_parts=2, **big)
    ufl = _matmul(hk, wfl, "nn", F32, "kv_f", **big)
    ub = _matmul(hb, wb_in, "nn", F32, "b_in", out_parts=2, **big)
    f_cum = _fgate_fwd(ufl, b_f)
    qa, ka, va = _attn_operands(ub, ukv, g_q, g_k, f_cum)
    o_b, o_lo, og_b, qb = _fox_fwd(qa, ka, va, ub)
    h2 = _matmul(og_b, wb_out, "nn", F32, "b_out", add=h1, **big)
    loss, dh2 = _loss(h2, target)

    dx_t = dict(tm=1408, tn=512, tk=2048, vmem_mib=56)
    dx_parts_t = dict(tm=704, tn=512, tk=2048, vmem_mib=58, k_whole=True)
    dw_t = dict(tm=512, tn=1024, tk=lp, vmem_mib=58)
    dw_f32_t = dict(tm=1024, tn=512, tk=lp, vmem_mib=58)
    def to_sibling(g8):
        return [_sibling_side([g8])] if dist else []

    def to_chips(partial):
        return [_chips_side([partial])] if dist else []

    def unpack(res, n_sides):
        if not dist:
            return res, [None] * n_sides
        *main, side_results = res
        return (main[0] if len(main) == 1 else tuple(main)), [r[0] for r in side_results]

    dwb_out = _matmul(og_b, dh2, "tn", F32, "b_out_dw", **dw_f32_t)
    g8_b_out = jnp.reshape(dwb_out, (N_DEV, cols, d))
    d_ogb, (got,) = unpack(_matmul(dh2, wb_out, "nt", F32, "b_out_dx", sides=to_sibling(g8_b_out), **dx_t), 1)
    p_b_out = _pair_sum(g8_b_out, got, "pair_sum_b_w_out") if dist else None
    doa = _fox_do(d_ogb, o_b, o_lo, ub)
    (dqn, dkn, dv, dfk), (r_b_out,) = unpack(_fox_bwd(qb, ka, va, doa, d, sides=to_chips(p_b_out)), 1)
    dub, dg_q = _fox_q_bwd(ub, d_ogb, o_b, dqn, g_q)
    dukv, dg_k = _fox_k_bwd(ukv, dkn, dv, g_k)
    d_f = jnp.pad(jnp.transpose(dfk[:, 0, :]), ((0, 0), (0, 128 - nh)))
    dufl, db_f = _fgate_bwd(ufl, b_f, d_f)
    dwb_in = _matmul(hb, dub, "tn", F32, "b_in_dw", out_parts=N_DEV, **dw_t)
    d_hb, (got,) = unpack(_matmul(dub, wb_in, "nt", F32, "b_in_dx", sides=to_sibling(dwb_in), **dx_parts_t), 1)
    p_b_in = _pair_sum(dwb_in, got, "pair_sum_b_w_in") if dist else None
    d_hk, (r_b_in,) = unpack(_matmul(dukv, wkv2, "nt", F32, "kv_dx", sides=to_chips(p_b_in), **dx_parts_t), 1)
    d_hk = _matmul(dufl, wfl, "nt", F32, "kv_f_dx", add=d_hk, **dx_t)
    dwkv2 = _matmul(hk, dukv, "tn", F32, "kv_dw", **dw_t)
    dwfl = _matmul(hk, dufl, "tn", F32, "kv_f_dw", **dw_t)
    g8_kv = _kv_grad_blocks(dwkv2, dwfl) if dist else None
    dh1, (dg_kv, dg_b) = _rms_bwd(h1, [kv_norm, b_norm], [d_hk, d_hb], dh2, "rms_kv_b_bwd")
    d_oga, (got,) = unpack(_matmul(dh1, wa_out, "nt", F32, "a_out_dx", sides=to_sibling(g8_kv), **dx_t), 1)
    p_kv = _pair_sum(g8_kv, got, "pair_sum_kv_w") if dist else None
    dwa_out = _matmul(og_a, dh1, "tn", F32, "a_out_dw", **dw_f32_t)
    g8_a_out = jnp.reshape(dwa_out, (N_DEV, cols, d))
    (du4, dlb, dg_aout), (r_kv, got) = unpack(
        _hgrn2_bwd(u4, o_a, d_oga, lb, a_out_norm, sides=to_chips(p_kv) + to_sibling(g8_a_out)), 2)
    p_a_out = _pair_sum(g8_a_out, got, "pair_sum_a_w_out") if dist else None
    dwa_in, (r_a_out,) = unpack(
        _matmul(hn_a, du4, "tn", F32, "a_in_dw", out_parts=N_DEV, sides=to_chips(p_a_out), **dw_t), 1)
    row_tiles = lp // _tile(lp, dx_parts_t["tm"], 64)
    if dist and row_tiles % 2 == 0:
        half = row_tiles // 2
        d_hna, ((got,),) = _matmul(du4, wa_in, "nt", F32, "a_in_dx_top", rows=(0, half),
                                   sides=to_sibling(dwa_in), **dx_parts_t)
        p_a_in = _pair_sum(dwa_in, got, "pair_sum_a_w_in")
        d_hna, ((r_a_in,),) = _matmul(du4, wa_in, "nt", F32, "a_in_dx_bottom", rows=(half, half), into=d_hna,
                                      sides=to_chips(p_a_in), **dx_parts_t)
    else:
        d_hna, (got,) = unpack(_matmul(du4, wa_in, "nt", F32, "a_in_dx", sides=to_sibling(dwa_in), **dx_parts_t), 1)
        if dist:
            (r_a_in,) = _alone(_chips_side([_pair_sum(dwa_in, got, "pair_sum_a_w_in")]), "grads_to_chips_a_w_in")
    dh0, (dg_a,) = _rms_bwd(h0, [a_norm], [d_hna], dh1, "rms_a_bwd")
    dgamma = _lb_bwd(gamma, dlb)

    grads = dict(meta=dh0[ROW_PAD:ROW_PAD + N_META], gamma=dgamma, a_norm=dg_a, a_out_norm=dg_aout, kv_norm=dg_kv,
                 b_f=db_f, g_k=dg_k, b_norm=dg_b, g_q=dg_q)
    if dist:
        grads.update(wa_in=r_a_in, wa_out=r_a_out, wkv=r_kv, wb_in=r_b_in, wb_out=r_b_out)
    else:
        grads.update(wa_in=dwa_in, wa_out=dwa_out, wkv2=dwkv2, wfl=dwfl, wb_in=dwb_in, wb_out=dwb_out)
    return loss, dh0[ROW_PAD + N_META:], grads


def kernel(x, meta, gamma_lb, a_norm, a_w_in, a_out_norm, a_w_out, kv_norm, kv_w, fox_b_f, fox_k_norm, b_norm, b_w_in, b_q_norm, b_w_out, loss_target, m_meta, m_gamma_lb, m_a_norm, m_a_w_in, m_a_out_norm, m_a_w_out, m_kv_norm, m_kv_w, m_fox_b_f, m_fox_k_norm, m_b_norm, m_b_w_in, m_b_q_norm, m_b_w_out, v_meta, v_gamma_lb, v_a_norm, v_a_w_in, v_a_out_norm, v_a_w_out, v_kv_norm, v_kv_w, v_fox_b_f, v_fox_k_norm, v_b_norm, v_b_w_in, v_b_q_norm, v_b_w_out):
    d = x.shape[-1]
    nh = d // HEAD
    cols = d // N_DEV
    me = 4 * lax.axis_index("x") + 2 * lax.axis_index("y") + lax.axis_index("c")

    sharded_small = jnp.concatenate([meta, gamma_lb, a_norm, a_out_norm, jnp.zeros((4, cols), F32)], axis=0)
    g_a_in, g_small = _all_gather([a_w_in[0].astype(MXU), sharded_small], "gather_weights")
    wa_out = None
    late = (a_w_out[0].astype(MXU), kv_w.astype(MXU), b_w_in[0].astype(MXU), b_w_out[0].astype(MXU))
    small = jnp.reshape(jnp.transpose(g_small, (1, 0, 2)), (-1, d))
    meta_f, gamma_f, a_norm_f, a_out_norm_f = small[:16], small[16:18], small[18:19], small[19:20]
    b_f = jnp.pad(jnp.reshape(fox_b_f, (1, nh)), ((0, 0), (0, 128 - nh)))
    g_k = jnp.reshape(fox_k_norm, (1, d))
    g_q = jnp.reshape(b_q_norm, (1, d))
    kv_norm_r = jnp.reshape(kv_norm, (1, d))

    loss, grad_x, g = _local_step(x[0], loss_target[0], meta_f, gamma_f, a_norm_f, g_a_in, a_out_norm_f, wa_out,
                                  kv_norm_r, late, b_f, g_k, b_norm, g_q, dist=True)
    loss = lax.psum(loss[0, 0], AXES)

    r_a_in = _adamw(g["wa_in"], a_w_in[0], m_a_w_in[0], v_a_w_in[0], "adamw_a_w_in")
    r_a_out = _adamw(g["wa_out"], a_w_out[0], m_a_w_out[0], v_a_w_out[0], "adamw_a_w_out")
    r_kv = _adamw(g["wkv"], kv_w, m_kv_w, v_kv_w, "adamw_kv_w")
    r_b_in = _adamw(g["wb_in"], b_w_in[0], m_b_w_in[0], v_b_w_in[0], "adamw_b_w_in")
    r_b_out = _adamw(g["wb_out"], b_w_out[0], m_b_w_out[0], v_b_w_out[0], "adamw_b_w_out")

    packed = jnp.concatenate(
        [g["meta"], g["gamma"], g["a_norm"], g["a_out_norm"], g["kv_norm"], g["b_norm"], g["g_k"], g["g_q"],
         jnp.pad(g["b_f"], ((0, 0), (0, d - 128))), jnp.zeros((7, d), F32)], axis=0)
    tot = _all_reduce_small(packed)
    mine = lax.dynamic_slice_in_dim(tot[:20], me * cols, cols, axis=1)
    gs = [mine[:16], mine[16:18], mine[18:19], mine[19:20], tot[20], tot[21:22], jnp.reshape(tot[22], (nh, HEAD)),
          jnp.reshape(tot[23], (1, nh, HEAD)), tot[24, :nh]]
    small_w = [meta, gamma_lb, a_norm, a_out_norm, kv_norm, b_norm, fox_k_norm, b_q_norm, fox_b_f]
    small_m = [m_meta, m_gamma_lb, m_a_norm, m_a_out_norm, m_kv_norm, m_b_norm, m_fox_k_norm, m_b_q_norm, m_fox_b_f]
    small_v = [v_meta, v_gamma_lb, v_a_norm, v_a_out_norm, v_kv_norm, v_b_norm, v_fox_k_norm, v_b_q_norm, v_fox_b_f]

    def pack(arrs):
        rows = [jnp.reshape(a, (-1, cols)) for a in arrs[:8]]
        rows.append(jnp.pad(jnp.reshape(arrs[8], (1, nh)), ((0, 0), (0, cols - nh))))
        n_rows = sum(r.shape[0] for r in rows)
        return jnp.concatenate(rows + [jnp.zeros((-n_rows % 8, cols), F32)], axis=0)

    _, sd, sm, sv = _adamw(pack(gs)[None], pack(small_w), pack(small_m), pack(small_v), "adamw_small")

    def unpack(p):
        out, at = [], 0
        for a in small_w[:8]:
            n = a.size // cols
            out.append(jnp.reshape(p[at:at + n], a.shape))
            at += n
        out.append(p[at, :nh])
        return out

    d_s, m_s, v_s = unpack(sd), unpack(sm), unpack(sv)

    def ordered(s, a_in, a_out, kv, b_in, b_out):
        return [s[0], s[1], s[2], a_in[None], s[3], a_out[None], s[4], kv, s[8], s[6], s[5], b_in[None], s[7], b_out[None]]

    outs = []
    for i, s in enumerate([gs, d_s, m_s, v_s]):
        outs += ordered(s, r_a_in[i], r_a_out[i], r_kv[i], r_b_in[i], r_b_out[i])
    return (loss, grad_x[None], *outs)
```

```python
import math

import jax
import jax.numpy as jnp
from jax import lax
from jax.experimental import pallas as pl
from jax.experimental.pallas import tpu as pltpu

HEAD = 128
CHUNK = 64
N_META = 16
ROW_PAD = 128 - N_META
EPS = 1e-6
MASK_VALUE = -1e30
ADAM_LR = 0.001
ADAM_B1 = 0.9
ADAM_B2 = 0.999
ADAM_EPS = 1e-08
ADAM_WD = 0.01
ADAM_STEP = 10
N_DEV = 8
N_CHIP = 4
MIB = 1024 * 1024
AXES = ("x", "y", "c")
MESH = pl.DeviceIdType.MESH

F32 = jnp.float32
MXU = jnp.bfloat16
PAYLOAD = jnp.bfloat16
HI = lax.Precision.HIGHEST

NN = (((1,), (0,)), ((), ()))
NT = (((1,), (1,)), ((), ()))
TN = (((0,), (0,)), ((), ()))


def _dot(a, b, dims=NN):
    return lax.dot_general(a.astype(MXU), b.astype(MXU), dims, preferred_element_type=F32)


def _dot_exact(a, b):
    return lax.dot_general(a, b, NN, precision=HI, preferred_element_type=F32)


def _sigmoid(x):
    return 1.0 / (1.0 + jnp.exp(-x))


def _tile(dim, target, unit=128):
    best = None
    t = unit
    while t <= min(dim, target):
        if dim % t == 0:
            best = t
        t += unit
    return best if best is not None else dim


def _cparams(semantics, vmem_mib):
    return pltpu.CompilerParams(dimension_semantics=semantics, vmem_limit_bytes=vmem_mib * MIB)


def _mat_spec(arr, br, bc, rc_of_grid):
    if arr.ndim == 2:
        return pl.BlockSpec((br, bc), rc_of_grid)
    assert arr.shape[2] % bc == 0, (arr.shape, bc)
    per = arr.shape[2] // bc

    def idx(*g):
        r, c = rc_of_grid(*g)
        return (c // per, r, c % per)

    return pl.BlockSpec((None, br, bc), idx)


def _mat_shape(arr):
    return (arr.shape[0], arr.shape[1]) if arr.ndim == 2 else (arr.shape[1], arr.shape[0] * arr.shape[2])


def _matmul(a, b, dims, out_dtype, name, *, add=None, out_parts=1, tm=512, tn=512, tk=512, vmem_mib=48, sides=(),
            k_whole=False, rows=None, into=None):
    ar, ac = _mat_shape(a)
    br_, bc_ = _mat_shape(b)
    if dims == "nn":
        m, k, n = ar, ac, bc_
        assert br_ == k
    elif dims == "nt":
        m, k, n = ar, ac, br_
        assert bc_ == k
    else:
        m, k, n = ac, ar, bc_
        assert br_ == k
    m_unit, n_unit, k_unit = m, n, k
    if a.ndim == 3:
        if dims == "tn":
            m_unit = math.gcd(m_unit, a.shape[2])
        else:
            k_unit = math.gcd(k_unit, a.shape[2])
    if b.ndim == 3:
        if dims == "nt":
            k_unit = math.gcd(k_unit, b.shape[2])
        else:
            n_unit = math.gcd(n_unit, b.shape[2])
    if out_parts > 1:
        n_unit = math.gcd(n_unit, n // out_parts)
    tm = _tile(m_unit, tm, 128 if dims == "tn" else 64)
    tn, tk = _tile(n_unit, tn), _tile(k_unit, tk)
    whole_k = dims == "nt" and tk < k and k_whole
    if whole_k:
        k_chunk, tk = tk, k
    gm, gn, gk = m // tm, n // tn, k // tk
    assert gm * tm == m and gn * tn == n and gk * tk == k, (name, m, n, k, tm, tn, tk)

    def chunk_of(ref, arr, c):
        if arr.ndim == 2:
            return ref[:, c * k_chunk:(c + 1) * k_chunk]
        per = arr.shape[2] // k_chunk
        return ref[c // per, :, (c % per) * k_chunk:(c % per + 1) * k_chunk]

    def all_cols(arr, rows, row_of_grid):
        if arr.ndim == 2:
            return pl.BlockSpec((rows, arr.shape[1]), lambda i, j, kk: (row_of_grid(i, j), 0))
        return pl.BlockSpec((arr.shape[0], rows, arr.shape[2]), lambda i, j, kk: (0, row_of_grid(i, j), 0))

    if dims == "nn":
        a_spec = _mat_spec(a, tm, tk, lambda i, j, kk: (i, kk))
        b_spec = _mat_spec(b, tk, tn, lambda i, j, kk: (kk, j))
        dn = NN
    elif whole_k:
        a_spec = all_cols(a, tm, lambda i, j: i)
        b_spec = all_cols(b, tn, lambda i, j: j)
        dn = NT
    elif dims == "nt":
        a_spec = _mat_spec(a, tm, tk, lambda i, j, kk: (i, kk))
        b_spec = _mat_spec(b, tn, tk, lambda i, j, kk: (j, kk))
        dn = NT
    else:
        a_spec = _mat_spec(a, tk, tm, lambda i, j, kk: (kk, i))
        b_spec = _mat_spec(b, tk, tn, lambda i, j, kk: (kk, j))
        dn = TN

    if out_parts > 1:
        per = (n // out_parts) // tn
        out_shape = jax.ShapeDtypeStruct((out_parts, m, n // out_parts), out_dtype)
        o_spec = pl.BlockSpec((None, tm, tn), lambda i, j, kk: (j // per, i, j % per))
    else:
        out_shape = jax.ShapeDtypeStruct((m, n), out_dtype)
        o_spec = pl.BlockSpec((tm, tn), lambda i, j, kk: (i, j))

    in_specs = [a_spec, b_spec]
    args = [a, b]
    if add is not None:
        in_specs.append(pl.BlockSpec((tm, tn), lambda i, j, kk: (i, j)))
        args.append(add)
    if rows is not None:
        first_tile, gm = rows

        def shifted(spec):
            return pl.BlockSpec(spec.block_shape, lambda i, j, kk: spec.index_map(i + first_tile, j, kk))

        in_specs = [shifted(s) for s in in_specs]
        o_spec = shifted(o_spec)
    aliases = {}
    if into is not None:
        aliases[len(args)] = 0
        in_specs.append(_any_spec())
        args.append(into)

    def body(*refs):
        a_ref, b_ref, *others, o_ref, acc_ref = refs
        add_ref = others[0] if add is not None else None
        kk = pl.program_id(2)
        if whole_k:
            part = sum(lax.dot_general(chunk_of(a_ref, a, c).astype(MXU), chunk_of(b_ref, b, c).astype(MXU), dn,
                                       preferred_element_type=F32) for c in range(k // k_chunk))
        else:
            part = lax.dot_general(a_ref[...].astype(MXU), b_ref[...].astype(MXU), dn, preferred_element_type=F32)

        def finish(total):
            if add_ref is not None:
                total = total + add_ref[...]
            o_ref[...] = total.astype(o_ref.dtype)

        if gk == 1:
            finish(part)
        else:
            @pl.when(kk == 0)
            def _():
                acc_ref[...] = part

            @pl.when(jnp.logical_and(kk > 0, kk < gk - 1))
            def _():
                acc_ref[...] += part

            @pl.when(kk == gk - 1)
            def _():
                finish(acc_ref[...] + part)

    (out,), side_results = _pallas(
        body,
        name=name,
        args=args,
        grid=(gm, gn, gk),
        in_specs=in_specs,
        out_specs=[o_spec],
        out_shape=[out_shape],
        scratch_shapes=[pltpu.VMEM((tm, tn) if gk > 1 else (8, 128), F32)],
        semantics=("parallel", "parallel", "arbitrary"),
        vmem_mib=vmem_mib,
        sides=sides,
        aliases=aliases,
    )
    return (out, side_results) if sides else out


def _rms_fwd(h, gains, name, tm=384):
    lp, d = h.shape
    tm = _tile(lp, tm)
    n = len(gains)

    def body(*refs):
        h_ref = refs[0]
        g_refs = refs[1:1 + n]
        o_refs = refs[1 + n:]
        x = h_ref[...]
        y = x * lax.rsqrt(jnp.mean(x * x, axis=-1, keepdims=True) + EPS)
        for g_ref, o_ref in zip(g_refs, o_refs):
            o_ref[...] = (y * g_ref[...]).astype(o_ref.dtype)

    row = pl.BlockSpec((tm, d), lambda i: (i, 0))
    vec = pl.BlockSpec((1, d), lambda i: (0, 0))
    return pl.pallas_call(
        body,
        name=name,
        grid=(lp // tm,),
        in_specs=[row] + [vec] * n,
        out_specs=[row] * n,
        out_shape=[jax.ShapeDtypeStruct((lp, d), MXU)] * n,
        compiler_params=_cparams(("parallel",), 40),
    )(h, *gains)


def _rms_bwd(h, gains, dys, res, name, tm=384):
    lp, d = h.shape
    tm = _tile(lp, tm)
    n = len(gains)

    def body(*refs):
        h_ref, res_ref = refs[0], refs[1]
        g_refs = refs[2:2 + n]
        dy_refs = refs[2 + n:2 + 2 * n]
        dh_ref = refs[2 + 2 * n]
        dg_refs = refs[3 + 2 * n:]
        i = pl.program_id(0)
        x = h_ref[...]
        rstd = lax.rsqrt(jnp.mean(x * x, axis=-1, keepdims=True) + EPS)
        xhat = x * rstd
        dh = res_ref[...]
        for g_ref, dy_ref, dg_ref in zip(g_refs, dy_refs, dg_refs):
            dy = dy_ref[...]
            gdy = dy * g_ref[...]
            dh = dh + rstd * (gdy - xhat * jnp.mean(gdy * xhat, axis=-1, keepdims=True))
            part = jnp.sum(dy * xhat, axis=0, keepdims=True)

            @pl.when(i == 0)
            def _():
                dg_ref[...] = part

            @pl.when(i > 0)
            def _():
                dg_ref[...] += part

        dh_ref[...] = dh

    row = pl.BlockSpec((tm, d), lambda i: (i, 0))
    vec = pl.BlockSpec((1, d), lambda i: (0, 0))
    outs = pl.pallas_call(
        body,
        name=name,
        grid=(lp // tm,),
        in_specs=[row, row] + [vec] * n + [row] * n,
        out_specs=[row] + [vec] * n,
        out_shape=[jax.ShapeDtypeStruct((lp, d), F32)] + [jax.ShapeDtypeStruct((1, d), F32)] * n,
        compiler_params=_cparams(("arbitrary",), 56),
    )(h, res, *gains, *dys)
    return outs[0], list(outs[1:])


def _lb_fwd(gamma):
    def body(g_ref, lb_ref):
        g = g_ref[...]
        e = jnp.exp(g - jnp.max(g, axis=0, keepdims=True))
        lb_ref[...] = (e / jnp.sum(e, axis=0, keepdims=True))[0:1, :]

    return pl.pallas_call(body, name="lb_fwd", out_shape=jax.ShapeDtypeStruct((1, gamma.shape[1]), F32))(gamma)


def _lb_bwd(gamma, dlb):
    def body(g_ref, dlb_ref, dg_ref):
        g = g_ref[...]
        e = jnp.exp(g - jnp.max(g, axis=0, keepdims=True))
        s = e / jnp.sum(e, axis=0, keepdims=True)
        rows = lax.broadcasted_iota(jnp.int32, g.shape, 0)
        ds = jnp.where(rows == 0, dlb_ref[...], 0.0)
        dg_ref[...] = s * (ds - jnp.sum(s * ds, axis=0, keepdims=True))

    return pl.pallas_call(body, name="lb_bwd", out_shape=jax.ShapeDtypeStruct(gamma.shape, F32))(gamma, dlb)


def _tri(n, lower):
    r = lax.broadcasted_iota(jnp.int32, (n, n), 0)
    c = lax.broadcasted_iota(jnp.int32, (n, n), 1)
    return jnp.where((r >= c) if lower else (r <= c), 1.0, 0.0).astype(F32)


def _group(nc):
    return max(u for u in (1, 2, 3, 6, 11) if nc % u == 0)


def _running_sum(tri, x):
    hi = x.astype(MXU)
    rest = x - hi.astype(F32)
    mid = rest.astype(MXU)
    lo = (rest - mid.astype(F32)).astype(MXU)
    return _dot(tri, hi) + _dot(tri, mid) + _dot(tri, lo)


def _causal(n):
    r = lax.broadcasted_iota(jnp.int32, (n, n), 0)
    c = lax.broadcasted_iota(jnp.int32, (n, n), 1)
    return r >= c


def _chunk_gates(u_ref, lb, c):
    sl = pl.ds(pl.multiple_of(c * CHUNK, CHUNK), CHUNK)
    valid = (c * CHUNK + lax.broadcasted_iota(jnp.int32, (CHUNK, HEAD), 0)) >= ROW_PAD
    uq = u_ref[0, sl, :]
    uf = u_ref[1, sl, :]
    sq = _sigmoid(uq)
    sf = _sigmoid(uf)
    fg = lb + (1.0 - lb) * sf
    return dict(sl=sl, valid=valid, uq=uq, sq=sq, sf=sf, fg=fg, q=jnp.where(valid, uq * sq, 0.0),
                logf=jnp.where(valid, jnp.log(fg), 0.0), k=jnp.where(valid, 1.0 - fg, 0.0),
                v=jnp.where(valid, u_ref[2, sl, :], 0.0))


def _chunk_decays(x, b):
    b_last = b[CHUNK - 1:CHUNK, :]
    b_mid = b[CHUNK // 2 - 1:CHUNK // 2, :]
    e_qi = jnp.exp(b - b_mid)
    e_ki = jnp.exp(b_mid - b)
    e_kd = jnp.exp(b_last - b)
    e_qe = jnp.exp(b)
    q, k = x["q"], x["k"]
    return dict(x, e_qi=e_qi, e_ki=e_ki, e_kd=e_kd, e_qe=e_qe, qi=q * e_qi, ki=k * e_ki, kd=k * e_kd, qe=q * e_qe,
                decay=jnp.exp(b_last))


def _chunks(u_ref, lb, tri_lower, cs):
    gates = [_chunk_gates(u_ref, lb, c) for c in cs]
    sums = [_running_sum(tri_lower, x["logf"]) for x in gates]
    return [_chunk_decays(x, b) for x, b in zip(gates, sums)]


def _hgrn2_fwd(u4, lb, g_out, name="hgrn2_fwd", sides=()):
    _, lp, d = u4.shape
    nh, nc = d // HEAD, lp // CHUNK
    per = _group(nc)

    def body(u_ref, lb_ref, g_ref, o_ref, og_ref):
        lb_v = lb_ref[...]
        g = g_ref[...]

        tri_lower = _tri(CHUNK, True).astype(MXU)
        causal = _causal(CHUNK)

        def step(i, st):
            xs = _chunks(u_ref, lb_v, tri_lower, [i * per + u for u in range(per)])
            scores = [_dot(x["qi"], x["ki"], NT) for x in xs]
            updates = [_dot(x["v"], x["kd"], TN) for x in xs]
            states = []
            for x, upd in zip(xs, updates):
                states.append(st)
                st = x["decay"] * st + upd
            outs = [_dot(jnp.where(causal, a, 0.0), x["v"]) + _dot(x["qe"], s, NT)
                    for x, a, s in zip(xs, scores, states)]
            for x, o in zip(xs, outs):
                o_ref[x["sl"], :] = o
                on = o * lax.rsqrt(jnp.mean(o * o, axis=-1, keepdims=True) + EPS) * g
                z = u_ref[3, x["sl"], :]
                og_ref[x["sl"], :] = (on * (z * _sigmoid(z))).astype(og_ref.dtype)
            return st

        lax.fori_loop(0, nc // per, step, jnp.zeros((HEAD, HEAD), F32))

    slab = pl.BlockSpec((lp, HEAD), lambda h: (0, h))
    vec = pl.BlockSpec((1, HEAD), lambda h: (0, h))
    outs, side_results = _pallas(
        body,
        name=name,
        args=(u4, lb, g_out),
        grid=(nh,),
        in_specs=[pl.BlockSpec((4, lp, HEAD), lambda h: (0, 0, h)), vec, vec],
        out_specs=[slab, slab],
        out_shape=[jax.ShapeDtypeStruct((lp, d), F32), jax.ShapeDtypeStruct((lp, d), MXU)],
        semantics=("parallel",),
        vmem_mib=48,
        sides=sides,
    )
    return (*outs, side_results) if sides else tuple(outs)


def _hgrn2_bwd(u4, o, d_og, lb, g_out, name="hgrn2_bwd", sides=()):
    _, lp, d = u4.shape
    nh, nc = d // HEAD, lp // CHUNK
    per = _group(nc)

    def body(u_ref, o_ref, dog_ref, lb_ref, g_ref, du_ref, dlb_ref, dg_ref, st_ref, do_ref):
        lb_v = lb_ref[...]
        g = g_ref[...]

        tri_lower = _tri(CHUNK, True).astype(MXU)
        tri_upper = _tri(CHUNK, False).astype(MXU)
        causal = _causal(CHUNK)

        def fwd_step(i, carry):
            st, dg_acc = carry
            cs = [i * per + u for u in range(per)]
            xs = _chunks(u_ref, lb_v, tri_lower, cs)
            updates = [_dot(x["v"], x["kd"], TN) for x in xs]
            for c, x, upd in zip(cs, xs, updates):
                st_ref[c] = st
                st = x["decay"] * st + upd
            for x in xs:
                sl = x["sl"]
                ov = o_ref[sl, :]
                rstd = lax.rsqrt(jnp.mean(ov * ov, axis=-1, keepdims=True) + EPS)
                on = ov * rstd
                z = u_ref[3, sl, :]
                sz = _sigmoid(z)
                dog = dog_ref[sl, :]
                dy = dog * (z * sz)
                dz = dog * (on * g) * (sz * (1.0 + z * (1.0 - sz)))
                du_ref[3, sl, :] = dz.astype(du_ref.dtype)
                gdy = dy * g
                do = rstd * (gdy - on * jnp.mean(gdy * on, axis=-1, keepdims=True))
                do_ref[sl, :] = jnp.where(x["valid"], do, 0.0)
                dg_acc = dg_acc + jnp.sum(dy * on, axis=0, keepdims=True)
            return st, dg_acc

        _, dg_tot = lax.fori_loop(0, nc // per, fwd_step, (jnp.zeros((HEAD, HEAD), F32), jnp.zeros((1, HEAD), F32)))
        dg_ref[...] = dg_tot

        def bwd_step(i, carry):
            gt, dlb_acc = carry
            cs = [nc - 1 - (i * per + u) for u in range(per)]
            xs = _chunks(u_ref, lb_v, tri_lower, cs)
            dos = [do_ref[x["sl"], :] for x in xs]
            sts = [st_ref[c] for c in cs]
            scores = [jnp.where(causal, _dot(x["qi"], x["ki"], NT), 0.0) for x in xs]
            d_scores = [jnp.where(causal, _dot(do, x["v"], NT), 0.0) for x, do in zip(xs, dos)]
            d_qes = [_dot(do, st) for do, st in zip(dos, sts)]
            g_updates = [_dot(do, x["qe"], TN) for x, do in zip(xs, dos)]
            gts = []
            for x, upd in zip(xs, g_updates):
                gts.append(gt)
                gt = x["decay"] * gt + upd
            d_kds = [_dot(x["v"], g_) for x, g_ in zip(xs, gts)]
            dvs = [_dot(x["kd"], g_, NT) + _dot(a, do, TN) for x, g_, a, do in zip(xs, gts, scores, dos)]
            d_qis = [_dot(d_a, x["ki"]) for x, d_a in zip(xs, d_scores)]
            d_kis = [_dot(d_a, x["qi"], TN) for x, d_a in zip(xs, d_scores)]
            rows = lax.broadcasted_iota(jnp.int32, (CHUNK, HEAD), 0)
            dbs = []
            for x, g_, st, d_qi, d_ki, d_qe, d_kd in zip(xs, gts, sts, d_qis, d_kis, d_qes, d_kds):
                t_qi, t_ki, t_qe, t_kd = d_qi * x["qi"], d_ki * x["ki"], d_qe * x["qe"], d_kd * x["kd"]
                d_decay = jnp.sum(g_ * st, axis=0, keepdims=True)
                d_mid = jnp.sum(t_ki - t_qi, axis=0, keepdims=True)
                d_last = jnp.sum(t_kd, axis=0, keepdims=True) + d_decay * x["decay"]
                dbs.append(t_qi - t_ki + t_qe - t_kd + jnp.where(rows == CHUNK // 2 - 1, d_mid, 0.0)
                           + jnp.where(rows == CHUNK - 1, d_last, 0.0))
            dlogfs = [_running_sum(tri_upper, db) for db in dbs]
            for x, dlogf, dv, d_qi, d_ki, d_qe, d_kd in zip(xs, dlogfs, dvs, d_qis, d_kis, d_qes, d_kds):
                sl = x["sl"]
                dq = d_qi * x["e_qi"] + d_qe * x["e_qe"]
                dk = d_ki * x["e_ki"] + d_kd * x["e_kd"]
                valid, sq, sf, uq = x["valid"], x["sq"], x["sf"], x["uq"]
                dfg = jnp.where(valid, dlogf / x["fg"] - dk, 0.0)
                du_ref[0, sl, :] = jnp.where(valid, dq * (sq * (1.0 + uq * (1.0 - sq))), 0.0).astype(du_ref.dtype)
                du_ref[1, sl, :] = (dfg * (1.0 - lb_v) * (sf * (1.0 - sf))).astype(du_ref.dtype)
                du_ref[2, sl, :] = jnp.where(valid, dv, 0.0).astype(du_ref.dtype)
                dlb_acc = dlb_acc + jnp.sum(dfg * (1.0 - sf), axis=0, keepdims=True)
            return gt, dlb_acc

        _, dlb_tot = lax.fori_loop(0, nc // per, bwd_step, (jnp.zeros((HEAD, HEAD), F32), jnp.zeros((1, HEAD), F32)))
        dlb_ref[...] = dlb_tot

    slab = pl.BlockSpec((lp, HEAD), lambda h: (0, h))
    vec = pl.BlockSpec((1, HEAD), lambda h: (0, h))
    quad = pl.BlockSpec((4, lp, HEAD), lambda h: (0, 0, h))
    outs, side_results = _pallas(
        body,
        name=name,
        args=(u4, o, d_og, lb, g_out),
        grid=(nh,),
        in_specs=[quad, slab, slab, vec, vec],
        out_specs=[quad, vec, vec],
        out_shape=[jax.ShapeDtypeStruct((4, lp, d), MXU), jax.ShapeDtypeStruct((1, d), F32),
                   jax.ShapeDtypeStruct((1, d), F32)],
        scratch_shapes=[pltpu.VMEM((nc, HEAD, HEAD), F32), pltpu.VMEM((lp, HEAD), F32)],
        semantics=("parallel",),
        vmem_mib=58,
        sides=sides,
    )
    return (*outs, side_results) if sides else tuple(outs)


WIDE = 2 * HEAD
INV_SCALE = HEAD ** 0.5


def _split3(x):
    hi = x.astype(MXU).astype(F32)
    rest = x - hi
    mid = rest.astype(MXU).astype(F32)
    return hi, mid, (rest - mid).astype(MXU).astype(F32)


def _extra_cols(rows, first, second):
    lane = lax.broadcasted_iota(jnp.int32, (rows, HEAD), 1)
    out = jnp.where(lane < 6, 1.0, 0.0).astype(F32)
    for base, terms in ((0, first), (3, second)):
        if terms is not None:
            for j, term in enumerate(terms):
                out = jnp.where(lane == base + j, term, out)
    return out


def _head_col(a, h):
    lane = lax.broadcasted_iota(jnp.int32, a.shape, 1)
    return jnp.sum(jnp.where(lane == h, a, 0.0), axis=-1, keepdims=True)


def _attn_operands(ub, ukv, g_q, g_k, f_cum, name="attn_operands", tm=384):
    _, lp, d = ub.shape
    tm = _tile(lp, tm)
    nh = d // HEAD

    def body(q_ref, k_ref, v_ref, gq_ref, gk_ref, f_ref, qa_ref, ka_ref, va_ref):
        i = pl.program_id(0)
        f = f_ref[...]
        is_pad = (i * tm + lax.broadcasted_iota(jnp.int32, (tm, 1), 0)) < ROW_PAD
        ones_only = _extra_cols(tm, None, (0.0, 0.0, 0.0)).astype(MXU)
        for h in range(nh):
            hs = slice(h * HEAD, (h + 1) * HEAD)
            lo, hi = h * WIDE, h * WIDE + HEAD
            f_h = _head_col(f, h)
            for x_ref, g_ref, o_ref in ((q_ref, gq_ref, qa_ref), (k_ref, gk_ref, ka_ref)):
                x = x_ref[:, hs]
                y = x * lax.rsqrt(jnp.mean(x * x, axis=-1, keepdims=True) + EPS)
                o_ref[:, lo:hi] = (y * g_ref[:, hs]).astype(o_ref.dtype)
            qa_ref[:, hi:hi + HEAD] = _extra_cols(tm, _split3(f_h * INV_SCALE), None).astype(MXU)
            f_key = jnp.where(is_pad, -MASK_VALUE, f_h)
            ka_ref[:, hi:hi + HEAD] = _extra_cols(tm, None, _split3(-f_key * INV_SCALE)).astype(MXU)
            va_ref[:, lo:hi] = v_ref[:, hs].astype(MXU)
            va_ref[:, hi:hi + HEAD] = ones_only

    row = pl.BlockSpec((tm, d), lambda i: (i, 0))
    wide = pl.BlockSpec((tm, nh * WIDE), lambda i: (i, 0))
    vec = pl.BlockSpec((1, d), lambda i: (0, 0))
    return pl.pallas_call(
        body,
        name=name,
        grid=(lp // tm,),
        in_specs=[pl.BlockSpec((None, tm, d), lambda i: (0, i, 0)), pl.BlockSpec((None, tm, d), lambda i: (0, i, 0)),
                  pl.BlockSpec((None, tm, d), lambda i: (1, i, 0)), vec, vec, pl.BlockSpec((tm, 128), lambda i: (i, 0))],
        out_specs=[wide, wide, wide],
        out_shape=[jax.ShapeDtypeStruct((lp, nh * WIDE), MXU)] * 3,
        compiler_params=_cparams(("parallel",), 56),
    )(ub, ukv, ukv, g_q, g_k, f_cum)


def _head_rms_bwd_tile(x, g, dy):
    rstd = lax.rsqrt(jnp.mean(x * x, axis=-1, keepdims=True) + EPS)
    xhat = x * rstd
    gdy = dy * g
    dx = rstd * (gdy - xhat * jnp.mean(gdy * xhat, axis=-1, keepdims=True))
    return dx, jnp.sum(dy * xhat, axis=0, keepdims=True)


def _fgate_fwd(ufl, b_f):
    lp = ufl.shape[0]
    nb = lp // 128

    def body(u_ref, b_ref, f_ref):
        def step(i, carry):
            sl = pl.ds(pl.multiple_of(i * 128, 128), 128)
            valid = (i * 128 + lax.broadcasted_iota(jnp.int32, (128, 128), 0)) >= ROW_PAD
            x = u_ref[sl, :] + b_ref[...]
            logf = jnp.where(valid, jnp.minimum(x, 0.0) - jnp.log(1.0 + jnp.exp(-jnp.abs(x))), 0.0)
            f = _dot_exact(_tri(128, True), logf) + carry
            f_ref[sl, :] = f
            return f[127:128, :]

        lax.fori_loop(0, nb, step, jnp.zeros((1, 128), F32))

    return pl.pallas_call(body, name="fgate_fwd", out_shape=jax.ShapeDtypeStruct((lp, 128), F32))(ufl, b_f)


def _fgate_bwd(ufl, b_f, d_f):
    lp = ufl.shape[0]
    nb = lp // 128

    def body(u_ref, b_ref, df_ref, du_ref, db_ref):
        def step(j, carry):
            later, db_acc = carry
            i = nb - 1 - j
            sl = pl.ds(pl.multiple_of(i * 128, 128), 128)
            valid = (i * 128 + lax.broadcasted_iota(jnp.int32, (128, 128), 0)) >= ROW_PAD
            x = u_ref[sl, :] + b_ref[...]
            df = df_ref[sl, :]
            dlogf = _dot_exact(_tri(128, False), df) + later
            dx = jnp.where(valid, dlogf * _sigmoid(-x), 0.0)
            du_ref[sl, :] = dx.astype(du_ref.dtype)
            return later + jnp.sum(df, axis=0, keepdims=True), db_acc + jnp.sum(dx, axis=0, keepdims=True)

        _, db_tot = lax.fori_loop(0, nb, step, (jnp.zeros((1, 128), F32), jnp.zeros((1, 128), F32)))
        db_ref[...] = db_tot

    return pl.pallas_call(
        body, name="fgate_bwd",
        out_shape=[jax.ShapeDtypeStruct((lp, 128), MXU), jax.ShapeDtypeStruct((1, 128), F32)],
    )(ufl, b_f, d_f)


STRIP = 32
PAIR = 2


def _strip_causal(r, t):
    row = r + lax.broadcasted_iota(jnp.int32, (STRIP, t), 0)
    col = lax.broadcasted_iota(jnp.int32, (STRIP, t), 1)
    return col <= row


def _fox_fwd(qa, ka, va, ub, name="fox_fwd", t=384):
    lp = qa.shape[0]
    d = ub.shape[2]
    t = _tile(lp, t)
    nh, nq = d // HEAD, lp // t
    scale = HEAD ** -0.5

    assert nh % PAIR == 0
    heads = range(PAIR)

    def body(q_ref, k_ref, v_ref, z_ref, o_ref, olo_ref, og_ref, qb_ref,
             s_ref, p_ref, m_ref, a_ref, l_ref, acc_ref):
        qb = pl.program_id(1)
        m_ref[...] = jnp.full((PAIR, t, 1), MASK_VALUE, F32)
        l_ref[...] = jnp.zeros((PAIR, t, 128), F32)
        acc_ref[...] = jnp.zeros((PAIR, 2 * t, HEAD), F32)
        p_ref[1] = jnp.zeros((PAIR, 2 * t, t), MXU)
        a_ref[1] = jnp.ones((PAIR, t, 1), F32)

        def scores(kb, buf):
            ks = pl.ds(pl.multiple_of(kb * t, t), t)
            for j in heads:
                ws = slice(j * WIDE, (j + 1) * WIDE)
                s_ref[buf, j] = _dot(q_ref[:, ws], k_ref[ks, ws], NT)

        def weighted_sum(kb, buf):
            ks = pl.ds(pl.multiple_of(kb * t, t), t)
            for j in heads:
                pv = _dot(p_ref[buf, j], v_ref[ks, j * WIDE:j * WIDE + HEAD])
                alpha = a_ref[buf, j]
                acc_ref[j, 0:t, :] = alpha * acc_ref[j, 0:t, :] + pv[0:t]
                acc_ref[j, t:2 * t, :] = alpha * acc_ref[j, t:2 * t, :] + pv[t:2 * t]

        def softmax_update(buf, diagonal):
            for j in heads:
                for r in range(0, t, STRIP):
                    rs = slice(r, r + STRIP)
                    x = s_ref[buf, j, rs, :] * scale
                    if diagonal:
                        x = jnp.where(_strip_causal(r, t), x, MASK_VALUE)
                    m_old = m_ref[j, rs, :]
                    m_new = jnp.maximum(m_old, jnp.max(x, axis=-1, keepdims=True))
                    alpha = jnp.exp(m_old - m_new)
                    p = jnp.exp(x - m_new)
                    m_ref[j, rs, :] = m_new
                    a_ref[buf, j, rs, :] = alpha
                    l_ref[j, rs, :] = alpha * l_ref[j, rs, :] + sum(p[:, c:c + 128] for c in range(0, t, 128))
                    p_hi = p.astype(MXU)
                    p_ref[buf, j, rs, :] = p_hi
                    p_ref[buf, j, t + r:t + r + STRIP, :] = (p - p_hi.astype(F32)).astype(MXU)

        def off_diagonal(kb, cur):
            weighted_sum(jnp.maximum(kb - 1, 0), 1 - cur)
            scores(kb + 1, 1 - cur)
            softmax_update(cur, False)

        def diagonal(cur):
            weighted_sum(jnp.maximum(qb - 1, 0), 1 - cur)
            softmax_update(cur, True)
            weighted_sum(qb, cur)

        def two_blocks(i, carry):
            off_diagonal(2 * i, 0)
            off_diagonal(2 * i + 1, 1)
            return carry

        scores(0, 0)
        lax.fori_loop(0, qb // 2, two_blocks, 0)

        @pl.when(lax.rem(qb, 2) == 0)
        def _():
            diagonal(0)

        @pl.when(lax.rem(qb, 2) == 1)
        def _():
            off_diagonal(qb - 1, 0)
            diagonal(1)

        is_pad = (qb * t + lax.broadcasted_iota(jnp.int32, (t, 1), 0)) < ROW_PAD
        for j in heads:
            hs = slice(j * HEAD, (j + 1) * HEAD)
            l = jnp.sum(l_ref[j], axis=-1, keepdims=True)
            o = acc_ref[j, 0:t, :] / l
            o_ref[:, hs] = o
            olo_ref[:, hs] = acc_ref[j, t:2 * t, :] / l
            z = z_ref[:, hs]
            og_ref[:, hs] = (o * (z * _sigmoid(z))).astype(og_ref.dtype)
            extra = q_ref[:, j * WIDE + HEAD:(j + 1) * WIDE].astype(F32)
            f_scaled = extra[:, 0:1] + extra[:, 1:2] + extra[:, 2:3]
            log_term = jnp.where(is_pad, MASK_VALUE * INV_SCALE, f_scaled - (m_ref[j] + jnp.log(l)) * INV_SCALE)
            qb_ref[:, j * WIDE:j * WIDE + HEAD] = q_ref[:, j * WIDE:j * WIDE + HEAD]
            qb_ref[:, j * WIDE + HEAD:(j + 1) * WIDE] = _extra_cols(t, _split3(log_term), None).astype(qb_ref.dtype)

    scratch = [pltpu.VMEM((2, PAIR, t, t), F32), pltpu.VMEM((2, PAIR, 2 * t, t), MXU), pltpu.VMEM((PAIR, t, 1), F32),
               pltpu.VMEM((2, PAIR, t, 1), F32), pltpu.VMEM((PAIR, t, 128), F32), pltpu.VMEM((PAIR, 2 * t, HEAD), F32)]
    tile = pl.BlockSpec((t, PAIR * HEAD), lambda g, i: (i, g))
    wide_tile = pl.BlockSpec((t, PAIR * WIDE), lambda g, i: (i, g))
    wide_all = pl.BlockSpec((lp, PAIR * WIDE), lambda g, i: (0, g))
    return pl.pallas_call(
        body,
        name=name,
        grid=(nh // PAIR, nq),
        in_specs=[wide_tile, wide_all, wide_all, pl.BlockSpec((None, t, PAIR * HEAD), lambda g, i: (1, i, g))],
        out_specs=[tile, tile, tile, wide_tile],
        out_shape=[jax.ShapeDtypeStruct((lp, d), F32), jax.ShapeDtypeStruct((lp, d), F32),
                   jax.ShapeDtypeStruct((lp, d), MXU), jax.ShapeDtypeStruct((lp, nh * WIDE), MXU)],
        scratch_shapes=scratch,
        compiler_params=_cparams(("parallel", "arbitrary"), 48),
    )(qa, ka, va, ub)


def _fox_bwd(qb, ka, va, doa, d, name="fox_bwd", t=384, sides=()):
    lp = qb.shape[0]
    t = _tile(lp, t)
    nh, nk = d // HEAD, lp // t
    scale = HEAD ** -0.5

    assert nh % PAIR == 0
    heads = range(PAIR)

    def body(q_ref, do_ref, k_ref, v_ref, dq_ref, dk_ref, dv_ref, dfk_ref,
             s_ref, dp_ref, p_ref, ds_ref, col_ref):
        kb = pl.program_id(1)

        @pl.when(kb == 0)
        def _():
            dq_ref[...] = jnp.zeros_like(dq_ref)

        dk_ref[...] = jnp.zeros_like(dk_ref)
        dv_ref[...] = jnp.zeros_like(dv_ref)
        col_ref[...] = jnp.zeros_like(col_ref)

        def step(qb, diagonal):
            qs = pl.ds(pl.multiple_of(qb * t, t), t)
            for j in heads:
                ws = slice(j * WIDE, (j + 1) * WIDE)
                s_ref[j] = _dot(q_ref[qs, ws], k_ref[:, ws], NT)
                dp_ref[j] = _dot(do_ref[qs, ws], v_ref[:, ws], NT)
            for j in heads:
                for r in range(0, t, STRIP):
                    rs = slice(r, r + STRIP)
                    x = s_ref[j, rs, :] * scale
                    if diagonal:
                        x = jnp.where(_strip_causal(r, t), x, MASK_VALUE)
                    p = jnp.exp(x)
                    ds = p * dp_ref[j, rs, :]
                    p_ref[j, rs, :] = p.astype(MXU)
                    ds_ref[j, rs, :] = (ds * scale).astype(MXU)
                    col_ref[j] += ds
            for j in heads:
                hs = slice(j * HEAD, (j + 1) * HEAD)
                narrow = slice(j * WIDE, j * WIDE + HEAD)
                dsb = ds_ref[j]
                dv_ref[:, hs] += _dot(p_ref[j], do_ref[qs, narrow], TN)
                dq_ref[qs, hs] += _dot(dsb, k_ref[:, narrow])
                dk_ref[:, hs] += _dot(dsb, q_ref[qs, narrow], TN)

        def off_diagonal(qb, carry):
            step(qb, False)
            return carry

        step(kb, True)
        lax.fori_loop(kb + 1, nk, off_diagonal, 0)
        for j in heads:
            dfk_ref[j] = -jnp.sum(col_ref[j], axis=0, keepdims=True)

    scratch = [pltpu.VMEM((PAIR, t, t), F32), pltpu.VMEM((PAIR, t, t), F32), pltpu.VMEM((PAIR, t, t), MXU),
               pltpu.VMEM((PAIR, t, t), MXU), pltpu.VMEM((PAIR, STRIP, t), F32)]
    whole = pl.BlockSpec((lp, PAIR * HEAD), lambda g, j: (0, g))
    tile = pl.BlockSpec((t, PAIR * HEAD), lambda g, j: (j, g))
    wide_all = pl.BlockSpec((lp, PAIR * WIDE), lambda g, j: (0, g))
    wide_tile = pl.BlockSpec((t, PAIR * WIDE), lambda g, j: (j, g))
    outs, side_results = _pallas(
        body,
        name=name,
        args=(qb, doa, ka, va),
        grid=(nh // PAIR, nk),
        in_specs=[wide_all, wide_all, wide_tile, wide_tile],
        out_specs=[whole, tile, tile, pl.BlockSpec((PAIR, 1, t), lambda g, j: (g, 0, j))],
        out_shape=[jax.ShapeDtypeStruct((lp, d), F32)] * 3 + [jax.ShapeDtypeStruct((nh, 1, lp), F32)],
        scratch_shapes=scratch,
        semantics=("parallel", "arbitrary"),
        vmem_mib=48,
        sides=sides,
    )
    return (*outs, side_results) if sides else tuple(outs)


def _fox_do(d_og, o, o_lo, ub, name="fox_do", tm=384):
    lp, d = o.shape
    tm = _tile(lp, tm)
    nh = d // HEAD

    def body(dog_ref, o_ref, olo_ref, z_ref, doa_ref):
        for h in range(nh):
            hs = slice(h * HEAD, (h + 1) * HEAD)
            z = z_ref[:, hs]
            do = (dog_ref[:, hs] * (z * _sigmoid(z))).astype(doa_ref.dtype)
            delta = jnp.sum(do.astype(F32) * (o_ref[:, hs] + olo_ref[:, hs]), axis=-1, keepdims=True)
            doa_ref[:, h * WIDE:h * WIDE + HEAD] = do
            doa_ref[:, h * WIDE + HEAD:(h + 1) * WIDE] = _extra_cols(tm, _split3(-delta), (0.0, 0.0, 0.0)).astype(
                doa_ref.dtype)

    row = pl.BlockSpec((tm, d), lambda i: (i, 0))
    return pl.pallas_call(
        body,
        name=name,
        grid=(lp // tm,),
        in_specs=[row, row, row, pl.BlockSpec((None, tm, d), lambda i: (1, i, 0))],
        out_specs=pl.BlockSpec((tm, nh * WIDE), lambda i: (i, 0)),
        out_shape=jax.ShapeDtypeStruct((lp, nh * WIDE), MXU),
        compiler_params=_cparams(("parallel",), 56),
    )(d_og, o, o_lo, ub)


def _fox_q_bwd(ub, d_og, o, dqn, g_q, name="fox_q_bwd", tm=384):
    _, lp, d = ub.shape
    tm = _tile(lp, tm)
    nh = d // HEAD

    def body(ub_ref, dog_ref, o_ref, dqn_ref, g_ref, dub_ref, dg_ref):
        i = pl.program_id(0)
        for h in range(nh):
            hs = slice(h * HEAD, (h + 1) * HEAD)
            z = ub_ref[1, :, hs]
            sz = _sigmoid(z)
            dub_ref[1, :, hs] = (dog_ref[:, hs] * o_ref[:, hs] * (sz * (1.0 + z * (1.0 - sz)))).astype(dub_ref.dtype)
            dx, dg = _head_rms_bwd_tile(ub_ref[0, :, hs], g_ref[:, hs], dqn_ref[:, hs])
            dub_ref[0, :, hs] = dx.astype(dub_ref.dtype)

            @pl.when(i == 0)
            def _():
                dg_ref[:, hs] = dg

            @pl.when(i > 0)
            def _():
                dg_ref[:, hs] += dg

    row = pl.BlockSpec((tm, d), lambda i: (i, 0))
    pair = pl.BlockSpec((2, tm, d), lambda i: (0, i, 0))
    vec = pl.BlockSpec((1, d), lambda i: (0, 0))
    return pl.pallas_call(
        body,
        name=name,
        grid=(lp // tm,),
        in_specs=[pair, row, row, row, vec],
        out_specs=[pair, vec],
        out_shape=[jax.ShapeDtypeStruct((2, lp, d), MXU), jax.ShapeDtypeStruct((1, d), F32)],
        compiler_params=_cparams(("arbitrary",), 56),
    )(ub, d_og, o, dqn, g_q)


def _fox_k_bwd(ukv, dkn, dv, g_k, name="fox_k_bwd", tm=384):
    _, lp, d = ukv.shape
    tm = _tile(lp, tm)
    nh = d // HEAD

    def body(ukv_ref, dkn_ref, dv_ref, g_ref, du_ref, dg_ref):
        i = pl.program_id(0)
        du_ref[1] = dv_ref[...].astype(du_ref.dtype)
        for h in range(nh):
            hs = slice(h * HEAD, (h + 1) * HEAD)
            dx, dg = _head_rms_bwd_tile(ukv_ref[:, hs], g_ref[:, hs], dkn_ref[:, hs])
            du_ref[0, :, hs] = dx.astype(du_ref.dtype)

            @pl.when(i == 0)
            def _():
                dg_ref[:, hs] = dg

            @pl.when(i > 0)
            def _():
                dg_ref[:, hs] += dg

    row = pl.BlockSpec((tm, d), lambda i: (i, 0))
    vec = pl.BlockSpec((1, d), lambda i: (0, 0))
    return pl.pallas_call(
        body,
        name=name,
        grid=(lp // tm,),
        in_specs=[pl.BlockSpec((None, tm, d), lambda i: (0, i, 0)), row, row, vec],
        out_specs=[pl.BlockSpec((2, tm, d), lambda i: (0, i, 0)), vec],
        out_shape=[jax.ShapeDtypeStruct((2, lp, d), MXU), jax.ShapeDtypeStruct((1, d), F32)],
        compiler_params=_cparams(("arbitrary",), 48),
    )(ukv, dkn, dv, g_k)


def _loss(h, target, name="loss_head"):
    lp, d = h.shape
    nb = lp // 128

    def body(h_ref, t_ref, loss_ref, dh_ref, acc_ref):
        i = pl.program_id(0)

        @pl.when(i == 0)
        def _():
            acc_ref[...] = jnp.zeros_like(acc_ref)
            dh_ref[...] = jnp.zeros_like(dh_ref)

        @pl.when(i > 0)
        def _():
            err = h_ref[...] - t_ref[...]
            dh_ref[...] = err * (1.0 / d)
            acc_ref[...] += jnp.sum(jnp.sum(err * err, axis=-1, keepdims=True) * (1.0 / d), axis=0, keepdims=True)

        @pl.when(i == nb - 1)
        def _():
            loss_ref[...] = 0.5 * acc_ref[...]

    return pl.pallas_call(
        body,
        name=name,
        grid=(nb,),
        in_specs=[pl.BlockSpec((128, d), lambda i: (i, 0)),
                  pl.BlockSpec((128, d), lambda i: (jnp.maximum(i - 1, 0), 0))],
        out_specs=[pl.BlockSpec((1, 1), lambda i: (0, 0)), pl.BlockSpec((128, d), lambda i: (i, 0))],
        out_shape=[jax.ShapeDtypeStruct((1, 1), F32), jax.ShapeDtypeStruct((lp, d), F32)],
        scratch_shapes=[pltpu.VMEM((1, 1), F32)],
        compiler_params=_cparams(("arbitrary",), 32),
    )(h, target)


def _adam_math(w, g, m, v):
    m = ADAM_B1 * m + (1.0 - ADAM_B1) * g
    v = ADAM_B2 * v + (1.0 - ADAM_B2) * (g * g)
    m_hat = m / (1.0 - ADAM_B1 ** ADAM_STEP)
    v_hat = v / (1.0 - ADAM_B2 ** ADAM_STEP)
    delta = -ADAM_LR * (m_hat / (jnp.sqrt(v_hat) + ADAM_EPS) + ADAM_WD * w)
    return delta, m, v


def _adamw(parts, w, m, v, name, tm=512):
    n, r, c = parts.shape
    tm = tm if r % tm == 0 else r

    def body(p_ref, w_ref, m_ref, v_ref, g_ref, d_ref, nm_ref, nv_ref):
        g = p_ref[0].astype(F32)
        for j in range(1, n):
            g = g + p_ref[j].astype(F32)
        g_ref[...] = g
        d_ref[...], nm_ref[...], nv_ref[...] = _adam_math(w_ref[...], g, m_ref[...], v_ref[...])

    row = pl.BlockSpec((tm, c), lambda i: (i, 0))
    return pl.pallas_call(
        body,
        name=name,
        grid=(r // tm,),
        in_specs=[pl.BlockSpec((n, tm, c), lambda i: (0, i, 0)), row, row, row],
        out_specs=[row] * 4,
        out_shape=[jax.ShapeDtypeStruct((r, c), F32)] * 4,
        compiler_params=_cparams(("parallel",), 48),
    )(parts, w, m, v)


def _place():
    x, y, c = lax.axis_index("x"), lax.axis_index("y"), lax.axis_index("c")
    return x, y, c


def _other_chips(x, y):
    return [(1 - x, y), (x, 1 - y), (1 - x, 1 - y)]


def _any_spec():
    return pl.BlockSpec(memory_space=pl.ANY)


class _Side:
    def __init__(self, ins, outs, sems, start, finish, aliases=None):
        self.ins, self.outs, self.sems = list(ins), list(outs), sems
        self.start, self.finish = start, finish
        self.aliases = dict(aliases or {})


def _pallas(body, *, name, out_shape, args=(), grid=(), in_specs=(), out_specs=(), scratch_shapes=(),
            semantics=(), vmem_mib=None, sides=(), aliases=None):
    n_in, n_out, n_scr = len(args), len(out_shape), len(scratch_shapes)
    side_ins = [a for s in sides for a in s.ins]
    side_outs = [o for s in sides for o in s.outs]
    side_sems = [pltpu.SemaphoreType.DMA((max(k, 1),)) for s in sides for k in s.sems]
    aliases, in_at, out_at = dict(aliases or {}), n_in, n_out
    for s in sides:
        aliases.update({in_at + i: out_at + o for i, o in s.aliases.items()})
        in_at, out_at = in_at + len(s.ins), out_at + len(s.outs)

    def wrapped(*refs):
        at = [0]

        def take(k):
            got = refs[at[0]:at[0] + k]
            at[0] += k
            return got

        main_in = take(n_in)
        s_in = [take(len(s.ins)) for s in sides]
        main_out = take(n_out)
        s_out = [take(len(s.outs)) for s in sides]
        main_scr = take(n_scr)
        s_sem = [take(3) for s in sides]

        def run(stage):
            for s, i_, o_, m_ in zip(sides, s_in, s_out, s_sem):
                getattr(s, stage)(i_, o_, *m_)

        first = last = None
        for k, g in enumerate(grid):
            i = pl.program_id(k)
            first = (i == 0) if first is None else jnp.logical_and(first, i == 0)
            last = (i == g - 1) if last is None else jnp.logical_and(last, i == g - 1)
        if sides:
            run("start") if first is None else pl.when(first)(lambda: run("start"))
        body(*main_in, *main_out, *main_scr)
        if sides:
            run("finish") if last is None else pl.when(last)(lambda: run("finish"))

    kw = {}
    if grid:
        kw["grid"] = grid
    if semantics or vmem_mib:
        sem = tuple("arbitrary" for _ in grid) if sides else tuple(semantics)
        kw["compiler_params"] = pltpu.CompilerParams(
            dimension_semantics=sem or None, vmem_limit_bytes=vmem_mib * MIB if vmem_mib else None)
    res = pl.pallas_call(
        wrapped,
        name=name,
        in_specs=list(in_specs) + [_any_spec()] * len(side_ins),
        out_specs=list(out_specs) + [_any_spec()] * len(side_outs),
        out_shape=list(out_shape) + side_outs,
        scratch_shapes=list(scratch_shapes) + side_sems,
        input_output_aliases=aliases,
        **kw,
    )(*args, *side_ins)
    main, rest, per_side = list(res[:n_out]), list(res[n_out:]), []
    for s in sides:
        per_side.append(rest[:len(s.outs)])
        rest = rest[len(s.outs):]
    return main, per_side


def _slot(p):
    return 4 * p[0] + 2 * p[1] + p[2]


def _sibling_side(grads):
    n = len(grads)

    def copies(ins, outs, send_sems, recv_sems):
        x, y, c = _place()
        return [pltpu.make_async_remote_copy(
            src_ref=ins[t].at[2 * chip + (1 - c)], dst_ref=outs[t].at[chip],
            send_sem=send_sems.at[4 * t + chip], recv_sem=recv_sems.at[4 * t + chip],
            device_id=(x, y, 1 - c), device_id_type=MESH) for t in range(n) for chip in range(N_CHIP)]

    def start(ins, outs, send_sems, recv_sems, local_sems):
        for cp in copies(ins, outs, send_sems, recv_sems):
            cp.start()

    def finish(ins, outs, send_sems, recv_sems, local_sems):
        for cp in copies(ins, outs, send_sems, recv_sems):
            cp.wait()

    outs = [jax.ShapeDtypeStruct((N_CHIP,) + g.shape[1:], g.dtype) for g in grads]
    return _Side(grads, outs, (4 * n, 4 * n, 0), start, finish)


def _chips_side(partials):
    n = len(partials)

    def copies(ins, outs, send_sems, recv_sems, local_sems):
        x, y, c = _place()
        my_chip = 2 * x + y
        local = [pltpu.make_async_copy(ins[t].at[my_chip], outs[t].at[my_chip], local_sems.at[t]) for t in range(n)]
        sends, recvs = [], []
        for t in range(n):
            for j, chip in enumerate(_other_chips(x, y)):
                their = 2 * chip[0] + chip[1]
                sems = dict(send_sem=send_sems.at[3 * t + j], recv_sem=recv_sems.at[3 * t + j],
                            device_id=(*chip, c), device_id_type=MESH)
                sends.append(pltpu.make_async_remote_copy(src_ref=ins[t].at[their], dst_ref=outs[t].at[my_chip], **sems))
                recvs.append(pltpu.make_async_remote_copy(src_ref=ins[t].at[my_chip], dst_ref=outs[t].at[their], **sems))
        return local, sends, recvs

    def start(ins, outs, send_sems, recv_sems, local_sems):
        local, sends, _ = copies(ins, outs, send_sems, recv_sems, local_sems)
        for cp in local + sends:
            cp.start()

    def finish(ins, outs, send_sems, recv_sems, local_sems):
        local, sends, recvs = copies(ins, outs, send_sems, recv_sems, local_sems)
        for cp in sends:
            cp.wait_send()
        for cp in recvs:
            cp.wait_recv()
        for cp in local:
            cp.wait()

    outs = [jax.ShapeDtypeStruct(p.shape, p.dtype) for p in partials]
    return _Side(partials, outs, (3 * n, 3 * n, n), start, finish)


def _gather_own_side(blocks):
    n = len(blocks)

    def copies(ins, outs, send_sems, recv_sems, local_sems):
        x, y, c = _place()
        me = (x, y, c)
        peers = [(x, y, 1 - c)] + [(*chip, c) for chip in _other_chips(x, y)]
        local = [pltpu.make_async_copy(ins[t], outs[t].at[_slot(me)], local_sems.at[t]) for t in range(n)]
        sends, recvs = [], []
        for t in range(n):
            for k, peer in enumerate(peers):
                sems = dict(send_sem=send_sems.at[4 * t + k], recv_sem=recv_sems.at[4 * t + k],
                            device_id=peer, device_id_type=MESH)
                sends.append(pltpu.make_async_remote_copy(src_ref=ins[t], dst_ref=outs[t].at[_slot(me)], **sems))
                recvs.append(pltpu.make_async_remote_copy(src_ref=ins[t], dst_ref=outs[t].at[_slot(peer)], **sems))
        return local, sends, recvs

    def start(ins, outs, send_sems, recv_sems, local_sems):
        local, sends, _ = copies(ins, outs, send_sems, recv_sems, local_sems)
        for cp in local + sends:
            cp.start()

    def finish(ins, outs, send_sems, recv_sems, local_sems):
        local, sends, recvs = copies(ins, outs, send_sems, recv_sems, local_sems)
        for cp in sends:
            cp.wait_send()
        for cp in recvs:
            cp.wait_recv()
        for cp in local:
            cp.wait()

    outs = [jax.ShapeDtypeStruct((N_DEV,) + b.shape, b.dtype) for b in blocks]
    return _Side(blocks, outs, (4 * n, 4 * n, n), start, finish)


def _gather_pass_side(gathered):
    n = len(gathered)

    def copies(outs, send_sems, recv_sems):
        x, y, c = _place()
        sends, recvs = [], []
        for t in range(n):
            for j, chip in enumerate(_other_chips(x, y)):
                sems = dict(send_sem=send_sems.at[3 * t + j], recv_sem=recv_sems.at[3 * t + j],
                            device_id=(x, y, 1 - c), device_id_type=MESH)
                mine, theirs = outs[t].at[_slot((*chip, c))], outs[t].at[_slot((*chip, 1 - c))]
                sends.append(pltpu.make_async_remote_copy(src_ref=mine, dst_ref=mine, **sems))
                recvs.append(pltpu.make_async_remote_copy(src_ref=mine, dst_ref=theirs, **sems))
        return sends, recvs

    def start(ins, outs, send_sems, recv_sems, local_sems):
        for cp in copies(outs, send_sems, recv_sems)[0]:
            cp.start()

    def finish(ins, outs, send_sems, recv_sems, local_sems):
        sends, recvs = copies(outs, send_sems, recv_sems)
        for cp in sends:
            cp.wait_send()
        for cp in recvs:
            cp.wait_recv()

    outs = [jax.ShapeDtypeStruct(g.shape, g.dtype) for g in gathered]
    return _Side(gathered, outs, (3 * n, 3 * n, 0), start, finish, aliases={t: t for t in range(n)})


def _alone(side, name):
    return _pallas(lambda: None, name=name, out_shape=[], sides=[side])[1][0]


def _all_gather(blocks, name):
    n = len(blocks)

    def body(*refs):
        in_refs, out_refs = refs[:n], refs[n:2 * n]
        send_sems, recv_sems, local_sems = refs[2 * n:]
        x, y, c = _place()
        me, sibling = (x, y, c), (x, y, 1 - c)
        chips = _other_chips(x, y)

        def slot(p):
            return 4 * p[0] + 2 * p[1] + p[2]

        def copy(t, k, block, to, src=None):
            dst = out_refs[t].at[slot(block)]
            return pltpu.make_async_remote_copy(
                src_ref=dst if src is None else src, dst_ref=dst,
                send_sem=send_sems.at[7 * t + k], recv_sem=recv_sems.at[7 * t + k],
                device_id=to, device_id_type=MESH)

        started = []
        for t in range(n):
            mine = pltpu.make_async_copy(in_refs[t], out_refs[t].at[slot(me)], local_sems.at[t])
            mine.start()
            started.append(mine)
        sends = []
        for t in range(n):
            first = [copy(t, 0, me, sibling, src=in_refs[t])]
            first += [copy(t, 1 + j, me, (*chip, c), src=in_refs[t]) for j, chip in enumerate(chips)]
            for cp in first:
                cp.start()
            sends += first
        for t in range(n):
            for j, chip in enumerate(chips):
                copy(t, 1 + j, (*chip, c), me).wait_recv()
                passed = copy(t, 4 + j, (*chip, c), sibling)
                passed.start()
                sends.append(passed)
        for t in range(n):
            copy(t, 0, sibling, me).wait_recv()
            for j, chip in enumerate(chips):
                copy(t, 4 + j, (*chip, 1 - c), me).wait_recv()
        for cp in sends:
            cp.wait_send()
        for mine in started:
            mine.wait()

    return pl.pallas_call(
        body,
        name=name,
        in_specs=[_any_spec()] * n,
        out_specs=[_any_spec()] * n,
        out_shape=[jax.ShapeDtypeStruct((N_DEV,) + b.shape, b.dtype) for b in blocks],
        scratch_shapes=[pltpu.SemaphoreType.DMA((7 * n,)), pltpu.SemaphoreType.DMA((7 * n,)),
                        pltpu.SemaphoreType.DMA((n,))],
    )(*blocks)


def _pair_sum(g8, got, name, tm=1024):
    _, r, c = g8.shape
    tm = tm if r % tm == 0 else r
    core = lax.axis_index("c")

    def body(core_ref, mine_ref, got_ref, o_ref):
        south_first = core_ref[0] == 0
        a, b = mine_ref[...], got_ref[...]
        o_ref[...] = (jnp.where(south_first, a, b) + jnp.where(south_first, b, a)).astype(o_ref.dtype)

    return pl.pallas_call(
        body,
        name=name,
        grid_spec=pltpu.PrefetchScalarGridSpec(
            num_scalar_prefetch=1,
            grid=(N_CHIP, r // tm),
            in_specs=[pl.BlockSpec((None, tm, c), lambda j, i, core_ref: (2 * j + core_ref[0], i, 0)),
                      pl.BlockSpec((None, tm, c), lambda j, i, core_ref: (j, i, 0))],
            out_specs=pl.BlockSpec((None, tm, c), lambda j, i, core_ref: (j, i, 0)),
        ),
        out_shape=jax.ShapeDtypeStruct((N_CHIP, r, c), PAYLOAD),
        compiler_params=_cparams(("parallel", "parallel"), 32),
    )(jnp.reshape(core, (1,)).astype(jnp.int32), g8, got)


def _all_reduce_small(s, name="small_all_reduce"):
    r, c = s.shape

    def body(s_ref, o_ref, buf_ref, send_sems, recv_sems):
        x, y, c_ = _place()
        me = 4 * x + 2 * y + c_
        buf_ref[me] = s_ref[...]

        def copy(k, slot, peer):
            return pltpu.make_async_remote_copy(
                src_ref=s_ref, dst_ref=buf_ref.at[slot],
                send_sem=send_sems.at[k - 1], recv_sem=recv_sems.at[k - 1],
                device_id=peer, device_id_type=MESH)

        peers = []
        for k in range(1, N_DEV):
            peer = (x ^ ((k >> 2) & 1), y ^ ((k >> 1) & 1), c_ ^ (k & 1))
            peers.append(peer)
            copy(k, me, peer).start()
        for k, peer in zip(range(1, N_DEV), peers):
            cp = copy(k, 4 * peer[0] + 2 * peer[1] + peer[2], peer)
            cp.wait_send()
            cp.wait_recv()
        total = buf_ref[0]
        for j in range(1, N_DEV):
            total = total + buf_ref[j]
        o_ref[...] = total

    return pl.pallas_call(
        body,
        name=name,
        out_shape=jax.ShapeDtypeStruct((r, c), F32),
        in_specs=[pl.BlockSpec(memory_space=pltpu.VMEM)],
        out_specs=pl.BlockSpec(memory_space=pltpu.VMEM),
        scratch_shapes=[pltpu.VMEM((N_DEV, r, c), F32), pltpu.SemaphoreType.DMA((N_DEV - 1,)),
                        pltpu.SemaphoreType.DMA((N_DEV - 1,))],
    )(s)


def _late_weights(g_kv, g_b_in, g_b_out):
    d = g_kv.shape[1]
    nh = d // HEAD
    wkv = jnp.reshape(jnp.transpose(g_kv, (1, 0, 2)), (d, -1))
    return wkv[:, :2 * d], jnp.pad(wkv[:, 2 * d:], ((0, 0), (0, 128 - nh))), g_b_in, jnp.reshape(g_b_out, (d, d))


def _kv_grad_blocks(dwkv2, dwfl):
    d = dwkv2.shape[0]
    dwkv = jnp.concatenate([dwkv2, dwfl[:, :d // HEAD]], axis=1)
    return jnp.transpose(jnp.reshape(dwkv, (d, N_DEV, -1)), (1, 0, 2))


def _local_step(x, target, meta, gamma, a_norm, wa_in, a_out_norm, wa_out, kv_norm, late, b_f, g_k, b_norm, g_q,
                dist):
    d = x.shape[1]
    nh = d // HEAD
    cols = d // N_DEV
    h0 = jnp.concatenate([jnp.zeros((ROW_PAD, d), F32), meta, x], axis=0)
    lp = h0.shape[0]
    big = dict(tm=1408, tn=512, tk=2048)

    lb = _lb_fwd(gamma)
    (hn_a,) = _rms_fwd(h0, [a_norm], "rms_a")
    if dist:
        a_out_blk, kv_blk, b_in_blk, b_out_blk = late
        u4, (first,) = _matmul(hn_a, wa_in, "nn", F32, "a_in", out_parts=4,
                               sides=[_gather_own_side([a_out_blk, b_in_blk, b_out_blk])], **big)
        o_a, og_a, ((g_a_out, g_b_in, g_b_out), (g_kv,)) = _hgrn2_fwd(
            u4, lb, a_out_norm, sides=[_gather_pass_side(first), _gather_own_side([kv_blk])])
        wa_out = jnp.reshape(g_a_out, (d, d))
        h1, ((g_kv,),) = _matmul(og_a, wa_out, "nn", F32, "a_out", add=h0, sides=[_gather_pass_side([g_kv])], **big)
        wkv2, wfl, wb_in, wb_out = _late_weights(g_kv, g_b_in, g_b_out)
    else:
        u4 = _matmul(hn_a, wa_in, "nn", F32, "a_in", out_parts=4, **big)
        o_a, og_a = _hgrn2_fwd(u4, lb, a_out_norm)
        h1 = _matmul(og_a, wa_out, "nn", F32, "a_out", add=h0, **big)
        wkv2, wfl, wb_in, wb_out = late
    hk, hb = _rms_fwd(h1, [kv_norm, b_norm], "rms_kv_b")
    ukv = _matmul(hk, wkv2, "nn", F32, "kv_in", out_parts=2, **big)
    ufl = _matmul(hk, wfl, "nn", F32, "kv_f", **big)
    ub = _matmul(hb, wb_in, "nn", F32, "b_in", out_parts=2, **big)
    f_cum = _fgate_fwd(ufl, b_f)
    qa, ka, va = _attn_operands(ub, ukv, g_q, g_k, f_cum)
    o_b, o_lo, og_b, qb = _fox_fwd(qa, ka, va, ub)
    h2 = _matmul(og_b, wb_out, "nn", F32, "b_out", add=h1, **big)
    loss, dh2 = _loss(h2, target)

    dx_t = dict(tm=1408, tn=512, tk=2048, vmem_mib=56)
    dx_parts_t = dict(tm=704, tn=512, tk=2048, vmem_mib=58, k_whole=True)
    dw_t = dict(tm=512, tn=1024, tk=lp, vmem_mib=58)
    dw_f32_t = dict(tm=1024, tn=512, tk=lp, vmem_mib=58)
    def to_sibling(g8):
        return [_sibling_side([g8])] if dist else []

    def to_chips(partial):
        return [_chips_side([partial])] if dist else []

    def unpack(res, n_sides):
        if not dist:
            return res, [None] * n_sides
        *main, side_results = res
        return (main[0] if len(main) == 1 else tuple(main)), [r[0] for r in side_results]

    dwb_out = _matmul(og_b, dh2, "tn", F32, "b_out_dw", **dw_f32_t)
    g8_b_out = jnp.reshape(dwb_out, (N_DEV, cols, d))
    d_ogb, (got,) = unpack(_matmul(dh2, wb_out, "nt", F32, "b_out_dx", sides=to_sibling(g8_b_out), **dx_t), 1)
    p_b_out = _pair_sum(g8_b_out, got, "pair_sum_b_w_out") if dist else None
    doa = _fox_do(d_ogb, o_b, o_lo, ub)
    (dqn, dkn, dv, dfk), (r_b_out,) = unpack(_fox_bwd(qb, ka, va, doa, d, sides=to_chips(p_b_out)), 1)
    dub, dg_q = _fox_q_bwd(ub, d_ogb, o_b, dqn, g_q)
    dukv, dg_k = _fox_k_bwd(ukv, dkn, dv, g_k)
    d_f = jnp.pad(jnp.transpose(dfk[:, 0, :]), ((0, 0), (0, 128 - nh)))
    dufl, db_f = _fgate_bwd(ufl, b_f, d_f)
    dwb_in = _matmul(hb, dub, "tn", F32, "b_in_dw", out_parts=N_DEV, **dw_t)
    d_hb, (got,) = unpack(_matmul(dub, wb_in, "nt", F32, "b_in_dx", sides=to_sibling(dwb_in), **dx_parts_t), 1)
    p_b_in = _pair_sum(dwb_in, got, "pair_sum_b_w_in") if dist else None
    d_hk, (r_b_in,) = unpack(_matmul(dukv, wkv2, "nt", F32, "kv_dx", sides=to_chips(p_b_in), **dx_parts_t), 1)
    d_hk = _matmul(dufl, wfl, "nt", F32, "kv_f_dx", add=d_hk, **dx_t)
    dwkv2 = _matmul(hk, dukv, "tn", F32, "kv_dw", **dw_t)
    dwfl = _matmul(hk, dufl, "tn", F32, "kv_f_dw", **dw_t)
    g8_kv = _kv_grad_blocks(dwkv2, dwfl) if dist else None
    dh1, (dg_kv, dg_b) = _rms_bwd(h1, [kv_norm, b_norm], [d_hk, d_hb], dh2, "rms_kv_b_bwd")
    d_oga, (got,) = unpack(_matmul(dh1, wa_out, "nt", F32, "a_out_dx", sides=to_sibling(g8_kv), **dx_t), 1)
    p_kv = _pair_sum(g8_kv, got, "pair_sum_kv_w") if dist else None
    dwa_out = _matmul(og_a, dh1, "tn", F32, "a_out_dw", **dw_f32_t)
    g8_a_out = jnp.reshape(dwa_out, (N_DEV, cols, d))
    (du4, dlb, dg_aout), (r_kv, got) = unpack(
        _hgrn2_bwd(u4, o_a, d_oga, lb, a_out_norm, sides=to_chips(p_kv) + to_sibling(g8_a_out)), 2)
    p_a_out = _pair_sum(g8_a_out, got, "pair_sum_a_w_out") if dist else None
    dwa_in, (r_a_out,) = unpack(
        _matmul(hn_a, du4, "tn", F32, "a_in_dw", out_parts=N_DEV, sides=to_chips(p_a_out), **dw_t), 1)
    row_tiles = lp // _tile(lp, dx_parts_t["tm"], 64)
    if dist and row_tiles >= 2:
        top = max(1, row_tiles // 3)
        d_hna, ((got,),) = _matmul(du4, wa_in, "nt", F32, "a_in_dx_top", rows=(0, top),
                                   sides=to_sibling(dwa_in), **dx_parts_t)
        p_a_in = _pair_sum(dwa_in, got, "pair_sum_a_w_in")
        d_hna, ((r_a_in,),) = _matmul(du4, wa_in, "nt", F32, "a_in_dx_bottom", rows=(top, row_tiles - top), into=d_hna,
                                      sides=to_chips(p_a_in), **dx_parts_t)
    else:
        d_hna, (got,) = unpack(_matmul(du4, wa_in, "nt", F32, "a_in_dx", sides=to_sibling(dwa_in), **dx_parts_t), 1)
        if dist:
            (r_a_in,) = _alone(_chips_side([_pair_sum(dwa_in, got, "pair_sum_a_w_in")]), "grads_to_chips_a_w_in")
    dh0, (dg_a,) = _rms_bwd(h0, [a_norm], [d_hna], dh1, "rms_a_bwd")
    dgamma = _lb_bwd(gamma, dlb)

    grads = dict(meta=dh0[ROW_PAD:ROW_PAD + N_META], gamma=dgamma, a_norm=dg_a, a_out_norm=dg_aout, kv_norm=dg_kv,
                 b_f=db_f, g_k=dg_k, b_norm=dg_b, g_q=dg_q)
    if dist:
        grads.update(wa_in=r_a_in, wa_out=r_a_out, wkv=r_kv, wb_in=r_b_in, wb_out=r_b_out)
    else:
        grads.update(wa_in=dwa_in, wa_out=dwa_out, wkv2=dwkv2, wfl=dwfl, wb_in=dwb_in, wb_out=dwb_out)
    return loss, dh0[ROW_PAD + N_META:], grads


def kernel(x, meta, gamma_lb, a_norm, a_w_in, a_out_norm, a_w_out, kv_norm, kv_w, fox_b_f, fox_k_norm, b_norm, b_w_in, b_q_norm, b_w_out, loss_target, m_meta, m_gamma_lb, m_a_norm, m_a_w_in, m_a_out_norm, m_a_w_out, m_kv_norm, m_kv_w, m_fox_b_f, m_fox_k_norm, m_b_norm, m_b_w_in, m_b_q_norm, m_b_w_out, v_meta, v_gamma_lb, v_a_norm, v_a_w_in, v_a_out_norm, v_a_w_out, v_kv_norm, v_kv_w, v_fox_b_f, v_fox_k_norm, v_b_norm, v_b_w_in, v_b_q_norm, v_b_w_out):
    d = x.shape[-1]
    nh = d // HEAD
    cols = d // N_DEV
    me = 4 * lax.axis_index("x") + 2 * lax.axis_index("y") + lax.axis_index("c")

    sharded_small = jnp.concatenate([meta, gamma_lb, a_norm, a_out_norm, jnp.zeros((4, cols), F32)], axis=0)
    g_a_in, g_small = _all_gather([a_w_in[0].astype(MXU), sharded_small], "gather_weights")
    wa_out = None
    late = (a_w_out[0].astype(MXU), kv_w.astype(MXU), b_w_in[0].astype(MXU), b_w_out[0].astype(MXU))
    small = jnp.reshape(jnp.transpose(g_small, (1, 0, 2)), (-1, d))
    meta_f, gamma_f, a_norm_f, a_out_norm_f = small[:16], small[16:18], small[18:19], small[19:20]
    b_f = jnp.pad(jnp.reshape(fox_b_f, (1, nh)), ((0, 0), (0, 128 - nh)))
    g_k = jnp.reshape(fox_k_norm, (1, d))
    g_q = jnp.reshape(b_q_norm, (1, d))
    kv_norm_r = jnp.reshape(kv_norm, (1, d))

    loss, grad_x, g = _local_step(x[0], loss_target[0], meta_f, gamma_f, a_norm_f, g_a_in, a_out_norm_f, wa_out,
                                  kv_norm_r, late, b_f, g_k, b_norm, g_q, dist=True)
    loss = lax.psum(loss[0, 0], AXES)

    r_a_in = _adamw(g["wa_in"], a_w_in[0], m_a_w_in[0], v_a_w_in[0], "adamw_a_w_in")
    r_a_out = _adamw(g["wa_out"], a_w_out[0], m_a_w_out[0], v_a_w_out[0], "adamw_a_w_out")
    r_kv = _adamw(g["wkv"], kv_w, m_kv_w, v_kv_w, "adamw_kv_w")
    r_b_in = _adamw(g["wb_in"], b_w_in[0], m_b_w_in[0], v_b_w_in[0], "adamw_b_w_in")
    r_b_out = _adamw(g["wb_out"], b_w_out[0], m_b_w_out[0], v_b_w_out[0], "adamw_b_w_out")

    packed = jnp.concatenate(
        [g["meta"], g["gamma"], g["a_norm"], g["a_out_norm"], g["kv_norm"], g["b_norm"], g["g_k"], g["g_q"],
         jnp.pad(g["b_f"], ((0, 0), (0, d - 128))), jnp.zeros((7, d), F32)], axis=0)
    tot = _all_reduce_small(packed)
    mine = lax.dynamic_slice_in_dim(tot[:20], me * cols, cols, axis=1)
    gs = [mine[:16], mine[16:18], mine[18:19], mine[19:20], tot[20], tot[21:22], jnp.reshape(tot[22], (nh, HEAD)),
          jnp.reshape(tot[23], (1, nh, HEAD)), tot[24, :nh]]
    small_w = [meta, gamma_lb, a_norm, a_out_norm, kv_norm, b_norm, fox_k_norm, b_q_norm, fox_b_f]
    small_m = [m_meta, m_gamma_lb, m_a_norm, m_a_out_norm, m_kv_norm, m_b_norm, m_fox_k_norm, m_b_q_norm, m_fox_b_f]
    small_v = [v_meta, v_gamma_lb, v_a_norm, v_a_out_norm, v_kv_norm, v_b_norm, v_fox_k_norm, v_b_q_norm, v_fox_b_f]

    def pack(arrs):
        rows = [jnp.reshape(a, (-1, cols)) for a in arrs[:8]]
        rows.append(jnp.pad(jnp.reshape(arrs[8], (1, nh)), ((0, 0), (0, cols - nh))))
        n_rows = sum(r.shape[0] for r in rows)
        return jnp.concatenate(rows + [jnp.zeros((-n_rows % 8, cols), F32)], axis=0)

    _, sd, sm, sv = _adamw(pack(gs)[None], pack(small_w), pack(small_m), pack(small_v), "adamw_small")

    def unpack(p):
        out, at = [], 0
        for a in small_w[:8]:
            n = a.size // cols
            out.append(jnp.reshape(p[at:at + n], a.shape))
            at += n
        out.append(p[at, :nh])
        return out

    d_s, m_s, v_s = unpack(sd), unpack(sm), unpack(sv)

    def ordered(s, a_in, a_out, kv, b_in, b_out):
        return [s[0], s[1], s[2], a_in[None], s[3], a_out[None], s[4], kv, s[8], s[6], s[5], b_in[None], s[7], b_out[None]]

    outs = []
    for i, s in enumerate([gs, d_s, m_s, v_s]):
        outs += ordered(s, r_a_in[i], r_a_out[i], r_kv[i], r_b_in[i], r_b_out[i])
    return (loss, grad_x[None], *outs)
```

```python
import math

import jax
import jax.numpy as jnp
from jax import lax
from jax.experimental import pallas as pl
from jax.experimental.pallas import tpu as pltpu

HEAD = 128
CHUNK = 64
N_META = 16
ROW_PAD = 128 - N_META
EPS = 1e-6
MASK_VALUE = -1e30
ADAM_LR = 0.001
ADAM_B1 = 0.9
ADAM_B2 = 0.999
ADAM_EPS = 1e-08
ADAM_WD = 0.01
ADAM_STEP = 10
N_DEV = 8
N_CHIP = 4
MIB = 1024 * 1024
AXES = ("x", "y", "c")
MESH = pl.DeviceIdType.MESH

F32 = jnp.float32
MXU = jnp.bfloat16
PAYLOAD = jnp.bfloat16
HI = lax.Precision.HIGHEST

NN = (((1,), (0,)), ((), ()))
NT = (((1,), (1,)), ((), ()))
TN = (((0,), (0,)), ((), ()))


def _dot(a, b, dims=NN):
    return lax.dot_general(a.astype(MXU), b.astype(MXU), dims, preferred_element_type=F32)


def _dot_exact(a, b):
    return lax.dot_general(a, b, NN, precision=HI, preferred_element_type=F32)


def _sigmoid(x):
    return 1.0 / (1.0 + jnp.exp(-x))


def _tile(dim, target, unit=128):
    best = None
    t = unit
    while t <= min(dim, target):
        if dim % t == 0:
            best = t
        t += unit
    return best if best is not None else dim


def _cparams(semantics, vmem_mib):
    return pltpu.CompilerParams(dimension_semantics=semantics, vmem_limit_bytes=vmem_mib * MIB)


def _mat_spec(arr, br, bc, rc_of_grid):
    if arr.ndim == 2:
        return pl.BlockSpec((br, bc), rc_of_grid)
    assert arr.shape[2] % bc == 0, (arr.shape, bc)
    per = arr.shape[2] // bc

    def idx(*g):
        r, c = rc_of_grid(*g)
        return (c // per, r, c % per)

    return pl.BlockSpec((None, br, bc), idx)


def _mat_shape(arr):
    return (arr.shape[0], arr.shape[1]) if arr.ndim == 2 else (arr.shape[1], arr.shape[0] * arr.shape[2])


def _matmul(a, b, dims, out_dtype, name, *, add=None, out_parts=1, tm=512, tn=512, tk=512, vmem_mib=48, sides=(),
            k_whole=False, rows=None, into=None):
    ar, ac = _mat_shape(a)
    br_, bc_ = _mat_shape(b)
    if dims == "nn":
        m, k, n = ar, ac, bc_
        assert br_ == k
    elif dims == "nt":
        m, k, n = ar, ac, br_
        assert bc_ == k
    else:
        m, k, n = ac, ar, bc_
        assert br_ == k
    m_unit, n_unit, k_unit = m, n, k
    if a.ndim == 3:
        if dims == "tn":
            m_unit = math.gcd(m_unit, a.shape[2])
        else:
            k_unit = math.gcd(k_unit, a.shape[2])
    if b.ndim == 3:
        if dims == "nt":
            k_unit = math.gcd(k_unit, b.shape[2])
        else:
            n_unit = math.gcd(n_unit, b.shape[2])
    if out_parts > 1:
        n_unit = math.gcd(n_unit, n // out_parts)
    tm = _tile(m_unit, tm, 128 if dims == "tn" else 64)
    tn, tk = _tile(n_unit, tn), _tile(k_unit, tk)
    whole_k = dims == "nt" and tk < k and k_whole
    if whole_k:
        k_chunk, tk = tk, k
    gm, gn, gk = m // tm, n // tn, k // tk
    assert gm * tm == m and gn * tn == n and gk * tk == k, (name, m, n, k, tm, tn, tk)

    def chunk_of(ref, arr, c):
        if arr.ndim == 2:
            return ref[:, c * k_chunk:(c + 1) * k_chunk]
        per = arr.shape[2] // k_chunk
        return ref[c // per, :, (c % per) * k_chunk:(c % per + 1) * k_chunk]

    def all_cols(arr, rows, row_of_grid):
        if arr.ndim == 2:
            return pl.BlockSpec((rows, arr.shape[1]), lambda i, j, kk: (row_of_grid(i, j), 0))
        return pl.BlockSpec((arr.shape[0], rows, arr.shape[2]), lambda i, j, kk: (0, row_of_grid(i, j), 0))

    if dims == "nn":
        a_spec = _mat_spec(a, tm, tk, lambda i, j, kk: (i, kk))
        b_spec = _mat_spec(b, tk, tn, lambda i, j, kk: (kk, j))
        dn = NN
    elif whole_k:
        a_spec = all_cols(a, tm, lambda i, j: i)
        b_spec = all_cols(b, tn, lambda i, j: j)
        dn = NT
    elif dims == "nt":
        a_spec = _mat_spec(a, tm, tk, lambda i, j, kk: (i, kk))
        b_spec = _mat_spec(b, tn, tk, lambda i, j, kk: (j, kk))
        dn = NT
    else:
        a_spec = _mat_spec(a, tk, tm, lambda i, j, kk: (kk, i))
        b_spec = _mat_spec(b, tk, tn, lambda i, j, kk: (kk, j))
        dn = TN

    if out_parts > 1:
        per = (n // out_parts) // tn
        out_shape = jax.ShapeDtypeStruct((out_parts, m, n // out_parts), out_dtype)
        o_spec = pl.BlockSpec((None, tm, tn), lambda i, j, kk: (j // per, i, j % per))
    else:
        out_shape = jax.ShapeDtypeStruct((m, n), out_dtype)
        o_spec = pl.BlockSpec((tm, tn), lambda i, j, kk: (i, j))

    in_specs = [a_spec, b_spec]
    args = [a, b]
    if add is not None:
        in_specs.append(pl.BlockSpec((tm, tn), lambda i, j, kk: (i, j)))
        args.append(add)
    if rows is not None:
        first_tile, gm = rows

        def shifted(spec):
            return pl.BlockSpec(spec.block_shape, lambda i, j, kk: spec.index_map(i + first_tile, j, kk))

        in_specs = [shifted(s) for s in in_specs]
        o_spec = shifted(o_spec)
    aliases = {}
    if into is not None:
        aliases[len(args)] = 0
        in_specs.append(_any_spec())
        args.append(into)

    def body(*refs):
        a_ref, b_ref, *others, o_ref, acc_ref = refs
        add_ref = others[0] if add is not None else None
        kk = pl.program_id(2)
        if whole_k:
            part = sum(lax.dot_general(chunk_of(a_ref, a, c).astype(MXU), chunk_of(b_ref, b, c).astype(MXU), dn,
                                       preferred_element_type=F32) for c in range(k // k_chunk))
        else:
            part = lax.dot_general(a_ref[...].astype(MXU), b_ref[...].astype(MXU), dn, preferred_element_type=F32)

        def finish(total):
            if add_ref is not None:
                total = total + add_ref[...]
            o_ref[...] = total.astype(o_ref.dtype)

        if gk == 1:
            finish(part)
        else:
            @pl.when(kk == 0)
            def _():
                acc_ref[...] = part

            @pl.when(jnp.logical_and(kk > 0, kk < gk - 1))
            def _():
                acc_ref[...] += part

            @pl.when(kk == gk - 1)
            def _():
                finish(acc_ref[...] + part)

    (out,), side_results = _pallas(
        body,
        name=name,
        args=args,
        grid=(gm, gn, gk),
        in_specs=in_specs,
        out_specs=[o_spec],
        out_shape=[out_shape],
        scratch_shapes=[pltpu.VMEM((tm, tn) if gk > 1 else (8, 128), F32)],
        semantics=("parallel", "parallel", "arbitrary"),
        vmem_mib=vmem_mib,
        sides=sides,
        aliases=aliases,
    )
    return (out, side_results) if sides else out


def _rms_fwd(h, gains, name, tm=384):
    lp, d = h.shape
    tm = _tile(lp, tm)
    n = len(gains)

    def body(*refs):
        h_ref = refs[0]
        g_refs = refs[1:1 + n]
        o_refs = refs[1 + n:]
        x = h_ref[...]
        y = x * lax.rsqrt(jnp.mean(x * x, axis=-1, keepdims=True) + EPS)
        for g_ref, o_ref in zip(g_refs, o_refs):
            o_ref[...] = (y * g_ref[...]).astype(o_ref.dtype)

    row = pl.BlockSpec((tm, d), lambda i: (i, 0))
    vec = pl.BlockSpec((1, d), lambda i: (0, 0))
    return pl.pallas_call(
        body,
        name=name,
        grid=(lp // tm,),
        in_specs=[row] + [vec] * n,
        out_specs=[row] * n,
        out_shape=[jax.ShapeDtypeStruct((lp, d), MXU)] * n,
        compiler_params=_cparams(("parallel",), 40),
    )(h, *gains)


def _rms_bwd(h, gains, dys, res, name, tm=384):
    lp, d = h.shape
    tm = _tile(lp, tm)
    n = len(gains)

    def body(*refs):
        h_ref, res_ref = refs[0], refs[1]
        g_refs = refs[2:2 + n]
        dy_refs = refs[2 + n:2 + 2 * n]
        dh_ref = refs[2 + 2 * n]
        dg_refs = refs[3 + 2 * n:]
        i = pl.program_id(0)
        x = h_ref[...]
        rstd = lax.rsqrt(jnp.mean(x * x, axis=-1, keepdims=True) + EPS)
        xhat = x * rstd
        dh = res_ref[...]
        for g_ref, dy_ref, dg_ref in zip(g_refs, dy_refs, dg_refs):
            dy = dy_ref[...]
            gdy = dy * g_ref[...]
            dh = dh + rstd * (gdy - xhat * jnp.mean(gdy * xhat, axis=-1, keepdims=True))
            part = jnp.sum(dy * xhat, axis=0, keepdims=True)

            @pl.when(i == 0)
            def _():
                dg_ref[...] = part

            @pl.when(i > 0)
            def _():
                dg_ref[...] += part

        dh_ref[...] = dh

    row = pl.BlockSpec((tm, d), lambda i: (i, 0))
    vec = pl.BlockSpec((1, d), lambda i: (0, 0))
    outs = pl.pallas_call(
        body,
        name=name,
        grid=(lp // tm,),
        in_specs=[row, row] + [vec] * n + [row] * n,
        out_specs=[row] + [vec] * n,
        out_shape=[jax.ShapeDtypeStruct((lp, d), F32)] + [jax.ShapeDtypeStruct((1, d), F32)] * n,
        compiler_params=_cparams(("arbitrary",), 56),
    )(h, res, *gains, *dys)
    return outs[0], list(outs[1:])


def _lb_fwd(gamma):
    def body(g_ref, lb_ref):
        g = g_ref[...]
        e = jnp.exp(g - jnp.max(g, axis=0, keepdims=True))
        lb_ref[...] = (e / jnp.sum(e, axis=0, keepdims=True))[0:1, :]

    return pl.pallas_call(body, name="lb_fwd", out_shape=jax.ShapeDtypeStruct((1, gamma.shape[1]), F32))(gamma)


def _lb_bwd(gamma, dlb):
    def body(g_ref, dlb_ref, dg_ref):
        g = g_ref[...]
        e = jnp.exp(g - jnp.max(g, axis=0, keepdims=True))
        s = e / jnp.sum(e, axis=0, keepdims=True)
        rows = lax.broadcasted_iota(jnp.int32, g.shape, 0)
        ds = jnp.where(rows == 0, dlb_ref[...], 0.0)
        dg_ref[...] = s * (ds - jnp.sum(s * ds, axis=0, keepdims=True))

    return pl.pallas_call(body, name="lb_bwd", out_shape=jax.ShapeDtypeStruct(gamma.shape, F32))(gamma, dlb)


def _tri(n, lower):
    r = lax.broadcasted_iota(jnp.int32, (n, n), 0)
    c = lax.broadcasted_iota(jnp.int32, (n, n), 1)
    return jnp.where((r >= c) if lower else (r <= c), 1.0, 0.0).astype(F32)


def _group(nc):
    return max(u for u in (1, 2, 3, 6, 11) if nc % u == 0)


def _running_sum(tri, x):
    hi = x.astype(MXU)
    rest = x - hi.astype(F32)
    mid = rest.astype(MXU)
    lo = (rest - mid.astype(F32)).astype(MXU)
    return _dot(tri, hi) + _dot(tri, mid) + _dot(tri, lo)


def _causal(n):
    r = lax.broadcasted_iota(jnp.int32, (n, n), 0)
    c = lax.broadcasted_iota(jnp.int32, (n, n), 1)
    return r >= c


def _chunk_gates(u_ref, lb, c):
    sl = pl.ds(pl.multiple_of(c * CHUNK, CHUNK), CHUNK)
    valid = (c * CHUNK + lax.broadcasted_iota(jnp.int32, (CHUNK, HEAD), 0)) >= ROW_PAD
    uq = u_ref[0, sl, :]
    uf = u_ref[1, sl, :]
    sq = _sigmoid(uq)
    sf = _sigmoid(uf)
    fg = lb + (1.0 - lb) * sf
    return dict(sl=sl, valid=valid, uq=uq, sq=sq, sf=sf, fg=fg, q=jnp.where(valid, uq * sq, 0.0),
                logf=jnp.where(valid, jnp.log(fg), 0.0), k=jnp.where(valid, 1.0 - fg, 0.0),
                v=jnp.where(valid, u_ref[2, sl, :], 0.0))


def _chunk_decays(x, b):
    b_last = b[CHUNK - 1:CHUNK, :]
    b_mid = b[CHUNK // 2 - 1:CHUNK // 2, :]
    e_qi = jnp.exp(b - b_mid)
    e_ki = jnp.exp(b_mid - b)
    e_kd = jnp.exp(b_last - b)
    e_qe = jnp.exp(b)
    q, k = x["q"], x["k"]
    return dict(x, e_qi=e_qi, e_ki=e_ki, e_kd=e_kd, e_qe=e_qe, qi=q * e_qi, ki=k * e_ki, kd=k * e_kd, qe=q * e_qe,
                decay=jnp.exp(b_last))


def _chunks(u_ref, lb, tri_lower, cs):
    gates = [_chunk_gates(u_ref, lb, c) for c in cs]
    sums = [_running_sum(tri_lower, x["logf"]) for x in gates]
    return [_chunk_decays(x, b) for x, b in zip(gates, sums)]


def _hgrn2_fwd(u4, lb, g_out, name="hgrn2_fwd", sides=()):
    _, lp, d = u4.shape
    nh, nc = d // HEAD, lp // CHUNK
    per = _group(nc)

    def body(u_ref, lb_ref, g_ref, o_ref, og_ref):
        lb_v = lb_ref[...]
        g = g_ref[...]

        tri_lower = _tri(CHUNK, True).astype(MXU)
        causal = _causal(CHUNK)

        def step(i, st):
            xs = _chunks(u_ref, lb_v, tri_lower, [i * per + u for u in range(per)])
            scores = [_dot(x["qi"], x["ki"], NT) for x in xs]
            updates = [_dot(x["v"], x["kd"], TN) for x in xs]
            states = []
            for x, upd in zip(xs, updates):
                states.append(st)
                st = x["decay"] * st + upd
            outs = [_dot(jnp.where(causal, a, 0.0), x["v"]) + _dot(x["qe"], s, NT)
                    for x, a, s in zip(xs, scores, states)]
            for x, o in zip(xs, outs):
                o_ref[x["sl"], :] = o
                on = o * lax.rsqrt(jnp.mean(o * o, axis=-1, keepdims=True) + EPS) * g
                z = u_ref[3, x["sl"], :]
                og_ref[x["sl"], :] = (on * (z * _sigmoid(z))).astype(og_ref.dtype)
            return st

        lax.fori_loop(0, nc // per, step, jnp.zeros((HEAD, HEAD), F32))

    slab = pl.BlockSpec((lp, HEAD), lambda h: (0, h))
    vec = pl.BlockSpec((1, HEAD), lambda h: (0, h))
    outs, side_results = _pallas(
        body,
        name=name,
        args=(u4, lb, g_out),
        grid=(nh,),
        in_specs=[pl.BlockSpec((4, lp, HEAD), lambda h: (0, 0, h)), vec, vec],
        out_specs=[slab, slab],
        out_shape=[jax.ShapeDtypeStruct((lp, d), F32), jax.ShapeDtypeStruct((lp, d), MXU)],
        semantics=("parallel",),
        vmem_mib=48,
        sides=sides,
    )
    return (*outs, side_results) if sides else tuple(outs)


def _hgrn2_bwd(u4, o, d_og, lb, g_out, name="hgrn2_bwd", sides=()):
    _, lp, d = u4.shape
    nh, nc = d // HEAD, lp // CHUNK
    per = _group(nc)

    def body(u_ref, o_ref, dog_ref, lb_ref, g_ref, du_ref, dlb_ref, dg_ref, st_ref, do_ref):
        lb_v = lb_ref[...]
        g = g_ref[...]

        tri_lower = _tri(CHUNK, True).astype(MXU)
        tri_upper = _tri(CHUNK, False).astype(MXU)
        causal = _causal(CHUNK)

        def fwd_step(i, carry):
            st, dg_acc = carry
            cs = [i * per + u for u in range(per)]
            xs = _chunks(u_ref, lb_v, tri_lower, cs)
            updates = [_dot(x["v"], x["kd"], TN) for x in xs]
            for c, x, upd in zip(cs, xs, updates):
                st_ref[c] = st
                st = x["decay"] * st + upd
            for x in xs:
                sl = x["sl"]
                ov = o_ref[sl, :]
                rstd = lax.rsqrt(jnp.mean(ov * ov, axis=-1, keepdims=True) + EPS)
                on = ov * rstd
                z = u_ref[3, sl, :]
                sz = _sigmoid(z)
                dog = dog_ref[sl, :]
                dy = dog * (z * sz)
                dz = dog * (on * g) * (sz * (1.0 + z * (1.0 - sz)))
                du_ref[3, sl, :] = dz.astype(du_ref.dtype)
                gdy = dy * g
                do = rstd * (gdy - on * jnp.mean(gdy * on, axis=-1, keepdims=True))
                do_ref[sl, :] = jnp.where(x["valid"], do, 0.0)
                dg_acc = dg_acc + jnp.sum(dy * on, axis=0, keepdims=True)
            return st, dg_acc

        _, dg_tot = lax.fori_loop(0, nc // per, fwd_step, (jnp.zeros((HEAD, HEAD), F32), jnp.zeros((1, HEAD), F32)))
        dg_ref[...] = dg_tot

        def bwd_step(i, carry):
            gt, dlb_acc = carry
            cs = [nc - 1 - (i * per + u) for u in range(per)]
            xs = _chunks(u_ref, lb_v, tri_lower, cs)
            dos = [do_ref[x["sl"], :] for x in xs]
            sts = [st_ref[c] for c in cs]
            scores = [jnp.where(causal, _dot(x["qi"], x["ki"], NT), 0.0) for x in xs]
            d_scores = [jnp.where(causal, _dot(do, x["v"], NT), 0.0) for x, do in zip(xs, dos)]
            d_qes = [_dot(do, st) for do, st in zip(dos, sts)]
            g_updates = [_dot(do, x["qe"], TN) for x, do in zip(xs, dos)]
            gts = []
            for x, upd in zip(xs, g_updates):
                gts.append(gt)
                gt = x["decay"] * gt + upd
            d_kds = [_dot(x["v"], g_) for x, g_ in zip(xs, gts)]
            dvs = [_dot(x["kd"], g_, NT) + _dot(a, do, TN) for x, g_, a, do in zip(xs, gts, scores, dos)]
            d_qis = [_dot(d_a, x["ki"]) for x, d_a in zip(xs, d_scores)]
            d_kis = [_dot(d_a, x["qi"], TN) for x, d_a in zip(xs, d_scores)]
            rows = lax.broadcasted_iota(jnp.int32, (CHUNK, HEAD), 0)
            dbs = []
            for x, g_, st, d_qi, d_ki, d_qe, d_kd in zip(xs, gts, sts, d_qis, d_kis, d_qes, d_kds):
                t_qi, t_ki, t_qe, t_kd = d_qi * x["qi"], d_ki * x["ki"], d_qe * x["qe"], d_kd * x["kd"]
                d_decay = jnp.sum(g_ * st, axis=0, keepdims=True)
                d_mid = jnp.sum(t_ki - t_qi, axis=0, keepdims=True)
                d_last = jnp.sum(t_kd, axis=0, keepdims=True) + d_decay * x["decay"]
                dbs.append(t_qi - t_ki + t_qe - t_kd + jnp.where(rows == CHUNK // 2 - 1, d_mid, 0.0)
                           + jnp.where(rows == CHUNK - 1, d_last, 0.0))
            dlogfs = [_running_sum(tri_upper, db) for db in dbs]
            for x, dlogf, dv, d_qi, d_ki, d_qe, d_kd in zip(xs, dlogfs, dvs, d_qis, d_kis, d_qes, d_kds):
                sl = x["sl"]
                dq = d_qi * x["e_qi"] + d_qe * x["e_qe"]
                dk = d_ki * x["e_ki"] + d_kd * x["e_kd"]
                valid, sq, sf, uq = x["valid"], x["sq"], x["sf"], x["uq"]
                dfg = jnp.where(valid, dlogf / x["fg"] - dk, 0.0)
                du_ref[0, sl, :] = jnp.where(valid, dq * (sq * (1.0 + uq * (1.0 - sq))), 0.0).astype(du_ref.dtype)
                du_ref[1, sl, :] = (dfg * (1.0 - lb_v) * (sf * (1.0 - sf))).astype(du_ref.dtype)
                du_ref[2, sl, :] = jnp.where(valid, dv, 0.0).astype(du_ref.dtype)
                dlb_acc = dlb_acc + jnp.sum(dfg * (1.0 - sf), axis=0, keepdims=True)
            return gt, dlb_acc

        _, dlb_tot = lax.fori_loop(0, nc // per, bwd_step, (jnp.zeros((HEAD, HEAD), F32), jnp.zeros((1, HEAD), F32)))
        dlb_ref[...] = dlb_tot

    slab = pl.BlockSpec((lp, HEAD), lambda h: (0, h))
    vec = pl.BlockSpec((1, HEAD), lambda h: (0, h))
    quad = pl.BlockSpec((4, lp, HEAD), lambda h: (0, 0, h))
    outs, side_results = _pallas(
        body,
        name=name,
        args=(u4, o, d_og, lb, g_out),
        grid=(nh,),
        in_specs=[quad, slab, slab, vec, vec],
        out_specs=[quad, vec, vec],
        out_shape=[jax.ShapeDtypeStruct((4, lp, d), MXU), jax.ShapeDtypeStruct((1, d), F32),
                   jax.ShapeDtypeStruct((1, d), F32)],
        scratch_shapes=[pltpu.VMEM((nc, HEAD, HEAD), F32), pltpu.VMEM((lp, HEAD), F32)],
        semantics=("parallel",),
        vmem_mib=58,
        sides=sides,
    )
    return (*outs, side_results) if sides else tuple(outs)


WIDE = 2 * HEAD
INV_SCALE = HEAD ** 0.5


def _split3(x):
    hi = x.astype(MXU).astype(F32)
    rest = x - hi
    mid = rest.astype(MXU).astype(F32)
    return hi, mid, (rest - mid).astype(MXU).astype(F32)


def _extra_cols(rows, first, second):
    lane = lax.broadcasted_iota(jnp.int32, (rows, HEAD), 1)
    out = jnp.where(lane < 6, 1.0, 0.0).astype(F32)
    for base, terms in ((0, first), (3, second)):
        if terms is not None:
            for j, term in enumerate(terms):
                out = jnp.where(lane == base + j, term, out)
    return out


def _head_col(a, h):
    lane = lax.broadcasted_iota(jnp.int32, a.shape, 1)
    return jnp.sum(jnp.where(lane == h, a, 0.0), axis=-1, keepdims=True)


def _attn_operands(ub, ukv, g_q, g_k, f_cum, name="attn_operands", tm=384):
    _, lp, d = ub.shape
    tm = _tile(lp, tm)
    nh = d // HEAD

    def body(q_ref, k_ref, v_ref, gq_ref, gk_ref, f_ref, qa_ref, ka_ref, va_ref):
        i = pl.program_id(0)
        f = f_ref[...]
        is_pad = (i * tm + lax.broadcasted_iota(jnp.int32, (tm, 1), 0)) < ROW_PAD
        ones_only = _extra_cols(tm, None, (0.0, 0.0, 0.0)).astype(MXU)
        for h in range(nh):
            hs = slice(h * HEAD, (h + 1) * HEAD)
            lo, hi = h * WIDE, h * WIDE + HEAD
            f_h = _head_col(f, h)
            for x_ref, g_ref, o_ref in ((q_ref, gq_ref, qa_ref), (k_ref, gk_ref, ka_ref)):
                x = x_ref[:, hs]
                y = x * lax.rsqrt(jnp.mean(x * x, axis=-1, keepdims=True) + EPS)
                o_ref[:, lo:hi] = (y * g_ref[:, hs]).astype(o_ref.dtype)
            qa_ref[:, hi:hi + HEAD] = _extra_cols(tm, _split3(f_h * INV_SCALE), None).astype(MXU)
            f_key = jnp.where(is_pad, -MASK_VALUE, f_h)
            ka_ref[:, hi:hi + HEAD] = _extra_cols(tm, None, _split3(-f_key * INV_SCALE)).astype(MXU)
            va_ref[:, lo:hi] = v_ref[:, hs].astype(MXU)
            va_ref[:, hi:hi + HEAD] = ones_only

    wide = pl.BlockSpec((tm, nh * WIDE), lambda i: (i, 0))
    vec = pl.BlockSpec((1, d), lambda i: (0, 0))
    return pl.pallas_call(
        body,
        name=name,
        grid=(lp // tm,),
        in_specs=[pl.BlockSpec((None, tm, d), lambda i: (0, i, 0)), pl.BlockSpec((None, tm, d), lambda i: (0, i, 0)),
                  pl.BlockSpec((None, tm, d), lambda i: (1, i, 0)), vec, vec, pl.BlockSpec((tm, 128), lambda i: (i, 0))],
        out_specs=[wide, wide, wide],
        out_shape=[jax.ShapeDtypeStruct((lp, nh * WIDE), MXU)] * 3,
        compiler_params=_cparams(("parallel",), 56),
    )(ub, ukv, ukv, g_q, g_k, f_cum)


def _head_rms_bwd_tile(x, g, dy):
    rstd = lax.rsqrt(jnp.mean(x * x, axis=-1, keepdims=True) + EPS)
    xhat = x * rstd
    gdy = dy * g
    dx = rstd * (gdy - xhat * jnp.mean(gdy * xhat, axis=-1, keepdims=True))
    return dx, jnp.sum(dy * xhat, axis=0, keepdims=True)


def _fgate_fwd(ufl, b_f):
    lp = ufl.shape[0]
    nb = lp // 128

    def body(u_ref, b_ref, f_ref):
        def step(i, carry):
            sl = pl.ds(pl.multiple_of(i * 128, 128), 128)
            valid = (i * 128 + lax.broadcasted_iota(jnp.int32, (128, 128), 0)) >= ROW_PAD
            x = u_ref[sl, :] + b_ref[...]
            logf = jnp.where(valid, jnp.minimum(x, 0.0) - jnp.log(1.0 + jnp.exp(-jnp.abs(x))), 0.0)
            f = _dot_exact(_tri(128, True), logf) + carry
            f_ref[sl, :] = f
            return f[127:128, :]

        lax.fori_loop(0, nb, step, jnp.zeros((1, 128), F32))

    return pl.pallas_call(body, name="fgate_fwd", out_shape=jax.ShapeDtypeStruct((lp, 128), F32))(ufl, b_f)


def _fgate_bwd(ufl, b_f, d_f):
    lp = ufl.shape[0]
    nb = lp // 128

    def body(u_ref, b_ref, df_ref, du_ref, db_ref):
        def step(j, carry):
            later, db_acc = carry
            i = nb - 1 - j
            sl = pl.ds(pl.multiple_of(i * 128, 128), 128)
            valid = (i * 128 + lax.broadcasted_iota(jnp.int32, (128, 128), 0)) >= ROW_PAD
            x = u_ref[sl, :] + b_ref[...]
            df = df_ref[sl, :]
            dlogf = _dot_exact(_tri(128, False), df) + later
            dx = jnp.where(valid, dlogf * _sigmoid(-x), 0.0)
            du_ref[sl, :] = dx.astype(du_ref.dtype)
            return later + jnp.sum(df, axis=0, keepdims=True), db_acc + jnp.sum(dx, axis=0, keepdims=True)

        _, db_tot = lax.fori_loop(0, nb, step, (jnp.zeros((1, 128), F32), jnp.zeros((1, 128), F32)))
        db_ref[...] = db_tot

    return pl.pallas_call(
        body, name="fgate_bwd",
        out_shape=[jax.ShapeDtypeStruct((lp, 128), MXU), jax.ShapeDtypeStruct((1, 128), F32)],
    )(ufl, b_f, d_f)


STRIP = 32
PAIR = 2


def _strip_causal(r, t):
    row = r + lax.broadcasted_iota(jnp.int32, (STRIP, t), 0)
    col = lax.broadcasted_iota(jnp.int32, (STRIP, t), 1)
    return col <= row


def _fox_fwd(qa, ka, va, ub, name="fox_fwd", t=384):
    lp = qa.shape[0]
    d = ub.shape[2]
    t = _tile(lp, t)
    nh, nq = d // HEAD, lp // t
    scale = HEAD ** -0.5

    assert nh % PAIR == 0
    heads = range(PAIR)

    def body(q_ref, k_ref, v_ref, z_ref, o_ref, olo_ref, og_ref, qb_ref,
             s_ref, p_ref, m_ref, a_ref, l_ref, acc_ref):
        qb = pl.program_id(1)
        m_ref[...] = jnp.full((PAIR, t, 1), MASK_VALUE, F32)
        l_ref[...] = jnp.zeros((PAIR, t, 128), F32)
        acc_ref[...] = jnp.zeros((PAIR, 2 * t, HEAD), F32)
        p_ref[1] = jnp.zeros((PAIR, 2 * t, t), MXU)
        a_ref[1] = jnp.ones((PAIR, t, 1), F32)

        def scores(kb, buf):
            ks = pl.ds(pl.multiple_of(kb * t, t), t)
            for j in heads:
                ws = slice(j * WIDE, (j + 1) * WIDE)
                s_ref[buf, j] = _dot(q_ref[:, ws], k_ref[ks, ws], NT)

        def weighted_sum(kb, buf):
            ks = pl.ds(pl.multiple_of(kb * t, t), t)
            for j in heads:
                pv = _dot(p_ref[buf, j], v_ref[ks, j * WIDE:j * WIDE + HEAD])
                alpha = a_ref[buf, j]
                acc_ref[j, 0:t, :] = alpha * acc_ref[j, 0:t, :] + pv[0:t]
                acc_ref[j, t:2 * t, :] = alpha * acc_ref[j, t:2 * t, :] + pv[t:2 * t]

        def softmax_update(buf, diagonal):
            for j in heads:
                for r in range(0, t, STRIP):
                    rs = slice(r, r + STRIP)
                    x = s_ref[buf, j, rs, :] * scale
                    if diagonal:
                        x = jnp.where(_strip_causal(r, t), x, MASK_VALUE)
                    m_old = m_ref[j, rs, :]
                    m_new = jnp.maximum(m_old, jnp.max(x, axis=-1, keepdims=True))
                    alpha = jnp.exp(m_old - m_new)
                    p = jnp.exp(x - m_new)
                    m_ref[j, rs, :] = m_new
                    a_ref[buf, j, rs, :] = alpha
                    l_ref[j, rs, :] = alpha * l_ref[j, rs, :] + sum(p[:, c:c + 128] for c in range(0, t, 128))
                    p_hi = p.astype(MXU)
                    p_ref[buf, j, rs, :] = p_hi
                    p_ref[buf, j, t + r:t + r + STRIP, :] = (p - p_hi.astype(F32)).astype(MXU)

        def off_diagonal(kb, cur):
            weighted_sum(jnp.maximum(kb - 1, 0), 1 - cur)
            scores(kb + 1, 1 - cur)
            softmax_update(cur, False)

        def diagonal(cur):
            weighted_sum(jnp.maximum(qb - 1, 0), 1 - cur)
            softmax_update(cur, True)
            weighted_sum(qb, cur)

        def two_blocks(i, carry):
            off_diagonal(2 * i, 0)
            off_diagonal(2 * i + 1, 1)
            return carry

        scores(0, 0)
        lax.fori_loop(0, qb // 2, two_blocks, 0)

        @pl.when(lax.rem(qb, 2) == 0)
        def _():
            diagonal(0)

        @pl.when(lax.rem(qb, 2) == 1)
        def _():
            off_diagonal(qb - 1, 0)
            diagonal(1)

        is_pad = (qb * t + lax.broadcasted_iota(jnp.int32, (t, 1), 0)) < ROW_PAD
        for j in heads:
            hs = slice(j * HEAD, (j + 1) * HEAD)
            l = jnp.sum(l_ref[j], axis=-1, keepdims=True)
            o = acc_ref[j, 0:t, :] / l
            o_ref[:, hs] = o
            olo_ref[:, hs] = acc_ref[j, t:2 * t, :] / l
            z = z_ref[:, hs]
            og_ref[:, hs] = (o * (z * _sigmoid(z))).astype(og_ref.dtype)
            extra = q_ref[:, j * WIDE + HEAD:(j + 1) * WIDE].astype(F32)
            f_scaled = extra[:, 0:1] + extra[:, 1:2] + extra[:, 2:3]
            log_term = jnp.where(is_pad, MASK_VALUE * INV_SCALE, f_scaled - (m_ref[j] + jnp.log(l)) * INV_SCALE)
            qb_ref[:, j * WIDE:j * WIDE + HEAD] = q_ref[:, j * WIDE:j * WIDE + HEAD]
            qb_ref[:, j * WIDE + HEAD:(j + 1) * WIDE] = _extra_cols(t, _split3(log_term), None).astype(qb_ref.dtype)

    scratch = [pltpu.VMEM((2, PAIR, t, t), F32), pltpu.VMEM((2, PAIR, 2 * t, t), MXU), pltpu.VMEM((PAIR, t, 1), F32),
               pltpu.VMEM((2, PAIR, t, 1), F32), pltpu.VMEM((PAIR, t, 128), F32), pltpu.VMEM((PAIR, 2 * t, HEAD), F32)]
    tile = pl.BlockSpec((t, PAIR * HEAD), lambda g, i: (i, g))
    wide_tile = pl.BlockSpec((t, PAIR * WIDE), lambda g, i: (i, g))
    wide_all = pl.BlockSpec((lp, PAIR * WIDE), lambda g, i: (0, g))
    return pl.pallas_call(
        body,
        name=name,
        grid=(nh // PAIR, nq),
        in_specs=[wide_tile, wide_all, wide_all, pl.BlockSpec((None, t, PAIR * HEAD), lambda g, i: (1, i, g))],
        out_specs=[tile, tile, tile, wide_tile],
        out_shape=[jax.ShapeDtypeStruct((lp, d), F32), jax.ShapeDtypeStruct((lp, d), F32),
                   jax.ShapeDtypeStruct((lp, d), MXU), jax.ShapeDtypeStruct((lp, nh * WIDE), MXU)],
        scratch_shapes=scratch,
        compiler_params=_cparams(("parallel", "arbitrary"), 48),
    )(qa, ka, va, ub)


def _fox_bwd(qb, ka, va, doa, ukv, g_k, ub, g_q, name="fox_bwd", t=384, sides=()):
    lp = qb.shape[0]
    d = ukv.shape[2]
    t = _tile(lp, t)
    nh, nk = d // HEAD, lp // t
    scale = HEAD ** -0.5

    assert nh % PAIR == 0
    heads = range(PAIR)

    def body(q_ref, do_ref, k_ref, v_ref, kraw_ref, gk_ref, qraw_ref, gq_ref, dqraw_ref, dgq_ref, dukv_ref, dgk_ref,
             dfk_ref, s_ref, dp_ref, p_ref, ds_ref, col_ref, dk_ref, dv_ref, dq_ref):
        kb = pl.program_id(1)

        @pl.when(kb == 0)
        def _():
            dq_ref[...] = jnp.zeros_like(dq_ref)
            dgk_ref[...] = jnp.zeros_like(dgk_ref)

        dk_ref[...] = jnp.zeros_like(dk_ref)
        dv_ref[...] = jnp.zeros_like(dv_ref)
        col_ref[...] = jnp.zeros_like(col_ref)

        def step(qb, diagonal):
            qs = pl.ds(pl.multiple_of(qb * t, t), t)
            for j in heads:
                ws = slice(j * WIDE, (j + 1) * WIDE)
                s_ref[j] = _dot(q_ref[qs, ws], k_ref[:, ws], NT)
                dp_ref[j] = _dot(do_ref[qs, ws], v_ref[:, ws], NT)
            for j in heads:
                for r in range(0, t, STRIP):
                    rs = slice(r, r + STRIP)
                    x = s_ref[j, rs, :] * scale
                    if diagonal:
                        x = jnp.where(_strip_causal(r, t), x, MASK_VALUE)
                    p = jnp.exp(x)
                    ds = p * dp_ref[j, rs, :]
                    p_ref[j, rs, :] = p.astype(MXU)
                    ds_ref[j, rs, :] = (ds * scale).astype(MXU)
                    col_ref[j] += ds
            for j in heads:
                hs = slice(j * HEAD, (j + 1) * HEAD)
                narrow = slice(j * WIDE, j * WIDE + HEAD)
                dsb = ds_ref[j]
                dv_ref[:, hs] += _dot(p_ref[j], do_ref[qs, narrow], TN)
                dq_ref[qs, hs] += _dot(dsb, k_ref[:, narrow])
                dk_ref[:, hs] += _dot(dsb, q_ref[qs, narrow], TN)

        def off_diagonal(qb, carry):
            step(qb, False)
            return carry

        step(kb, True)
        lax.fori_loop(kb + 1, nk, off_diagonal, 0)
        for j in heads:
            hs = slice(j * HEAD, (j + 1) * HEAD)
            dfk_ref[j] = -jnp.sum(col_ref[j], axis=0, keepdims=True)
            dx, dg = _head_rms_bwd_tile(kraw_ref[:, hs], gk_ref[:, hs], dk_ref[:, hs])
            dukv_ref[0, :, hs] = dx.astype(dukv_ref.dtype)
            dukv_ref[1, :, hs] = dv_ref[:, hs].astype(dukv_ref.dtype)
            dgk_ref[:, hs] += dg

        @pl.when(kb == nk - 1)
        def _():
            dgq_ref[...] = jnp.zeros_like(dgq_ref)

            def rows(c, carry):
                rs = pl.ds(pl.multiple_of(c * t, t), t)
                for j in heads:
                    hs = slice(j * HEAD, (j + 1) * HEAD)
                    dx, dg = _head_rms_bwd_tile(qraw_ref[rs, hs], gq_ref[:, hs], dq_ref[rs, hs])
                    dqraw_ref[rs, hs] = dx.astype(dqraw_ref.dtype)
                    dgq_ref[:, hs] += dg
                return carry

            lax.fori_loop(0, nk, rows, 0)

    scratch = [pltpu.VMEM((PAIR, t, t), F32), pltpu.VMEM((PAIR, t, t), F32), pltpu.VMEM((PAIR, t, t), MXU),
               pltpu.VMEM((PAIR, t, t), MXU), pltpu.VMEM((PAIR, STRIP, t), F32),
               pltpu.VMEM((t, PAIR * HEAD), F32), pltpu.VMEM((t, PAIR * HEAD), F32), pltpu.VMEM((lp, PAIR * HEAD), F32)]
    whole = pl.BlockSpec((lp, PAIR * HEAD), lambda g, j: (0, g))
    wide_all = pl.BlockSpec((lp, PAIR * WIDE), lambda g, j: (0, g))
    wide_tile = pl.BlockSpec((t, PAIR * WIDE), lambda g, j: (j, g))
    vec = pl.BlockSpec((1, PAIR * HEAD), lambda g, j: (0, g))
    outs, side_results = _pallas(
        body,
        name=name,
        args=(qb, doa, ka, va, ukv, g_k, ub, g_q),
        grid=(nh // PAIR, nk),
        in_specs=[wide_all, wide_all, wide_tile, wide_tile,
                  pl.BlockSpec((None, t, PAIR * HEAD), lambda g, j: (0, j, g)), vec,
                  pl.BlockSpec((None, lp, PAIR * HEAD), lambda g, j: (0, 0, g)), vec],
        out_specs=[whole, vec, pl.BlockSpec((2, t, PAIR * HEAD), lambda g, j: (0, j, g)), vec,
                   pl.BlockSpec((PAIR, 1, t), lambda g, j: (g, 0, j))],
        out_shape=[jax.ShapeDtypeStruct((lp, d), MXU), jax.ShapeDtypeStruct((1, d), F32),
                   jax.ShapeDtypeStruct((2, lp, d), MXU), jax.ShapeDtypeStruct((1, d), F32),
                   jax.ShapeDtypeStruct((nh, 1, lp), F32)],
        scratch_shapes=scratch,
        semantics=("parallel", "arbitrary"),
        vmem_mib=48,
        sides=sides,
    )
    return (*outs, side_results) if sides else tuple(outs)


def _fox_do(d_og, o, o_lo, ub, name="fox_do", tm=384):
    lp, d = o.shape
    tm = _tile(lp, tm)
    nh = d // HEAD

    def body(dog_ref, o_ref, olo_ref, z_ref, doa_ref):
        for h in range(nh):
            hs = slice(h * HEAD, (h + 1) * HEAD)
            z = z_ref[:, hs]
            do = (dog_ref[:, hs] * (z * _sigmoid(z))).astype(doa_ref.dtype)
            delta = jnp.sum(do.astype(F32) * (o_ref[:, hs] + olo_ref[:, hs]), axis=-1, keepdims=True)
            doa_ref[:, h * WIDE:h * WIDE + HEAD] = do
            doa_ref[:, h * WIDE + HEAD:(h + 1) * WIDE] = _extra_cols(tm, _split3(-delta), (0.0, 0.0, 0.0)).astype(
                doa_ref.dtype)

    row = pl.BlockSpec((tm, d), lambda i: (i, 0))
    return pl.pallas_call(
        body,
        name=name,
        grid=(lp // tm,),
        in_specs=[row, row, row, pl.BlockSpec((None, tm, d), lambda i: (1, i, 0))],
        out_specs=pl.BlockSpec((tm, nh * WIDE), lambda i: (i, 0)),
        out_shape=jax.ShapeDtypeStruct((lp, nh * WIDE), MXU),
        compiler_params=_cparams(("parallel",), 56),
    )(d_og, o, o_lo, ub)


def _fox_gate_bwd(ub, d_og, o, dq, name="fox_gate_bwd", tm=384):
    _, lp, d = ub.shape
    tm = _tile(lp, tm)

    def body(z_ref, dog_ref, o_ref, dq_ref, dub_ref):
        z = z_ref[...]
        sz = _sigmoid(z)
        dub_ref[0] = dq_ref[...]
        dub_ref[1] = (dog_ref[...] * o_ref[...] * (sz * (1.0 + z * (1.0 - sz)))).astype(dub_ref.dtype)

    row = pl.BlockSpec((tm, d), lambda i: (i, 0))
    return pl.pallas_call(
        body,
        name=name,
        grid=(lp // tm,),
        in_specs=[pl.BlockSpec((None, tm, d), lambda i: (1, i, 0)), row, row, row],
        out_specs=pl.BlockSpec((2, tm, d), lambda i: (0, i, 0)),
        out_shape=jax.ShapeDtypeStruct((2, lp, d), MXU),
        compiler_params=_cparams(("parallel",), 48),
    )(ub, d_og, o, dq)


def _loss(h, target, name="loss_head"):
    lp, d = h.shape
    nb = lp // 128

    def body(h_ref, t_ref, loss_ref, dh_ref, acc_ref):
        i = pl.program_id(0)

        @pl.when(i == 0)
        def _():
            acc_ref[...] = jnp.zeros_like(acc_ref)
            dh_ref[...] = jnp.zeros_like(dh_ref)

        @pl.when(i > 0)
        def _():
            err = h_ref[...] - t_ref[...]
            dh_ref[...] = err * (1.0 / d)
            acc_ref[...] += jnp.sum(jnp.sum(err * err, axis=-1, keepdims=True) * (1.0 / d), axis=0, keepdims=True)

        @pl.when(i == nb - 1)
        def _():
            loss_ref[...] = 0.5 * acc_ref[...]

    return pl.pallas_call(
        body,
        name=name,
        grid=(nb,),
        in_specs=[pl.BlockSpec((128, d), lambda i: (i, 0)),
                  pl.BlockSpec((128, d), lambda i: (jnp.maximum(i - 1, 0), 0))],
        out_specs=[pl.BlockSpec((1, 1), lambda i: (0, 0)), pl.BlockSpec((128, d), lambda i: (i, 0))],
        out_shape=[jax.ShapeDtypeStruct((1, 1), F32), jax.ShapeDtypeStruct((lp, d), F32)],
        scratch_shapes=[pltpu.VMEM((1, 1), F32)],
        compiler_params=_cparams(("arbitrary",), 32),
    )(h, target)


def _adam_math(w, g, m, v):
    m = ADAM_B1 * m + (1.0 - ADAM_B1) * g
    v = ADAM_B2 * v + (1.0 - ADAM_B2) * (g * g)
    m_hat = m / (1.0 - ADAM_B1 ** ADAM_STEP)
    v_hat = v / (1.0 - ADAM_B2 ** ADAM_STEP)
    delta = -ADAM_LR * (m_hat / (jnp.sqrt(v_hat) + ADAM_EPS) + ADAM_WD * w)
    return delta, m, v


def _adamw(parts, w, m, v, name, tm=512):
    n, r, c = parts.shape
    tm = tm if r % tm == 0 else r

    def body(p_ref, w_ref, m_ref, v_ref, g_ref, d_ref, nm_ref, nv_ref):
        g = p_ref[0].astype(F32)
        for j in range(1, n):
            g = g + p_ref[j].astype(F32)
        g_ref[...] = g
        d_ref[...], nm_ref[...], nv_ref[...] = _adam_math(w_ref[...], g, m_ref[...], v_ref[...])

    row = pl.BlockSpec((tm, c), lambda i: (i, 0))
    return pl.pallas_call(
        body,
        name=name,
        grid=(r // tm,),
        in_specs=[pl.BlockSpec((n, tm, c), lambda i: (0, i, 0)), row, row, row],
        out_specs=[row] * 4,
        out_shape=[jax.ShapeDtypeStruct((r, c), F32)] * 4,
        compiler_params=_cparams(("parallel",), 48),
    )(parts, w, m, v)


def _place():
    x, y, c = lax.axis_index("x"), lax.axis_index("y"), lax.axis_index("c")
    return x, y, c


def _other_chips(x, y):
    return [(1 - x, y), (x, 1 - y), (1 - x, 1 - y)]


def _any_spec():
    return pl.BlockSpec(memory_space=pl.ANY)


class _Side:
    def __init__(self, ins, outs, sems, start, finish, aliases=None):
        self.ins, self.outs, self.sems = list(ins), list(outs), sems
        self.start, self.finish = start, finish
        self.aliases = dict(aliases or {})


def _pallas(body, *, name, out_shape, args=(), grid=(), in_specs=(), out_specs=(), scratch_shapes=(),
            semantics=(), vmem_mib=None, sides=(), aliases=None):
    n_in, n_out, n_scr = len(args), len(out_shape), len(scratch_shapes)
    side_ins = [a for s in sides for a in s.ins]
    side_outs = [o for s in sides for o in s.outs]
    side_sems = [pltpu.SemaphoreType.DMA((max(k, 1),)) for s in sides for k in s.sems]
    aliases, in_at, out_at = dict(aliases or {}), n_in, n_out
    for s in sides:
        aliases.update({in_at + i: out_at + o for i, o in s.aliases.items()})
        in_at, out_at = in_at + len(s.ins), out_at + len(s.outs)

    def wrapped(*refs):
        at = [0]

        def take(k):
            got = refs[at[0]:at[0] + k]
            at[0] += k
            return got

        main_in = take(n_in)
        s_in = [take(len(s.ins)) for s in sides]
        main_out = take(n_out)
        s_out = [take(len(s.outs)) for s in sides]
        main_scr = take(n_scr)
        s_sem = [take(3) for s in sides]

        def run(stage):
            for s, i_, o_, m_ in zip(sides, s_in, s_out, s_sem):
                getattr(s, stage)(i_, o_, *m_)

        first = last = None
        for k, g in enumerate(grid):
            i = pl.program_id(k)
            first = (i == 0) if first is None else jnp.logical_and(first, i == 0)
            last = (i == g - 1) if last is None else jnp.logical_and(last, i == g - 1)
        if sides:
            run("start") if first is None else pl.when(first)(lambda: run("start"))
        body(*main_in, *main_out, *main_scr)
        if sides:
            run("finish") if last is None else pl.when(last)(lambda: run("finish"))

    kw = {}
    if grid:
        kw["grid"] = grid
    if semantics or vmem_mib:
        sem = tuple("arbitrary" for _ in grid) if sides else tuple(semantics)
        kw["compiler_params"] = pltpu.CompilerParams(
            dimension_semantics=sem or None, vmem_limit_bytes=vmem_mib * MIB if vmem_mib else None)
    res = pl.pallas_call(
        wrapped,
        name=name,
        in_specs=list(in_specs) + [_any_spec()] * len(side_ins),
        out_specs=list(out_specs) + [_any_spec()] * len(side_outs),
        out_shape=list(out_shape) + side_outs,
        scratch_shapes=list(scratch_shapes) + side_sems,
        input_output_aliases=aliases,
        **kw,
    )(*args, *side_ins)
    main, rest, per_side = list(res[:n_out]), list(res[n_out:]), []
    for s in sides:
        per_side.append(rest[:len(s.outs)])
        rest = rest[len(s.outs):]
    return main, per_side


def _slot(p):
    return 4 * p[0] + 2 * p[1] + p[2]


def _sibling_side(grads):
    n = len(grads)

    def copies(ins, outs, send_sems, recv_sems):
        x, y, c = _place()
        return [pltpu.make_async_remote_copy(
            src_ref=ins[t].at[2 * chip + (1 - c)], dst_ref=outs[t].at[chip],
            send_sem=send_sems.at[4 * t + chip], recv_sem=recv_sems.at[4 * t + chip],
            device_id=(x, y, 1 - c), device_id_type=MESH) for t in range(n) for chip in range(N_CHIP)]

    def start(ins, outs, send_sems, recv_sems, local_sems):
        for cp in copies(ins, outs, send_sems, recv_sems):
            cp.start()

    def finish(ins, outs, send_sems, recv_sems, local_sems):
        for cp in copies(ins, outs, send_sems, recv_sems):
            cp.wait()

    outs = [jax.ShapeDtypeStruct((N_CHIP,) + g.shape[1:], g.dtype) for g in grads]
    return _Side(grads, outs, (4 * n, 4 * n, 0), start, finish)


def _chips_side(partials):
    n = len(partials)

    def copies(ins, outs, send_sems, recv_sems, local_sems):
        x, y, c = _place()
        my_chip = 2 * x + y
        local = [pltpu.make_async_copy(ins[t].at[my_chip], outs[t].at[my_chip], local_sems.at[t]) for t in range(n)]
        sends, recvs = [], []
        for t in range(n):
            for j, chip in enumerate(_other_chips(x, y)):
                their = 2 * chip[0] + chip[1]
                sems = dict(send_sem=send_sems.at[3 * t + j], recv_sem=recv_sems.at[3 * t + j],
                            device_id=(*chip, c), device_id_type=MESH)
                sends.append(pltpu.make_async_remote_copy(src_ref=ins[t].at[their], dst_ref=outs[t].at[my_chip], **sems))
                recvs.append(pltpu.make_async_remote_copy(src_ref=ins[t].at[my_chip], dst_ref=outs[t].at[their], **sems))
        return local, sends, recvs

    def start(ins, outs, send_sems, recv_sems, local_sems):
        local, sends, _ = copies(ins, outs, send_sems, recv_sems, local_sems)
        for cp in local + sends:
            cp.start()

    def finish(ins, outs, send_sems, recv_sems, local_sems):
        local, sends, recvs = copies(ins, outs, send_sems, recv_sems, local_sems)
        for cp in sends:
            cp.wait_send()
        for cp in recvs:
            cp.wait_recv()
        for cp in local:
            cp.wait()

    outs = [jax.ShapeDtypeStruct(p.shape, p.dtype) for p in partials]
    return _Side(partials, outs, (3 * n, 3 * n, n), start, finish)


def _gather_own_side(blocks):
    n = len(blocks)

    def copies(ins, outs, send_sems, recv_sems, local_sems):
        x, y, c = _place()
        me = (x, y, c)
        peers = [(x, y, 1 - c)] + [(*chip, c) for chip in _other_chips(x, y)]
        local = [pltpu.make_async_copy(ins[t], outs[t].at[_slot(me)], local_sems.at[t]) for t in range(n)]
        sends, recvs = [], []
        for t in range(n):
            for k, peer in enumerate(peers):
                sems = dict(send_sem=send_sems.at[4 * t + k], recv_sem=recv_sems.at[4 * t + k],
                            device_id=peer, device_id_type=MESH)
                sends.append(pltpu.make_async_remote_copy(src_ref=ins[t], dst_ref=outs[t].at[_slot(me)], **sems))
                recvs.append(pltpu.make_async_remote_copy(src_ref=ins[t], dst_ref=outs[t].at[_slot(peer)], **sems))
        return local, sends, recvs

    def start(ins, outs, send_sems, recv_sems, local_sems):
        local, sends, _ = copies(ins, outs, send_sems, recv_sems, local_sems)
        for cp in local + sends:
            cp.start()

    def finish(ins, outs, send_sems, recv_sems, local_sems):
        local, sends, recvs = copies(ins, outs, send_sems, recv_sems, local_sems)
        for cp in sends:
            cp.wait_send()
        for cp in recvs:
            cp.wait_recv()
        for cp in local:
            cp.wait()

    outs = [jax.ShapeDtypeStruct((N_DEV,) + b.shape, b.dtype) for b in blocks]
    return _Side(blocks, outs, (4 * n, 4 * n, n), start, finish)


def _gather_pass_side(gathered):
    n = len(gathered)

    def copies(outs, send_sems, recv_sems):
        x, y, c = _place()
        sends, recvs = [], []
        for t in range(n):
            for j, chip in enumerate(_other_chips(x, y)):
                sems = dict(send_sem=send_sems.at[3 * t + j], recv_sem=recv_sems.at[3 * t + j],
                            device_id=(x, y, 1 - c), device_id_type=MESH)
                mine, theirs = outs[t].at[_slot((*chip, c))], outs[t].at[_slot((*chip, 1 - c))]
                sends.append(pltpu.make_async_remote_copy(src_ref=mine, dst_ref=mine, **sems))
                recvs.append(pltpu.make_async_remote_copy(src_ref=mine, dst_ref=theirs, **sems))
        return sends, recvs

    def start(ins, outs, send_sems, recv_sems, local_sems):
        for cp in copies(outs, send_sems, recv_sems)[0]:
            cp.start()

    def finish(ins, outs, send_sems, recv_sems, local_sems):
        sends, recvs = copies(outs, send_sems, recv_sems)
        for cp in sends:
            cp.wait_send()
        for cp in recvs:
            cp.wait_recv()

    outs = [jax.ShapeDtypeStruct(g.shape, g.dtype) for g in gathered]
    return _Side(gathered, outs, (3 * n, 3 * n, 0), start, finish, aliases={t: t for t in range(n)})


def _alone(side, name):
    return _pallas(lambda: None, name=name, out_shape=[], sides=[side])[1][0]


def _all_gather(blocks, name):
    n = len(blocks)

    def body(*refs):
        in_refs, out_refs = refs[:n], refs[n:2 * n]
        send_sems, recv_sems, local_sems = refs[2 * n:]
        x, y, c = _place()
        me, sibling = (x, y, c), (x, y, 1 - c)
        chips = _other_chips(x, y)

        def slot(p):
            return 4 * p[0] + 2 * p[1] + p[2]

        def copy(t, k, block, to, src=None):
            dst = out_refs[t].at[slot(block)]
            return pltpu.make_async_remote_copy(
                src_ref=dst if src is None else src, dst_ref=dst,
                send_sem=send_sems.at[7 * t + k], recv_sem=recv_sems.at[7 * t + k],
                device_id=to, device_id_type=MESH)

        started = []
        for t in range(n):
            mine = pltpu.make_async_copy(in_refs[t], out_refs[t].at[slot(me)], local_sems.at[t])
            mine.start()
            started.append(mine)
        sends = []
        for t in range(n):
            first = [copy(t, 0, me, sibling, src=in_refs[t])]
            first += [copy(t, 1 + j, me, (*chip, c), src=in_refs[t]) for j, chip in enumerate(chips)]
            for cp in first:
                cp.start()
            sends += first
        for t in range(n):
            for j, chip in enumerate(chips):
                copy(t, 1 + j, (*chip, c), me).wait_recv()
                passed = copy(t, 4 + j, (*chip, c), sibling)
                passed.start()
                sends.append(passed)
        for t in range(n):
            copy(t, 0, sibling, me).wait_recv()
            for j, chip in enumerate(chips):
                copy(t, 4 + j, (*chip, 1 - c), me).wait_recv()
        for cp in sends:
            cp.wait_send()
        for mine in started:
            mine.wait()

    return pl.pallas_call(
        body,
        name=name,
        in_specs=[_any_spec()] * n,
        out_specs=[_any_spec()] * n,
        out_shape=[jax.ShapeDtypeStruct((N_DEV,) + b.shape, b.dtype) for b in blocks],
        scratch_shapes=[pltpu.SemaphoreType.DMA((7 * n,)), pltpu.SemaphoreType.DMA((7 * n,)),
                        pltpu.SemaphoreType.DMA((n,))],
    )(*blocks)


def _pair_sum(g8, got, name, tm=1024):
    _, r, c = g8.shape
    tm = tm if r % tm == 0 else r
    core = lax.axis_index("c")

    def body(core_ref, mine_ref, got_ref, o_ref):
        south_first = core_ref[0] == 0
        a, b = mine_ref[...], got_ref[...]
        o_ref[...] = (jnp.where(south_first, a, b) + jnp.where(south_first, b, a)).astype(o_ref.dtype)

    return pl.pallas_call(
        body,
        name=name,
        grid_spec=pltpu.PrefetchScalarGridSpec(
            num_scalar_prefetch=1,
            grid=(N_CHIP, r // tm),
            in_specs=[pl.BlockSpec((None, tm, c), lambda j, i, core_ref: (2 * j + core_ref[0], i, 0)),
                      pl.BlockSpec((None, tm, c), lambda j, i, core_ref: (j, i, 0))],
            out_specs=pl.BlockSpec((None, tm, c), lambda j, i, core_ref: (j, i, 0)),
        ),
        out_shape=jax.ShapeDtypeStruct((N_CHIP, r, c), PAYLOAD),
        compiler_params=_cparams(("parallel", "parallel"), 32),
    )(jnp.reshape(core, (1,)).astype(jnp.int32), g8, got)


def _all_reduce_small(s, name="small_all_reduce"):
    r, c = s.shape

    def body(s_ref, o_ref, buf_ref, send_sems, recv_sems):
        x, y, c_ = _place()
        me = 4 * x + 2 * y + c_
        buf_ref[me] = s_ref[...]

        def copy(k, slot, peer):
            return pltpu.make_async_remote_copy(
                src_ref=s_ref, dst_ref=buf_ref.at[slot],
                send_sem=send_sems.at[k - 1], recv_sem=recv_sems.at[k - 1],
                device_id=peer, device_id_type=MESH)

        peers = []
        for k in range(1, N_DEV):
            peer = (x ^ ((k >> 2) & 1), y ^ ((k >> 1) & 1), c_ ^ (k & 1))
            peers.append(peer)
            copy(k, me, peer).start()
        for k, peer in zip(range(1, N_DEV), peers):
            cp = copy(k, 4 * peer[0] + 2 * peer[1] + peer[2], peer)
            cp.wait_send()
            cp.wait_recv()
        total = buf_ref[0]
        for j in range(1, N_DEV):
            total = total + buf_ref[j]
        o_ref[...] = total

    return pl.pallas_call(
        body,
        name=name,
        out_shape=jax.ShapeDtypeStruct((r, c), F32),
        in_specs=[pl.BlockSpec(memory_space=pltpu.VMEM)],
        out_specs=pl.BlockSpec(memory_space=pltpu.VMEM),
        scratch_shapes=[pltpu.VMEM((N_DEV, r, c), F32), pltpu.SemaphoreType.DMA((N_DEV - 1,)),
                        pltpu.SemaphoreType.DMA((N_DEV - 1,))],
    )(s)


def _late_weights(g_kv, g_b_in, g_b_out):
    d = g_kv.shape[1]
    nh = d // HEAD
    wkv = jnp.reshape(jnp.transpose(g_kv, (1, 0, 2)), (d, -1))
    return wkv[:, :2 * d], jnp.pad(wkv[:, 2 * d:], ((0, 0), (0, 128 - nh))), g_b_in, jnp.reshape(g_b_out, (d, d))


def _kv_grad_blocks(dwkv2, dwfl):
    d = dwkv2.shape[0]
    dwkv = jnp.concatenate([dwkv2, dwfl[:, :d // HEAD]], axis=1)
    return jnp.transpose(jnp.reshape(dwkv, (d, N_DEV, -1)), (1, 0, 2))


def _local_step(x, target, meta, gamma, a_norm, wa_in, a_out_norm, wa_out, kv_norm, late, b_f, g_k, b_norm, g_q,
                dist):
    d = x.shape[1]
    nh = d // HEAD
    cols = d // N_DEV
    h0 = jnp.concatenate([jnp.zeros((ROW_PAD, d), F32), meta, x], axis=0)
    lp = h0.shape[0]
    big = dict(tm=1408, tn=512, tk=2048)

    lb = _lb_fwd(gamma)
    (hn_a,) = _rms_fwd(h0, [a_norm], "rms_a")
    if dist:
        a_out_blk, kv_blk, b_in_blk, b_out_blk = late
        u4, (first,) = _matmul(hn_a, wa_in, "nn", F32, "a_in", out_parts=4,
                               sides=[_gather_own_side([a_out_blk, b_in_blk, b_out_blk])], **big)
        o_a, og_a, ((g_a_out, g_b_in, g_b_out), (g_kv,)) = _hgrn2_fwd(
            u4, lb, a_out_norm, sides=[_gather_pass_side(first), _gather_own_side([kv_blk])])
        wa_out = jnp.reshape(g_a_out, (d, d))
        h1, ((g_kv,),) = _matmul(og_a, wa_out, "nn", F32, "a_out", add=h0, sides=[_gather_pass_side([g_kv])], **big)
        wkv2, wfl, wb_in, wb_out = _late_weights(g_kv, g_b_in, g_b_out)
    else:
        u4 = _matmul(hn_a, wa_in, "nn", F32, "a_in", out_parts=4, **big)
        o_a, og_a = _hgrn2_fwd(u4, lb, a_out_norm)
        h1 = _matmul(og_a, wa_out, "nn", F32, "a_out", add=h0, **big)
        wkv2, wfl, wb_in, wb_out = late
    hk, hb = _rms_fwd(h1, [kv_norm, b_norm], "rms_kv_b")
    ukv = _matmul(hk, wkv2, "nn", F32, "kv_in", out_parts=2, **big)
    ufl = _matmul(hk, wfl, "nn", F32, "kv_f", **big)
    ub = _matmul(hb, wb_in, "nn", F32, "b_in", out_parts=2, **big)
    f_cum = _fgate_fwd(ufl, b_f)
    qa, ka, va = _attn_operands(ub, ukv, g_q, g_k, f_cum)
    o_b, o_lo, og_b, qb = _fox_fwd(qa, ka, va, ub)
    h2 = _matmul(og_b, wb_out, "nn", F32, "b_out", add=h1, **big)
    loss, dh2 = _loss(h2, target)

    dx_t = dict(tm=1408, tn=512, tk=2048, vmem_mib=56)
    dx_parts_t = dict(tm=704, tn=512, tk=2048, vmem_mib=58, k_whole=True)
    dw_t = dict(tm=512, tn=1024, tk=lp, vmem_mib=58)
    dw_f32_t = dict(tm=1024, tn=512, tk=lp, vmem_mib=58)
    def to_sibling(g8):
        return [_sibling_side([g8])] if dist else []

    def to_chips(partial):
        return [_chips_side([partial])] if dist else []

    def unpack(res, n_sides):
        if not dist:
            return res, [None] * n_sides
        *main, side_results = res
        return (main[0] if len(main) == 1 else tuple(main)), [r[0] for r in side_results]

    dwb_out = _matmul(og_b, dh2, "tn", F32, "b_out_dw", **dw_f32_t)
    g8_b_out = jnp.reshape(dwb_out, (N_DEV, cols, d))
    d_ogb, (got,) = unpack(_matmul(dh2, wb_out, "nt", F32, "b_out_dx", sides=to_sibling(g8_b_out), **dx_t), 1)
    p_b_out = _pair_sum(g8_b_out, got, "pair_sum_b_w_out") if dist else None
    doa = _fox_do(d_ogb, o_b, o_lo, ub)
    (dq, dg_q, dukv, dg_k, dfk), (r_b_out,) = unpack(
        _fox_bwd(qb, ka, va, doa, ukv, g_k, ub, g_q, sides=to_chips(p_b_out)), 1)
    dub = _fox_gate_bwd(ub, d_ogb, o_b, dq)
    d_f = jnp.pad(jnp.transpose(dfk[:, 0, :]), ((0, 0), (0, 128 - nh)))
    dufl, db_f = _fgate_bwd(ufl, b_f, d_f)
    dwb_in = _matmul(hb, dub, "tn", F32, "b_in_dw", out_parts=N_DEV, **dw_t)
    d_hb, (got,) = unpack(_matmul(dub, wb_in, "nt", F32, "b_in_dx", sides=to_sibling(dwb_in), **dx_parts_t), 1)
    p_b_in = _pair_sum(dwb_in, got, "pair_sum_b_w_in") if dist else None
    d_hk, (r_b_in,) = unpack(_matmul(dukv, wkv2, "nt", F32, "kv_dx", sides=to_chips(p_b_in), **dx_parts_t), 1)
    d_hk = _matmul(dufl, wfl, "nt", F32, "kv_f_dx", add=d_hk, **dx_t)
    dwkv2 = _matmul(hk, dukv, "tn", F32, "kv_dw", **dw_t)
    dwfl = _matmul(hk, dufl, "tn", F32, "kv_f_dw", **dw_t)
    g8_kv = _kv_grad_blocks(dwkv2, dwfl) if dist else None
    dh1, (dg_kv, dg_b) = _rms_bwd(h1, [kv_norm, b_norm], [d_hk, d_hb], dh2, "rms_kv_b_bwd")
    d_oga, (got,) = unpack(_matmul(dh1, wa_out, "nt", F32, "a_out_dx", sides=to_sibling(g8_kv), **dx_t), 1)
    p_kv = _pair_sum(g8_kv, got, "pair_sum_kv_w") if dist else None
    dwa_out = _matmul(og_a, dh1, "tn", F32, "a_out_dw", **dw_f32_t)
    g8_a_out = jnp.reshape(dwa_out, (N_DEV, cols, d))
    (du4, dlb, dg_aout), (r_kv, got) = unpack(
        _hgrn2_bwd(u4, o_a, d_oga, lb, a_out_norm, sides=to_chips(p_kv) + to_sibling(g8_a_out)), 2)
    p_a_out = _pair_sum(g8_a_out, got, "pair_sum_a_w_out") if dist else None
    dwa_in, (r_a_out,) = unpack(
        _matmul(hn_a, du4, "tn", F32, "a_in_dw", out_parts=N_DEV, sides=to_chips(p_a_out), **dw_t), 1)
    row_tiles = lp // _tile(lp, dx_parts_t["tm"], 64)
    if dist and row_tiles >= 2:
        top = max(1, row_tiles // 3)
        d_hna, ((got,),) = _matmul(du4, wa_in, "nt", F32, "a_in_dx_top", rows=(0, top),
                                   sides=to_sibling(dwa_in), **dx_parts_t)
        p_a_in = _pair_sum(dwa_in, got, "pair_sum_a_w_in")
        d_hna, ((r_a_in,),) = _matmul(du4, wa_in, "nt", F32, "a_in_dx_bottom", rows=(top, row_tiles - top), into=d_hna,
                                      sides=to_chips(p_a_in), **dx_parts_t)
    else:
        d_hna, (got,) = unpack(_matmul(du4, wa_in, "nt", F32, "a_in_dx", sides=to_sibling(dwa_in), **dx_parts_t), 1)
        if dist:
            (r_a_in,) = _alone(_chips_side([_pair_sum(dwa_in, got, "pair_sum_a_w_in")]), "grads_to_chips_a_w_in")
    dh0, (dg_a,) = _rms_bwd(h0, [a_norm], [d_hna], dh1, "rms_a_bwd")
    dgamma = _lb_bwd(gamma, dlb)

    grads = dict(meta=dh0[ROW_PAD:ROW_PAD + N_META], gamma=dgamma, a_norm=dg_a, a_out_norm=dg_aout, kv_norm=dg_kv,
                 b_f=db_f, g_k=dg_k, b_norm=dg_b, g_q=dg_q)
    if dist:
        grads.update(wa_in=r_a_in, wa_out=r_a_out, wkv=r_kv, wb_in=r_b_in, wb_out=r_b_out)
    else:
        grads.update(wa_in=dwa_in, wa_out=dwa_out, wkv2=dwkv2, wfl=dwfl, wb_in=dwb_in, wb_out=dwb_out)
    return loss, dh0[ROW_PAD + N_META:], grads


def kernel(x, meta, gamma_lb, a_norm, a_w_in, a_out_norm, a_w_out, kv_norm, kv_w, fox_b_f, fox_k_norm, b_norm, b_w_in, b_q_norm, b_w_out, loss_target, m_meta, m_gamma_lb, m_a_norm, m_a_w_in, m_a_out_norm, m_a_w_out, m_kv_norm, m_kv_w, m_fox_b_f, m_fox_k_norm, m_b_norm, m_b_w_in, m_b_q_norm, m_b_w_out, v_meta, v_gamma_lb, v_a_norm, v_a_w_in, v_a_out_norm, v_a_w_out, v_kv_norm, v_kv_w, v_fox_b_f, v_fox_k_norm, v_b_norm, v_b_w_in, v_b_q_norm, v_b_w_out):
    d = x.shape[-1]
    nh = d // HEAD
    cols = d // N_DEV
    me = 4 * lax.axis_index("x") + 2 * lax.axis_index("y") + lax.axis_index("c")

    sharded_small = jnp.concatenate([meta, gamma_lb, a_norm, a_out_norm, jnp.zeros((4, cols), F32)], axis=0)
    g_a_in, g_small = _all_gather([a_w_in[0].astype(MXU), sharded_small], "gather_weights")
    wa_out = None
    late = (a_w_out[0].astype(MXU), kv_w.astype(MXU), b_w_in[0].astype(MXU), b_w_out[0].astype(MXU))
    small = jnp.reshape(jnp.transpose(g_small, (1, 0, 2)), (-1, d))
    meta_f, gamma_f, a_norm_f, a_out_norm_f = small[:16], small[16:18], small[18:19], small[19:20]
    b_f = jnp.pad(jnp.reshape(fox_b_f, (1, nh)), ((0, 0), (0, 128 - nh)))
    g_k = jnp.reshape(fox_k_norm, (1, d))
    g_q = jnp.reshape(b_q_norm, (1, d))
    kv_norm_r = jnp.reshape(kv_norm, (1, d))

    loss, grad_x, g = _local_step(x[0], loss_target[0], meta_f, gamma_f, a_norm_f, g_a_in, a_out_norm_f, wa_out,
                                  kv_norm_r, late, b_f, g_k, b_norm, g_q, dist=True)
    loss = lax.psum(loss[0, 0], AXES)

    r_a_in = _adamw(g["wa_in"], a_w_in[0], m_a_w_in[0], v_a_w_in[0], "adamw_a_w_in")
    r_a_out = _adamw(g["wa_out"], a_w_out[0], m_a_w_out[0], v_a_w_out[0], "adamw_a_w_out")
    r_kv = _adamw(g["wkv"], kv_w, m_kv_w, v_kv_w, "adamw_kv_w")
    r_b_in = _adamw(g["wb_in"], b_w_in[0], m_b_w_in[0], v_b_w_in[0], "adamw_b_w_in")
    r_b_out = _adamw(g["wb_out"], b_w_out[0], m_b_w_out[0], v_b_w_out[0], "adamw_b_w_out")

    packed = jnp.concatenate(
        [g["meta"], g["gamma"], g["a_norm"], g["a_out_norm"], g["kv_norm"], g["b_norm"], g["g_k"], g["g_q"],
         jnp.pad(g["b_f"], ((0, 0), (0, d - 128))), jnp.zeros((7, d), F32)], axis=0)
    tot = _all_reduce_small(packed)
    mine = lax.dynamic_slice_in_dim(tot[:20], me * cols, cols, axis=1)
    gs = [mine[:16], mine[16:18], mine[18:19], mine[19:20], tot[20], tot[21:22], jnp.reshape(tot[22], (nh, HEAD)),
          jnp.reshape(tot[23], (1, nh, HEAD)), tot[24, :nh]]
    small_w = [meta, gamma_lb, a_norm, a_out_norm, kv_norm, b_norm, fox_k_norm, b_q_norm, fox_b_f]
    small_m = [m_meta, m_gamma_lb, m_a_norm, m_a_out_norm, m_kv_norm, m_b_norm, m_fox_k_norm, m_b_q_norm, m_fox_b_f]
    small_v = [v_meta, v_gamma_lb, v_a_norm, v_a_out_norm, v_kv_norm, v_b_norm, v_fox_k_norm, v_b_q_norm, v_fox_b_f]

    def pack(arrs):
        rows = [jnp.reshape(a, (-1, cols)) for a in arrs[:8]]
        rows.append(jnp.pad(jnp.reshape(arrs[8], (1, nh)), ((0, 0), (0, cols - nh))))
        n_rows = sum(r.shape[0] for r in rows)
        return jnp.concatenate(rows + [jnp.zeros((-n_rows % 8, cols), F32)], axis=0)

    _, sd, sm, sv = _adamw(pack(gs)[None], pack(small_w), pack(small_m), pack(small_v), "adamw_small")

    def unpack(p):
        out, at = [], 0
        for a in small_w[:8]:
            n = a.size // cols
            out.append(jnp.reshape(p[at:at + n], a.shape))
            at += n
        out.append(p[at, :nh])
        return out

    d_s, m_s, v_s = unpack(sd), unpack(sm), unpack(sv)

    def ordered(s, a_in, a_out, kv, b_in, b_out):
        return [s[0], s[1], s[2], a_in[None], s[3], a_out[None], s[4], kv, s[8], s[6], s[5], b_in[None], s[7], b_out[None]]

    outs = []
    for i, s in enumerate([gs, d_s, m_s, v_s]):
        outs += ordered(s, r_a_in[i], r_a_out[i], r_kv[i], r_b_in[i], r_b_out[i])
    return (loss, grad_x[None], *outs)
```

```python
import math

import jax
import jax.numpy as jnp
from jax import lax
from jax.experimental import pallas as pl
from jax.experimental.pallas import tpu as pltpu

HEAD = 128
CHUNK = 64
N_META = 16
ROW_PAD = 128 - N_META
EPS = 1e-6
MASK_VALUE = -1e30
ADAM_LR = 0.001
ADAM_B1 = 0.9
ADAM_B2 = 0.999
ADAM_EPS = 1e-08
ADAM_WD = 0.01
ADAM_STEP = 10
N_DEV = 8
N_CHIP = 4
MIB = 1024 * 1024
AXES = ("x", "y", "c")
MESH = pl.DeviceIdType.MESH

F32 = jnp.float32
MXU = jnp.bfloat16
PAYLOAD = jnp.bfloat16
HI = lax.Precision.HIGHEST

NN = (((1,), (0,)), ((), ()))
NT = (((1,), (1,)), ((), ()))
TN = (((0,), (0,)), ((), ()))


def _dot(a, b, dims=NN):
    return lax.dot_general(a.astype(MXU), b.astype(MXU), dims, preferred_element_type=F32)


def _dot_exact(a, b):
    return lax.dot_general(a, b, NN, precision=HI, preferred_element_type=F32)


def _sigmoid(x):
    return 1.0 / (1.0 + jnp.exp(-x))


def _tile(dim, target, unit=128):
    best = None
    t = unit
    while t <= min(dim, target):
        if dim % t == 0:
            best = t
        t += unit
    return best if best is not None else dim


def _cparams(semantics, vmem_mib):
    return pltpu.CompilerParams(dimension_semantics=semantics, vmem_limit_bytes=vmem_mib * MIB)


def _mat_spec(arr, br, bc, rc_of_grid):
    if arr.ndim == 2:
        return pl.BlockSpec((br, bc), rc_of_grid)
    assert arr.shape[2] % bc == 0, (arr.shape, bc)
    per = arr.shape[2] // bc

    def idx(*g):
        r, c = rc_of_grid(*g)
        return (c // per, r, c % per)

    return pl.BlockSpec((None, br, bc), idx)


def _mat_shape(arr):
    return (arr.shape[0], arr.shape[1]) if arr.ndim == 2 else (arr.shape[1], arr.shape[0] * arr.shape[2])


def _matmul(a, b, dims, out_dtype, name, *, add=None, out_parts=1, tm=512, tn=512, tk=512, vmem_mib=48, sides=(),
            k_whole=False, rows=None, into=None):
    ar, ac = _mat_shape(a)
    br_, bc_ = _mat_shape(b)
    if dims == "nn":
        m, k, n = ar, ac, bc_
        assert br_ == k
    elif dims == "nt":
        m, k, n = ar, ac, br_
        assert bc_ == k
    else:
        m, k, n = ac, ar, bc_
        assert br_ == k
    m_unit, n_unit, k_unit = m, n, k
    if a.ndim == 3:
        if dims == "tn":
            m_unit = math.gcd(m_unit, a.shape[2])
        else:
            k_unit = math.gcd(k_unit, a.shape[2])
    if b.ndim == 3:
        if dims == "nt":
            k_unit = math.gcd(k_unit, b.shape[2])
        else:
            n_unit = math.gcd(n_unit, b.shape[2])
    if out_parts > 1:
        n_unit = math.gcd(n_unit, n // out_parts)
    tm = _tile(m_unit, tm, 128 if dims == "tn" else 64)
    tn, tk = _tile(n_unit, tn), _tile(k_unit, tk)
    whole_k = dims == "nt" and tk < k and k_whole
    if whole_k:
        k_chunk, tk = tk, k
    gm, gn, gk = m // tm, n // tn, k // tk
    assert gm * tm == m and gn * tn == n and gk * tk == k, (name, m, n, k, tm, tn, tk)

    def chunk_of(ref, arr, c):
        if arr.ndim == 2:
            return ref[:, c * k_chunk:(c + 1) * k_chunk]
        per = arr.shape[2] // k_chunk
        return ref[c // per, :, (c % per) * k_chunk:(c % per + 1) * k_chunk]

    def all_cols(arr, rows, row_of_grid):
        if arr.ndim == 2:
            return pl.BlockSpec((rows, arr.shape[1]), lambda i, j, kk: (row_of_grid(i, j), 0))
        return pl.BlockSpec((arr.shape[0], rows, arr.shape[2]), lambda i, j, kk: (0, row_of_grid(i, j), 0))

    if dims == "nn":
        a_spec = _mat_spec(a, tm, tk, lambda i, j, kk: (i, kk))
        b_spec = _mat_spec(b, tk, tn, lambda i, j, kk: (kk, j))
        dn = NN
    elif whole_k:
        a_spec = all_cols(a, tm, lambda i, j: i)
        b_spec = all_cols(b, tn, lambda i, j: j)
        dn = NT
    elif dims == "nt":
        a_spec = _mat_spec(a, tm, tk, lambda i, j, kk: (i, kk))
        b_spec = _mat_spec(b, tn, tk, lambda i, j, kk: (j, kk))
        dn = NT
    else:
        a_spec = _mat_spec(a, tk, tm, lambda i, j, kk: (kk, i))
        b_spec = _mat_spec(b, tk, tn, lambda i, j, kk: (kk, j))
        dn = TN

    if out_parts > 1:
        per = (n // out_parts) // tn
        out_shape = jax.ShapeDtypeStruct((out_parts, m, n // out_parts), out_dtype)
        o_spec = pl.BlockSpec((None, tm, tn), lambda i, j, kk: (j // per, i, j % per))
    else:
        out_shape = jax.ShapeDtypeStruct((m, n), out_dtype)
        o_spec = pl.BlockSpec((tm, tn), lambda i, j, kk: (i, j))

    in_specs = [a_spec, b_spec]
    args = [a, b]
    if add is not None:
        in_specs.append(pl.BlockSpec((tm, tn), lambda i, j, kk: (i, j)))
        args.append(add)
    if rows is not None:
        first_tile, gm = rows

        def shifted(spec):
            return pl.BlockSpec(spec.block_shape, lambda i, j, kk: spec.index_map(i + first_tile, j, kk))

        in_specs = [shifted(s) for s in in_specs]
        o_spec = shifted(o_spec)
    aliases = {}
    if into is not None:
        aliases[len(args)] = 0
        in_specs.append(_any_spec())
        args.append(into)

    def body(*refs):
        a_ref, b_ref, *others, o_ref, acc_ref = refs
        add_ref = others[0] if add is not None else None
        kk = pl.program_id(2)
        if whole_k:
            part = sum(lax.dot_general(chunk_of(a_ref, a, c).astype(MXU), chunk_of(b_ref, b, c).astype(MXU), dn,
                                       preferred_element_type=F32) for c in range(k // k_chunk))
        else:
            part = lax.dot_general(a_ref[...].astype(MXU), b_ref[...].astype(MXU), dn, preferred_element_type=F32)

        def finish(total):
            if add_ref is not None:
                total = total + add_ref[...]
            o_ref[...] = total.astype(o_ref.dtype)

        if gk == 1:
            finish(part)
        else:
            @pl.when(kk == 0)
            def _():
                acc_ref[...] = part

            @pl.when(jnp.logical_and(kk > 0, kk < gk - 1))
            def _():
                acc_ref[...] += part

            @pl.when(kk == gk - 1)
            def _():
                finish(acc_ref[...] + part)

    (out,), side_results = _pallas(
        body,
        name=name,
        args=args,
        grid=(gm, gn, gk),
        in_specs=in_specs,
        out_specs=[o_spec],
        out_shape=[out_shape],
        scratch_shapes=[pltpu.VMEM((tm, tn) if gk > 1 else (8, 128), F32)],
        semantics=("parallel", "parallel", "arbitrary"),
        vmem_mib=vmem_mib,
        sides=sides,
        aliases=aliases,
    )
    return (out, side_results) if sides else out


def _rms_fwd(h, gains, name, tm=384):
    lp, d = h.shape
    tm = _tile(lp, tm)
    n = len(gains)

    def body(*refs):
        h_ref = refs[0]
        g_refs = refs[1:1 + n]
        o_refs = refs[1 + n:]
        x = h_ref[...]
        y = x * lax.rsqrt(jnp.mean(x * x, axis=-1, keepdims=True) + EPS)
        for g_ref, o_ref in zip(g_refs, o_refs):
            o_ref[...] = (y * g_ref[...]).astype(o_ref.dtype)

    row = pl.BlockSpec((tm, d), lambda i: (i, 0))
    vec = pl.BlockSpec((1, d), lambda i: (0, 0))
    return pl.pallas_call(
        body,
        name=name,
        grid=(lp // tm,),
        in_specs=[row] + [vec] * n,
        out_specs=[row] * n,
        out_shape=[jax.ShapeDtypeStruct((lp, d), MXU)] * n,
        compiler_params=_cparams(("parallel",), 40),
    )(h, *gains)


def _rms_bwd(h, gains, dys, res, name, tm=384):
    lp, d = h.shape
    tm = _tile(lp, tm)
    n = len(gains)

    def body(*refs):
        h_ref, res_ref = refs[0], refs[1]
        g_refs = refs[2:2 + n]
        dy_refs = refs[2 + n:2 + 2 * n]
        dh_ref = refs[2 + 2 * n]
        dg_refs = refs[3 + 2 * n:]
        i = pl.program_id(0)
        x = h_ref[...]
        rstd = lax.rsqrt(jnp.mean(x * x, axis=-1, keepdims=True) + EPS)
        xhat = x * rstd
        dh = res_ref[...]
        for g_ref, dy_ref, dg_ref in zip(g_refs, dy_refs, dg_refs):
            dy = dy_ref[...]
            gdy = dy * g_ref[...]
            dh = dh + rstd * (gdy - xhat * jnp.mean(gdy * xhat, axis=-1, keepdims=True))
            part = jnp.sum(dy * xhat, axis=0, keepdims=True)

            @pl.when(i == 0)
            def _():
                dg_ref[...] = part

            @pl.when(i > 0)
            def _():
                dg_ref[...] += part

        dh_ref[...] = dh

    row = pl.BlockSpec((tm, d), lambda i: (i, 0))
    vec = pl.BlockSpec((1, d), lambda i: (0, 0))
    outs = pl.pallas_call(
        body,
        name=name,
        grid=(lp // tm,),
        in_specs=[row, row] + [vec] * n + [row] * n,
        out_specs=[row] + [vec] * n,
        out_shape=[jax.ShapeDtypeStruct((lp, d), F32)] + [jax.ShapeDtypeStruct((1, d), F32)] * n,
        compiler_params=_cparams(("arbitrary",), 56),
    )(h, res, *gains, *dys)
    return outs[0], list(outs[1:])


def _lb_fwd(gamma):
    def body(g_ref, lb_ref):
        g = g_ref[...]
        e = jnp.exp(g - jnp.max(g, axis=0, keepdims=True))
        lb_ref[...] = (e / jnp.sum(e, axis=0, keepdims=True))[0:1, :]

    return pl.pallas_call(body, name="lb_fwd", out_shape=jax.ShapeDtypeStruct((1, gamma.shape[1]), F32))(gamma)


def _lb_bwd(gamma, dlb):
    def body(g_ref, dlb_ref, dg_ref):
        g = g_ref[...]
        e = jnp.exp(g - jnp.max(g, axis=0, keepdims=True))
        s = e / jnp.sum(e, axis=0, keepdims=True)
        rows = lax.broadcasted_iota(jnp.int32, g.shape, 0)
        ds = jnp.where(rows == 0, dlb_ref[...], 0.0)
        dg_ref[...] = s * (ds - jnp.sum(s * ds, axis=0, keepdims=True))

    return pl.pallas_call(body, name="lb_bwd", out_shape=jax.ShapeDtypeStruct(gamma.shape, F32))(gamma, dlb)


def _tri(n, lower):
    r = lax.broadcasted_iota(jnp.int32, (n, n), 0)
    c = lax.broadcasted_iota(jnp.int32, (n, n), 1)
    return jnp.where((r >= c) if lower else (r <= c), 1.0, 0.0).astype(F32)


def _group(nc, most):
    return max(u for u in range(1, most + 1) if nc % u == 0)


def _running_sum(tri, x):
    hi = x.astype(MXU)
    rest = x - hi.astype(F32)
    mid = rest.astype(MXU)
    lo = (rest - mid.astype(F32)).astype(MXU)
    return _dot(tri, hi) + _dot(tri, mid) + _dot(tri, lo)


def _causal(n):
    r = lax.broadcasted_iota(jnp.int32, (n, n), 0)
    c = lax.broadcasted_iota(jnp.int32, (n, n), 1)
    return r >= c


def _chunk_gates(u_ref, lb, c):
    sl = pl.ds(pl.multiple_of(c * CHUNK, CHUNK), CHUNK)
    valid = (c * CHUNK + lax.broadcasted_iota(jnp.int32, (CHUNK, HEAD), 0)) >= ROW_PAD
    uq = u_ref[0, sl, :]
    uf = u_ref[1, sl, :]
    sq = _sigmoid(uq)
    sf = _sigmoid(uf)
    fg = lb + (1.0 - lb) * sf
    return dict(sl=sl, valid=valid, uq=uq, sq=sq, sf=sf, fg=fg, q=jnp.where(valid, uq * sq, 0.0),
                logf=jnp.where(valid, jnp.log(fg), 0.0), k=jnp.where(valid, 1.0 - fg, 0.0),
                v=jnp.where(valid, u_ref[2, sl, :], 0.0))


def _chunk_decays(x, b):
    b_last = b[CHUNK - 1:CHUNK, :]
    b_mid = b[CHUNK // 2 - 1:CHUNK // 2, :]
    e_qi = jnp.exp(b - b_mid)
    e_ki = jnp.exp(b_mid - b)
    e_kd = jnp.exp(b_last - b)
    e_qe = jnp.exp(b)
    q, k = x["q"], x["k"]
    return dict(x, e_qi=e_qi, e_ki=e_ki, e_kd=e_kd, e_qe=e_qe, qi=q * e_qi, ki=k * e_ki, kd=k * e_kd, qe=q * e_qe,
                decay=jnp.exp(b_last))


def _chunks(u_ref, lb, tri_lower, cs):
    gates = [_chunk_gates(u_ref, lb, c) for c in cs]
    sums = [_running_sum(tri_lower, x["logf"]) for x in gates]
    return [_chunk_decays(x, b) for x, b in zip(gates, sums)]


def _hgrn2_fwd(u4, lb, g_out, name="hgrn2_fwd", sides=()):
    _, lp, d = u4.shape
    nh, nc = d // HEAD, lp // CHUNK
    per = _group(nc, 22)

    def body(u_ref, lb_ref, g_ref, o_ref, og_ref):
        lb_v = lb_ref[...]
        g = g_ref[...]

        tri_lower = _tri(CHUNK, True).astype(MXU)
        causal = _causal(CHUNK)

        def step(i, st):
            xs = _chunks(u_ref, lb_v, tri_lower, [i * per + u for u in range(per)])
            scores = [_dot(x["qi"], x["ki"], NT) for x in xs]
            updates = [_dot(x["v"], x["kd"], TN) for x in xs]
            states = []
            for x, upd in zip(xs, updates):
                states.append(st)
                st = x["decay"] * st + upd
            outs = [_dot(jnp.where(causal, a, 0.0), x["v"]) + _dot(x["qe"], s, NT)
                    for x, a, s in zip(xs, scores, states)]
            for x, o in zip(xs, outs):
                o_ref[x["sl"], :] = o
                on = o * lax.rsqrt(jnp.mean(o * o, axis=-1, keepdims=True) + EPS) * g
                z = u_ref[3, x["sl"], :]
                og_ref[x["sl"], :] = (on * (z * _sigmoid(z))).astype(og_ref.dtype)
            return st

        lax.fori_loop(0, nc // per, step, jnp.zeros((HEAD, HEAD), F32))

    slab = pl.BlockSpec((lp, HEAD), lambda h: (0, h))
    vec = pl.BlockSpec((1, HEAD), lambda h: (0, h))
    outs, side_results = _pallas(
        body,
        name=name,
        args=(u4, lb, g_out),
        grid=(nh,),
        in_specs=[pl.BlockSpec((4, lp, HEAD), lambda h: (0, 0, h)), vec, vec],
        out_specs=[slab, slab],
        out_shape=[jax.ShapeDtypeStruct((lp, d), F32), jax.ShapeDtypeStruct((lp, d), MXU)],
        semantics=("parallel",),
        vmem_mib=48,
        sides=sides,
    )
    return (*outs, side_results) if sides else tuple(outs)


def _hgrn2_bwd(u4, o, d_og, lb, g_out, name="hgrn2_bwd", sides=()):
    _, lp, d = u4.shape
    nh, nc = d // HEAD, lp // CHUNK
    per = _group(nc, 11)

    def body(u_ref, o_ref, dog_ref, lb_ref, g_ref, du_ref, dlb_ref, dg_ref, st_ref, do_ref):
        lb_v = lb_ref[...]
        g = g_ref[...]

        tri_lower = _tri(CHUNK, True).astype(MXU)
        tri_upper = _tri(CHUNK, False).astype(MXU)
        causal = _causal(CHUNK)

        def fwd_step(i, carry):
            st, dg_acc = carry
            cs = [i * per + u for u in range(per)]
            xs = _chunks(u_ref, lb_v, tri_lower, cs)
            updates = [_dot(x["v"], x["kd"], TN) for x in xs]
            for c, x, upd in zip(cs, xs, updates):
                st_ref[c] = st
                st = x["decay"] * st + upd
            for x in xs:
                sl = x["sl"]
                ov = o_ref[sl, :]
                rstd = lax.rsqrt(jnp.mean(ov * ov, axis=-1, keepdims=True) + EPS)
                on = ov * rstd
                z = u_ref[3, sl, :]
                sz = _sigmoid(z)
                dog = dog_ref[sl, :]
                dy = dog * (z * sz)
                dz = dog * (on * g) * (sz * (1.0 + z * (1.0 - sz)))
                du_ref[3, sl, :] = dz.astype(du_ref.dtype)
                gdy = dy * g
                do = rstd * (gdy - on * jnp.mean(gdy * on, axis=-1, keepdims=True))
                do_ref[sl, :] = jnp.where(x["valid"], do, 0.0)
                dg_acc = dg_acc + jnp.sum(dy * on, axis=0, keepdims=True)
            return st, dg_acc

        _, dg_tot = lax.fori_loop(0, nc // per, fwd_step, (jnp.zeros((HEAD, HEAD), F32), jnp.zeros((1, HEAD), F32)))
        dg_ref[...] = dg_tot

        def bwd_step(i, carry):
            gt, dlb_acc = carry
            cs = [nc - 1 - (i * per + u) for u in range(per)]
            xs = _chunks(u_ref, lb_v, tri_lower, cs)
            dos = [do_ref[x["sl"], :] for x in xs]
            sts = [st_ref[c] for c in cs]
            scores = [jnp.where(causal, _dot(x["qi"], x["ki"], NT), 0.0) for x in xs]
            d_scores = [jnp.where(causal, _dot(do, x["v"], NT), 0.0) for x, do in zip(xs, dos)]
            d_qes = [_dot(do, st) for do, st in zip(dos, sts)]
            g_updates = [_dot(do, x["qe"], TN) for x, do in zip(xs, dos)]
            gts = []
            for x, upd in zip(xs, g_updates):
                gts.append(gt)
                gt = x["decay"] * gt + upd
            d_kds = [_dot(x["v"], g_) for x, g_ in zip(xs, gts)]
            dvs = [_dot(x["kd"], g_, NT) + _dot(a, do, TN) for x, g_, a, do in zip(xs, gts, scores, dos)]
            d_qis = [_dot(d_a, x["ki"]) for x, d_a in zip(xs, d_scores)]
            d_kis = [_dot(d_a, x["qi"], TN) for x, d_a in zip(xs, d_scores)]
            rows = lax.broadcasted_iota(jnp.int32, (CHUNK, HEAD), 0)
            dbs = []
            for x, g_, st, d_qi, d_ki, d_qe, d_kd in zip(xs, gts, sts, d_qis, d_kis, d_qes, d_kds):
                t_qi, t_ki, t_qe, t_kd = d_qi * x["qi"], d_ki * x["ki"], d_qe * x["qe"], d_kd * x["kd"]
                d_decay = jnp.sum(g_ * st, axis=0, keepdims=True)
                d_mid = jnp.sum(t_ki - t_qi, axis=0, keepdims=True)
                d_last = jnp.sum(t_kd, axis=0, keepdims=True) + d_decay * x["decay"]
                dbs.append(t_qi - t_ki + t_qe - t_kd + jnp.where(rows == CHUNK // 2 - 1, d_mid, 0.0)
                           + jnp.where(rows == CHUNK - 1, d_last, 0.0))
            dlogfs = [_running_sum(tri_upper, db) for db in dbs]
            for x, dlogf, dv, d_qi, d_ki, d_qe, d_kd in zip(xs, dlogfs, dvs, d_qis, d_kis, d_qes, d_kds):
                sl = x["sl"]
                dq = d_qi * x["e_qi"] + d_qe * x["e_qe"]
                dk = d_ki * x["e_ki"] + d_kd * x["e_kd"]
                valid, sq, sf, uq = x["valid"], x["sq"], x["sf"], x["uq"]
                dfg = jnp.where(valid, dlogf / x["fg"] - dk, 0.0)
                du_ref[0, sl, :] = jnp.where(valid, dq * (sq * (1.0 + uq * (1.0 - sq))), 0.0).astype(du_ref.dtype)
                du_ref[1, sl, :] = (dfg * (1.0 - lb_v) * (sf * (1.0 - sf))).astype(du_ref.dtype)
                du_ref[2, sl, :] = jnp.where(valid, dv, 0.0).astype(du_ref.dtype)
                dlb_acc = dlb_acc + jnp.sum(dfg * (1.0 - sf), axis=0, keepdims=True)
            return gt, dlb_acc

        _, dlb_tot = lax.fori_loop(0, nc // per, bwd_step, (jnp.zeros((HEAD, HEAD), F32), jnp.zeros((1, HEAD), F32)))
        dlb_ref[...] = dlb_tot

    slab = pl.BlockSpec((lp, HEAD), lambda h: (0, h))
    vec = pl.BlockSpec((1, HEAD), lambda h: (0, h))
    quad = pl.BlockSpec((4, lp, HEAD), lambda h: (0, 0, h))
    outs, side_results = _pallas(
        body,
        name=name,
        args=(u4, o, d_og, lb, g_out),
        grid=(nh,),
        in_specs=[quad, slab, slab, vec, vec],
        out_specs=[quad, vec, vec],
        out_shape=[jax.ShapeDtypeStruct((4, lp, d), MXU), jax.ShapeDtypeStruct((1, d), F32),
                   jax.ShapeDtypeStruct((1, d), F32)],
        scratch_shapes=[pltpu.VMEM((nc, HEAD, HEAD), F32), pltpu.VMEM((lp, HEAD), F32)],
        semantics=("parallel",),
        vmem_mib=58,
        sides=sides,
    )
    return (*outs, side_results) if sides else tuple(outs)


WIDE = 2 * HEAD
INV_SCALE = HEAD ** 0.5


def _split3(x):
    hi = x.astype(MXU).astype(F32)
    rest = x - hi
    mid = rest.astype(MXU).astype(F32)
    return hi, mid, (rest - mid).astype(MXU).astype(F32)


def _extra_cols(rows, first, second):
    lane = lax.broadcasted_iota(jnp.int32, (rows, HEAD), 1)
    out = jnp.where(lane < 6, 1.0, 0.0).astype(F32)
    for base, terms in ((0, first), (3, second)):
        if terms is not None:
            for j, term in enumerate(terms):
                out = jnp.where(lane == base + j, term, out)
    return out


def _head_col(a, h):
    lane = lax.broadcasted_iota(jnp.int32, a.shape, 1)
    return jnp.sum(jnp.where(lane == h, a, 0.0), axis=-1, keepdims=True)


def _attn_operands(ub, ukv, g_q, g_k, f_cum, name="attn_operands", tm=384):
    _, lp, d = ub.shape
    tm = _tile(lp, tm)
    nh = d // HEAD

    def body(q_ref, k_ref, v_ref, gq_ref, gk_ref, f_ref, qa_ref, ka_ref, va_ref):
        i = pl.program_id(0)
        f = f_ref[...]
        is_pad = (i * tm + lax.broadcasted_iota(jnp.int32, (tm, 1), 0)) < ROW_PAD
        ones_only = _extra_cols(tm, None, (0.0, 0.0, 0.0)).astype(MXU)
        for h in range(nh):
            hs = slice(h * HEAD, (h + 1) * HEAD)
            lo, hi = h * WIDE, h * WIDE + HEAD
            f_h = _head_col(f, h)
            for x_ref, g_ref, o_ref in ((q_ref, gq_ref, qa_ref), (k_ref, gk_ref, ka_ref)):
                x = x_ref[:, hs]
                y = x * lax.rsqrt(jnp.mean(x * x, axis=-1, keepdims=True) + EPS)
                o_ref[:, lo:hi] = (y * g_ref[:, hs]).astype(o_ref.dtype)
            qa_ref[:, hi:hi + HEAD] = _extra_cols(tm, _split3(f_h * INV_SCALE), None).astype(MXU)
            f_key = jnp.where(is_pad, -MASK_VALUE, f_h)
            ka_ref[:, hi:hi + HEAD] = _extra_cols(tm, None, _split3(-f_key * INV_SCALE)).astype(MXU)
            va_ref[:, lo:hi] = v_ref[:, hs].astype(MXU)
            va_ref[:, hi:hi + HEAD] = ones_only

    wide = pl.BlockSpec((tm, nh * WIDE), lambda i: (i, 0))
    vec = pl.BlockSpec((1, d), lambda i: (0, 0))
    return pl.pallas_call(
        body,
        name=name,
        grid=(lp // tm,),
        in_specs=[pl.BlockSpec((None, tm, d), lambda i: (0, i, 0)), pl.BlockSpec((None, tm, d), lambda i: (0, i, 0)),
                  pl.BlockSpec((None, tm, d), lambda i: (1, i, 0)), vec, vec, pl.BlockSpec((tm, 128), lambda i: (i, 0))],
        out_specs=[wide, wide, wide],
        out_shape=[jax.ShapeDtypeStruct((lp, nh * WIDE), MXU)] * 3,
        compiler_params=_cparams(("parallel",), 56),
    )(ub, ukv, ukv, g_q, g_k, f_cum)


def _head_rms_bwd_tile(x, g, dy):
    rstd = lax.rsqrt(jnp.mean(x * x, axis=-1, keepdims=True) + EPS)
    xhat = x * rstd
    gdy = dy * g
    dx = rstd * (gdy - xhat * jnp.mean(gdy * xhat, axis=-1, keepdims=True))
    return dx, jnp.sum(dy * xhat, axis=0, keepdims=True)


def _fgate_fwd(ufl, b_f):
    lp = ufl.shape[0]
    nb = lp // 128

    def body(u_ref, b_ref, f_ref):
        def step(i, carry):
            sl = pl.ds(pl.multiple_of(i * 128, 128), 128)
            valid = (i * 128 + lax.broadcasted_iota(jnp.int32, (128, 128), 0)) >= ROW_PAD
            x = u_ref[sl, :] + b_ref[...]
            logf = jnp.where(valid, jnp.minimum(x, 0.0) - jnp.log(1.0 + jnp.exp(-jnp.abs(x))), 0.0)
            f = _dot_exact(_tri(128, True), logf) + carry
            f_ref[sl, :] = f
            return f[127:128, :]

        lax.fori_loop(0, nb, step, jnp.zeros((1, 128), F32))

    return pl.pallas_call(body, name="fgate_fwd", out_shape=jax.ShapeDtypeStruct((lp, 128), F32))(ufl, b_f)


def _fgate_bwd(ufl, b_f, d_f):
    lp = ufl.shape[0]
    nb = lp // 128

    def body(u_ref, b_ref, df_ref, du_ref, db_ref):
        def step(j, carry):
            later, db_acc = carry
            i = nb - 1 - j
            sl = pl.ds(pl.multiple_of(i * 128, 128), 128)
            valid = (i * 128 + lax.broadcasted_iota(jnp.int32, (128, 128), 0)) >= ROW_PAD
            x = u_ref[sl, :] + b_ref[...]
            df = df_ref[sl, :]
            dlogf = _dot_exact(_tri(128, False), df) + later
            dx = jnp.where(valid, dlogf * _sigmoid(-x), 0.0)
            du_ref[sl, :] = dx.astype(du_ref.dtype)
            return later + jnp.sum(df, axis=0, keepdims=True), db_acc + jnp.sum(dx, axis=0, keepdims=True)

        _, db_tot = lax.fori_loop(0, nb, step, (jnp.zeros((1, 128), F32), jnp.zeros((1, 128), F32)))
        db_ref[...] = db_tot

    return pl.pallas_call(
        body, name="fgate_bwd",
        out_shape=[jax.ShapeDtypeStruct((lp, 128), MXU), jax.ShapeDtypeStruct((1, 128), F32)],
    )(ufl, b_f, d_f)


STRIP = 32
PAIR = 2


def _strip_causal(r, t):
    row = r + lax.broadcasted_iota(jnp.int32, (STRIP, t), 0)
    col = lax.broadcasted_iota(jnp.int32, (STRIP, t), 1)
    return col <= row


def _fox_fwd(qa, ka, va, ub, name="fox_fwd", t=384):
    lp = qa.shape[0]
    d = ub.shape[2]
    t = _tile(lp, t)
    nh, nq = d // HEAD, lp // t
    scale = HEAD ** -0.5

    assert nh % PAIR == 0
    heads = range(PAIR)

    def body(q_ref, k_ref, v_ref, z_ref, o_ref, olo_ref, og_ref, qb_ref,
             s_ref, p_ref, m_ref, a_ref, l_ref, acc_ref):
        qb = pl.program_id(1)
        m_ref[...] = jnp.full((PAIR, t, 1), MASK_VALUE, F32)
        l_ref[...] = jnp.zeros((PAIR, t, 128), F32)
        acc_ref[...] = jnp.zeros((PAIR, 2 * t, HEAD), F32)
        p_ref[1] = jnp.zeros((PAIR, 2 * t, t), MXU)
        a_ref[1] = jnp.ones((PAIR, t, 1), F32)

        def scores(kb, buf):
            ks = pl.ds(pl.multiple_of(kb * t, t), t)
            for j in heads:
                ws = slice(j * WIDE, (j + 1) * WIDE)
                s_ref[buf, j] = _dot(q_ref[:, ws], k_ref[ks, ws], NT)

        def weighted_sum(kb, buf):
            ks = pl.ds(pl.multiple_of(kb * t, t), t)
            for j in heads:
                pv = _dot(p_ref[buf, j], v_ref[ks, j * WIDE:j * WIDE + HEAD])
                alpha = a_ref[buf, j]
                acc_ref[j, 0:t, :] = alpha * acc_ref[j, 0:t, :] + pv[0:t]
                acc_ref[j, t:2 * t, :] = alpha * acc_ref[j, t:2 * t, :] + pv[t:2 * t]

        def softmax_update(buf, diagonal):
            for j in heads:
                for r in range(0, t, STRIP):
                    rs = slice(r, r + STRIP)
                    x = s_ref[buf, j, rs, :] * scale
                    if diagonal:
                        x = jnp.where(_strip_causal(r, t), x, MASK_VALUE)
                    m_old = m_ref[j, rs, :]
                    m_new = jnp.maximum(m_old, jnp.max(x, axis=-1, keepdims=True))
                    alpha = jnp.exp(m_old - m_new)
                    p = jnp.exp(x - m_new)
                    m_ref[j, rs, :] = m_new
                    a_ref[buf, j, rs, :] = alpha
                    l_ref[j, rs, :] = alpha * l_ref[j, rs, :] + sum(p[:, c:c + 128] for c in range(0, t, 128))
                    p_hi = p.astype(MXU)
                    p_ref[buf, j, rs, :] = p_hi
                    p_ref[buf, j, t + r:t + r + STRIP, :] = (p - p_hi.astype(F32)).astype(MXU)

        def off_diagonal(kb, cur):
            weighted_sum(jnp.maximum(kb - 1, 0), 1 - cur)
            scores(kb + 1, 1 - cur)
            softmax_update(cur, False)

        def diagonal(cur):
            weighted_sum(jnp.maximum(qb - 1, 0), 1 - cur)
            softmax_update(cur, True)
            weighted_sum(qb, cur)

        def two_blocks(i, carry):
            off_diagonal(2 * i, 0)
            off_diagonal(2 * i + 1, 1)
            return carry

        scores(0, 0)
        lax.fori_loop(0, qb // 2, two_blocks, 0)

        @pl.when(lax.rem(qb, 2) == 0)
        def _():
            diagonal(0)

        @pl.when(lax.rem(qb, 2) == 1)
        def _():
            off_diagonal(qb - 1, 0)
            diagonal(1)

        is_pad = (qb * t + lax.broadcasted_iota(jnp.int32, (t, 1), 0)) < ROW_PAD
        for j in heads:
            hs = slice(j * HEAD, (j + 1) * HEAD)
            l = jnp.sum(l_ref[j], axis=-1, keepdims=True)
            o = acc_ref[j, 0:t, :] / l
            o_ref[:, hs] = o
            olo_ref[:, hs] = acc_ref[j, t:2 * t, :] / l
            z = z_ref[:, hs]
            og_ref[:, hs] = (o * (z * _sigmoid(z))).astype(og_ref.dtype)
            extra = q_ref[:, j * WIDE + HEAD:(j + 1) * WIDE].astype(F32)
            f_scaled = extra[:, 0:1] + extra[:, 1:2] + extra[:, 2:3]
            log_term = jnp.where(is_pad, MASK_VALUE * INV_SCALE, f_scaled - (m_ref[j] + jnp.log(l)) * INV_SCALE)
            qb_ref[:, j * WIDE:j * WIDE + HEAD] = q_ref[:, j * WIDE:j * WIDE + HEAD]
            qb_ref[:, j * WIDE + HEAD:(j + 1) * WIDE] = _extra_cols(t, _split3(log_term), None).astype(qb_ref.dtype)

    scratch = [pltpu.VMEM((2, PAIR, t, t), F32), pltpu.VMEM((2, PAIR, 2 * t, t), MXU), pltpu.VMEM((PAIR, t, 1), F32),
               pltpu.VMEM((2, PAIR, t, 1), F32), pltpu.VMEM((PAIR, t, 128), F32), pltpu.VMEM((PAIR, 2 * t, HEAD), F32)]
    tile = pl.BlockSpec((t, PAIR * HEAD), lambda g, i: (i, g))
    wide_tile = pl.BlockSpec((t, PAIR * WIDE), lambda g, i: (i, g))
    wide_all = pl.BlockSpec((lp, PAIR * WIDE), lambda g, i: (0, g))
    return pl.pallas_call(
        body,
        name=name,
        grid=(nh // PAIR, nq),
        in_specs=[wide_tile, wide_all, wide_all, pl.BlockSpec((None, t, PAIR * HEAD), lambda g, i: (1, i, g))],
        out_specs=[tile, tile, tile, wide_tile],
        out_shape=[jax.ShapeDtypeStruct((lp, d), F32), jax.ShapeDtypeStruct((lp, d), F32),
                   jax.ShapeDtypeStruct((lp, d), MXU), jax.ShapeDtypeStruct((lp, nh * WIDE), MXU)],
        scratch_shapes=scratch,
        compiler_params=_cparams(("parallel", "arbitrary"), 48),
    )(qa, ka, va, ub)


def _fox_bwd(qb, ka, va, doa, ukv, g_k, ub, g_q, name="fox_bwd", t=384, sides=()):
    lp = qb.shape[0]
    d = ukv.shape[2]
    t = _tile(lp, t)
    nh, nk = d // HEAD, lp // t
    scale = HEAD ** -0.5

    assert nh % PAIR == 0
    heads = range(PAIR)

    def body(q_ref, do_ref, k_ref, v_ref, kraw_ref, gk_ref, qraw_ref, gq_ref, dqraw_ref, dgq_ref, dukv_ref, dgk_ref,
             dfk_ref, s_ref, dp_ref, p_ref, ds_ref, col_ref, dk_ref, dv_ref, dq_ref):
        kb = pl.program_id(1)

        @pl.when(kb == 0)
        def _():
            dq_ref[...] = jnp.zeros_like(dq_ref)
            dgk_ref[...] = jnp.zeros_like(dgk_ref)

        dk_ref[...] = jnp.zeros_like(dk_ref)
        dv_ref[...] = jnp.zeros_like(dv_ref)
        col_ref[...] = jnp.zeros_like(col_ref)

        def step(qb, diagonal):
            qs = pl.ds(pl.multiple_of(qb * t, t), t)
            for j in heads:
                ws = slice(j * WIDE, (j + 1) * WIDE)
                s_ref[j] = _dot(q_ref[qs, ws], k_ref[:, ws], NT)
                dp_ref[j] = _dot(do_ref[qs, ws], v_ref[:, ws], NT)
            for j in heads:
                for r in range(0, t, STRIP):
                    rs = slice(r, r + STRIP)
                    x = s_ref[j, rs, :] * scale
                    if diagonal:
                        x = jnp.where(_strip_causal(r, t), x, MASK_VALUE)
                    p = jnp.exp(x)
                    ds = p * dp_ref[j, rs, :]
                    p_ref[j, rs, :] = p.astype(MXU)
                    ds_ref[j, rs, :] = (ds * scale).astype(MXU)
                    col_ref[j] += ds
            for j in heads:
                hs = slice(j * HEAD, (j + 1) * HEAD)
                narrow = slice(j * WIDE, j * WIDE + HEAD)
                dsb = ds_ref[j]
                dv_ref[:, hs] += _dot(p_ref[j], do_ref[qs, narrow], TN)
                dq_ref[qs, hs] += _dot(dsb, k_ref[:, narrow])
                dk_ref[:, hs] += _dot(dsb, q_ref[qs, narrow], TN)

        def off_diagonal(qb, carry):
            step(qb, False)
            return carry

        step(kb, True)
        lax.fori_loop(kb + 1, nk, off_diagonal, 0)
        for j in heads:
            hs = slice(j * HEAD, (j + 1) * HEAD)
            dfk_ref[j] = -jnp.sum(col_ref[j], axis=0, keepdims=True)
            dx, dg = _head_rms_bwd_tile(kraw_ref[:, hs], gk_ref[:, hs], dk_ref[:, hs])
            dukv_ref[0, :, hs] = dx.astype(dukv_ref.dtype)
            dukv_ref[1, :, hs] = dv_ref[:, hs].astype(dukv_ref.dtype)
            dgk_ref[:, hs] += dg

        @pl.when(kb == nk - 1)
        def _():
            dgq_ref[...] = jnp.zeros_like(dgq_ref)

            def rows(c, carry):
                rs = pl.ds(pl.multiple_of(c * t, t), t)
                for j in heads:
                    hs = slice(j * HEAD, (j + 1) * HEAD)
                    dx, dg = _head_rms_bwd_tile(qraw_ref[rs, hs], gq_ref[:, hs], dq_ref[rs, hs])
                    dqraw_ref[rs, hs] = dx.astype(dqraw_ref.dtype)
                    dgq_ref[:, hs] += dg
                return carry

            lax.fori_loop(0, nk, rows, 0)

    scratch = [pltpu.VMEM((PAIR, t, t), F32), pltpu.VMEM((PAIR, t, t), F32), pltpu.VMEM((PAIR, t, t), MXU),
               pltpu.VMEM((PAIR, t, t), MXU), pltpu.VMEM((PAIR, STRIP, t), F32),
               pltpu.VMEM((t, PAIR * HEAD), F32), pltpu.VMEM((t, PAIR * HEAD), F32), pltpu.VMEM((lp, PAIR * HEAD), F32)]
    whole = pl.BlockSpec((lp, PAIR * HEAD), lambda g, j: (0, g))
    wide_all = pl.BlockSpec((lp, PAIR * WIDE), lambda g, j: (0, g))
    wide_tile = pl.BlockSpec((t, PAIR * WIDE), lambda g, j: (j, g))
    vec = pl.BlockSpec((1, PAIR * HEAD), lambda g, j: (0, g))
    outs, side_results = _pallas(
        body,
        name=name,
        args=(qb, doa, ka, va, ukv, g_k, ub, g_q),
        grid=(nh // PAIR, nk),
        in_specs=[wide_all, wide_all, wide_tile, wide_tile,
                  pl.BlockSpec((None, t, PAIR * HEAD), lambda g, j: (0, j, g)), vec,
                  pl.BlockSpec((None, lp, PAIR * HEAD), lambda g, j: (0, 0, g)), vec],
        out_specs=[whole, vec, pl.BlockSpec((2, t, PAIR * HEAD), lambda g, j: (0, j, g)), vec,
                   pl.BlockSpec((PAIR, 1, t), lambda g, j: (g, 0, j))],
        out_shape=[jax.ShapeDtypeStruct((lp, d), MXU), jax.ShapeDtypeStruct((1, d), F32),
                   jax.ShapeDtypeStruct((2, lp, d), MXU), jax.ShapeDtypeStruct((1, d), F32),
                   jax.ShapeDtypeStruct((nh, 1, lp), F32)],
        scratch_shapes=scratch,
        semantics=("parallel", "arbitrary"),
        vmem_mib=48,
        sides=sides,
    )
    return (*outs, side_results) if sides else tuple(outs)


def _fox_do(d_og, o, o_lo, ub, name="fox_do", tm=384):
    lp, d = o.shape
    tm = _tile(lp, tm)
    nh = d // HEAD

    def body(dog_ref, o_ref, olo_ref, z_ref, doa_ref):
        for h in range(nh):
            hs = slice(h * HEAD, (h + 1) * HEAD)
            z = z_ref[:, hs]
            do = (dog_ref[:, hs] * (z * _sigmoid(z))).astype(doa_ref.dtype)
            delta = jnp.sum(do.astype(F32) * (o_ref[:, hs] + olo_ref[:, hs]), axis=-1, keepdims=True)
            doa_ref[:, h * WIDE:h * WIDE + HEAD] = do
            doa_ref[:, h * WIDE + HEAD:(h + 1) * WIDE] = _extra_cols(tm, _split3(-delta), (0.0, 0.0, 0.0)).astype(
                doa_ref.dtype)

    row = pl.BlockSpec((tm, d), lambda i: (i, 0))
    return pl.pallas_call(
        body,
        name=name,
        grid=(lp // tm,),
        in_specs=[row, row, row, pl.BlockSpec((None, tm, d), lambda i: (1, i, 0))],
        out_specs=pl.BlockSpec((tm, nh * WIDE), lambda i: (i, 0)),
        out_shape=jax.ShapeDtypeStruct((lp, nh * WIDE), MXU),
        compiler_params=_cparams(("parallel",), 56),
    )(d_og, o, o_lo, ub)


def _fox_gate_bwd(ub, d_og, o, dq, name="fox_gate_bwd", tm=384):
    _, lp, d = ub.shape
    tm = _tile(lp, tm)

    def body(z_ref, dog_ref, o_ref, dq_ref, dub_ref):
        z = z_ref[...]
        sz = _sigmoid(z)
        dub_ref[0] = dq_ref[...]
        dub_ref[1] = (dog_ref[...] * o_ref[...] * (sz * (1.0 + z * (1.0 - sz)))).astype(dub_ref.dtype)

    row = pl.BlockSpec((tm, d), lambda i: (i, 0))
    return pl.pallas_call(
        body,
        name=name,
        grid=(lp // tm,),
        in_specs=[pl.BlockSpec((None, tm, d), lambda i: (1, i, 0)), row, row, row],
        out_specs=pl.BlockSpec((2, tm, d), lambda i: (0, i, 0)),
        out_shape=jax.ShapeDtypeStruct((2, lp, d), MXU),
        compiler_params=_cparams(("parallel",), 48),
    )(ub, d_og, o, dq)


def _loss(h, target, name="loss_head"):
    lp, d = h.shape
    nb = lp // 128

    def body(h_ref, t_ref, loss_ref, dh_ref, acc_ref):
        i = pl.program_id(0)

        @pl.when(i == 0)
        def _():
            acc_ref[...] = jnp.zeros_like(acc_ref)
            dh_ref[...] = jnp.zeros_like(dh_ref)

        @pl.when(i > 0)
        def _():
            err = h_ref[...] - t_ref[...]
            dh_ref[...] = err * (1.0 / d)
            acc_ref[...] += jnp.sum(jnp.sum(err * err, axis=-1, keepdims=True) * (1.0 / d), axis=0, keepdims=True)

        @pl.when(i == nb - 1)
        def _():
            loss_ref[...] = 0.5 * acc_ref[...]

    return pl.pallas_call(
        body,
        name=name,
        grid=(nb,),
        in_specs=[pl.BlockSpec((128, d), lambda i: (i, 0)),
                  pl.BlockSpec((128, d), lambda i: (jnp.maximum(i - 1, 0), 0))],
        out_specs=[pl.BlockSpec((1, 1), lambda i: (0, 0)), pl.BlockSpec((128, d), lambda i: (i, 0))],
        out_shape=[jax.ShapeDtypeStruct((1, 1), F32), jax.ShapeDtypeStruct((lp, d), F32)],
        scratch_shapes=[pltpu.VMEM((1, 1), F32)],
        compiler_params=_cparams(("arbitrary",), 32),
    )(h, target)


def _adam_math(w, g, m, v):
    m = ADAM_B1 * m + (1.0 - ADAM_B1) * g
    v = ADAM_B2 * v + (1.0 - ADAM_B2) * (g * g)
    m_hat = m / (1.0 - ADAM_B1 ** ADAM_STEP)
    v_hat = v / (1.0 - ADAM_B2 ** ADAM_STEP)
    delta = -ADAM_LR * (m_hat / (jnp.sqrt(v_hat) + ADAM_EPS) + ADAM_WD * w)
    return delta, m, v


def _adamw(parts, w, m, v, name, tm=512):
    n, r, c = parts.shape
    tm = tm if r % tm == 0 else r

    def body(p_ref, w_ref, m_ref, v_ref, g_ref, d_ref, nm_ref, nv_ref):
        g = p_ref[0].astype(F32)
        for j in range(1, n):
            g = g + p_ref[j].astype(F32)
        g_ref[...] = g
        d_ref[...], nm_ref[...], nv_ref[...] = _adam_math(w_ref[...], g, m_ref[...], v_ref[...])

    row = pl.BlockSpec((tm, c), lambda i: (i, 0))
    return pl.pallas_call(
        body,
        name=name,
        grid=(r // tm,),
        in_specs=[pl.BlockSpec((n, tm, c), lambda i: (0, i, 0)), row, row, row],
        out_specs=[row] * 4,
        out_shape=[jax.ShapeDtypeStruct((r, c), F32)] * 4,
        compiler_params=_cparams(("parallel",), 48),
    )(parts, w, m, v)


def _place():
    x, y, c = lax.axis_index("x"), lax.axis_index("y"), lax.axis_index("c")
    return x, y, c


def _other_chips(x, y):
    return [(1 - x, y), (x, 1 - y), (1 - x, 1 - y)]


def _any_spec():
    return pl.BlockSpec(memory_space=pl.ANY)


class _Side:
    def __init__(self, ins, outs, sems, start, finish, aliases=None):
        self.ins, self.outs, self.sems = list(ins), list(outs), sems
        self.start, self.finish = start, finish
        self.aliases = dict(aliases or {})


def _pallas(body, *, name, out_shape, args=(), grid=(), in_specs=(), out_specs=(), scratch_shapes=(),
            semantics=(), vmem_mib=None, sides=(), aliases=None):
    n_in, n_out, n_scr = len(args), len(out_shape), len(scratch_shapes)
    side_ins = [a for s in sides for a in s.ins]
    side_outs = [o for s in sides for o in s.outs]
    side_sems = [pltpu.SemaphoreType.DMA((max(k, 1),)) for s in sides for k in s.sems]
    aliases, in_at, out_at = dict(aliases or {}), n_in, n_out
    for s in sides:
        aliases.update({in_at + i: out_at + o for i, o in s.aliases.items()})
        in_at, out_at = in_at + len(s.ins), out_at + len(s.outs)

    def wrapped(*refs):
        at = [0]

        def take(k):
            got = refs[at[0]:at[0] + k]
            at[0] += k
            return got

        main_in = take(n_in)
        s_in = [take(len(s.ins)) for s in sides]
        main_out = take(n_out)
        s_out = [take(len(s.outs)) for s in sides]
        main_scr = take(n_scr)
        s_sem = [take(3) for s in sides]

        def run(stage):
            for s, i_, o_, m_ in zip(sides, s_in, s_out, s_sem):
                getattr(s, stage)(i_, o_, *m_)

        first = last = None
        for k, g in enumerate(grid):
            i = pl.program_id(k)
            first = (i == 0) if first is None else jnp.logical_and(first, i == 0)
            last = (i == g - 1) if last is None else jnp.logical_and(last, i == g - 1)
        if sides:
            run("start") if first is None else pl.when(first)(lambda: run("start"))
        body(*main_in, *main_out, *main_scr)
        if sides:
            run("finish") if last is None else pl.when(last)(lambda: run("finish"))

    kw = {}
    if grid:
        kw["grid"] = grid
    if semantics or vmem_mib:
        sem = tuple("arbitrary" for _ in grid) if sides else tuple(semantics)
        kw["compiler_params"] = pltpu.CompilerParams(
            dimension_semantics=sem or None, vmem_limit_bytes=vmem_mib * MIB if vmem_mib else None)
    res = pl.pallas_call(
        wrapped,
        name=name,
        in_specs=list(in_specs) + [_any_spec()] * len(side_ins),
        out_specs=list(out_specs) + [_any_spec()] * len(side_outs),
        out_shape=list(out_shape) + side_outs,
        scratch_shapes=list(scratch_shapes) + side_sems,
        input_output_aliases=aliases,
        **kw,
    )(*args, *side_ins)
    main, rest, per_side = list(res[:n_out]), list(res[n_out:]), []
    for s in sides:
        per_side.append(rest[:len(s.outs)])
        rest = rest[len(s.outs):]
    return main, per_side


def _slot(p):
    return 4 * p[0] + 2 * p[1] + p[2]


def _sibling_side(grads):
    n = len(grads)

    def copies(ins, outs, send_sems, recv_sems):
        x, y, c = _place()
        return [pltpu.make_async_remote_copy(
            src_ref=ins[t].at[2 * chip + (1 - c)], dst_ref=outs[t].at[chip],
            send_sem=send_sems.at[4 * t + chip], recv_sem=recv_sems.at[4 * t + chip],
            device_id=(x, y, 1 - c), device_id_type=MESH) for t in range(n) for chip in range(N_CHIP)]

    def start(ins, outs, send_sems, recv_sems, local_sems):
        for cp in copies(ins, outs, send_sems, recv_sems):
            cp.start()

    def finish(ins, outs, send_sems, recv_sems, local_sems):
        for cp in copies(ins, outs, send_sems, recv_sems):
            cp.wait()

    outs = [jax.ShapeDtypeStruct((N_CHIP,) + g.shape[1:], g.dtype) for g in grads]
    return _Side(grads, outs, (4 * n, 4 * n, 0), start, finish)


def _chips_side(partials):
    n = len(partials)

    def copies(ins, outs, send_sems, recv_sems, local_sems):
        x, y, c = _place()
        my_chip = 2 * x + y
        local = [pltpu.make_async_copy(ins[t].at[my_chip], outs[t].at[my_chip], local_sems.at[t]) for t in range(n)]
        sends, recvs = [], []
        for t in range(n):
            for j, chip in enumerate(_other_chips(x, y)):
                their = 2 * chip[0] + chip[1]
                sems = dict(send_sem=send_sems.at[3 * t + j], recv_sem=recv_sems.at[3 * t + j],
                            device_id=(*chip, c), device_id_type=MESH)
                sends.append(pltpu.make_async_remote_copy(src_ref=ins[t].at[their], dst_ref=outs[t].at[my_chip], **sems))
                recvs.append(pltpu.make_async_remote_copy(src_ref=ins[t].at[my_chip], dst_ref=outs[t].at[their], **sems))
        return local, sends, recvs

    def start(ins, outs, send_sems, recv_sems, local_sems):
        local, sends, _ = copies(ins, outs, send_sems, recv_sems, local_sems)
        for cp in local + sends:
            cp.start()

    def finish(ins, outs, send_sems, recv_sems, local_sems):
        local, sends, recvs = copies(ins, outs, send_sems, recv_sems, local_sems)
        for cp in sends:
            cp.wait_send()
        for cp in recvs:
            cp.wait_recv()
        for cp in local:
            cp.wait()

    outs = [jax.ShapeDtypeStruct(p.shape, p.dtype) for p in partials]
    return _Side(partials, outs, (3 * n, 3 * n, n), start, finish)


def _gather_own_side(blocks):
    n = len(blocks)

    def copies(ins, outs, send_sems, recv_sems, local_sems):
        x, y, c = _place()
        me = (x, y, c)
        peers = [(x, y, 1 - c)] + [(*chip, c) for chip in _other_chips(x, y)]
        local = [pltpu.make_async_copy(ins[t], outs[t].at[_slot(me)], local_sems.at[t]) for t in range(n)]
        sends, recvs = [], []
        for t in range(n):
            for k, peer in enumerate(peers):
                sems = dict(send_sem=send_sems.at[4 * t + k], recv_sem=recv_sems.at[4 * t + k],
                            device_id=peer, device_id_type=MESH)
                sends.append(pltpu.make_async_remote_copy(src_ref=ins[t], dst_ref=outs[t].at[_slot(me)], **sems))
                recvs.append(pltpu.make_async_remote_copy(src_ref=ins[t], dst_ref=outs[t].at[_slot(peer)], **sems))
        return local, sends, recvs

    def start(ins, outs, send_sems, recv_sems, local_sems):
        local, sends, _ = copies(ins, outs, send_sems, recv_sems, local_sems)
        for cp in local + sends:
            cp.start()

    def finish(ins, outs, send_sems, recv_sems, local_sems):
        local, sends, recvs = copies(ins, outs, send_sems, recv_sems, local_sems)
        for cp in sends:
            cp.wait_send()
        for cp in recvs:
            cp.wait_recv()
        for cp in local:
            cp.wait()

    outs = [jax.ShapeDtypeStruct((N_DEV,) + b.shape, b.dtype) for b in blocks]
    return _Side(blocks, outs, (4 * n, 4 * n, n), start, finish)


def _gather_pass_side(gathered):
    n = len(gathered)

    def copies(outs, send_sems, recv_sems):
        x, y, c = _place()
        sends, recvs = [], []
        for t in range(n):
            for j, chip in enumerate(_other_chips(x, y)):
                sems = dict(send_sem=send_sems.at[3 * t + j], recv_sem=recv_sems.at[3 * t + j],
                            device_id=(x, y, 1 - c), device_id_type=MESH)
                mine, theirs = outs[t].at[_slot((*chip, c))], outs[t].at[_slot((*chip, 1 - c))]
                sends.append(pltpu.make_async_remote_copy(src_ref=mine, dst_ref=mine, **sems))
                recvs.append(pltpu.make_async_remote_copy(src_ref=mine, dst_ref=theirs, **sems))
        return sends, recvs

    def start(ins, outs, send_sems, recv_sems, local_sems):
        for cp in copies(outs, send_sems, recv_sems)[0]:
            cp.start()

    def finish(ins, outs, send_sems, recv_sems, local_sems):
        sends, recvs = copies(outs, send_sems, recv_sems)
        for cp in sends:
            cp.wait_send()
        for cp in recvs:
            cp.wait_recv()

    outs = [jax.ShapeDtypeStruct(g.shape, g.dtype) for g in gathered]
    return _Side(gathered, outs, (3 * n, 3 * n, 0), start, finish, aliases={t: t for t in range(n)})


def _alone(side, name):
    return _pallas(lambda: None, name=name, out_shape=[], sides=[side])[1][0]


def _all_gather(blocks, name):
    n = len(blocks)

    def body(*refs):
        in_refs, out_refs = refs[:n], refs[n:2 * n]
        send_sems, recv_sems, local_sems = refs[2 * n:]
        x, y, c = _place()
        me, sibling = (x, y, c), (x, y, 1 - c)
        chips = _other_chips(x, y)

        def slot(p):
            return 4 * p[0] + 2 * p[1] + p[2]

        def copy(t, k, block, to, src=None):
            dst = out_refs[t].at[slot(block)]
            return pltpu.make_async_remote_copy(
                src_ref=dst if src is None else src, dst_ref=dst,
                send_sem=send_sems.at[7 * t + k], recv_sem=recv_sems.at[7 * t + k],
                device_id=to, device_id_type=MESH)

        started = []
        for t in range(n):
            mine = pltpu.make_async_copy(in_refs[t], out_refs[t].at[slot(me)], local_sems.at[t])
            mine.start()
            started.append(mine)
        sends = []
        for t in range(n):
            first = [copy(t, 0, me, sibling, src=in_refs[t])]
            first += [copy(t, 1 + j, me, (*chip, c), src=in_refs[t]) for j, chip in enumerate(chips)]
            for cp in first:
                cp.start()
            sends += first
        for t in range(n):
            for j, chip in enumerate(chips):
                copy(t, 1 + j, (*chip, c), me).wait_recv()
                passed = copy(t, 4 + j, (*chip, c), sibling)
                passed.start()
                sends.append(passed)
        for t in range(n):
            copy(t, 0, sibling, me).wait_recv()
            for j, chip in enumerate(chips):
                copy(t, 4 + j, (*chip, 1 - c), me).wait_recv()
        for cp in sends:
            cp.wait_send()
        for mine in started:
            mine.wait()

    return pl.pallas_call(
        body,
        name=name,
        in_specs=[_any_spec()] * n,
        out_specs=[_any_spec()] * n,
        out_shape=[jax.ShapeDtypeStruct((N_DEV,) + b.shape, b.dtype) for b in blocks],
        scratch_shapes=[pltpu.SemaphoreType.DMA((7 * n,)), pltpu.SemaphoreType.DMA((7 * n,)),
                        pltpu.SemaphoreType.DMA((n,))],
    )(*blocks)


def _pair_sum(g8, got, name, tm=1024):
    _, r, c = g8.shape
    tm = tm if r % tm == 0 else r
    core = lax.axis_index("c")

    def body(core_ref, mine_ref, got_ref, o_ref):
        south_first = core_ref[0] == 0
        a, b = mine_ref[...], got_ref[...]
        o_ref[...] = (jnp.where(south_first, a, b) + jnp.where(south_first, b, a)).astype(o_ref.dtype)

    return pl.pallas_call(
        body,
        name=name,
        grid_spec=pltpu.PrefetchScalarGridSpec(
            num_scalar_prefetch=1,
            grid=(N_CHIP, r // tm),
            in_specs=[pl.BlockSpec((None, tm, c), lambda j, i, core_ref: (2 * j + core_ref[0], i, 0)),
                      pl.BlockSpec((None, tm, c), lambda j, i, core_ref: (j, i, 0))],
            out_specs=pl.BlockSpec((None, tm, c), lambda j, i, core_ref: (j, i, 0)),
        ),
        out_shape=jax.ShapeDtypeStruct((N_CHIP, r, c), PAYLOAD),
        compiler_params=_cparams(("parallel", "parallel"), 32),
    )(jnp.reshape(core, (1,)).astype(jnp.int32), g8, got)


def _all_reduce_small(s, name="small_all_reduce"):
    r, c = s.shape

    def body(s_ref, o_ref, buf_ref, send_sems, recv_sems):
        x, y, c_ = _place()
        me = 4 * x + 2 * y + c_
        buf_ref[me] = s_ref[...]

        def copy(k, slot, peer):
            return pltpu.make_async_remote_copy(
                src_ref=s_ref, dst_ref=buf_ref.at[slot],
                send_sem=send_sems.at[k - 1], recv_sem=recv_sems.at[k - 1],
                device_id=peer, device_id_type=MESH)

        peers = []
        for k in range(1, N_DEV):
            peer = (x ^ ((k >> 2) & 1), y ^ ((k >> 1) & 1), c_ ^ (k & 1))
            peers.append(peer)
            copy(k, me, peer).start()
        for k, peer in zip(range(1, N_DEV), peers):
            cp = copy(k, 4 * peer[0] + 2 * peer[1] + peer[2], peer)
            cp.wait_send()
            cp.wait_recv()
        total = buf_ref[0]
        for j in range(1, N_DEV):
            total = total + buf_ref[j]
        o_ref[...] = total

    return pl.pallas_call(
        body,
        name=name,
        out_shape=jax.ShapeDtypeStruct((r, c), F32),
        in_specs=[pl.BlockSpec(memory_space=pltpu.VMEM)],
        out_specs=pl.BlockSpec(memory_space=pltpu.VMEM),
        scratch_shapes=[pltpu.VMEM((N_DEV, r, c), F32), pltpu.SemaphoreType.DMA((N_DEV - 1,)),
                        pltpu.SemaphoreType.DMA((N_DEV - 1,))],
    )(s)


def _kv_weights(g_kv):
    d = g_kv.shape[1]
    nh = d // HEAD
    wkv = jnp.reshape(jnp.transpose(g_kv, (1, 0, 2)), (d, -1))
    return wkv[:, :2 * d], jnp.pad(wkv[:, 2 * d:], ((0, 0), (0, 128 - nh)))


def _kv_grad_blocks(dwkv2, dwfl):
    d = dwkv2.shape[0]
    dwkv = jnp.concatenate([dwkv2, dwfl[:, :d // HEAD]], axis=1)
    return jnp.transpose(jnp.reshape(dwkv, (d, N_DEV, -1)), (1, 0, 2))


def _local_step(x, target, meta, gamma, a_norm, wa_in, a_out_norm, wa_out, kv_norm, late, b_f, g_k, b_norm, g_q,
                dist):
    d = x.shape[1]
    nh = d // HEAD
    cols = d // N_DEV
    h0 = jnp.concatenate([jnp.zeros((ROW_PAD, d), F32), meta, x], axis=0)
    lp = h0.shape[0]
    big = dict(tm=1408, tn=512, tk=2048)

    lb = _lb_fwd(gamma)
    (hn_a,) = _rms_fwd(h0, [a_norm], "rms_a")
    if dist:
        a_out_blk, kv_blk, b_in_blk, b_out_blk = late
        u4, (first,) = _matmul(hn_a, wa_in, "nn", F32, "a_in", out_parts=4,
                               sides=[_gather_own_side([a_out_blk, b_in_blk])], **big)
        o_a, og_a, ((g_a_out, g_b_in), (g_kv,)) = _hgrn2_fwd(
            u4, lb, a_out_norm, sides=[_gather_pass_side(first), _gather_own_side([kv_blk])])
        wa_out = jnp.reshape(g_a_out, (d, d))
        h1, ((g_kv,), (g_b_out,)) = _matmul(og_a, wa_out, "nn", F32, "a_out", add=h0,
                                            sides=[_gather_pass_side([g_kv]), _gather_own_side([b_out_blk])], **big)
        (wkv2, wfl), wb_in = _kv_weights(g_kv), g_b_in
    else:
        u4 = _matmul(hn_a, wa_in, "nn", F32, "a_in", out_parts=4, **big)
        o_a, og_a = _hgrn2_fwd(u4, lb, a_out_norm)
        h1 = _matmul(og_a, wa_out, "nn", F32, "a_out", add=h0, **big)
        wkv2, wfl, wb_in, wb_out = late
    hk, hb = _rms_fwd(h1, [kv_norm, b_norm], "rms_kv_b")
    if dist:
        ukv, ((g_b_out,),) = _matmul(hk, wkv2, "nn", F32, "kv_in", out_parts=2,
                                     sides=[_gather_pass_side([g_b_out])], **big)
        wb_out = jnp.reshape(g_b_out, (d, d))
    else:
        ukv = _matmul(hk, wkv2, "nn", F32, "kv_in", out_parts=2, **big)
    ufl = _matmul(hk, wfl, "nn", F32, "kv_f", **big)
    ub = _matmul(hb, wb_in, "nn", F32, "b_in", out_parts=2, **big)
    f_cum = _fgate_fwd(ufl, b_f)
    qa, ka, va = _attn_operands(ub, ukv, g_q, g_k, f_cum)
    o_b, o_lo, og_b, qb = _fox_fwd(qa, ka, va, ub)
    h2 = _matmul(og_b, wb_out, "nn", F32, "b_out", add=h1, **big)
    loss, dh2 = _loss(h2, target)

    dx_t = dict(tm=1408, tn=512, tk=2048, vmem_mib=56)
    dx_parts_t = dict(tm=704, tn=512, tk=2048, vmem_mib=58, k_whole=True)
    dw_t = dict(tm=512, tn=1024, tk=lp, vmem_mib=58)
    dw_f32_t = dict(tm=1024, tn=512, tk=lp, vmem_mib=58)
    def to_sibling(g8):
        return [_sibling_side([g8])] if dist else []

    def to_chips(partial):
        return [_chips_side([partial])] if dist else []

    def unpack(res, n_sides):
        if not dist:
            return res, [None] * n_sides
        *main, side_results = res
        return (main[0] if len(main) == 1 else tuple(main)), [r[0] for r in side_results]

    dwb_out = _matmul(og_b, dh2, "tn", F32, "b_out_dw", **dw_f32_t)
    g8_b_out = jnp.reshape(dwb_out, (N_DEV, cols, d))
    d_ogb, (got,) = unpack(_matmul(dh2, wb_out, "nt", F32, "b_out_dx", sides=to_sibling(g8_b_out), **dx_t), 1)
    p_b_out = _pair_sum(g8_b_out, got, "pair_sum_b_w_out") if dist else None
    doa = _fox_do(d_ogb, o_b, o_lo, ub)
    (dq, dg_q, dukv, dg_k, dfk), (r_b_out,) = unpack(
        _fox_bwd(qb, ka, va, doa, ukv, g_k, ub, g_q, sides=to_chips(p_b_out)), 1)
    dub = _fox_gate_bwd(ub, d_ogb, o_b, dq)
    d_f = jnp.pad(jnp.transpose(dfk[:, 0, :]), ((0, 0), (0, 128 - nh)))
    dufl, db_f = _fgate_bwd(ufl, b_f, d_f)
    dwb_in = _matmul(hb, dub, "tn", F32, "b_in_dw", out_parts=N_DEV, **dw_t)
    d_hb, (got,) = unpack(_matmul(dub, wb_in, "nt", F32, "b_in_dx", sides=to_sibling(dwb_in), **dx_parts_t), 1)
    p_b_in = _pair_sum(dwb_in, got, "pair_sum_b_w_in") if dist else None
    d_hk, (r_b_in,) = unpack(_matmul(dukv, wkv2, "nt", F32, "kv_dx", sides=to_chips(p_b_in), **dx_parts_t), 1)
    d_hk = _matmul(dufl, wfl, "nt", F32, "kv_f_dx", add=d_hk, **dx_t)
    dwkv2 = _matmul(hk, dukv, "tn", F32, "kv_dw", **dw_t)
    dwfl = _matmul(hk, dufl, "tn", F32, "kv_f_dw", **dw_t)
    g8_kv = _kv_grad_blocks(dwkv2, dwfl) if dist else None
    dh1, (dg_kv, dg_b) = _rms_bwd(h1, [kv_norm, b_norm], [d_hk, d_hb], dh2, "rms_kv_b_bwd")
    d_oga, (got,) = unpack(_matmul(dh1, wa_out, "nt", F32, "a_out_dx", sides=to_sibling(g8_kv), **dx_t), 1)
    p_kv = _pair_sum(g8_kv, got, "pair_sum_kv_w") if dist else None
    dwa_out = _matmul(og_a, dh1, "tn", F32, "a_out_dw", **dw_f32_t)
    g8_a_out = jnp.reshape(dwa_out, (N_DEV, cols, d))
    (du4, dlb, dg_aout), (r_kv, got) = unpack(
        _hgrn2_bwd(u4, o_a, d_oga, lb, a_out_norm, sides=to_chips(p_kv) + to_sibling(g8_a_out)), 2)
    p_a_out = _pair_sum(g8_a_out, got, "pair_sum_a_w_out") if dist else None
    dwa_in, (r_a_out,) = unpack(
        _matmul(hn_a, du4, "tn", F32, "a_in_dw", out_parts=N_DEV, sides=to_chips(p_a_out), **dw_t), 1)
    row_tiles = lp // _tile(lp, dx_parts_t["tm"], 64)
    if dist and row_tiles >= 2:
        top = max(1, row_tiles // 6)
        d_hna, ((got,),) = _matmul(du4, wa_in, "nt", F32, "a_in_dx_top", rows=(0, top),
                                   sides=to_sibling(dwa_in), **dx_parts_t)
        p_a_in = _pair_sum(dwa_in, got, "pair_sum_a_w_in")
        d_hna, ((r_a_in,),) = _matmul(du4, wa_in, "nt", F32, "a_in_dx_bottom", rows=(top, row_tiles - top), into=d_hna,
                                      sides=to_chips(p_a_in), **dx_parts_t)
    else:
        d_hna, (got,) = unpack(_matmul(du4, wa_in, "nt", F32, "a_in_dx", sides=to_sibling(dwa_in), **dx_parts_t), 1)
        if dist:
            (r_a_in,) = _alone(_chips_side([_pair_sum(dwa_in, got, "pair_sum_a_w_in")]), "grads_to_chips_a_w_in")
    dh0, (dg_a,) = _rms_bwd(h0, [a_norm], [d_hna], dh1, "rms_a_bwd")
    dgamma = _lb_bwd(gamma, dlb)

    grads = dict(meta=dh0[ROW_PAD:ROW_PAD + N_META], gamma=dgamma, a_norm=dg_a, a_out_norm=dg_aout, kv_norm=dg_kv,
                 b_f=db_f, g_k=dg_k, b_norm=dg_b, g_q=dg_q)
    if dist:
        grads.update(wa_in=r_a_in, wa_out=r_a_out, wkv=r_kv, wb_in=r_b_in, wb_out=r_b_out)
    else:
        grads.update(wa_in=dwa_in, wa_out=dwa_out, wkv2=dwkv2, wfl=dwfl, wb_in=dwb_in, wb_out=dwb_out)
    return loss, dh0[ROW_PAD + N_META:], grads


def kernel(x, meta, gamma_lb, a_norm, a_w_in, a_out_norm, a_w_out, kv_norm, kv_w, fox_b_f, fox_k_norm, b_norm, b_w_in, b_q_norm, b_w_out, loss_target, m_meta, m_gamma_lb, m_a_norm, m_a_w_in, m_a_out_norm, m_a_w_out, m_kv_norm, m_kv_w, m_fox_b_f, m_fox_k_norm, m_b_norm, m_b_w_in, m_b_q_norm, m_b_w_out, v_meta, v_gamma_lb, v_a_norm, v_a_w_in, v_a_out_norm, v_a_w_out, v_kv_norm, v_kv_w, v_fox_b_f, v_fox_k_norm, v_b_norm, v_b_w_in, v_b_q_norm, v_b_w_out):
    d = x.shape[-1]
    nh = d // HEAD
    cols = d // N_DEV
    me = 4 * lax.axis_index("x") + 2 * lax.axis_index("y") + lax.axis_index("c")

    sharded_small = jnp.concatenate([meta, gamma_lb, a_norm, a_out_norm, jnp.zeros((4, cols), F32)], axis=0)
    g_a_in, g_small = _all_gather([a_w_in[0].astype(MXU), sharded_small], "gather_weights")
    wa_out = None
    late = (a_w_out[0].astype(MXU), kv_w.astype(MXU), b_w_in[0].astype(MXU), b_w_out[0].astype(MXU))
    small = jnp.reshape(jnp.transpose(g_small, (1, 0, 2)), (-1, d))
    meta_f, gamma_f, a_norm_f, a_out_norm_f = small[:16], small[16:18], small[18:19], small[19:20]
    b_f = jnp.pad(jnp.reshape(fox_b_f, (1, nh)), ((0, 0), (0, 128 - nh)))
    g_k = jnp.reshape(fox_k_norm, (1, d))
    g_q = jnp.reshape(b_q_norm, (1, d))
    kv_norm_r = jnp.reshape(kv_norm, (1, d))

    loss, grad_x, g = _local_step(x[0], loss_target[0], meta_f, gamma_f, a_norm_f, g_a_in, a_out_norm_f, wa_out,
                                  kv_norm_r, late, b_f, g_k, b_norm, g_q, dist=True)
    loss = lax.psum(loss[0, 0], AXES)

    r_a_in = _adamw(g["wa_in"], a_w_in[0], m_a_w_in[0], v_a_w_in[0], "adamw_a_w_in")
    r_a_out = _adamw(g["wa_out"], a_w_out[0], m_a_w_out[0], v_a_w_out[0], "adamw_a_w_out")
    r_kv = _adamw(g["wkv"], kv_w, m_kv_w, v_kv_w, "adamw_kv_w")
    r_b_in = _adamw(g["wb_in"], b_w_in[0], m_b_w_in[0], v_b_w_in[0], "adamw_b_w_in")
    r_b_out = _adamw(g["wb_out"], b_w_out[0], m_b_w_out[0], v_b_w_out[0], "adamw_b_w_out")

    packed = jnp.concatenate(
        [g["meta"], g["gamma"], g["a_norm"], g["a_out_norm"], g["kv_norm"], g["b_norm"], g["g_k"], g["g_q"],
         jnp.pad(g["b_f"], ((0, 0), (0, d - 128))), jnp.zeros((7, d), F32)], axis=0)
    tot = _all_reduce_small(packed)
    mine = lax.dynamic_slice_in_dim(tot[:20], me * cols, cols, axis=1)
    gs = [mine[:16], mine[16:18], mine[18:19], mine[19:20], tot[20], tot[21:22], jnp.reshape(tot[22], (nh, HEAD)),
          jnp.reshape(tot[23], (1, nh, HEAD)), tot[24, :nh]]
    small_w = [meta, gamma_lb, a_norm, a_out_norm, kv_norm, b_norm, fox_k_norm, b_q_norm, fox_b_f]
    small_m = [m_meta, m_gamma_lb, m_a_norm, m_a_out_norm, m_kv_norm, m_b_norm, m_fox_k_norm, m_b_q_norm, m_fox_b_f]
    small_v = [v_meta, v_gamma_lb, v_a_norm, v_a_out_norm, v_kv_norm, v_b_norm, v_fox_k_norm, v_b_q_norm, v_fox_b_f]

    def pack(arrs):
        rows = [jnp.reshape(a, (-1, cols)) for a in arrs[:8]]
        rows.append(jnp.pad(jnp.reshape(arrs[8], (1, nh)), ((0, 0), (0, cols - nh))))
        n_rows = sum(r.shape[0] for r in rows)
        return jnp.concatenate(rows + [jnp.zeros((-n_rows % 8, cols), F32)], axis=0)

    _, sd, sm, sv = _adamw(pack(gs)[None], pack(small_w), pack(small_m), pack(small_v), "adamw_small")

    def unpack(p):
        out, at = [], 0
        for a in small_w[:8]:
            n = a.size // cols
            out.append(jnp.reshape(p[at:at + n], a.shape))
            at += n
        out.append(p[at, :nh])
        return out

    d_s, m_s, v_s = unpack(sd), unpack(sm), unpack(sv)

    def ordered(s, a_in, a_out, kv, b_in, b_out):
        return [s[0], s[1], s[2], a_in[None], s[3], a_out[None], s[4], kv, s[8], s[6], s[5], b_in[None], s[7], b_out[None]]

    outs = []
    for i, s in enumerate([gs, d_s, m_s, v_s]):
        outs += ordered(s, r_a_in[i], r_a_out[i], r_kv[i], r_b_in[i], r_b_out[i])
    return (loss, grad_x[None], *outs)
```

```python
import math

import jax
import jax.numpy as jnp
from jax import lax
from jax.experimental import pallas as pl
from jax.experimental.pallas import tpu as pltpu

HEAD = 128
CHUNK = 64
N_META = 16
ROW_PAD = 128 - N_META
EPS = 1e-6
MASK_VALUE = -1e30
ADAM_LR = 0.001
ADAM_B1 = 0.9
ADAM_B2 = 0.999
ADAM_EPS = 1e-08
ADAM_WD = 0.01
ADAM_STEP = 10
N_DEV = 8
N_CHIP = 4
MIB = 1024 * 1024
AXES = ("x", "y", "c")
MESH = pl.DeviceIdType.MESH

F32 = jnp.float32
MXU = jnp.bfloat16
PAYLOAD = jnp.bfloat16
HI = lax.Precision.HIGHEST

NN = (((1,), (0,)), ((), ()))
NT = (((1,), (1,)), ((), ()))
TN = (((0,), (0,)), ((), ()))


def _dot(a, b, dims=NN):
    return lax.dot_general(a.astype(MXU), b.astype(MXU), dims, preferred_element_type=F32)


def _dot_exact(a, b):
    return lax.dot_general(a, b, NN, precision=HI, preferred_element_type=F32)


def _sigmoid(x):
    return 1.0 / (1.0 + jnp.exp(-x))


def _tile(dim, target, unit=128):
    best = None
    t = unit
    while t <= min(dim, target):
        if dim % t == 0:
            best = t
        t += unit
    return best if best is not None else dim


def _cparams(semantics, vmem_mib):
    return pltpu.CompilerParams(dimension_semantics=semantics, vmem_limit_bytes=vmem_mib * MIB)


def _mat_spec(arr, br, bc, rc_of_grid):
    if arr.ndim == 2:
        return pl.BlockSpec((br, bc), rc_of_grid)
    assert arr.shape[2] % bc == 0, (arr.shape, bc)
    per = arr.shape[2] // bc

    def idx(*g):
        r, c = rc_of_grid(*g)
        return (c // per, r, c % per)

    return pl.BlockSpec((None, br, bc), idx)


def _mat_shape(arr):
    return (arr.shape[0], arr.shape[1]) if arr.ndim == 2 else (arr.shape[1], arr.shape[0] * arr.shape[2])


def _matmul(a, b, dims, out_dtype, name, *, add=None, out_parts=1, tm=512, tn=512, tk=512, vmem_mib=48, sides=(),
            k_whole=False, rows=None, into=None):
    ar, ac = _mat_shape(a)
    br_, bc_ = _mat_shape(b)
    if dims == "nn":
        m, k, n = ar, ac, bc_
        assert br_ == k
    elif dims == "nt":
        m, k, n = ar, ac, br_
        assert bc_ == k
    else:
        m, k, n = ac, ar, bc_
        assert br_ == k
    m_unit, n_unit, k_unit = m, n, k
    if a.ndim == 3:
        if dims == "tn":
            m_unit = math.gcd(m_unit, a.shape[2])
        else:
            k_unit = math.gcd(k_unit, a.shape[2])
    if b.ndim == 3:
        if dims == "nt":
            k_unit = math.gcd(k_unit, b.shape[2])
        else:
            n_unit = math.gcd(n_unit, b.shape[2])
    if out_parts > 1:
        n_unit = math.gcd(n_unit, n // out_parts)
    tm = _tile(m_unit, tm, 128 if dims == "tn" else 64)
    tn, tk = _tile(n_unit, tn), _tile(k_unit, tk)
    whole_k = dims == "nt" and tk < k and k_whole
    if whole_k:
        k_chunk, tk = tk, k
    gm, gn, gk = m // tm, n // tn, k // tk
    assert gm * tm == m and gn * tn == n and gk * tk == k, (name, m, n, k, tm, tn, tk)

    def chunk_of(ref, arr, c):
        if arr.ndim == 2:
            return ref[:, c * k_chunk:(c + 1) * k_chunk]
        per = arr.shape[2] // k_chunk
        return ref[c // per, :, (c % per) * k_chunk:(c % per + 1) * k_chunk]

    def all_cols(arr, rows, row_of_grid):
        if arr.ndim == 2:
            return pl.BlockSpec((rows, arr.shape[1]), lambda i, j, kk: (row_of_grid(i, j), 0))
        return pl.BlockSpec((arr.shape[0], rows, arr.shape[2]), lambda i, j, kk: (0, row_of_grid(i, j), 0))

    if dims == "nn":
        a_spec = _mat_spec(a, tm, tk, lambda i, j, kk: (i, kk))
        b_spec = _mat_spec(b, tk, tn, lambda i, j, kk: (kk, j))
        dn = NN
    elif whole_k:
        a_spec = all_cols(a, tm, lambda i, j: i)
        b_spec = all_cols(b, tn, lambda i, j: j)
        dn = NT
    elif dims == "nt":
        a_spec = _mat_spec(a, tm, tk, lambda i, j, kk: (i, kk))
        b_spec = _mat_spec(b, tn, tk, lambda i, j, kk: (j, kk))
        dn = NT
    else:
        a_spec = _mat_spec(a, tk, tm, lambda i, j, kk: (kk, i))
        b_spec = _mat_spec(b, tk, tn, lambda i, j, kk: (kk, j))
        dn = TN

    if out_parts > 1:
        per = (n // out_parts) // tn
        out_shape = jax.ShapeDtypeStruct((out_parts, m, n // out_parts), out_dtype)
        o_spec = pl.BlockSpec((None, tm, tn), lambda i, j, kk: (j // per, i, j % per))
    else:
        out_shape = jax.ShapeDtypeStruct((m, n), out_dtype)
        o_spec = pl.BlockSpec((tm, tn), lambda i, j, kk: (i, j))

    in_specs = [a_spec, b_spec]
    args = [a, b]
    if add is not None:
        in_specs.append(pl.BlockSpec((tm, tn), lambda i, j, kk: (i, j)))
        args.append(add)
    if rows is not None:
        first_tile, gm = rows

        def shifted(spec):
            return pl.BlockSpec(spec.block_shape, lambda i, j, kk: spec.index_map(i + first_tile, j, kk))

        in_specs = [shifted(s) for s in in_specs]
        o_spec = shifted(o_spec)
    aliases = {}
    if into is not None:
        aliases[len(args)] = 0
        in_specs.append(_any_spec())
        args.append(into)

    def body(*refs):
        a_ref, b_ref, *others, o_ref, acc_ref = refs
        add_ref = others[0] if add is not None else None
        kk = pl.program_id(2)
        if whole_k:
            part = sum(lax.dot_general(chunk_of(a_ref, a, c).astype(MXU), chunk_of(b_ref, b, c).astype(MXU), dn,
                                       preferred_element_type=F32) for c in range(k // k_chunk))
        else:
            part = lax.dot_general(a_ref[...].astype(MXU), b_ref[...].astype(MXU), dn, preferred_element_type=F32)

        def finish(total):
            if add_ref is not None:
                total = total + add_ref[...]
            o_ref[...] = total.astype(o_ref.dtype)

        if gk == 1:
            finish(part)
        else:
            @pl.when(kk == 0)
            def _():
                acc_ref[...] = part

            @pl.when(jnp.logical_and(kk > 0, kk < gk - 1))
            def _():
                acc_ref[...] += part

            @pl.when(kk == gk - 1)
            def _():
                finish(acc_ref[...] + part)

    (out,), side_results = _pallas(
        body,
        name=name,
        args=args,
        grid=(gm, gn, gk),
        in_specs=in_specs,
        out_specs=[o_spec],
        out_shape=[out_shape],
        scratch_shapes=[pltpu.VMEM((tm, tn) if gk > 1 else (8, 128), F32)],
        semantics=("parallel", "parallel", "arbitrary"),
        vmem_mib=vmem_mib,
        sides=sides,
        aliases=aliases,
    )
    return (out, side_results) if sides else out


def _rms_fwd(h, gains, name, tm=384):
    lp, d = h.shape
    tm = _tile(lp, tm)
    n = len(gains)

    def body(*refs):
        h_ref = refs[0]
        g_refs = refs[1:1 + n]
        o_refs = refs[1 + n:]
        x = h_ref[...]
        y = x * lax.rsqrt(jnp.mean(x * x, axis=-1, keepdims=True) + EPS)
        for g_ref, o_ref in zip(g_refs, o_refs):
            o_ref[...] = (y * g_ref[...]).astype(o_ref.dtype)

    row = pl.BlockSpec((tm, d), lambda i: (i, 0))
    vec = pl.BlockSpec((1, d), lambda i: (0, 0))
    return pl.pallas_call(
        body,
        name=name,
        grid=(lp // tm,),
        in_specs=[row] + [vec] * n,
        out_specs=[row] * n,
        out_shape=[jax.ShapeDtypeStruct((lp, d), MXU)] * n,
        compiler_params=_cparams(("parallel",), 40),
    )(h, *gains)


def _rms_bwd(h, gains, dys, res, name, tm=384):
    lp, d = h.shape
    tm = _tile(lp, tm)
    n = len(gains)

    def body(*refs):
        h_ref, res_ref = refs[0], refs[1]
        g_refs = refs[2:2 + n]
        dy_refs = refs[2 + n:2 + 2 * n]
        dh_ref = refs[2 + 2 * n]
        dg_refs = refs[3 + 2 * n:]
        i = pl.program_id(0)
        x = h_ref[...]
        rstd = lax.rsqrt(jnp.mean(x * x, axis=-1, keepdims=True) + EPS)
        xhat = x * rstd
        dh = res_ref[...]
        for g_ref, dy_ref, dg_ref in zip(g_refs, dy_refs, dg_refs):
            dy = dy_ref[...]
            gdy = dy * g_ref[...]
            dh = dh + rstd * (gdy - xhat * jnp.mean(gdy * xhat, axis=-1, keepdims=True))
            part = jnp.sum(dy * xhat, axis=0, keepdims=True)

            @pl.when(i == 0)
            def _():
                dg_ref[...] = part

            @pl.when(i > 0)
            def _():
                dg_ref[...] += part

        dh_ref[...] = dh

    row = pl.BlockSpec((tm, d), lambda i: (i, 0))
    vec = pl.BlockSpec((1, d), lambda i: (0, 0))
    outs = pl.pallas_call(
        body,
        name=name,
        grid=(lp // tm,),
        in_specs=[row, row] + [vec] * n + [row] * n,
        out_specs=[row] + [vec] * n,
        out_shape=[jax.ShapeDtypeStruct((lp, d), F32)] + [jax.ShapeDtypeStruct((1, d), F32)] * n,
        compiler_params=_cparams(("arbitrary",), 56),
    )(h, res, *gains, *dys)
    return outs[0], list(outs[1:])


def _lb_fwd(gamma):
    def body(g_ref, lb_ref):
        g = g_ref[...]
        e = jnp.exp(g - jnp.max(g, axis=0, keepdims=True))
        lb_ref[...] = (e / jnp.sum(e, axis=0, keepdims=True))[0:1, :]

    return pl.pallas_call(body, name="lb_fwd", out_shape=jax.ShapeDtypeStruct((1, gamma.shape[1]), F32))(gamma)


def _lb_bwd(gamma, dlb):
    def body(g_ref, dlb_ref, dg_ref):
        g = g_ref[...]
        e = jnp.exp(g - jnp.max(g, axis=0, keepdims=True))
        s = e / jnp.sum(e, axis=0, keepdims=True)
        rows = lax.broadcasted_iota(jnp.int32, g.shape, 0)
        ds = jnp.where(rows == 0, dlb_ref[...], 0.0)
        dg_ref[...] = s * (ds - jnp.sum(s * ds, axis=0, keepdims=True))

    return pl.pallas_call(body, name="lb_bwd", out_shape=jax.ShapeDtypeStruct(gamma.shape, F32))(gamma, dlb)


def _tri(n, lower):
    r = lax.broadcasted_iota(jnp.int32, (n, n), 0)
    c = lax.broadcasted_iota(jnp.int32, (n, n), 1)
    return jnp.where((r >= c) if lower else (r <= c), 1.0, 0.0).astype(F32)


def _group(nc, most):
    return max(u for u in range(1, most + 1) if nc % u == 0)


def _running_sum(tri, x):
    hi = x.astype(MXU)
    rest = x - hi.astype(F32)
    mid = rest.astype(MXU)
    lo = (rest - mid.astype(F32)).astype(MXU)
    return _dot(tri, hi) + _dot(tri, mid) + _dot(tri, lo)


def _causal(n):
    r = lax.broadcasted_iota(jnp.int32, (n, n), 0)
    c = lax.broadcasted_iota(jnp.int32, (n, n), 1)
    return r >= c


def _chunk_gates(u_ref, lb, c):
    sl = pl.ds(pl.multiple_of(c * CHUNK, CHUNK), CHUNK)
    valid = (c * CHUNK + lax.broadcasted_iota(jnp.int32, (CHUNK, HEAD), 0)) >= ROW_PAD
    uq = u_ref[0, sl, :]
    uf = u_ref[1, sl, :]
    sq = _sigmoid(uq)
    sf = _sigmoid(uf)
    fg = lb + (1.0 - lb) * sf
    return dict(sl=sl, valid=valid, uq=uq, sq=sq, sf=sf, fg=fg, q=jnp.where(valid, uq * sq, 0.0),
                logf=jnp.where(valid, jnp.log(fg), 0.0), k=jnp.where(valid, 1.0 - fg, 0.0),
                v=jnp.where(valid, u_ref[2, sl, :], 0.0))


def _chunk_decays(x, b):
    b_last = b[CHUNK - 1:CHUNK, :]
    b_mid = b[CHUNK // 2 - 1:CHUNK // 2, :]
    e_qi = jnp.exp(b - b_mid)
    e_ki = jnp.exp(b_mid - b)
    e_kd = jnp.exp(b_last - b)
    e_qe = jnp.exp(b)
    q, k = x["q"], x["k"]
    return dict(x, e_qi=e_qi, e_ki=e_ki, e_kd=e_kd, e_qe=e_qe, qi=q * e_qi, ki=k * e_ki, kd=k * e_kd, qe=q * e_qe,
                decay=jnp.exp(b_last))


def _chunks(u_ref, lb, tri_lower, cs):
    gates = [_chunk_gates(u_ref, lb, c) for c in cs]
    sums = [_running_sum(tri_lower, x["logf"]) for x in gates]
    return [_chunk_decays(x, b) for x, b in zip(gates, sums)]


def _hgrn2_fwd(u4, lb, g_out, name="hgrn2_fwd", sides=()):
    _, lp, d = u4.shape
    nh, nc = d // HEAD, lp // CHUNK
    per = _group(nc, 22)

    def body(u_ref, lb_ref, g_ref, o_ref, og_ref):
        lb_v = lb_ref[...]
        g = g_ref[...]

        tri_lower = _tri(CHUNK, True).astype(MXU)
        causal = _causal(CHUNK)

        def step(i, st):
            xs = _chunks(u_ref, lb_v, tri_lower, [i * per + u for u in range(per)])
            scores = [_dot(x["qi"], x["ki"], NT) for x in xs]
            updates = [_dot(x["v"], x["kd"], TN) for x in xs]
            states = []
            for x, upd in zip(xs, updates):
                states.append(st)
                st = x["decay"] * st + upd
            outs = [_dot(jnp.where(causal, a, 0.0), x["v"]) + _dot(x["qe"], s, NT)
                    for x, a, s in zip(xs, scores, states)]
            for x, o in zip(xs, outs):
                o_ref[x["sl"], :] = o
                on = o * lax.rsqrt(jnp.mean(o * o, axis=-1, keepdims=True) + EPS) * g
                z = u_ref[3, x["sl"], :]
                og_ref[x["sl"], :] = (on * (z * _sigmoid(z))).astype(og_ref.dtype)
            return st

        lax.fori_loop(0, nc // per, step, jnp.zeros((HEAD, HEAD), F32))

    slab = pl.BlockSpec((lp, HEAD), lambda h: (0, h))
    vec = pl.BlockSpec((1, HEAD), lambda h: (0, h))
    outs, side_results = _pallas(
        body,
        name=name,
        args=(u4, lb, g_out),
        grid=(nh,),
        in_specs=[pl.BlockSpec((4, lp, HEAD), lambda h: (0, 0, h)), vec, vec],
        out_specs=[slab, slab],
        out_shape=[jax.ShapeDtypeStruct((lp, d), F32), jax.ShapeDtypeStruct((lp, d), MXU)],
        semantics=("parallel",),
        vmem_mib=48,
        sides=sides,
    )
    return (*outs, side_results) if sides else tuple(outs)


def _hgrn2_bwd(u4, o, d_og, lb, g_out, name="hgrn2_bwd", sides=()):
    _, lp, d = u4.shape
    nh, nc = d // HEAD, lp // CHUNK
    per = _group(nc, 11)

    def body(u_ref, o_ref, dog_ref, lb_ref, g_ref, du_ref, dlb_ref, dg_ref, st_ref, do_ref):
        lb_v = lb_ref[...]
        g = g_ref[...]

        tri_lower = _tri(CHUNK, True).astype(MXU)
        tri_upper = _tri(CHUNK, False).astype(MXU)
        causal = _causal(CHUNK)

        def fwd_step(i, carry):
            st, dg_acc = carry
            cs = [i * per + u for u in range(per)]
            xs = _chunks(u_ref, lb_v, tri_lower, cs)
            updates = [_dot(x["v"], x["kd"], TN) for x in xs]
            for c, x, upd in zip(cs, xs, updates):
                st_ref[c] = st
                st = x["decay"] * st + upd
            for x in xs:
                sl = x["sl"]
                ov = o_ref[sl, :]
                rstd = lax.rsqrt(jnp.mean(ov * ov, axis=-1, keepdims=True) + EPS)
                on = ov * rstd
                z = u_ref[3, sl, :]
                sz = _sigmoid(z)
                dog = dog_ref[sl, :]
                dy = dog * (z * sz)
                dz = dog * (on * g) * (sz * (1.0 + z * (1.0 - sz)))
                du_ref[3, sl, :] = dz.astype(du_ref.dtype)
                gdy = dy * g
                do = rstd * (gdy - on * jnp.mean(gdy * on, axis=-1, keepdims=True))
                do_ref[sl, :] = jnp.where(x["valid"], do, 0.0)
                dg_acc = dg_acc + jnp.sum(dy * on, axis=0, keepdims=True)
            return st, dg_acc

        _, dg_tot = lax.fori_loop(0, nc // per, fwd_step, (jnp.zeros((HEAD, HEAD), F32), jnp.zeros((1, HEAD), F32)))
        dg_ref[...] = dg_tot

        def bwd_step(i, carry):
            gt, dlb_acc = carry
            cs = [nc - 1 - (i * per + u) for u in range(per)]
            xs = _chunks(u_ref, lb_v, tri_lower, cs)
            dos = [do_ref[x["sl"], :] for x in xs]
            sts = [st_ref[c] for c in cs]
            scores = [jnp.where(causal, _dot(x["qi"], x["ki"], NT), 0.0) for x in xs]
            d_scores = [jnp.where(causal, _dot(do, x["v"], NT), 0.0) for x, do in zip(xs, dos)]
            d_qes = [_dot(do, st) for do, st in zip(dos, sts)]
            g_updates = [_dot(do, x["qe"], TN) for x, do in zip(xs, dos)]
            gts = []
            for x, upd in zip(xs, g_updates):
                gts.append(gt)
                gt = x["decay"] * gt + upd
            d_kds = [_dot(x["v"], g_) for x, g_ in zip(xs, gts)]
            dvs = [_dot(x["kd"], g_, NT) + _dot(a, do, TN) for x, g_, a, do in zip(xs, gts, scores, dos)]
            d_qis = [_dot(d_a, x["ki"]) for x, d_a in zip(xs, d_scores)]
            d_kis = [_dot(d_a, x["qi"], TN) for x, d_a in zip(xs, d_scores)]
            rows = lax.broadcasted_iota(jnp.int32, (CHUNK, HEAD), 0)
            dbs = []
            for x, g_, st, d_qi, d_ki, d_qe, d_kd in zip(xs, gts, sts, d_qis, d_kis, d_qes, d_kds):
                t_qi, t_ki, t_qe, t_kd = d_qi * x["qi"], d_ki * x["ki"], d_qe * x["qe"], d_kd * x["kd"]
                d_decay = jnp.sum(g_ * st, axis=0, keepdims=True)
                d_mid = jnp.sum(t_ki - t_qi, axis=0, keepdims=True)
                d_last = jnp.sum(t_kd, axis=0, keepdims=True) + d_decay * x["decay"]
                dbs.append(t_qi - t_ki + t_qe - t_kd + jnp.where(rows == CHUNK // 2 - 1, d_mid, 0.0)
                           + jnp.where(rows == CHUNK - 1, d_last, 0.0))
            dlogfs = [_running_sum(tri_upper, db) for db in dbs]
            for x, dlogf, dv, d_qi, d_ki, d_qe, d_kd in zip(xs, dlogfs, dvs, d_qis, d_kis, d_qes, d_kds):
                sl = x["sl"]
                dq = d_qi * x["e_qi"] + d_qe * x["e_qe"]
                dk = d_ki * x["e_ki"] + d_kd * x["e_kd"]
                valid, sq, sf, uq = x["valid"], x["sq"], x["sf"], x["uq"]
                dfg = jnp.where(valid, dlogf / x["fg"] - dk, 0.0)
                du_ref[0, sl, :] = jnp.where(valid, dq * (sq * (1.0 + uq * (1.0 - sq))), 0.0).astype(du_ref.dtype)
                du_ref[1, sl, :] = (dfg * (1.0 - lb_v) * (sf * (1.0 - sf))).astype(du_ref.dtype)
                du_ref[2, sl, :] = jnp.where(valid, dv, 0.0).astype(du_ref.dtype)
                dlb_acc = dlb_acc + jnp.sum(dfg * (1.0 - sf), axis=0, keepdims=True)
            return gt, dlb_acc

        _, dlb_tot = lax.fori_loop(0, nc // per, bwd_step, (jnp.zeros((HEAD, HEAD), F32), jnp.zeros((1, HEAD), F32)))
        dlb_ref[...] = dlb_tot

    slab = pl.BlockSpec((lp, HEAD), lambda h: (0, h))
    vec = pl.BlockSpec((1, HEAD), lambda h: (0, h))
    quad = pl.BlockSpec((4, lp, HEAD), lambda h: (0, 0, h))
    outs, side_results = _pallas(
        body,
        name=name,
        args=(u4, o, d_og, lb, g_out),
        grid=(nh,),
        in_specs=[quad, slab, slab, vec, vec],
        out_specs=[quad, vec, vec],
        out_shape=[jax.ShapeDtypeStruct((4, lp, d), MXU), jax.ShapeDtypeStruct((1, d), F32),
                   jax.ShapeDtypeStruct((1, d), F32)],
        scratch_shapes=[pltpu.VMEM((nc, HEAD, HEAD), F32), pltpu.VMEM((lp, HEAD), F32)],
        semantics=("parallel",),
        vmem_mib=58,
        sides=sides,
    )
    return (*outs, side_results) if sides else tuple(outs)


WIDE = 2 * HEAD
INV_SCALE = HEAD ** 0.5


def _split3(x):
    hi = x.astype(MXU).astype(F32)
    rest = x - hi
    mid = rest.astype(MXU).astype(F32)
    return hi, mid, (rest - mid).astype(MXU).astype(F32)


def _extra_cols(rows, first, second):
    lane = lax.broadcasted_iota(jnp.int32, (rows, HEAD), 1)
    out = jnp.where(lane < 6, 1.0, 0.0).astype(F32)
    for base, terms in ((0, first), (3, second)):
        if terms is not None:
            for j, term in enumerate(terms):
                out = jnp.where(lane == base + j, term, out)
    return out


def _head_col(a, h):
    lane = lax.broadcasted_iota(jnp.int32, a.shape, 1)
    return jnp.sum(jnp.where(lane == h, a, 0.0), axis=-1, keepdims=True)


def _attn_operands(ub, ukv, g_q, g_k, f_cum, name="attn_operands", tm=384):
    _, lp, d = ub.shape
    tm = _tile(lp, tm)
    nh = d // HEAD

    def body(q_ref, k_ref, v_ref, gq_ref, gk_ref, f_ref, qa_ref, ka_ref, va_ref):
        i = pl.program_id(0)
        f = f_ref[...]
        is_pad = (i * tm + lax.broadcasted_iota(jnp.int32, (tm, 1), 0)) < ROW_PAD
        ones_only = _extra_cols(tm, None, (0.0, 0.0, 0.0)).astype(MXU)
        for h in range(nh):
            hs = slice(h * HEAD, (h + 1) * HEAD)
            lo, hi = h * WIDE, h * WIDE + HEAD
            f_h = _head_col(f, h)
            for x_ref, g_ref, o_ref in ((q_ref, gq_ref, qa_ref), (k_ref, gk_ref, ka_ref)):
                x = x_ref[:, hs]
                y = x * lax.rsqrt(jnp.mean(x * x, axis=-1, keepdims=True) + EPS)
                o_ref[:, lo:hi] = (y * g_ref[:, hs]).astype(o_ref.dtype)
            qa_ref[:, hi:hi + HEAD] = _extra_cols(tm, _split3(f_h * INV_SCALE), None).astype(MXU)
            f_key = jnp.where(is_pad, -MASK_VALUE, f_h)
            ka_ref[:, hi:hi + HEAD] = _extra_cols(tm, None, _split3(-f_key * INV_SCALE)).astype(MXU)
            va_ref[:, lo:hi] = v_ref[:, hs].astype(MXU)
            va_ref[:, hi:hi + HEAD] = ones_only

    wide = pl.BlockSpec((tm, nh * WIDE), lambda i: (i, 0))
    vec = pl.BlockSpec((1, d), lambda i: (0, 0))
    return pl.pallas_call(
        body,
        name=name,
        grid=(lp // tm,),
        in_specs=[pl.BlockSpec((None, tm, d), lambda i: (0, i, 0)), pl.BlockSpec((None, tm, d), lambda i: (0, i, 0)),
                  pl.BlockSpec((None, tm, d), lambda i: (1, i, 0)), vec, vec, pl.BlockSpec((tm, 128), lambda i: (i, 0))],
        out_specs=[wide, wide, wide],
        out_shape=[jax.ShapeDtypeStruct((lp, nh * WIDE), MXU)] * 3,
        compiler_params=_cparams(("parallel",), 56),
    )(ub, ukv, ukv, g_q, g_k, f_cum)


def _head_rms_bwd_tile(x, g, dy):
    rstd = lax.rsqrt(jnp.mean(x * x, axis=-1, keepdims=True) + EPS)
    xhat = x * rstd
    gdy = dy * g
    dx = rstd * (gdy - xhat * jnp.mean(gdy * xhat, axis=-1, keepdims=True))
    return dx, jnp.sum(dy * xhat, axis=0, keepdims=True)


def _fgate_fwd(ufl, b_f):
    lp = ufl.shape[0]
    nb = lp // 128

    def body(u_ref, b_ref, f_ref):
        def step(i, carry):
            sl = pl.ds(pl.multiple_of(i * 128, 128), 128)
            valid = (i * 128 + lax.broadcasted_iota(jnp.int32, (128, 128), 0)) >= ROW_PAD
            x = u_ref[sl, :] + b_ref[...]
            logf = jnp.where(valid, jnp.minimum(x, 0.0) - jnp.log(1.0 + jnp.exp(-jnp.abs(x))), 0.0)
            f = _dot_exact(_tri(128, True), logf) + carry
            f_ref[sl, :] = f
            return f[127:128, :]

        lax.fori_loop(0, nb, step, jnp.zeros((1, 128), F32))

    return pl.pallas_call(body, name="fgate_fwd", out_shape=jax.ShapeDtypeStruct((lp, 128), F32))(ufl, b_f)


def _fgate_bwd(ufl, b_f, d_f):
    lp = ufl.shape[0]
    nb = lp // 128

    def body(u_ref, b_ref, df_ref, du_ref, db_ref):
        def step(j, carry):
            later, db_acc = carry
            i = nb - 1 - j
            sl = pl.ds(pl.multiple_of(i * 128, 128), 128)
            valid = (i * 128 + lax.broadcasted_iota(jnp.int32, (128, 128), 0)) >= ROW_PAD
            x = u_ref[sl, :] + b_ref[...]
            df = df_ref[sl, :]
            dlogf = _dot_exact(_tri(128, False), df) + later
            dx = jnp.where(valid, dlogf * _sigmoid(-x), 0.0)
            du_ref[sl, :] = dx.astype(du_ref.dtype)
            return later + jnp.sum(df, axis=0, keepdims=True), db_acc + jnp.sum(dx, axis=0, keepdims=True)

        _, db_tot = lax.fori_loop(0, nb, step, (jnp.zeros((1, 128), F32), jnp.zeros((1, 128), F32)))
        db_ref[...] = db_tot

    return pl.pallas_call(
        body, name="fgate_bwd",
        out_shape=[jax.ShapeDtypeStruct((lp, 128), MXU), jax.ShapeDtypeStruct((1, 128), F32)],
    )(ufl, b_f, d_f)


STRIP = 32
PAIR = 2


def _strip_causal(r, t):
    row = r + lax.broadcasted_iota(jnp.int32, (STRIP, t), 0)
    col = lax.broadcasted_iota(jnp.int32, (STRIP, t), 1)
    return col <= row


def _fox_fwd(qa, ka, va, ub, name="fox_fwd", t=384):
    lp = qa.shape[0]
    d = ub.shape[2]
    t = _tile(lp, t)
    nh, nq = d // HEAD, lp // t
    scale = HEAD ** -0.5

    assert nh % PAIR == 0
    heads = range(PAIR)

    def body(q_ref, k_ref, v_ref, z_ref, o_ref, olo_ref, og_ref, qb_ref,
             s_ref, p_ref, m_ref, a_ref, l_ref, acc_ref):
        qb = pl.program_id(1)
        m_ref[...] = jnp.full((PAIR, t, 1), MASK_VALUE, F32)
        l_ref[...] = jnp.zeros((PAIR, t, 128), F32)
        acc_ref[...] = jnp.zeros((PAIR, 2 * t, HEAD), F32)
        p_ref[1] = jnp.zeros((PAIR, 2 * t, t), MXU)
        a_ref[1] = jnp.ones((PAIR, t, 1), F32)

        def scores(kb, buf):
            ks = pl.ds(pl.multiple_of(kb * t, t), t)
            for j in heads:
                ws = slice(j * WIDE, (j + 1) * WIDE)
                s_ref[buf, j] = _dot(q_ref[:, ws], k_ref[ks, ws], NT)

        def weighted_sum(kb, buf):
            ks = pl.ds(pl.multiple_of(kb * t, t), t)
            for j in heads:
                pv = _dot(p_ref[buf, j], v_ref[ks, j * WIDE:j * WIDE + HEAD])
                alpha = a_ref[buf, j]
                acc_ref[j, 0:t, :] = alpha * acc_ref[j, 0:t, :] + pv[0:t]
                acc_ref[j, t:2 * t, :] = alpha * acc_ref[j, t:2 * t, :] + pv[t:2 * t]

        def softmax_update(buf, diagonal):
            for j in heads:
                for r in range(0, t, STRIP):
                    rs = slice(r, r + STRIP)
                    x = s_ref[buf, j, rs, :] * scale
                    if diagonal:
                        x = jnp.where(_strip_causal(r, t), x, MASK_VALUE)
                    m_old = m_ref[j, rs, :]
                    m_new = jnp.maximum(m_old, jnp.max(x, axis=-1, keepdims=True))
                    alpha = jnp.exp(m_old - m_new)
                    p = jnp.exp(x - m_new)
                    m_ref[j, rs, :] = m_new
                    a_ref[buf, j, rs, :] = alpha
                    l_ref[j, rs, :] = alpha * l_ref[j, rs, :] + sum(p[:, c:c + 128] for c in range(0, t, 128))
                    p_hi = p.astype(MXU)
                    p_ref[buf, j, rs, :] = p_hi
                    p_ref[buf, j, t + r:t + r + STRIP, :] = (p - p_hi.astype(F32)).astype(MXU)

        def off_diagonal(kb, cur):
            weighted_sum(jnp.maximum(kb - 1, 0), 1 - cur)
            scores(kb + 1, 1 - cur)
            softmax_update(cur, False)

        def diagonal(cur):
            weighted_sum(jnp.maximum(qb - 1, 0), 1 - cur)
            softmax_update(cur, True)
            weighted_sum(qb, cur)

        def two_blocks(i, carry):
            off_diagonal(2 * i, 0)
            off_diagonal(2 * i + 1, 1)
            return carry

        scores(0, 0)
        lax.fori_loop(0, qb // 2, two_blocks, 0)

        @pl.when(lax.rem(qb, 2) == 0)
        def _():
            diagonal(0)

        @pl.when(lax.rem(qb, 2) == 1)
        def _():
            off_diagonal(qb - 1, 0)
            diagonal(1)

        is_pad = (qb * t + lax.broadcasted_iota(jnp.int32, (t, 1), 0)) < ROW_PAD
        for j in heads:
            hs = slice(j * HEAD, (j + 1) * HEAD)
            l = jnp.sum(l_ref[j], axis=-1, keepdims=True)
            o = acc_ref[j, 0:t, :] / l
            o_ref[:, hs] = o
            olo_ref[:, hs] = acc_ref[j, t:2 * t, :] / l
            z = z_ref[:, hs]
            og_ref[:, hs] = (o * (z * _sigmoid(z))).astype(og_ref.dtype)
            extra = q_ref[:, j * WIDE + HEAD:(j + 1) * WIDE].astype(F32)
            f_scaled = extra[:, 0:1] + extra[:, 1:2] + extra[:, 2:3]
            log_term = jnp.where(is_pad, MASK_VALUE * INV_SCALE, f_scaled - (m_ref[j] + jnp.log(l)) * INV_SCALE)
            qb_ref[:, j * WIDE:j * WIDE + HEAD] = q_ref[:, j * WIDE:j * WIDE + HEAD]
            qb_ref[:, j * WIDE + HEAD:(j + 1) * WIDE] = _extra_cols(t, _split3(log_term), None).astype(qb_ref.dtype)

    scratch = [pltpu.VMEM((2, PAIR, t, t), F32), pltpu.VMEM((2, PAIR, 2 * t, t), MXU), pltpu.VMEM((PAIR, t, 1), F32),
               pltpu.VMEM((2, PAIR, t, 1), F32), pltpu.VMEM((PAIR, t, 128), F32), pltpu.VMEM((PAIR, 2 * t, HEAD), F32)]
    tile = pl.BlockSpec((t, PAIR * HEAD), lambda g, i: (i, g))
    wide_tile = pl.BlockSpec((t, PAIR * WIDE), lambda g, i: (i, g))
    wide_all = pl.BlockSpec((lp, PAIR * WIDE), lambda g, i: (0, g))
    return pl.pallas_call(
        body,
        name=name,
        grid=(nh // PAIR, nq),
        in_specs=[wide_tile, wide_all, wide_all, pl.BlockSpec((None, t, PAIR * HEAD), lambda g, i: (1, i, g))],
        out_specs=[tile, tile, tile, wide_tile],
        out_shape=[jax.ShapeDtypeStruct((lp, d), F32), jax.ShapeDtypeStruct((lp, d), F32),
                   jax.ShapeDtypeStruct((lp, d), MXU), jax.ShapeDtypeStruct((lp, nh * WIDE), MXU)],
        scratch_shapes=scratch,
        compiler_params=_cparams(("parallel", "arbitrary"), 48),
    )(qa, ka, va, ub)


def _fox_bwd(qb, ka, va, doa, ukv, g_k, ub, g_q, name="fox_bwd", t=384, sides=()):
    lp = qb.shape[0]
    d = ukv.shape[2]
    t = _tile(lp, t)
    nh, nk = d // HEAD, lp // t
    scale = HEAD ** -0.5

    assert nh % PAIR == 0
    heads = range(PAIR)

    def body(q_ref, do_ref, k_ref, v_ref, kraw_ref, gk_ref, qraw_ref, gq_ref, dqraw_ref, dgq_ref, dukv_ref, dgk_ref,
             dfk_ref, s_ref, dp_ref, p_ref, ds_ref, col_ref, dk_ref, dv_ref, dq_ref):
        kb = pl.program_id(1)

        @pl.when(kb == 0)
        def _():
            dq_ref[...] = jnp.zeros_like(dq_ref)
            dgk_ref[...] = jnp.zeros_like(dgk_ref)

        dk_ref[...] = jnp.zeros_like(dk_ref)
        dv_ref[...] = jnp.zeros_like(dv_ref)
        col_ref[...] = jnp.zeros_like(col_ref)

        def step(qb, diagonal):
            qs = pl.ds(pl.multiple_of(qb * t, t), t)
            for j in heads:
                ws = slice(j * WIDE, (j + 1) * WIDE)
                s_ref[j] = _dot(q_ref[qs, ws], k_ref[:, ws], NT)
                dp_ref[j] = _dot(do_ref[qs, ws], v_ref[:, ws], NT)
            for j in heads:
                for r in range(0, t, STRIP):
                    rs = slice(r, r + STRIP)
                    x = s_ref[j, rs, :] * scale
                    if diagonal:
                        x = jnp.where(_strip_causal(r, t), x, MASK_VALUE)
                    p = jnp.exp(x)
                    ds = p * dp_ref[j, rs, :]
                    p_ref[j, rs, :] = p.astype(MXU)
                    ds_ref[j, rs, :] = (ds * scale).astype(MXU)
                    col_ref[j] += ds
            for j in heads:
                hs = slice(j * HEAD, (j + 1) * HEAD)
                narrow = slice(j * WIDE, j * WIDE + HEAD)
                dsb = ds_ref[j]
                dv_ref[:, hs] += _dot(p_ref[j], do_ref[qs, narrow], TN)
                dq_ref[qs, hs] += _dot(dsb, k_ref[:, narrow])
                dk_ref[:, hs] += _dot(dsb, q_ref[qs, narrow], TN)

        def off_diagonal(qb, carry):
            step(qb, False)
            return carry

        step(kb, True)
        lax.fori_loop(kb + 1, nk, off_diagonal, 0)
        for j in heads:
            hs = slice(j * HEAD, (j + 1) * HEAD)
            dfk_ref[j] = -jnp.sum(col_ref[j], axis=0, keepdims=True)
            dx, dg = _head_rms_bwd_tile(kraw_ref[:, hs], gk_ref[:, hs], dk_ref[:, hs])
            dukv_ref[0, :, hs] = dx.astype(dukv_ref.dtype)
            dukv_ref[1, :, hs] = dv_ref[:, hs].astype(dukv_ref.dtype)
            dgk_ref[:, hs] += dg

        @pl.when(kb == nk - 1)
        def _():
            dgq_ref[...] = jnp.zeros_like(dgq_ref)

            def rows(c, carry):
                rs = pl.ds(pl.multiple_of(c * t, t), t)
                for j in heads:
                    hs = slice(j * HEAD, (j + 1) * HEAD)
                    dx, dg = _head_rms_bwd_tile(qraw_ref[rs, hs], gq_ref[:, hs], dq_ref[rs, hs])
                    dqraw_ref[rs, hs] = dx.astype(dqraw_ref.dtype)
                    dgq_ref[:, hs] += dg
                return carry

            lax.fori_loop(0, nk, rows, 0)

    scratch = [pltpu.VMEM((PAIR, t, t), F32), pltpu.VMEM((PAIR, t, t), F32), pltpu.VMEM((PAIR, t, t), MXU),
               pltpu.VMEM((PAIR, t, t), MXU), pltpu.VMEM((PAIR, STRIP, t), F32),
               pltpu.VMEM((t, PAIR * HEAD), F32), pltpu.VMEM((t, PAIR * HEAD), F32), pltpu.VMEM((lp, PAIR * HEAD), F32)]
    whole = pl.BlockSpec((lp, PAIR * HEAD), lambda g, j: (0, g))
    wide_all = pl.BlockSpec((lp, PAIR * WIDE), lambda g, j: (0, g))
    wide_tile = pl.BlockSpec((t, PAIR * WIDE), lambda g, j: (j, g))
    vec = pl.BlockSpec((1, PAIR * HEAD), lambda g, j: (0, g))
    outs, side_results = _pallas(
        body,
        name=name,
        args=(qb, doa, ka, va, ukv, g_k, ub, g_q),
        grid=(nh // PAIR, nk),
        in_specs=[wide_all, wide_all, wide_tile, wide_tile,
                  pl.BlockSpec((None, t, PAIR * HEAD), lambda g, j: (0, j, g)), vec,
                  pl.BlockSpec((None, lp, PAIR * HEAD), lambda g, j: (0, 0, g)), vec],
        out_specs=[whole, vec, pl.BlockSpec((2, t, PAIR * HEAD), lambda g, j: (0, j, g)), vec,
                   pl.BlockSpec((PAIR, 1, t), lambda g, j: (g, 0, j))],
        out_shape=[jax.ShapeDtypeStruct((lp, d), MXU), jax.ShapeDtypeStruct((1, d), F32),
                   jax.ShapeDtypeStruct((2, lp, d), MXU), jax.ShapeDtypeStruct((1, d), F32),
                   jax.ShapeDtypeStruct((nh, 1, lp), F32)],
        scratch_shapes=scratch,
        semantics=("parallel", "arbitrary"),
        vmem_mib=48,
        sides=sides,
    )
    return (*outs, side_results) if sides else tuple(outs)


def _fox_do(d_og, o, o_lo, ub, name="fox_do", tm=384):
    lp, d = o.shape
    tm = _tile(lp, tm)
    nh = d // HEAD

    def body(dog_ref, o_ref, olo_ref, z_ref, doa_ref):
        for h in range(nh):
            hs = slice(h * HEAD, (h + 1) * HEAD)
            z = z_ref[:, hs]
            do = (dog_ref[:, hs] * (z * _sigmoid(z))).astype(doa_ref.dtype)
            delta = jnp.sum(do.astype(F32) * (o_ref[:, hs] + olo_ref[:, hs]), axis=-1, keepdims=True)
            doa_ref[:, h * WIDE:h * WIDE + HEAD] = do
            doa_ref[:, h * WIDE + HEAD:(h + 1) * WIDE] = _extra_cols(tm, _split3(-delta), (0.0, 0.0, 0.0)).astype(
                doa_ref.dtype)

    row = pl.BlockSpec((tm, d), lambda i: (i, 0))
    return pl.pallas_call(
        body,
        name=name,
        grid=(lp // tm,),
        in_specs=[row, row, row, pl.BlockSpec((None, tm, d), lambda i: (1, i, 0))],
        out_specs=pl.BlockSpec((tm, nh * WIDE), lambda i: (i, 0)),
        out_shape=jax.ShapeDtypeStruct((lp, nh * WIDE), MXU),
        compiler_params=_cparams(("parallel",), 56),
    )(d_og, o, o_lo, ub)


def _fox_gate_bwd(ub, d_og, o, dq, name="fox_gate_bwd", tm=384):
    _, lp, d = ub.shape
    tm = _tile(lp, tm)

    def body(z_ref, dog_ref, o_ref, dq_ref, dub_ref):
        z = z_ref[...]
        sz = _sigmoid(z)
        dub_ref[0] = dq_ref[...]
        dub_ref[1] = (dog_ref[...] * o_ref[...] * (sz * (1.0 + z * (1.0 - sz)))).astype(dub_ref.dtype)

    row = pl.BlockSpec((tm, d), lambda i: (i, 0))
    return pl.pallas_call(
        body,
        name=name,
        grid=(lp // tm,),
        in_specs=[pl.BlockSpec((None, tm, d), lambda i: (1, i, 0)), row, row, row],
        out_specs=pl.BlockSpec((2, tm, d), lambda i: (0, i, 0)),
        out_shape=jax.ShapeDtypeStruct((2, lp, d), MXU),
        compiler_params=_cparams(("parallel",), 48),
    )(ub, d_og, o, dq)


def _loss(h, target, name="loss_head"):
    lp, d = h.shape
    nb = lp // 128

    def body(h_ref, t_ref, loss_ref, dh_ref, acc_ref):
        i = pl.program_id(0)

        @pl.when(i == 0)
        def _():
            acc_ref[...] = jnp.zeros_like(acc_ref)
            dh_ref[...] = jnp.zeros_like(dh_ref)

        @pl.when(i > 0)
        def _():
            err = h_ref[...] - t_ref[...]
            dh_ref[...] = err * (1.0 / d)
            acc_ref[...] += jnp.sum(jnp.sum(err * err, axis=-1, keepdims=True) * (1.0 / d), axis=0, keepdims=True)

        @pl.when(i == nb - 1)
        def _():
            loss_ref[...] = 0.5 * acc_ref[...]

    return pl.pallas_call(
        body,
        name=name,
        grid=(nb,),
        in_specs=[pl.BlockSpec((128, d), lambda i: (i, 0)),
                  pl.BlockSpec((128, d), lambda i: (jnp.maximum(i - 1, 0), 0))],
        out_specs=[pl.BlockSpec((1, 1), lambda i: (0, 0)), pl.BlockSpec((128, d), lambda i: (i, 0))],
        out_shape=[jax.ShapeDtypeStruct((1, 1), F32), jax.ShapeDtypeStruct((lp, d), F32)],
        scratch_shapes=[pltpu.VMEM((1, 1), F32)],
        compiler_params=_cparams(("arbitrary",), 32),
    )(h, target)


def _adam_math(w, g, m, v):
    m = ADAM_B1 * m + (1.0 - ADAM_B1) * g
    v = ADAM_B2 * v + (1.0 - ADAM_B2) * (g * g)
    m_hat = m / (1.0 - ADAM_B1 ** ADAM_STEP)
    v_hat = v / (1.0 - ADAM_B2 ** ADAM_STEP)
    delta = -ADAM_LR * (m_hat / (jnp.sqrt(v_hat) + ADAM_EPS) + ADAM_WD * w)
    return delta, m, v


def _adamw(parts, w, m, v, name, tm=512):
    n, r, c = parts.shape
    tm = tm if r % tm == 0 else r

    def body(p_ref, w_ref, m_ref, v_ref, g_ref, d_ref, nm_ref, nv_ref):
        g = p_ref[0].astype(F32)
        for j in range(1, n):
            g = g + p_ref[j].astype(F32)
        g_ref[...] = g
        d_ref[...], nm_ref[...], nv_ref[...] = _adam_math(w_ref[...], g, m_ref[...], v_ref[...])

    row = pl.BlockSpec((tm, c), lambda i: (i, 0))
    return pl.pallas_call(
        body,
        name=name,
        grid=(r // tm,),
        in_specs=[pl.BlockSpec((n, tm, c), lambda i: (0, i, 0)), row, row, row],
        out_specs=[row] * 4,
        out_shape=[jax.ShapeDtypeStruct((r, c), F32)] * 4,
        compiler_params=_cparams(("parallel",), 48),
    )(parts, w, m, v)


def _adamw_small(gs, ws, ms, vs, name="adamw_small"):
    n = len(ws)
    flat = [jnp.reshape(a, (-1, a.shape[-1])) for group in (gs, ws, ms, vs) for a in group]

    def body(*refs):
        g_refs, w_refs, m_refs, v_refs = (refs[k * n:(k + 1) * n] for k in range(4))
        outs = refs[4 * n:]
        for i in range(n):
            d_new, m_new, v_new = _adam_math(w_refs[i][...], g_refs[i][...], m_refs[i][...], v_refs[i][...])
            outs[i][...], outs[n + i][...], outs[2 * n + i][...] = d_new, m_new, v_new

    res = pl.pallas_call(
        body, name=name, out_shape=[jax.ShapeDtypeStruct(a.shape, F32) for a in flat[n:2 * n]] * 3)(*flat)
    return [[jnp.reshape(r, w.shape) for r, w in zip(res[k * n:(k + 1) * n], ws)] for k in range(3)]


def _place():
    x, y, c = lax.axis_index("x"), lax.axis_index("y"), lax.axis_index("c")
    return x, y, c


def _other_chips(x, y):
    return [(1 - x, y), (x, 1 - y), (1 - x, 1 - y)]


def _any_spec():
    return pl.BlockSpec(memory_space=pl.ANY)


class _Side:
    def __init__(self, ins, outs, sems, start, finish, aliases=None):
        self.ins, self.outs, self.sems = list(ins), list(outs), sems
        self.start, self.finish = start, finish
        self.aliases = dict(aliases or {})


def _pallas(body, *, name, out_shape, args=(), grid=(), in_specs=(), out_specs=(), scratch_shapes=(),
            semantics=(), vmem_mib=None, sides=(), aliases=None):
    n_in, n_out, n_scr = len(args), len(out_shape), len(scratch_shapes)
    side_ins = [a for s in sides for a in s.ins]
    side_outs = [o for s in sides for o in s.outs]
    side_sems = [pltpu.SemaphoreType.DMA((max(k, 1),)) for s in sides for k in s.sems]
    aliases, in_at, out_at = dict(aliases or {}), n_in, n_out
    for s in sides:
        aliases.update({in_at + i: out_at + o for i, o in s.aliases.items()})
        in_at, out_at = in_at + len(s.ins), out_at + len(s.outs)

    def wrapped(*refs):
        at = [0]

        def take(k):
            got = refs[at[0]:at[0] + k]
            at[0] += k
            return got

        main_in = take(n_in)
        s_in = [take(len(s.ins)) for s in sides]
        main_out = take(n_out)
        s_out = [take(len(s.outs)) for s in sides]
        main_scr = take(n_scr)
        s_sem = [take(3) for s in sides]

        def run(stage):
            for s, i_, o_, m_ in zip(sides, s_in, s_out, s_sem):
                getattr(s, stage)(i_, o_, *m_)

        first = last = None
        for k, g in enumerate(grid):
            i = pl.program_id(k)
            first = (i == 0) if first is None else jnp.logical_and(first, i == 0)
            last = (i == g - 1) if last is None else jnp.logical_and(last, i == g - 1)
        if sides:
            run("start") if first is None else pl.when(first)(lambda: run("start"))
        body(*main_in, *main_out, *main_scr)
        if sides:
            run("finish") if last is None else pl.when(last)(lambda: run("finish"))

    kw = {}
    if grid:
        kw["grid"] = grid
    if semantics or vmem_mib:
        sem = tuple("arbitrary" for _ in grid) if sides else tuple(semantics)
        kw["compiler_params"] = pltpu.CompilerParams(
            dimension_semantics=sem or None, vmem_limit_bytes=vmem_mib * MIB if vmem_mib else None)
    res = pl.pallas_call(
        wrapped,
        name=name,
        in_specs=list(in_specs) + [_any_spec()] * len(side_ins),
        out_specs=list(out_specs) + [_any_spec()] * len(side_outs),
        out_shape=list(out_shape) + side_outs,
        scratch_shapes=list(scratch_shapes) + side_sems,
        input_output_aliases=aliases,
        **kw,
    )(*args, *side_ins)
    main, rest, per_side = list(res[:n_out]), list(res[n_out:]), []
    for s in sides:
        per_side.append(rest[:len(s.outs)])
        rest = rest[len(s.outs):]
    return main, per_side


def _slot(p):
    return 4 * p[0] + 2 * p[1] + p[2]


def _sibling_side(grads):
    n = len(grads)

    def copies(ins, outs, send_sems, recv_sems):
        x, y, c = _place()
        return [pltpu.make_async_remote_copy(
            src_ref=ins[t].at[2 * chip + (1 - c)], dst_ref=outs[t].at[chip],
            send_sem=send_sems.at[4 * t + chip], recv_sem=recv_sems.at[4 * t + chip],
            device_id=(x, y, 1 - c), device_id_type=MESH) for t in range(n) for chip in range(N_CHIP)]

    def start(ins, outs, send_sems, recv_sems, local_sems):
        for cp in copies(ins, outs, send_sems, recv_sems):
            cp.start()

    def finish(ins, outs, send_sems, recv_sems, local_sems):
        for cp in copies(ins, outs, send_sems, recv_sems):
            cp.wait()

    outs = [jax.ShapeDtypeStruct((N_CHIP,) + g.shape[1:], g.dtype) for g in grads]
    return _Side(grads, outs, (4 * n, 4 * n, 0), start, finish)


def _chips_side(partials):
    n = len(partials)

    def copies(ins, outs, send_sems, recv_sems, local_sems):
        x, y, c = _place()
        my_chip = 2 * x + y
        local = [pltpu.make_async_copy(ins[t].at[my_chip], outs[t].at[my_chip], local_sems.at[t]) for t in range(n)]
        sends, recvs = [], []
        for t in range(n):
            for j, chip in enumerate(_other_chips(x, y)):
                their = 2 * chip[0] + chip[1]
                sems = dict(send_sem=send_sems.at[3 * t + j], recv_sem=recv_sems.at[3 * t + j],
                            device_id=(*chip, c), device_id_type=MESH)
                sends.append(pltpu.make_async_remote_copy(src_ref=ins[t].at[their], dst_ref=outs[t].at[my_chip], **sems))
                recvs.append(pltpu.make_async_remote_copy(src_ref=ins[t].at[my_chip], dst_ref=outs[t].at[their], **sems))
        return local, sends, recvs

    def start(ins, outs, send_sems, recv_sems, local_sems):
        local, sends, _ = copies(ins, outs, send_sems, recv_sems, local_sems)
        for cp in local + sends:
            cp.start()

    def finish(ins, outs, send_sems, recv_sems, local_sems):
        local, sends, recvs = copies(ins, outs, send_sems, recv_sems, local_sems)
        for cp in sends:
            cp.wait_send()
        for cp in recvs:
            cp.wait_recv()
        for cp in local:
            cp.wait()

    outs = [jax.ShapeDtypeStruct(p.shape, p.dtype) for p in partials]
    return _Side(partials, outs, (3 * n, 3 * n, n), start, finish)


def _gather_own_side(blocks):
    n = len(blocks)

    def copies(ins, outs, send_sems, recv_sems, local_sems):
        x, y, c = _place()
        me = (x, y, c)
        peers = [(x, y, 1 - c)] + [(*chip, c) for chip in _other_chips(x, y)]
        local = [pltpu.make_async_copy(ins[t], outs[t].at[_slot(me)], local_sems.at[t]) for t in range(n)]
        sends, recvs = [], []
        for t in range(n):
            for k, peer in enumerate(peers):
                sems = dict(send_sem=send_sems.at[4 * t + k], recv_sem=recv_sems.at[4 * t + k],
                            device_id=peer, device_id_type=MESH)
                sends.append(pltpu.make_async_remote_copy(src_ref=ins[t], dst_ref=outs[t].at[_slot(me)], **sems))
                recvs.append(pltpu.make_async_remote_copy(src_ref=ins[t], dst_ref=outs[t].at[_slot(peer)], **sems))
        return local, sends, recvs

    def start(ins, outs, send_sems, recv_sems, local_sems):
        local, sends, _ = copies(ins, outs, send_sems, recv_sems, local_sems)
        for cp in local + sends:
            cp.start()

    def finish(ins, outs, send_sems, recv_sems, local_sems):
        local, sends, recvs = copies(ins, outs, send_sems, recv_sems, local_sems)
        for cp in sends:
            cp.wait_send()
        for cp in recvs:
            cp.wait_recv()
        for cp in local:
            cp.wait()

    outs = [jax.ShapeDtypeStruct((N_DEV,) + b.shape, b.dtype) for b in blocks]
    return _Side(blocks, outs, (4 * n, 4 * n, n), start, finish)


def _gather_pass_side(gathered):
    n = len(gathered)

    def copies(outs, send_sems, recv_sems):
        x, y, c = _place()
        sends, recvs = [], []
        for t in range(n):
            for j, chip in enumerate(_other_chips(x, y)):
                sems = dict(send_sem=send_sems.at[3 * t + j], recv_sem=recv_sems.at[3 * t + j],
                            device_id=(x, y, 1 - c), device_id_type=MESH)
                mine, theirs = outs[t].at[_slot((*chip, c))], outs[t].at[_slot((*chip, 1 - c))]
                sends.append(pltpu.make_async_remote_copy(src_ref=mine, dst_ref=mine, **sems))
                recvs.append(pltpu.make_async_remote_copy(src_ref=mine, dst_ref=theirs, **sems))
        return sends, recvs

    def start(ins, outs, send_sems, recv_sems, local_sems):
        for cp in copies(outs, send_sems, recv_sems)[0]:
            cp.start()

    def finish(ins, outs, send_sems, recv_sems, local_sems):
        sends, recvs = copies(outs, send_sems, recv_sems)
        for cp in sends:
            cp.wait_send()
        for cp in recvs:
            cp.wait_recv()

    outs = [jax.ShapeDtypeStruct(g.shape, g.dtype) for g in gathered]
    return _Side(gathered, outs, (3 * n, 3 * n, 0), start, finish, aliases={t: t for t in range(n)})


def _alone(side, name):
    return _pallas(lambda: None, name=name, out_shape=[], sides=[side])[1][0]


def _all_gather(blocks, name):
    n = len(blocks)

    def body(*refs):
        in_refs, out_refs = refs[:n], refs[n:2 * n]
        send_sems, recv_sems, local_sems = refs[2 * n:]
        x, y, c = _place()
        me, sibling = (x, y, c), (x, y, 1 - c)
        chips = _other_chips(x, y)

        def slot(p):
            return 4 * p[0] + 2 * p[1] + p[2]

        def copy(t, k, block, to, src=None):
            dst = out_refs[t].at[slot(block)]
            return pltpu.make_async_remote_copy(
                src_ref=dst if src is None else src, dst_ref=dst,
                send_sem=send_sems.at[7 * t + k], recv_sem=recv_sems.at[7 * t + k],
                device_id=to, device_id_type=MESH)

        started = []
        for t in range(n):
            mine = pltpu.make_async_copy(in_refs[t], out_refs[t].at[slot(me)], local_sems.at[t])
            mine.start()
            started.append(mine)
        sends = []
        for t in range(n):
            first = [copy(t, 0, me, sibling, src=in_refs[t])]
            first += [copy(t, 1 + j, me, (*chip, c), src=in_refs[t]) for j, chip in enumerate(chips)]
            for cp in first:
                cp.start()
            sends += first
        for t in range(n):
            for j, chip in enumerate(chips):
                copy(t, 1 + j, (*chip, c), me).wait_recv()
                passed = copy(t, 4 + j, (*chip, c), sibling)
                passed.start()
                sends.append(passed)
        for t in range(n):
            copy(t, 0, sibling, me).wait_recv()
            for j, chip in enumerate(chips):
                copy(t, 4 + j, (*chip, 1 - c), me).wait_recv()
        for cp in sends:
            cp.wait_send()
        for mine in started:
            mine.wait()

    return pl.pallas_call(
        body,
        name=name,
        in_specs=[_any_spec()] * n,
        out_specs=[_any_spec()] * n,
        out_shape=[jax.ShapeDtypeStruct((N_DEV,) + b.shape, b.dtype) for b in blocks],
        scratch_shapes=[pltpu.SemaphoreType.DMA((7 * n,)), pltpu.SemaphoreType.DMA((7 * n,)),
                        pltpu.SemaphoreType.DMA((n,))],
    )(*blocks)


def _pair_sum(g8, got, name, tm=1024):
    _, r, c = g8.shape
    tm = tm if r % tm == 0 else r
    core = lax.axis_index("c")

    def body(core_ref, mine_ref, got_ref, o_ref):
        south_first = core_ref[0] == 0
        a, b = mine_ref[...], got_ref[...]
        o_ref[...] = (jnp.where(south_first, a, b) + jnp.where(south_first, b, a)).astype(o_ref.dtype)

    return pl.pallas_call(
        body,
        name=name,
        grid_spec=pltpu.PrefetchScalarGridSpec(
            num_scalar_prefetch=1,
            grid=(N_CHIP, r // tm),
            in_specs=[pl.BlockSpec((None, tm, c), lambda j, i, core_ref: (2 * j + core_ref[0], i, 0)),
                      pl.BlockSpec((None, tm, c), lambda j, i, core_ref: (j, i, 0))],
            out_specs=pl.BlockSpec((None, tm, c), lambda j, i, core_ref: (j, i, 0)),
        ),
        out_shape=jax.ShapeDtypeStruct((N_CHIP, r, c), PAYLOAD),
        compiler_params=_cparams(("parallel", "parallel"), 32),
    )(jnp.reshape(core, (1,)).astype(jnp.int32), g8, got)


def _all_reduce_small(s, name="small_all_reduce"):
    r, c = s.shape

    def body(s_ref, o_ref, buf_ref, send_sems, recv_sems):
        x, y, c_ = _place()
        me = 4 * x + 2 * y + c_
        buf_ref[me] = s_ref[...]

        def copy(k, slot, peer):
            return pltpu.make_async_remote_copy(
                src_ref=s_ref, dst_ref=buf_ref.at[slot],
                send_sem=send_sems.at[k - 1], recv_sem=recv_sems.at[k - 1],
                device_id=peer, device_id_type=MESH)

        peers = []
        for k in range(1, N_DEV):
            peer = (x ^ ((k >> 2) & 1), y ^ ((k >> 1) & 1), c_ ^ (k & 1))
            peers.append(peer)
            copy(k, me, peer).start()
        for k, peer in zip(range(1, N_DEV), peers):
            cp = copy(k, 4 * peer[0] + 2 * peer[1] + peer[2], peer)
            cp.wait_send()
            cp.wait_recv()
        total = buf_ref[0]
        for j in range(1, N_DEV):
            total = total + buf_ref[j]
        o_ref[...] = total

    return pl.pallas_call(
        body,
        name=name,
        out_shape=jax.ShapeDtypeStruct((r, c), F32),
        in_specs=[pl.BlockSpec(memory_space=pltpu.VMEM)],
        out_specs=pl.BlockSpec(memory_space=pltpu.VMEM),
        scratch_shapes=[pltpu.VMEM((N_DEV, r, c), F32), pltpu.SemaphoreType.DMA((N_DEV - 1,)),
                        pltpu.SemaphoreType.DMA((N_DEV - 1,))],
    )(s)


def _kv_weights(g_kv):
    d = g_kv.shape[1]
    nh = d // HEAD
    wkv = jnp.reshape(jnp.transpose(g_kv, (1, 0, 2)), (d, -1))
    return wkv[:, :2 * d], jnp.pad(wkv[:, 2 * d:], ((0, 0), (0, 128 - nh)))


def _kv_grad_blocks(dwkv2, dwfl):
    d = dwkv2.shape[0]
    dwkv = jnp.concatenate([dwkv2, dwfl[:, :d // HEAD]], axis=1)
    return jnp.transpose(jnp.reshape(dwkv, (d, N_DEV, -1)), (1, 0, 2))


def _local_step(x, target, meta, gamma, a_norm, wa_in, a_out_norm, wa_out, kv_norm, late, b_f, g_k, b_norm, g_q,
                dist):
    d = x.shape[1]
    nh = d // HEAD
    cols = d // N_DEV
    h0 = jnp.concatenate([jnp.zeros((ROW_PAD, d), F32), meta, x], axis=0)
    lp = h0.shape[0]
    big = dict(tm=1408, tn=512, tk=2048)

    lb = _lb_fwd(gamma)
    (hn_a,) = _rms_fwd(h0, [a_norm], "rms_a")
    if dist:
        a_out_blk, kv_blk, b_in_blk, b_out_blk = late
        u4, (first,) = _matmul(hn_a, wa_in, "nn", F32, "a_in", out_parts=4,
                               sides=[_gather_own_side([a_out_blk, b_in_blk])], **big)
        o_a, og_a, ((g_a_out, g_b_in), (g_kv,)) = _hgrn2_fwd(
            u4, lb, a_out_norm, sides=[_gather_pass_side(first), _gather_own_side([kv_blk])])
        wa_out = jnp.reshape(g_a_out, (d, d))
        h1, ((g_kv,), (g_b_out,)) = _matmul(og_a, wa_out, "nn", F32, "a_out", add=h0,
                                            sides=[_gather_pass_side([g_kv]), _gather_own_side([b_out_blk])], **big)
        (wkv2, wfl), wb_in = _kv_weights(g_kv), g_b_in
    else:
        u4 = _matmul(hn_a, wa_in, "nn", F32, "a_in", out_parts=4, **big)
        o_a, og_a = _hgrn2_fwd(u4, lb, a_out_norm)
        h1 = _matmul(og_a, wa_out, "nn", F32, "a_out", add=h0, **big)
        wkv2, wfl, wb_in, wb_out = late
    hk, hb = _rms_fwd(h1, [kv_norm, b_norm], "rms_kv_b")
    if dist:
        ukv, ((g_b_out,),) = _matmul(hk, wkv2, "nn", F32, "kv_in", out_parts=2,
                                     sides=[_gather_pass_side([g_b_out])], **big)
        wb_out = jnp.reshape(g_b_out, (d, d))
    else:
        ukv = _matmul(hk, wkv2, "nn", F32, "kv_in", out_parts=2, **big)
    ufl = _matmul(hk, wfl, "nn", F32, "kv_f", **big)
    ub = _matmul(hb, wb_in, "nn", F32, "b_in", out_parts=2, **big)
    f_cum = _fgate_fwd(ufl, b_f)
    qa, ka, va = _attn_operands(ub, ukv, g_q, g_k, f_cum)
    o_b, o_lo, og_b, qb = _fox_fwd(qa, ka, va, ub)
    h2 = _matmul(og_b, wb_out, "nn", F32, "b_out", add=h1, **big)
    loss, dh2 = _loss(h2, target)

    dx_t = dict(tm=1408, tn=512, tk=2048, vmem_mib=56)
    dx_parts_t = dict(tm=704, tn=512, tk=2048, vmem_mib=58, k_whole=True)
    dw_t = dict(tm=512, tn=1024, tk=lp, vmem_mib=58)
    dw_f32_t = dict(tm=1024, tn=512, tk=lp, vmem_mib=58)
    def to_sibling(g8):
        return [_sibling_side([g8])] if dist else []

    def to_chips(partial):
        return [_chips_side([partial])] if dist else []

    def unpack(res, n_sides):
        if not dist:
            return res, [None] * n_sides
        *main, side_results = res
        return (main[0] if len(main) == 1 else tuple(main)), [r[0] for r in side_results]

    dwb_out = _matmul(og_b, dh2, "tn", F32, "b_out_dw", **dw_f32_t)
    g8_b_out = jnp.reshape(dwb_out, (N_DEV, cols, d))
    d_ogb, (got,) = unpack(_matmul(dh2, wb_out, "nt", F32, "b_out_dx", sides=to_sibling(g8_b_out), **dx_t), 1)
    p_b_out = _pair_sum(g8_b_out, got, "pair_sum_b_w_out") if dist else None
    doa = _fox_do(d_ogb, o_b, o_lo, ub)
    (dq, dg_q, dukv, dg_k, dfk), (r_b_out,) = unpack(
        _fox_bwd(qb, ka, va, doa, ukv, g_k, ub, g_q, sides=to_chips(p_b_out)), 1)
    dub = _fox_gate_bwd(ub, d_ogb, o_b, dq)
    d_f = jnp.pad(jnp.transpose(dfk[:, 0, :]), ((0, 0), (0, 128 - nh)))
    dufl, db_f = _fgate_bwd(ufl, b_f, d_f)
    dwb_in = _matmul(hb, dub, "tn", F32, "b_in_dw", out_parts=N_DEV, **dw_t)
    d_hb, (got,) = unpack(_matmul(dub, wb_in, "nt", F32, "b_in_dx", sides=to_sibling(dwb_in), **dx_parts_t), 1)
    p_b_in = _pair_sum(dwb_in, got, "pair_sum_b_w_in") if dist else None
    d_hk, (r_b_in,) = unpack(_matmul(dukv, wkv2, "nt", F32, "kv_dx", sides=to_chips(p_b_in), **dx_parts_t), 1)
    d_hk = _matmul(dufl, wfl, "nt", F32, "kv_f_dx", add=d_hk, **dx_t)
    dwkv2 = _matmul(hk, dukv, "tn", F32, "kv_dw", **dw_t)
    dwfl = _matmul(hk, dufl, "tn", F32, "kv_f_dw", **dw_t)
    g8_kv = _kv_grad_blocks(dwkv2, dwfl) if dist else None
    dh1, (dg_kv, dg_b) = _rms_bwd(h1, [kv_norm, b_norm], [d_hk, d_hb], dh2, "rms_kv_b_bwd")
    d_oga, (got,) = unpack(_matmul(dh1, wa_out, "nt", F32, "a_out_dx", sides=to_sibling(g8_kv), **dx_t), 1)
    p_kv = _pair_sum(g8_kv, got, "pair_sum_kv_w") if dist else None
    dwa_out = _matmul(og_a, dh1, "tn", F32, "a_out_dw", **dw_f32_t)
    g8_a_out = jnp.reshape(dwa_out, (N_DEV, cols, d))
    (du4, dlb, dg_aout), (r_kv, got) = unpack(
        _hgrn2_bwd(u4, o_a, d_oga, lb, a_out_norm, sides=to_chips(p_kv) + to_sibling(g8_a_out)), 2)
    p_a_out = _pair_sum(g8_a_out, got, "pair_sum_a_w_out") if dist else None
    dwa_in, (r_a_out,) = unpack(
        _matmul(hn_a, du4, "tn", F32, "a_in_dw", out_parts=N_DEV, sides=to_chips(p_a_out), **dw_t), 1)
    row_tiles = lp // _tile(lp, dx_parts_t["tm"], 64)
    if dist and row_tiles >= 2:
        top = max(1, row_tiles // 6)
        d_hna, ((got,),) = _matmul(du4, wa_in, "nt", F32, "a_in_dx_top", rows=(0, top),
                                   sides=to_sibling(dwa_in), **dx_parts_t)
        p_a_in = _pair_sum(dwa_in, got, "pair_sum_a_w_in")
        d_hna, ((r_a_in,),) = _matmul(du4, wa_in, "nt", F32, "a_in_dx_bottom", rows=(top, row_tiles - top), into=d_hna,
                                      sides=to_chips(p_a_in), **dx_parts_t)
    else:
        d_hna, (got,) = unpack(_matmul(du4, wa_in, "nt", F32, "a_in_dx", sides=to_sibling(dwa_in), **dx_parts_t), 1)
        if dist:
            (r_a_in,) = _alone(_chips_side([_pair_sum(dwa_in, got, "pair_sum_a_w_in")]), "grads_to_chips_a_w_in")
    dh0, (dg_a,) = _rms_bwd(h0, [a_norm], [d_hna], dh1, "rms_a_bwd")
    dgamma = _lb_bwd(gamma, dlb)

    grads = dict(meta=dh0[ROW_PAD:ROW_PAD + N_META], gamma=dgamma, a_norm=dg_a, a_out_norm=dg_aout, kv_norm=dg_kv,
                 b_f=db_f, g_k=dg_k, b_norm=dg_b, g_q=dg_q)
    if dist:
        grads.update(wa_in=r_a_in, wa_out=r_a_out, wkv=r_kv, wb_in=r_b_in, wb_out=r_b_out)
    else:
        grads.update(wa_in=dwa_in, wa_out=dwa_out, wkv2=dwkv2, wfl=dwfl, wb_in=dwb_in, wb_out=dwb_out)
    return loss, dh0[ROW_PAD + N_META:], grads


def kernel(x, meta, gamma_lb, a_norm, a_w_in, a_out_norm, a_w_out, kv_norm, kv_w, fox_b_f, fox_k_norm, b_norm, b_w_in, b_q_norm, b_w_out, loss_target, m_meta, m_gamma_lb, m_a_norm, m_a_w_in, m_a_out_norm, m_a_w_out, m_kv_norm, m_kv_w, m_fox_b_f, m_fox_k_norm, m_b_norm, m_b_w_in, m_b_q_norm, m_b_w_out, v_meta, v_gamma_lb, v_a_norm, v_a_w_in, v_a_out_norm, v_a_w_out, v_kv_norm, v_kv_w, v_fox_b_f, v_fox_k_norm, v_b_norm, v_b_w_in, v_b_q_norm, v_b_w_out):
    d = x.shape[-1]
    nh = d // HEAD
    cols = d // N_DEV
    me = 4 * lax.axis_index("x") + 2 * lax.axis_index("y") + lax.axis_index("c")

    sharded_small = jnp.concatenate([meta, gamma_lb, a_norm, a_out_norm, jnp.zeros((4, cols), F32)], axis=0)
    g_a_in, g_small = _all_gather([a_w_in[0].astype(MXU), sharded_small], "gather_weights")
    wa_out = None
    late = (a_w_out[0].astype(MXU), kv_w.astype(MXU), b_w_in[0].astype(MXU), b_w_out[0].astype(MXU))
    small = jnp.reshape(jnp.transpose(g_small, (1, 0, 2)), (-1, d))
    meta_f, gamma_f, a_norm_f, a_out_norm_f = small[:16], small[16:18], small[18:19], small[19:20]
    b_f = jnp.pad(jnp.reshape(fox_b_f, (1, nh)), ((0, 0), (0, 128 - nh)))
    g_k = jnp.reshape(fox_k_norm, (1, d))
    g_q = jnp.reshape(b_q_norm, (1, d))
    kv_norm_r = jnp.reshape(kv_norm, (1, d))

    loss, grad_x, g = _local_step(x[0], loss_target[0], meta_f, gamma_f, a_norm_f, g_a_in, a_out_norm_f, wa_out,
                                  kv_norm_r, late, b_f, g_k, b_norm, g_q, dist=True)
    loss = lax.psum(loss[0, 0], AXES)

    r_a_in = _adamw(g["wa_in"], a_w_in[0], m_a_w_in[0], v_a_w_in[0], "adamw_a_w_in")
    r_a_out = _adamw(g["wa_out"], a_w_out[0], m_a_w_out[0], v_a_w_out[0], "adamw_a_w_out")
    r_kv = _adamw(g["wkv"], kv_w, m_kv_w, v_kv_w, "adamw_kv_w")
    r_b_in = _adamw(g["wb_in"], b_w_in[0], m_b_w_in[0], v_b_w_in[0], "adamw_b_w_in")
    r_b_out = _adamw(g["wb_out"], b_w_out[0], m_b_w_out[0], v_b_w_out[0], "adamw_b_w_out")

    packed = jnp.concatenate(
        [g["meta"], g["gamma"], g["a_norm"], g["a_out_norm"], g["kv_norm"], g["b_norm"], g["g_k"], g["g_q"],
         jnp.pad(g["b_f"], ((0, 0), (0, d - 128))), jnp.zeros((7, d), F32)], axis=0)
    tot = _all_reduce_small(packed)
    mine = lax.dynamic_slice_in_dim(tot[:20], me * cols, cols, axis=1)
    gs = [mine[:16], mine[16:18], mine[18:19], mine[19:20], tot[20], tot[21:22], jnp.reshape(tot[22], (nh, HEAD)),
          jnp.reshape(tot[23], (1, nh, HEAD)), tot[24, :nh]]
    small_w = [meta, gamma_lb, a_norm, a_out_norm, kv_norm, b_norm, fox_k_norm, b_q_norm, fox_b_f]
    small_m = [m_meta, m_gamma_lb, m_a_norm, m_a_out_norm, m_kv_norm, m_b_norm, m_fox_k_norm, m_b_q_norm, m_fox_b_f]
    small_v = [v_meta, v_gamma_lb, v_a_norm, v_a_out_norm, v_kv_norm, v_b_norm, v_fox_k_norm, v_b_q_norm, v_fox_b_f]

    d_s, m_s, v_s = _adamw_small(gs, small_w, small_m, small_v)

    def ordered(s, a_in, a_out, kv, b_in, b_out):
        return [s[0], s[1], s[2], a_in[None], s[3], a_out[None], s[4], kv, s[8], s[6], s[5], b_in[None], s[7], b_out[None]]

    outs = []
    for i, s in enumerate([gs, d_s, m_s, v_s]):
        outs += ordered(s, r_a_in[i], r_a_out[i], r_kv[i], r_b_in[i], r_b_out[i])
    return (loss, grad_x[None], *outs)
```

```python
import math

import jax
import jax.numpy as jnp
from jax import lax
from jax.experimental import pallas as pl
from jax.experimental.pallas import tpu as pltpu

HEAD = 128
CHUNK = 64
N_META = 16
ROW_PAD = 128 - N_META
EPS = 1e-6
MASK_VALUE = -1e30
ADAM_LR = 0.001
ADAM_B1 = 0.9
ADAM_B2 = 0.999
ADAM_EPS = 1e-08
ADAM_WD = 0.01
ADAM_STEP = 10
N_DEV = 8
N_CHIP = 4
MIB = 1024 * 1024
AXES = ("x", "y", "c")
MESH = pl.DeviceIdType.MESH

F32 = jnp.float32
MXU = jnp.bfloat16
PAYLOAD = jnp.bfloat16
HI = lax.Precision.HIGHEST

NN = (((1,), (0,)), ((), ()))
NT = (((1,), (1,)), ((), ()))
TN = (((0,), (0,)), ((), ()))


def _dot(a, b, dims=NN):
    return lax.dot_general(a.astype(MXU), b.astype(MXU), dims, preferred_element_type=F32)


def _dot_exact(a, b):
    return lax.dot_general(a, b, NN, precision=HI, preferred_element_type=F32)


def _sigmoid(x):
    return 1.0 / (1.0 + jnp.exp(-x))


def _tile(dim, target, unit=128):
    best = None
    t = unit
    while t <= min(dim, target):
        if dim % t == 0:
            best = t
        t += unit
    return best if best is not None else dim


def _cparams(semantics, vmem_mib):
    return pltpu.CompilerParams(dimension_semantics=semantics, vmem_limit_bytes=vmem_mib * MIB)


def _mat_spec(arr, br, bc, rc_of_grid):
    if arr.ndim == 2:
        return pl.BlockSpec((br, bc), rc_of_grid)
    assert arr.shape[2] % bc == 0, (arr.shape, bc)
    per = arr.shape[2] // bc

    def idx(*g):
        r, c = rc_of_grid(*g)
        return (c // per, r, c % per)

    return pl.BlockSpec((None, br, bc), idx)


def _mat_shape(arr):
    return (arr.shape[0], arr.shape[1]) if arr.ndim == 2 else (arr.shape[1], arr.shape[0] * arr.shape[2])


def _matmul(a, b, dims, out_dtype, name, *, add=None, out_parts=1, tm=512, tn=512, tk=512, vmem_mib=48, sides=(),
            k_whole=False, rows=None, into=None):
    ar, ac = _mat_shape(a)
    br_, bc_ = _mat_shape(b)
    if dims == "nn":
        m, k, n = ar, ac, bc_
        assert br_ == k
    elif dims == "nt":
        m, k, n = ar, ac, br_
        assert bc_ == k
    else:
        m, k, n = ac, ar, bc_
        assert br_ == k
    m_unit, n_unit, k_unit = m, n, k
    if a.ndim == 3:
        if dims == "tn":
            m_unit = math.gcd(m_unit, a.shape[2])
        else:
            k_unit = math.gcd(k_unit, a.shape[2])
    if b.ndim == 3:
        if dims == "nt":
            k_unit = math.gcd(k_unit, b.shape[2])
        else:
            n_unit = math.gcd(n_unit, b.shape[2])
    if out_parts > 1:
        n_unit = math.gcd(n_unit, n // out_parts)
    tm = _tile(m_unit, tm, 128 if dims == "tn" else 64)
    tn, tk = _tile(n_unit, tn), _tile(k_unit, tk)
    whole_k = dims == "nt" and tk < k and k_whole
    if whole_k:
        k_chunk, tk = tk, k
    gm, gn, gk = m // tm, n // tn, k // tk
    assert gm * tm == m and gn * tn == n and gk * tk == k, (name, m, n, k, tm, tn, tk)

    def chunk_of(ref, arr, c):
        if arr.ndim == 2:
            return ref[:, c * k_chunk:(c + 1) * k_chunk]
        per = arr.shape[2] // k_chunk
        return ref[c // per, :, (c % per) * k_chunk:(c % per + 1) * k_chunk]

    def all_cols(arr, rows, row_of_grid):
        if arr.ndim == 2:
            return pl.BlockSpec((rows, arr.shape[1]), lambda i, j, kk: (row_of_grid(i, j), 0))
        return pl.BlockSpec((arr.shape[0], rows, arr.shape[2]), lambda i, j, kk: (0, row_of_grid(i, j), 0))

    if dims == "nn":
        a_spec = _mat_spec(a, tm, tk, lambda i, j, kk: (i, kk))
        b_spec = _mat_spec(b, tk, tn, lambda i, j, kk: (kk, j))
        dn = NN
    elif whole_k:
        a_spec = all_cols(a, tm, lambda i, j: i)
        b_spec = all_cols(b, tn, lambda i, j: j)
        dn = NT
    elif dims == "nt":
        a_spec = _mat_spec(a, tm, tk, lambda i, j, kk: (i, kk))
        b_spec = _mat_spec(b, tn, tk, lambda i, j, kk: (j, kk))
        dn = NT
    else:
        a_spec = _mat_spec(a, tk, tm, lambda i, j, kk: (kk, i))
        b_spec = _mat_spec(b, tk, tn, lambda i, j, kk: (kk, j))
        dn = TN

    if out_parts > 1:
        per = (n // out_parts) // tn
        out_shape = jax.ShapeDtypeStruct((out_parts, m, n // out_parts), out_dtype)
        o_spec = pl.BlockSpec((None, tm, tn), lambda i, j, kk: (j // per, i, j % per))
    else:
        out_shape = jax.ShapeDtypeStruct((m, n), out_dtype)
        o_spec = pl.BlockSpec((tm, tn), lambda i, j, kk: (i, j))

    in_specs = [a_spec, b_spec]
    args = [a, b]
    if add is not None:
        in_specs.append(pl.BlockSpec((tm, tn), lambda i, j, kk: (i, j)))
        args.append(add)
    if rows is not None:
        first_tile, gm = rows

        def shifted(spec):
            return pl.BlockSpec(spec.block_shape, lambda i, j, kk: spec.index_map(i + first_tile, j, kk))

        in_specs = [shifted(s) for s in in_specs]
        o_spec = shifted(o_spec)
    aliases = {}
    if into is not None:
        aliases[len(args)] = 0
        in_specs.append(_any_spec())
        args.append(into)

    def body(*refs):
        a_ref, b_ref, *others, o_ref, acc_ref = refs
        add_ref = others[0] if add is not None else None
        kk = pl.program_id(2)
        if whole_k:
            part = sum(lax.dot_general(chunk_of(a_ref, a, c).astype(MXU), chunk_of(b_ref, b, c).astype(MXU), dn,
                                       preferred_element_type=F32) for c in range(k // k_chunk))
        else:
            part = lax.dot_general(a_ref[...].astype(MXU), b_ref[...].astype(MXU), dn, preferred_element_type=F32)

        def finish(total):
            if add_ref is not None:
                total = total + add_ref[...]
            o_ref[...] = total.astype(o_ref.dtype)

        if gk == 1:
            finish(part)
        else:
            @pl.when(kk == 0)
            def _():
                acc_ref[...] = part

            @pl.when(jnp.logical_and(kk > 0, kk < gk - 1))
            def _():
                acc_ref[...] += part

            @pl.when(kk == gk - 1)
            def _():
                finish(acc_ref[...] + part)

    (out,), side_results = _pallas(
        body,
        name=name,
        args=args,
        grid=(gm, gn, gk),
        in_specs=in_specs,
        out_specs=[o_spec],
        out_shape=[out_shape],
        scratch_shapes=[pltpu.VMEM((tm, tn) if gk > 1 else (8, 128), F32)],
        semantics=("parallel", "parallel", "arbitrary"),
        vmem_mib=vmem_mib,
        sides=sides,
        aliases=aliases,
    )
    return (out, side_results) if sides else out


def _rms_fwd(h, gains, name, tm=384):
    lp, d = h.shape
    tm = _tile(lp, tm)
    n = len(gains)

    def body(*refs):
        h_ref = refs[0]
        g_refs = refs[1:1 + n]
        o_refs = refs[1 + n:]
        x = h_ref[...]
        y = x * lax.rsqrt(jnp.mean(x * x, axis=-1, keepdims=True) + EPS)
        for g_ref, o_ref in zip(g_refs, o_refs):
            o_ref[...] = (y * g_ref[...]).astype(o_ref.dtype)

    row = pl.BlockSpec((tm, d), lambda i: (i, 0))
    vec = pl.BlockSpec((1, d), lambda i: (0, 0))
    return pl.pallas_call(
        body,
        name=name,
        grid=(lp // tm,),
        in_specs=[row] + [vec] * n,
        out_specs=[row] * n,
        out_shape=[jax.ShapeDtypeStruct((lp, d), MXU)] * n,
        compiler_params=_cparams(("parallel",), 40),
    )(h, *gains)


def _embed(x, meta, gain, name="embed"):
    seq, d = x.shape
    lp = ROW_PAD + N_META + seq
    blk = ROW_PAD + N_META

    def body(x_ref, meta_ref, g_ref, h_ref, o_ref):
        i = pl.program_id(0)

        @pl.when(i == 0)
        def _():
            h_ref[0:ROW_PAD, :] = jnp.zeros((ROW_PAD, d), F32)
            h_ref[ROW_PAD:blk, :] = meta_ref[...]

        @pl.when(i > 0)
        def _():
            h_ref[...] = x_ref[...]

        h = h_ref[...]
        y = h * lax.rsqrt(jnp.mean(h * h, axis=-1, keepdims=True) + EPS)
        o_ref[...] = (y * g_ref[...]).astype(o_ref.dtype)

    row = pl.BlockSpec((blk, d), lambda i: (i, 0))
    return pl.pallas_call(
        body,
        name=name,
        grid=(lp // blk,),
        in_specs=[pl.BlockSpec((blk, d), lambda i: (jnp.maximum(i - 1, 0), 0)),
                  pl.BlockSpec((N_META, d), lambda i: (0, 0)), pl.BlockSpec((1, d), lambda i: (0, 0))],
        out_specs=[row, row],
        out_shape=[jax.ShapeDtypeStruct((lp, d), F32), jax.ShapeDtypeStruct((lp, d), MXU)],
        compiler_params=_cparams(("arbitrary",), 32),
    )(x, meta, gain)


def _embed_bwd(h, gain, dy, res, name="embed_bwd"):
    lp, d = h.shape
    blk = ROW_PAD + N_META

    def body(h_ref, res_ref, g_ref, dy_ref, dx_ref, dmeta_ref, dg_ref):
        i = pl.program_id(0)
        x = h_ref[...]
        rstd = lax.rsqrt(jnp.mean(x * x, axis=-1, keepdims=True) + EPS)
        xhat = x * rstd
        dy = dy_ref[...]
        gdy = dy * g_ref[...]
        dh = res_ref[...] + rstd * (gdy - xhat * jnp.mean(gdy * xhat, axis=-1, keepdims=True))
        part = jnp.sum(dy * xhat, axis=0, keepdims=True)

        @pl.when(i == 0)
        def _():
            dmeta_ref[...] = dh[ROW_PAD:blk, :]
            dg_ref[...] = part

        @pl.when(i > 0)
        def _():
            dx_ref[...] = dh
            dg_ref[...] += part

    row = pl.BlockSpec((blk, d), lambda i: (i, 0))
    vec = pl.BlockSpec((1, d), lambda i: (0, 0))
    return pl.pallas_call(
        body,
        name=name,
        grid=(lp // blk,),
        in_specs=[row, row, vec, row],
        out_specs=[pl.BlockSpec((blk, d), lambda i: (jnp.maximum(i - 1, 0), 0)),
                   pl.BlockSpec((N_META, d), lambda i: (0, 0)), vec],
        out_shape=[jax.ShapeDtypeStruct((lp - blk, d), F32), jax.ShapeDtypeStruct((N_META, d), F32),
                   jax.ShapeDtypeStruct((1, d), F32)],
        compiler_params=_cparams(("arbitrary",), 32),
    )(h, res, gain, dy)


def _rms_bwd(h, gains, dys, res, name, tm=384):
    lp, d = h.shape
    tm = _tile(lp, tm)
    n = len(gains)

    def body(*refs):
        h_ref, res_ref = refs[0], refs[1]
        g_refs = refs[2:2 + n]
        dy_refs = refs[2 + n:2 + 2 * n]
        dh_ref = refs[2 + 2 * n]
        dg_refs = refs[3 + 2 * n:]
        i = pl.program_id(0)
        x = h_ref[...]
        rstd = lax.rsqrt(jnp.mean(x * x, axis=-1, keepdims=True) + EPS)
        xhat = x * rstd
        dh = res_ref[...]
        for g_ref, dy_ref, dg_ref in zip(g_refs, dy_refs, dg_refs):
            dy = dy_ref[...]
            gdy = dy * g_ref[...]
            dh = dh + rstd * (gdy - xhat * jnp.mean(gdy * xhat, axis=-1, keepdims=True))
            part = jnp.sum(dy * xhat, axis=0, keepdims=True)

            @pl.when(i == 0)
            def _():
                dg_ref[...] = part

            @pl.when(i > 0)
            def _():
                dg_ref[...] += part

        dh_ref[...] = dh

    row = pl.BlockSpec((tm, d), lambda i: (i, 0))
    vec = pl.BlockSpec((1, d), lambda i: (0, 0))
    outs = pl.pallas_call(
        body,
        name=name,
        grid=(lp // tm,),
        in_specs=[row, row] + [vec] * n + [row] * n,
        out_specs=[row] + [vec] * n,
        out_shape=[jax.ShapeDtypeStruct((lp, d), F32)] + [jax.ShapeDtypeStruct((1, d), F32)] * n,
        compiler_params=_cparams(("arbitrary",), 56),
    )(h, res, *gains, *dys)
    return outs[0], list(outs[1:])


def _lb_fwd(gamma):
    def body(g_ref, lb_ref):
        g = g_ref[...]
        e = jnp.exp(g - jnp.max(g, axis=0, keepdims=True))
        lb_ref[...] = (e / jnp.sum(e, axis=0, keepdims=True))[0:1, :]

    return pl.pallas_call(body, name="lb_fwd", out_shape=jax.ShapeDtypeStruct((1, gamma.shape[1]), F32))(gamma)


def _lb_bwd(gamma, dlb):
    def body(g_ref, dlb_ref, dg_ref):
        g = g_ref[...]
        e = jnp.exp(g - jnp.max(g, axis=0, keepdims=True))
        s = e / jnp.sum(e, axis=0, keepdims=True)
        rows = lax.broadcasted_iota(jnp.int32, g.shape, 0)
        ds = jnp.where(rows == 0, dlb_ref[...], 0.0)
        dg_ref[...] = s * (ds - jnp.sum(s * ds, axis=0, keepdims=True))

    return pl.pallas_call(body, name="lb_bwd", out_shape=jax.ShapeDtypeStruct(gamma.shape, F32))(gamma, dlb)


def _tri(n, lower):
    r = lax.broadcasted_iota(jnp.int32, (n, n), 0)
    c = lax.broadcasted_iota(jnp.int32, (n, n), 1)
    return jnp.where((r >= c) if lower else (r <= c), 1.0, 0.0).astype(F32)


def _group(nc, most):
    return max(u for u in range(1, most + 1) if nc % u == 0)


def _running_sum(tri, x):
    hi = x.astype(MXU)
    rest = x - hi.astype(F32)
    mid = rest.astype(MXU)
    lo = (rest - mid.astype(F32)).astype(MXU)
    return _dot(tri, hi) + _dot(tri, mid) + _dot(tri, lo)


def _causal(n):
    r = lax.broadcasted_iota(jnp.int32, (n, n), 0)
    c = lax.broadcasted_iota(jnp.int32, (n, n), 1)
    return r >= c


def _chunk_gates(u_ref, lb, c):
    sl = pl.ds(pl.multiple_of(c * CHUNK, CHUNK), CHUNK)
    valid = (c * CHUNK + lax.broadcasted_iota(jnp.int32, (CHUNK, HEAD), 0)) >= ROW_PAD
    uq = u_ref[0, sl, :]
    uf = u_ref[1, sl, :]
    sq = _sigmoid(uq)
    sf = _sigmoid(uf)
    fg = lb + (1.0 - lb) * sf
    return dict(sl=sl, valid=valid, uq=uq, sq=sq, sf=sf, fg=fg, q=jnp.where(valid, uq * sq, 0.0),
                logf=jnp.where(valid, jnp.log(fg), 0.0), k=jnp.where(valid, 1.0 - fg, 0.0),
                v=jnp.where(valid, u_ref[2, sl, :], 0.0))


def _chunk_decays(x, b):
    b_last = b[CHUNK - 1:CHUNK, :]
    b_mid = b[CHUNK // 2 - 1:CHUNK // 2, :]
    e_qi = jnp.exp(b - b_mid)
    e_ki = jnp.exp(b_mid - b)
    e_kd = jnp.exp(b_last - b)
    e_qe = jnp.exp(b)
    q, k = x["q"], x["k"]
    return dict(x, e_qi=e_qi, e_ki=e_ki, e_kd=e_kd, e_qe=e_qe, qi=q * e_qi, ki=k * e_ki, kd=k * e_kd, qe=q * e_qe,
                decay=jnp.exp(b_last))


def _chunks(u_ref, lb, tri_lower, cs):
    gates = [_chunk_gates(u_ref, lb, c) for c in cs]
    sums = [_running_sum(tri_lower, x["logf"]) for x in gates]
    return [_chunk_decays(x, b) for x, b in zip(gates, sums)]


def _hgrn2_fwd(u4, lb, g_out, name="hgrn2_fwd", sides=()):
    _, lp, d = u4.shape
    nh, nc = d // HEAD, lp // CHUNK
    per = _group(nc, 22)

    def body(u_ref, lb_ref, g_ref, o_ref, og_ref):
        lb_v = lb_ref[...]
        g = g_ref[...]

        tri_lower = _tri(CHUNK, True).astype(MXU)
        causal = _causal(CHUNK)

        def step(i, st):
            xs = _chunks(u_ref, lb_v, tri_lower, [i * per + u for u in range(per)])
            scores = [_dot(x["qi"], x["ki"], NT) for x in xs]
            updates = [_dot(x["v"], x["kd"], TN) for x in xs]
            states = []
            for x, upd in zip(xs, updates):
                states.append(st)
                st = x["decay"] * st + upd
            outs = [_dot(jnp.where(causal, a, 0.0), x["v"]) + _dot(x["qe"], s, NT)
                    for x, a, s in zip(xs, scores, states)]
            for x, o in zip(xs, outs):
                o_ref[x["sl"], :] = o
                on = o * lax.rsqrt(jnp.mean(o * o, axis=-1, keepdims=True) + EPS) * g
                z = u_ref[3, x["sl"], :]
                og_ref[x["sl"], :] = (on * (z * _sigmoid(z))).astype(og_ref.dtype)
            return st

        lax.fori_loop(0, nc // per, step, jnp.zeros((HEAD, HEAD), F32))

    slab = pl.BlockSpec((lp, HEAD), lambda h: (0, h))
    vec = pl.BlockSpec((1, HEAD), lambda h: (0, h))
    outs, side_results = _pallas(
        body,
        name=name,
        args=(u4, lb, g_out),
        grid=(nh,),
        in_specs=[pl.BlockSpec((4, lp, HEAD), lambda h: (0, 0, h)), vec, vec],
        out_specs=[slab, slab],
        out_shape=[jax.ShapeDtypeStruct((lp, d), F32), jax.ShapeDtypeStruct((lp, d), MXU)],
        semantics=("parallel",),
        vmem_mib=48,
        sides=sides,
    )
    return (*outs, side_results) if sides else tuple(outs)


def _hgrn2_bwd(u4, o, d_og, lb, g_out, name="hgrn2_bwd", sides=()):
    _, lp, d = u4.shape
    nh, nc = d // HEAD, lp // CHUNK
    per = _group(nc, 11)

    def body(u_ref, o_ref, dog_ref, lb_ref, g_ref, du_ref, dlb_ref, dg_ref, st_ref, do_ref):
        lb_v = lb_ref[...]
        g = g_ref[...]

        tri_lower = _tri(CHUNK, True).astype(MXU)
        tri_upper = _tri(CHUNK, False).astype(MXU)
        causal = _causal(CHUNK)

        def fwd_step(i, carry):
            st, dg_acc = carry
            cs = [i * per + u for u in range(per)]
            xs = _chunks(u_ref, lb_v, tri_lower, cs)
            updates = [_dot(x["v"], x["kd"], TN) for x in xs]
            for c, x, upd in zip(cs, xs, updates):
                st_ref[c] = st
                st = x["decay"] * st + upd
            for x in xs:
                sl = x["sl"]
                ov = o_ref[sl, :]
                rstd = lax.rsqrt(jnp.mean(ov * ov, axis=-1, keepdims=True) + EPS)
                on = ov * rstd
                z = u_ref[3, sl, :]
                sz = _sigmoid(z)
                dog = dog_ref[sl, :]
                dy = dog * (z * sz)
                dz = dog * (on * g) * (sz * (1.0 + z * (1.0 - sz)))
                du_ref[3, sl, :] = dz.astype(du_ref.dtype)
                gdy = dy * g
                do = rstd * (gdy - on * jnp.mean(gdy * on, axis=-1, keepdims=True))
                do_ref[sl, :] = jnp.where(x["valid"], do, 0.0)
                dg_acc = dg_acc + jnp.sum(dy * on, axis=0, keepdims=True)
            return st, dg_acc

        _, dg_tot = lax.fori_loop(0, nc // per, fwd_step, (jnp.zeros((HEAD, HEAD), F32), jnp.zeros((1, HEAD), F32)))
        dg_ref[...] = dg_tot

        def bwd_step(i, carry):
            gt, dlb_acc = carry
            cs = [nc - 1 - (i * per + u) for u in range(per)]
            xs = _chunks(u_ref, lb_v, tri_lower, cs)
            dos = [do_ref[x["sl"], :] for x in xs]
            sts = [st_ref[c] for c in cs]
            scores = [jnp.where(causal, _dot(x["qi"], x["ki"], NT), 0.0) for x in xs]
            d_scores = [jnp.where(causal, _dot(do, x["v"], NT), 0.0) for x, do in zip(xs, dos)]
            d_qes = [_dot(do, st) for do, st in zip(dos, sts)]
            g_updates = [_dot(do, x["qe"], TN) for x, do in zip(xs, dos)]
            gts = []
            for x, upd in zip(xs, g_updates):
                gts.append(gt)
                gt = x["decay"] * gt + upd
            d_kds = [_dot(x["v"], g_) for x, g_ in zip(xs, gts)]
            dvs = [_dot(x["kd"], g_, NT) + _dot(a, do, TN) for x, g_, a, do in zip(xs, gts, scores, dos)]
            d_qis = [_dot(d_a, x["ki"]) for x, d_a in zip(xs, d_scores)]
            d_kis = [_dot(d_a, x["qi"], TN) for x, d_a in zip(xs, d_scores)]
            rows = lax.broadcasted_iota(jnp.int32, (CHUNK, HEAD), 0)
            dbs = []
            for x, g_, st, d_qi, d_ki, d_qe, d_kd in zip(xs, gts, sts, d_qis, d_kis, d_qes, d_kds):
                t_qi, t_ki, t_qe, t_kd = d_qi * x["qi"], d_ki * x["ki"], d_qe * x["qe"], d_kd * x["kd"]
                d_decay = jnp.sum(g_ * st, axis=0, keepdims=True)
                d_mid = jnp.sum(t_ki - t_qi, axis=0, keepdims=True)
                d_last = jnp.sum(t_kd, axis=0, keepdims=True) + d_decay * x["decay"]
                dbs.append(t_qi - t_ki + t_qe - t_kd + jnp.where(rows == CHUNK // 2 - 1, d_mid, 0.0)
                           + jnp.where(rows == CHUNK - 1, d_last, 0.0))
            dlogfs = [_running_sum(tri_upper, db) for db in dbs]
            for x, dlogf, dv, d_qi, d_ki, d_qe, d_kd in zip(xs, dlogfs, dvs, d_qis, d_kis, d_qes, d_kds):
                sl = x["sl"]
                dq = d_qi * x["e_qi"] + d_qe * x["e_qe"]
                dk = d_ki * x["e_ki"] + d_kd * x["e_kd"]
                valid, sq, sf, uq = x["valid"], x["sq"], x["sf"], x["uq"]
                dfg = jnp.where(valid, dlogf / x["fg"] - dk, 0.0)
                du_ref[0, sl, :] = jnp.where(valid, dq * (sq * (1.0 + uq * (1.0 - sq))), 0.0).astype(du_ref.dtype)
                du_ref[1, sl, :] = (dfg * (1.0 - lb_v) * (sf * (1.0 - sf))).astype(du_ref.dtype)
                du_ref[2, sl, :] = jnp.where(valid, dv, 0.0).astype(du_ref.dtype)
                dlb_acc = dlb_acc + jnp.sum(dfg * (1.0 - sf), axis=0, keepdims=True)
            return gt, dlb_acc

        _, dlb_tot = lax.fori_loop(0, nc // per, bwd_step, (jnp.zeros((HEAD, HEAD), F32), jnp.zeros((1, HEAD), F32)))
        dlb_ref[...] = dlb_tot

    slab = pl.BlockSpec((lp, HEAD), lambda h: (0, h))
    vec = pl.BlockSpec((1, HEAD), lambda h: (0, h))
    quad = pl.BlockSpec((4, lp, HEAD), lambda h: (0, 0, h))
    outs, side_results = _pallas(
        body,
        name=name,
        args=(u4, o, d_og, lb, g_out),
        grid=(nh,),
        in_specs=[quad, slab, slab, vec, vec],
        out_specs=[quad, vec, vec],
        out_shape=[jax.ShapeDtypeStruct((4, lp, d), MXU), jax.ShapeDtypeStruct((1, d), F32),
                   jax.ShapeDtypeStruct((1, d), F32)],
        scratch_shapes=[pltpu.VMEM((nc, HEAD, HEAD), F32), pltpu.VMEM((lp, HEAD), F32)],
        semantics=("parallel",),
        vmem_mib=58,
        sides=sides,
    )
    return (*outs, side_results) if sides else tuple(outs)


WIDE = 2 * HEAD
INV_SCALE = HEAD ** 0.5


def _split3(x):
    hi = x.astype(MXU).astype(F32)
    rest = x - hi
    mid = rest.astype(MXU).astype(F32)
    return hi, mid, (rest - mid).astype(MXU).astype(F32)


def _extra_cols(rows, first, second):
    lane = lax.broadcasted_iota(jnp.int32, (rows, HEAD), 1)
    out = jnp.where(lane < 6, 1.0, 0.0).astype(F32)
    for base, terms in ((0, first), (3, second)):
        if terms is not None:
            for j, term in enumerate(terms):
                out = jnp.where(lane == base + j, term, out)
    return out


def _head_col(a, h):
    lane = lax.broadcasted_iota(jnp.int32, a.shape, 1)
    return jnp.sum(jnp.where(lane == h, a, 0.0), axis=-1, keepdims=True)


def _attn_operands(ub, ukv, g_q, g_k, f_cum, name="attn_operands", tm=384):
    _, lp, d = ub.shape
    tm = _tile(lp, tm)
    nh = d // HEAD

    def body(q_ref, k_ref, v_ref, gq_ref, gk_ref, f_ref, qa_ref, ka_ref, va_ref):
        i = pl.program_id(0)
        f = f_ref[...]
        is_pad = (i * tm + lax.broadcasted_iota(jnp.int32, (tm, 1), 0)) < ROW_PAD
        ones_only = _extra_cols(tm, None, (0.0, 0.0, 0.0)).astype(MXU)
        for h in range(nh):
            hs = slice(h * HEAD, (h + 1) * HEAD)
            lo, hi = h * WIDE, h * WIDE + HEAD
            f_h = _head_col(f, h)
            for x_ref, g_ref, o_ref in ((q_ref, gq_ref, qa_ref), (k_ref, gk_ref, ka_ref)):
                x = x_ref[:, hs]
                y = x * lax.rsqrt(jnp.mean(x * x, axis=-1, keepdims=True) + EPS)
                o_ref[:, lo:hi] = (y * g_ref[:, hs]).astype(o_ref.dtype)
            qa_ref[:, hi:hi + HEAD] = _extra_cols(tm, _split3(f_h * INV_SCALE), None).astype(MXU)
            f_key = jnp.where(is_pad, -MASK_VALUE, f_h)
            ka_ref[:, hi:hi + HEAD] = _extra_cols(tm, None, _split3(-f_key * INV_SCALE)).astype(MXU)
            va_ref[:, lo:hi] = v_ref[:, hs].astype(MXU)
            va_ref[:, hi:hi + HEAD] = ones_only

    wide = pl.BlockSpec((tm, nh * WIDE), lambda i: (i, 0))
    vec = pl.BlockSpec((1, d), lambda i: (0, 0))
    return pl.pallas_call(
        body,
        name=name,
        grid=(lp // tm,),
        in_specs=[pl.BlockSpec((None, tm, d), lambda i: (0, i, 0)), pl.BlockSpec((None, tm, d), lambda i: (0, i, 0)),
                  pl.BlockSpec((None, tm, d), lambda i: (1, i, 0)), vec, vec, pl.BlockSpec((tm, 128), lambda i: (i, 0))],
        out_specs=[wide, wide, wide],
        out_shape=[jax.ShapeDtypeStruct((lp, nh * WIDE), MXU)] * 3,
        compiler_params=_cparams(("parallel",), 56),
    )(ub, ukv, ukv, g_q, g_k, f_cum)


def _head_rms_bwd_tile(x, g, dy):
    rstd = lax.rsqrt(jnp.mean(x * x, axis=-1, keepdims=True) + EPS)
    xhat = x * rstd
    gdy = dy * g
    dx = rstd * (gdy - xhat * jnp.mean(gdy * xhat, axis=-1, keepdims=True))
    return dx, jnp.sum(dy * xhat, axis=0, keepdims=True)


def _fgate_fwd(ufl, b_f):
    lp = ufl.shape[0]
    nb = lp // 128

    def body(u_ref, b_ref, f_ref):
        def step(i, carry):
            sl = pl.ds(pl.multiple_of(i * 128, 128), 128)
            valid = (i * 128 + lax.broadcasted_iota(jnp.int32, (128, 128), 0)) >= ROW_PAD
            x = u_ref[sl, :] + b_ref[...]
            logf = jnp.where(valid, jnp.minimum(x, 0.0) - jnp.log(1.0 + jnp.exp(-jnp.abs(x))), 0.0)
            f = _dot_exact(_tri(128, True), logf) + carry
            f_ref[sl, :] = f
            return f[127:128, :]

        lax.fori_loop(0, nb, step, jnp.zeros((1, 128), F32))

    return pl.pallas_call(body, name="fgate_fwd", out_shape=jax.ShapeDtypeStruct((lp, 128), F32))(ufl, b_f)


def _fgate_bwd(ufl, b_f, d_f):
    lp = ufl.shape[0]
    nb = lp // 128

    def body(u_ref, b_ref, df_ref, du_ref, db_ref):
        def step(j, carry):
            later, db_acc = carry
            i = nb - 1 - j
            sl = pl.ds(pl.multiple_of(i * 128, 128), 128)
            valid = (i * 128 + lax.broadcasted_iota(jnp.int32, (128, 128), 0)) >= ROW_PAD
            x = u_ref[sl, :] + b_ref[...]
            df = df_ref[sl, :]
            dlogf = _dot_exact(_tri(128, False), df) + later
            dx = jnp.where(valid, dlogf * _sigmoid(-x), 0.0)
            du_ref[sl, :] = dx.astype(du_ref.dtype)
            return later + jnp.sum(df, axis=0, keepdims=True), db_acc + jnp.sum(dx, axis=0, keepdims=True)

        _, db_tot = lax.fori_loop(0, nb, step, (jnp.zeros((1, 128), F32), jnp.zeros((1, 128), F32)))
        db_ref[...] = db_tot

    return pl.pallas_call(
        body, name="fgate_bwd",
        out_shape=[jax.ShapeDtypeStruct((lp, 128), MXU), jax.ShapeDtypeStruct((1, 128), F32)],
    )(ufl, b_f, d_f)


STRIP = 32
PAIR = 2


def _strip_causal(r, t):
    row = r + lax.broadcasted_iota(jnp.int32, (STRIP, t), 0)
    col = lax.broadcasted_iota(jnp.int32, (STRIP, t), 1)
    return col <= row


def _fox_fwd(qa, ka, va, ub, name="fox_fwd", t=384):
    lp = qa.shape[0]
    d = ub.shape[2]
    t = _tile(lp, t)
    nh, nq = d // HEAD, lp // t
    scale = HEAD ** -0.5

    assert nh % PAIR == 0
    heads = range(PAIR)

    def body(q_ref, k_ref, v_ref, z_ref, o_ref, olo_ref, og_ref, qb_ref,
             s_ref, p_ref, m_ref, a_ref, l_ref, acc_ref):
        qb = pl.program_id(1)
        m_ref[...] = jnp.full((PAIR, t, 1), MASK_VALUE, F32)
        l_ref[...] = jnp.zeros((PAIR, t, 128), F32)
        acc_ref[...] = jnp.zeros((PAIR, 2 * t, HEAD), F32)
        p_ref[1] = jnp.zeros((PAIR, 2 * t, t), MXU)
        a_ref[1] = jnp.ones((PAIR, t, 1), F32)

        def scores(kb, buf):
            ks = pl.ds(pl.multiple_of(kb * t, t), t)
            for j in heads:
                ws = slice(j * WIDE, (j + 1) * WIDE)
                s_ref[buf, j] = _dot(q_ref[:, ws], k_ref[ks, ws], NT)

        def weighted_sum(kb, buf):
            ks = pl.ds(pl.multiple_of(kb * t, t), t)
            for j in heads:
                pv = _dot(p_ref[buf, j], v_ref[ks, j * WIDE:j * WIDE + HEAD])
                alpha = a_ref[buf, j]
                acc_ref[j, 0:t, :] = alpha * acc_ref[j, 0:t, :] + pv[0:t]
                acc_ref[j, t:2 * t, :] = alpha * acc_ref[j, t:2 * t, :] + pv[t:2 * t]

        def softmax_update(buf, diagonal):
            for j in heads:
                for r in range(0, t, STRIP):
                    rs = slice(r, r + STRIP)
                    x = s_ref[buf, j, rs, :] * scale
                    if diagonal:
                        x = jnp.where(_strip_causal(r, t), x, MASK_VALUE)
                    m_old = m_ref[j, rs, :]
                    m_new = jnp.maximum(m_old, jnp.max(x, axis=-1, keepdims=True))
                    alpha = jnp.exp(m_old - m_new)
                    p = jnp.exp(x - m_new)
                    m_ref[j, rs, :] = m_new
                    a_ref[buf, j, rs, :] = alpha
                    l_ref[j, rs, :] = alpha * l_ref[j, rs, :] + sum(p[:, c:c + 128] for c in range(0, t, 128))
                    p_hi = p.astype(MXU)
                    p_ref[buf, j, rs, :] = p_hi
                    p_ref[buf, j, t + r:t + r + STRIP, :] = (p - p_hi.astype(F32)).astype(MXU)

        def off_diagonal(kb, cur):
            weighted_sum(jnp.maximum(kb - 1, 0), 1 - cur)
            scores(kb + 1, 1 - cur)
            softmax_update(cur, False)

        def diagonal(cur):
            weighted_sum(jnp.maximum(qb - 1, 0), 1 - cur)
            softmax_update(cur, True)
            weighted_sum(qb, cur)

        def two_blocks(i, carry):
            off_diagonal(2 * i, 0)
            off_diagonal(2 * i + 1, 1)
            return carry

        scores(0, 0)
        lax.fori_loop(0, qb // 2, two_blocks, 0)

        @pl.when(lax.rem(qb, 2) == 0)
        def _():
            diagonal(0)

        @pl.when(lax.rem(qb, 2) == 1)
        def _():
            off_diagonal(qb - 1, 0)
            diagonal(1)

        is_pad = (qb * t + lax.broadcasted_iota(jnp.int32, (t, 1), 0)) < ROW_PAD
        for j in heads:
            hs = slice(j * HEAD, (j + 1) * HEAD)
            l = jnp.sum(l_ref[j], axis=-1, keepdims=True)
            o = acc_ref[j, 0:t, :] / l
            o_ref[:, hs] = o
            olo_ref[:, hs] = acc_ref[j, t:2 * t, :] / l
            z = z_ref[:, hs]
            og_ref[:, hs] = (o * (z * _sigmoid(z))).astype(og_ref.dtype)
            extra = q_ref[:, j * WIDE + HEAD:(j + 1) * WIDE].astype(F32)
            f_scaled = extra[:, 0:1] + extra[:, 1:2] + extra[:, 2:3]
            log_term = jnp.where(is_pad, MASK_VALUE * INV_SCALE, f_scaled - (m_ref[j] + jnp.log(l)) * INV_SCALE)
            qb_ref[:, j * WIDE:j * WIDE + HEAD] = q_ref[:, j * WIDE:j * WIDE + HEAD]
            qb_ref[:, j * WIDE + HEAD:(j + 1) * WIDE] = _extra_cols(t, _split3(log_term), None).astype(qb_ref.dtype)

    scratch = [pltpu.VMEM((2, PAIR, t, t), F32), pltpu.VMEM((2, PAIR, 2 * t, t), MXU), pltpu.VMEM((PAIR, t, 1), F32),
               pltpu.VMEM((2, PAIR, t, 1), F32), pltpu.VMEM((PAIR, t, 128), F32), pltpu.VMEM((PAIR, 2 * t, HEAD), F32)]
    tile = pl.BlockSpec((t, PAIR * HEAD), lambda g, i: (i, g))
    wide_tile = pl.BlockSpec((t, PAIR * WIDE), lambda g, i: (i, g))
    wide_all = pl.BlockSpec((lp, PAIR * WIDE), lambda g, i: (0, g))
    return pl.pallas_call(
        body,
        name=name,
        grid=(nh // PAIR, nq),
        in_specs=[wide_tile, wide_all, wide_all, pl.BlockSpec((None, t, PAIR * HEAD), lambda g, i: (1, i, g))],
        out_specs=[tile, tile, tile, wide_tile],
        out_shape=[jax.ShapeDtypeStruct((lp, d), F32), jax.ShapeDtypeStruct((lp, d), F32),
                   jax.ShapeDtypeStruct((lp, d), MXU), jax.ShapeDtypeStruct((lp, nh * WIDE), MXU)],
        scratch_shapes=scratch,
        compiler_params=_cparams(("parallel", "arbitrary"), 48),
    )(qa, ka, va, ub)


def _fox_bwd(qb, ka, va, doa, ukv, g_k, ub, g_q, name="fox_bwd", t=384, sides=()):
    lp = qb.shape[0]
    d = ukv.shape[2]
    t = _tile(lp, t)
    nh, nk = d // HEAD, lp // t
    scale = HEAD ** -0.5

    assert nh % PAIR == 0
    heads = range(PAIR)

    def body(q_ref, do_ref, k_ref, v_ref, kraw_ref, gk_ref, qraw_ref, gq_ref, dqraw_ref, dgq_ref, dukv_ref, dgk_ref,
             dfk_ref, s_ref, dp_ref, p_ref, ds_ref, col_ref, dk_ref, dv_ref, dq_ref):
        kb = pl.program_id(1)

        @pl.when(kb == 0)
        def _():
            dq_ref[...] = jnp.zeros_like(dq_ref)
            dgk_ref[...] = jnp.zeros_like(dgk_ref)

        dk_ref[...] = jnp.zeros_like(dk_ref)
        dv_ref[...] = jnp.zeros_like(dv_ref)
        col_ref[...] = jnp.zeros_like(col_ref)

        def step(qb, diagonal):
            qs = pl.ds(pl.multiple_of(qb * t, t), t)
            for j in heads:
                ws = slice(j * WIDE, (j + 1) * WIDE)
                s_ref[j] = _dot(q_ref[qs, ws], k_ref[:, ws], NT)
                dp_ref[j] = _dot(do_ref[qs, ws], v_ref[:, ws], NT)
            for j in heads:
                for r in range(0, t, STRIP):
                    rs = slice(r, r + STRIP)
                    x = s_ref[j, rs, :] * scale
                    if diagonal:
                        x = jnp.where(_strip_causal(r, t), x, MASK_VALUE)
                    p = jnp.exp(x)
                    ds = p * dp_ref[j, rs, :]
                    p_ref[j, rs, :] = p.astype(MXU)
                    ds_ref[j, rs, :] = (ds * scale).astype(MXU)
                    col_ref[j] += ds
            for j in heads:
                hs = slice(j * HEAD, (j + 1) * HEAD)
                narrow = slice(j * WIDE, j * WIDE + HEAD)
                dsb = ds_ref[j]
                dv_ref[:, hs] += _dot(p_ref[j], do_ref[qs, narrow], TN)
                dq_ref[qs, hs] += _dot(dsb, k_ref[:, narrow])
                dk_ref[:, hs] += _dot(dsb, q_ref[qs, narrow], TN)

        def off_diagonal(qb, carry):
            step(qb, False)
            return carry

        step(kb, True)
        lax.fori_loop(kb + 1, nk, off_diagonal, 0)
        for j in heads:
            hs = slice(j * HEAD, (j + 1) * HEAD)
            dfk_ref[j] = -jnp.sum(col_ref[j], axis=0, keepdims=True)
            dx, dg = _head_rms_bwd_tile(kraw_ref[:, hs], gk_ref[:, hs], dk_ref[:, hs])
            dukv_ref[0, :, hs] = dx.astype(dukv_ref.dtype)
            dukv_ref[1, :, hs] = dv_ref[:, hs].astype(dukv_ref.dtype)
            dgk_ref[:, hs] += dg

        @pl.when(kb == nk - 1)
        def _():
            dgq_ref[...] = jnp.zeros_like(dgq_ref)

            def rows(c, carry):
                rs = pl.ds(pl.multiple_of(c * t, t), t)
                for j in heads:
                    hs = slice(j * HEAD, (j + 1) * HEAD)
                    dx, dg = _head_rms_bwd_tile(qraw_ref[rs, hs], gq_ref[:, hs], dq_ref[rs, hs])
                    dqraw_ref[rs, hs] = dx.astype(dqraw_ref.dtype)
                    dgq_ref[:, hs] += dg
                return carry

            lax.fori_loop(0, nk, rows, 0)

    scratch = [pltpu.VMEM((PAIR, t, t), F32), pltpu.VMEM((PAIR, t, t), F32), pltpu.VMEM((PAIR, t, t), MXU),
               pltpu.VMEM((PAIR, t, t), MXU), pltpu.VMEM((PAIR, STRIP, t), F32),
               pltpu.VMEM((t, PAIR * HEAD), F32), pltpu.VMEM((t, PAIR * HEAD), F32), pltpu.VMEM((lp, PAIR * HEAD), F32)]
    whole = pl.BlockSpec((lp, PAIR * HEAD), lambda g, j: (0, g))
    wide_all = pl.BlockSpec((lp, PAIR * WIDE), lambda g, j: (0, g))
    wide_tile = pl.BlockSpec((t, PAIR * WIDE), lambda g, j: (j, g))
    vec = pl.BlockSpec((1, PAIR * HEAD), lambda g, j: (0, g))
    outs, side_results = _pallas(
        body,
        name=name,
        args=(qb, doa, ka, va, ukv, g_k, ub, g_q),
        grid=(nh // PAIR, nk),
        in_specs=[wide_all, wide_all, wide_tile, wide_tile,
                  pl.BlockSpec((None, t, PAIR * HEAD), lambda g, j: (0, j, g)), vec,
                  pl.BlockSpec((None, lp, PAIR * HEAD), lambda g, j: (0, 0, g)), vec],
        out_specs=[whole, vec, pl.BlockSpec((2, t, PAIR * HEAD), lambda g, j: (0, j, g)), vec,
                   pl.BlockSpec((PAIR, 1, t), lambda g, j: (g, 0, j))],
        out_shape=[jax.ShapeDtypeStruct((lp, d), MXU), jax.ShapeDtypeStruct((1, d), F32),
                   jax.ShapeDtypeStruct((2, lp, d), MXU), jax.ShapeDtypeStruct((1, d), F32),
                   jax.ShapeDtypeStruct((nh, 1, lp), F32)],
        scratch_shapes=scratch,
        semantics=("parallel", "arbitrary"),
        vmem_mib=48,
        sides=sides,
    )
    return (*outs, side_results) if sides else tuple(outs)


def _fox_do(d_og, o, o_lo, ub, name="fox_do", tm=384):
    lp, d = o.shape
    tm = _tile(lp, tm)
    nh = d // HEAD

    def body(dog_ref, o_ref, olo_ref, z_ref, doa_ref):
        for h in range(nh):
            hs = slice(h * HEAD, (h + 1) * HEAD)
            z = z_ref[:, hs]
            do = (dog_ref[:, hs] * (z * _sigmoid(z))).astype(doa_ref.dtype)
            delta = jnp.sum(do.astype(F32) * (o_ref[:, hs] + olo_ref[:, hs]), axis=-1, keepdims=True)
            doa_ref[:, h * WIDE:h * WIDE + HEAD] = do
            doa_ref[:, h * WIDE + HEAD:(h + 1) * WIDE] = _extra_cols(tm, _split3(-delta), (0.0, 0.0, 0.0)).astype(
                doa_ref.dtype)

    row = pl.BlockSpec((tm, d), lambda i: (i, 0))
    return pl.pallas_call(
        body,
        name=name,
        grid=(lp // tm,),
        in_specs=[row, row, row, pl.BlockSpec((None, tm, d), lambda i: (1, i, 0))],
        out_specs=pl.BlockSpec((tm, nh * WIDE), lambda i: (i, 0)),
        out_shape=jax.ShapeDtypeStruct((lp, nh * WIDE), MXU),
        compiler_params=_cparams(("parallel",), 56),
    )(d_og, o, o_lo, ub)


def _fox_gate_bwd(ub, d_og, o, dq, name="fox_gate_bwd", tm=384):
    _, lp, d = ub.shape
    tm = _tile(lp, tm)

    def body(z_ref, dog_ref, o_ref, dq_ref, dub_ref):
        z = z_ref[...]
        sz = _sigmoid(z)
        dub_ref[0] = dq_ref[...]
        dub_ref[1] = (dog_ref[...] * o_ref[...] * (sz * (1.0 + z * (1.0 - sz)))).astype(dub_ref.dtype)

    row = pl.BlockSpec((tm, d), lambda i: (i, 0))
    return pl.pallas_call(
        body,
        name=name,
        grid=(lp // tm,),
        in_specs=[pl.BlockSpec((None, tm, d), lambda i: (1, i, 0)), row, row, row],
        out_specs=pl.BlockSpec((2, tm, d), lambda i: (0, i, 0)),
        out_shape=jax.ShapeDtypeStruct((2, lp, d), MXU),
        compiler_params=_cparams(("parallel",), 48),
    )(ub, d_og, o, dq)


def _loss(h, target, name="loss_head"):
    lp, d = h.shape
    nb = lp // 128

    def body(h_ref, t_ref, loss_ref, dh_ref, acc_ref):
        i = pl.program_id(0)

        @pl.when(i == 0)
        def _():
            acc_ref[...] = jnp.zeros_like(acc_ref)
            dh_ref[...] = jnp.zeros_like(dh_ref)

        @pl.when(i > 0)
        def _():
            err = h_ref[...] - t_ref[...]
            dh_ref[...] = err * (1.0 / d)
            acc_ref[...] += jnp.sum(jnp.sum(err * err, axis=-1, keepdims=True) * (1.0 / d), axis=0, keepdims=True)

        @pl.when(i == nb - 1)
        def _():
            loss_ref[...] = 0.5 * acc_ref[...]

    return pl.pallas_call(
        body,
        name=name,
        grid=(nb,),
        in_specs=[pl.BlockSpec((128, d), lambda i: (i, 0)),
                  pl.BlockSpec((128, d), lambda i: (jnp.maximum(i - 1, 0), 0))],
        out_specs=[pl.BlockSpec((1, 1), lambda i: (0, 0)), pl.BlockSpec((128, d), lambda i: (i, 0))],
        out_shape=[jax.ShapeDtypeStruct((1, 1), F32), jax.ShapeDtypeStruct((lp, d), F32)],
        scratch_shapes=[pltpu.VMEM((1, 1), F32)],
        compiler_params=_cparams(("arbitrary",), 32),
    )(h, target)


def _adam_math(w, g, m, v):
    m = ADAM_B1 * m + (1.0 - ADAM_B1) * g
    v = ADAM_B2 * v + (1.0 - ADAM_B2) * (g * g)
    m_hat = m / (1.0 - ADAM_B1 ** ADAM_STEP)
    v_hat = v / (1.0 - ADAM_B2 ** ADAM_STEP)
    delta = -ADAM_LR * (m_hat / (jnp.sqrt(v_hat) + ADAM_EPS) + ADAM_WD * w)
    return delta, m, v


def _adamw(parts, w, m, v, name, tm=512):
    n, r, c = parts.shape
    tm = tm if r % tm == 0 else r

    def body(p_ref, w_ref, m_ref, v_ref, g_ref, d_ref, nm_ref, nv_ref):
        g = p_ref[0].astype(F32)
        for j in range(1, n):
            g = g + p_ref[j].astype(F32)
        g_ref[...] = g
        d_ref[...], nm_ref[...], nv_ref[...] = _adam_math(w_ref[...], g, m_ref[...], v_ref[...])

    row = pl.BlockSpec((tm, c), lambda i: (i, 0))
    return pl.pallas_call(
        body,
        name=name,
        grid=(r // tm,),
        in_specs=[pl.BlockSpec((n, tm, c), lambda i: (0, i, 0)), row, row, row],
        out_specs=[row] * 4,
        out_shape=[jax.ShapeDtypeStruct((r, c), F32)] * 4,
        compiler_params=_cparams(("parallel",), 48),
    )(parts, w, m, v)


def _adamw_small(gs, ws, ms, vs, name="adamw_small"):
    n = len(ws)
    flat = [jnp.reshape(a, (-1, a.shape[-1])) for group in (gs, ws, ms, vs) for a in group]

    def body(*refs):
        g_refs, w_refs, m_refs, v_refs = (refs[k * n:(k + 1) * n] for k in range(4))
        outs = refs[4 * n:]
        for i in range(n):
            d_new, m_new, v_new = _adam_math(w_refs[i][...], g_refs[i][...], m_refs[i][...], v_refs[i][...])
            outs[i][...], outs[n + i][...], outs[2 * n + i][...] = d_new, m_new, v_new

    res = pl.pallas_call(
        body, name=name, out_shape=[jax.ShapeDtypeStruct(a.shape, F32) for a in flat[n:2 * n]] * 3)(*flat)
    return [[jnp.reshape(r, w.shape) for r, w in zip(res[k * n:(k + 1) * n], ws)] for k in range(3)]


def _place():
    x, y, c = lax.axis_index("x"), lax.axis_index("y"), lax.axis_index("c")
    return x, y, c


def _other_chips(x, y):
    return [(1 - x, y), (x, 1 - y), (1 - x, 1 - y)]


def _any_spec():
    return pl.BlockSpec(memory_space=pl.ANY)


class _Side:
    def __init__(self, ins, outs, sems, start, finish, aliases=None):
        self.ins, self.outs, self.sems = list(ins), list(outs), sems
        self.start, self.finish = start, finish
        self.aliases = dict(aliases or {})


def _pallas(body, *, name, out_shape, args=(), grid=(), in_specs=(), out_specs=(), scratch_shapes=(),
            semantics=(), vmem_mib=None, sides=(), aliases=None):
    n_in, n_out, n_scr = len(args), len(out_shape), len(scratch_shapes)
    side_ins = [a for s in sides for a in s.ins]
    side_outs = [o for s in sides for o in s.outs]
    side_sems = [pltpu.SemaphoreType.DMA((max(k, 1),)) for s in sides for k in s.sems]
    aliases, in_at, out_at = dict(aliases or {}), n_in, n_out
    for s in sides:
        aliases.update({in_at + i: out_at + o for i, o in s.aliases.items()})
        in_at, out_at = in_at + len(s.ins), out_at + len(s.outs)

    def wrapped(*refs):
        at = [0]

        def take(k):
            got = refs[at[0]:at[0] + k]
            at[0] += k
            return got

        main_in = take(n_in)
        s_in = [take(len(s.ins)) for s in sides]
        main_out = take(n_out)
        s_out = [take(len(s.outs)) for s in sides]
        main_scr = take(n_scr)
        s_sem = [take(3) for s in sides]

        def run(stage):
            for s, i_, o_, m_ in zip(sides, s_in, s_out, s_sem):
                getattr(s, stage)(i_, o_, *m_)

        first = last = None
        for k, g in enumerate(grid):
            i = pl.program_id(k)
            first = (i == 0) if first is None else jnp.logical_and(first, i == 0)
            last = (i == g - 1) if last is None else jnp.logical_and(last, i == g - 1)
        if sides:
            run("start") if first is None else pl.when(first)(lambda: run("start"))
        body(*main_in, *main_out, *main_scr)
        if sides:
            run("finish") if last is None else pl.when(last)(lambda: run("finish"))

    kw = {}
    if grid:
        kw["grid"] = grid
    if semantics or vmem_mib:
        sem = tuple("arbitrary" for _ in grid) if sides else tuple(semantics)
        kw["compiler_params"] = pltpu.CompilerParams(
            dimension_semantics=sem or None, vmem_limit_bytes=vmem_mib * MIB if vmem_mib else None)
    res = pl.pallas_call(
        wrapped,
        name=name,
        in_specs=list(in_specs) + [_any_spec()] * len(side_ins),
        out_specs=list(out_specs) + [_any_spec()] * len(side_outs),
        out_shape=list(out_shape) + side_outs,
        scratch_shapes=list(scratch_shapes) + side_sems,
        input_output_aliases=aliases,
        **kw,
    )(*args, *side_ins)
    main, rest, per_side = list(res[:n_out]), list(res[n_out:]), []
    for s in sides:
        per_side.append(rest[:len(s.outs)])
        rest = rest[len(s.outs):]
    return main, per_side


def _slot(p):
    return 4 * p[0] + 2 * p[1] + p[2]


def _sibling_side(grads):
    n = len(grads)

    def copies(ins, outs, send_sems, recv_sems):
        x, y, c = _place()
        return [pltpu.make_async_remote_copy(
            src_ref=ins[t].at[2 * chip + (1 - c)], dst_ref=outs[t].at[chip],
            send_sem=send_sems.at[4 * t + chip], recv_sem=recv_sems.at[4 * t + chip],
            device_id=(x, y, 1 - c), device_id_type=MESH) for t in range(n) for chip in range(N_CHIP)]

    def start(ins, outs, send_sems, recv_sems, local_sems):
        for cp in copies(ins, outs, send_sems, recv_sems):
            cp.start()

    def finish(ins, outs, send_sems, recv_sems, local_sems):
        for cp in copies(ins, outs, send_sems, recv_sems):
            cp.wait()

    outs = [jax.ShapeDtypeStruct((N_CHIP,) + g.shape[1:], g.dtype) for g in grads]
    return _Side(grads, outs, (4 * n, 4 * n, 0), start, finish)


def _chips_side(partials):
    n = len(partials)

    def copies(ins, outs, send_sems, recv_sems, local_sems):
        x, y, c = _place()
        my_chip = 2 * x + y
        local = [pltpu.make_async_copy(ins[t].at[my_chip], outs[t].at[my_chip], local_sems.at[t]) for t in range(n)]
        sends, recvs = [], []
        for t in range(n):
            for j, chip in enumerate(_other_chips(x, y)):
                their = 2 * chip[0] + chip[1]
                sems = dict(send_sem=send_sems.at[3 * t + j], recv_sem=recv_sems.at[3 * t + j],
                            device_id=(*chip, c), device_id_type=MESH)
                sends.append(pltpu.make_async_remote_copy(src_ref=ins[t].at[their], dst_ref=outs[t].at[my_chip], **sems))
                recvs.append(pltpu.make_async_remote_copy(src_ref=ins[t].at[my_chip], dst_ref=outs[t].at[their], **sems))
        return local, sends, recvs

    def start(ins, outs, send_sems, recv_sems, local_sems):
        local, sends, _ = copies(ins, outs, send_sems, recv_sems, local_sems)
        for cp in local + sends:
            cp.start()

    def finish(ins, outs, send_sems, recv_sems, local_sems):
        local, sends, recvs = copies(ins, outs, send_sems, recv_sems, local_sems)
        for cp in sends:
            cp.wait_send()
        for cp in recvs:
            cp.wait_recv()
        for cp in local:
            cp.wait()

    outs = [jax.ShapeDtypeStruct(p.shape, p.dtype) for p in partials]
    return _Side(partials, outs, (3 * n, 3 * n, n), start, finish)


def _gather_own_side(blocks):
    n = len(blocks)

    def copies(ins, outs, send_sems, recv_sems, local_sems):
        x, y, c = _place()
        me = (x, y, c)
        peers = [(x, y, 1 - c)] + [(*chip, c) for chip in _other_chips(x, y)]
        local = [pltpu.make_async_copy(ins[t], outs[t].at[_slot(me)], local_sems.at[t]) for t in range(n)]
        sends, recvs = [], []
        for t in range(n):
            for k, peer in enumerate(peers):
                sems = dict(send_sem=send_sems.at[4 * t + k], recv_sem=recv_sems.at[4 * t + k],
                            device_id=peer, device_id_type=MESH)
                sends.append(pltpu.make_async_remote_copy(src_ref=ins[t], dst_ref=outs[t].at[_slot(me)], **sems))
                recvs.append(pltpu.make_async_remote_copy(src_ref=ins[t], dst_ref=outs[t].at[_slot(peer)], **sems))
        return local, sends, recvs

    def start(ins, outs, send_sems, recv_sems, local_sems):
        local, sends, _ = copies(ins, outs, send_sems, recv_sems, local_sems)
        for cp in local + sends:
            cp.start()

    def finish(ins, outs, send_sems, recv_sems, local_sems):
        local, sends, recvs = copies(ins, outs, send_sems, recv_sems, local_sems)
        for cp in sends:
            cp.wait_send()
        for cp in recvs:
            cp.wait_recv()
        for cp in local:
            cp.wait()

    outs = [jax.ShapeDtypeStruct((N_DEV,) + b.shape, b.dtype) for b in blocks]
    return _Side(blocks, outs, (4 * n, 4 * n, n), start, finish)


def _gather_pass_side(gathered):
    n = len(gathered)

    def copies(outs, send_sems, recv_sems):
        x, y, c = _place()
        sends, recvs = [], []
        for t in range(n):
            for j, chip in enumerate(_other_chips(x, y)):
                sems = dict(send_sem=send_sems.at[3 * t + j], recv_sem=recv_sems.at[3 * t + j],
                            device_id=(x, y, 1 - c), device_id_type=MESH)
                mine, theirs = outs[t].at[_slot((*chip, c))], outs[t].at[_slot((*chip, 1 - c))]
                sends.append(pltpu.make_async_remote_copy(src_ref=mine, dst_ref=mine, **sems))
                recvs.append(pltpu.make_async_remote_copy(src_ref=mine, dst_ref=theirs, **sems))
        return sends, recvs

    def start(ins, outs, send_sems, recv_sems, local_sems):
        for cp in copies(outs, send_sems, recv_sems)[0]:
            cp.start()

    def finish(ins, outs, send_sems, recv_sems, local_sems):
        sends, recvs = copies(outs, send_sems, recv_sems)
        for cp in sends:
            cp.wait_send()
        for cp in recvs:
            cp.wait_recv()

    outs = [jax.ShapeDtypeStruct(g.shape, g.dtype) for g in gathered]
    return _Side(gathered, outs, (3 * n, 3 * n, 0), start, finish, aliases={t: t for t in range(n)})


def _alone(side, name):
    return _pallas(lambda: None, name=name, out_shape=[], sides=[side])[1][0]


def _all_gather(blocks, name):
    n = len(blocks)

    def body(*refs):
        in_refs, out_refs = refs[:n], refs[n:2 * n]
        send_sems, recv_sems, local_sems = refs[2 * n:]
        x, y, c = _place()
        me, sibling = (x, y, c), (x, y, 1 - c)
        chips = _other_chips(x, y)

        def slot(p):
            return 4 * p[0] + 2 * p[1] + p[2]

        def copy(t, k, block, to, src=None):
            dst = out_refs[t].at[slot(block)]
            return pltpu.make_async_remote_copy(
                src_ref=dst if src is None else src, dst_ref=dst,
                send_sem=send_sems.at[7 * t + k], recv_sem=recv_sems.at[7 * t + k],
                device_id=to, device_id_type=MESH)

        started = []
        for t in range(n):
            mine = pltpu.make_async_copy(in_refs[t], out_refs[t].at[slot(me)], local_sems.at[t])
            mine.start()
            started.append(mine)
        sends = []
        for t in range(n):
            first = [copy(t, 0, me, sibling, src=in_refs[t])]
            first += [copy(t, 1 + j, me, (*chip, c), src=in_refs[t]) for j, chip in enumerate(chips)]
            for cp in first:
                cp.start()
            sends += first
        for t in range(n):
            for j, chip in enumerate(chips):
                copy(t, 1 + j, (*chip, c), me).wait_recv()
                passed = copy(t, 4 + j, (*chip, c), sibling)
                passed.start()
                sends.append(passed)
        for t in range(n):
            copy(t, 0, sibling, me).wait_recv()
            for j, chip in enumerate(chips):
                copy(t, 4 + j, (*chip, 1 - c), me).wait_recv()
        for cp in sends:
            cp.wait_send()
        for mine in started:
            mine.wait()

    return pl.pallas_call(
        body,
        name=name,
        in_specs=[_any_spec()] * n,
        out_specs=[_any_spec()] * n,
        out_shape=[jax.ShapeDtypeStruct((N_DEV,) + b.shape, b.dtype) for b in blocks],
        scratch_shapes=[pltpu.SemaphoreType.DMA((7 * n,)), pltpu.SemaphoreType.DMA((7 * n,)),
                        pltpu.SemaphoreType.DMA((n,))],
    )(*blocks)


def _pair_sum(g8, got, name, tm=1024):
    _, r, c = g8.shape
    tm = tm if r % tm == 0 else r
    core = lax.axis_index("c")

    def body(core_ref, mine_ref, got_ref, o_ref):
        south_first = core_ref[0] == 0
        a, b = mine_ref[...], got_ref[...]
        o_ref[...] = (jnp.where(south_first, a, b) + jnp.where(south_first, b, a)).astype(o_ref.dtype)

    return pl.pallas_call(
        body,
        name=name,
        grid_spec=pltpu.PrefetchScalarGridSpec(
            num_scalar_prefetch=1,
            grid=(N_CHIP, r // tm),
            in_specs=[pl.BlockSpec((None, tm, c), lambda j, i, core_ref: (2 * j + core_ref[0], i, 0)),
                      pl.BlockSpec((None, tm, c), lambda j, i, core_ref: (j, i, 0))],
            out_specs=pl.BlockSpec((None, tm, c), lambda j, i, core_ref: (j, i, 0)),
        ),
        out_shape=jax.ShapeDtypeStruct((N_CHIP, r, c), PAYLOAD),
        compiler_params=_cparams(("parallel", "parallel"), 32),
    )(jnp.reshape(core, (1,)).astype(jnp.int32), g8, got)


def _all_reduce_small(s, name="small_all_reduce"):
    r, c = s.shape

    def body(s_ref, o_ref, buf_ref, send_sems, recv_sems):
        x, y, c_ = _place()
        me = 4 * x + 2 * y + c_
        buf_ref[me] = s_ref[...]

        def copy(k, slot, peer):
            return pltpu.make_async_remote_copy(
                src_ref=s_ref, dst_ref=buf_ref.at[slot],
                send_sem=send_sems.at[k - 1], recv_sem=recv_sems.at[k - 1],
                device_id=peer, device_id_type=MESH)

        peers = []
        for k in range(1, N_DEV):
            peer = (x ^ ((k >> 2) & 1), y ^ ((k >> 1) & 1), c_ ^ (k & 1))
            peers.append(peer)
            copy(k, me, peer).start()
        for k, peer in zip(range(1, N_DEV), peers):
            cp = copy(k, 4 * peer[0] + 2 * peer[1] + peer[2], peer)
            cp.wait_send()
            cp.wait_recv()
        total = buf_ref[0]
        for j in range(1, N_DEV):
            total = total + buf_ref[j]
        o_ref[...] = total

    return pl.pallas_call(
        body,
        name=name,
        out_shape=jax.ShapeDtypeStruct((r, c), F32),
        in_specs=[pl.BlockSpec(memory_space=pltpu.VMEM)],
        out_specs=pl.BlockSpec(memory_space=pltpu.VMEM),
        scratch_shapes=[pltpu.VMEM((N_DEV, r, c), F32), pltpu.SemaphoreType.DMA((N_DEV - 1,)),
                        pltpu.SemaphoreType.DMA((N_DEV - 1,))],
    )(s)


def _kv_weights(g_kv):
    d = g_kv.shape[1]
    nh = d // HEAD
    wkv = jnp.reshape(jnp.transpose(g_kv, (1, 0, 2)), (d, -1))
    return wkv[:, :2 * d], jnp.pad(wkv[:, 2 * d:], ((0, 0), (0, 128 - nh)))


def _kv_grad_blocks(dwkv2, dwfl):
    d = dwkv2.shape[0]
    dwkv = jnp.concatenate([dwkv2, dwfl[:, :d // HEAD]], axis=1)
    return jnp.transpose(jnp.reshape(dwkv, (d, N_DEV, -1)), (1, 0, 2))


def _local_step(x, target, meta, gamma, a_norm, wa_in, a_out_norm, wa_out, kv_norm, late, b_f, g_k, b_norm, g_q,
                dist):
    d = x.shape[1]
    nh = d // HEAD
    cols = d // N_DEV
    big = dict(tm=1408, tn=512, tk=2048)

    lb = _lb_fwd(gamma)
    h0, hn_a = _embed(x, meta, a_norm)
    lp = h0.shape[0]
    if dist:
        a_out_blk, kv_blk, b_in_blk, b_out_blk = late
        u4, (first,) = _matmul(hn_a, wa_in, "nn", F32, "a_in", out_parts=4,
                               sides=[_gather_own_side([a_out_blk, b_in_blk])], **big)
        o_a, og_a, ((g_a_out, g_b_in), (g_kv,)) = _hgrn2_fwd(
            u4, lb, a_out_norm, sides=[_gather_pass_side(first), _gather_own_side([kv_blk])])
        wa_out = jnp.reshape(g_a_out, (d, d))
        h1, ((g_kv,), (g_b_out,)) = _matmul(og_a, wa_out, "nn", F32, "a_out", add=h0,
                                            sides=[_gather_pass_side([g_kv]), _gather_own_side([b_out_blk])], **big)
        (wkv2, wfl), wb_in = _kv_weights(g_kv), g_b_in
    else:
        u4 = _matmul(hn_a, wa_in, "nn", F32, "a_in", out_parts=4, **big)
        o_a, og_a = _hgrn2_fwd(u4, lb, a_out_norm)
        h1 = _matmul(og_a, wa_out, "nn", F32, "a_out", add=h0, **big)
        wkv2, wfl, wb_in, wb_out = late
    hk, hb = _rms_fwd(h1, [kv_norm, b_norm], "rms_kv_b")
    if dist:
        ukv, ((g_b_out,),) = _matmul(hk, wkv2, "nn", F32, "kv_in", out_parts=2,
                                     sides=[_gather_pass_side([g_b_out])], **big)
        wb_out = jnp.reshape(g_b_out, (d, d))
    else:
        ukv = _matmul(hk, wkv2, "nn", F32, "kv_in", out_parts=2, **big)
    ufl = _matmul(hk, wfl, "nn", F32, "kv_f", **big)
    ub = _matmul(hb, wb_in, "nn", F32, "b_in", out_parts=2, **big)
    f_cum = _fgate_fwd(ufl, b_f)
    qa, ka, va = _attn_operands(ub, ukv, g_q, g_k, f_cum)
    o_b, o_lo, og_b, qb = _fox_fwd(qa, ka, va, ub)
    h2 = _matmul(og_b, wb_out, "nn", F32, "b_out", add=h1, **big)
    loss, dh2 = _loss(h2, target)

    dx_t = dict(tm=1408, tn=512, tk=2048, vmem_mib=56)
    dx_parts_t = dict(tm=704, tn=512, tk=2048, vmem_mib=58, k_whole=True)
    dw_t = dict(tm=512, tn=1024, tk=lp, vmem_mib=58)
    dw_f32_t = dict(tm=1024, tn=512, tk=lp, vmem_mib=58)
    def to_sibling(g8):
        return [_sibling_side([g8])] if dist else []

    def to_chips(partial):
        return [_chips_side([partial])] if dist else []

    def unpack(res, n_sides):
        if not dist:
            return res, [None] * n_sides
        *main, side_results = res
        return (main[0] if len(main) == 1 else tuple(main)), [r[0] for r in side_results]

    dwb_out = _matmul(og_b, dh2, "tn", F32, "b_out_dw", **dw_f32_t)
    g8_b_out = jnp.reshape(dwb_out, (N_DEV, cols, d))
    d_ogb, (got,) = unpack(_matmul(dh2, wb_out, "nt", F32, "b_out_dx", sides=to_sibling(g8_b_out), **dx_t), 1)
    p_b_out = _pair_sum(g8_b_out, got, "pair_sum_b_w_out") if dist else None
    doa = _fox_do(d_ogb, o_b, o_lo, ub)
    (dq, dg_q, dukv, dg_k, dfk), (r_b_out,) = unpack(
        _fox_bwd(qb, ka, va, doa, ukv, g_k, ub, g_q, sides=to_chips(p_b_out)), 1)
    dub = _fox_gate_bwd(ub, d_ogb, o_b, dq)
    d_f = jnp.pad(jnp.transpose(dfk[:, 0, :]), ((0, 0), (0, 128 - nh)))
    dufl, db_f = _fgate_bwd(ufl, b_f, d_f)
    dwb_in = _matmul(hb, dub, "tn", F32, "b_in_dw", out_parts=N_DEV, **dw_t)
    d_hb, (got,) = unpack(_matmul(dub, wb_in, "nt", F32, "b_in_dx", sides=to_sibling(dwb_in), **dx_parts_t), 1)
    p_b_in = _pair_sum(dwb_in, got, "pair_sum_b_w_in") if dist else None
    d_hk, (r_b_in,) = unpack(_matmul(dukv, wkv2, "nt", F32, "kv_dx", sides=to_chips(p_b_in), **dx_parts_t), 1)
    d_hk = _matmul(dufl, wfl, "nt", F32, "kv_f_dx", add=d_hk, **dx_t)
    dwkv2 = _matmul(hk, dukv, "tn", F32, "kv_dw", **dw_t)
    dwfl = _matmul(hk, dufl, "tn", F32, "kv_f_dw", **dw_t)
    g8_kv = _kv_grad_blocks(dwkv2, dwfl) if dist else None
    dh1, (dg_kv, dg_b) = _rms_bwd(h1, [kv_norm, b_norm], [d_hk, d_hb], dh2, "rms_kv_b_bwd")
    d_oga, (got,) = unpack(_matmul(dh1, wa_out, "nt", F32, "a_out_dx", sides=to_sibling(g8_kv), **dx_t), 1)
    p_kv = _pair_sum(g8_kv, got, "pair_sum_kv_w") if dist else None
    dwa_out = _matmul(og_a, dh1, "tn", F32, "a_out_dw", **dw_f32_t)
    g8_a_out = jnp.reshape(dwa_out, (N_DEV, cols, d))
    (du4, dlb, dg_aout), (r_kv, got) = unpack(
        _hgrn2_bwd(u4, o_a, d_oga, lb, a_out_norm, sides=to_chips(p_kv) + to_sibling(g8_a_out)), 2)
    p_a_out = _pair_sum(g8_a_out, got, "pair_sum_a_w_out") if dist else None
    dwa_in, (r_a_out,) = unpack(
        _matmul(hn_a, du4, "tn", F32, "a_in_dw", out_parts=N_DEV, sides=to_chips(p_a_out), **dw_t), 1)
    row_tiles = lp // _tile(lp, dx_parts_t["tm"], 64)
    if dist and row_tiles >= 2:
        top = max(1, row_tiles // 6)
        d_hna, ((got,),) = _matmul(du4, wa_in, "nt", F32, "a_in_dx_top", rows=(0, top),
                                   sides=to_sibling(dwa_in), **dx_parts_t)
        p_a_in = _pair_sum(dwa_in, got, "pair_sum_a_w_in")
        d_hna, ((r_a_in,),) = _matmul(du4, wa_in, "nt", F32, "a_in_dx_bottom", rows=(top, row_tiles - top), into=d_hna,
                                      sides=to_chips(p_a_in), **dx_parts_t)
    else:
        d_hna, (got,) = unpack(_matmul(du4, wa_in, "nt", F32, "a_in_dx", sides=to_sibling(dwa_in), **dx_parts_t), 1)
        if dist:
            (r_a_in,) = _alone(_chips_side([_pair_sum(dwa_in, got, "pair_sum_a_w_in")]), "grads_to_chips_a_w_in")
    grad_x, dmeta, dg_a = _embed_bwd(h0, a_norm, d_hna, dh1)
    dgamma = _lb_bwd(gamma, dlb)

    grads = dict(meta=dmeta, gamma=dgamma, a_norm=dg_a, a_out_norm=dg_aout, kv_norm=dg_kv,
                 b_f=db_f, g_k=dg_k, b_norm=dg_b, g_q=dg_q)
    if dist:
        grads.update(wa_in=r_a_in, wa_out=r_a_out, wkv=r_kv, wb_in=r_b_in, wb_out=r_b_out)
    else:
        grads.update(wa_in=dwa_in, wa_out=dwa_out, wkv2=dwkv2, wfl=dwfl, wb_in=dwb_in, wb_out=dwb_out)
    return loss, grad_x, grads


def kernel(x, meta, gamma_lb, a_norm, a_w_in, a_out_norm, a_w_out, kv_norm, kv_w, fox_b_f, fox_k_norm, b_norm, b_w_in, b_q_norm, b_w_out, loss_target, m_meta, m_gamma_lb, m_a_norm, m_a_w_in, m_a_out_norm, m_a_w_out, m_kv_norm, m_kv_w, m_fox_b_f, m_fox_k_norm, m_b_norm, m_b_w_in, m_b_q_norm, m_b_w_out, v_meta, v_gamma_lb, v_a_norm, v_a_w_in, v_a_out_norm, v_a_w_out, v_kv_norm, v_kv_w, v_fox_b_f, v_fox_k_norm, v_b_norm, v_b_w_in, v_b_q_norm, v_b_w_out):
    d = x.shape[-1]
    nh = d // HEAD
    cols = d // N_DEV
    me = 4 * lax.axis_index("x") + 2 * lax.axis_index("y") + lax.axis_index("c")

    sharded_small = jnp.concatenate([meta, gamma_lb, a_norm, a_out_norm, jnp.zeros((4, cols), F32)], axis=0)
    g_a_in, g_small = _all_gather([a_w_in[0].astype(MXU), sharded_small], "gather_weights")
    wa_out = None
    late = (a_w_out[0].astype(MXU), kv_w.astype(MXU), b_w_in[0].astype(MXU), b_w_out[0].astype(MXU))
    small = jnp.reshape(jnp.transpose(g_small, (1, 0, 2)), (-1, d))
    meta_f, gamma_f, a_norm_f, a_out_norm_f = small[:16], small[16:18], small[18:19], small[19:20]
    b_f = jnp.pad(jnp.reshape(fox_b_f, (1, nh)), ((0, 0), (0, 128 - nh)))
    g_k = jnp.reshape(fox_k_norm, (1, d))
    g_q = jnp.reshape(b_q_norm, (1, d))
    kv_norm_r = jnp.reshape(kv_norm, (1, d))

    loss, grad_x, g = _local_step(x[0], loss_target[0], meta_f, gamma_f, a_norm_f, g_a_in, a_out_norm_f, wa_out,
                                  kv_norm_r, late, b_f, g_k, b_norm, g_q, dist=True)
    loss = lax.psum(loss[0, 0], AXES)

    r_a_in = _adamw(g["wa_in"], a_w_in[0], m_a_w_in[0], v_a_w_in[0], "adamw_a_w_in")
    r_a_out = _adamw(g["wa_out"], a_w_out[0], m_a_w_out[0], v_a_w_out[0], "adamw_a_w_out")
    r_kv = _adamw(g["wkv"], kv_w, m_kv_w, v_kv_w, "adamw_kv_w")
    r_b_in = _adamw(g["wb_in"], b_w_in[0], m_b_w_in[0], v_b_w_in[0], "adamw_b_w_in")
    r_b_out = _adamw(g["wb_out"], b_w_out[0], m_b_w_out[0], v_b_w_out[0], "adamw_b_w_out")

    packed = jnp.concatenate(
        [g["meta"], g["gamma"], g["a_norm"], g["a_out_norm"], g["kv_norm"], g["b_norm"], g["g_k"], g["g_q"],
         jnp.pad(g["b_f"], ((0, 0), (0, d - 128))), jnp.zeros((7, d), F32)], axis=0)
    tot = _all_reduce_small(packed)
    mine = lax.dynamic_slice_in_dim(tot[:20], me * cols, cols, axis=1)
    gs = [mine[:16], mine[16:18], mine[18:19], mine[19:20], tot[20], tot[21:22], jnp.reshape(tot[22], (nh, HEAD)),
          jnp.reshape(tot[23], (1, nh, HEAD)), tot[24, :nh]]
    small_w = [meta, gamma_lb, a_norm, a_out_norm, kv_norm, b_norm, fox_k_norm, b_q_norm, fox_b_f]
    small_m = [m_meta, m_gamma_lb, m_a_norm, m_a_out_norm, m_kv_norm, m_b_norm, m_fox_k_norm, m_b_q_norm, m_fox_b_f]
    small_v = [v_meta, v_gamma_lb, v_a_norm, v_a_out_norm, v_kv_norm, v_b_norm, v_fox_k_norm, v_b_q_norm, v_fox_b_f]

    d_s, m_s, v_s = _adamw_small(gs, small_w, small_m, small_v)

    def ordered(s, a_in, a_out, kv, b_in, b_out):
        return [s[0], s[1], s[2], a_in[None], s[3], a_out[None], s[4], kv, s[8], s[6], s[5], b_in[None], s[7], b_out[None]]

    outs = []
    for i, s in enumerate([gs, d_s, m_s, v_s]):
        outs += ordered(s, r_a_in[i], r_a_out[i], r_kv[i], r_b_in[i], r_b_out[i])
    return (loss, grad_x[None], *outs)
```

```python
import math

import jax
import jax.numpy as jnp
from jax import lax
from jax.experimental import pallas as pl
from jax.experimental.pallas import tpu as pltpu

HEAD = 128
CHUNK = 64
N_META = 16
ROW_PAD = 128 - N_META
EPS = 1e-6
MASK_VALUE = -1e30
ADAM_LR = 0.001
ADAM_B1 = 0.9
ADAM_B2 = 0.999
ADAM_EPS = 1e-08
ADAM_WD = 0.01
ADAM_STEP = 10
N_DEV = 8
N_CHIP = 4
MIB = 1024 * 1024
AXES = ("x", "y", "c")
MESH = pl.DeviceIdType.MESH

F32 = jnp.float32
MXU = jnp.bfloat16
PAYLOAD = jnp.bfloat16
HI = lax.Precision.HIGHEST

NN = (((1,), (0,)), ((), ()))
NT = (((1,), (1,)), ((), ()))
TN = (((0,), (0,)), ((), ()))


def _dot(a, b, dims=NN):
    return lax.dot_general(a.astype(MXU), b.astype(MXU), dims, preferred_element_type=F32)


def _dot_exact(a, b):
    return lax.dot_general(a, b, NN, precision=HI, preferred_element_type=F32)


def _sigmoid(x):
    return 1.0 / (1.0 + jnp.exp(-x))


def _tile(dim, target, unit=128):
    best = None
    t = unit
    while t <= min(dim, target):
        if dim % t == 0:
            best = t
        t += unit
    return best if best is not None else dim


def _cparams(semantics, vmem_mib):
    return pltpu.CompilerParams(dimension_semantics=semantics, vmem_limit_bytes=vmem_mib * MIB)


def _mat_spec(arr, br, bc, rc_of_grid):
    if arr.ndim == 2:
        return pl.BlockSpec((br, bc), rc_of_grid)
    assert arr.shape[2] % bc == 0, (arr.shape, bc)
    per = arr.shape[2] // bc

    def idx(*g):
        r, c = rc_of_grid(*g)
        return (c // per, r, c % per)

    return pl.BlockSpec((None, br, bc), idx)


def _mat_shape(arr):
    return (arr.shape[0], arr.shape[1]) if arr.ndim == 2 else (arr.shape[1], arr.shape[0] * arr.shape[2])


def _matmul(a, b, dims, out_dtype, name, *, add=None, out_parts=1, tm=512, tn=512, tk=512, vmem_mib=48, sides=(),
            k_whole=False, rows=None, into=None):
    ar, ac = _mat_shape(a)
    br_, bc_ = _mat_shape(b)
    if dims == "nn":
        m, k, n = ar, ac, bc_
        assert br_ == k
    elif dims == "nt":
        m, k, n = ar, ac, br_
        assert bc_ == k
    else:
        m, k, n = ac, ar, bc_
        assert br_ == k
    m_unit, n_unit, k_unit = m, n, k
    if a.ndim == 3:
        if dims == "tn":
            m_unit = math.gcd(m_unit, a.shape[2])
        else:
            k_unit = math.gcd(k_unit, a.shape[2])
    if b.ndim == 3:
        if dims == "nt":
            k_unit = math.gcd(k_unit, b.shape[2])
        else:
            n_unit = math.gcd(n_unit, b.shape[2])
    if out_parts > 1:
        n_unit = math.gcd(n_unit, n // out_parts)
    tm = _tile(m_unit, tm, 128 if dims == "tn" else 64)
    tn, tk = _tile(n_unit, tn), _tile(k_unit, tk)
    whole_k = dims == "nt" and tk < k and k_whole
    if whole_k:
        k_chunk, tk = tk, k
    gm, gn, gk = m // tm, n // tn, k // tk
    assert gm * tm == m and gn * tn == n and gk * tk == k, (name, m, n, k, tm, tn, tk)

    def chunk_of(ref, arr, c):
        if arr.ndim == 2:
            return ref[:, c * k_chunk:(c + 1) * k_chunk]
        per = arr.shape[2] // k_chunk
        return ref[c // per, :, (c % per) * k_chunk:(c % per + 1) * k_chunk]

    def all_cols(arr, rows, row_of_grid):
        if arr.ndim == 2:
            return pl.BlockSpec((rows, arr.shape[1]), lambda i, j, kk: (row_of_grid(i, j), 0))
        return pl.BlockSpec((arr.shape[0], rows, arr.shape[2]), lambda i, j, kk: (0, row_of_grid(i, j), 0))

    if dims == "nn":
        a_spec = _mat_spec(a, tm, tk, lambda i, j, kk: (i, kk))
        b_spec = _mat_spec(b, tk, tn, lambda i, j, kk: (kk, j))
        dn = NN
    elif whole_k:
        a_spec = all_cols(a, tm, lambda i, j: i)
        b_spec = all_cols(b, tn, lambda i, j: j)
        dn = NT
    elif dims == "nt":
        a_spec = _mat_spec(a, tm, tk, lambda i, j, kk: (i, kk))
        b_spec = _mat_spec(b, tn, tk, lambda i, j, kk: (j, kk))
        dn = NT
    else:
        a_spec = _mat_spec(a, tk, tm, lambda i, j, kk: (kk, i))
        b_spec = _mat_spec(b, tk, tn, lambda i, j, kk: (kk, j))
        dn = TN

    if out_parts > 1:
        per = (n // out_parts) // tn
        out_shape = jax.ShapeDtypeStruct((out_parts, m, n // out_parts), out_dtype)
        o_spec = pl.BlockSpec((None, tm, tn), lambda i, j, kk: (j // per, i, j % per))
    else:
        out_shape = jax.ShapeDtypeStruct((m, n), out_dtype)
        o_spec = pl.BlockSpec((tm, tn), lambda i, j, kk: (i, j))

    in_specs = [a_spec, b_spec]
    args = [a, b]
    if add is not None:
        in_specs.append(pl.BlockSpec((tm, tn), lambda i, j, kk: (i, j)))
        args.append(add)
    if rows is not None:
        first_tile, gm = rows

        def shifted(spec):
            return pl.BlockSpec(spec.block_shape, lambda i, j, kk: spec.index_map(i + first_tile, j, kk))

        in_specs = [shifted(s) for s in in_specs]
        o_spec = shifted(o_spec)
    aliases = {}
    if into is not None:
        aliases[len(args)] = 0
        in_specs.append(_any_spec())
        args.append(into)

    def body(*refs):
        a_ref, b_ref, *others, o_ref, acc_ref = refs
        add_ref = others[0] if add is not None else None
        kk = pl.program_id(2)
        if whole_k:
            part = sum(lax.dot_general(chunk_of(a_ref, a, c).astype(MXU), chunk_of(b_ref, b, c).astype(MXU), dn,
                                       preferred_element_type=F32) for c in range(k // k_chunk))
        else:
            part = lax.dot_general(a_ref[...].astype(MXU), b_ref[...].astype(MXU), dn, preferred_element_type=F32)

        def finish(total):
            if add_ref is not None:
                total = total + add_ref[...]
            o_ref[...] = total.astype(o_ref.dtype)

        if gk == 1:
            finish(part)
        else:
            @pl.when(kk == 0)
            def _():
                acc_ref[...] = part

            @pl.when(jnp.logical_and(kk > 0, kk < gk - 1))
            def _():
                acc_ref[...] += part

            @pl.when(kk == gk - 1)
            def _():
                finish(acc_ref[...] + part)

    (out,), side_results = _pallas(
        body,
        name=name,
        args=args,
        grid=(gm, gn, gk),
        in_specs=in_specs,
        out_specs=[o_spec],
        out_shape=[out_shape],
        scratch_shapes=[pltpu.VMEM((tm, tn) if gk > 1 else (8, 128), F32)],
        semantics=("parallel", "parallel", "arbitrary"),
        vmem_mib=vmem_mib,
        sides=sides,
        aliases=aliases,
    )
    return (out, side_results) if sides else out


def _rms_fwd(h, gains, name, tm=384):
    lp, d = h.shape
    tm = _tile(lp, tm)
    n = len(gains)

    def body(*refs):
        h_ref = refs[0]
        g_refs = refs[1:1 + n]
        o_refs = refs[1 + n:]
        x = h_ref[...]
        y = x * lax.rsqrt(jnp.mean(x * x, axis=-1, keepdims=True) + EPS)
        for g_ref, o_ref in zip(g_refs, o_refs):
            o_ref[...] = (y * g_ref[...]).astype(o_ref.dtype)

    row = pl.BlockSpec((tm, d), lambda i: (i, 0))
    vec = pl.BlockSpec((1, d), lambda i: (0, 0))
    return pl.pallas_call(
        body,
        name=name,
        grid=(lp // tm,),
        in_specs=[row] + [vec] * n,
        out_specs=[row] * n,
        out_shape=[jax.ShapeDtypeStruct((lp, d), MXU)] * n,
        compiler_params=_cparams(("parallel",), 40),
    )(h, *gains)


def _embed(x, meta, gain, name="embed"):
    seq, d = x.shape
    lp = ROW_PAD + N_META + seq
    blk = ROW_PAD + N_META

    def body(x_ref, meta_ref, g_ref, h_ref, o_ref):
        i = pl.program_id(0)

        @pl.when(i == 0)
        def _():
            h_ref[0:ROW_PAD, :] = jnp.zeros((ROW_PAD, d), F32)
            h_ref[ROW_PAD:blk, :] = meta_ref[...]

        @pl.when(i > 0)
        def _():
            h_ref[...] = x_ref[...]

        h = h_ref[...]
        y = h * lax.rsqrt(jnp.mean(h * h, axis=-1, keepdims=True) + EPS)
        o_ref[...] = (y * g_ref[...]).astype(o_ref.dtype)

    row = pl.BlockSpec((blk, d), lambda i: (i, 0))
    return pl.pallas_call(
        body,
        name=name,
        grid=(lp // blk,),
        in_specs=[pl.BlockSpec((blk, d), lambda i: (jnp.maximum(i - 1, 0), 0)),
                  pl.BlockSpec((N_META, d), lambda i: (0, 0)), pl.BlockSpec((1, d), lambda i: (0, 0))],
        out_specs=[row, row],
        out_shape=[jax.ShapeDtypeStruct((lp, d), F32), jax.ShapeDtypeStruct((lp, d), MXU)],
        compiler_params=_cparams(("arbitrary",), 32),
    )(x, meta, gain)


def _embed_bwd(h, gain, dy, res, name="embed_bwd"):
    lp, d = h.shape
    blk = ROW_PAD + N_META

    def body(h_ref, res_ref, g_ref, dy_ref, dx_ref, dmeta_ref, dg_ref):
        i = pl.program_id(0)
        x = h_ref[...]
        rstd = lax.rsqrt(jnp.mean(x * x, axis=-1, keepdims=True) + EPS)
        xhat = x * rstd
        dy = dy_ref[...]
        gdy = dy * g_ref[...]
        dh = res_ref[...] + rstd * (gdy - xhat * jnp.mean(gdy * xhat, axis=-1, keepdims=True))
        part = jnp.sum(dy * xhat, axis=0, keepdims=True)

        @pl.when(i == 0)
        def _():
            dmeta_ref[...] = dh[ROW_PAD:blk, :]
            dg_ref[...] = part

        @pl.when(i > 0)
        def _():
            dx_ref[...] = dh
            dg_ref[...] += part

    row = pl.BlockSpec((blk, d), lambda i: (i, 0))
    vec = pl.BlockSpec((1, d), lambda i: (0, 0))
    return pl.pallas_call(
        body,
        name=name,
        grid=(lp // blk,),
        in_specs=[row, row, vec, row],
        out_specs=[pl.BlockSpec((blk, d), lambda i: (jnp.maximum(i - 1, 0), 0)),
                   pl.BlockSpec((N_META, d), lambda i: (0, 0)), vec],
        out_shape=[jax.ShapeDtypeStruct((lp - blk, d), F32), jax.ShapeDtypeStruct((N_META, d), F32),
                   jax.ShapeDtypeStruct((1, d), F32)],
        compiler_params=_cparams(("arbitrary",), 32),
    )(h, res, gain, dy)


def _rms_bwd(h, gains, dys, res, name, tm=384):
    lp, d = h.shape
    tm = _tile(lp, tm)
    n = len(gains)

    def body(*refs):
        h_ref, res_ref = refs[0], refs[1]
        g_refs = refs[2:2 + n]
        dy_refs = refs[2 + n:2 + 2 * n]
        dh_ref = refs[2 + 2 * n]
        dg_refs = refs[3 + 2 * n:]
        i = pl.program_id(0)
        x = h_ref[...]
        rstd = lax.rsqrt(jnp.mean(x * x, axis=-1, keepdims=True) + EPS)
        xhat = x * rstd
        dh = res_ref[...]
        for g_ref, dy_ref, dg_ref in zip(g_refs, dy_refs, dg_refs):
            dy = dy_ref[...]
            gdy = dy * g_ref[...]
            dh = dh + rstd * (gdy - xhat * jnp.mean(gdy * xhat, axis=-1, keepdims=True))
            part = jnp.sum(dy * xhat, axis=0, keepdims=True)

            @pl.when(i == 0)
            def _():
                dg_ref[...] = part

            @pl.when(i > 0)
            def _():
                dg_ref[...] += part

        dh_ref[...] = dh

    row = pl.BlockSpec((tm, d), lambda i: (i, 0))
    vec = pl.BlockSpec((1, d), lambda i: (0, 0))
    outs = pl.pallas_call(
        body,
        name=name,
        grid=(lp // tm,),
        in_specs=[row, row] + [vec] * n + [row] * n,
        out_specs=[row] + [vec] * n,
        out_shape=[jax.ShapeDtypeStruct((lp, d), F32)] + [jax.ShapeDtypeStruct((1, d), F32)] * n,
        compiler_params=_cparams(("arbitrary",), 56),
    )(h, res, *gains, *dys)
    return outs[0], list(outs[1:])


def _lb_fwd(gamma):
    def body(g_ref, lb_ref):
        g = g_ref[...]
        e = jnp.exp(g - jnp.max(g, axis=0, keepdims=True))
        lb_ref[...] = (e / jnp.sum(e, axis=0, keepdims=True))[0:1, :]

    return pl.pallas_call(body, name="lb_fwd", out_shape=jax.ShapeDtypeStruct((1, gamma.shape[1]), F32))(gamma)


def _lb_bwd(gamma, dlb):
    def body(g_ref, dlb_ref, dg_ref):
        g = g_ref[...]
        e = jnp.exp(g - jnp.max(g, axis=0, keepdims=True))
        s = e / jnp.sum(e, axis=0, keepdims=True)
        rows = lax.broadcasted_iota(jnp.int32, g.shape, 0)
        ds = jnp.where(rows == 0, dlb_ref[...], 0.0)
        dg_ref[...] = s * (ds - jnp.sum(s * ds, axis=0, keepdims=True))

    return pl.pallas_call(body, name="lb_bwd", out_shape=jax.ShapeDtypeStruct(gamma.shape, F32))(gamma, dlb)


def _tri(n, lower):
    r = lax.broadcasted_iota(jnp.int32, (n, n), 0)
    c = lax.broadcasted_iota(jnp.int32, (n, n), 1)
    return jnp.where((r >= c) if lower else (r <= c), 1.0, 0.0).astype(F32)


def _group(nc, most):
    return max(u for u in range(1, most + 1) if nc % u == 0)


def _running_sum(tri, x):
    hi = x.astype(MXU)
    rest = x - hi.astype(F32)
    mid = rest.astype(MXU)
    lo = (rest - mid.astype(F32)).astype(MXU)
    return _dot(tri, hi) + _dot(tri, mid) + _dot(tri, lo)


def _causal(n):
    r = lax.broadcasted_iota(jnp.int32, (n, n), 0)
    c = lax.broadcasted_iota(jnp.int32, (n, n), 1)
    return r >= c


def _chunk_gates(u_ref, lb, c):
    sl = pl.ds(pl.multiple_of(c * CHUNK, CHUNK), CHUNK)
    valid = (c * CHUNK + lax.broadcasted_iota(jnp.int32, (CHUNK, HEAD), 0)) >= ROW_PAD
    uq = u_ref[0, sl, :]
    uf = u_ref[1, sl, :]
    sq = _sigmoid(uq)
    sf = _sigmoid(uf)
    fg = lb + (1.0 - lb) * sf
    return dict(sl=sl, valid=valid, uq=uq, sq=sq, sf=sf, fg=fg, q=jnp.where(valid, uq * sq, 0.0),
                logf=jnp.where(valid, jnp.log(fg), 0.0), k=jnp.where(valid, 1.0 - fg, 0.0),
                v=jnp.where(valid, u_ref[2, sl, :], 0.0))


def _chunk_decays(x, b):
    b_last = b[CHUNK - 1:CHUNK, :]
    b_mid = b[CHUNK // 2 - 1:CHUNK // 2, :]
    e_qi = jnp.exp(b - b_mid)
    e_ki = jnp.exp(b_mid - b)
    e_kd = jnp.exp(b_last - b)
    e_qe = jnp.exp(b)
    q, k = x["q"], x["k"]
    return dict(x, e_qi=e_qi, e_ki=e_ki, e_kd=e_kd, e_qe=e_qe, qi=q * e_qi, ki=k * e_ki, kd=k * e_kd, qe=q * e_qe,
                decay=jnp.exp(b_last))


def _chunks(u_ref, lb, tri_lower, cs):
    gates = [_chunk_gates(u_ref, lb, c) for c in cs]
    sums = [_running_sum(tri_lower, x["logf"]) for x in gates]
    return [_chunk_decays(x, b) for x, b in zip(gates, sums)]


def _hgrn2_fwd(u4, lb, g_out, name="hgrn2_fwd", sides=()):
    _, lp, d = u4.shape
    nh, nc = d // HEAD, lp // CHUNK
    per = _group(nc, 22)

    def body(u_ref, lb_ref, g_ref, o_ref, og_ref):
        lb_v = lb_ref[...]
        g = g_ref[...]

        tri_lower = _tri(CHUNK, True).astype(MXU)
        causal = _causal(CHUNK)

        def step(i, st):
            xs = _chunks(u_ref, lb_v, tri_lower, [i * per + u for u in range(per)])
            scores = [_dot(x["qi"], x["ki"], NT) for x in xs]
            updates = [_dot(x["v"], x["kd"], TN) for x in xs]
            states = []
            for x, upd in zip(xs, updates):
                states.append(st)
                st = x["decay"] * st + upd
            outs = [_dot(jnp.where(causal, a, 0.0), x["v"]) + _dot(x["qe"], s, NT)
                    for x, a, s in zip(xs, scores, states)]
            for x, o in zip(xs, outs):
                o_ref[x["sl"], :] = o
                on = o * lax.rsqrt(jnp.mean(o * o, axis=-1, keepdims=True) + EPS) * g
                z = u_ref[3, x["sl"], :]
                og_ref[x["sl"], :] = (on * (z * _sigmoid(z))).astype(og_ref.dtype)
            return st

        lax.fori_loop(0, nc // per, step, jnp.zeros((HEAD, HEAD), F32))

    slab = pl.BlockSpec((lp, HEAD), lambda h: (0, h))
    vec = pl.BlockSpec((1, HEAD), lambda h: (0, h))
    outs, side_results = _pallas(
        body,
        name=name,
        args=(u4, lb, g_out),
        grid=(nh,),
        in_specs=[pl.BlockSpec((4, lp, HEAD), lambda h: (0, 0, h)), vec, vec],
        out_specs=[slab, slab],
        out_shape=[jax.ShapeDtypeStruct((lp, d), F32), jax.ShapeDtypeStruct((lp, d), MXU)],
        semantics=("parallel",),
        vmem_mib=48,
        sides=sides,
    )
    return (*outs, side_results) if sides else tuple(outs)


def _hgrn2_bwd(u4, o, d_og, lb, g_out, name="hgrn2_bwd", sides=()):
    _, lp, d = u4.shape
    nh, nc = d // HEAD, lp // CHUNK
    per = _group(nc, 11)

    def body(u_ref, o_ref, dog_ref, lb_ref, g_ref, du_ref, dlb_ref, dg_ref, st_ref, do_ref):
        lb_v = lb_ref[...]
        g = g_ref[...]

        tri_lower = _tri(CHUNK, True).astype(MXU)
        tri_upper = _tri(CHUNK, False).astype(MXU)
        causal = _causal(CHUNK)

        def fwd_step(i, carry):
            st, dg_acc = carry
            cs = [i * per + u for u in range(per)]
            xs = _chunks(u_ref, lb_v, tri_lower, cs)
            updates = [_dot(x["v"], x["kd"], TN) for x in xs]
            for c, x, upd in zip(cs, xs, updates):
                st_ref[c] = st
                st = x["decay"] * st + upd
            for x in xs:
                sl = x["sl"]
                ov = o_ref[sl, :]
                rstd = lax.rsqrt(jnp.mean(ov * ov, axis=-1, keepdims=True) + EPS)
                on = ov * rstd
                z = u_ref[3, sl, :]
                sz = _sigmoid(z)
                dog = dog_ref[sl, :]
                dy = dog * (z * sz)
                dz = dog * (on * g) * (sz * (1.0 + z * (1.0 - sz)))
                du_ref[3, sl, :] = dz.astype(du_ref.dtype)
                gdy = dy * g
                do = rstd * (gdy - on * jnp.mean(gdy * on, axis=-1, keepdims=True))
                do_ref[sl, :] = jnp.where(x["valid"], do, 0.0)
                dg_acc = dg_acc + jnp.sum(dy * on, axis=0, keepdims=True)
            return st, dg_acc

        _, dg_tot = lax.fori_loop(0, nc // per, fwd_step, (jnp.zeros((HEAD, HEAD), F32), jnp.zeros((1, HEAD), F32)))
        dg_ref[...] = dg_tot

        def bwd_step(i, carry):
            gt, dlb_acc = carry
            cs = [nc - 1 - (i * per + u) for u in range(per)]
            xs = _chunks(u_ref, lb_v, tri_lower, cs)
            dos = [do_ref[x["sl"], :] for x in xs]
            sts = [st_ref[c] for c in cs]
            scores = [jnp.where(causal, _dot(x["qi"], x["ki"], NT), 0.0) for x in xs]
            d_scores = [jnp.where(causal, _dot(do, x["v"], NT), 0.0) for x, do in zip(xs, dos)]
            d_qes = [_dot(do, st) for do, st in zip(dos, sts)]
            g_updates = [_dot(do, x["qe"], TN) for x, do in zip(xs, dos)]
            gts = []
            for x, upd in zip(xs, g_updates):
                gts.append(gt)
                gt = x["decay"] * gt + upd
            d_kds = [_dot(x["v"], g_) for x, g_ in zip(xs, gts)]
            dvs = [_dot(x["kd"], g_, NT) + _dot(a, do, TN) for x, g_, a, do in zip(xs, gts, scores, dos)]
            d_qis = [_dot(d_a, x["ki"]) for x, d_a in zip(xs, d_scores)]
            d_kis = [_dot(d_a, x["qi"], TN) for x, d_a in zip(xs, d_scores)]
            rows = lax.broadcasted_iota(jnp.int32, (CHUNK, HEAD), 0)
            dbs = []
            for x, g_, st, d_qi, d_ki, d_qe, d_kd in zip(xs, gts, sts, d_qis, d_kis, d_qes, d_kds):
                t_qi, t_ki, t_qe, t_kd = d_qi * x["qi"], d_ki * x["ki"], d_qe * x["qe"], d_kd * x["kd"]
                d_decay = jnp.sum(g_ * st, axis=0, keepdims=True)
                d_mid = jnp.sum(t_ki - t_qi, axis=0, keepdims=True)
                d_last = jnp.sum(t_kd, axis=0, keepdims=True) + d_decay * x["decay"]
                dbs.append(t_qi - t_ki + t_qe - t_kd + jnp.where(rows == CHUNK // 2 - 1, d_mid, 0.0)
                           + jnp.where(rows == CHUNK - 1, d_last, 0.0))
            dlogfs = [_running_sum(tri_upper, db) for db in dbs]
            for x, dlogf, dv, d_qi, d_ki, d_qe, d_kd in zip(xs, dlogfs, dvs, d_qis, d_kis, d_qes, d_kds):
                sl = x["sl"]
                dq = d_qi * x["e_qi"] + d_qe * x["e_qe"]
                dk = d_ki * x["e_ki"] + d_kd * x["e_kd"]
                valid, sq, sf, uq = x["valid"], x["sq"], x["sf"], x["uq"]
                dfg = jnp.where(valid, dlogf / x["fg"] - dk, 0.0)
                du_ref[0, sl, :] = jnp.where(valid, dq * (sq * (1.0 + uq * (1.0 - sq))), 0.0).astype(du_ref.dtype)
                du_ref[1, sl, :] = (dfg * (1.0 - lb_v) * (sf * (1.0 - sf))).astype(du_ref.dtype)
                du_ref[2, sl, :] = jnp.where(valid, dv, 0.0).astype(du_ref.dtype)
                dlb_acc = dlb_acc + jnp.sum(dfg * (1.0 - sf), axis=0, keepdims=True)
            return gt, dlb_acc

        _, dlb_tot = lax.fori_loop(0, nc // per, bwd_step, (jnp.zeros((HEAD, HEAD), F32), jnp.zeros((1, HEAD), F32)))
        dlb_ref[...] = dlb_tot

    slab = pl.BlockSpec((lp, HEAD), lambda h: (0, h))
    vec = pl.BlockSpec((1, HEAD), lambda h: (0, h))
    quad = pl.BlockSpec((4, lp, HEAD), lambda h: (0, 0, h))
    outs, side_results = _pallas(
        body,
        name=name,
        args=(u4, o, d_og, lb, g_out),
        grid=(nh,),
        in_specs=[quad, slab, slab, vec, vec],
        out_specs=[quad, vec, vec],
        out_shape=[jax.ShapeDtypeStruct((4, lp, d), MXU), jax.ShapeDtypeStruct((1, d), F32),
                   jax.ShapeDtypeStruct((1, d), F32)],
        scratch_shapes=[pltpu.VMEM((nc, HEAD, HEAD), F32), pltpu.VMEM((lp, HEAD), F32)],
        semantics=("parallel",),
        vmem_mib=58,
        sides=sides,
    )
    return (*outs, side_results) if sides else tuple(outs)


WIDE = 2 * HEAD
INV_SCALE = HEAD ** 0.5


def _split3(x):
    hi = x.astype(MXU).astype(F32)
    rest = x - hi
    mid = rest.astype(MXU).astype(F32)
    return hi, mid, (rest - mid).astype(MXU).astype(F32)


def _extra_cols(rows, first, second):
    lane = lax.broadcasted_iota(jnp.int32, (rows, HEAD), 1)
    out = jnp.where(lane < 6, 1.0, 0.0).astype(F32)
    for base, terms in ((0, first), (3, second)):
        if terms is not None:
            for j, term in enumerate(terms):
                out = jnp.where(lane == base + j, term, out)
    return out


def _head_col(a, h):
    lane = lax.broadcasted_iota(jnp.int32, a.shape, 1)
    return jnp.sum(jnp.where(lane == h, a, 0.0), axis=-1, keepdims=True)


def _attn_operands(ub, ukv, g_q, g_k, f_cum, name="attn_operands", tm=384):
    _, lp, d = ub.shape
    tm = _tile(lp, tm)
    nh = d // HEAD

    def body(q_ref, k_ref, v_ref, gq_ref, gk_ref, f_ref, qa_ref, ka_ref, va_ref):
        i = pl.program_id(0)
        f = f_ref[...]
        is_pad = (i * tm + lax.broadcasted_iota(jnp.int32, (tm, 1), 0)) < ROW_PAD
        ones_only = _extra_cols(tm, None, (0.0, 0.0, 0.0)).astype(MXU)
        for h in range(nh):
            hs = slice(h * HEAD, (h + 1) * HEAD)
            lo, hi = h * WIDE, h * WIDE + HEAD
            f_h = _head_col(f, h)
            for x_ref, g_ref, o_ref in ((q_ref, gq_ref, qa_ref), (k_ref, gk_ref, ka_ref)):
                x = x_ref[:, hs]
                y = x * lax.rsqrt(jnp.mean(x * x, axis=-1, keepdims=True) + EPS)
                o_ref[:, lo:hi] = (y * g_ref[:, hs]).astype(o_ref.dtype)
            qa_ref[:, hi:hi + HEAD] = _extra_cols(tm, _split3(f_h * INV_SCALE), None).astype(MXU)
            f_key = jnp.where(is_pad, -MASK_VALUE, f_h)
            ka_ref[:, hi:hi + HEAD] = _extra_cols(tm, None, _split3(-f_key * INV_SCALE)).astype(MXU)
            va_ref[:, lo:hi] = v_ref[:, hs].astype(MXU)
            va_ref[:, hi:hi + HEAD] = ones_only

    wide = pl.BlockSpec((tm, nh * WIDE), lambda i: (i, 0))
    vec = pl.BlockSpec((1, d), lambda i: (0, 0))
    return pl.pallas_call(
        body,
        name=name,
        grid=(lp // tm,),
        in_specs=[pl.BlockSpec((None, tm, d), lambda i: (0, i, 0)), pl.BlockSpec((None, tm, d), lambda i: (0, i, 0)),
                  pl.BlockSpec((None, tm, d), lambda i: (1, i, 0)), vec, vec, pl.BlockSpec((tm, 128), lambda i: (i, 0))],
        out_specs=[wide, wide, wide],
        out_shape=[jax.ShapeDtypeStruct((lp, nh * WIDE), MXU)] * 3,
        compiler_params=_cparams(("parallel",), 56),
    )(ub, ukv, ukv, g_q, g_k, f_cum)


def _head_rms_bwd_tile(x, g, dy):
    rstd = lax.rsqrt(jnp.mean(x * x, axis=-1, keepdims=True) + EPS)
    xhat = x * rstd
    gdy = dy * g
    dx = rstd * (gdy - xhat * jnp.mean(gdy * xhat, axis=-1, keepdims=True))
    return dx, jnp.sum(dy * xhat, axis=0, keepdims=True)


def _fgate_fwd(ufl, b_f):
    lp = ufl.shape[0]
    nb = lp // 128

    def body(u_ref, b_ref, f_ref):
        def step(i, carry):
            sl = pl.ds(pl.multiple_of(i * 128, 128), 128)
            valid = (i * 128 + lax.broadcasted_iota(jnp.int32, (128, 128), 0)) >= ROW_PAD
            x = u_ref[sl, :] + b_ref[...]
            logf = jnp.where(valid, jnp.minimum(x, 0.0) - jnp.log(1.0 + jnp.exp(-jnp.abs(x))), 0.0)
            f = _dot_exact(_tri(128, True), logf) + carry
            f_ref[sl, :] = f
            return f[127:128, :]

        lax.fori_loop(0, nb, step, jnp.zeros((1, 128), F32))

    return pl.pallas_call(body, name="fgate_fwd", out_shape=jax.ShapeDtypeStruct((lp, 128), F32))(ufl, b_f)


def _fgate_bwd(ufl, b_f, d_f):
    lp = ufl.shape[0]
    nb = lp // 128

    def body(u_ref, b_ref, df_ref, du_ref, db_ref):
        def step(j, carry):
            later, db_acc = carry
            i = nb - 1 - j
            sl = pl.ds(pl.multiple_of(i * 128, 128), 128)
            valid = (i * 128 + lax.broadcasted_iota(jnp.int32, (128, 128), 0)) >= ROW_PAD
            x = u_ref[sl, :] + b_ref[...]
            df = df_ref[sl, :]
            dlogf = _dot_exact(_tri(128, False), df) + later
            dx = jnp.where(valid, dlogf * _sigmoid(-x), 0.0)
            du_ref[sl, :] = dx.astype(du_ref.dtype)
            return later + jnp.sum(df, axis=0, keepdims=True), db_acc + jnp.sum(dx, axis=0, keepdims=True)

        _, db_tot = lax.fori_loop(0, nb, step, (jnp.zeros((1, 128), F32), jnp.zeros((1, 128), F32)))
        db_ref[...] = db_tot

    return pl.pallas_call(
        body, name="fgate_bwd",
        out_shape=[jax.ShapeDtypeStruct((lp, 128), MXU), jax.ShapeDtypeStruct((1, 128), F32)],
    )(ufl, b_f, d_f)


STRIP = 32
PAIR = 2


def _strip_causal(r, t):
    row = r + lax.broadcasted_iota(jnp.int32, (STRIP, t), 0)
    col = lax.broadcasted_iota(jnp.int32, (STRIP, t), 1)
    return col <= row


def _fox_fwd(qa, ka, va, ub, name="fox_fwd", t=384):
    lp = qa.shape[0]
    d = ub.shape[2]
    t = _tile(lp, t)
    nh, nq = d // HEAD, lp // t
    scale = HEAD ** -0.5

    assert nh % PAIR == 0
    heads = range(PAIR)

    def body(q_ref, k_ref, v_ref, z_ref, o_ref, olo_ref, og_ref, qb_ref,
             s_ref, p_ref, m_ref, a_ref, l_ref, acc_ref):
        qb = pl.program_id(1)
        m_ref[...] = jnp.full((PAIR, t, 1), MASK_VALUE, F32)
        l_ref[...] = jnp.zeros((PAIR, t, 128), F32)
        acc_ref[...] = jnp.zeros((PAIR, 2 * t, HEAD), F32)
        p_ref[1] = jnp.zeros((PAIR, 2 * t, t), MXU)
        a_ref[1] = jnp.ones((PAIR, t, 1), F32)

        def scores(kb, buf):
            ks = pl.ds(pl.multiple_of(kb * t, t), t)
            for j in heads:
                ws = slice(j * WIDE, (j + 1) * WIDE)
                s_ref[buf, j] = _dot(q_ref[:, ws], k_ref[ks, ws], NT)

        def weighted_sum(kb, buf):
            ks = pl.ds(pl.multiple_of(kb * t, t), t)
            for j in heads:
                pv = _dot(p_ref[buf, j], v_ref[ks, j * WIDE:j * WIDE + HEAD])
                alpha = a_ref[buf, j]
                acc_ref[j, 0:t, :] = alpha * acc_ref[j, 0:t, :] + pv[0:t]
                acc_ref[j, t:2 * t, :] = alpha * acc_ref[j, t:2 * t, :] + pv[t:2 * t]

        def softmax_update(buf, diagonal):
            for j in heads:
                for r in range(0, t, STRIP):
                    rs = slice(r, r + STRIP)
                    x = s_ref[buf, j, rs, :] * scale
                    if diagonal:
                        x = jnp.where(_strip_causal(r, t), x, MASK_VALUE)
                    m_old = m_ref[j, rs, :]
                    m_new = jnp.maximum(m_old, jnp.max(x, axis=-1, keepdims=True))
                    alpha = jnp.exp(m_old - m_new)
                    p = jnp.exp(x - m_new)
                    m_ref[j, rs, :] = m_new
                    a_ref[buf, j, rs, :] = alpha
                    l_ref[j, rs, :] = alpha * l_ref[j, rs, :] + sum(p[:, c:c + 128] for c in range(0, t, 128))
                    p_hi = p.astype(MXU)
                    p_ref[buf, j, rs, :] = p_hi
                    p_ref[buf, j, t + r:t + r + STRIP, :] = (p - p_hi.astype(F32)).astype(MXU)

        def off_diagonal(kb, cur):
            weighted_sum(jnp.maximum(kb - 1, 0), 1 - cur)
            scores(kb + 1, 1 - cur)
            softmax_update(cur, False)

        def diagonal(cur):
            weighted_sum(jnp.maximum(qb - 1, 0), 1 - cur)
            softmax_update(cur, True)
            weighted_sum(qb, cur)

        def two_blocks(i, carry):
            off_diagonal(2 * i, 0)
            off_diagonal(2 * i + 1, 1)
            return carry

        scores(0, 0)
        lax.fori_loop(0, qb // 2, two_blocks, 0)

        @pl.when(lax.rem(qb, 2) == 0)
        def _():
            diagonal(0)

        @pl.when(lax.rem(qb, 2) == 1)
        def _():
            off_diagonal(qb - 1, 0)
            diagonal(1)

        is_pad = (qb * t + lax.broadcasted_iota(jnp.int32, (t, 1), 0)) < ROW_PAD
        for j in heads:
            hs = slice(j * HEAD, (j + 1) * HEAD)
            l = jnp.sum(l_ref[j], axis=-1, keepdims=True)
            o = acc_ref[j, 0:t, :] / l
            o_ref[:, hs] = o
            olo_ref[:, hs] = acc_ref[j, t:2 * t, :] / l
            z = z_ref[:, hs]
            og_ref[:, hs] = (o * (z * _sigmoid(z))).astype(og_ref.dtype)
            extra = q_ref[:, j * WIDE + HEAD:(j + 1) * WIDE].astype(F32)
            f_scaled = extra[:, 0:1] + extra[:, 1:2] + extra[:, 2:3]
            log_term = jnp.where(is_pad, MASK_VALUE * INV_SCALE, f_scaled - (m_ref[j] + jnp.log(l)) * INV_SCALE)
            qb_ref[:, j * WIDE:j * WIDE + HEAD] = q_ref[:, j * WIDE:j * WIDE + HEAD]
            qb_ref[:, j * WIDE + HEAD:(j + 1) * WIDE] = _extra_cols(t, _split3(log_term), None).astype(qb_ref.dtype)

    scratch = [pltpu.VMEM((2, PAIR, t, t), F32), pltpu.VMEM((2, PAIR, 2 * t, t), MXU), pltpu.VMEM((PAIR, t, 1), F32),
               pltpu.VMEM((2, PAIR, t, 1), F32), pltpu.VMEM((PAIR, t, 128), F32), pltpu.VMEM((PAIR, 2 * t, HEAD), F32)]
    tile = pl.BlockSpec((t, PAIR * HEAD), lambda g, i: (i, g))
    wide_tile = pl.BlockSpec((t, PAIR * WIDE), lambda g, i: (i, g))
    wide_all = pl.BlockSpec((lp, PAIR * WIDE), lambda g, i: (0, g))
    return pl.pallas_call(
        body,
        name=name,
        grid=(nh // PAIR, nq),
        in_specs=[wide_tile, wide_all, wide_all, pl.BlockSpec((None, t, PAIR * HEAD), lambda g, i: (1, i, g))],
        out_specs=[tile, tile, tile, wide_tile],
        out_shape=[jax.ShapeDtypeStruct((lp, d), F32), jax.ShapeDtypeStruct((lp, d), F32),
                   jax.ShapeDtypeStruct((lp, d), MXU), jax.ShapeDtypeStruct((lp, nh * WIDE), MXU)],
        scratch_shapes=scratch,
        compiler_params=_cparams(("parallel", "arbitrary"), 48),
    )(qa, ka, va, ub)


def _fox_bwd(qb, ka, va, doa, ukv, g_k, ub, g_q, name="fox_bwd", t=384, sides=()):
    lp = qb.shape[0]
    d = ukv.shape[2]
    t = _tile(lp, t)
    nh, nk = d // HEAD, lp // t
    scale = HEAD ** -0.5

    assert nh % PAIR == 0
    heads = range(PAIR)

    def body(q_ref, do_ref, k_ref, v_ref, kraw_ref, gk_ref, qraw_ref, gq_ref, dqraw_ref, dgq_ref, dukv_ref, dgk_ref,
             dfk_ref, s_ref, dp_ref, p_ref, ds_ref, col_ref, dk_ref, dv_ref, dq_ref):
        kb = pl.program_id(1)

        @pl.when(kb == 0)
        def _():
            dq_ref[...] = jnp.zeros_like(dq_ref)
            dgk_ref[...] = jnp.zeros_like(dgk_ref)

        dk_ref[...] = jnp.zeros_like(dk_ref)
        dv_ref[...] = jnp.zeros_like(dv_ref)
        col_ref[...] = jnp.zeros_like(col_ref)

        def step(qb, diagonal):
            qs = pl.ds(pl.multiple_of(qb * t, t), t)
            for j in heads:
                ws = slice(j * WIDE, (j + 1) * WIDE)
                s_ref[j] = _dot(q_ref[qs, ws], k_ref[:, ws], NT)
                dp_ref[j] = _dot(do_ref[qs, ws], v_ref[:, ws], NT)
            for j in heads:
                for r in range(0, t, STRIP):
                    rs = slice(r, r + STRIP)
                    x = s_ref[j, rs, :] * scale
                    if diagonal:
                        x = jnp.where(_strip_causal(r, t), x, MASK_VALUE)
                    p = jnp.exp(x)
                    ds = p * dp_ref[j, rs, :]
                    p_ref[j, rs, :] = p.astype(MXU)
                    ds_ref[j, rs, :] = (ds * scale).astype(MXU)
                    col_ref[j] += ds
            for j in heads:
                hs = slice(j * HEAD, (j + 1) * HEAD)
                narrow = slice(j * WIDE, j * WIDE + HEAD)
                dsb = ds_ref[j]
                dv_ref[:, hs] += _dot(p_ref[j], do_ref[qs, narrow], TN)
                dq_ref[qs, hs] += _dot(dsb, k_ref[:, narrow])
                dk_ref[:, hs] += _dot(dsb, q_ref[qs, narrow], TN)

        def off_diagonal(qb, carry):
            step(qb, False)
            return carry

        step(kb, True)
        lax.fori_loop(kb + 1, nk, off_diagonal, 0)
        for j in heads:
            hs = slice(j * HEAD, (j + 1) * HEAD)
            dfk_ref[j] = -jnp.sum(col_ref[j], axis=0, keepdims=True)
            dx, dg = _head_rms_bwd_tile(kraw_ref[:, hs], gk_ref[:, hs], dk_ref[:, hs])
            dukv_ref[0, :, hs] = dx.astype(dukv_ref.dtype)
            dukv_ref[1, :, hs] = dv_ref[:, hs].astype(dukv_ref.dtype)
            dgk_ref[:, hs] += dg

        @pl.when(kb == nk - 1)
        def _():
            dgq_ref[...] = jnp.zeros_like(dgq_ref)

            def rows(c, carry):
                rs = pl.ds(pl.multiple_of(c * t, t), t)
                for j in heads:
                    hs = slice(j * HEAD, (j + 1) * HEAD)
                    dx, dg = _head_rms_bwd_tile(qraw_ref[rs, hs], gq_ref[:, hs], dq_ref[rs, hs])
                    dqraw_ref[rs, hs] = dx.astype(dqraw_ref.dtype)
                    dgq_ref[:, hs] += dg
                return carry

            lax.fori_loop(0, nk, rows, 0)

    scratch = [pltpu.VMEM((PAIR, t, t), F32), pltpu.VMEM((PAIR, t, t), F32), pltpu.VMEM((PAIR, t, t), MXU),
               pltpu.VMEM((PAIR, t, t), MXU), pltpu.VMEM((PAIR, STRIP, t), F32),
               pltpu.VMEM((t, PAIR * HEAD), F32), pltpu.VMEM((t, PAIR * HEAD), F32), pltpu.VMEM((lp, PAIR * HEAD), F32)]
    whole = pl.BlockSpec((lp, PAIR * HEAD), lambda g, j: (0, g))
    wide_all = pl.BlockSpec((lp, PAIR * WIDE), lambda g, j: (0, g))
    wide_tile = pl.BlockSpec((t, PAIR * WIDE), lambda g, j: (j, g))
    vec = pl.BlockSpec((1, PAIR * HEAD), lambda g, j: (0, g))
    outs, side_results = _pallas(
        body,
        name=name,
        args=(qb, doa, ka, va, ukv, g_k, ub, g_q),
        grid=(nh // PAIR, nk),
        in_specs=[wide_all, wide_all, wide_tile, wide_tile,
                  pl.BlockSpec((None, t, PAIR * HEAD), lambda g, j: (0, j, g)), vec,
                  pl.BlockSpec((None, lp, PAIR * HEAD), lambda g, j: (0, 0, g)), vec],
        out_specs=[whole, vec, pl.BlockSpec((2, t, PAIR * HEAD), lambda g, j: (0, j, g)), vec,
                   pl.BlockSpec((PAIR, 1, t), lambda g, j: (g, 0, j))],
        out_shape=[jax.ShapeDtypeStruct((lp, d), MXU), jax.ShapeDtypeStruct((1, d), F32),
                   jax.ShapeDtypeStruct((2, lp, d), MXU), jax.ShapeDtypeStruct((1, d), F32),
                   jax.ShapeDtypeStruct((nh, 1, lp), F32)],
        scratch_shapes=scratch,
        semantics=("parallel", "arbitrary"),
        vmem_mib=48,
        sides=sides,
    )
    return (*outs, side_results) if sides else tuple(outs)


def _fox_do(d_og, o, o_lo, ub, name="fox_do", tm=384):
    lp, d = o.shape
    tm = _tile(lp, tm)
    nh = d // HEAD

    def body(dog_ref, o_ref, olo_ref, z_ref, doa_ref):
        for h in range(nh):
            hs = slice(h * HEAD, (h + 1) * HEAD)
            z = z_ref[:, hs]
            do = (dog_ref[:, hs] * (z * _sigmoid(z))).astype(doa_ref.dtype)
            delta = jnp.sum(do.astype(F32) * (o_ref[:, hs] + olo_ref[:, hs]), axis=-1, keepdims=True)
            doa_ref[:, h * WIDE:h * WIDE + HEAD] = do
            doa_ref[:, h * WIDE + HEAD:(h + 1) * WIDE] = _extra_cols(tm, _split3(-delta), (0.0, 0.0, 0.0)).astype(
                doa_ref.dtype)

    row = pl.BlockSpec((tm, d), lambda i: (i, 0))
    return pl.pallas_call(
        body,
        name=name,
        grid=(lp // tm,),
        in_specs=[row, row, row, pl.BlockSpec((None, tm, d), lambda i: (1, i, 0))],
        out_specs=pl.BlockSpec((tm, nh * WIDE), lambda i: (i, 0)),
        out_shape=jax.ShapeDtypeStruct((lp, nh * WIDE), MXU),
        compiler_params=_cparams(("parallel",), 56),
    )(d_og, o, o_lo, ub)


def _fox_gate_bwd(ub, d_og, o, dq, name="fox_gate_bwd", tm=384):
    _, lp, d = ub.shape
    tm = _tile(lp, tm)

    def body(z_ref, dog_ref, o_ref, dq_ref, dub_ref):
        z = z_ref[...]
        sz = _sigmoid(z)
        dub_ref[0] = dq_ref[...]
        dub_ref[1] = (dog_ref[...] * o_ref[...] * (sz * (1.0 + z * (1.0 - sz)))).astype(dub_ref.dtype)

    row = pl.BlockSpec((tm, d), lambda i: (i, 0))
    return pl.pallas_call(
        body,
        name=name,
        grid=(lp // tm,),
        in_specs=[pl.BlockSpec((None, tm, d), lambda i: (1, i, 0)), row, row, row],
        out_specs=pl.BlockSpec((2, tm, d), lambda i: (0, i, 0)),
        out_shape=jax.ShapeDtypeStruct((2, lp, d), MXU),
        compiler_params=_cparams(("parallel",), 48),
    )(ub, d_og, o, dq)


def _loss(h, target, name="loss_head"):
    lp, d = h.shape
    nb = lp // 128

    def body(h_ref, t_ref, loss_ref, dh_ref, acc_ref):
        i = pl.program_id(0)

        @pl.when(i == 0)
        def _():
            acc_ref[...] = jnp.zeros_like(acc_ref)
            dh_ref[...] = jnp.zeros_like(dh_ref)

        @pl.when(i > 0)
        def _():
            err = h_ref[...] - t_ref[...]
            dh_ref[...] = err * (1.0 / d)
            acc_ref[...] += jnp.sum(jnp.sum(err * err, axis=-1, keepdims=True) * (1.0 / d), axis=0, keepdims=True)

        @pl.when(i == nb - 1)
        def _():
            loss_ref[...] = 0.5 * acc_ref[...]

    return pl.pallas_call(
        body,
        name=name,
        grid=(nb,),
        in_specs=[pl.BlockSpec((128, d), lambda i: (i, 0)),
                  pl.BlockSpec((128, d), lambda i: (jnp.maximum(i - 1, 0), 0))],
        out_specs=[pl.BlockSpec((1, 1), lambda i: (0, 0)), pl.BlockSpec((128, d), lambda i: (i, 0))],
        out_shape=[jax.ShapeDtypeStruct((1, 1), F32), jax.ShapeDtypeStruct((lp, d), F32)],
        scratch_shapes=[pltpu.VMEM((1, 1), F32)],
        compiler_params=_cparams(("arbitrary",), 32),
    )(h, target)


def _adam_math(w, g, m, v):
    m = ADAM_B1 * m + (1.0 - ADAM_B1) * g
    v = ADAM_B2 * v + (1.0 - ADAM_B2) * (g * g)
    m_hat = m / (1.0 - ADAM_B1 ** ADAM_STEP)
    v_hat = v / (1.0 - ADAM_B2 ** ADAM_STEP)
    delta = -ADAM_LR * (m_hat / (jnp.sqrt(v_hat) + ADAM_EPS) + ADAM_WD * w)
    return delta, m, v


def _adamw(parts, w, m, v, name, tm=512):
    n, r, c = parts.shape
    if r % tm == 0:
        grid, blk, at = r // tm, (tm, c), lambda i: (i, 0)
    else:
        tc = _tile(c, tm)
        grid, blk, at = c // tc, (r, tc), lambda i: (0, i)

    def body(p_ref, w_ref, m_ref, v_ref, g_ref, d_ref, nm_ref, nv_ref):
        g = p_ref[0].astype(F32)
        for j in range(1, n):
            g = g + p_ref[j].astype(F32)
        g_ref[...] = g
        d_ref[...], nm_ref[...], nv_ref[...] = _adam_math(w_ref[...], g, m_ref[...], v_ref[...])

    row = pl.BlockSpec(blk, at)
    return pl.pallas_call(
        body,
        name=name,
        grid=(grid,),
        in_specs=[pl.BlockSpec((n,) + blk, lambda i: (0,) + at(i)), row, row, row],
        out_specs=[row] * 4,
        out_shape=[jax.ShapeDtypeStruct((r, c), F32)] * 4,
        compiler_params=_cparams(("parallel",), 48),
    )(parts, w, m, v)


def _adamw_small(gs, ws, ms, vs, name="adamw_small"):
    n = len(ws)
    flat = [jnp.reshape(a, (-1, a.shape[-1])) for group in (gs, ws, ms, vs) for a in group]

    def body(*refs):
        g_refs, w_refs, m_refs, v_refs = (refs[k * n:(k + 1) * n] for k in range(4))
        outs = refs[4 * n:]
        for i in range(n):
            d_new, m_new, v_new = _adam_math(w_refs[i][...], g_refs[i][...], m_refs[i][...], v_refs[i][...])
            outs[i][...], outs[n + i][...], outs[2 * n + i][...] = d_new, m_new, v_new

    res = pl.pallas_call(
        body, name=name, out_shape=[jax.ShapeDtypeStruct(a.shape, F32) for a in flat[n:2 * n]] * 3)(*flat)
    return [[jnp.reshape(r, w.shape) for r, w in zip(res[k * n:(k + 1) * n], ws)] for k in range(3)]


def _place():
    x, y, c = lax.axis_index("x"), lax.axis_index("y"), lax.axis_index("c")
    return x, y, c


def _other_chips(x, y):
    return [(1 - x, y), (x, 1 - y), (1 - x, 1 - y)]


def _any_spec():
    return pl.BlockSpec(memory_space=pl.ANY)


class _Side:
    def __init__(self, ins, outs, sems, start, finish, aliases=None):
        self.ins, self.outs, self.sems = list(ins), list(outs), sems
        self.start, self.finish = start, finish
        self.aliases = dict(aliases or {})


def _pallas(body, *, name, out_shape, args=(), grid=(), in_specs=(), out_specs=(), scratch_shapes=(),
            semantics=(), vmem_mib=None, sides=(), aliases=None):
    n_in, n_out, n_scr = len(args), len(out_shape), len(scratch_shapes)
    side_ins = [a for s in sides for a in s.ins]
    side_outs = [o for s in sides for o in s.outs]
    side_sems = [pltpu.SemaphoreType.DMA((max(k, 1),)) for s in sides for k in s.sems]
    aliases, in_at, out_at = dict(aliases or {}), n_in, n_out
    for s in sides:
        aliases.update({in_at + i: out_at + o for i, o in s.aliases.items()})
        in_at, out_at = in_at + len(s.ins), out_at + len(s.outs)

    def wrapped(*refs):
        at = [0]

        def take(k):
            got = refs[at[0]:at[0] + k]
            at[0] += k
            return got

        main_in = take(n_in)
        s_in = [take(len(s.ins)) for s in sides]
        main_out = take(n_out)
        s_out = [take(len(s.outs)) for s in sides]
        main_scr = take(n_scr)
        s_sem = [take(3) for s in sides]

        def run(stage):
            for s, i_, o_, m_ in zip(sides, s_in, s_out, s_sem):
                getattr(s, stage)(i_, o_, *m_)

        first = last = None
        for k, g in enumerate(grid):
            i = pl.program_id(k)
            first = (i == 0) if first is None else jnp.logical_and(first, i == 0)
            last = (i == g - 1) if last is None else jnp.logical_and(last, i == g - 1)
        if sides:
            run("start") if first is None else pl.when(first)(lambda: run("start"))
        body(*main_in, *main_out, *main_scr)
        if sides:
            run("finish") if last is None else pl.when(last)(lambda: run("finish"))

    kw = {}
    if grid:
        kw["grid"] = grid
    if semantics or vmem_mib:
        sem = tuple("arbitrary" for _ in grid) if sides else tuple(semantics)
        kw["compiler_params"] = pltpu.CompilerParams(
            dimension_semantics=sem or None, vmem_limit_bytes=vmem_mib * MIB if vmem_mib else None)
    res = pl.pallas_call(
        wrapped,
        name=name,
        in_specs=list(in_specs) + [_any_spec()] * len(side_ins),
        out_specs=list(out_specs) + [_any_spec()] * len(side_outs),
        out_shape=list(out_shape) + side_outs,
        scratch_shapes=list(scratch_shapes) + side_sems,
        input_output_aliases=aliases,
        **kw,
    )(*args, *side_ins)
    main, rest, per_side = list(res[:n_out]), list(res[n_out:]), []
    for s in sides:
        per_side.append(rest[:len(s.outs)])
        rest = rest[len(s.outs):]
    return main, per_side


def _slot(p):
    return 4 * p[0] + 2 * p[1] + p[2]


def _sibling_side(grads):
    n = len(grads)

    def copies(ins, outs, send_sems, recv_sems):
        x, y, c = _place()
        return [pltpu.make_async_remote_copy(
            src_ref=ins[t].at[2 * chip + (1 - c)], dst_ref=outs[t].at[chip],
            send_sem=send_sems.at[4 * t + chip], recv_sem=recv_sems.at[4 * t + chip],
            device_id=(x, y, 1 - c), device_id_type=MESH) for t in range(n) for chip in range(N_CHIP)]

    def start(ins, outs, send_sems, recv_sems, local_sems):
        for cp in copies(ins, outs, send_sems, recv_sems):
            cp.start()

    def finish(ins, outs, send_sems, recv_sems, local_sems):
        for cp in copies(ins, outs, send_sems, recv_sems):
            cp.wait()

    outs = [jax.ShapeDtypeStruct((N_CHIP,) + g.shape[1:], g.dtype) for g in grads]
    return _Side(grads, outs, (4 * n, 4 * n, 0), start, finish)


def _chips_side(partials):
    n = len(partials)

    def copies(ins, outs, send_sems, recv_sems, local_sems):
        x, y, c = _place()
        my_chip = 2 * x + y
        local = [pltpu.make_async_copy(ins[t].at[my_chip], outs[t].at[my_chip], local_sems.at[t]) for t in range(n)]
        sends, recvs = [], []
        for t in range(n):
            for j, chip in enumerate(_other_chips(x, y)):
                their = 2 * chip[0] + chip[1]
                sems = dict(send_sem=send_sems.at[3 * t + j], recv_sem=recv_sems.at[3 * t + j],
                            device_id=(*chip, c), device_id_type=MESH)
                sends.append(pltpu.make_async_remote_copy(src_ref=ins[t].at[their], dst_ref=outs[t].at[my_chip], **sems))
                recvs.append(pltpu.make_async_remote_copy(src_ref=ins[t].at[my_chip], dst_ref=outs[t].at[their], **sems))
        return local, sends, recvs

    def start(ins, outs, send_sems, recv_sems, local_sems):
        local, sends, _ = copies(ins, outs, send_sems, recv_sems, local_sems)
        for cp in local + sends:
            cp.start()

    def finish(ins, outs, send_sems, recv_sems, local_sems):
        local, sends, recvs = copies(ins, outs, send_sems, recv_sems, local_sems)
        for cp in sends:
            cp.wait_send()
        for cp in recvs:
            cp.wait_recv()
        for cp in local:
            cp.wait()

    outs = [jax.ShapeDtypeStruct(p.shape, p.dtype) for p in partials]
    return _Side(partials, outs, (3 * n, 3 * n, n), start, finish)


def _gather_own_side(blocks):
    n = len(blocks)

    def copies(ins, outs, send_sems, recv_sems, local_sems):
        x, y, c = _place()
        me = (x, y, c)
        peers = [(x, y, 1 - c)] + [(*chip, c) for chip in _other_chips(x, y)]
        local = [pltpu.make_async_copy(ins[t], outs[t].at[_slot(me)], local_sems.at[t]) for t in range(n)]
        sends, recvs = [], []
        for t in range(n):
            for k, peer in enumerate(peers):
                sems = dict(send_sem=send_sems.at[4 * t + k], recv_sem=recv_sems.at[4 * t + k],
                            device_id=peer, device_id_type=MESH)
                sends.append(pltpu.make_async_remote_copy(src_ref=ins[t], dst_ref=outs[t].at[_slot(me)], **sems))
                recvs.append(pltpu.make_async_remote_copy(src_ref=ins[t], dst_ref=outs[t].at[_slot(peer)], **sems))
        return local, sends, recvs

    def start(ins, outs, send_sems, recv_sems, local_sems):
        local, sends, _ = copies(ins, outs, send_sems, recv_sems, local_sems)
        for cp in local + sends:
            cp.start()

    def finish(ins, outs, send_sems, recv_sems, local_sems):
        local, sends, recvs = copies(ins, outs, send_sems, recv_sems, local_sems)
        for cp in sends:
            cp.wait_send()
        for cp in recvs:
            cp.wait_recv()
        for cp in local:
            cp.wait()

    outs = [jax.ShapeDtypeStruct((N_DEV,) + b.shape, b.dtype) for b in blocks]
    return _Side(blocks, outs, (4 * n, 4 * n, n), start, finish)


def _gather_pass_side(gathered):
    n = len(gathered)

    def copies(outs, send_sems, recv_sems):
        x, y, c = _place()
        sends, recvs = [], []
        for t in range(n):
            for j, chip in enumerate(_other_chips(x, y)):
                sems = dict(send_sem=send_sems.at[3 * t + j], recv_sem=recv_sems.at[3 * t + j],
                            device_id=(x, y, 1 - c), device_id_type=MESH)
                mine, theirs = outs[t].at[_slot((*chip, c))], outs[t].at[_slot((*chip, 1 - c))]
                sends.append(pltpu.make_async_remote_copy(src_ref=mine, dst_ref=mine, **sems))
                recvs.append(pltpu.make_async_remote_copy(src_ref=mine, dst_ref=theirs, **sems))
        return sends, recvs

    def start(ins, outs, send_sems, recv_sems, local_sems):
        for cp in copies(outs, send_sems, recv_sems)[0]:
            cp.start()

    def finish(ins, outs, send_sems, recv_sems, local_sems):
        sends, recvs = copies(outs, send_sems, recv_sems)
        for cp in sends:
            cp.wait_send()
        for cp in recvs:
            cp.wait_recv()

    outs = [jax.ShapeDtypeStruct(g.shape, g.dtype) for g in gathered]
    return _Side(gathered, outs, (3 * n, 3 * n, 0), start, finish, aliases={t: t for t in range(n)})


def _alone(side, name):
    return _pallas(lambda: None, name=name, out_shape=[], sides=[side])[1][0]


def _all_gather(blocks, name):
    n = len(blocks)

    def body(*refs):
        in_refs, out_refs = refs[:n], refs[n:2 * n]
        send_sems, recv_sems, local_sems = refs[2 * n:]
        x, y, c = _place()
        me, sibling = (x, y, c), (x, y, 1 - c)
        chips = _other_chips(x, y)

        def slot(p):
            return 4 * p[0] + 2 * p[1] + p[2]

        def copy(t, k, block, to, src=None):
            dst = out_refs[t].at[slot(block)]
            return pltpu.make_async_remote_copy(
                src_ref=dst if src is None else src, dst_ref=dst,
                send_sem=send_sems.at[7 * t + k], recv_sem=recv_sems.at[7 * t + k],
                device_id=to, device_id_type=MESH)

        started = []
        for t in range(n):
            mine = pltpu.make_async_copy(in_refs[t], out_refs[t].at[slot(me)], local_sems.at[t])
            mine.start()
            started.append(mine)
        sends = []
        for t in range(n):
            first = [copy(t, 0, me, sibling, src=in_refs[t])]
            first += [copy(t, 1 + j, me, (*chip, c), src=in_refs[t]) for j, chip in enumerate(chips)]
            for cp in first:
                cp.start()
            sends += first
        for t in range(n):
            for j, chip in enumerate(chips):
                copy(t, 1 + j, (*chip, c), me).wait_recv()
                passed = copy(t, 4 + j, (*chip, c), sibling)
                passed.start()
                sends.append(passed)
        for t in range(n):
            copy(t, 0, sibling, me).wait_recv()
            for j, chip in enumerate(chips):
                copy(t, 4 + j, (*chip, 1 - c), me).wait_recv()
        for cp in sends:
            cp.wait_send()
        for mine in started:
            mine.wait()

    return pl.pallas_call(
        body,
        name=name,
        in_specs=[_any_spec()] * n,
        out_specs=[_any_spec()] * n,
        out_shape=[jax.ShapeDtypeStruct((N_DEV,) + b.shape, b.dtype) for b in blocks],
        scratch_shapes=[pltpu.SemaphoreType.DMA((7 * n,)), pltpu.SemaphoreType.DMA((7 * n,)),
                        pltpu.SemaphoreType.DMA((n,))],
    )(*blocks)


def _pair_sum(g8, got, name, tm=1024):
    _, r, c = g8.shape
    tm = tm if r % tm == 0 else r
    core = lax.axis_index("c")

    def body(core_ref, mine_ref, got_ref, o_ref):
        south_first = core_ref[0] == 0
        a, b = mine_ref[...], got_ref[...]
        o_ref[...] = (jnp.where(south_first, a, b) + jnp.where(south_first, b, a)).astype(o_ref.dtype)

    return pl.pallas_call(
        body,
        name=name,
        grid_spec=pltpu.PrefetchScalarGridSpec(
            num_scalar_prefetch=1,
            grid=(N_CHIP, r // tm),
            in_specs=[pl.BlockSpec((None, tm, c), lambda j, i, core_ref: (2 * j + core_ref[0], i, 0)),
                      pl.BlockSpec((None, tm, c), lambda j, i, core_ref: (j, i, 0))],
            out_specs=pl.BlockSpec((None, tm, c), lambda j, i, core_ref: (j, i, 0)),
        ),
        out_shape=jax.ShapeDtypeStruct((N_CHIP, r, c), PAYLOAD),
        compiler_params=_cparams(("parallel", "parallel"), 32),
    )(jnp.reshape(core, (1,)).astype(jnp.int32), g8, got)


def _all_reduce_small(s, name="small_all_reduce"):
    r, c = s.shape

    def body(s_ref, o_ref, buf_ref, send_sems, recv_sems):
        x, y, c_ = _place()
        me = 4 * x + 2 * y + c_
        buf_ref[me] = s_ref[...]

        def copy(k, slot, peer):
            return pltpu.make_async_remote_copy(
                src_ref=s_ref, dst_ref=buf_ref.at[slot],
                send_sem=send_sems.at[k - 1], recv_sem=recv_sems.at[k - 1],
                device_id=peer, device_id_type=MESH)

        peers = []
        for k in range(1, N_DEV):
            peer = (x ^ ((k >> 2) & 1), y ^ ((k >> 1) & 1), c_ ^ (k & 1))
            peers.append(peer)
            copy(k, me, peer).start()
        for k, peer in zip(range(1, N_DEV), peers):
            cp = copy(k, 4 * peer[0] + 2 * peer[1] + peer[2], peer)
            cp.wait_send()
            cp.wait_recv()
        total = buf_ref[0]
        for j in range(1, N_DEV):
            total = total + buf_ref[j]
        o_ref[...] = total

    return pl.pallas_call(
        body,
        name=name,
        out_shape=jax.ShapeDtypeStruct((r, c), F32),
        in_specs=[pl.BlockSpec(memory_space=pltpu.VMEM)],
        out_specs=pl.BlockSpec(memory_space=pltpu.VMEM),
        scratch_shapes=[pltpu.VMEM((N_DEV, r, c), F32), pltpu.SemaphoreType.DMA((N_DEV - 1,)),
                        pltpu.SemaphoreType.DMA((N_DEV - 1,))],
    )(s)


def _kv_weights(g_kv):
    d = g_kv.shape[1]
    nh = d // HEAD
    wkv = jnp.reshape(jnp.transpose(g_kv, (1, 0, 2)), (d, -1))
    return wkv[:, :2 * d], jnp.pad(wkv[:, 2 * d:], ((0, 0), (0, 128 - nh)))


def _kv_grad_blocks(dwkv2_t, dwfl_t):
    d = dwkv2_t.shape[1]
    return jnp.reshape(jnp.concatenate([dwkv2_t, dwfl_t[:d // HEAD]], axis=0), (N_DEV, -1, d))


def _local_step(x, target, meta, gamma, a_norm, wa_in, a_out_norm, wa_out, kv_norm, late, b_f, g_k, b_norm, g_q,
                dist):
    d = x.shape[1]
    nh = d // HEAD
    cols = d // N_DEV
    big = dict(tm=1408, tn=512, tk=2048)

    lb = _lb_fwd(gamma)
    h0, hn_a = _embed(x, meta, a_norm)
    lp = h0.shape[0]
    if dist:
        a_out_blk, kv_blk, b_in_blk, b_out_blk = late
        u4, (first,) = _matmul(hn_a, wa_in, "nn", F32, "a_in", out_parts=4,
                               sides=[_gather_own_side([a_out_blk, b_in_blk])], **big)
        o_a, og_a, ((g_a_out, g_b_in), (g_kv,)) = _hgrn2_fwd(
            u4, lb, a_out_norm, sides=[_gather_pass_side(first), _gather_own_side([kv_blk])])
        wa_out = jnp.reshape(g_a_out, (d, d))
        h1, ((g_kv,), (g_b_out,)) = _matmul(og_a, wa_out, "nn", F32, "a_out", add=h0,
                                            sides=[_gather_pass_side([g_kv]), _gather_own_side([b_out_blk])], **big)
        (wkv2, wfl), wb_in = _kv_weights(g_kv), g_b_in
    else:
        u4 = _matmul(hn_a, wa_in, "nn", F32, "a_in", out_parts=4, **big)
        o_a, og_a = _hgrn2_fwd(u4, lb, a_out_norm)
        h1 = _matmul(og_a, wa_out, "nn", F32, "a_out", add=h0, **big)
        wkv2, wfl, wb_in, wb_out = late
    hk, hb = _rms_fwd(h1, [kv_norm, b_norm], "rms_kv_b")
    if dist:
        ukv, ((g_b_out,),) = _matmul(hk, wkv2, "nn", F32, "kv_in", out_parts=2,
                                     sides=[_gather_pass_side([g_b_out])], **big)
        wb_out = jnp.reshape(g_b_out, (d, d))
    else:
        ukv = _matmul(hk, wkv2, "nn", F32, "kv_in", out_parts=2, **big)
    ufl = _matmul(hk, wfl, "nn", F32, "kv_f", **big)
    ub = _matmul(hb, wb_in, "nn", F32, "b_in", out_parts=2, **big)
    f_cum = _fgate_fwd(ufl, b_f)
    qa, ka, va = _attn_operands(ub, ukv, g_q, g_k, f_cum)
    o_b, o_lo, og_b, qb = _fox_fwd(qa, ka, va, ub)
    h2 = _matmul(og_b, wb_out, "nn", F32, "b_out", add=h1, **big)
    loss, dh2 = _loss(h2, target)

    dx_t = dict(tm=1408, tn=512, tk=2048, vmem_mib=56)
    dx_parts_t = dict(tm=704, tn=512, tk=2048, vmem_mib=58, k_whole=True)
    dw_t = dict(tm=512, tn=1024, tk=lp, vmem_mib=58)
    dw_f32_t = dict(tm=1024, tn=512, tk=lp, vmem_mib=58)
    def to_sibling(g8):
        return [_sibling_side([g8])] if dist else []

    def to_chips(partial):
        return [_chips_side([partial])] if dist else []

    def unpack(res, n_sides):
        if not dist:
            return res, [None] * n_sides
        *main, side_results = res
        return (main[0] if len(main) == 1 else tuple(main)), [r[0] for r in side_results]

    dwb_out = _matmul(og_b, dh2, "tn", F32, "b_out_dw", **dw_f32_t)
    g8_b_out = jnp.reshape(dwb_out, (N_DEV, cols, d))
    d_ogb, (got,) = unpack(_matmul(dh2, wb_out, "nt", F32, "b_out_dx", sides=to_sibling(g8_b_out), **dx_t), 1)
    p_b_out = _pair_sum(g8_b_out, got, "pair_sum_b_w_out") if dist else None
    doa = _fox_do(d_ogb, o_b, o_lo, ub)
    (dq, dg_q, dukv, dg_k, dfk), (r_b_out,) = unpack(
        _fox_bwd(qb, ka, va, doa, ukv, g_k, ub, g_q, sides=to_chips(p_b_out)), 1)
    dub = _fox_gate_bwd(ub, d_ogb, o_b, dq)
    d_f = jnp.pad(jnp.transpose(dfk[:, 0, :]), ((0, 0), (0, 128 - nh)))
    dufl, db_f = _fgate_bwd(ufl, b_f, d_f)
    dwb_in = _matmul(hb, dub, "tn", F32, "b_in_dw", out_parts=N_DEV, **dw_t)
    d_hb, (got,) = unpack(_matmul(dub, wb_in, "nt", F32, "b_in_dx", sides=to_sibling(dwb_in), **dx_parts_t), 1)
    p_b_in = _pair_sum(dwb_in, got, "pair_sum_b_w_in") if dist else None
    d_hk, (r_b_in,) = unpack(_matmul(dukv, wkv2, "nt", F32, "kv_dx", sides=to_chips(p_b_in), **dx_parts_t), 1)
    d_hk = _matmul(dufl, wfl, "nt", F32, "kv_f_dx", add=d_hk, **dx_t)
    dwkv2_t = _matmul(dukv, hk, "tn", F32, "kv_dw", **dw_t)
    dwfl_t = _matmul(dufl, hk, "tn", F32, "kv_f_dw", **dw_t)
    g8_kv = _kv_grad_blocks(dwkv2_t, dwfl_t) if dist else None
    dh1, (dg_kv, dg_b) = _rms_bwd(h1, [kv_norm, b_norm], [d_hk, d_hb], dh2, "rms_kv_b_bwd")
    d_oga, (got,) = unpack(_matmul(dh1, wa_out, "nt", F32, "a_out_dx", sides=to_sibling(g8_kv), **dx_t), 1)
    p_kv = _pair_sum(g8_kv, got, "pair_sum_kv_w") if dist else None
    dwa_out = _matmul(og_a, dh1, "tn", F32, "a_out_dw", **dw_f32_t)
    g8_a_out = jnp.reshape(dwa_out, (N_DEV, cols, d))
    (du4, dlb, dg_aout), (r_kv, got) = unpack(
        _hgrn2_bwd(u4, o_a, d_oga, lb, a_out_norm, sides=to_chips(p_kv) + to_sibling(g8_a_out)), 2)
    p_a_out = _pair_sum(g8_a_out, got, "pair_sum_a_w_out") if dist else None
    dwa_in, (r_a_out,) = unpack(
        _matmul(hn_a, du4, "tn", F32, "a_in_dw", out_parts=N_DEV, sides=to_chips(p_a_out), **dw_t), 1)
    row_tiles = lp // _tile(lp, dx_parts_t["tm"], 64)
    if dist and row_tiles >= 2:
        top = max(1, row_tiles // 6)
        d_hna, ((got,),) = _matmul(du4, wa_in, "nt", F32, "a_in_dx_top", rows=(0, top),
                                   sides=to_sibling(dwa_in), **dx_parts_t)
        p_a_in = _pair_sum(dwa_in, got, "pair_sum_a_w_in")
        d_hna, ((r_a_in,),) = _matmul(du4, wa_in, "nt", F32, "a_in_dx_bottom", rows=(top, row_tiles - top), into=d_hna,
                                      sides=to_chips(p_a_in), **dx_parts_t)
    else:
        d_hna, (got,) = unpack(_matmul(du4, wa_in, "nt", F32, "a_in_dx", sides=to_sibling(dwa_in), **dx_parts_t), 1)
        if dist:
            (r_a_in,) = _alone(_chips_side([_pair_sum(dwa_in, got, "pair_sum_a_w_in")]), "grads_to_chips_a_w_in")
    grad_x, dmeta, dg_a = _embed_bwd(h0, a_norm, d_hna, dh1)
    dgamma = _lb_bwd(gamma, dlb)

    grads = dict(meta=dmeta, gamma=dgamma, a_norm=dg_a, a_out_norm=dg_aout, kv_norm=dg_kv,
                 b_f=db_f, g_k=dg_k, b_norm=dg_b, g_q=dg_q)
    if dist:
        grads.update(wa_in=r_a_in, wa_out=r_a_out, wkv=r_kv, wb_in=r_b_in, wb_out=r_b_out)
    else:
        grads.update(wa_in=dwa_in, wa_out=dwa_out, wkv2=dwkv2_t.T, wfl=dwfl_t.T, wb_in=dwb_in, wb_out=dwb_out)
    return loss, grad_x, grads


def kernel(x, meta, gamma_lb, a_norm, a_w_in, a_out_norm, a_w_out, kv_norm, kv_w, fox_b_f, fox_k_norm, b_norm, b_w_in, b_q_norm, b_w_out, loss_target, m_meta, m_gamma_lb, m_a_norm, m_a_w_in, m_a_out_norm, m_a_w_out, m_kv_norm, m_kv_w, m_fox_b_f, m_fox_k_norm, m_b_norm, m_b_w_in, m_b_q_norm, m_b_w_out, v_meta, v_gamma_lb, v_a_norm, v_a_w_in, v_a_out_norm, v_a_w_out, v_kv_norm, v_kv_w, v_fox_b_f, v_fox_k_norm, v_b_norm, v_b_w_in, v_b_q_norm, v_b_w_out):
    d = x.shape[-1]
    nh = d // HEAD
    cols = d // N_DEV
    me = 4 * lax.axis_index("x") + 2 * lax.axis_index("y") + lax.axis_index("c")

    sharded_small = jnp.concatenate([meta, gamma_lb, a_norm, a_out_norm, jnp.zeros((4, cols), F32)], axis=0)
    g_a_in, g_small = _all_gather([a_w_in[0].astype(MXU), sharded_small], "gather_weights")
    wa_out = None
    late = (a_w_out[0].astype(MXU), kv_w.astype(MXU), b_w_in[0].astype(MXU), b_w_out[0].astype(MXU))
    small = jnp.reshape(jnp.transpose(g_small, (1, 0, 2)), (-1, d))
    meta_f, gamma_f, a_norm_f, a_out_norm_f = small[:16], small[16:18], small[18:19], small[19:20]
    b_f = jnp.pad(jnp.reshape(fox_b_f, (1, nh)), ((0, 0), (0, 128 - nh)))
    g_k = jnp.reshape(fox_k_norm, (1, d))
    g_q = jnp.reshape(b_q_norm, (1, d))
    kv_norm_r = jnp.reshape(kv_norm, (1, d))

    loss, grad_x, g = _local_step(x[0], loss_target[0], meta_f, gamma_f, a_norm_f, g_a_in, a_out_norm_f, wa_out,
                                  kv_norm_r, late, b_f, g_k, b_norm, g_q, dist=True)
    loss = lax.psum(loss[0, 0], AXES)

    r_a_in = _adamw(g["wa_in"], a_w_in[0], m_a_w_in[0], v_a_w_in[0], "adamw_a_w_in")
    r_a_out = _adamw(g["wa_out"], a_w_out[0], m_a_w_out[0], v_a_w_out[0], "adamw_a_w_out")
    r_kv = [jnp.transpose(r) for r in _adamw(g["wkv"], kv_w.T, m_kv_w.T, v_kv_w.T, "adamw_kv_w")]
    r_b_in = _adamw(g["wb_in"], b_w_in[0], m_b_w_in[0], v_b_w_in[0], "adamw_b_w_in")
    r_b_out = _adamw(g["wb_out"], b_w_out[0], m_b_w_out[0], v_b_w_out[0], "adamw_b_w_out")

    packed = jnp.concatenate(
        [g["meta"], g["gamma"], g["a_norm"], g["a_out_norm"], g["kv_norm"], g["b_norm"], g["g_k"], g["g_q"],
         jnp.pad(g["b_f"], ((0, 0), (0, d - 128))), jnp.zeros((7, d), F32)], axis=0)
    tot = _all_reduce_small(packed)
    mine = lax.dynamic_slice_in_dim(tot[:20], me * cols, cols, axis=1)
    gs = [mine[:16], mine[16:18], mine[18:19], mine[19:20], tot[20], tot[21:22], jnp.reshape(tot[22], (nh, HEAD)),
          jnp.reshape(tot[23], (1, nh, HEAD)), tot[24, :nh]]
    small_w = [meta, gamma_lb, a_norm, a_out_norm, kv_norm, b_norm, fox_k_norm, b_q_norm, fox_b_f]
    small_m = [m_meta, m_gamma_lb, m_a_norm, m_a_out_norm, m_kv_norm, m_b_norm, m_fox_k_norm, m_b_q_norm, m_fox_b_f]
    small_v = [v_meta, v_gamma_lb, v_a_norm, v_a_out_norm, v_kv_norm, v_b_norm, v_fox_k_norm, v_b_q_norm, v_fox_b_f]

    d_s, m_s, v_s = _adamw_small(gs, small_w, small_m, small_v)

    def ordered(s, a_in, a_out, kv, b_in, b_out):
        return [s[0], s[1], s[2], a_in[None], s[3], a_out[None], s[4], kv, s[8], s[6], s[5], b_in[None], s[7], b_out[None]]

    outs = []
    for i, s in enumerate([gs, d_s, m_s, v_s]):
        outs += ordered(s, r_a_in[i], r_a_out[i], r_kv[i], r_b_in[i], r_b_out[i])
    return (loss, grad_x[None], *outs)
```

```python
import math

import jax
import jax.numpy as jnp
from jax import lax
from jax.experimental import pallas as pl
from jax.experimental.pallas import tpu as pltpu

HEAD = 128
CHUNK = 64
N_META = 16
ROW_PAD = 128 - N_META
EPS = 1e-6
MASK_VALUE = -1e30
ADAM_LR = 0.001
ADAM_B1 = 0.9
ADAM_B2 = 0.999
ADAM_EPS = 1e-08
ADAM_WD = 0.01
ADAM_STEP = 10
N_DEV = 8
N_CHIP = 4
MIB = 1024 * 1024
AXES = ("x", "y", "c")
MESH = pl.DeviceIdType.MESH

F32 = jnp.float32
MXU = jnp.bfloat16
PAYLOAD = jnp.bfloat16
HI = lax.Precision.HIGHEST

NN = (((1,), (0,)), ((), ()))
NT = (((1,), (1,)), ((), ()))
TN = (((0,), (0,)), ((), ()))


def _dot(a, b, dims=NN):
    return lax.dot_general(a.astype(MXU), b.astype(MXU), dims, preferred_element_type=F32)


def _dot_exact(a, b):
    return lax.dot_general(a, b, NN, precision=HI, preferred_element_type=F32)


def _sigmoid(x):
    return 1.0 / (1.0 + jnp.exp(-x))


def _tile(dim, target, unit=128):
    best = None
    t = unit
    while t <= min(dim, target):
        if dim % t == 0:
            best = t
        t += unit
    return best if best is not None else dim


def _cparams(semantics, vmem_mib):
    return pltpu.CompilerParams(dimension_semantics=semantics, vmem_limit_bytes=vmem_mib * MIB)


def _mat_spec(arr, br, bc, rc_of_grid):
    if arr.ndim == 2:
        return pl.BlockSpec((br, bc), rc_of_grid)
    assert arr.shape[2] % bc == 0, (arr.shape, bc)
    per = arr.shape[2] // bc

    def idx(*g):
        r, c = rc_of_grid(*g)
        return (c // per, r, c % per)

    return pl.BlockSpec((None, br, bc), idx)


def _mat_shape(arr):
    return (arr.shape[0], arr.shape[1]) if arr.ndim == 2 else (arr.shape[1], arr.shape[0] * arr.shape[2])


def _matmul(a, b, dims, out_dtype, name, *, add=None, out_parts=1, tm=512, tn=512, tk=512, vmem_mib=48, sides=(),
            k_whole=False, rows=None, into=None):
    ar, ac = _mat_shape(a)
    br_, bc_ = _mat_shape(b)
    if dims == "nn":
        m, k, n = ar, ac, bc_
        assert br_ == k
    elif dims == "nt":
        m, k, n = ar, ac, br_
        assert bc_ == k
    else:
        m, k, n = ac, ar, bc_
        assert br_ == k
    m_unit, n_unit, k_unit = m, n, k
    if a.ndim == 3:
        if dims == "tn":
            m_unit = math.gcd(m_unit, a.shape[2])
        else:
            k_unit = math.gcd(k_unit, a.shape[2])
    if b.ndim == 3:
        if dims == "nt":
            k_unit = math.gcd(k_unit, b.shape[2])
        else:
            n_unit = math.gcd(n_unit, b.shape[2])
    if out_parts > 1:
        n_unit = math.gcd(n_unit, n // out_parts)
    tm = _tile(m_unit, tm, 128 if dims == "tn" else 64)
    tn, tk = _tile(n_unit, tn), _tile(k_unit, tk)
    whole_k = dims == "nt" and tk < k and k_whole
    if whole_k:
        k_chunk, tk = tk, k
    gm, gn, gk = m // tm, n // tn, k // tk
    assert gm * tm == m and gn * tn == n and gk * tk == k, (name, m, n, k, tm, tn, tk)

    def chunk_of(ref, arr, c):
        if arr.ndim == 2:
            return ref[:, c * k_chunk:(c + 1) * k_chunk]
        per = arr.shape[2] // k_chunk
        return ref[c // per, :, (c % per) * k_chunk:(c % per + 1) * k_chunk]

    def all_cols(arr, rows, row_of_grid):
        if arr.ndim == 2:
            return pl.BlockSpec((rows, arr.shape[1]), lambda i, j, kk: (row_of_grid(i, j), 0))
        return pl.BlockSpec((arr.shape[0], rows, arr.shape[2]), lambda i, j, kk: (0, row_of_grid(i, j), 0))

    if dims == "nn":
        a_spec = _mat_spec(a, tm, tk, lambda i, j, kk: (i, kk))
        b_spec = _mat_spec(b, tk, tn, lambda i, j, kk: (kk, j))
        dn = NN
    elif whole_k:
        a_spec = all_cols(a, tm, lambda i, j: i)
        b_spec = all_cols(b, tn, lambda i, j: j)
        dn = NT
    elif dims == "nt":
        a_spec = _mat_spec(a, tm, tk, lambda i, j, kk: (i, kk))
        b_spec = _mat_spec(b, tn, tk, lambda i, j, kk: (j, kk))
        dn = NT
    else:
        a_spec = _mat_spec(a, tk, tm, lambda i, j, kk: (kk, i))
        b_spec = _mat_spec(b, tk, tn, lambda i, j, kk: (kk, j))
        dn = TN

    if out_parts > 1:
        per = (n // out_parts) // tn
        out_shape = jax.ShapeDtypeStruct((out_parts, m, n // out_parts), out_dtype)
        o_spec = pl.BlockSpec((None, tm, tn), lambda i, j, kk: (j // per, i, j % per))
    else:
        out_shape = jax.ShapeDtypeStruct((m, n), out_dtype)
        o_spec = pl.BlockSpec((tm, tn), lambda i, j, kk: (i, j))

    in_specs = [a_spec, b_spec]
    args = [a, b]
    if add is not None:
        in_specs.append(pl.BlockSpec((tm, tn), lambda i, j, kk: (i, j)))
        args.append(add)
    if rows is not None:
        first_tile, gm = rows

        def shifted(spec):
            return pl.BlockSpec(spec.block_shape, lambda i, j, kk: spec.index_map(i + first_tile, j, kk))

        in_specs = [shifted(s) for s in in_specs]
        o_spec = shifted(o_spec)
    aliases = {}
    if into is not None:
        aliases[len(args)] = 0
        in_specs.append(_any_spec())
        args.append(into)

    def body(*refs):
        a_ref, b_ref, *others, o_ref, acc_ref = refs
        add_ref = others[0] if add is not None else None
        kk = pl.program_id(2)
        if whole_k:
            part = sum(lax.dot_general(chunk_of(a_ref, a, c).astype(MXU), chunk_of(b_ref, b, c).astype(MXU), dn,
                                       preferred_element_type=F32) for c in range(k // k_chunk))
        else:
            part = lax.dot_general(a_ref[...].astype(MXU), b_ref[...].astype(MXU), dn, preferred_element_type=F32)

        def finish(total):
            if add_ref is not None:
                total = total + add_ref[...]
            o_ref[...] = total.astype(o_ref.dtype)

        if gk == 1:
            finish(part)
        else:
            @pl.when(kk == 0)
            def _():
                acc_ref[...] = part

            @pl.when(jnp.logical_and(kk > 0, kk < gk - 1))
            def _():
                acc_ref[...] += part

            @pl.when(kk == gk - 1)
            def _():
                finish(acc_ref[...] + part)

    (out,), side_results = _pallas(
        body,
        name=name,
        args=args,
        grid=(gm, gn, gk),
        in_specs=in_specs,
        out_specs=[o_spec],
        out_shape=[out_shape],
        scratch_shapes=[pltpu.VMEM((tm, tn) if gk > 1 else (8, 128), F32)],
        semantics=("parallel", "parallel", "arbitrary"),
        vmem_mib=vmem_mib,
        sides=sides,
        aliases=aliases,
    )
    return (out, side_results) if sides else out


def _rms_fwd(h, gains, name, tm=384):
    lp, d = h.shape
    tm = _tile(lp, tm)
    n = len(gains)

    def body(*refs):
        h_ref = refs[0]
        g_refs = refs[1:1 + n]
        o_refs = refs[1 + n:]
        x = h_ref[...]
        y = x * lax.rsqrt(jnp.mean(x * x, axis=-1, keepdims=True) + EPS)
        for g_ref, o_ref in zip(g_refs, o_refs):
            o_ref[...] = (y * g_ref[...]).astype(o_ref.dtype)

    row = pl.BlockSpec((tm, d), lambda i: (i, 0))
    vec = pl.BlockSpec((1, d), lambda i: (0, 0))
    return pl.pallas_call(
        body,
        name=name,
        grid=(lp // tm,),
        in_specs=[row] + [vec] * n,
        out_specs=[row] * n,
        out_shape=[jax.ShapeDtypeStruct((lp, d), MXU)] * n,
        compiler_params=_cparams(("parallel",), 40),
    )(h, *gains)


def _embed(x, meta, gain, name="embed"):
    seq, d = x.shape
    lp = ROW_PAD + N_META + seq
    blk = ROW_PAD + N_META

    def body(x_ref, meta_ref, g_ref, h_ref, o_ref):
        i = pl.program_id(0)

        @pl.when(i == 0)
        def _():
            h_ref[0:ROW_PAD, :] = jnp.zeros((ROW_PAD, d), F32)
            h_ref[ROW_PAD:blk, :] = meta_ref[...]

        @pl.when(i > 0)
        def _():
            h_ref[...] = x_ref[...]

        h = h_ref[...]
        y = h * lax.rsqrt(jnp.mean(h * h, axis=-1, keepdims=True) + EPS)
        o_ref[...] = (y * g_ref[...]).astype(o_ref.dtype)

    row = pl.BlockSpec((blk, d), lambda i: (i, 0))
    return pl.pallas_call(
        body,
        name=name,
        grid=(lp // blk,),
        in_specs=[pl.BlockSpec((blk, d), lambda i: (jnp.maximum(i - 1, 0), 0)),
                  pl.BlockSpec((N_META, d), lambda i: (0, 0)), pl.BlockSpec((1, d), lambda i: (0, 0))],
        out_specs=[row, row],
        out_shape=[jax.ShapeDtypeStruct((lp, d), F32), jax.ShapeDtypeStruct((lp, d), MXU)],
        compiler_params=_cparams(("arbitrary",), 32),
    )(x, meta, gain)


def _embed_bwd(h, gain, dy, res, name="embed_bwd"):
    lp, d = h.shape
    blk = ROW_PAD + N_META

    def body(h_ref, res_ref, g_ref, dy_ref, dx_ref, dmeta_ref, dg_ref):
        i = pl.program_id(0)
        x = h_ref[...]
        rstd = lax.rsqrt(jnp.mean(x * x, axis=-1, keepdims=True) + EPS)
        xhat = x * rstd
        dy = dy_ref[...]
        gdy = dy * g_ref[...]
        dh = res_ref[...] + rstd * (gdy - xhat * jnp.mean(gdy * xhat, axis=-1, keepdims=True))
        part = jnp.sum(dy * xhat, axis=0, keepdims=True)

        @pl.when(i == 0)
        def _():
            dmeta_ref[...] = dh[ROW_PAD:blk, :]
            dg_ref[...] = part

        @pl.when(i > 0)
        def _():
            dx_ref[...] = dh
            dg_ref[...] += part

    row = pl.BlockSpec((blk, d), lambda i: (i, 0))
    vec = pl.BlockSpec((1, d), lambda i: (0, 0))
    return pl.pallas_call(
        body,
        name=name,
        grid=(lp // blk,),
        in_specs=[row, row, vec, row],
        out_specs=[pl.BlockSpec((blk, d), lambda i: (jnp.maximum(i - 1, 0), 0)),
                   pl.BlockSpec((N_META, d), lambda i: (0, 0)), vec],
        out_shape=[jax.ShapeDtypeStruct((lp - blk, d), F32), jax.ShapeDtypeStruct((N_META, d), F32),
                   jax.ShapeDtypeStruct((1, d), F32)],
        compiler_params=_cparams(("arbitrary",), 32),
    )(h, res, gain, dy)


def _rms_bwd(h, gains, dys, res, name, tm=384):
    lp, d = h.shape
    tm = _tile(lp, tm)
    n = len(gains)

    def body(*refs):
        h_ref, res_ref = refs[0], refs[1]
        g_refs = refs[2:2 + n]
        dy_refs = refs[2 + n:2 + 2 * n]
        dh_ref = refs[2 + 2 * n]
        dg_refs = refs[3 + 2 * n:]
        i = pl.program_id(0)
        x = h_ref[...]
        rstd = lax.rsqrt(jnp.mean(x * x, axis=-1, keepdims=True) + EPS)
        xhat = x * rstd
        dh = res_ref[...]
        for g_ref, dy_ref, dg_ref in zip(g_refs, dy_refs, dg_refs):
            dy = dy_ref[...]
            gdy = dy * g_ref[...]
            dh = dh + rstd * (gdy - xhat * jnp.mean(gdy * xhat, axis=-1, keepdims=True))
            part = jnp.sum(dy * xhat, axis=0, keepdims=True)

            @pl.when(i == 0)
            def _():
                dg_ref[...] = part

            @pl.when(i > 0)
            def _():
                dg_ref[...] += part

        dh_ref[...] = dh

    row = pl.BlockSpec((tm, d), lambda i: (i, 0))
    vec = pl.BlockSpec((1, d), lambda i: (0, 0))
    outs = pl.pallas_call(
        body,
        name=name,
        grid=(lp // tm,),
        in_specs=[row, row] + [vec] * n + [row] * n,
        out_specs=[row] + [vec] * n,
        out_shape=[jax.ShapeDtypeStruct((lp, d), F32)] + [jax.ShapeDtypeStruct((1, d), F32)] * n,
        compiler_params=_cparams(("arbitrary",), 56),
    )(h, res, *gains, *dys)
    return outs[0], list(outs[1:])


def _lb_fwd(gamma):
    def body(g_ref, lb_ref):
        g = g_ref[...]
        e = jnp.exp(g - jnp.max(g, axis=0, keepdims=True))
        lb_ref[...] = (e / jnp.sum(e, axis=0, keepdims=True))[0:1, :]

    return pl.pallas_call(body, name="lb_fwd", out_shape=jax.ShapeDtypeStruct((1, gamma.shape[1]), F32))(gamma)


def _lb_bwd(gamma, dlb):
    def body(g_ref, dlb_ref, dg_ref):
        g = g_ref[...]
        e = jnp.exp(g - jnp.max(g, axis=0, keepdims=True))
        s = e / jnp.sum(e, axis=0, keepdims=True)
        rows = lax.broadcasted_iota(jnp.int32, g.shape, 0)
        ds = jnp.where(rows == 0, dlb_ref[...], 0.0)
        dg_ref[...] = s * (ds - jnp.sum(s * ds, axis=0, keepdims=True))

    return pl.pallas_call(body, name="lb_bwd", out_shape=jax.ShapeDtypeStruct(gamma.shape, F32))(gamma, dlb)


def _tri(n, lower):
    r = lax.broadcasted_iota(jnp.int32, (n, n), 0)
    c = lax.broadcasted_iota(jnp.int32, (n, n), 1)
    return jnp.where((r >= c) if lower else (r <= c), 1.0, 0.0).astype(F32)


def _group(nc, most):
    return max(u for u in range(1, most + 1) if nc % u == 0)


def _running_sum(tri, x):
    hi = x.astype(MXU)
    rest = x - hi.astype(F32)
    mid = rest.astype(MXU)
    lo = (rest - mid.astype(F32)).astype(MXU)
    return _dot(tri, hi) + _dot(tri, mid) + _dot(tri, lo)


def _causal(n):
    r = lax.broadcasted_iota(jnp.int32, (n, n), 0)
    c = lax.broadcasted_iota(jnp.int32, (n, n), 1)
    return r >= c


def _chunk_gates(u_ref, lb, c):
    sl = pl.ds(pl.multiple_of(c * CHUNK, CHUNK), CHUNK)
    valid = (c * CHUNK + lax.broadcasted_iota(jnp.int32, (CHUNK, HEAD), 0)) >= ROW_PAD
    uq = u_ref[0, sl, :]
    uf = u_ref[1, sl, :]
    sq = _sigmoid(uq)
    sf = _sigmoid(uf)
    fg = lb + (1.0 - lb) * sf
    return dict(sl=sl, valid=valid, uq=uq, sq=sq, sf=sf, fg=fg, q=jnp.where(valid, uq * sq, 0.0),
                logf=jnp.where(valid, jnp.log(fg), 0.0), k=jnp.where(valid, 1.0 - fg, 0.0),
                v=jnp.where(valid, u_ref[2, sl, :], 0.0))


def _chunk_decays(x, b):
    b_last = b[CHUNK - 1:CHUNK, :]
    b_mid = b[CHUNK // 2 - 1:CHUNK // 2, :]
    e_qi = jnp.exp(b - b_mid)
    e_ki = jnp.exp(b_mid - b)
    e_kd = jnp.exp(b_last - b)
    e_qe = jnp.exp(b)
    q, k = x["q"], x["k"]
    return dict(x, e_qi=e_qi, e_ki=e_ki, e_kd=e_kd, e_qe=e_qe, qi=q * e_qi, ki=k * e_ki, kd=k * e_kd, qe=q * e_qe,
                decay=jnp.exp(b_last))


def _chunks(u_ref, lb, tri_lower, cs):
    gates = [_chunk_gates(u_ref, lb, c) for c in cs]
    sums = [_running_sum(tri_lower, x["logf"]) for x in gates]
    return [_chunk_decays(x, b) for x, b in zip(gates, sums)]


def _hgrn2_fwd(u4, lb, g_out, name="hgrn2_fwd", sides=()):
    _, lp, d = u4.shape
    nh, nc = d // HEAD, lp // CHUNK
    per = _group(nc, 22)

    def body(u_ref, lb_ref, g_ref, o_ref, og_ref):
        lb_v = lb_ref[...]
        g = g_ref[...]

        tri_lower = _tri(CHUNK, True).astype(MXU)
        causal = _causal(CHUNK)

        def step(i, st):
            xs = _chunks(u_ref, lb_v, tri_lower, [i * per + u for u in range(per)])
            scores = [_dot(x["qi"], x["ki"], NT) for x in xs]
            updates = [_dot(x["v"], x["kd"], TN) for x in xs]
            states = []
            for x, upd in zip(xs, updates):
                states.append(st)
                st = x["decay"] * st + upd
            outs = [_dot(jnp.where(causal, a, 0.0), x["v"]) + _dot(x["qe"], s, NT)
                    for x, a, s in zip(xs, scores, states)]
            for x, o in zip(xs, outs):
                o_ref[x["sl"], :] = o
                on = o * lax.rsqrt(jnp.mean(o * o, axis=-1, keepdims=True) + EPS) * g
                z = u_ref[3, x["sl"], :]
                og_ref[x["sl"], :] = (on * (z * _sigmoid(z))).astype(og_ref.dtype)
            return st

        lax.fori_loop(0, nc // per, step, jnp.zeros((HEAD, HEAD), F32))

    slab = pl.BlockSpec((lp, HEAD), lambda h: (0, h))
    vec = pl.BlockSpec((1, HEAD), lambda h: (0, h))
    outs, side_results = _pallas(
        body,
        name=name,
        args=(u4, lb, g_out),
        grid=(nh,),
        in_specs=[pl.BlockSpec((4, lp, HEAD), lambda h: (0, 0, h)), vec, vec],
        out_specs=[slab, slab],
        out_shape=[jax.ShapeDtypeStruct((lp, d), F32), jax.ShapeDtypeStruct((lp, d), MXU)],
        semantics=("parallel",),
        vmem_mib=48,
        sides=sides,
    )
    return (*outs, side_results) if sides else tuple(outs)


def _hgrn2_bwd(u4, o, d_og, lb, g_out, name="hgrn2_bwd", sides=()):
    _, lp, d = u4.shape
    nh, nc = d // HEAD, lp // CHUNK
    per_f, per = _group(nc, 6), _group(nc, 11)

    def body(u_ref, o_ref, dog_ref, lb_ref, g_ref, du_ref, dlb_ref, dg_ref, st_ref, do_ref):
        lb_v = lb_ref[...]
        g = g_ref[...]

        tri_lower = _tri(CHUNK, True).astype(MXU)
        tri_upper = _tri(CHUNK, False).astype(MXU)
        causal = _causal(CHUNK)

        def fwd_step(i, carry):
            st, dg_acc = carry
            cs = [i * per_f + u for u in range(per_f)]
            xs = _chunks(u_ref, lb_v, tri_lower, cs)
            updates = [_dot(x["v"], x["kd"], TN) for x in xs]
            for c, x, upd in zip(cs, xs, updates):
                st_ref[c] = st
                st = x["decay"] * st + upd
            for x in xs:
                sl = x["sl"]
                ov = o_ref[sl, :]
                rstd = lax.rsqrt(jnp.mean(ov * ov, axis=-1, keepdims=True) + EPS)
                on = ov * rstd
                z = u_ref[3, sl, :]
                sz = _sigmoid(z)
                dog = dog_ref[sl, :]
                dy = dog * (z * sz)
                dz = dog * (on * g) * (sz * (1.0 + z * (1.0 - sz)))
                du_ref[3, sl, :] = dz.astype(du_ref.dtype)
                gdy = dy * g
                do = rstd * (gdy - on * jnp.mean(gdy * on, axis=-1, keepdims=True))
                do_ref[sl, :] = jnp.where(x["valid"], do, 0.0)
                dg_acc = dg_acc + jnp.sum(dy * on, axis=0, keepdims=True)
            return st, dg_acc

        _, dg_tot = lax.fori_loop(0, nc // per_f, fwd_step, (jnp.zeros((HEAD, HEAD), F32), jnp.zeros((1, HEAD), F32)))
        dg_ref[...] = dg_tot

        def bwd_step(i, carry):
            gt, dlb_acc = carry
            cs = [nc - 1 - (i * per + u) for u in range(per)]
            xs = _chunks(u_ref, lb_v, tri_lower, cs)
            dos = [do_ref[x["sl"], :] for x in xs]
            sts = [st_ref[c] for c in cs]
            scores = [jnp.where(causal, _dot(x["qi"], x["ki"], NT), 0.0) for x in xs]
            d_scores = [jnp.where(causal, _dot(do, x["v"], NT), 0.0) for x, do in zip(xs, dos)]
            d_qes = [_dot(do, st) for do, st in zip(dos, sts)]
            g_updates = [_dot(do, x["qe"], TN) for x, do in zip(xs, dos)]
            gts = []
            for x, upd in zip(xs, g_updates):
                gts.append(gt)
                gt = x["decay"] * gt + upd
            d_kds = [_dot(x["v"], g_) for x, g_ in zip(xs, gts)]
            dvs = [_dot(x["kd"], g_, NT) + _dot(a, do, TN) for x, g_, a, do in zip(xs, gts, scores, dos)]
            d_qis = [_dot(d_a, x["ki"]) for x, d_a in zip(xs, d_scores)]
            d_kis = [_dot(d_a, x["qi"], TN) for x, d_a in zip(xs, d_scores)]
            rows = lax.broadcasted_iota(jnp.int32, (CHUNK, HEAD), 0)
            dbs = []
            for x, g_, st, d_qi, d_ki, d_qe, d_kd in zip(xs, gts, sts, d_qis, d_kis, d_qes, d_kds):
                t_qi, t_ki, t_qe, t_kd = d_qi * x["qi"], d_ki * x["ki"], d_qe * x["qe"], d_kd * x["kd"]
                d_decay = jnp.sum(g_ * st, axis=0, keepdims=True)
                d_mid = jnp.sum(t_ki - t_qi, axis=0, keepdims=True)
                d_last = jnp.sum(t_kd, axis=0, keepdims=True) + d_decay * x["decay"]
                dbs.append(t_qi - t_ki + t_qe - t_kd + jnp.where(rows == CHUNK // 2 - 1, d_mid, 0.0)
                           + jnp.where(rows == CHUNK - 1, d_last, 0.0))
            dlogfs = [_running_sum(tri_upper, db) for db in dbs]
            for x, dlogf, dv, d_qi, d_ki, d_qe, d_kd in zip(xs, dlogfs, dvs, d_qis, d_kis, d_qes, d_kds):
                sl = x["sl"]
                dq = d_qi * x["e_qi"] + d_qe * x["e_qe"]
                dk = d_ki * x["e_ki"] + d_kd * x["e_kd"]
                valid, sq, sf, uq = x["valid"], x["sq"], x["sf"], x["uq"]
                dfg = jnp.where(valid, dlogf / x["fg"] - dk, 0.0)
                du_ref[0, sl, :] = jnp.where(valid, dq * (sq * (1.0 + uq * (1.0 - sq))), 0.0).astype(du_ref.dtype)
                du_ref[1, sl, :] = (dfg * (1.0 - lb_v) * (sf * (1.0 - sf))).astype(du_ref.dtype)
                du_ref[2, sl, :] = jnp.where(valid, dv, 0.0).astype(du_ref.dtype)
                dlb_acc = dlb_acc + jnp.sum(dfg * (1.0 - sf), axis=0, keepdims=True)
            return gt, dlb_acc

        _, dlb_tot = lax.fori_loop(0, nc // per, bwd_step, (jnp.zeros((HEAD, HEAD), F32), jnp.zeros((1, HEAD), F32)))
        dlb_ref[...] = dlb_tot

    slab = pl.BlockSpec((lp, HEAD), lambda h: (0, h))
    vec = pl.BlockSpec((1, HEAD), lambda h: (0, h))
    quad = pl.BlockSpec((4, lp, HEAD), lambda h: (0, 0, h))
    outs, side_results = _pallas(
        body,
        name=name,
        args=(u4, o, d_og, lb, g_out),
        grid=(nh,),
        in_specs=[quad, slab, slab, vec, vec],
        out_specs=[quad, vec, vec],
        out_shape=[jax.ShapeDtypeStruct((4, lp, d), MXU), jax.ShapeDtypeStruct((1, d), F32),
                   jax.ShapeDtypeStruct((1, d), F32)],
        scratch_shapes=[pltpu.VMEM((nc, HEAD, HEAD), F32), pltpu.VMEM((lp, HEAD), F32)],
        semantics=("parallel",),
        vmem_mib=58,
        sides=sides,
    )
    return (*outs, side_results) if sides else tuple(outs)


WIDE = 2 * HEAD
INV_SCALE = HEAD ** 0.5


def _split3(x):
    hi = x.astype(MXU).astype(F32)
    rest = x - hi
    mid = rest.astype(MXU).astype(F32)
    return hi, mid, (rest - mid).astype(MXU).astype(F32)


def _extra_cols(rows, first, second):
    lane = lax.broadcasted_iota(jnp.int32, (rows, HEAD), 1)
    out = jnp.where(lane < 6, 1.0, 0.0).astype(F32)
    for base, terms in ((0, first), (3, second)):
        if terms is not None:
            for j, term in enumerate(terms):
                out = jnp.where(lane == base + j, term, out)
    return out


def _head_col(a, h):
    lane = lax.broadcasted_iota(jnp.int32, a.shape, 1)
    return jnp.sum(jnp.where(lane == h, a, 0.0), axis=-1, keepdims=True)


def _attn_operands(ub, ukv, g_q, g_k, f_cum, name="attn_operands", tm=384):
    _, lp, d = ub.shape
    tm = _tile(lp, tm)
    nh = d // HEAD

    def body(q_ref, k_ref, v_ref, gq_ref, gk_ref, f_ref, qa_ref, ka_ref, va_ref):
        i = pl.program_id(0)
        f = f_ref[...]
        is_pad = (i * tm + lax.broadcasted_iota(jnp.int32, (tm, 1), 0)) < ROW_PAD
        ones_only = _extra_cols(tm, None, (0.0, 0.0, 0.0)).astype(MXU)
        for h in range(nh):
            hs = slice(h * HEAD, (h + 1) * HEAD)
            lo, hi = h * WIDE, h * WIDE + HEAD
            f_h = _head_col(f, h)
            for x_ref, g_ref, o_ref in ((q_ref, gq_ref, qa_ref), (k_ref, gk_ref, ka_ref)):
                x = x_ref[:, hs]
                y = x * lax.rsqrt(jnp.mean(x * x, axis=-1, keepdims=True) + EPS)
                o_ref[:, lo:hi] = (y * g_ref[:, hs]).astype(o_ref.dtype)
            qa_ref[:, hi:hi + HEAD] = _extra_cols(tm, _split3(f_h * INV_SCALE), None).astype(MXU)
            f_key = jnp.where(is_pad, -MASK_VALUE, f_h)
            ka_ref[:, hi:hi + HEAD] = _extra_cols(tm, None, _split3(-f_key * INV_SCALE)).astype(MXU)
            va_ref[:, lo:hi] = v_ref[:, hs].astype(MXU)
            va_ref[:, hi:hi + HEAD] = ones_only

    wide = pl.BlockSpec((tm, nh * WIDE), lambda i: (i, 0))
    vec = pl.BlockSpec((1, d), lambda i: (0, 0))
    return pl.pallas_call(
        body,
        name=name,
        grid=(lp // tm,),
        in_specs=[pl.BlockSpec((None, tm, d), lambda i: (0, i, 0)), pl.BlockSpec((None, tm, d), lambda i: (0, i, 0)),
                  pl.BlockSpec((None, tm, d), lambda i: (1, i, 0)), vec, vec, pl.BlockSpec((tm, 128), lambda i: (i, 0))],
        out_specs=[wide, wide, wide],
        out_shape=[jax.ShapeDtypeStruct((lp, nh * WIDE), MXU)] * 3,
        compiler_params=_cparams(("parallel",), 56),
    )(ub, ukv, ukv, g_q, g_k, f_cum)


def _head_rms_bwd_tile(x, g, dy):
    rstd = lax.rsqrt(jnp.mean(x * x, axis=-1, keepdims=True) + EPS)
    xhat = x * rstd
    gdy = dy * g
    dx = rstd * (gdy - xhat * jnp.mean(gdy * xhat, axis=-1, keepdims=True))
    return dx, jnp.sum(dy * xhat, axis=0, keepdims=True)


def _fgate_fwd(ufl, b_f):
    lp = ufl.shape[0]
    nb = lp // 128

    def body(u_ref, b_ref, f_ref):
        def step(i, carry):
            sl = pl.ds(pl.multiple_of(i * 128, 128), 128)
            valid = (i * 128 + lax.broadcasted_iota(jnp.int32, (128, 128), 0)) >= ROW_PAD
            x = u_ref[sl, :] + b_ref[...]
            logf = jnp.where(valid, jnp.minimum(x, 0.0) - jnp.log(1.0 + jnp.exp(-jnp.abs(x))), 0.0)
            f = _dot_exact(_tri(128, True), logf) + carry
            f_ref[sl, :] = f
            return f[127:128, :]

        lax.fori_loop(0, nb, step, jnp.zeros((1, 128), F32))

    return pl.pallas_call(body, name="fgate_fwd", out_shape=jax.ShapeDtypeStruct((lp, 128), F32))(ufl, b_f)


def _fgate_bwd(ufl, b_f, d_f):
    lp = ufl.shape[0]
    nb = lp // 128

    def body(u_ref, b_ref, df_ref, du_ref, db_ref):
        def step(j, carry):
            later, db_acc = carry
            i = nb - 1 - j
            sl = pl.ds(pl.multiple_of(i * 128, 128), 128)
            valid = (i * 128 + lax.broadcasted_iota(jnp.int32, (128, 128), 0)) >= ROW_PAD
            x = u_ref[sl, :] + b_ref[...]
            df = df_ref[sl, :]
            dlogf = _dot_exact(_tri(128, False), df) + later
            dx = jnp.where(valid, dlogf * _sigmoid(-x), 0.0)
            du_ref[sl, :] = dx.astype(du_ref.dtype)
            return later + jnp.sum(df, axis=0, keepdims=True), db_acc + jnp.sum(dx, axis=0, keepdims=True)

        _, db_tot = lax.fori_loop(0, nb, step, (jnp.zeros((1, 128), F32), jnp.zeros((1, 128), F32)))
        db_ref[...] = db_tot

    return pl.pallas_call(
        body, name="fgate_bwd",
        out_shape=[jax.ShapeDtypeStruct((lp, 128), MXU), jax.ShapeDtypeStruct((1, 128), F32)],
    )(ufl, b_f, d_f)


STRIP = 32
PAIR = 2


def _strip_causal(r, t):
    row = r + lax.broadcasted_iota(jnp.int32, (STRIP, t), 0)
    col = lax.broadcasted_iota(jnp.int32, (STRIP, t), 1)
    return col <= row


def _fox_fwd(qa, ka, va, ub, name="fox_fwd", t=384):
    lp = qa.shape[0]
    d = ub.shape[2]
    t = _tile(lp, t)
    nh, nq = d // HEAD, lp // t
    scale = HEAD ** -0.5

    assert nh % PAIR == 0
    heads = range(PAIR)

    def body(q_ref, k_ref, v_ref, z_ref, o_ref, olo_ref, og_ref, qb_ref,
             s_ref, p_ref, m_ref, a_ref, l_ref, acc_ref):
        qb = pl.program_id(1)
        m_ref[...] = jnp.full((PAIR, t, 1), MASK_VALUE, F32)
        l_ref[...] = jnp.zeros((PAIR, t, 128), F32)
        acc_ref[...] = jnp.zeros((PAIR, 2 * t, HEAD), F32)
        p_ref[1] = jnp.zeros((PAIR, 2 * t, t), MXU)
        a_ref[1] = jnp.ones((PAIR, t, 1), F32)

        def scores(kb, buf):
            ks = pl.ds(pl.multiple_of(kb * t, t), t)
            for j in heads:
                ws = slice(j * WIDE, (j + 1) * WIDE)
                s_ref[buf, j] = _dot(q_ref[:, ws], k_ref[ks, ws], NT)

        def weighted_sum(kb, buf):
            ks = pl.ds(pl.multiple_of(kb * t, t), t)
            for j in heads:
                pv = _dot(p_ref[buf, j], v_ref[ks, j * WIDE:j * WIDE + HEAD])
                alpha = a_ref[buf, j]
                acc_ref[j, 0:t, :] = alpha * acc_ref[j, 0:t, :] + pv[0:t]
                acc_ref[j, t:2 * t, :] = alpha * acc_ref[j, t:2 * t, :] + pv[t:2 * t]

        def softmax_update(buf, diagonal):
            for j in heads:
                for r in range(0, t, STRIP):
                    rs = slice(r, r + STRIP)
                    x = s_ref[buf, j, rs, :] * scale
                    if diagonal:
                        x = jnp.where(_strip_causal(r, t), x, MASK_VALUE)
                    m_old = m_ref[j, rs, :]
                    m_new = jnp.maximum(m_old, jnp.max(x, axis=-1, keepdims=True))
                    alpha = jnp.exp(m_old - m_new)
                    p = jnp.exp(x - m_new)
                    m_ref[j, rs, :] = m_new
                    a_ref[buf, j, rs, :] = alpha
                    l_ref[j, rs, :] = alpha * l_ref[j, rs, :] + sum(p[:, c:c + 128] for c in range(0, t, 128))
                    p_hi = p.astype(MXU)
                    p_ref[buf, j, rs, :] = p_hi
                    p_ref[buf, j, t + r:t + r + STRIP, :] = (p - p_hi.astype(F32)).astype(MXU)

        def off_diagonal(kb, cur):
            weighted_sum(jnp.maximum(kb - 1, 0), 1 - cur)
            scores(kb + 1, 1 - cur)
            softmax_update(cur, False)

        def diagonal(cur):
            weighted_sum(jnp.maximum(qb - 1, 0), 1 - cur)
            softmax_update(cur, True)
            weighted_sum(qb, cur)

        def two_blocks(i, carry):
            off_diagonal(2 * i, 0)
            off_diagonal(2 * i + 1, 1)
            return carry

        scores(0, 0)
        lax.fori_loop(0, qb // 2, two_blocks, 0)

        @pl.when(lax.rem(qb, 2) == 0)
        def _():
            diagonal(0)

        @pl.when(lax.rem(qb, 2) == 1)
        def _():
            off_diagonal(qb - 1, 0)
            diagonal(1)

        is_pad = (qb * t + lax.broadcasted_iota(jnp.int32, (t, 1), 0)) < ROW_PAD
        for j in heads:
            hs = slice(j * HEAD, (j + 1) * HEAD)
            l = jnp.sum(l_ref[j], axis=-1, keepdims=True)
            o = acc_ref[j, 0:t, :] / l
            o_ref[:, hs] = o
            olo_ref[:, hs] = acc_ref[j, t:2 * t, :] / l
            z = z_ref[:, hs]
            og_ref[:, hs] = (o * (z * _sigmoid(z))).astype(og_ref.dtype)
            extra = q_ref[:, j * WIDE + HEAD:(j + 1) * WIDE].astype(F32)
            f_scaled = extra[:, 0:1] + extra[:, 1:2] + extra[:, 2:3]
            log_term = jnp.where(is_pad, MASK_VALUE * INV_SCALE, f_scaled - (m_ref[j] + jnp.log(l)) * INV_SCALE)
            qb_ref[:, j * WIDE:j * WIDE + HEAD] = q_ref[:, j * WIDE:j * WIDE + HEAD]
            qb_ref[:, j * WIDE + HEAD:(j + 1) * WIDE] = _extra_cols(t, _split3(log_term), None).astype(qb_ref.dtype)

    scratch = [pltpu.VMEM((2, PAIR, t, t), F32), pltpu.VMEM((2, PAIR, 2 * t, t), MXU), pltpu.VMEM((PAIR, t, 1), F32),
               pltpu.VMEM((2, PAIR, t, 1), F32), pltpu.VMEM((PAIR, t, 128), F32), pltpu.VMEM((PAIR, 2 * t, HEAD), F32)]
    tile = pl.BlockSpec((t, PAIR * HEAD), lambda g, i: (i, g))
    wide_tile = pl.BlockSpec((t, PAIR * WIDE), lambda g, i: (i, g))
    wide_all = pl.BlockSpec((lp, PAIR * WIDE), lambda g, i: (0, g))
    return pl.pallas_call(
        body,
        name=name,
        grid=(nh // PAIR, nq),
        in_specs=[wide_tile, wide_all, wide_all, pl.BlockSpec((None, t, PAIR * HEAD), lambda g, i: (1, i, g))],
        out_specs=[tile, tile, tile, wide_tile],
        out_shape=[jax.ShapeDtypeStruct((lp, d), F32), jax.ShapeDtypeStruct((lp, d), F32),
                   jax.ShapeDtypeStruct((lp, d), MXU), jax.ShapeDtypeStruct((lp, nh * WIDE), MXU)],
        scratch_shapes=scratch,
        compiler_params=_cparams(("parallel", "arbitrary"), 48),
    )(qa, ka, va, ub)


def _fox_bwd(qb, ka, va, doa, ukv, g_k, ub, g_q, name="fox_bwd", t=384, sides=()):
    lp = qb.shape[0]
    d = ukv.shape[2]
    t = _tile(lp, t)
    nh, nk = d // HEAD, lp // t
    scale = HEAD ** -0.5

    assert nh % PAIR == 0
    heads = range(PAIR)

    def body(q_ref, do_ref, k_ref, v_ref, kraw_ref, gk_ref, qraw_ref, gq_ref, dqraw_ref, dgq_ref, dukv_ref, dgk_ref,
             dfk_ref, s_ref, dp_ref, p_ref, ds_ref, col_ref, dk_ref, dv_ref, dq_ref):
        kb = pl.program_id(1)

        @pl.when(kb == 0)
        def _():
            dq_ref[...] = jnp.zeros_like(dq_ref)
            dgk_ref[...] = jnp.zeros_like(dgk_ref)

        dk_ref[...] = jnp.zeros_like(dk_ref)
        dv_ref[...] = jnp.zeros_like(dv_ref)
        col_ref[...] = jnp.zeros_like(col_ref)

        def step(qb, diagonal):
            qs = pl.ds(pl.multiple_of(qb * t, t), t)
            for j in heads:
                ws = slice(j * WIDE, (j + 1) * WIDE)
                s_ref[j] = _dot(q_ref[qs, ws], k_ref[:, ws], NT)
                dp_ref[j] = _dot(do_ref[qs, ws], v_ref[:, ws], NT)
            for j in heads:
                for r in range(0, t, STRIP):
                    rs = slice(r, r + STRIP)
                    x = s_ref[j, rs, :] * scale
                    if diagonal:
                        x = jnp.where(_strip_causal(r, t), x, MASK_VALUE)
                    p = jnp.exp(x)
                    ds = p * dp_ref[j, rs, :]
                    p_ref[j, rs, :] = p.astype(MXU)
                    ds_ref[j, rs, :] = (ds * scale).astype(MXU)
                    col_ref[j] += ds
            for j in heads:
                hs = slice(j * HEAD, (j + 1) * HEAD)
                narrow = slice(j * WIDE, j * WIDE + HEAD)
                dsb = ds_ref[j]
                dv_ref[:, hs] += _dot(p_ref[j], do_ref[qs, narrow], TN)
                dq_ref[qs, hs] += _dot(dsb, k_ref[:, narrow])
                dk_ref[:, hs] += _dot(dsb, q_ref[qs, narrow], TN)

        def off_diagonal(qb, carry):
            step(qb, False)
            return carry

        step(kb, True)
        lax.fori_loop(kb + 1, nk, off_diagonal, 0)
        for j in heads:
            hs = slice(j * HEAD, (j + 1) * HEAD)
            dfk_ref[j] = -jnp.sum(col_ref[j], axis=0, keepdims=True)
            dx, dg = _head_rms_bwd_tile(kraw_ref[:, hs], gk_ref[:, hs], dk_ref[:, hs])
            dukv_ref[0, :, hs] = dx.astype(dukv_ref.dtype)
            dukv_ref[1, :, hs] = dv_ref[:, hs].astype(dukv_ref.dtype)
            dgk_ref[:, hs] += dg

        @pl.when(kb == nk - 1)
        def _():
            dgq_ref[...] = jnp.zeros_like(dgq_ref)

            def rows(c, carry):
                rs = pl.ds(pl.multiple_of(c * t, t), t)
                for j in heads:
                    hs = slice(j * HEAD, (j + 1) * HEAD)
                    dx, dg = _head_rms_bwd_tile(qraw_ref[rs, hs], gq_ref[:, hs], dq_ref[rs, hs])
                    dqraw_ref[rs, hs] = dx.astype(dqraw_ref.dtype)
                    dgq_ref[:, hs] += dg
                return carry

            lax.fori_loop(0, nk, rows, 0)

    scratch = [pltpu.VMEM((PAIR, t, t), F32), pltpu.VMEM((PAIR, t, t), F32), pltpu.VMEM((PAIR, t, t), MXU),
               pltpu.VMEM((PAIR, t, t), MXU), pltpu.VMEM((PAIR, STRIP, t), F32),
               pltpu.VMEM((t, PAIR * HEAD), F32), pltpu.VMEM((t, PAIR * HEAD), F32), pltpu.VMEM((lp, PAIR * HEAD), F32)]
    whole = pl.BlockSpec((lp, PAIR * HEAD), lambda g, j: (0, g))
    wide_all = pl.BlockSpec((lp, PAIR * WIDE), lambda g, j: (0, g))
    wide_tile = pl.BlockSpec((t, PAIR * WIDE), lambda g, j: (j, g))
    vec = pl.BlockSpec((1, PAIR * HEAD), lambda g, j: (0, g))
    outs, side_results = _pallas(
        body,
        name=name,
        args=(qb, doa, ka, va, ukv, g_k, ub, g_q),
        grid=(nh // PAIR, nk),
        in_specs=[wide_all, wide_all, wide_tile, wide_tile,
                  pl.BlockSpec((None, t, PAIR * HEAD), lambda g, j: (0, j, g)), vec,
                  pl.BlockSpec((None, lp, PAIR * HEAD), lambda g, j: (0, 0, g)), vec],
        out_specs=[whole, vec, pl.BlockSpec((2, t, PAIR * HEAD), lambda g, j: (0, j, g)), vec,
                   pl.BlockSpec((PAIR, 1, t), lambda g, j: (g, 0, j))],
        out_shape=[jax.ShapeDtypeStruct((lp, d), MXU), jax.ShapeDtypeStruct((1, d), F32),
                   jax.ShapeDtypeStruct((2, lp, d), MXU), jax.ShapeDtypeStruct((1, d), F32),
                   jax.ShapeDtypeStruct((nh, 1, lp), F32)],
        scratch_shapes=scratch,
        semantics=("parallel", "arbitrary"),
        vmem_mib=48,
        sides=sides,
    )
    return (*outs, side_results) if sides else tuple(outs)


def _fox_do(d_og, o, o_lo, ub, name="fox_do", tm=384):
    lp, d = o.shape
    tm = _tile(lp, tm)
    nh = d // HEAD

    def body(dog_ref, o_ref, olo_ref, z_ref, doa_ref):
        for h in range(nh):
            hs = slice(h * HEAD, (h + 1) * HEAD)
            z = z_ref[:, hs]
            do = (dog_ref[:, hs] * (z * _sigmoid(z))).astype(doa_ref.dtype)
            delta = jnp.sum(do.astype(F32) * (o_ref[:, hs] + olo_ref[:, hs]), axis=-1, keepdims=True)
            doa_ref[:, h * WIDE:h * WIDE + HEAD] = do
            doa_ref[:, h * WIDE + HEAD:(h + 1) * WIDE] = _extra_cols(tm, _split3(-delta), (0.0, 0.0, 0.0)).astype(
                doa_ref.dtype)

    row = pl.BlockSpec((tm, d), lambda i: (i, 0))
    return pl.pallas_call(
        body,
        name=name,
        grid=(lp // tm,),
        in_specs=[row, row, row, pl.BlockSpec((None, tm, d), lambda i: (1, i, 0))],
        out_specs=pl.BlockSpec((tm, nh * WIDE), lambda i: (i, 0)),
        out_shape=jax.ShapeDtypeStruct((lp, nh * WIDE), MXU),
        compiler_params=_cparams(("parallel",), 56),
    )(d_og, o, o_lo, ub)


def _fox_gate_bwd(ub, d_og, o, dq, name="fox_gate_bwd", tm=384):
    _, lp, d = ub.shape
    tm = _tile(lp, tm)

    def body(z_ref, dog_ref, o_ref, dq_ref, dub_ref):
        z = z_ref[...]
        sz = _sigmoid(z)
        dub_ref[0] = dq_ref[...]
        dub_ref[1] = (dog_ref[...] * o_ref[...] * (sz * (1.0 + z * (1.0 - sz)))).astype(dub_ref.dtype)

    row = pl.BlockSpec((tm, d), lambda i: (i, 0))
    return pl.pallas_call(
        body,
        name=name,
        grid=(lp // tm,),
        in_specs=[pl.BlockSpec((None, tm, d), lambda i: (1, i, 0)), row, row, row],
        out_specs=pl.BlockSpec((2, tm, d), lambda i: (0, i, 0)),
        out_shape=jax.ShapeDtypeStruct((2, lp, d), MXU),
        compiler_params=_cparams(("parallel",), 48),
    )(ub, d_og, o, dq)


def _loss(h, target, name="loss_head"):
    lp, d = h.shape
    nb = lp // 128

    def body(h_ref, t_ref, loss_ref, dh_ref, acc_ref):
        i = pl.program_id(0)

        @pl.when(i == 0)
        def _():
            acc_ref[...] = jnp.zeros_like(acc_ref)
            dh_ref[...] = jnp.zeros_like(dh_ref)

        @pl.when(i > 0)
        def _():
            err = h_ref[...] - t_ref[...]
            dh_ref[...] = err * (1.0 / d)
            acc_ref[...] += jnp.sum(jnp.sum(err * err, axis=-1, keepdims=True) * (1.0 / d), axis=0, keepdims=True)

        @pl.when(i == nb - 1)
        def _():
            loss_ref[...] = 0.5 * acc_ref[...]

    return pl.pallas_call(
        body,
        name=name,
        grid=(nb,),
        in_specs=[pl.BlockSpec((128, d), lambda i: (i, 0)),
                  pl.BlockSpec((128, d), lambda i: (jnp.maximum(i - 1, 0), 0))],
        out_specs=[pl.BlockSpec((1, 1), lambda i: (0, 0)), pl.BlockSpec((128, d), lambda i: (i, 0))],
        out_shape=[jax.ShapeDtypeStruct((1, 1), F32), jax.ShapeDtypeStruct((lp, d), F32)],
        scratch_shapes=[pltpu.VMEM((1, 1), F32)],
        compiler_params=_cparams(("arbitrary",), 32),
    )(h, target)


def _adam_math(w, g, m, v):
    m = ADAM_B1 * m + (1.0 - ADAM_B1) * g
    v = ADAM_B2 * v + (1.0 - ADAM_B2) * (g * g)
    m_hat = m / (1.0 - ADAM_B1 ** ADAM_STEP)
    v_hat = v / (1.0 - ADAM_B2 ** ADAM_STEP)
    delta = -ADAM_LR * (m_hat / (jnp.sqrt(v_hat) + ADAM_EPS) + ADAM_WD * w)
    return delta, m, v


def _adamw(parts, w, m, v, name, tm=512):
    n, r, c = parts.shape
    if r % tm == 0:
        grid, blk, at = r // tm, (tm, c), lambda i: (i, 0)
    else:
        tc = _tile(c, tm)
        grid, blk, at = c // tc, (r, tc), lambda i: (0, i)

    def body(p_ref, w_ref, m_ref, v_ref, g_ref, d_ref, nm_ref, nv_ref):
        g = p_ref[0].astype(F32)
        for j in range(1, n):
            g = g + p_ref[j].astype(F32)
        g_ref[...] = g
        d_ref[...], nm_ref[...], nv_ref[...] = _adam_math(w_ref[...], g, m_ref[...], v_ref[...])

    row = pl.BlockSpec(blk, at)
    return pl.pallas_call(
        body,
        name=name,
        grid=(grid,),
        in_specs=[pl.BlockSpec((n,) + blk, lambda i: (0,) + at(i)), row, row, row],
        out_specs=[row] * 4,
        out_shape=[jax.ShapeDtypeStruct((r, c), F32)] * 4,
        compiler_params=_cparams(("parallel",), 48),
    )(parts, w, m, v)


def _adamw_small(gs, ws, ms, vs, name="adamw_small"):
    n = len(ws)
    flat = [jnp.reshape(a, (-1, a.shape[-1])) for group in (gs, ws, ms, vs) for a in group]

    def body(*refs):
        g_refs, w_refs, m_refs, v_refs = (refs[k * n:(k + 1) * n] for k in range(4))
        outs = refs[4 * n:]
        for i in range(n):
            d_new, m_new, v_new = _adam_math(w_refs[i][...], g_refs[i][...], m_refs[i][...], v_refs[i][...])
            outs[i][...], outs[n + i][...], outs[2 * n + i][...] = d_new, m_new, v_new

    res = pl.pallas_call(
        body, name=name, out_shape=[jax.ShapeDtypeStruct(a.shape, F32) for a in flat[n:2 * n]] * 3)(*flat)
    return [[jnp.reshape(r, w.shape) for r, w in zip(res[k * n:(k + 1) * n], ws)] for k in range(3)]


def _place():
    x, y, c = lax.axis_index("x"), lax.axis_index("y"), lax.axis_index("c")
    return x, y, c


def _other_chips(x, y):
    return [(1 - x, y), (x, 1 - y), (1 - x, 1 - y)]


def _any_spec():
    return pl.BlockSpec(memory_space=pl.ANY)


class _Side:
    def __init__(self, ins, outs, sems, start, finish, aliases=None):
        self.ins, self.outs, self.sems = list(ins), list(outs), sems
        self.start, self.finish = start, finish
        self.aliases = dict(aliases or {})


def _pallas(body, *, name, out_shape, args=(), grid=(), in_specs=(), out_specs=(), scratch_shapes=(),
            semantics=(), vmem_mib=None, sides=(), aliases=None):
    n_in, n_out, n_scr = len(args), len(out_shape), len(scratch_shapes)
    side_ins = [a for s in sides for a in s.ins]
    side_outs = [o for s in sides for o in s.outs]
    side_sems = [pltpu.SemaphoreType.DMA((max(k, 1),)) for s in sides for k in s.sems]
    aliases, in_at, out_at = dict(aliases or {}), n_in, n_out
    for s in sides:
        aliases.update({in_at + i: out_at + o for i, o in s.aliases.items()})
        in_at, out_at = in_at + len(s.ins), out_at + len(s.outs)

    def wrapped(*refs):
        at = [0]

        def take(k):
            got = refs[at[0]:at[0] + k]
            at[0] += k
            return got

        main_in = take(n_in)
        s_in = [take(len(s.ins)) for s in sides]
        main_out = take(n_out)
        s_out = [take(len(s.outs)) for s in sides]
        main_scr = take(n_scr)
        s_sem = [take(3) for s in sides]

        def run(stage):
            for s, i_, o_, m_ in zip(sides, s_in, s_out, s_sem):
                getattr(s, stage)(i_, o_, *m_)

        first = last = None
        for k, g in enumerate(grid):
            i = pl.program_id(k)
            first = (i == 0) if first is None else jnp.logical_and(first, i == 0)
            last = (i == g - 1) if last is None else jnp.logical_and(last, i == g - 1)
        if sides:
            run("start") if first is None else pl.when(first)(lambda: run("start"))
        body(*main_in, *main_out, *main_scr)
        if sides:
            run("finish") if last is None else pl.when(last)(lambda: run("finish"))

    kw = {}
    if grid:
        kw["grid"] = grid
    if semantics or vmem_mib:
        sem = tuple("arbitrary" for _ in grid) if sides else tuple(semantics)
        kw["compiler_params"] = pltpu.CompilerParams(
            dimension_semantics=sem or None, vmem_limit_bytes=vmem_mib * MIB if vmem_mib else None)
    res = pl.pallas_call(
        wrapped,
        name=name,
        in_specs=list(in_specs) + [_any_spec()] * len(side_ins),
        out_specs=list(out_specs) + [_any_spec()] * len(side_outs),
        out_shape=list(out_shape) + side_outs,
        scratch_shapes=list(scratch_shapes) + side_sems,
        input_output_aliases=aliases,
        **kw,
    )(*args, *side_ins)
    main, rest, per_side = list(res[:n_out]), list(res[n_out:]), []
    for s in sides:
        per_side.append(rest[:len(s.outs)])
        rest = rest[len(s.outs):]
    return main, per_side


def _slot(p):
    return 4 * p[0] + 2 * p[1] + p[2]


def _sibling_side(grads):
    n = len(grads)

    def copies(ins, outs, send_sems, recv_sems):
        x, y, c = _place()
        return [pltpu.make_async_remote_copy(
            src_ref=ins[t].at[2 * chip + (1 - c)], dst_ref=outs[t].at[chip],
            send_sem=send_sems.at[4 * t + chip], recv_sem=recv_sems.at[4 * t + chip],
            device_id=(x, y, 1 - c), device_id_type=MESH) for t in range(n) for chip in range(N_CHIP)]

    def start(ins, outs, send_sems, recv_sems, local_sems):
        for cp in copies(ins, outs, send_sems, recv_sems):
            cp.start()

    def finish(ins, outs, send_sems, recv_sems, local_sems):
        for cp in copies(ins, outs, send_sems, recv_sems):
            cp.wait()

    outs = [jax.ShapeDtypeStruct((N_CHIP,) + g.shape[1:], g.dtype) for g in grads]
    return _Side(grads, outs, (4 * n, 4 * n, 0), start, finish)


def _chips_side(partials):
    n = len(partials)

    def copies(ins, outs, send_sems, recv_sems, local_sems):
        x, y, c = _place()
        my_chip = 2 * x + y
        local = [pltpu.make_async_copy(ins[t].at[my_chip], outs[t].at[my_chip], local_sems.at[t]) for t in range(n)]
        sends, recvs = [], []
        for t in range(n):
            for j, chip in enumerate(_other_chips(x, y)):
                their = 2 * chip[0] + chip[1]
                sems = dict(send_sem=send_sems.at[3 * t + j], recv_sem=recv_sems.at[3 * t + j],
                            device_id=(*chip, c), device_id_type=MESH)
                sends.append(pltpu.make_async_remote_copy(src_ref=ins[t].at[their], dst_ref=outs[t].at[my_chip], **sems))
                recvs.append(pltpu.make_async_remote_copy(src_ref=ins[t].at[my_chip], dst_ref=outs[t].at[their], **sems))
        return local, sends, recvs

    def start(ins, outs, send_sems, recv_sems, local_sems):
        local, sends, _ = copies(ins, outs, send_sems, recv_sems, local_sems)
        for cp in local + sends:
            cp.start()

    def finish(ins, outs, send_sems, recv_sems, local_sems):
        local, sends, recvs = copies(ins, outs, send_sems, recv_sems, local_sems)
        for cp in sends:
            cp.wait_send()
        for cp in recvs:
            cp.wait_recv()
        for cp in local:
            cp.wait()

    outs = [jax.ShapeDtypeStruct(p.shape, p.dtype) for p in partials]
    return _Side(partials, outs, (3 * n, 3 * n, n), start, finish)


def _gather_own_side(blocks):
    n = len(blocks)

    def copies(ins, outs, send_sems, recv_sems, local_sems):
        x, y, c = _place()
        me = (x, y, c)
        peers = [(x, y, 1 - c)] + [(*chip, c) for chip in _other_chips(x, y)]
        local = [pltpu.make_async_copy(ins[t], outs[t].at[_slot(me)], local_sems.at[t]) for t in range(n)]
        sends, recvs = [], []
        for t in range(n):
            for k, peer in enumerate(peers):
                sems = dict(send_sem=send_sems.at[4 * t + k], recv_sem=recv_sems.at[4 * t + k],
                            device_id=peer, device_id_type=MESH)
                sends.append(pltpu.make_async_remote_copy(src_ref=ins[t], dst_ref=outs[t].at[_slot(me)], **sems))
                recvs.append(pltpu.make_async_remote_copy(src_ref=ins[t], dst_ref=outs[t].at[_slot(peer)], **sems))
        return local, sends, recvs

    def start(ins, outs, send_sems, recv_sems, local_sems):
        local, sends, _ = copies(ins, outs, send_sems, recv_sems, local_sems)
        for cp in local + sends:
            cp.start()

    def finish(ins, outs, send_sems, recv_sems, local_sems):
        local, sends, recvs = copies(ins, outs, send_sems, recv_sems, local_sems)
        for cp in sends:
            cp.wait_send()
        for cp in recvs:
            cp.wait_recv()
        for cp in local:
            cp.wait()

    outs = [jax.ShapeDtypeStruct((N_DEV,) + b.shape, b.dtype) for b in blocks]
    return _Side(blocks, outs, (4 * n, 4 * n, n), start, finish)


def _gather_pass_side(gathered):
    n = len(gathered)

    def copies(outs, send_sems, recv_sems):
        x, y, c = _place()
        sends, recvs = [], []
        for t in range(n):
            for j, chip in enumerate(_other_chips(x, y)):
                sems = dict(send_sem=send_sems.at[3 * t + j], recv_sem=recv_sems.at[3 * t + j],
                            device_id=(x, y, 1 - c), device_id_type=MESH)
                mine, theirs = outs[t].at[_slot((*chip, c))], outs[t].at[_slot((*chip, 1 - c))]
                sends.append(pltpu.make_async_remote_copy(src_ref=mine, dst_ref=mine, **sems))
                recvs.append(pltpu.make_async_remote_copy(src_ref=mine, dst_ref=theirs, **sems))
        return sends, recvs

    def start(ins, outs, send_sems, recv_sems, local_sems):
        for cp in copies(outs, send_sems, recv_sems)[0]:
            cp.start()

    def finish(ins, outs, send_sems, recv_sems, local_sems):
        sends, recvs = copies(outs, send_sems, recv_sems)
        for cp in sends:
            cp.wait_send()
        for cp in recvs:
            cp.wait_recv()

    outs = [jax.ShapeDtypeStruct(g.shape, g.dtype) for g in gathered]
    return _Side(gathered, outs, (3 * n, 3 * n, 0), start, finish, aliases={t: t for t in range(n)})


def _alone(side, name):
    return _pallas(lambda: None, name=name, out_shape=[], sides=[side])[1][0]


def _all_gather(blocks, name):
    n = len(blocks)

    def body(*refs):
        in_refs, out_refs = refs[:n], refs[n:2 * n]
        send_sems, recv_sems, local_sems = refs[2 * n:]
        x, y, c = _place()
        me, sibling = (x, y, c), (x, y, 1 - c)
        chips = _other_chips(x, y)

        def slot(p):
            return 4 * p[0] + 2 * p[1] + p[2]

        def copy(t, k, block, to, src=None):
            dst = out_refs[t].at[slot(block)]
            return pltpu.make_async_remote_copy(
                src_ref=dst if src is None else src, dst_ref=dst,
                send_sem=send_sems.at[7 * t + k], recv_sem=recv_sems.at[7 * t + k],
                device_id=to, device_id_type=MESH)

        started = []
        for t in range(n):
            mine = pltpu.make_async_copy(in_refs[t], out_refs[t].at[slot(me)], local_sems.at[t])
            mine.start()
            started.append(mine)
        sends = []
        for t in range(n):
            first = [copy(t, 0, me, sibling, src=in_refs[t])]
            first += [copy(t, 1 + j, me, (*chip, c), src=in_refs[t]) for j, chip in enumerate(chips)]
            for cp in first:
                cp.start()
            sends += first
        for t in range(n):
            for j, chip in enumerate(chips):
                copy(t, 1 + j, (*chip, c), me).wait_recv()
                passed = copy(t, 4 + j, (*chip, c), sibling)
                passed.start()
                sends.append(passed)
        for t in range(n):
            copy(t, 0, sibling, me).wait_recv()
            for j, chip in enumerate(chips):
                copy(t, 4 + j, (*chip, 1 - c), me).wait_recv()
        for cp in sends:
            cp.wait_send()
        for mine in started:
            mine.wait()

    return pl.pallas_call(
        body,
        name=name,
        in_specs=[_any_spec()] * n,
        out_specs=[_any_spec()] * n,
        out_shape=[jax.ShapeDtypeStruct((N_DEV,) + b.shape, b.dtype) for b in blocks],
        scratch_shapes=[pltpu.SemaphoreType.DMA((7 * n,)), pltpu.SemaphoreType.DMA((7 * n,)),
                        pltpu.SemaphoreType.DMA((n,))],
    )(*blocks)


def _pair_sum(g8, got, name, tm=1024):
    _, r, c = g8.shape
    tm = tm if r % tm == 0 else r
    core = lax.axis_index("c")

    def body(core_ref, mine_ref, got_ref, o_ref):
        south_first = core_ref[0] == 0
        a, b = mine_ref[...], got_ref[...]
        o_ref[...] = (jnp.where(south_first, a, b) + jnp.where(south_first, b, a)).astype(o_ref.dtype)

    return pl.pallas_call(
        body,
        name=name,
        grid_spec=pltpu.PrefetchScalarGridSpec(
            num_scalar_prefetch=1,
            grid=(N_CHIP, r // tm),
            in_specs=[pl.BlockSpec((None, tm, c), lambda j, i, core_ref: (2 * j + core_ref[0], i, 0)),
                      pl.BlockSpec((None, tm, c), lambda j, i, core_ref: (j, i, 0))],
            out_specs=pl.BlockSpec((None, tm, c), lambda j, i, core_ref: (j, i, 0)),
        ),
        out_shape=jax.ShapeDtypeStruct((N_CHIP, r, c), PAYLOAD),
        compiler_params=_cparams(("parallel", "parallel"), 32),
    )(jnp.reshape(core, (1,)).astype(jnp.int32), g8, got)


def _all_reduce_small(s, name="small_all_reduce"):
    r, c = s.shape

    def body(s_ref, o_ref, buf_ref, send_sems, recv_sems):
        x, y, c_ = _place()
        me = 4 * x + 2 * y + c_
        buf_ref[me] = s_ref[...]

        def copy(k, slot, peer):
            return pltpu.make_async_remote_copy(
                src_ref=s_ref, dst_ref=buf_ref.at[slot],
                send_sem=send_sems.at[k - 1], recv_sem=recv_sems.at[k - 1],
                device_id=peer, device_id_type=MESH)

        peers = []
        for k in range(1, N_DEV):
            peer = (x ^ ((k >> 2) & 1), y ^ ((k >> 1) & 1), c_ ^ (k & 1))
            peers.append(peer)
            copy(k, me, peer).start()
        for k, peer in zip(range(1, N_DEV), peers):
            cp = copy(k, 4 * peer[0] + 2 * peer[1] + peer[2], peer)
            cp.wait_send()
            cp.wait_recv()
        total = buf_ref[0]
        for j in range(1, N_DEV):
            total = total + buf_ref[j]
        o_ref[...] = total

    return pl.pallas_call(
        body,
        name=name,
        out_shape=jax.ShapeDtypeStruct((r, c), F32),
        in_specs=[pl.BlockSpec(memory_space=pltpu.VMEM)],
        out_specs=pl.BlockSpec(memory_space=pltpu.VMEM),
        scratch_shapes=[pltpu.VMEM((N_DEV, r, c), F32), pltpu.SemaphoreType.DMA((N_DEV - 1,)),
                        pltpu.SemaphoreType.DMA((N_DEV - 1,))],
    )(s)


def _kv_weights(g_kv):
    d = g_kv.shape[1]
    nh = d // HEAD
    wkv = jnp.reshape(jnp.transpose(g_kv, (1, 0, 2)), (d, -1))
    return wkv[:, :2 * d], jnp.pad(wkv[:, 2 * d:], ((0, 0), (0, 128 - nh)))


def _kv_grad_blocks(dwkv2_t, dwfl_t):
    d = dwkv2_t.shape[1]
    return jnp.reshape(jnp.concatenate([dwkv2_t, dwfl_t[:d // HEAD]], axis=0), (N_DEV, -1, d))


def _local_step(x, target, meta, gamma, a_norm, wa_in, a_out_norm, wa_out, kv_norm, late, b_f, g_k, b_norm, g_q,
                dist):
    d = x.shape[1]
    nh = d // HEAD
    cols = d // N_DEV
    big = dict(tm=1408, tn=512, tk=2048)

    lb = _lb_fwd(gamma)
    h0, hn_a = _embed(x, meta, a_norm)
    lp = h0.shape[0]
    if dist:
        a_out_blk, kv_blk, b_in_blk, b_out_blk = late
        u4, (first,) = _matmul(hn_a, wa_in, "nn", F32, "a_in", out_parts=4,
                               sides=[_gather_own_side([a_out_blk, b_in_blk])], **big)
        o_a, og_a, ((g_a_out, g_b_in), (g_kv,)) = _hgrn2_fwd(
            u4, lb, a_out_norm, sides=[_gather_pass_side(first), _gather_own_side([kv_blk])])
        wa_out = jnp.reshape(g_a_out, (d, d))
        h1, ((g_kv,), (g_b_out,)) = _matmul(og_a, wa_out, "nn", F32, "a_out", add=h0,
                                            sides=[_gather_pass_side([g_kv]), _gather_own_side([b_out_blk])], **big)
        (wkv2, wfl), wb_in = _kv_weights(g_kv), g_b_in
    else:
        u4 = _matmul(hn_a, wa_in, "nn", F32, "a_in", out_parts=4, **big)
        o_a, og_a = _hgrn2_fwd(u4, lb, a_out_norm)
        h1 = _matmul(og_a, wa_out, "nn", F32, "a_out", add=h0, **big)
        wkv2, wfl, wb_in, wb_out = late
    hk, hb = _rms_fwd(h1, [kv_norm, b_norm], "rms_kv_b")
    if dist:
        ukv, ((g_b_out,),) = _matmul(hk, wkv2, "nn", F32, "kv_in", out_parts=2,
                                     sides=[_gather_pass_side([g_b_out])], **big)
        wb_out = jnp.reshape(g_b_out, (d, d))
    else:
        ukv = _matmul(hk, wkv2, "nn", F32, "kv_in", out_parts=2, **big)
    ufl = _matmul(hk, wfl, "nn", F32, "kv_f", **big)
    ub = _matmul(hb, wb_in, "nn", F32, "b_in", out_parts=2, **big)
    f_cum = _fgate_fwd(ufl, b_f)
    qa, ka, va = _attn_operands(ub, ukv, g_q, g_k, f_cum)
    o_b, o_lo, og_b, qb = _fox_fwd(qa, ka, va, ub)
    h2 = _matmul(og_b, wb_out, "nn", F32, "b_out", add=h1, **big)
    loss, dh2 = _loss(h2, target)

    dx_t = dict(tm=1408, tn=512, tk=2048, vmem_mib=56)
    dx_parts_t = dict(tm=704, tn=512, tk=2048, vmem_mib=58, k_whole=True)
    dw_t = dict(tm=512, tn=1024, tk=lp, vmem_mib=58)
    dw_f32_t = dict(tm=1024, tn=512, tk=lp, vmem_mib=58)
    def to_sibling(g8):
        return [_sibling_side([g8])] if dist else []

    def to_chips(partial):
        return [_chips_side([partial])] if dist else []

    def unpack(res, n_sides):
        if not dist:
            return res, [None] * n_sides
        *main, side_results = res
        return (main[0] if len(main) == 1 else tuple(main)), [r[0] for r in side_results]

    dwb_out = _matmul(og_b, dh2, "tn", F32, "b_out_dw", **dw_f32_t)
    g8_b_out = jnp.reshape(dwb_out, (N_DEV, cols, d))
    d_ogb, (got,) = unpack(_matmul(dh2, wb_out, "nt", F32, "b_out_dx", sides=to_sibling(g8_b_out), **dx_t), 1)
    p_b_out = _pair_sum(g8_b_out, got, "pair_sum_b_w_out") if dist else None
    doa = _fox_do(d_ogb, o_b, o_lo, ub)
    (dq, dg_q, dukv, dg_k, dfk), (r_b_out,) = unpack(
        _fox_bwd(qb, ka, va, doa, ukv, g_k, ub, g_q, sides=to_chips(p_b_out)), 1)
    dub = _fox_gate_bwd(ub, d_ogb, o_b, dq)
    d_f = jnp.pad(jnp.transpose(dfk[:, 0, :]), ((0, 0), (0, 128 - nh)))
    dufl, db_f = _fgate_bwd(ufl, b_f, d_f)
    dwb_in = _matmul(hb, dub, "tn", F32, "b_in_dw", out_parts=N_DEV, **dw_t)
    d_hb, (got,) = unpack(_matmul(dub, wb_in, "nt", F32, "b_in_dx", sides=to_sibling(dwb_in), **dx_parts_t), 1)
    p_b_in = _pair_sum(dwb_in, got, "pair_sum_b_w_in") if dist else None
    d_hk, (r_b_in,) = unpack(_matmul(dukv, wkv2, "nt", F32, "kv_dx", sides=to_chips(p_b_in), **dx_parts_t), 1)
    d_hk = _matmul(dufl, wfl, "nt", F32, "kv_f_dx", add=d_hk, **dx_t)
    dwkv2_t = _matmul(dukv, hk, "tn", F32, "kv_dw", **dw_t)
    dwfl_t = _matmul(dufl, hk, "tn", F32, "kv_f_dw", **dw_t)
    g8_kv = _kv_grad_blocks(dwkv2_t, dwfl_t) if dist else None
    dh1, (dg_kv, dg_b) = _rms_bwd(h1, [kv_norm, b_norm], [d_hk, d_hb], dh2, "rms_kv_b_bwd")
    d_oga, (got,) = unpack(_matmul(dh1, wa_out, "nt", F32, "a_out_dx", sides=to_sibling(g8_kv), **dx_t), 1)
    p_kv = _pair_sum(g8_kv, got, "pair_sum_kv_w") if dist else None
    dwa_out = _matmul(og_a, dh1, "tn", F32, "a_out_dw", **dw_f32_t)
    g8_a_out = jnp.reshape(dwa_out, (N_DEV, cols, d))
    (du4, dlb, dg_aout), (r_kv, got) = unpack(
        _hgrn2_bwd(u4, o_a, d_oga, lb, a_out_norm, sides=to_chips(p_kv) + to_sibling(g8_a_out)), 2)
    p_a_out = _pair_sum(g8_a_out, got, "pair_sum_a_w_out") if dist else None
    dwa_in, (r_a_out,) = unpack(
        _matmul(hn_a, du4, "tn", F32, "a_in_dw", out_parts=N_DEV, sides=to_chips(p_a_out), **dw_t), 1)
    row_tiles = lp // _tile(lp, dx_parts_t["tm"], 64)
    if dist and row_tiles >= 2:
        top = max(1, row_tiles // 6)
        d_hna, ((got,),) = _matmul(du4, wa_in, "nt", F32, "a_in_dx_top", rows=(0, top),
                                   sides=to_sibling(dwa_in), **dx_parts_t)
        p_a_in = _pair_sum(dwa_in, got, "pair_sum_a_w_in")
        d_hna, ((r_a_in,),) = _matmul(du4, wa_in, "nt", F32, "a_in_dx_bottom", rows=(top, row_tiles - top), into=d_hna,
                                      sides=to_chips(p_a_in), **dx_parts_t)
    else:
        d_hna, (got,) = unpack(_matmul(du4, wa_in, "nt", F32, "a_in_dx", sides=to_sibling(dwa_in), **dx_parts_t), 1)
        if dist:
            (r_a_in,) = _alone(_chips_side([_pair_sum(dwa_in, got, "pair_sum_a_w_in")]), "grads_to_chips_a_w_in")
    grad_x, dmeta, dg_a = _embed_bwd(h0, a_norm, d_hna, dh1)
    dgamma = _lb_bwd(gamma, dlb)

    grads = dict(meta=dmeta, gamma=dgamma, a_norm=dg_a, a_out_norm=dg_aout, kv_norm=dg_kv,
                 b_f=db_f, g_k=dg_k, b_norm=dg_b, g_q=dg_q)
    if dist:
        grads.update(wa_in=r_a_in, wa_out=r_a_out, wkv=r_kv, wb_in=r_b_in, wb_out=r_b_out)
    else:
        grads.update(wa_in=dwa_in, wa_out=dwa_out, wkv2=dwkv2_t.T, wfl=dwfl_t.T, wb_in=dwb_in, wb_out=dwb_out)
    return loss, grad_x, grads


def kernel(x, meta, gamma_lb, a_norm, a_w_in, a_out_norm, a_w_out, kv_norm, kv_w, fox_b_f, fox_k_norm, b_norm, b_w_in, b_q_norm, b_w_out, loss_target, m_meta, m_gamma_lb, m_a_norm, m_a_w_in, m_a_out_norm, m_a_w_out, m_kv_norm, m_kv_w, m_fox_b_f, m_fox_k_norm, m_b_norm, m_b_w_in, m_b_q_norm, m_b_w_out, v_meta, v_gamma_lb, v_a_norm, v_a_w_in, v_a_out_norm, v_a_w_out, v_kv_norm, v_kv_w, v_fox_b_f, v_fox_k_norm, v_b_norm, v_b_w_in, v_b_q_norm, v_b_w_out):
    d = x.shape[-1]
    nh = d // HEAD
    cols = d // N_DEV
    me = 4 * lax.axis_index("x") + 2 * lax.axis_index("y") + lax.axis_index("c")

    sharded_small = jnp.concatenate([meta, gamma_lb, a_norm, a_out_norm, jnp.zeros((4, cols), F32)], axis=0)
    g_a_in, g_small = _all_gather([a_w_in[0].astype(MXU), sharded_small], "gather_weights")
    wa_out = None
    late = (a_w_out[0].astype(MXU), kv_w.astype(MXU), b_w_in[0].astype(MXU), b_w_out[0].astype(MXU))
    small = jnp.reshape(jnp.transpose(g_small, (1, 0, 2)), (-1, d))
    meta_f, gamma_f, a_norm_f, a_out_norm_f = small[:16], small[16:18], small[18:19], small[19:20]
    b_f = jnp.pad(jnp.reshape(fox_b_f, (1, nh)), ((0, 0), (0, 128 - nh)))
    g_k = jnp.reshape(fox_k_norm, (1, d))
    g_q = jnp.reshape(b_q_norm, (1, d))
    kv_norm_r = jnp.reshape(kv_norm, (1, d))

    loss, grad_x, g = _local_step(x[0], loss_target[0], meta_f, gamma_f, a_norm_f, g_a_in, a_out_norm_f, wa_out,
                                  kv_norm_r, late, b_f, g_k, b_norm, g_q, dist=True)
    loss = lax.psum(loss[0, 0], AXES)

    r_a_in = _adamw(g["wa_in"], a_w_in[0], m_a_w_in[0], v_a_w_in[0], "adamw_a_w_in")
    r_a_out = _adamw(g["wa_out"], a_w_out[0], m_a_w_out[0], v_a_w_out[0], "adamw_a_w_out")
    r_kv = [jnp.transpose(r) for r in _adamw(g["wkv"], kv_w.T, m_kv_w.T, v_kv_w.T, "adamw_kv_w")]
    r_b_in = _adamw(g["wb_in"], b_w_in[0], m_b_w_in[0], v_b_w_in[0], "adamw_b_w_in")
    r_b_out = _adamw(g["wb_out"], b_w_out[0], m_b_w_out[0], v_b_w_out[0], "adamw_b_w_out")

    packed = jnp.concatenate(
        [g["meta"], g["gamma"], g["a_norm"], g["a_out_norm"], g["kv_norm"], g["b_norm"], g["g_k"], g["g_q"],
         jnp.pad(g["b_f"], ((0, 0), (0, d - 128))), jnp.zeros((7, d), F32)], axis=0)
    tot = _all_reduce_small(packed)
    mine = lax.dynamic_slice_in_dim(tot[:20], me * cols, cols, axis=1)
    gs = [mine[:16], mine[16:18], mine[18:19], mine[19:20], tot[20], tot[21:22], jnp.reshape(tot[22], (nh, HEAD)),
          jnp.reshape(tot[23], (1, nh, HEAD)), tot[24, :nh]]
    small_w = [meta, gamma_lb, a_norm, a_out_norm, kv_norm, b_norm, fox_k_norm, b_q_norm, fox_b_f]
    small_m = [m_meta, m_gamma_lb, m_a_norm, m_a_out_norm, m_kv_norm, m_b_norm, m_fox_k_norm, m_b_q_norm, m_fox_b_f]
    small_v = [v_meta, v_gamma_lb, v_a_norm, v_a_out_norm, v_kv_norm, v_b_norm, v_fox_k_norm, v_b_q_norm, v_fox_b_f]

    d_s, m_s, v_s = _adamw_small(gs, small_w, small_m, small_v)

    def ordered(s, a_in, a_out, kv, b_in, b_out):
        return [s[0], s[1], s[2], a_in[None], s[3], a_out[None], s[4], kv, s[8], s[6], s[5], b_in[None], s[7], b_out[None]]

    outs = []
    for i, s in enumerate([gs, d_s, m_s, v_s]):
        outs += ordered(s, r_a_in[i], r_a_out[i], r_kv[i], r_b_in[i], r_b_out[i])
    return (loss, grad_x[None], *outs)
```

```python
import math

import jax
import jax.numpy as jnp
from jax import lax
from jax.experimental import pallas as pl
from jax.experimental.pallas import tpu as pltpu

HEAD = 128
CHUNK = 64
N_META = 16
ROW_PAD = 128 - N_META
EPS = 1e-6
MASK_VALUE = -1e30
ADAM_LR = 0.001
ADAM_B1 = 0.9
ADAM_B2 = 0.999
ADAM_EPS = 1e-08
ADAM_WD = 0.01
ADAM_STEP = 10
N_DEV = 8
N_CHIP = 4
MIB = 1024 * 1024
AXES = ("x", "y", "c")
MESH = pl.DeviceIdType.MESH

F32 = jnp.float32
MXU = jnp.bfloat16
PAYLOAD = jnp.bfloat16
HI = lax.Precision.HIGHEST

NN = (((1,), (0,)), ((), ()))
NT = (((1,), (1,)), ((), ()))
TN = (((0,), (0,)), ((), ()))


def _dot(a, b, dims=NN):
    return lax.dot_general(a.astype(MXU), b.astype(MXU), dims, preferred_element_type=F32)


def _dot_exact(a, b):
    return lax.dot_general(a, b, NN, precision=HI, preferred_element_type=F32)


def _sigmoid(x):
    return 1.0 / (1.0 + jnp.exp(-x))


def _tile(dim, target, unit=128):
    best = None
    t = unit
    while t <= min(dim, target):
        if dim % t == 0:
            best = t
        t += unit
    return best if best is not None else dim


def _cparams(semantics, vmem_mib):
    return pltpu.CompilerParams(dimension_semantics=semantics, vmem_limit_bytes=vmem_mib * MIB)


def _mat_spec(arr, br, bc, rc_of_grid):
    if arr.ndim == 2:
        return pl.BlockSpec((br, bc), rc_of_grid)
    assert arr.shape[2] % bc == 0, (arr.shape, bc)
    per = arr.shape[2] // bc

    def idx(*g):
        r, c = rc_of_grid(*g)
        return (c // per, r, c % per)

    return pl.BlockSpec((None, br, bc), idx)


def _mat_shape(arr):
    return (arr.shape[0], arr.shape[1]) if arr.ndim == 2 else (arr.shape[1], arr.shape[0] * arr.shape[2])


def _matmul(a, b, dims, out_dtype, name, *, add=None, out_parts=1, tm=512, tn=512, tk=512, vmem_mib=48, sides=(),
            k_whole=False, rows=None, into=None):
    ar, ac = _mat_shape(a)
    br_, bc_ = _mat_shape(b)
    if dims == "nn":
        m, k, n = ar, ac, bc_
        assert br_ == k
    elif dims == "nt":
        m, k, n = ar, ac, br_
        assert bc_ == k
    else:
        m, k, n = ac, ar, bc_
        assert br_ == k
    m_unit, n_unit, k_unit = m, n, k
    if a.ndim == 3:
        if dims == "tn":
            m_unit = math.gcd(m_unit, a.shape[2])
        else:
            k_unit = math.gcd(k_unit, a.shape[2])
    if b.ndim == 3:
        if dims == "nt":
            k_unit = math.gcd(k_unit, b.shape[2])
        else:
            n_unit = math.gcd(n_unit, b.shape[2])
    if out_parts > 1:
        n_unit = math.gcd(n_unit, n // out_parts)
    tm = _tile(m_unit, tm, 128 if dims == "tn" else 64)
    tn, tk = _tile(n_unit, tn), _tile(k_unit, tk)
    whole_k = dims == "nt" and tk < k and k_whole
    if whole_k:
        k_chunk, tk = tk, k
    gm, gn, gk = m // tm, n // tn, k // tk
    assert gm * tm == m and gn * tn == n and gk * tk == k, (name, m, n, k, tm, tn, tk)

    def chunk_of(ref, arr, c):
        if arr.ndim == 2:
            return ref[:, c * k_chunk:(c + 1) * k_chunk]
        per = arr.shape[2] // k_chunk
        return ref[c // per, :, (c % per) * k_chunk:(c % per + 1) * k_chunk]

    def all_cols(arr, rows, row_of_grid):
        if arr.ndim == 2:
            return pl.BlockSpec((rows, arr.shape[1]), lambda i, j, kk: (row_of_grid(i, j), 0))
        return pl.BlockSpec((arr.shape[0], rows, arr.shape[2]), lambda i, j, kk: (0, row_of_grid(i, j), 0))

    if dims == "nn":
        a_spec = _mat_spec(a, tm, tk, lambda i, j, kk: (i, kk))
        b_spec = _mat_spec(b, tk, tn, lambda i, j, kk: (kk, j))
        dn = NN
    elif whole_k:
        a_spec = all_cols(a, tm, lambda i, j: i)
        b_spec = all_cols(b, tn, lambda i, j: j)
        dn = NT
    elif dims == "nt":
        a_spec = _mat_spec(a, tm, tk, lambda i, j, kk: (i, kk))
        b_spec = _mat_spec(b, tn, tk, lambda i, j, kk: (j, kk))
        dn = NT
    else:
        a_spec = _mat_spec(a, tk, tm, lambda i, j, kk: (kk, i))
        b_spec = _mat_spec(b, tk, tn, lambda i, j, kk: (kk, j))
        dn = TN

    if out_parts > 1:
        per = (n // out_parts) // tn
        out_shape = jax.ShapeDtypeStruct((out_parts, m, n // out_parts), out_dtype)
        o_spec = pl.BlockSpec((None, tm, tn), lambda i, j, kk: (j // per, i, j % per))
    else:
        out_shape = jax.ShapeDtypeStruct((m, n), out_dtype)
        o_spec = pl.BlockSpec((tm, tn), lambda i, j, kk: (i, j))

    in_specs = [a_spec, b_spec]
    args = [a, b]
    if add is not None:
        in_specs.append(pl.BlockSpec((tm, tn), lambda i, j, kk: (i, j)))
        args.append(add)
    if rows is not None:
        first_tile, gm = rows

        def shifted(spec):
            return pl.BlockSpec(spec.block_shape, lambda i, j, kk: spec.index_map(i + first_tile, j, kk))

        in_specs = [shifted(s) for s in in_specs]
        o_spec = shifted(o_spec)
    aliases = {}
    if into is not None:
        aliases[len(args)] = 0
        in_specs.append(_any_spec())
        args.append(into)

    def body(*refs):
        a_ref, b_ref, *others, o_ref, acc_ref = refs
        add_ref = others[0] if add is not None else None
        kk = pl.program_id(2)
        if whole_k:
            part = sum(lax.dot_general(chunk_of(a_ref, a, c).astype(MXU), chunk_of(b_ref, b, c).astype(MXU), dn,
                                       preferred_element_type=F32) for c in range(k // k_chunk))
        else:
            part = lax.dot_general(a_ref[...].astype(MXU), b_ref[...].astype(MXU), dn, preferred_element_type=F32)

        def finish(total):
            if add_ref is not None:
                total = total + add_ref[...]
            o_ref[...] = total.astype(o_ref.dtype)

        if gk == 1:
            finish(part)
        else:
            @pl.when(kk == 0)
            def _():
                acc_ref[...] = part

            @pl.when(jnp.logical_and(kk > 0, kk < gk - 1))
            def _():
                acc_ref[...] += part

            @pl.when(kk == gk - 1)
            def _():
                finish(acc_ref[...] + part)

    (out,), side_results = _pallas(
        body,
        name=name,
        args=args,
        grid=(gm, gn, gk),
        in_specs=in_specs,
        out_specs=[o_spec],
        out_shape=[out_shape],
        scratch_shapes=[pltpu.VMEM((tm, tn) if gk > 1 else (8, 128), F32)],
        semantics=("parallel", "parallel", "arbitrary"),
        vmem_mib=vmem_mib,
        sides=sides,
        aliases=aliases,
    )
    return (out, side_results) if sides else out


def _rms_fwd(h, gains, name, tm=384):
    lp, d = h.shape
    tm = _tile(lp, tm)
    n = len(gains)

    def body(*refs):
        h_ref = refs[0]
        g_refs = refs[1:1 + n]
        o_refs = refs[1 + n:]
        x = h_ref[...]
        y = x * lax.rsqrt(jnp.mean(x * x, axis=-1, keepdims=True) + EPS)
        for g_ref, o_ref in zip(g_refs, o_refs):
            o_ref[...] = (y * g_ref[...]).astype(o_ref.dtype)

    row = pl.BlockSpec((tm, d), lambda i: (i, 0))
    vec = pl.BlockSpec((1, d), lambda i: (0, 0))
    return pl.pallas_call(
        body,
        name=name,
        grid=(lp // tm,),
        in_specs=[row] + [vec] * n,
        out_specs=[row] * n,
        out_shape=[jax.ShapeDtypeStruct((lp, d), MXU)] * n,
        compiler_params=_cparams(("parallel",), 40),
    )(h, *gains)


def _embed(x, meta, gain, name="embed", sides=()):
    seq, d = x.shape
    lp = ROW_PAD + N_META + seq
    blk = ROW_PAD + N_META

    def body(x_ref, meta_ref, g_ref, h_ref, o_ref):
        i = pl.program_id(0)

        @pl.when(i == 0)
        def _():
            h_ref[0:ROW_PAD, :] = jnp.zeros((ROW_PAD, d), F32)
            h_ref[ROW_PAD:blk, :] = meta_ref[...]

        @pl.when(i > 0)
        def _():
            h_ref[...] = x_ref[...]

        h = h_ref[...]
        y = h * lax.rsqrt(jnp.mean(h * h, axis=-1, keepdims=True) + EPS)
        o_ref[...] = (y * g_ref[...]).astype(o_ref.dtype)

    row = pl.BlockSpec((blk, d), lambda i: (i, 0))
    outs, side_results = _pallas(
        body,
        name=name,
        args=(x, meta, gain),
        grid=(lp // blk,),
        in_specs=[pl.BlockSpec((blk, d), lambda i: (jnp.maximum(i - 1, 0), 0)),
                  pl.BlockSpec((N_META, d), lambda i: (0, 0)), pl.BlockSpec((1, d), lambda i: (0, 0))],
        out_specs=[row, row],
        out_shape=[jax.ShapeDtypeStruct((lp, d), F32), jax.ShapeDtypeStruct((lp, d), MXU)],
        semantics=("arbitrary",),
        vmem_mib=32,
        sides=sides,
    )
    return (*outs, side_results) if sides else tuple(outs)


def _embed_bwd(h, gain, dy, res, name="embed_bwd"):
    lp, d = h.shape
    blk = ROW_PAD + N_META

    def body(h_ref, res_ref, g_ref, dy_ref, dx_ref, dmeta_ref, dg_ref):
        i = pl.program_id(0)
        x = h_ref[...]
        rstd = lax.rsqrt(jnp.mean(x * x, axis=-1, keepdims=True) + EPS)
        xhat = x * rstd
        dy = dy_ref[...]
        gdy = dy * g_ref[...]
        dh = res_ref[...] + rstd * (gdy - xhat * jnp.mean(gdy * xhat, axis=-1, keepdims=True))
        part = jnp.sum(dy * xhat, axis=0, keepdims=True)

        @pl.when(i == 0)
        def _():
            dmeta_ref[...] = dh[ROW_PAD:blk, :]
            dg_ref[...] = part

        @pl.when(i > 0)
        def _():
            dx_ref[...] = dh
            dg_ref[...] += part

    row = pl.BlockSpec((blk, d), lambda i: (i, 0))
    vec = pl.BlockSpec((1, d), lambda i: (0, 0))
    return pl.pallas_call(
        body,
        name=name,
        grid=(lp // blk,),
        in_specs=[row, row, vec, row],
        out_specs=[pl.BlockSpec((blk, d), lambda i: (jnp.maximum(i - 1, 0), 0)),
                   pl.BlockSpec((N_META, d), lambda i: (0, 0)), vec],
        out_shape=[jax.ShapeDtypeStruct((lp - blk, d), F32), jax.ShapeDtypeStruct((N_META, d), F32),
                   jax.ShapeDtypeStruct((1, d), F32)],
        compiler_params=_cparams(("arbitrary",), 32),
    )(h, res, gain, dy)


def _rms_bwd(h, gains, dys, res, name, tm=384):
    lp, d = h.shape
    tm = _tile(lp, tm)
    n = len(gains)

    def body(*refs):
        h_ref, res_ref = refs[0], refs[1]
        g_refs = refs[2:2 + n]
        dy_refs = refs[2 + n:2 + 2 * n]
        dh_ref = refs[2 + 2 * n]
        dg_refs = refs[3 + 2 * n:]
        i = pl.program_id(0)
        x = h_ref[...]
        rstd = lax.rsqrt(jnp.mean(x * x, axis=-1, keepdims=True) + EPS)
        xhat = x * rstd
        dh = res_ref[...]
        for g_ref, dy_ref, dg_ref in zip(g_refs, dy_refs, dg_refs):
            dy = dy_ref[...]
            gdy = dy * g_ref[...]
            dh = dh + rstd * (gdy - xhat * jnp.mean(gdy * xhat, axis=-1, keepdims=True))
            part = jnp.sum(dy * xhat, axis=0, keepdims=True)

            @pl.when(i == 0)
            def _():
                dg_ref[...] = part

            @pl.when(i > 0)
            def _():
                dg_ref[...] += part

        dh_ref[...] = dh

    row = pl.BlockSpec((tm, d), lambda i: (i, 0))
    vec = pl.BlockSpec((1, d), lambda i: (0, 0))
    outs = pl.pallas_call(
        body,
        name=name,
        grid=(lp // tm,),
        in_specs=[row, row] + [vec] * n + [row] * n,
        out_specs=[row] + [vec] * n,
        out_shape=[jax.ShapeDtypeStruct((lp, d), F32)] + [jax.ShapeDtypeStruct((1, d), F32)] * n,
        compiler_params=_cparams(("arbitrary",), 56),
    )(h, res, *gains, *dys)
    return outs[0], list(outs[1:])


def _lb_fwd(gamma):
    def body(g_ref, lb_ref):
        g = g_ref[...]
        e = jnp.exp(g - jnp.max(g, axis=0, keepdims=True))
        lb_ref[...] = (e / jnp.sum(e, axis=0, keepdims=True))[0:1, :]

    return pl.pallas_call(body, name="lb_fwd", out_shape=jax.ShapeDtypeStruct((1, gamma.shape[1]), F32))(gamma)


def _lb_bwd(gamma, dlb):
    def body(g_ref, dlb_ref, dg_ref):
        g = g_ref[...]
        e = jnp.exp(g - jnp.max(g, axis=0, keepdims=True))
        s = e / jnp.sum(e, axis=0, keepdims=True)
        rows = lax.broadcasted_iota(jnp.int32, g.shape, 0)
        ds = jnp.where(rows == 0, dlb_ref[...], 0.0)
        dg_ref[...] = s * (ds - jnp.sum(s * ds, axis=0, keepdims=True))

    return pl.pallas_call(body, name="lb_bwd", out_shape=jax.ShapeDtypeStruct(gamma.shape, F32))(gamma, dlb)


def _tri(n, lower):
    r = lax.broadcasted_iota(jnp.int32, (n, n), 0)
    c = lax.broadcasted_iota(jnp.int32, (n, n), 1)
    return jnp.where((r >= c) if lower else (r <= c), 1.0, 0.0).astype(F32)


def _group(nc, most):
    return max(u for u in range(1, most + 1) if nc % u == 0)


def _running_sum(tri, x):
    hi = x.astype(MXU)
    rest = x - hi.astype(F32)
    mid = rest.astype(MXU)
    lo = (rest - mid.astype(F32)).astype(MXU)
    return _dot(tri, hi) + _dot(tri, mid) + _dot(tri, lo)


def _causal(n):
    r = lax.broadcasted_iota(jnp.int32, (n, n), 0)
    c = lax.broadcasted_iota(jnp.int32, (n, n), 1)
    return r >= c


def _chunk_gates(u_ref, lb, c):
    sl = pl.ds(pl.multiple_of(c * CHUNK, CHUNK), CHUNK)
    valid = (c * CHUNK + lax.broadcasted_iota(jnp.int32, (CHUNK, HEAD), 0)) >= ROW_PAD
    uq = u_ref[0, sl, :]
    uf = u_ref[1, sl, :]
    sq = _sigmoid(uq)
    sf = _sigmoid(uf)
    fg = lb + (1.0 - lb) * sf
    return dict(sl=sl, valid=valid, uq=uq, sq=sq, sf=sf, fg=fg, q=jnp.where(valid, uq * sq, 0.0),
                logf=jnp.where(valid, jnp.log(fg), 0.0), k=jnp.where(valid, 1.0 - fg, 0.0),
                v=jnp.where(valid, u_ref[2, sl, :], 0.0))


def _chunk_decays(x, b):
    b_last = b[CHUNK - 1:CHUNK, :]
    b_mid = b[CHUNK // 2 - 1:CHUNK // 2, :]
    e_qi = jnp.exp(b - b_mid)
    e_ki = jnp.exp(b_mid - b)
    e_kd = jnp.exp(b_last - b)
    e_qe = jnp.exp(b)
    q, k = x["q"], x["k"]
    return dict(x, e_qi=e_qi, e_ki=e_ki, e_kd=e_kd, e_qe=e_qe, qi=q * e_qi, ki=k * e_ki, kd=k * e_kd, qe=q * e_qe,
                decay=jnp.exp(b_last))


def _chunks(u_ref, lb, tri_lower, cs):
    gates = [_chunk_gates(u_ref, lb, c) for c in cs]
    sums = [_running_sum(tri_lower, x["logf"]) for x in gates]
    return [_chunk_decays(x, b) for x, b in zip(gates, sums)]


def _hgrn2_fwd(u4, lb, g_out, name="hgrn2_fwd", sides=()):
    _, lp, d = u4.shape
    nh, nc = d // HEAD, lp // CHUNK
    per = _group(nc, 22)

    def body(u_ref, lb_ref, g_ref, o_ref, og_ref):
        lb_v = lb_ref[...]
        g = g_ref[...]

        tri_lower = _tri(CHUNK, True).astype(MXU)
        causal = _causal(CHUNK)

        def step(i, st):
            xs = _chunks(u_ref, lb_v, tri_lower, [i * per + u for u in range(per)])
            scores = [_dot(x["qi"], x["ki"], NT) for x in xs]
            updates = [_dot(x["v"], x["kd"], TN) for x in xs]
            states = []
            for x, upd in zip(xs, updates):
                states.append(st)
                st = x["decay"] * st + upd
            outs = [_dot(jnp.where(causal, a, 0.0), x["v"]) + _dot(x["qe"], s, NT)
                    for x, a, s in zip(xs, scores, states)]
            for x, o in zip(xs, outs):
                o_ref[x["sl"], :] = o
                on = o * lax.rsqrt(jnp.mean(o * o, axis=-1, keepdims=True) + EPS) * g
                z = u_ref[3, x["sl"], :]
                og_ref[x["sl"], :] = (on * (z * _sigmoid(z))).astype(og_ref.dtype)
            return st

        lax.fori_loop(0, nc // per, step, jnp.zeros((HEAD, HEAD), F32))

    slab = pl.BlockSpec((lp, HEAD), lambda h: (0, h))
    vec = pl.BlockSpec((1, HEAD), lambda h: (0, h))
    outs, side_results = _pallas(
        body,
        name=name,
        args=(u4, lb, g_out),
        grid=(nh,),
        in_specs=[pl.BlockSpec((4, lp, HEAD), lambda h: (0, 0, h)), vec, vec],
        out_specs=[slab, slab],
        out_shape=[jax.ShapeDtypeStruct((lp, d), F32), jax.ShapeDtypeStruct((lp, d), MXU)],
        semantics=("parallel",),
        vmem_mib=48,
        sides=sides,
    )
    return (*outs, side_results) if sides else tuple(outs)


def _hgrn2_bwd(u4, o, d_og, lb, g_out, name="hgrn2_bwd", sides=()):
    _, lp, d = u4.shape
    nh, nc = d // HEAD, lp // CHUNK
    per_f, per = _group(nc, 6), _group(nc, 11)

    def body(u_ref, o_ref, dog_ref, lb_ref, g_ref, du_ref, dlb_ref, dg_ref, st_ref, do_ref):
        lb_v = lb_ref[...]
        g = g_ref[...]

        tri_lower = _tri(CHUNK, True).astype(MXU)
        tri_upper = _tri(CHUNK, False).astype(MXU)
        causal = _causal(CHUNK)

        def fwd_step(i, carry):
            st, dg_acc = carry
            cs = [i * per_f + u for u in range(per_f)]
            xs = _chunks(u_ref, lb_v, tri_lower, cs)
            updates = [_dot(x["v"], x["kd"], TN) for x in xs]
            for c, x, upd in zip(cs, xs, updates):
                st_ref[c] = st
                st = x["decay"] * st + upd
            for x in xs:
                sl = x["sl"]
                ov = o_ref[sl, :]
                rstd = lax.rsqrt(jnp.mean(ov * ov, axis=-1, keepdims=True) + EPS)
                on = ov * rstd
                z = u_ref[3, sl, :]
                sz = _sigmoid(z)
                dog = dog_ref[sl, :]
                dy = dog * (z * sz)
                dz = dog * (on * g) * (sz * (1.0 + z * (1.0 - sz)))
                du_ref[3, sl, :] = dz.astype(du_ref.dtype)
                gdy = dy * g
                do = rstd * (gdy - on * jnp.mean(gdy * on, axis=-1, keepdims=True))
                do_ref[sl, :] = jnp.where(x["valid"], do, 0.0)
                dg_acc = dg_acc + jnp.sum(dy * on, axis=0, keepdims=True)
            return st, dg_acc

        _, dg_tot = lax.fori_loop(0, nc // per_f, fwd_step, (jnp.zeros((HEAD, HEAD), F32), jnp.zeros((1, HEAD), F32)))
        dg_ref[...] = dg_tot

        def bwd_step(i, carry):
            gt, dlb_acc = carry
            cs = [nc - 1 - (i * per + u) for u in range(per)]
            xs = _chunks(u_ref, lb_v, tri_lower, cs)
            dos = [do_ref[x["sl"], :] for x in xs]
            sts = [st_ref[c] for c in cs]
            scores = [jnp.where(causal, _dot(x["qi"], x["ki"], NT), 0.0) for x in xs]
            d_scores = [jnp.where(causal, _dot(do, x["v"], NT), 0.0) for x, do in zip(xs, dos)]
            d_qes = [_dot(do, st) for do, st in zip(dos, sts)]
            g_updates = [_dot(do, x["qe"], TN) for x, do in zip(xs, dos)]
            gts = []
            for x, upd in zip(xs, g_updates):
                gts.append(gt)
                gt = x["decay"] * gt + upd
            d_kds = [_dot(x["v"], g_) for x, g_ in zip(xs, gts)]
            dvs = [_dot(x["kd"], g_, NT) + _dot(a, do, TN) for x, g_, a, do in zip(xs, gts, scores, dos)]
            d_qis = [_dot(d_a, x["ki"]) for x, d_a in zip(xs, d_scores)]
            d_kis = [_dot(d_a, x["qi"], TN) for x, d_a in zip(xs, d_scores)]
            rows = lax.broadcasted_iota(jnp.int32, (CHUNK, HEAD), 0)
            dbs = []
            for x, g_, st, d_qi, d_ki, d_qe, d_kd in zip(xs, gts, sts, d_qis, d_kis, d_qes, d_kds):
                t_qi, t_ki, t_qe, t_kd = d_qi * x["qi"], d_ki * x["ki"], d_qe * x["qe"], d_kd * x["kd"]
                d_decay = jnp.sum(g_ * st, axis=0, keepdims=True)
                d_mid = jnp.sum(t_ki - t_qi, axis=0, keepdims=True)
                d_last = jnp.sum(t_kd, axis=0, keepdims=True) + d_decay * x["decay"]
                dbs.append(t_qi - t_ki + t_qe - t_kd + jnp.where(rows == CHUNK // 2 - 1, d_mid, 0.0)
                           + jnp.where(rows == CHUNK - 1, d_last, 0.0))
            dlogfs = [_running_sum(tri_upper, db) for db in dbs]
            for x, dlogf, dv, d_qi, d_ki, d_qe, d_kd in zip(xs, dlogfs, dvs, d_qis, d_kis, d_qes, d_kds):
                sl = x["sl"]
                dq = d_qi * x["e_qi"] + d_qe * x["e_qe"]
                dk = d_ki * x["e_ki"] + d_kd * x["e_kd"]
                valid, sq, sf, uq = x["valid"], x["sq"], x["sf"], x["uq"]
                dfg = jnp.where(valid, dlogf / x["fg"] - dk, 0.0)
                du_ref[0, sl, :] = jnp.where(valid, dq * (sq * (1.0 + uq * (1.0 - sq))), 0.0).astype(du_ref.dtype)
                du_ref[1, sl, :] = (dfg * (1.0 - lb_v) * (sf * (1.0 - sf))).astype(du_ref.dtype)
                du_ref[2, sl, :] = jnp.where(valid, dv, 0.0).astype(du_ref.dtype)
                dlb_acc = dlb_acc + jnp.sum(dfg * (1.0 - sf), axis=0, keepdims=True)
            return gt, dlb_acc

        _, dlb_tot = lax.fori_loop(0, nc // per, bwd_step, (jnp.zeros((HEAD, HEAD), F32), jnp.zeros((1, HEAD), F32)))
        dlb_ref[...] = dlb_tot

    slab = pl.BlockSpec((lp, HEAD), lambda h: (0, h))
    vec = pl.BlockSpec((1, HEAD), lambda h: (0, h))
    quad = pl.BlockSpec((4, lp, HEAD), lambda h: (0, 0, h))
    outs, side_results = _pallas(
        body,
        name=name,
        args=(u4, o, d_og, lb, g_out),
        grid=(nh,),
        in_specs=[quad, slab, slab, vec, vec],
        out_specs=[quad, vec, vec],
        out_shape=[jax.ShapeDtypeStruct((4, lp, d), MXU), jax.ShapeDtypeStruct((1, d), F32),
                   jax.ShapeDtypeStruct((1, d), F32)],
        scratch_shapes=[pltpu.VMEM((nc, HEAD, HEAD), F32), pltpu.VMEM((lp, HEAD), F32)],
        semantics=("parallel",),
        vmem_mib=58,
        sides=sides,
    )
    return (*outs, side_results) if sides else tuple(outs)


WIDE = 2 * HEAD
INV_SCALE = HEAD ** 0.5


def _split3(x):
    hi = x.astype(MXU).astype(F32)
    rest = x - hi
    mid = rest.astype(MXU).astype(F32)
    return hi, mid, (rest - mid).astype(MXU).astype(F32)


def _extra_cols(rows, first, second):
    lane = lax.broadcasted_iota(jnp.int32, (rows, HEAD), 1)
    out = jnp.where(lane < 6, 1.0, 0.0).astype(F32)
    for base, terms in ((0, first), (3, second)):
        if terms is not None:
            for j, term in enumerate(terms):
                out = jnp.where(lane == base + j, term, out)
    return out


def _head_col(a, h):
    lane = lax.broadcasted_iota(jnp.int32, a.shape, 1)
    return jnp.sum(jnp.where(lane == h, a, 0.0), axis=-1, keepdims=True)


def _attn_operands(ub, ukv, g_q, g_k, f_cum, name="attn_operands", tm=384):
    _, lp, d = ub.shape
    tm = _tile(lp, tm)
    nh = d // HEAD

    def body(q_ref, k_ref, v_ref, gq_ref, gk_ref, f_ref, qa_ref, ka_ref, va_ref):
        i = pl.program_id(0)
        f = f_ref[...]
        is_pad = (i * tm + lax.broadcasted_iota(jnp.int32, (tm, 1), 0)) < ROW_PAD
        ones_only = _extra_cols(tm, None, (0.0, 0.0, 0.0)).astype(MXU)
        for h in range(nh):
            hs = slice(h * HEAD, (h + 1) * HEAD)
            lo, hi = h * WIDE, h * WIDE + HEAD
            f_h = _head_col(f, h)
            for x_ref, g_ref, o_ref in ((q_ref, gq_ref, qa_ref), (k_ref, gk_ref, ka_ref)):
                x = x_ref[:, hs]
                y = x * lax.rsqrt(jnp.mean(x * x, axis=-1, keepdims=True) + EPS)
                o_ref[:, lo:hi] = (y * g_ref[:, hs]).astype(o_ref.dtype)
            qa_ref[:, hi:hi + HEAD] = _extra_cols(tm, _split3(f_h * INV_SCALE), None).astype(MXU)
            f_key = jnp.where(is_pad, -MASK_VALUE, f_h)
            ka_ref[:, hi:hi + HEAD] = _extra_cols(tm, None, _split3(-f_key * INV_SCALE)).astype(MXU)
            va_ref[:, lo:hi] = v_ref[:, hs].astype(MXU)
            va_ref[:, hi:hi + HEAD] = ones_only

    wide = pl.BlockSpec((tm, nh * WIDE), lambda i: (i, 0))
    vec = pl.BlockSpec((1, d), lambda i: (0, 0))
    return pl.pallas_call(
        body,
        name=name,
        grid=(lp // tm,),
        in_specs=[pl.BlockSpec((None, tm, d), lambda i: (0, i, 0)), pl.BlockSpec((None, tm, d), lambda i: (0, i, 0)),
                  pl.BlockSpec((None, tm, d), lambda i: (1, i, 0)), vec, vec, pl.BlockSpec((tm, 128), lambda i: (i, 0))],
        out_specs=[wide, wide, wide],
        out_shape=[jax.ShapeDtypeStruct((lp, nh * WIDE), MXU)] * 3,
        compiler_params=_cparams(("parallel",), 56),
    )(ub, ukv, ukv, g_q, g_k, f_cum)


def _head_rms_bwd_tile(x, g, dy):
    rstd = lax.rsqrt(jnp.mean(x * x, axis=-1, keepdims=True) + EPS)
    xhat = x * rstd
    gdy = dy * g
    dx = rstd * (gdy - xhat * jnp.mean(gdy * xhat, axis=-1, keepdims=True))
    return dx, jnp.sum(dy * xhat, axis=0, keepdims=True)


def _fgate_fwd(ufl, b_f):
    lp = ufl.shape[0]
    nb = lp // 128

    def body(u_ref, b_ref, f_ref):
        def step(i, carry):
            sl = pl.ds(pl.multiple_of(i * 128, 128), 128)
            valid = (i * 128 + lax.broadcasted_iota(jnp.int32, (128, 128), 0)) >= ROW_PAD
            x = u_ref[sl, :] + b_ref[...]
            logf = jnp.where(valid, jnp.minimum(x, 0.0) - jnp.log(1.0 + jnp.exp(-jnp.abs(x))), 0.0)
            f = _dot_exact(_tri(128, True), logf) + carry
            f_ref[sl, :] = f
            return f[127:128, :]

        lax.fori_loop(0, nb, step, jnp.zeros((1, 128), F32))

    return pl.pallas_call(body, name="fgate_fwd", out_shape=jax.ShapeDtypeStruct((lp, 128), F32))(ufl, b_f)


def _fgate_bwd(ufl, b_f, d_f):
    lp = ufl.shape[0]
    nb = lp // 128

    def body(u_ref, b_ref, df_ref, du_ref, db_ref):
        def step(j, carry):
            later, db_acc = carry
            i = nb - 1 - j
            sl = pl.ds(pl.multiple_of(i * 128, 128), 128)
            valid = (i * 128 + lax.broadcasted_iota(jnp.int32, (128, 128), 0)) >= ROW_PAD
            x = u_ref[sl, :] + b_ref[...]
            df = df_ref[sl, :]
            dlogf = _dot_exact(_tri(128, False), df) + later
            dx = jnp.where(valid, dlogf * _sigmoid(-x), 0.0)
            du_ref[sl, :] = dx.astype(du_ref.dtype)
            return later + jnp.sum(df, axis=0, keepdims=True), db_acc + jnp.sum(dx, axis=0, keepdims=True)

        _, db_tot = lax.fori_loop(0, nb, step, (jnp.zeros((1, 128), F32), jnp.zeros((1, 128), F32)))
        db_ref[...] = db_tot

    return pl.pallas_call(
        body, name="fgate_bwd",
        out_shape=[jax.ShapeDtypeStruct((lp, 128), MXU), jax.ShapeDtypeStruct((1, 128), F32)],
    )(ufl, b_f, d_f)


STRIP = 32
PAIR = 2


def _strip_causal(r, t):
    row = r + lax.broadcasted_iota(jnp.int32, (STRIP, t), 0)
    col = lax.broadcasted_iota(jnp.int32, (STRIP, t), 1)
    return col <= row


def _fox_fwd(qa, ka, va, ub, name="fox_fwd", t=384):
    lp = qa.shape[0]
    d = ub.shape[2]
    t = _tile(lp, t)
    nh, nq = d // HEAD, lp // t
    scale = HEAD ** -0.5

    assert nh % PAIR == 0
    heads = range(PAIR)

    def body(q_ref, k_ref, v_ref, z_ref, o_ref, olo_ref, og_ref, qb_ref,
             s_ref, p_ref, m_ref, a_ref, l_ref, acc_ref):
        qb = pl.program_id(1)
        m_ref[...] = jnp.full((PAIR, t, 1), MASK_VALUE, F32)
        l_ref[...] = jnp.zeros((PAIR, t, 128), F32)
        acc_ref[...] = jnp.zeros((PAIR, 2 * t, HEAD), F32)
        p_ref[1] = jnp.zeros((PAIR, 2 * t, t), MXU)
        a_ref[1] = jnp.ones((PAIR, t, 1), F32)

        def scores(kb, buf):
            ks = pl.ds(pl.multiple_of(kb * t, t), t)
            for j in heads:
                ws = slice(j * WIDE, (j + 1) * WIDE)
                s_ref[buf, j] = _dot(q_ref[:, ws], k_ref[ks, ws], NT)

        def weighted_sum(kb, buf):
            ks = pl.ds(pl.multiple_of(kb * t, t), t)
            for j in heads:
                pv = _dot(p_ref[buf, j], v_ref[ks, j * WIDE:j * WIDE + HEAD])
                alpha = a_ref[buf, j]
                acc_ref[j, 0:t, :] = alpha * acc_ref[j, 0:t, :] + pv[0:t]
                acc_ref[j, t:2 * t, :] = alpha * acc_ref[j, t:2 * t, :] + pv[t:2 * t]

        def softmax_update(buf, diagonal):
            for j in heads:
                for r in range(0, t, STRIP):
                    rs = slice(r, r + STRIP)
                    x = s_ref[buf, j, rs, :] * scale
                    if diagonal:
                        x = jnp.where(_strip_causal(r, t), x, MASK_VALUE)
                    m_old = m_ref[j, rs, :]
                    m_new = jnp.maximum(m_old, jnp.max(x, axis=-1, keepdims=True))
                    alpha = jnp.exp(m_old - m_new)
                    p = jnp.exp(x - m_new)
                    m_ref[j, rs, :] = m_new
                    a_ref[buf, j, rs, :] = alpha
                    l_ref[j, rs, :] = alpha * l_ref[j, rs, :] + sum(p[:, c:c + 128] for c in range(0, t, 128))
                    p_hi = p.astype(MXU)
                    p_ref[buf, j, rs, :] = p_hi
                    p_ref[buf, j, t + r:t + r + STRIP, :] = (p - p_hi.astype(F32)).astype(MXU)

        def off_diagonal(kb, cur):
            weighted_sum(jnp.maximum(kb - 1, 0), 1 - cur)
            scores(kb + 1, 1 - cur)
            softmax_update(cur, False)

        def diagonal(cur):
            weighted_sum(jnp.maximum(qb - 1, 0), 1 - cur)
            softmax_update(cur, True)
            weighted_sum(qb, cur)

        def two_blocks(i, carry):
            off_diagonal(2 * i, 0)
            off_diagonal(2 * i + 1, 1)
            return carry

        scores(0, 0)
        lax.fori_loop(0, qb // 2, two_blocks, 0)

        @pl.when(lax.rem(qb, 2) == 0)
        def _():
            diagonal(0)

        @pl.when(lax.rem(qb, 2) == 1)
        def _():
            off_diagonal(qb - 1, 0)
            diagonal(1)

        is_pad = (qb * t + lax.broadcasted_iota(jnp.int32, (t, 1), 0)) < ROW_PAD
        for j in heads:
            hs = slice(j * HEAD, (j + 1) * HEAD)
            l = jnp.sum(l_ref[j], axis=-1, keepdims=True)
            o = acc_ref[j, 0:t, :] / l
            o_ref[:, hs] = o
            olo_ref[:, hs] = acc_ref[j, t:2 * t, :] / l
            z = z_ref[:, hs]
            og_ref[:, hs] = (o * (z * _sigmoid(z))).astype(og_ref.dtype)
            extra = q_ref[:, j * WIDE + HEAD:(j + 1) * WIDE].astype(F32)
            f_scaled = extra[:, 0:1] + extra[:, 1:2] + extra[:, 2:3]
            log_term = jnp.where(is_pad, MASK_VALUE * INV_SCALE, f_scaled - (m_ref[j] + jnp.log(l)) * INV_SCALE)
            qb_ref[:, j * WIDE:j * WIDE + HEAD] = q_ref[:, j * WIDE:j * WIDE + HEAD]
            qb_ref[:, j * WIDE + HEAD:(j + 1) * WIDE] = _extra_cols(t, _split3(log_term), None).astype(qb_ref.dtype)

    scratch = [pltpu.VMEM((2, PAIR, t, t), F32), pltpu.VMEM((2, PAIR, 2 * t, t), MXU), pltpu.VMEM((PAIR, t, 1), F32),
               pltpu.VMEM((2, PAIR, t, 1), F32), pltpu.VMEM((PAIR, t, 128), F32), pltpu.VMEM((PAIR, 2 * t, HEAD), F32)]
    tile = pl.BlockSpec((t, PAIR * HEAD), lambda g, i: (i, g))
    wide_tile = pl.BlockSpec((t, PAIR * WIDE), lambda g, i: (i, g))
    wide_all = pl.BlockSpec((lp, PAIR * WIDE), lambda g, i: (0, g))
    return pl.pallas_call(
        body,
        name=name,
        grid=(nh // PAIR, nq),
        in_specs=[wide_tile, wide_all, wide_all, pl.BlockSpec((None, t, PAIR * HEAD), lambda g, i: (1, i, g))],
        out_specs=[tile, tile, tile, wide_tile],
        out_shape=[jax.ShapeDtypeStruct((lp, d), F32), jax.ShapeDtypeStruct((lp, d), F32),
                   jax.ShapeDtypeStruct((lp, d), MXU), jax.ShapeDtypeStruct((lp, nh * WIDE), MXU)],
        scratch_shapes=scratch,
        compiler_params=_cparams(("parallel", "arbitrary"), 48),
    )(qa, ka, va, ub)


def _fox_bwd(qb, ka, va, doa, ukv, g_k, ub, g_q, name="fox_bwd", t=384, sides=()):
    lp = qb.shape[0]
    d = ukv.shape[2]
    t = _tile(lp, t)
    nh, nk = d // HEAD, lp // t
    scale = HEAD ** -0.5

    assert nh % PAIR == 0
    heads = range(PAIR)

    def body(q_ref, do_ref, k_ref, v_ref, kraw_ref, gk_ref, qraw_ref, gq_ref, dqraw_ref, dgq_ref, dukv_ref, dgk_ref,
             dfk_ref, s_ref, dp_ref, p_ref, ds_ref, col_ref, dk_ref, dv_ref, dq_ref):
        kb = pl.program_id(1)

        @pl.when(kb == 0)
        def _():
            dq_ref[...] = jnp.zeros_like(dq_ref)
            dgk_ref[...] = jnp.zeros_like(dgk_ref)

        dk_ref[...] = jnp.zeros_like(dk_ref)
        dv_ref[...] = jnp.zeros_like(dv_ref)
        col_ref[...] = jnp.zeros_like(col_ref)

        def step(qb, diagonal):
            qs = pl.ds(pl.multiple_of(qb * t, t), t)
            for j in heads:
                ws = slice(j * WIDE, (j + 1) * WIDE)
                s_ref[j] = _dot(q_ref[qs, ws], k_ref[:, ws], NT)
                dp_ref[j] = _dot(do_ref[qs, ws], v_ref[:, ws], NT)
            for j in heads:
                for r in range(0, t, STRIP):
                    rs = slice(r, r + STRIP)
                    x = s_ref[j, rs, :] * scale
                    if diagonal:
                        x = jnp.where(_strip_causal(r, t), x, MASK_VALUE)
                    p = jnp.exp(x)
                    ds = p * dp_ref[j, rs, :]
                    p_ref[j, rs, :] = p.astype(MXU)
                    ds_ref[j, rs, :] = (ds * scale).astype(MXU)
                    col_ref[j] += ds
            for j in heads:
                hs = slice(j * HEAD, (j + 1) * HEAD)
                narrow = slice(j * WIDE, j * WIDE + HEAD)
                dsb = ds_ref[j]
                dv_ref[:, hs] += _dot(p_ref[j], do_ref[qs, narrow], TN)
                dq_ref[qs, hs] += _dot(dsb, k_ref[:, narrow])
                dk_ref[:, hs] += _dot(dsb, q_ref[qs, narrow], TN)

        def off_diagonal(qb, carry):
            step(qb, False)
            return carry

        step(kb, True)
        lax.fori_loop(kb + 1, nk, off_diagonal, 0)
        for j in heads:
            hs = slice(j * HEAD, (j + 1) * HEAD)
            dfk_ref[j] = -jnp.sum(col_ref[j], axis=0, keepdims=True)
            dx, dg = _head_rms_bwd_tile(kraw_ref[:, hs], gk_ref[:, hs], dk_ref[:, hs])
            dukv_ref[0, :, hs] = dx.astype(dukv_ref.dtype)
            dukv_ref[1, :, hs] = dv_ref[:, hs].astype(dukv_ref.dtype)
            dgk_ref[:, hs] += dg

        @pl.when(kb == nk - 1)
        def _():
            dgq_ref[...] = jnp.zeros_like(dgq_ref)

            def rows(c, carry):
                rs = pl.ds(pl.multiple_of(c * t, t), t)
                for j in heads:
                    hs = slice(j * HEAD, (j + 1) * HEAD)
                    dx, dg = _head_rms_bwd_tile(qraw_ref[rs, hs], gq_ref[:, hs], dq_ref[rs, hs])
                    dqraw_ref[rs, hs] = dx.astype(dqraw_ref.dtype)
                    dgq_ref[:, hs] += dg
                return carry

            lax.fori_loop(0, nk, rows, 0)

    scratch = [pltpu.VMEM((PAIR, t, t), F32), pltpu.VMEM((PAIR, t, t), F32), pltpu.VMEM((PAIR, t, t), MXU),
               pltpu.VMEM((PAIR, t, t), MXU), pltpu.VMEM((PAIR, STRIP, t), F32),
               pltpu.VMEM((t, PAIR * HEAD), F32), pltpu.VMEM((t, PAIR * HEAD), F32), pltpu.VMEM((lp, PAIR * HEAD), F32)]
    whole = pl.BlockSpec((lp, PAIR * HEAD), lambda g, j: (0, g))
    wide_all = pl.BlockSpec((lp, PAIR * WIDE), lambda g, j: (0, g))
    wide_tile = pl.BlockSpec((t, PAIR * WIDE), lambda g, j: (j, g))
    vec = pl.BlockSpec((1, PAIR * HEAD), lambda g, j: (0, g))
    outs, side_results = _pallas(
        body,
        name=name,
        args=(qb, doa, ka, va, ukv, g_k, ub, g_q),
        grid=(nh // PAIR, nk),
        in_specs=[wide_all, wide_all, wide_tile, wide_tile,
                  pl.BlockSpec((None, t, PAIR * HEAD), lambda g, j: (0, j, g)), vec,
                  pl.BlockSpec((None, lp, PAIR * HEAD), lambda g, j: (0, 0, g)), vec],
        out_specs=[whole, vec, pl.BlockSpec((2, t, PAIR * HEAD), lambda g, j: (0, j, g)), vec,
                   pl.BlockSpec((PAIR, 1, t), lambda g, j: (g, 0, j))],
        out_shape=[jax.ShapeDtypeStruct((lp, d), MXU), jax.ShapeDtypeStruct((1, d), F32),
                   jax.ShapeDtypeStruct((2, lp, d), MXU), jax.ShapeDtypeStruct((1, d), F32),
                   jax.ShapeDtypeStruct((nh, 1, lp), F32)],
        scratch_shapes=scratch,
        semantics=("parallel", "arbitrary"),
        vmem_mib=48,
        sides=sides,
    )
    return (*outs, side_results) if sides else tuple(outs)


def _fox_do(d_og, o, o_lo, ub, name="fox_do", tm=384):
    lp, d = o.shape
    tm = _tile(lp, tm)
    nh = d // HEAD

    def body(dog_ref, o_ref, olo_ref, z_ref, doa_ref):
        for h in range(nh):
            hs = slice(h * HEAD, (h + 1) * HEAD)
            z = z_ref[:, hs]
            do = (dog_ref[:, hs] * (z * _sigmoid(z))).astype(doa_ref.dtype)
            delta = jnp.sum(do.astype(F32) * (o_ref[:, hs] + olo_ref[:, hs]), axis=-1, keepdims=True)
            doa_ref[:, h * WIDE:h * WIDE + HEAD] = do
            doa_ref[:, h * WIDE + HEAD:(h + 1) * WIDE] = _extra_cols(tm, _split3(-delta), (0.0, 0.0, 0.0)).astype(
                doa_ref.dtype)

    row = pl.BlockSpec((tm, d), lambda i: (i, 0))
    return pl.pallas_call(
        body,
        name=name,
        grid=(lp // tm,),
        in_specs=[row, row, row, pl.BlockSpec((None, tm, d), lambda i: (1, i, 0))],
        out_specs=pl.BlockSpec((tm, nh * WIDE), lambda i: (i, 0)),
        out_shape=jax.ShapeDtypeStruct((lp, nh * WIDE), MXU),
        compiler_params=_cparams(("parallel",), 56),
    )(d_og, o, o_lo, ub)


def _fox_gate_bwd(ub, d_og, o, dq, name="fox_gate_bwd", tm=384):
    _, lp, d = ub.shape
    tm = _tile(lp, tm)

    def body(z_ref, dog_ref, o_ref, dq_ref, dub_ref):
        z = z_ref[...]
        sz = _sigmoid(z)
        dub_ref[0] = dq_ref[...]
        dub_ref[1] = (dog_ref[...] * o_ref[...] * (sz * (1.0 + z * (1.0 - sz)))).astype(dub_ref.dtype)

    row = pl.BlockSpec((tm, d), lambda i: (i, 0))
    return pl.pallas_call(
        body,
        name=name,
        grid=(lp // tm,),
        in_specs=[pl.BlockSpec((None, tm, d), lambda i: (1, i, 0)), row, row, row],
        out_specs=pl.BlockSpec((2, tm, d), lambda i: (0, i, 0)),
        out_shape=jax.ShapeDtypeStruct((2, lp, d), MXU),
        compiler_params=_cparams(("parallel",), 48),
    )(ub, d_og, o, dq)


def _loss(h, target, name="loss_head"):
    lp, d = h.shape
    nb = lp // 128

    def body(h_ref, t_ref, loss_ref, dh_ref, acc_ref):
        i = pl.program_id(0)

        @pl.when(i == 0)
        def _():
            acc_ref[...] = jnp.zeros_like(acc_ref)
            dh_ref[...] = jnp.zeros_like(dh_ref)

        @pl.when(i > 0)
        def _():
            err = h_ref[...] - t_ref[...]
            dh_ref[...] = err * (1.0 / d)
            acc_ref[...] += jnp.sum(jnp.sum(err * err, axis=-1, keepdims=True) * (1.0 / d), axis=0, keepdims=True)

        @pl.when(i == nb - 1)
        def _():
            loss_ref[...] = 0.5 * acc_ref[...]

    return pl.pallas_call(
        body,
        name=name,
        grid=(nb,),
        in_specs=[pl.BlockSpec((128, d), lambda i: (i, 0)),
                  pl.BlockSpec((128, d), lambda i: (jnp.maximum(i - 1, 0), 0))],
        out_specs=[pl.BlockSpec((1, 1), lambda i: (0, 0)), pl.BlockSpec((128, d), lambda i: (i, 0))],
        out_shape=[jax.ShapeDtypeStruct((1, 1), F32), jax.ShapeDtypeStruct((lp, d), F32)],
        scratch_shapes=[pltpu.VMEM((1, 1), F32)],
        compiler_params=_cparams(("arbitrary",), 32),
    )(h, target)


def _adam_math(w, g, m, v):
    m = ADAM_B1 * m + (1.0 - ADAM_B1) * g
    v = ADAM_B2 * v + (1.0 - ADAM_B2) * (g * g)
    m_hat = m / (1.0 - ADAM_B1 ** ADAM_STEP)
    v_hat = v / (1.0 - ADAM_B2 ** ADAM_STEP)
    delta = -ADAM_LR * (m_hat / (jnp.sqrt(v_hat) + ADAM_EPS) + ADAM_WD * w)
    return delta, m, v


def _adamw(parts, w, m, v, name, tm=512):
    n, r, c = parts.shape
    if r % tm == 0:
        grid, blk, at = r // tm, (tm, c), lambda i: (i, 0)
    else:
        tc = _tile(c, tm)
        grid, blk, at = c // tc, (r, tc), lambda i: (0, i)

    def body(p_ref, w_ref, m_ref, v_ref, g_ref, d_ref, nm_ref, nv_ref):
        g = p_ref[0].astype(F32)
        for j in range(1, n):
            g = g + p_ref[j].astype(F32)
        g_ref[...] = g
        d_ref[...], nm_ref[...], nv_ref[...] = _adam_math(w_ref[...], g, m_ref[...], v_ref[...])

    row = pl.BlockSpec(blk, at)
    return pl.pallas_call(
        body,
        name=name,
        grid=(grid,),
        in_specs=[pl.BlockSpec((n,) + blk, lambda i: (0,) + at(i)), row, row, row],
        out_specs=[row] * 4,
        out_shape=[jax.ShapeDtypeStruct((r, c), F32)] * 4,
        compiler_params=_cparams(("parallel",), 48),
    )(parts, w, m, v)


def _adamw_small(gs, ws, ms, vs, name="adamw_small"):
    n = len(ws)
    flat = [jnp.reshape(a, (-1, a.shape[-1])) for group in (gs, ws, ms, vs) for a in group]

    def body(*refs):
        g_refs, w_refs, m_refs, v_refs = (refs[k * n:(k + 1) * n] for k in range(4))
        outs = refs[4 * n:]
        for i in range(n):
            d_new, m_new, v_new = _adam_math(w_refs[i][...], g_refs[i][...], m_refs[i][...], v_refs[i][...])
            outs[i][...], outs[n + i][...], outs[2 * n + i][...] = d_new, m_new, v_new

    res = pl.pallas_call(
        body, name=name, out_shape=[jax.ShapeDtypeStruct(a.shape, F32) for a in flat[n:2 * n]] * 3)(*flat)
    return [[jnp.reshape(r, w.shape) for r, w in zip(res[k * n:(k + 1) * n], ws)] for k in range(3)]


def _place():
    x, y, c = lax.axis_index("x"), lax.axis_index("y"), lax.axis_index("c")
    return x, y, c


def _other_chips(x, y):
    return [(1 - x, y), (x, 1 - y), (1 - x, 1 - y)]


def _any_spec():
    return pl.BlockSpec(memory_space=pl.ANY)


class _Side:
    def __init__(self, ins, outs, sems, start, finish, aliases=None):
        self.ins, self.outs, self.sems = list(ins), list(outs), sems
        self.start, self.finish = start, finish
        self.aliases = dict(aliases or {})


def _pallas(body, *, name, out_shape, args=(), grid=(), in_specs=(), out_specs=(), scratch_shapes=(),
            semantics=(), vmem_mib=None, sides=(), aliases=None):
    n_in, n_out, n_scr = len(args), len(out_shape), len(scratch_shapes)
    side_ins = [a for s in sides for a in s.ins]
    side_outs = [o for s in sides for o in s.outs]
    side_sems = [pltpu.SemaphoreType.DMA((max(k, 1),)) for s in sides for k in s.sems]
    aliases, in_at, out_at = dict(aliases or {}), n_in, n_out
    for s in sides:
        aliases.update({in_at + i: out_at + o for i, o in s.aliases.items()})
        in_at, out_at = in_at + len(s.ins), out_at + len(s.outs)

    def wrapped(*refs):
        at = [0]

        def take(k):
            got = refs[at[0]:at[0] + k]
            at[0] += k
            return got

        main_in = take(n_in)
        s_in = [take(len(s.ins)) for s in sides]
        main_out = take(n_out)
        s_out = [take(len(s.outs)) for s in sides]
        main_scr = take(n_scr)
        s_sem = [take(3) for s in sides]

        def run(stage):
            for s, i_, o_, m_ in zip(sides, s_in, s_out, s_sem):
                getattr(s, stage)(i_, o_, *m_)

        first = last = None
        for k, g in enumerate(grid):
            i = pl.program_id(k)
            first = (i == 0) if first is None else jnp.logical_and(first, i == 0)
            last = (i == g - 1) if last is None else jnp.logical_and(last, i == g - 1)
        if sides:
            run("start") if first is None else pl.when(first)(lambda: run("start"))
        body(*main_in, *main_out, *main_scr)
        if sides:
            run("finish") if last is None else pl.when(last)(lambda: run("finish"))

    kw = {}
    if grid:
        kw["grid"] = grid
    if semantics or vmem_mib:
        sem = tuple("arbitrary" for _ in grid) if sides else tuple(semantics)
        kw["compiler_params"] = pltpu.CompilerParams(
            dimension_semantics=sem or None, vmem_limit_bytes=vmem_mib * MIB if vmem_mib else None)
    res = pl.pallas_call(
        wrapped,
        name=name,
        in_specs=list(in_specs) + [_any_spec()] * len(side_ins),
        out_specs=list(out_specs) + [_any_spec()] * len(side_outs),
        out_shape=list(out_shape) + side_outs,
        scratch_shapes=list(scratch_shapes) + side_sems,
        input_output_aliases=aliases,
        **kw,
    )(*args, *side_ins)
    main, rest, per_side = list(res[:n_out]), list(res[n_out:]), []
    for s in sides:
        per_side.append(rest[:len(s.outs)])
        rest = rest[len(s.outs):]
    return main, per_side


def _slot(p):
    return 4 * p[0] + 2 * p[1] + p[2]


def _sibling_side(grads):
    n = len(grads)

    def copies(ins, outs, send_sems, recv_sems):
        x, y, c = _place()
        return [pltpu.make_async_remote_copy(
            src_ref=ins[t].at[2 * chip + (1 - c)], dst_ref=outs[t].at[chip],
            send_sem=send_sems.at[4 * t + chip], recv_sem=recv_sems.at[4 * t + chip],
            device_id=(x, y, 1 - c), device_id_type=MESH) for t in range(n) for chip in range(N_CHIP)]

    def start(ins, outs, send_sems, recv_sems, local_sems):
        for cp in copies(ins, outs, send_sems, recv_sems):
            cp.start()

    def finish(ins, outs, send_sems, recv_sems, local_sems):
        for cp in copies(ins, outs, send_sems, recv_sems):
            cp.wait()

    outs = [jax.ShapeDtypeStruct((N_CHIP,) + g.shape[1:], g.dtype) for g in grads]
    return _Side(grads, outs, (4 * n, 4 * n, 0), start, finish)


def _chips_side(partials):
    n = len(partials)

    def copies(ins, outs, send_sems, recv_sems, local_sems):
        x, y, c = _place()
        my_chip = 2 * x + y
        local = [pltpu.make_async_copy(ins[t].at[my_chip], outs[t].at[my_chip], local_sems.at[t]) for t in range(n)]
        sends, recvs = [], []
        for t in range(n):
            for j, chip in enumerate(_other_chips(x, y)):
                their = 2 * chip[0] + chip[1]
                sems = dict(send_sem=send_sems.at[3 * t + j], recv_sem=recv_sems.at[3 * t + j],
                            device_id=(*chip, c), device_id_type=MESH)
                sends.append(pltpu.make_async_remote_copy(src_ref=ins[t].at[their], dst_ref=outs[t].at[my_chip], **sems))
                recvs.append(pltpu.make_async_remote_copy(src_ref=ins[t].at[my_chip], dst_ref=outs[t].at[their], **sems))
        return local, sends, recvs

    def start(ins, outs, send_sems, recv_sems, local_sems):
        local, sends, _ = copies(ins, outs, send_sems, recv_sems, local_sems)
        for cp in local + sends:
            cp.start()

    def finish(ins, outs, send_sems, recv_sems, local_sems):
        local, sends, recvs = copies(ins, outs, send_sems, recv_sems, local_sems)
        for cp in sends:
            cp.wait_send()
        for cp in recvs:
            cp.wait_recv()
        for cp in local:
            cp.wait()

    outs = [jax.ShapeDtypeStruct(p.shape, p.dtype) for p in partials]
    return _Side(partials, outs, (3 * n, 3 * n, n), start, finish)


def _gather_own_side(blocks):
    n = len(blocks)

    def copies(ins, outs, send_sems, recv_sems, local_sems):
        x, y, c = _place()
        me = (x, y, c)
        peers = [(x, y, 1 - c)] + [(*chip, c) for chip in _other_chips(x, y)]
        local = [pltpu.make_async_copy(ins[t], outs[t].at[_slot(me)], local_sems.at[t]) for t in range(n)]
        sends, recvs = [], []
        for t in range(n):
            for k, peer in enumerate(peers):
                sems = dict(send_sem=send_sems.at[4 * t + k], recv_sem=recv_sems.at[4 * t + k],
                            device_id=peer, device_id_type=MESH)
                sends.append(pltpu.make_async_remote_copy(src_ref=ins[t], dst_ref=outs[t].at[_slot(me)], **sems))
                recvs.append(pltpu.make_async_remote_copy(src_ref=ins[t], dst_ref=outs[t].at[_slot(peer)], **sems))
        return local, sends, recvs

    def start(ins, outs, send_sems, recv_sems, local_sems):
        local, sends, _ = copies(ins, outs, send_sems, recv_sems, local_sems)
        for cp in local + sends:
            cp.start()

    def finish(ins, outs, send_sems, recv_sems, local_sems):
        local, sends, recvs = copies(ins, outs, send_sems, recv_sems, local_sems)
        for cp in sends:
            cp.wait_send()
        for cp in recvs:
            cp.wait_recv()
        for cp in local:
            cp.wait()

    outs = [jax.ShapeDtypeStruct((N_DEV,) + b.shape, b.dtype) for b in blocks]
    return _Side(blocks, outs, (4 * n, 4 * n, n), start, finish)


def _gather_pass_side(gathered):
    n = len(gathered)

    def copies(outs, send_sems, recv_sems):
        x, y, c = _place()
        sends, recvs = [], []
        for t in range(n):
            for j, chip in enumerate(_other_chips(x, y)):
                sems = dict(send_sem=send_sems.at[3 * t + j], recv_sem=recv_sems.at[3 * t + j],
                            device_id=(x, y, 1 - c), device_id_type=MESH)
                mine, theirs = outs[t].at[_slot((*chip, c))], outs[t].at[_slot((*chip, 1 - c))]
                sends.append(pltpu.make_async_remote_copy(src_ref=mine, dst_ref=mine, **sems))
                recvs.append(pltpu.make_async_remote_copy(src_ref=mine, dst_ref=theirs, **sems))
        return sends, recvs

    def start(ins, outs, send_sems, recv_sems, local_sems):
        for cp in copies(outs, send_sems, recv_sems)[0]:
            cp.start()

    def finish(ins, outs, send_sems, recv_sems, local_sems):
        sends, recvs = copies(outs, send_sems, recv_sems)
        for cp in sends:
            cp.wait_send()
        for cp in recvs:
            cp.wait_recv()

    outs = [jax.ShapeDtypeStruct(g.shape, g.dtype) for g in gathered]
    return _Side(gathered, outs, (3 * n, 3 * n, 0), start, finish, aliases={t: t for t in range(n)})


def _alone(side, name):
    return _pallas(lambda: None, name=name, out_shape=[], sides=[side])[1][0]


def _gather_side(blocks):
    n = len(blocks)

    def parts(ins, outs, send_sems, recv_sems, local_sems):
        x, y, c = _place()
        me, sibling = (x, y, c), (x, y, 1 - c)

        def copy(t, k, block, to, src=None):
            dst = outs[t].at[_slot(block)]
            return pltpu.make_async_remote_copy(
                src_ref=dst if src is None else src, dst_ref=dst,
                send_sem=send_sems.at[7 * t + k], recv_sem=recv_sems.at[7 * t + k],
                device_id=to, device_id_type=MESH)

        local = [pltpu.make_async_copy(ins[t], outs[t].at[_slot(me)], local_sems.at[t]) for t in range(n)]
        own = [copy(t, 0, me, sibling, src=ins[t]) for t in range(n)]
        own += [copy(t, 1 + j, me, (*chip, c), src=ins[t]) for t in range(n) for j, chip in enumerate(_other_chips(x, y))]
        return me, sibling, c, _other_chips(x, y), copy, local, own

    def start(ins, outs, send_sems, recv_sems, local_sems):
        *_, local, own = parts(ins, outs, send_sems, recv_sems, local_sems)
        for cp in local + own:
            cp.start()

    def finish(ins, outs, send_sems, recv_sems, local_sems):
        me, sibling, c, chips, copy, local, own = parts(ins, outs, send_sems, recv_sems, local_sems)
        passed = []
        for t in range(n):
            for j, chip in enumerate(chips):
                copy(t, 1 + j, (*chip, c), me).wait_recv()
                passed.append(copy(t, 4 + j, (*chip, c), sibling))
                passed[-1].start()
        for t in range(n):
            copy(t, 0, sibling, me).wait_recv()
            for j, chip in enumerate(chips):
                copy(t, 4 + j, (*chip, 1 - c), me).wait_recv()
        for cp in own + passed:
            cp.wait_send()
        for cp in local:
            cp.wait()

    outs = [jax.ShapeDtypeStruct((N_DEV,) + b.shape, b.dtype) for b in blocks]
    return _Side(blocks, outs, (7 * n, 7 * n, n), start, finish)


def _all_gather(blocks, name):
    return _alone(_gather_side(blocks), name)


def _pair_sum(g8, got, name, tm=1024):
    _, r, c = g8.shape
    tm = tm if r % tm == 0 else r
    core = lax.axis_index("c")

    def body(core_ref, mine_ref, got_ref, o_ref):
        south_first = core_ref[0] == 0
        a, b = mine_ref[...], got_ref[...]
        o_ref[...] = (jnp.where(south_first, a, b) + jnp.where(south_first, b, a)).astype(o_ref.dtype)

    return pl.pallas_call(
        body,
        name=name,
        grid_spec=pltpu.PrefetchScalarGridSpec(
            num_scalar_prefetch=1,
            grid=(N_CHIP, r // tm),
            in_specs=[pl.BlockSpec((None, tm, c), lambda j, i, core_ref: (2 * j + core_ref[0], i, 0)),
                      pl.BlockSpec((None, tm, c), lambda j, i, core_ref: (j, i, 0))],
            out_specs=pl.BlockSpec((None, tm, c), lambda j, i, core_ref: (j, i, 0)),
        ),
        out_shape=jax.ShapeDtypeStruct((N_CHIP, r, c), PAYLOAD),
        compiler_params=_cparams(("parallel", "parallel"), 32),
    )(jnp.reshape(core, (1,)).astype(jnp.int32), g8, got)


def _all_reduce_small(s, name="small_all_reduce"):
    r, c = s.shape

    def body(s_ref, o_ref, buf_ref, send_sems, recv_sems):
        x, y, c_ = _place()
        me = 4 * x + 2 * y + c_
        buf_ref[me] = s_ref[...]

        def copy(k, slot, peer):
            return pltpu.make_async_remote_copy(
                src_ref=s_ref, dst_ref=buf_ref.at[slot],
                send_sem=send_sems.at[k - 1], recv_sem=recv_sems.at[k - 1],
                device_id=peer, device_id_type=MESH)

        peers = []
        for k in range(1, N_DEV):
            peer = (x ^ ((k >> 2) & 1), y ^ ((k >> 1) & 1), c_ ^ (k & 1))
            peers.append(peer)
            copy(k, me, peer).start()
        for k, peer in zip(range(1, N_DEV), peers):
            cp = copy(k, 4 * peer[0] + 2 * peer[1] + peer[2], peer)
            cp.wait_send()
            cp.wait_recv()
        total = buf_ref[0]
        for j in range(1, N_DEV):
            total = total + buf_ref[j]
        o_ref[...] = total

    return pl.pallas_call(
        body,
        name=name,
        out_shape=jax.ShapeDtypeStruct((r, c), F32),
        in_specs=[pl.BlockSpec(memory_space=pltpu.VMEM)],
        out_specs=pl.BlockSpec(memory_space=pltpu.VMEM),
        scratch_shapes=[pltpu.VMEM((N_DEV, r, c), F32), pltpu.SemaphoreType.DMA((N_DEV - 1,)),
                        pltpu.SemaphoreType.DMA((N_DEV - 1,))],
    )(s)


def _kv_weights(g_kv):
    d = g_kv.shape[1]
    nh = d // HEAD
    wkv = jnp.reshape(jnp.transpose(g_kv, (1, 0, 2)), (d, -1))
    return wkv[:, :2 * d], jnp.pad(wkv[:, 2 * d:], ((0, 0), (0, 128 - nh)))


def _kv_grad_blocks(dwkv2_t, dwfl_t):
    d = dwkv2_t.shape[1]
    return jnp.reshape(jnp.concatenate([dwkv2_t, dwfl_t[:d // HEAD]], axis=0), (N_DEV, -1, d))


def _local_step(x, target, meta, gamma, a_norm, wa_in, a_out_norm, wa_out, kv_norm, late, b_f, g_k, b_norm, g_q,
                dist):
    d = x.shape[1]
    nh = d // HEAD
    cols = d // N_DEV
    big = dict(tm=1408, tn=512, tk=2048)

    lb = _lb_fwd(gamma)
    if dist:
        h0, hn_a, ((wa_in,),) = _embed(x, meta, a_norm, sides=[_gather_side([wa_in])])
    else:
        h0, hn_a = _embed(x, meta, a_norm)
    lp = h0.shape[0]
    if dist:
        a_out_blk, kv_blk, b_in_blk, b_out_blk = late
        u4, (first,) = _matmul(hn_a, wa_in, "nn", F32, "a_in", out_parts=4,
                               sides=[_gather_own_side([a_out_blk, b_in_blk])], **big)
        o_a, og_a, ((g_a_out, g_b_in), (g_kv,)) = _hgrn2_fwd(
            u4, lb, a_out_norm, sides=[_gather_pass_side(first), _gather_own_side([kv_blk])])
        wa_out = jnp.reshape(g_a_out, (d, d))
        h1, ((g_kv,), (g_b_out,)) = _matmul(og_a, wa_out, "nn", F32, "a_out", add=h0,
                                            sides=[_gather_pass_side([g_kv]), _gather_own_side([b_out_blk])], **big)
        (wkv2, wfl), wb_in = _kv_weights(g_kv), g_b_in
    else:
        u4 = _matmul(hn_a, wa_in, "nn", F32, "a_in", out_parts=4, **big)
        o_a, og_a = _hgrn2_fwd(u4, lb, a_out_norm)
        h1 = _matmul(og_a, wa_out, "nn", F32, "a_out", add=h0, **big)
        wkv2, wfl, wb_in, wb_out = late
    hk, hb = _rms_fwd(h1, [kv_norm, b_norm], "rms_kv_b")
    if dist:
        ukv, ((g_b_out,),) = _matmul(hk, wkv2, "nn", F32, "kv_in", out_parts=2,
                                     sides=[_gather_pass_side([g_b_out])], **big)
        wb_out = jnp.reshape(g_b_out, (d, d))
    else:
        ukv = _matmul(hk, wkv2, "nn", F32, "kv_in", out_parts=2, **big)
    ufl = _matmul(hk, wfl, "nn", F32, "kv_f", **big)
    ub = _matmul(hb, wb_in, "nn", F32, "b_in", out_parts=2, **big)
    f_cum = _fgate_fwd(ufl, b_f)
    qa, ka, va = _attn_operands(ub, ukv, g_q, g_k, f_cum)
    o_b, o_lo, og_b, qb = _fox_fwd(qa, ka, va, ub)
    h2 = _matmul(og_b, wb_out, "nn", F32, "b_out", add=h1, **big)
    loss, dh2 = _loss(h2, target)

    dx_t = dict(tm=1408, tn=512, tk=2048, vmem_mib=56)
    dx_parts_t = dict(tm=704, tn=512, tk=2048, vmem_mib=58, k_whole=True)
    dw_t = dict(tm=512, tn=1024, tk=lp, vmem_mib=58)
    dw_f32_t = dict(tm=1024, tn=512, tk=lp, vmem_mib=58)
    def to_sibling(g8):
        return [_sibling_side([g8])] if dist else []

    def to_chips(partial):
        return [_chips_side([partial])] if dist else []

    def unpack(res, n_sides):
        if not dist:
            return res, [None] * n_sides
        *main, side_results = res
        return (main[0] if len(main) == 1 else tuple(main)), [r[0] for r in side_results]

    dwb_out = _matmul(og_b, dh2, "tn", F32, "b_out_dw", **dw_f32_t)
    g8_b_out = jnp.reshape(dwb_out, (N_DEV, cols, d))
    d_ogb, (got,) = unpack(_matmul(dh2, wb_out, "nt", F32, "b_out_dx", sides=to_sibling(g8_b_out), **dx_t), 1)
    p_b_out = _pair_sum(g8_b_out, got, "pair_sum_b_w_out") if dist else None
    doa = _fox_do(d_ogb, o_b, o_lo, ub)
    (dq, dg_q, dukv, dg_k, dfk), (r_b_out,) = unpack(
        _fox_bwd(qb, ka, va, doa, ukv, g_k, ub, g_q, sides=to_chips(p_b_out)), 1)
    dub = _fox_gate_bwd(ub, d_ogb, o_b, dq)
    d_f = jnp.pad(jnp.transpose(dfk[:, 0, :]), ((0, 0), (0, 128 - nh)))
    dufl, db_f = _fgate_bwd(ufl, b_f, d_f)
    dwb_in = _matmul(hb, dub, "tn", F32, "b_in_dw", out_parts=N_DEV, **dw_t)
    d_hb, (got,) = unpack(_matmul(dub, wb_in, "nt", F32, "b_in_dx", sides=to_sibling(dwb_in), **dx_parts_t), 1)
    p_b_in = _pair_sum(dwb_in, got, "pair_sum_b_w_in") if dist else None
    d_hk, (r_b_in,) = unpack(_matmul(dukv, wkv2, "nt", F32, "kv_dx", sides=to_chips(p_b_in), **dx_parts_t), 1)
    d_hk = _matmul(dufl, wfl, "nt", F32, "kv_f_dx", add=d_hk, **dx_t)
    dwkv2_t = _matmul(dukv, hk, "tn", F32, "kv_dw", **dw_t)
    dwfl_t = _matmul(dufl, hk, "tn", F32, "kv_f_dw", **dw_t)
    g8_kv = _kv_grad_blocks(dwkv2_t, dwfl_t) if dist else None
    dh1, (dg_kv, dg_b) = _rms_bwd(h1, [kv_norm, b_norm], [d_hk, d_hb], dh2, "rms_kv_b_bwd")
    d_oga, (got,) = unpack(_matmul(dh1, wa_out, "nt", F32, "a_out_dx", sides=to_sibling(g8_kv), **dx_t), 1)
    p_kv = _pair_sum(g8_kv, got, "pair_sum_kv_w") if dist else None
    dwa_out = _matmul(og_a, dh1, "tn", F32, "a_out_dw", **dw_f32_t)
    g8_a_out = jnp.reshape(dwa_out, (N_DEV, cols, d))
    (du4, dlb, dg_aout), (r_kv, got) = unpack(
        _hgrn2_bwd(u4, o_a, d_oga, lb, a_out_norm, sides=to_chips(p_kv) + to_sibling(g8_a_out)), 2)
    p_a_out = _pair_sum(g8_a_out, got, "pair_sum_a_w_out") if dist else None
    dwa_in, (r_a_out,) = unpack(
        _matmul(hn_a, du4, "tn", F32, "a_in_dw", out_parts=N_DEV, sides=to_chips(p_a_out), **dw_t), 1)
    row_tiles = lp // _tile(lp, dx_parts_t["tm"], 64)
    if dist and row_tiles >= 2:
        top = max(1, row_tiles // 6)
        d_hna, ((got,),) = _matmul(du4, wa_in, "nt", F32, "a_in_dx_top", rows=(0, top),
                                   sides=to_sibling(dwa_in), **dx_parts_t)
        p_a_in = _pair_sum(dwa_in, got, "pair_sum_a_w_in")
        d_hna, ((r_a_in,),) = _matmul(du4, wa_in, "nt", F32, "a_in_dx_bottom", rows=(top, row_tiles - top), into=d_hna,
                                      sides=to_chips(p_a_in), **dx_parts_t)
    else:
        d_hna, (got,) = unpack(_matmul(du4, wa_in, "nt", F32, "a_in_dx", sides=to_sibling(dwa_in), **dx_parts_t), 1)
        if dist:
            (r_a_in,) = _alone(_chips_side([_pair_sum(dwa_in, got, "pair_sum_a_w_in")]), "grads_to_chips_a_w_in")
    grad_x, dmeta, dg_a = _embed_bwd(h0, a_norm, d_hna, dh1)
    dgamma = _lb_bwd(gamma, dlb)

    grads = dict(meta=dmeta, gamma=dgamma, a_norm=dg_a, a_out_norm=dg_aout, kv_norm=dg_kv,
                 b_f=db_f, g_k=dg_k, b_norm=dg_b, g_q=dg_q)
    if dist:
        grads.update(wa_in=r_a_in, wa_out=r_a_out, wkv=r_kv, wb_in=r_b_in, wb_out=r_b_out)
    else:
        grads.update(wa_in=dwa_in, wa_out=dwa_out, wkv2=dwkv2_t.T, wfl=dwfl_t.T, wb_in=dwb_in, wb_out=dwb_out)
    return loss, grad_x, grads


def kernel(x, meta, gamma_lb, a_norm, a_w_in, a_out_norm, a_w_out, kv_norm, kv_w, fox_b_f, fox_k_norm, b_norm, b_w_in, b_q_norm, b_w_out, loss_target, m_meta, m_gamma_lb, m_a_norm, m_a_w_in, m_a_out_norm, m_a_w_out, m_kv_norm, m_kv_w, m_fox_b_f, m_fox_k_norm, m_b_norm, m_b_w_in, m_b_q_norm, m_b_w_out, v_meta, v_gamma_lb, v_a_norm, v_a_w_in, v_a_out_norm, v_a_w_out, v_kv_norm, v_kv_w, v_fox_b_f, v_fox_k_norm, v_b_norm, v_b_w_in, v_b_q_norm, v_b_w_out):
    d = x.shape[-1]
    nh = d // HEAD
    cols = d // N_DEV
    me = 4 * lax.axis_index("x") + 2 * lax.axis_index("y") + lax.axis_index("c")

    sharded_small = jnp.concatenate([meta, gamma_lb, a_norm, a_out_norm, jnp.zeros((4, cols), F32)], axis=0)
    (g_small,) = _all_gather([sharded_small], "gather_small")
    g_a_in, wa_out = a_w_in[0].astype(MXU), None
    late = (a_w_out[0].astype(MXU), kv_w.astype(MXU), b_w_in[0].astype(MXU), b_w_out[0].astype(MXU))
    small = jnp.reshape(jnp.transpose(g_small, (1, 0, 2)), (-1, d))
    meta_f, gamma_f, a_norm_f, a_out_norm_f = small[:16], small[16:18], small[18:19], small[19:20]
    b_f = jnp.pad(jnp.reshape(fox_b_f, (1, nh)), ((0, 0), (0, 128 - nh)))
    g_k = jnp.reshape(fox_k_norm, (1, d))
    g_q = jnp.reshape(b_q_norm, (1, d))
    kv_norm_r = jnp.reshape(kv_norm, (1, d))

    loss, grad_x, g = _local_step(x[0], loss_target[0], meta_f, gamma_f, a_norm_f, g_a_in, a_out_norm_f, wa_out,
                                  kv_norm_r, late, b_f, g_k, b_norm, g_q, dist=True)
    loss = lax.psum(loss[0, 0], AXES)

    r_a_in = _adamw(g["wa_in"], a_w_in[0], m_a_w_in[0], v_a_w_in[0], "adamw_a_w_in")
    r_a_out = _adamw(g["wa_out"], a_w_out[0], m_a_w_out[0], v_a_w_out[0], "adamw_a_w_out")
    r_kv = [jnp.transpose(r) for r in _adamw(g["wkv"], kv_w.T, m_kv_w.T, v_kv_w.T, "adamw_kv_w")]
    r_b_in = _adamw(g["wb_in"], b_w_in[0], m_b_w_in[0], v_b_w_in[0], "adamw_b_w_in")
    r_b_out = _adamw(g["wb_out"], b_w_out[0], m_b_w_out[0], v_b_w_out[0], "adamw_b_w_out")

    packed = jnp.concatenate(
        [g["meta"], g["gamma"], g["a_norm"], g["a_out_norm"], g["kv_norm"], g["b_norm"], g["g_k"], g["g_q"],
         jnp.pad(g["b_f"], ((0, 0), (0, d - 128))), jnp.zeros((7, d), F32)], axis=0)
    tot = _all_reduce_small(packed)
    mine = lax.dynamic_slice_in_dim(tot[:20], me * cols, cols, axis=1)
    gs = [mine[:16], mine[16:18], mine[18:19], mine[19:20], tot[20], tot[21:22], jnp.reshape(tot[22], (nh, HEAD)),
          jnp.reshape(tot[23], (1, nh, HEAD)), tot[24, :nh]]
    small_w = [meta, gamma_lb, a_norm, a_out_norm, kv_norm, b_norm, fox_k_norm, b_q_norm, fox_b_f]
    small_m = [m_meta, m_gamma_lb, m_a_norm, m_a_out_norm, m_kv_norm, m_b_norm, m_fox_k_norm, m_b_q_norm, m_fox_b_f]
    small_v = [v_meta, v_gamma_lb, v_a_norm, v_a_out_norm, v_kv_norm, v_b_norm, v_fox_k_norm, v_b_q_norm, v_fox_b_f]

    d_s, m_s, v_s = _adamw_small(gs, small_w, small_m, small_v)

    def ordered(s, a_in, a_out, kv, b_in, b_out):
        return [s[0], s[1], s[2], a_in[None], s[3], a_out[None], s[4], kv, s[8], s[6], s[5], b_in[None], s[7], b_out[None]]

    outs = []
    for i, s in enumerate([gs, d_s, m_s, v_s]):
        outs += ordered(s, r_a_in[i], r_a_out[i], r_kv[i], r_b_in[i], r_b_out[i])
    return (loss, grad_x[None], *outs)
```

```python
import math

import jax
import jax.numpy as jnp
from jax import lax
from jax.experimental import pallas as pl
from jax.experimental.pallas import tpu as pltpu

HEAD = 128
CHUNK = 64
N_META = 16
ROW_PAD = 128 - N_META
EPS = 1e-6
MASK_VALUE = -1e30
ADAM_LR = 0.001
ADAM_B1 = 0.9
ADAM_B2 = 0.999
ADAM_EPS = 1e-08
ADAM_WD = 0.01
ADAM_STEP = 10
N_DEV = 8
N_CHIP = 4
MIB = 1024 * 1024
AXES = ("x", "y", "c")
MESH = pl.DeviceIdType.MESH

F32 = jnp.float32
MXU = jnp.bfloat16
PAYLOAD = jnp.bfloat16
HI = lax.Precision.HIGHEST

NN = (((1,), (0,)), ((), ()))
NT = (((1,), (1,)), ((), ()))
TN = (((0,), (0,)), ((), ()))


def _dot(a, b, dims=NN):
    return lax.dot_general(a.astype(MXU), b.astype(MXU), dims, preferred_element_type=F32)


def _dot_exact(a, b):
    return lax.dot_general(a, b, NN, precision=HI, preferred_element_type=F32)


def _sigmoid(x):
    return 1.0 / (1.0 + jnp.exp(-x))


def _tile(dim, target, unit=128):
    best = None
    t = unit
    while t <= min(dim, target):
        if dim % t == 0:
            best = t
        t += unit
    return best if best is not None else dim


def _cparams(semantics, vmem_mib):
    return pltpu.CompilerParams(dimension_semantics=semantics, vmem_limit_bytes=vmem_mib * MIB)


def _mat_spec(arr, br, bc, rc_of_grid):
    if arr.ndim == 2:
        return pl.BlockSpec((br, bc), rc_of_grid)
    assert arr.shape[2] % bc == 0, (arr.shape, bc)
    per = arr.shape[2] // bc

    def idx(*g):
        r, c = rc_of_grid(*g)
        return (c // per, r, c % per)

    return pl.BlockSpec((None, br, bc), idx)


def _mat_shape(arr):
    return (arr.shape[0], arr.shape[1]) if arr.ndim == 2 else (arr.shape[1], arr.shape[0] * arr.shape[2])


def _matmul(a, b, dims, out_dtype, name, *, add=None, out_parts=1, tm=512, tn=512, tk=512, vmem_mib=48, sides=(),
            k_whole=False, rows=None, into=None):
    ar, ac = _mat_shape(a)
    br_, bc_ = _mat_shape(b)
    if dims == "nn":
        m, k, n = ar, ac, bc_
        assert br_ == k
    elif dims == "nt":
        m, k, n = ar, ac, br_
        assert bc_ == k
    else:
        m, k, n = ac, ar, bc_
        assert br_ == k
    m_unit, n_unit, k_unit = m, n, k
    if a.ndim == 3:
        if dims == "tn":
            m_unit = math.gcd(m_unit, a.shape[2])
        else:
            k_unit = math.gcd(k_unit, a.shape[2])
    if b.ndim == 3:
        if dims == "nt":
            k_unit = math.gcd(k_unit, b.shape[2])
        else:
            n_unit = math.gcd(n_unit, b.shape[2])
    if out_parts > 1:
        n_unit = math.gcd(n_unit, n // out_parts)
    tm = _tile(m_unit, tm, 128 if dims == "tn" else 64)
    tn, tk = _tile(n_unit, tn), _tile(k_unit, tk)
    whole_k = dims == "nt" and tk < k and k_whole
    if whole_k:
        k_chunk, tk = tk, k
    gm, gn, gk = m // tm, n // tn, k // tk
    assert gm * tm == m and gn * tn == n and gk * tk == k, (name, m, n, k, tm, tn, tk)

    def chunk_of(ref, arr, c):
        if arr.ndim == 2:
            return ref[:, c * k_chunk:(c + 1) * k_chunk]
        per = arr.shape[2] // k_chunk
        return ref[c // per, :, (c % per) * k_chunk:(c % per + 1) * k_chunk]

    def all_cols(arr, rows, row_of_grid):
        if arr.ndim == 2:
            return pl.BlockSpec((rows, arr.shape[1]), lambda i, j, kk: (row_of_grid(i, j), 0))
        return pl.BlockSpec((arr.shape[0], rows, arr.shape[2]), lambda i, j, kk: (0, row_of_grid(i, j), 0))

    if dims == "nn":
        a_spec = _mat_spec(a, tm, tk, lambda i, j, kk: (i, kk))
        b_spec = _mat_spec(b, tk, tn, lambda i, j, kk: (kk, j))
        dn = NN
    elif whole_k:
        a_spec = all_cols(a, tm, lambda i, j: i)
        b_spec = all_cols(b, tn, lambda i, j: j)
        dn = NT
    elif dims == "nt":
        a_spec = _mat_spec(a, tm, tk, lambda i, j, kk: (i, kk))
        b_spec = _mat_spec(b, tn, tk, lambda i, j, kk: (j, kk))
        dn = NT
    else:
        a_spec = _mat_spec(a, tk, tm, lambda i, j, kk: (kk, i))
        b_spec = _mat_spec(b, tk, tn, lambda i, j, kk: (kk, j))
        dn = TN

    if out_parts > 1:
        per = (n // out_parts) // tn
        out_shape = jax.ShapeDtypeStruct((out_parts, m, n // out_parts), out_dtype)
        o_spec = pl.BlockSpec((None, tm, tn), lambda i, j, kk: (j // per, i, j % per))
    else:
        out_shape = jax.ShapeDtypeStruct((m, n), out_dtype)
        o_spec = pl.BlockSpec((tm, tn), lambda i, j, kk: (i, j))

    in_specs = [a_spec, b_spec]
    args = [a, b]
    if add is not None:
        in_specs.append(pl.BlockSpec((tm, tn), lambda i, j, kk: (i, j)))
        args.append(add)
    if rows is not None:
        first_tile, gm = rows

        def shifted(spec):
            return pl.BlockSpec(spec.block_shape, lambda i, j, kk: spec.index_map(i + first_tile, j, kk))

        in_specs = [shifted(s) for s in in_specs]
        o_spec = shifted(o_spec)
    aliases = {}
    if into is not None:
        aliases[len(args)] = 0
        in_specs.append(_any_spec())
        args.append(into)

    def body(*refs):
        a_ref, b_ref, *others, o_ref, acc_ref = refs
        add_ref = others[0] if add is not None else None
        kk = pl.program_id(2)
        if whole_k:
            part = sum(lax.dot_general(chunk_of(a_ref, a, c).astype(MXU), chunk_of(b_ref, b, c).astype(MXU), dn,
                                       preferred_element_type=F32) for c in range(k // k_chunk))
        else:
            part = lax.dot_general(a_ref[...].astype(MXU), b_ref[...].astype(MXU), dn, preferred_element_type=F32)

        def finish(total):
            if add_ref is not None:
                total = total + add_ref[...]
            o_ref[...] = total.astype(o_ref.dtype)

        if gk == 1:
            finish(part)
        else:
            @pl.when(kk == 0)
            def _():
                acc_ref[...] = part

            @pl.when(jnp.logical_and(kk > 0, kk < gk - 1))
            def _():
                acc_ref[...] += part

            @pl.when(kk == gk - 1)
            def _():
                finish(acc_ref[...] + part)

    (out,), side_results = _pallas(
        body,
        name=name,
        args=args,
        grid=(gm, gn, gk),
        in_specs=in_specs,
        out_specs=[o_spec],
        out_shape=[out_shape],
        scratch_shapes=[pltpu.VMEM((tm, tn) if gk > 1 else (8, 128), F32)],
        semantics=("parallel", "parallel", "arbitrary"),
        vmem_mib=vmem_mib,
        sides=sides,
        aliases=aliases,
    )
    return (out, side_results) if sides else out


def _rms_fwd(h, gains, name, tm=384):
    lp, d = h.shape
    tm = _tile(lp, tm)
    n = len(gains)

    def body(*refs):
        h_ref = refs[0]
        g_refs = refs[1:1 + n]
        o_refs = refs[1 + n:]
        x = h_ref[...]
        y = x * lax.rsqrt(jnp.mean(x * x, axis=-1, keepdims=True) + EPS)
        for g_ref, o_ref in zip(g_refs, o_refs):
            o_ref[...] = (y * g_ref[...]).astype(o_ref.dtype)

    row = pl.BlockSpec((tm, d), lambda i: (i, 0))
    vec = pl.BlockSpec((1, d), lambda i: (0, 0))
    return pl.pallas_call(
        body,
        name=name,
        grid=(lp // tm,),
        in_specs=[row] + [vec] * n,
        out_specs=[row] * n,
        out_shape=[jax.ShapeDtypeStruct((lp, d), MXU)] * n,
        compiler_params=_cparams(("parallel",), 40),
    )(h, *gains)


def _embed(x, meta, gain, name="embed", sides=()):
    seq, d = x.shape
    lp = ROW_PAD + N_META + seq
    blk = ROW_PAD + N_META

    def body(x_ref, meta_ref, g_ref, h_ref, o_ref):
        i = pl.program_id(0)

        @pl.when(i == 0)
        def _():
            h_ref[0:ROW_PAD, :] = jnp.zeros((ROW_PAD, d), F32)
            h_ref[ROW_PAD:blk, :] = meta_ref[...]

        @pl.when(i > 0)
        def _():
            h_ref[...] = x_ref[...]

        h = h_ref[...]
        y = h * lax.rsqrt(jnp.mean(h * h, axis=-1, keepdims=True) + EPS)
        o_ref[...] = (y * g_ref[...]).astype(o_ref.dtype)

    row = pl.BlockSpec((blk, d), lambda i: (i, 0))
    outs, side_results = _pallas(
        body,
        name=name,
        args=(x, meta, gain),
        grid=(lp // blk,),
        in_specs=[pl.BlockSpec((blk, d), lambda i: (jnp.maximum(i - 1, 0), 0)),
                  pl.BlockSpec((N_META, d), lambda i: (0, 0)), pl.BlockSpec((1, d), lambda i: (0, 0))],
        out_specs=[row, row],
        out_shape=[jax.ShapeDtypeStruct((lp, d), F32), jax.ShapeDtypeStruct((lp, d), MXU)],
        semantics=("arbitrary",),
        vmem_mib=32,
        sides=sides,
    )
    return (*outs, side_results) if sides else tuple(outs)


def _embed_bwd(h, gain, dy, res, name="embed_bwd"):
    lp, d = h.shape
    blk = ROW_PAD + N_META

    def body(h_ref, res_ref, g_ref, dy_ref, dx_ref, dmeta_ref, dg_ref):
        i = pl.program_id(0)
        x = h_ref[...]
        rstd = lax.rsqrt(jnp.mean(x * x, axis=-1, keepdims=True) + EPS)
        xhat = x * rstd
        dy = dy_ref[...]
        gdy = dy * g_ref[...]
        dh = res_ref[...] + rstd * (gdy - xhat * jnp.mean(gdy * xhat, axis=-1, keepdims=True))
        part = jnp.sum(dy * xhat, axis=0, keepdims=True)

        @pl.when(i == 0)
        def _():
            dmeta_ref[...] = dh[ROW_PAD:blk, :]
            dg_ref[...] = part

        @pl.when(i > 0)
        def _():
            dx_ref[...] = dh
            dg_ref[...] += part

    row = pl.BlockSpec((blk, d), lambda i: (i, 0))
    vec = pl.BlockSpec((1, d), lambda i: (0, 0))
    return pl.pallas_call(
        body,
        name=name,
        grid=(lp // blk,),
        in_specs=[row, row, vec, row],
        out_specs=[pl.BlockSpec((blk, d), lambda i: (jnp.maximum(i - 1, 0), 0)),
                   pl.BlockSpec((N_META, d), lambda i: (0, 0)), vec],
        out_shape=[jax.ShapeDtypeStruct((lp - blk, d), F32), jax.ShapeDtypeStruct((N_META, d), F32),
                   jax.ShapeDtypeStruct((1, d), F32)],
        compiler_params=_cparams(("arbitrary",), 32),
    )(h, res, gain, dy)


def _rms_bwd(h, gains, dys, res, name, tm=384):
    lp, d = h.shape
    tm = _tile(lp, tm)
    n = len(gains)

    def body(*refs):
        h_ref, res_ref = refs[0], refs[1]
        g_refs = refs[2:2 + n]
        dy_refs = refs[2 + n:2 + 2 * n]
        dh_ref = refs[2 + 2 * n]
        dg_refs = refs[3 + 2 * n:]
        i = pl.program_id(0)
        x = h_ref[...]
        rstd = lax.rsqrt(jnp.mean(x * x, axis=-1, keepdims=True) + EPS)
        xhat = x * rstd
        dh = res_ref[...]
        for g_ref, dy_ref, dg_ref in zip(g_refs, dy_refs, dg_refs):
            dy = dy_ref[...]
            gdy = dy * g_ref[...]
            dh = dh + rstd * (gdy - xhat * jnp.mean(gdy * xhat, axis=-1, keepdims=True))
            part = jnp.sum(dy * xhat, axis=0, keepdims=True)

            @pl.when(i == 0)
            def _():
                dg_ref[...] = part

            @pl.when(i > 0)
            def _():
                dg_ref[...] += part

        dh_ref[...] = dh

    row = pl.BlockSpec((tm, d), lambda i: (i, 0))
    vec = pl.BlockSpec((1, d), lambda i: (0, 0))
    outs = pl.pallas_call(
        body,
        name=name,
        grid=(lp // tm,),
        in_specs=[row, row] + [vec] * n + [row] * n,
        out_specs=[row] + [vec] * n,
        out_shape=[jax.ShapeDtypeStruct((lp, d), F32)] + [jax.ShapeDtypeStruct((1, d), F32)] * n,
        compiler_params=_cparams(("arbitrary",), 56),
    )(h, res, *gains, *dys)
    return outs[0], list(outs[1:])


def _lb_fwd(gamma):
    def body(g_ref, lb_ref):
        g = g_ref[...]
        e = jnp.exp(g - jnp.max(g, axis=0, keepdims=True))
        lb_ref[...] = (e / jnp.sum(e, axis=0, keepdims=True))[0:1, :]

    return pl.pallas_call(body, name="lb_fwd", out_shape=jax.ShapeDtypeStruct((1, gamma.shape[1]), F32))(gamma)


def _lb_bwd(gamma, dlb):
    def body(g_ref, dlb_ref, dg_ref):
        g = g_ref[...]
        e = jnp.exp(g - jnp.max(g, axis=0, keepdims=True))
        s = e / jnp.sum(e, axis=0, keepdims=True)
        rows = lax.broadcasted_iota(jnp.int32, g.shape, 0)
        ds = jnp.where(rows == 0, dlb_ref[...], 0.0)
        dg_ref[...] = s * (ds - jnp.sum(s * ds, axis=0, keepdims=True))

    return pl.pallas_call(body, name="lb_bwd", out_shape=jax.ShapeDtypeStruct(gamma.shape, F32))(gamma, dlb)


def _tri(n, lower):
    r = lax.broadcasted_iota(jnp.int32, (n, n), 0)
    c = lax.broadcasted_iota(jnp.int32, (n, n), 1)
    return jnp.where((r >= c) if lower else (r <= c), 1.0, 0.0).astype(F32)


def _group(nc, most):
    return max(u for u in range(1, most + 1) if nc % u == 0)


def _running_sum(tri, x):
    hi = x.astype(MXU)
    rest = x - hi.astype(F32)
    mid = rest.astype(MXU)
    lo = (rest - mid.astype(F32)).astype(MXU)
    return _dot(tri, hi) + _dot(tri, mid) + _dot(tri, lo)


def _causal(n):
    r = lax.broadcasted_iota(jnp.int32, (n, n), 0)
    c = lax.broadcasted_iota(jnp.int32, (n, n), 1)
    return r >= c


def _chunk_gates(u_ref, lb, c):
    sl = pl.ds(pl.multiple_of(c * CHUNK, CHUNK), CHUNK)
    valid = (c * CHUNK + lax.broadcasted_iota(jnp.int32, (CHUNK, HEAD), 0)) >= ROW_PAD
    uq = u_ref[0, sl, :]
    uf = u_ref[1, sl, :]
    sq = _sigmoid(uq)
    sf = _sigmoid(uf)
    fg = lb + (1.0 - lb) * sf
    return dict(sl=sl, valid=valid, uq=uq, sq=sq, sf=sf, fg=fg, q=jnp.where(valid, uq * sq, 0.0),
                logf=jnp.where(valid, jnp.log(fg), 0.0), k=jnp.where(valid, 1.0 - fg, 0.0),
                v=jnp.where(valid, u_ref[2, sl, :], 0.0))


def _chunk_decays(x, b):
    b_last = b[CHUNK - 1:CHUNK, :]
    b_mid = b[CHUNK // 2 - 1:CHUNK // 2, :]
    e_qi = jnp.exp(b - b_mid)
    e_ki = jnp.exp(b_mid - b)
    e_kd = jnp.exp(b_last - b)
    e_qe = jnp.exp(b)
    q, k = x["q"], x["k"]
    return dict(x, e_qi=e_qi, e_ki=e_ki, e_kd=e_kd, e_qe=e_qe, qi=q * e_qi, ki=k * e_ki, kd=k * e_kd, qe=q * e_qe,
                decay=jnp.exp(b_last))


def _chunks(u_ref, lb, tri_lower, cs):
    gates = [_chunk_gates(u_ref, lb, c) for c in cs]
    sums = [_running_sum(tri_lower, x["logf"]) for x in gates]
    return [_chunk_decays(x, b) for x, b in zip(gates, sums)]


def _hgrn2_fwd(u4, lb, g_out, name="hgrn2_fwd", sides=()):
    _, lp, d = u4.shape
    nh, nc = d // HEAD, lp // CHUNK
    per = _group(nc, 22)

    def body(u_ref, lb_ref, g_ref, o_ref, og_ref):
        lb_v = lb_ref[...]
        g = g_ref[...]

        tri_lower = _tri(CHUNK, True).astype(MXU)
        causal = _causal(CHUNK)

        def step(i, st):
            xs = _chunks(u_ref, lb_v, tri_lower, [i * per + u for u in range(per)])
            scores = [_dot(x["qi"], x["ki"], NT) for x in xs]
            updates = [_dot(x["v"], x["kd"], TN) for x in xs]
            states = []
            for x, upd in zip(xs, updates):
                states.append(st)
                st = x["decay"] * st + upd
            outs = [_dot(jnp.where(causal, a, 0.0), x["v"]) + _dot(x["qe"], s, NT)
                    for x, a, s in zip(xs, scores, states)]
            for x, o in zip(xs, outs):
                o_ref[x["sl"], :] = o
                on = o * lax.rsqrt(jnp.mean(o * o, axis=-1, keepdims=True) + EPS) * g
                z = u_ref[3, x["sl"], :]
                og_ref[x["sl"], :] = (on * (z * _sigmoid(z))).astype(og_ref.dtype)
            return st

        lax.fori_loop(0, nc // per, step, jnp.zeros((HEAD, HEAD), F32))

    slab = pl.BlockSpec((lp, HEAD), lambda h: (0, h))
    vec = pl.BlockSpec((1, HEAD), lambda h: (0, h))
    outs, side_results = _pallas(
        body,
        name=name,
        args=(u4, lb, g_out),
        grid=(nh,),
        in_specs=[pl.BlockSpec((4, lp, HEAD), lambda h: (0, 0, h)), vec, vec],
        out_specs=[slab, slab],
        out_shape=[jax.ShapeDtypeStruct((lp, d), F32), jax.ShapeDtypeStruct((lp, d), MXU)],
        semantics=("parallel",),
        vmem_mib=48,
        sides=sides,
    )
    return (*outs, side_results) if sides else tuple(outs)


def _hgrn2_bwd(u4, o, d_og, lb, g_out, name="hgrn2_bwd", sides=()):
    _, lp, d = u4.shape
    nh, nc = d // HEAD, lp // CHUNK
    per_f, per = _group(nc, 6), _group(nc, 11)

    def body(u_ref, o_ref, dog_ref, lb_ref, g_ref, du_ref, dlb_ref, dg_ref, st_ref, do_ref):
        lb_v = lb_ref[...]
        g = g_ref[...]

        tri_lower = _tri(CHUNK, True).astype(MXU)
        tri_upper = _tri(CHUNK, False).astype(MXU)
        causal = _causal(CHUNK)

        def fwd_step(i, carry):
            st, dg_acc = carry
            cs = [i * per_f + u for u in range(per_f)]
            xs = _chunks(u_ref, lb_v, tri_lower, cs)
            updates = [_dot(x["v"], x["kd"], TN) for x in xs]
            for c, x, upd in zip(cs, xs, updates):
                st_ref[c] = st
                st = x["decay"] * st + upd
            for x in xs:
                sl = x["sl"]
                ov = o_ref[sl, :]
                rstd = lax.rsqrt(jnp.mean(ov * ov, axis=-1, keepdims=True) + EPS)
                on = ov * rstd
                z = u_ref[3, sl, :]
                sz = _sigmoid(z)
                dog = dog_ref[sl, :]
                dy = dog * (z * sz)
                dz = dog * (on * g) * (sz * (1.0 + z * (1.0 - sz)))
                du_ref[3, sl, :] = dz.astype(du_ref.dtype)
                gdy = dy * g
                do = rstd * (gdy - on * jnp.mean(gdy * on, axis=-1, keepdims=True))
                do_ref[sl, :] = jnp.where(x["valid"], do, 0.0)
                dg_acc = dg_acc + jnp.sum(dy * on, axis=0, keepdims=True)
            return st, dg_acc

        _, dg_tot = lax.fori_loop(0, nc // per_f, fwd_step, (jnp.zeros((HEAD, HEAD), F32), jnp.zeros((1, HEAD), F32)))
        dg_ref[...] = dg_tot

        def bwd_step(i, carry):
            gt, dlb_acc = carry
            cs = [nc - 1 - (i * per + u) for u in range(per)]
            xs = _chunks(u_ref, lb_v, tri_lower, cs)
            dos = [do_ref[x["sl"], :] for x in xs]
            sts = [st_ref[c] for c in cs]
            scores = [jnp.where(causal, _dot(x["qi"], x["ki"], NT), 0.0) for x in xs]
            d_scores = [jnp.where(causal, _dot(do, x["v"], NT), 0.0) for x, do in zip(xs, dos)]
            d_qes = [_dot(do, st) for do, st in zip(dos, sts)]
            g_updates = [_dot(do, x["qe"], TN) for x, do in zip(xs, dos)]
            gts = []
            for x, upd in zip(xs, g_updates):
                gts.append(gt)
                gt = x["decay"] * gt + upd
            d_kds = [_dot(x["v"], g_) for x, g_ in zip(xs, gts)]
            dvs = [_dot(x["kd"], g_, NT) + _dot(a, do, TN) for x, g_, a, do in zip(xs, gts, scores, dos)]
            d_qis = [_dot(d_a, x["ki"]) for x, d_a in zip(xs, d_scores)]
            d_kis = [_dot(d_a, x["qi"], TN) for x, d_a in zip(xs, d_scores)]
            rows = lax.broadcasted_iota(jnp.int32, (CHUNK, HEAD), 0)
            dbs = []
            for x, g_, st, d_qi, d_ki, d_qe, d_kd in zip(xs, gts, sts, d_qis, d_kis, d_qes, d_kds):
                t_qi, t_ki, t_qe, t_kd = d_qi * x["qi"], d_ki * x["ki"], d_qe * x["qe"], d_kd * x["kd"]
                d_decay = jnp.sum(g_ * st, axis=0, keepdims=True)
                d_mid = jnp.sum(t_ki - t_qi, axis=0, keepdims=True)
                d_last = jnp.sum(t_kd, axis=0, keepdims=True) + d_decay * x["decay"]
                dbs.append(t_qi - t_ki + t_qe - t_kd + jnp.where(rows == CHUNK // 2 - 1, d_mid, 0.0)
                           + jnp.where(rows == CHUNK - 1, d_last, 0.0))
            dlogfs = [_running_sum(tri_upper, db) for db in dbs]
            for x, dlogf, dv, d_qi, d_ki, d_qe, d_kd in zip(xs, dlogfs, dvs, d_qis, d_kis, d_qes, d_kds):
                sl = x["sl"]
                dq = d_qi * x["e_qi"] + d_qe * x["e_qe"]
                dk = d_ki * x["e_ki"] + d_kd * x["e_kd"]
                valid, sq, sf, uq = x["valid"], x["sq"], x["sf"], x["uq"]
                dfg = jnp.where(valid, dlogf / x["fg"] - dk, 0.0)
                du_ref[0, sl, :] = jnp.where(valid, dq * (sq * (1.0 + uq * (1.0 - sq))), 0.0).astype(du_ref.dtype)
                du_ref[1, sl, :] = (dfg * (1.0 - lb_v) * (sf * (1.0 - sf))).astype(du_ref.dtype)
                du_ref[2, sl, :] = jnp.where(valid, dv, 0.0).astype(du_ref.dtype)
                dlb_acc = dlb_acc + jnp.sum(dfg * (1.0 - sf), axis=0, keepdims=True)
            return gt, dlb_acc

        _, dlb_tot = lax.fori_loop(0, nc // per, bwd_step, (jnp.zeros((HEAD, HEAD), F32), jnp.zeros((1, HEAD), F32)))
        dlb_ref[...] = dlb_tot

    slab = pl.BlockSpec((lp, HEAD), lambda h: (0, h))
    vec = pl.BlockSpec((1, HEAD), lambda h: (0, h))
    quad = pl.BlockSpec((4, lp, HEAD), lambda h: (0, 0, h))
    outs, side_results = _pallas(
        body,
        name=name,
        args=(u4, o, d_og, lb, g_out),
        grid=(nh,),
        in_specs=[quad, slab, slab, vec, vec],
        out_specs=[quad, vec, vec],
        out_shape=[jax.ShapeDtypeStruct((4, lp, d), MXU), jax.ShapeDtypeStruct((1, d), F32),
                   jax.ShapeDtypeStruct((1, d), F32)],
        scratch_shapes=[pltpu.VMEM((nc, HEAD, HEAD), F32), pltpu.VMEM((lp, HEAD), F32)],
        semantics=("parallel",),
        vmem_mib=58,
        sides=sides,
    )
    return (*outs, side_results) if sides else tuple(outs)


WIDE = 2 * HEAD
INV_SCALE = HEAD ** 0.5


def _split3(x):
    hi = x.astype(MXU).astype(F32)
    rest = x - hi
    mid = rest.astype(MXU).astype(F32)
    return hi, mid, (rest - mid).astype(MXU).astype(F32)


def _extra_cols(rows, first, second):
    lane = lax.broadcasted_iota(jnp.int32, (rows, HEAD), 1)
    out = jnp.where(lane < 6, 1.0, 0.0).astype(F32)
    for base, terms in ((0, first), (3, second)):
        if terms is not None:
            for j, term in enumerate(terms):
                out = jnp.where(lane == base + j, term, out)
    return out


def _head_col(a, h):
    lane = lax.broadcasted_iota(jnp.int32, a.shape, 1)
    return jnp.sum(jnp.where(lane == h, a, 0.0), axis=-1, keepdims=True)


def _attn_operands(ub, ukv, g_q, g_k, f_cum, name="attn_operands", tm=384):
    _, lp, d = ub.shape
    tm = _tile(lp, tm)
    nh = d // HEAD

    def body(q_ref, k_ref, v_ref, gq_ref, gk_ref, f_ref, qa_ref, ka_ref, va_ref):
        i = pl.program_id(0)
        f = f_ref[...]
        is_pad = (i * tm + lax.broadcasted_iota(jnp.int32, (tm, 1), 0)) < ROW_PAD
        ones_only = _extra_cols(tm, None, (0.0, 0.0, 0.0)).astype(MXU)
        for h in range(nh):
            hs = slice(h * HEAD, (h + 1) * HEAD)
            lo, hi = h * WIDE, h * WIDE + HEAD
            f_h = _head_col(f, h)
            for x_ref, g_ref, o_ref in ((q_ref, gq_ref, qa_ref), (k_ref, gk_ref, ka_ref)):
                x = x_ref[:, hs]
                y = x * lax.rsqrt(jnp.mean(x * x, axis=-1, keepdims=True) + EPS)
                o_ref[:, lo:hi] = (y * g_ref[:, hs]).astype(o_ref.dtype)
            qa_ref[:, hi:hi + HEAD] = _extra_cols(tm, _split3(f_h * INV_SCALE), None).astype(MXU)
            f_key = jnp.where(is_pad, -MASK_VALUE, f_h)
            ka_ref[:, hi:hi + HEAD] = _extra_cols(tm, None, _split3(-f_key * INV_SCALE)).astype(MXU)
            va_ref[:, lo:hi] = v_ref[:, hs].astype(MXU)
            va_ref[:, hi:hi + HEAD] = ones_only

    wide = pl.BlockSpec((tm, nh * WIDE), lambda i: (i, 0))
    vec = pl.BlockSpec((1, d), lambda i: (0, 0))
    return pl.pallas_call(
        body,
        name=name,
        grid=(lp // tm,),
        in_specs=[pl.BlockSpec((None, tm, d), lambda i: (0, i, 0)), pl.BlockSpec((None, tm, d), lambda i: (0, i, 0)),
                  pl.BlockSpec((None, tm, d), lambda i: (1, i, 0)), vec, vec, pl.BlockSpec((tm, 128), lambda i: (i, 0))],
        out_specs=[wide, wide, wide],
        out_shape=[jax.ShapeDtypeStruct((lp, nh * WIDE), MXU)] * 3,
        compiler_params=_cparams(("parallel",), 56),
    )(ub, ukv, ukv, g_q, g_k, f_cum)


def _head_rms_bwd_tile(x, g, dy):
    rstd = lax.rsqrt(jnp.mean(x * x, axis=-1, keepdims=True) + EPS)
    xhat = x * rstd
    gdy = dy * g
    dx = rstd * (gdy - xhat * jnp.mean(gdy * xhat, axis=-1, keepdims=True))
    return dx, jnp.sum(dy * xhat, axis=0, keepdims=True)


def _fgate_fwd(ufl, b_f):
    lp = ufl.shape[0]
    nb = lp // 128

    def body(u_ref, b_ref, f_ref):
        def step(i, carry):
            sl = pl.ds(pl.multiple_of(i * 128, 128), 128)
            valid = (i * 128 + lax.broadcasted_iota(jnp.int32, (128, 128), 0)) >= ROW_PAD
            x = u_ref[sl, :] + b_ref[...]
            logf = jnp.where(valid, jnp.minimum(x, 0.0) - jnp.log(1.0 + jnp.exp(-jnp.abs(x))), 0.0)
            f = _dot_exact(_tri(128, True), logf) + carry
            f_ref[sl, :] = f
            return f[127:128, :]

        lax.fori_loop(0, nb, step, jnp.zeros((1, 128), F32))

    return pl.pallas_call(body, name="fgate_fwd", out_shape=jax.ShapeDtypeStruct((lp, 128), F32))(ufl, b_f)


def _fgate_bwd(ufl, b_f, d_f):
    lp = ufl.shape[0]
    nb = lp // 128

    def body(u_ref, b_ref, df_ref, du_ref, db_ref):
        def step(j, carry):
            later, db_acc = carry
            i = nb - 1 - j
            sl = pl.ds(pl.multiple_of(i * 128, 128), 128)
            valid = (i * 128 + lax.broadcasted_iota(jnp.int32, (128, 128), 0)) >= ROW_PAD
            x = u_ref[sl, :] + b_ref[...]
            df = df_ref[sl, :]
            dlogf = _dot_exact(_tri(128, False), df) + later
            dx = jnp.where(valid, dlogf * _sigmoid(-x), 0.0)
            du_ref[sl, :] = dx.astype(du_ref.dtype)
            return later + jnp.sum(df, axis=0, keepdims=True), db_acc + jnp.sum(dx, axis=0, keepdims=True)

        _, db_tot = lax.fori_loop(0, nb, step, (jnp.zeros((1, 128), F32), jnp.zeros((1, 128), F32)))
        db_ref[...] = db_tot

    return pl.pallas_call(
        body, name="fgate_bwd",
        out_shape=[jax.ShapeDtypeStruct((lp, 128), MXU), jax.ShapeDtypeStruct((1, 128), F32)],
    )(ufl, b_f, d_f)


STRIP = 32
PAIR = 2


def _strip_causal(r, t):
    row = r + lax.broadcasted_iota(jnp.int32, (STRIP, t), 0)
    col = lax.broadcasted_iota(jnp.int32, (STRIP, t), 1)
    return col <= row


def _fox_fwd(qa, ka, va, ub, name="fox_fwd", t=384):
    lp = qa.shape[0]
    d = ub.shape[2]
    t = _tile(lp, t)
    nh, nq = d // HEAD, lp // t
    scale = HEAD ** -0.5

    assert nh % PAIR == 0
    heads = range(PAIR)

    def body(q_ref, k_ref, v_ref, z_ref, o_ref, olo_ref, og_ref, qb_ref,
             s_ref, p_ref, m_ref, a_ref, l_ref, acc_ref):
        qb = pl.program_id(1)
        m_ref[...] = jnp.full((PAIR, t, 1), MASK_VALUE, F32)
        l_ref[...] = jnp.zeros((PAIR, t, 128), F32)
        acc_ref[...] = jnp.zeros((PAIR, 2 * t, HEAD), F32)
        p_ref[1] = jnp.zeros((PAIR, 2 * t, t), MXU)
        a_ref[1] = jnp.ones((PAIR, t, 1), F32)

        def scores(kb, buf):
            ks = pl.ds(pl.multiple_of(kb * t, t), t)
            for j in heads:
                ws = slice(j * WIDE, (j + 1) * WIDE)
                s_ref[buf, j] = _dot(q_ref[:, ws], k_ref[ks, ws], NT)

        def weighted_sum(kb, buf):
            ks = pl.ds(pl.multiple_of(kb * t, t), t)
            for j in heads:
                pv = _dot(p_ref[buf, j], v_ref[ks, j * WIDE:j * WIDE + HEAD])
                alpha = a_ref[buf, j]
                acc_ref[j, 0:t, :] = alpha * acc_ref[j, 0:t, :] + pv[0:t]
                acc_ref[j, t:2 * t, :] = alpha * acc_ref[j, t:2 * t, :] + pv[t:2 * t]

        def softmax_update(buf, diagonal):
            for j in heads:
                for r in range(0, t, STRIP):
                    rs = slice(r, r + STRIP)
                    x = s_ref[buf, j, rs, :] * scale
                    if diagonal:
                        x = jnp.where(_strip_causal(r, t), x, MASK_VALUE)
                    m_old = m_ref[j, rs, :]
                    m_new = jnp.maximum(m_old, jnp.max(x, axis=-1, keepdims=True))
                    alpha = jnp.exp(m_old - m_new)
                    p = jnp.exp(x - m_new)
                    m_ref[j, rs, :] = m_new
                    a_ref[buf, j, rs, :] = alpha
                    l_ref[j, rs, :] = alpha * l_ref[j, rs, :] + sum(p[:, c:c + 128] for c in range(0, t, 128))
                    p_hi = p.astype(MXU)
                    p_ref[buf, j, rs, :] = p_hi
                    p_ref[buf, j, t + r:t + r + STRIP, :] = (p - p_hi.astype(F32)).astype(MXU)

        def off_diagonal(kb, cur):
            weighted_sum(jnp.maximum(kb - 1, 0), 1 - cur)
            scores(kb + 1, 1 - cur)
            softmax_update(cur, False)

        def diagonal(cur):
            weighted_sum(jnp.maximum(qb - 1, 0), 1 - cur)
            softmax_update(cur, True)
            weighted_sum(qb, cur)

        def two_blocks(i, carry):
            off_diagonal(2 * i, 0)
            off_diagonal(2 * i + 1, 1)
            return carry

        scores(0, 0)
        lax.fori_loop(0, qb // 2, two_blocks, 0)

        @pl.when(lax.rem(qb, 2) == 0)
        def _():
            diagonal(0)

        @pl.when(lax.rem(qb, 2) == 1)
        def _():
            off_diagonal(qb - 1, 0)
            diagonal(1)

        is_pad = (qb * t + lax.broadcasted_iota(jnp.int32, (t, 1), 0)) < ROW_PAD
        for j in heads:
            hs = slice(j * HEAD, (j + 1) * HEAD)
            l = jnp.sum(l_ref[j], axis=-1, keepdims=True)
            o = acc_ref[j, 0:t, :] / l
            o_ref[:, hs] = o
            olo_ref[:, hs] = acc_ref[j, t:2 * t, :] / l
            z = z_ref[:, hs]
            og_ref[:, hs] = (o * (z * _sigmoid(z))).astype(og_ref.dtype)
            extra = q_ref[:, j * WIDE + HEAD:(j + 1) * WIDE].astype(F32)
            f_scaled = extra[:, 0:1] + extra[:, 1:2] + extra[:, 2:3]
            log_term = jnp.where(is_pad, MASK_VALUE * INV_SCALE, f_scaled - (m_ref[j] + jnp.log(l)) * INV_SCALE)
            qb_ref[:, j * WIDE:j * WIDE + HEAD] = q_ref[:, j * WIDE:j * WIDE + HEAD]
            qb_ref[:, j * WIDE + HEAD:(j + 1) * WIDE] = _extra_cols(t, _split3(log_term), None).astype(qb_ref.dtype)

    scratch = [pltpu.VMEM((2, PAIR, t, t), F32), pltpu.VMEM((2, PAIR, 2 * t, t), MXU), pltpu.VMEM((PAIR, t, 1), F32),
               pltpu.VMEM((2, PAIR, t, 1), F32), pltpu.VMEM((PAIR, t, 128), F32), pltpu.VMEM((PAIR, 2 * t, HEAD), F32)]
    tile = pl.BlockSpec((t, PAIR * HEAD), lambda g, i: (i, g))
    wide_tile = pl.BlockSpec((t, PAIR * WIDE), lambda g, i: (i, g))
    wide_all = pl.BlockSpec((lp, PAIR * WIDE), lambda g, i: (0, g))
    return pl.pallas_call(
        body,
        name=name,
        grid=(nh // PAIR, nq),
        in_specs=[wide_tile, wide_all, wide_all, pl.BlockSpec((None, t, PAIR * HEAD), lambda g, i: (1, i, g))],
        out_specs=[tile, tile, tile, wide_tile],
        out_shape=[jax.ShapeDtypeStruct((lp, d), F32), jax.ShapeDtypeStruct((lp, d), F32),
                   jax.ShapeDtypeStruct((lp, d), MXU), jax.ShapeDtypeStruct((lp, nh * WIDE), MXU)],
        scratch_shapes=scratch,
        compiler_params=_cparams(("parallel", "arbitrary"), 48),
    )(qa, ka, va, ub)


def _fox_bwd(qb, ka, va, doa, ukv, g_k, ub, g_q, name="fox_bwd", t=384, sides=()):
    lp = qb.shape[0]
    d = ukv.shape[2]
    t = _tile(lp, t)
    nh, nk = d // HEAD, lp // t
    scale = HEAD ** -0.5

    assert nh % PAIR == 0
    heads = range(PAIR)

    def body(q_ref, do_ref, k_ref, v_ref, kraw_ref, gk_ref, qraw_ref, gq_ref, dqraw_ref, dgq_ref, dukv_ref, dgk_ref,
             dfk_ref, s_ref, dp_ref, p_ref, ds_ref, col_ref, dk_ref, dv_ref, dq_ref):
        kb = pl.program_id(1)

        @pl.when(kb == 0)
        def _():
            dq_ref[...] = jnp.zeros_like(dq_ref)
            dgk_ref[...] = jnp.zeros_like(dgk_ref)

        dk_ref[...] = jnp.zeros_like(dk_ref)
        dv_ref[...] = jnp.zeros_like(dv_ref)
        col_ref[...] = jnp.zeros_like(col_ref)

        def step(qb, diagonal):
            qs = pl.ds(pl.multiple_of(qb * t, t), t)
            for j in heads:
                ws = slice(j * WIDE, (j + 1) * WIDE)
                s_ref[j] = _dot(q_ref[qs, ws], k_ref[:, ws], NT)
                dp_ref[j] = _dot(do_ref[qs, ws], v_ref[:, ws], NT)
            for j in heads:
                for r in range(0, t, STRIP):
                    rs = slice(r, r + STRIP)
                    x = s_ref[j, rs, :] * scale
                    if diagonal:
                        x = jnp.where(_strip_causal(r, t), x, MASK_VALUE)
                    p = jnp.exp(x)
                    ds = p * dp_ref[j, rs, :]
                    p_ref[j, rs, :] = p.astype(MXU)
                    ds_ref[j, rs, :] = (ds * scale).astype(MXU)
                    col_ref[j] += ds
            for j in heads:
                hs = slice(j * HEAD, (j + 1) * HEAD)
                narrow = slice(j * WIDE, j * WIDE + HEAD)
                dsb = ds_ref[j]
                dv_ref[:, hs] += _dot(p_ref[j], do_ref[qs, narrow], TN)
                dq_ref[qs, hs] += _dot(dsb, k_ref[:, narrow])
                dk_ref[:, hs] += _dot(dsb, q_ref[qs, narrow], TN)

        def off_diagonal(qb, carry):
            step(qb, False)
            return carry

        step(kb, True)
        lax.fori_loop(kb + 1, nk, off_diagonal, 0)
        for j in heads:
            hs = slice(j * HEAD, (j + 1) * HEAD)
            dfk_ref[j] = -jnp.sum(col_ref[j], axis=0, keepdims=True)
            dx, dg = _head_rms_bwd_tile(kraw_ref[:, hs], gk_ref[:, hs], dk_ref[:, hs])
            dukv_ref[0, :, hs] = dx.astype(dukv_ref.dtype)
            dukv_ref[1, :, hs] = dv_ref[:, hs].astype(dukv_ref.dtype)
            dgk_ref[:, hs] += dg

        @pl.when(kb == nk - 1)
        def _():
            dgq_ref[...] = jnp.zeros_like(dgq_ref)

            def rows(c, carry):
                rs = pl.ds(pl.multiple_of(c * t, t), t)
                for j in heads:
                    hs = slice(j * HEAD, (j + 1) * HEAD)
                    dx, dg = _head_rms_bwd_tile(qraw_ref[rs, hs], gq_ref[:, hs], dq_ref[rs, hs])
                    dqraw_ref[rs, hs] = dx.astype(dqraw_ref.dtype)
                    dgq_ref[:, hs] += dg
                return carry

            lax.fori_loop(0, nk, rows, 0)

    scratch = [pltpu.VMEM((PAIR, t, t), F32), pltpu.VMEM((PAIR, t, t), F32), pltpu.VMEM((PAIR, t, t), MXU),
               pltpu.VMEM((PAIR, t, t), MXU), pltpu.VMEM((PAIR, STRIP, t), F32),
               pltpu.VMEM((t, PAIR * HEAD), F32), pltpu.VMEM((t, PAIR * HEAD), F32), pltpu.VMEM((lp, PAIR * HEAD), F32)]
    whole = pl.BlockSpec((lp, PAIR * HEAD), lambda g, j: (0, g))
    wide_all = pl.BlockSpec((lp, PAIR * WIDE), lambda g, j: (0, g))
    wide_tile = pl.BlockSpec((t, PAIR * WIDE), lambda g, j: (j, g))
    vec = pl.BlockSpec((1, PAIR * HEAD), lambda g, j: (0, g))
    outs, side_results = _pallas(
        body,
        name=name,
        args=(qb, doa, ka, va, ukv, g_k, ub, g_q),
        grid=(nh // PAIR, nk),
        in_specs=[wide_all, wide_all, wide_tile, wide_tile,
                  pl.BlockSpec((None, t, PAIR * HEAD), lambda g, j: (0, j, g)), vec,
                  pl.BlockSpec((None, lp, PAIR * HEAD), lambda g, j: (0, 0, g)), vec],
        out_specs=[whole, vec, pl.BlockSpec((2, t, PAIR * HEAD), lambda g, j: (0, j, g)), vec,
                   pl.BlockSpec((PAIR, 1, t), lambda g, j: (g, 0, j))],
        out_shape=[jax.ShapeDtypeStruct((lp, d), MXU), jax.ShapeDtypeStruct((1, d), F32),
                   jax.ShapeDtypeStruct((2, lp, d), MXU), jax.ShapeDtypeStruct((1, d), F32),
                   jax.ShapeDtypeStruct((nh, 1, lp), F32)],
        scratch_shapes=scratch,
        semantics=("parallel", "arbitrary"),
        vmem_mib=48,
        sides=sides,
    )
    return (*outs, side_results) if sides else tuple(outs)


def _fox_do(d_og, o, o_lo, ub, name="fox_do", tm=384):
    lp, d = o.shape
    tm = _tile(lp, tm)
    nh = d // HEAD

    def body(dog_ref, o_ref, olo_ref, z_ref, doa_ref):
        for h in range(nh):
            hs = slice(h * HEAD, (h + 1) * HEAD)
            z = z_ref[:, hs]
            do = (dog_ref[:, hs] * (z * _sigmoid(z))).astype(doa_ref.dtype)
            delta = jnp.sum(do.astype(F32) * (o_ref[:, hs] + olo_ref[:, hs]), axis=-1, keepdims=True)
            doa_ref[:, h * WIDE:h * WIDE + HEAD] = do
            doa_ref[:, h * WIDE + HEAD:(h + 1) * WIDE] = _extra_cols(tm, _split3(-delta), (0.0, 0.0, 0.0)).astype(
                doa_ref.dtype)

    row = pl.BlockSpec((tm, d), lambda i: (i, 0))
    return pl.pallas_call(
        body,
        name=name,
        grid=(lp // tm,),
        in_specs=[row, row, row, pl.BlockSpec((None, tm, d), lambda i: (1, i, 0))],
        out_specs=pl.BlockSpec((tm, nh * WIDE), lambda i: (i, 0)),
        out_shape=jax.ShapeDtypeStruct((lp, nh * WIDE), MXU),
        compiler_params=_cparams(("parallel",), 56),
    )(d_og, o, o_lo, ub)


def _fox_gate_bwd(ub, d_og, o, dq, name="fox_gate_bwd", tm=384):
    _, lp, d = ub.shape
    tm = _tile(lp, tm)

    def body(z_ref, dog_ref, o_ref, dq_ref, dub_ref):
        z = z_ref[...]
        sz = _sigmoid(z)
        dub_ref[0] = dq_ref[...]
        dub_ref[1] = (dog_ref[...] * o_ref[...] * (sz * (1.0 + z * (1.0 - sz)))).astype(dub_ref.dtype)

    row = pl.BlockSpec((tm, d), lambda i: (i, 0))
    return pl.pallas_call(
        body,
        name=name,
        grid=(lp // tm,),
        in_specs=[pl.BlockSpec((None, tm, d), lambda i: (1, i, 0)), row, row, row],
        out_specs=pl.BlockSpec((2, tm, d), lambda i: (0, i, 0)),
        out_shape=jax.ShapeDtypeStruct((2, lp, d), MXU),
        compiler_params=_cparams(("parallel",), 48),
    )(ub, d_og, o, dq)


def _loss(h, target, name="loss_head"):
    lp, d = h.shape
    nb = lp // 128

    def body(h_ref, t_ref, loss_ref, dh_ref, acc_ref):
        i = pl.program_id(0)

        @pl.when(i == 0)
        def _():
            acc_ref[...] = jnp.zeros_like(acc_ref)
            dh_ref[...] = jnp.zeros_like(dh_ref)

        @pl.when(i > 0)
        def _():
            err = h_ref[...] - t_ref[...]
            dh_ref[...] = err * (1.0 / d)
            acc_ref[...] += jnp.sum(jnp.sum(err * err, axis=-1, keepdims=True) * (1.0 / d), axis=0, keepdims=True)

        @pl.when(i == nb - 1)
        def _():
            loss_ref[...] = 0.5 * acc_ref[...]

    return pl.pallas_call(
        body,
        name=name,
        grid=(nb,),
        in_specs=[pl.BlockSpec((128, d), lambda i: (i, 0)),
                  pl.BlockSpec((128, d), lambda i: (jnp.maximum(i - 1, 0), 0))],
        out_specs=[pl.BlockSpec((1, 1), lambda i: (0, 0)), pl.BlockSpec((128, d), lambda i: (i, 0))],
        out_shape=[jax.ShapeDtypeStruct((1, 1), F32), jax.ShapeDtypeStruct((lp, d), F32)],
        scratch_shapes=[pltpu.VMEM((1, 1), F32)],
        compiler_params=_cparams(("arbitrary",), 32),
    )(h, target)


def _adam_math(w, g, m, v):
    m = ADAM_B1 * m + (1.0 - ADAM_B1) * g
    v = ADAM_B2 * v + (1.0 - ADAM_B2) * (g * g)
    m_hat = m / (1.0 - ADAM_B1 ** ADAM_STEP)
    v_hat = v / (1.0 - ADAM_B2 ** ADAM_STEP)
    delta = -ADAM_LR * (m_hat / (jnp.sqrt(v_hat) + ADAM_EPS) + ADAM_WD * w)
    return delta, m, v


def _adamw(parts, w, m, v, name, tm=512):
    n, r, c = parts.shape
    if r % tm == 0:
        grid, blk, at = r // tm, (tm, c), lambda i: (i, 0)
    else:
        tc = _tile(c, tm)
        grid, blk, at = c // tc, (r, tc), lambda i: (0, i)

    def body(p_ref, w_ref, m_ref, v_ref, g_ref, d_ref, nm_ref, nv_ref):
        g = p_ref[0].astype(F32)
        for j in range(1, n):
            g = g + p_ref[j].astype(F32)
        g_ref[...] = g
        d_ref[...], nm_ref[...], nv_ref[...] = _adam_math(w_ref[...], g, m_ref[...], v_ref[...])

    row = pl.BlockSpec(blk, at)
    return pl.pallas_call(
        body,
        name=name,
        grid=(grid,),
        in_specs=[pl.BlockSpec((n,) + blk, lambda i: (0,) + at(i)), row, row, row],
        out_specs=[row] * 4,
        out_shape=[jax.ShapeDtypeStruct((r, c), F32)] * 4,
        compiler_params=_cparams(("parallel",), 48),
    )(parts, w, m, v)


def _adamw_small(gs, ws, ms, vs, name="adamw_small"):
    n = len(ws)
    flat = [jnp.reshape(a, (-1, a.shape[-1])) for group in (gs, ws, ms, vs) for a in group]

    def body(*refs):
        g_refs, w_refs, m_refs, v_refs = (refs[k * n:(k + 1) * n] for k in range(4))
        outs = refs[4 * n:]
        for i in range(n):
            d_new, m_new, v_new = _adam_math(w_refs[i][...], g_refs[i][...], m_refs[i][...], v_refs[i][...])
            outs[i][...], outs[n + i][...], outs[2 * n + i][...] = d_new, m_new, v_new

    res = pl.pallas_call(
        body, name=name, out_shape=[jax.ShapeDtypeStruct(a.shape, F32) for a in flat[n:2 * n]] * 3)(*flat)
    return [[jnp.reshape(r, w.shape) for r, w in zip(res[k * n:(k + 1) * n], ws)] for k in range(3)]


def _place():
    x, y, c = lax.axis_index("x"), lax.axis_index("y"), lax.axis_index("c")
    return x, y, c


def _other_chips(x, y):
    return [(1 - x, y), (x, 1 - y), (1 - x, 1 - y)]


def _any_spec():
    return pl.BlockSpec(memory_space=pl.ANY)


class _Side:
    def __init__(self, ins, outs, sems, start, finish, aliases=None):
        self.ins, self.outs, self.sems = list(ins), list(outs), sems
        self.start, self.finish = start, finish
        self.aliases = dict(aliases or {})


def _pallas(body, *, name, out_shape, args=(), grid=(), in_specs=(), out_specs=(), scratch_shapes=(),
            semantics=(), vmem_mib=None, sides=(), aliases=None, prefetch=()):
    n_in, n_out, n_scr, n_pre = len(args), len(out_shape), len(scratch_shapes), len(prefetch)
    side_ins = [a for s in sides for a in s.ins]
    side_outs = [o for s in sides for o in s.outs]
    side_sems = [pltpu.SemaphoreType.DMA((max(k, 1),)) for s in sides for k in s.sems]
    aliases, in_at, out_at = {n_pre + i: o for i, o in (aliases or {}).items()}, n_pre + n_in, n_out
    for s in sides:
        aliases.update({in_at + i: out_at + o for i, o in s.aliases.items()})
        in_at, out_at = in_at + len(s.ins), out_at + len(s.outs)

    def wrapped(*refs):
        at = [n_pre]

        def take(k):
            got = refs[at[0]:at[0] + k]
            at[0] += k
            return got

        main_in = take(n_in)
        s_in = [take(len(s.ins)) for s in sides]
        main_out = take(n_out)
        s_out = [take(len(s.outs)) for s in sides]
        main_scr = take(n_scr)
        s_sem = [take(3) for s in sides]

        def run(stage):
            for s, i_, o_, m_ in zip(sides, s_in, s_out, s_sem):
                getattr(s, stage)(i_, o_, *m_)

        first = last = None
        for k, g in enumerate(grid):
            i = pl.program_id(k)
            first = (i == 0) if first is None else jnp.logical_and(first, i == 0)
            last = (i == g - 1) if last is None else jnp.logical_and(last, i == g - 1)
        if sides:
            run("start") if first is None else pl.when(first)(lambda: run("start"))
        body(*main_in, *main_out, *main_scr)
        if sides:
            run("finish") if last is None else pl.when(last)(lambda: run("finish"))

    kw = {}
    specs = dict(in_specs=list(in_specs) + [_any_spec()] * len(side_ins),
                 out_specs=list(out_specs) + [_any_spec()] * len(side_outs),
                 scratch_shapes=list(scratch_shapes) + side_sems)
    if prefetch:
        kw["grid_spec"] = pltpu.PrefetchScalarGridSpec(num_scalar_prefetch=n_pre, grid=grid, **specs)
    else:
        kw.update(specs)
        if grid:
            kw["grid"] = grid
    if semantics or vmem_mib:
        sem = tuple("arbitrary" for _ in grid) if sides else tuple(semantics)
        kw["compiler_params"] = pltpu.CompilerParams(
            dimension_semantics=sem or None, vmem_limit_bytes=vmem_mib * MIB if vmem_mib else None)
    res = pl.pallas_call(
        wrapped,
        name=name,
        out_shape=list(out_shape) + side_outs,
        input_output_aliases=aliases,
        **kw,
    )(*prefetch, *args, *side_ins)
    main, rest, per_side = list(res[:n_out]), list(res[n_out:]), []
    for s in sides:
        per_side.append(rest[:len(s.outs)])
        rest = rest[len(s.outs):]
    return main, per_side


def _slot(p):
    return 4 * p[0] + 2 * p[1] + p[2]


def _sibling_side(grads):
    n = len(grads)

    def copies(ins, outs, send_sems, recv_sems):
        x, y, c = _place()
        return [pltpu.make_async_remote_copy(
            src_ref=ins[t].at[2 * chip + (1 - c)], dst_ref=outs[t].at[chip],
            send_sem=send_sems.at[4 * t + chip], recv_sem=recv_sems.at[4 * t + chip],
            device_id=(x, y, 1 - c), device_id_type=MESH) for t in range(n) for chip in range(N_CHIP)]

    def start(ins, outs, send_sems, recv_sems, local_sems):
        for cp in copies(ins, outs, send_sems, recv_sems):
            cp.start()

    def finish(ins, outs, send_sems, recv_sems, local_sems):
        for cp in copies(ins, outs, send_sems, recv_sems):
            cp.wait()

    outs = [jax.ShapeDtypeStruct((N_CHIP,) + g.shape[1:], g.dtype) for g in grads]
    return _Side(grads, outs, (4 * n, 4 * n, 0), start, finish)


def _chips_side(partials):
    n = len(partials)

    def copies(ins, outs, send_sems, recv_sems, local_sems):
        x, y, c = _place()
        my_chip = 2 * x + y
        local = [pltpu.make_async_copy(ins[t].at[my_chip], outs[t].at[my_chip], local_sems.at[t]) for t in range(n)]
        sends, recvs = [], []
        for t in range(n):
            for j, chip in enumerate(_other_chips(x, y)):
                their = 2 * chip[0] + chip[1]
                sems = dict(send_sem=send_sems.at[3 * t + j], recv_sem=recv_sems.at[3 * t + j],
                            device_id=(*chip, c), device_id_type=MESH)
                sends.append(pltpu.make_async_remote_copy(src_ref=ins[t].at[their], dst_ref=outs[t].at[my_chip], **sems))
                recvs.append(pltpu.make_async_remote_copy(src_ref=ins[t].at[my_chip], dst_ref=outs[t].at[their], **sems))
        return local, sends, recvs

    def start(ins, outs, send_sems, recv_sems, local_sems):
        local, sends, _ = copies(ins, outs, send_sems, recv_sems, local_sems)
        for cp in local + sends:
            cp.start()

    def finish(ins, outs, send_sems, recv_sems, local_sems):
        local, sends, recvs = copies(ins, outs, send_sems, recv_sems, local_sems)
        for cp in sends:
            cp.wait_send()
        for cp in recvs:
            cp.wait_recv()
        for cp in local:
            cp.wait()

    outs = [jax.ShapeDtypeStruct(p.shape, p.dtype) for p in partials]
    return _Side(partials, outs, (3 * n, 3 * n, n), start, finish)


def _gather_own_side(blocks):
    n = len(blocks)

    def copies(ins, outs, send_sems, recv_sems, local_sems):
        x, y, c = _place()
        me = (x, y, c)
        peers = [(x, y, 1 - c)] + [(*chip, c) for chip in _other_chips(x, y)]
        local = [pltpu.make_async_copy(ins[t], outs[t].at[_slot(me)], local_sems.at[t]) for t in range(n)]
        sends, recvs = [], []
        for t in range(n):
            for k, peer in enumerate(peers):
                sems = dict(send_sem=send_sems.at[4 * t + k], recv_sem=recv_sems.at[4 * t + k],
                            device_id=peer, device_id_type=MESH)
                sends.append(pltpu.make_async_remote_copy(src_ref=ins[t], dst_ref=outs[t].at[_slot(me)], **sems))
                recvs.append(pltpu.make_async_remote_copy(src_ref=ins[t], dst_ref=outs[t].at[_slot(peer)], **sems))
        return local, sends, recvs

    def start(ins, outs, send_sems, recv_sems, local_sems):
        local, sends, _ = copies(ins, outs, send_sems, recv_sems, local_sems)
        for cp in local + sends:
            cp.start()

    def finish(ins, outs, send_sems, recv_sems, local_sems):
        local, sends, recvs = copies(ins, outs, send_sems, recv_sems, local_sems)
        for cp in sends:
            cp.wait_send()
        for cp in recvs:
            cp.wait_recv()
        for cp in local:
            cp.wait()

    outs = [jax.ShapeDtypeStruct((N_DEV,) + b.shape, b.dtype) for b in blocks]
    return _Side(blocks, outs, (4 * n, 4 * n, n), start, finish)


def _gather_pass_side(gathered):
    n = len(gathered)

    def copies(outs, send_sems, recv_sems):
        x, y, c = _place()
        sends, recvs = [], []
        for t in range(n):
            for j, chip in enumerate(_other_chips(x, y)):
                sems = dict(send_sem=send_sems.at[3 * t + j], recv_sem=recv_sems.at[3 * t + j],
                            device_id=(x, y, 1 - c), device_id_type=MESH)
                mine, theirs = outs[t].at[_slot((*chip, c))], outs[t].at[_slot((*chip, 1 - c))]
                sends.append(pltpu.make_async_remote_copy(src_ref=mine, dst_ref=mine, **sems))
                recvs.append(pltpu.make_async_remote_copy(src_ref=mine, dst_ref=theirs, **sems))
        return sends, recvs

    def start(ins, outs, send_sems, recv_sems, local_sems):
        for cp in copies(outs, send_sems, recv_sems)[0]:
            cp.start()

    def finish(ins, outs, send_sems, recv_sems, local_sems):
        sends, recvs = copies(outs, send_sems, recv_sems)
        for cp in sends:
            cp.wait_send()
        for cp in recvs:
            cp.wait_recv()

    outs = [jax.ShapeDtypeStruct(g.shape, g.dtype) for g in gathered]
    return _Side(gathered, outs, (3 * n, 3 * n, 0), start, finish, aliases={t: t for t in range(n)})


def _alone(side, name):
    return _pallas(lambda: None, name=name, out_shape=[], sides=[side])[1][0]


def _gather_side(blocks):
    n = len(blocks)

    def parts(ins, outs, send_sems, recv_sems, local_sems):
        x, y, c = _place()
        me, sibling = (x, y, c), (x, y, 1 - c)

        def copy(t, k, block, to, src=None):
            dst = outs[t].at[_slot(block)]
            return pltpu.make_async_remote_copy(
                src_ref=dst if src is None else src, dst_ref=dst,
                send_sem=send_sems.at[7 * t + k], recv_sem=recv_sems.at[7 * t + k],
                device_id=to, device_id_type=MESH)

        local = [pltpu.make_async_copy(ins[t], outs[t].at[_slot(me)], local_sems.at[t]) for t in range(n)]
        own = [copy(t, 0, me, sibling, src=ins[t]) for t in range(n)]
        own += [copy(t, 1 + j, me, (*chip, c), src=ins[t]) for t in range(n) for j, chip in enumerate(_other_chips(x, y))]
        return me, sibling, c, _other_chips(x, y), copy, local, own

    def start(ins, outs, send_sems, recv_sems, local_sems):
        *_, local, own = parts(ins, outs, send_sems, recv_sems, local_sems)
        for cp in local + own:
            cp.start()

    def finish(ins, outs, send_sems, recv_sems, local_sems):
        me, sibling, c, chips, copy, local, own = parts(ins, outs, send_sems, recv_sems, local_sems)
        passed = []
        for t in range(n):
            for j, chip in enumerate(chips):
                copy(t, 1 + j, (*chip, c), me).wait_recv()
                passed.append(copy(t, 4 + j, (*chip, c), sibling))
                passed[-1].start()
        for t in range(n):
            copy(t, 0, sibling, me).wait_recv()
            for j, chip in enumerate(chips):
                copy(t, 4 + j, (*chip, 1 - c), me).wait_recv()
        for cp in own + passed:
            cp.wait_send()
        for cp in local:
            cp.wait()

    outs = [jax.ShapeDtypeStruct((N_DEV,) + b.shape, b.dtype) for b in blocks]
    return _Side(blocks, outs, (7 * n, 7 * n, n), start, finish)


def _gather_near_side(blocks):
    n = len(blocks)

    def copies(ins, outs, send_sems, recv_sems, local_sems):
        x, y, c = _place()
        me = (x, y, c)
        peers = [(x, y, 1 - c), (1 - x, y, c), (x, 1 - y, c)]
        local = [pltpu.make_async_copy(ins[t], outs[t].at[_slot(me)], local_sems.at[t]) for t in range(n)]
        sends, recvs = [], []
        for t in range(n):
            for k, peer in enumerate(peers):
                sems = dict(send_sem=send_sems.at[3 * t + k], recv_sem=recv_sems.at[3 * t + k],
                            device_id=peer, device_id_type=MESH)
                sends.append(pltpu.make_async_remote_copy(src_ref=ins[t], dst_ref=outs[t].at[_slot(me)], **sems))
                recvs.append(pltpu.make_async_remote_copy(src_ref=ins[t], dst_ref=outs[t].at[_slot(peer)], **sems))
        return local, sends, recvs

    def start(ins, outs, send_sems, recv_sems, local_sems):
        local, sends, _ = copies(ins, outs, send_sems, recv_sems, local_sems)
        for cp in local + sends:
            cp.start()

    def finish(ins, outs, send_sems, recv_sems, local_sems):
        local, sends, recvs = copies(ins, outs, send_sems, recv_sems, local_sems)
        for cp in sends:
            cp.wait_send()
        for cp in recvs:
            cp.wait_recv()
        for cp in local:
            cp.wait()

    outs = [jax.ShapeDtypeStruct((N_DEV,) + b.shape, b.dtype) for b in blocks]
    return _Side(blocks, outs, (3 * n, 3 * n, n), start, finish)


def _gather_far_side(gathered):
    n = len(gathered)

    def parts(outs, send_sems, recv_sems):
        x, y, c = _place()
        me, sibling, far = (x, y, c), (x, y, 1 - c), (1 - x, 1 - y, c)
        chips = [(1 - x, y), (x, 1 - y), (1 - x, 1 - y)]

        def copy(t, k, block, to):
            ref = outs[t].at[_slot(block)]
            return pltpu.make_async_remote_copy(
                src_ref=ref, dst_ref=ref, send_sem=send_sems.at[4 * t + k], recv_sem=recv_sems.at[4 * t + k],
                device_id=to, device_id_type=MESH)

        early = [copy(t, j, (*chips[j], c), sibling) for t in range(n) for j in range(2)]
        early += [copy(t, 3, me, far) for t in range(n)]
        return c, far, sibling, chips, copy, early

    def start(ins, outs, send_sems, recv_sems, local_sems):
        for cp in parts(outs, send_sems, recv_sems)[-1]:
            cp.start()

    def finish(ins, outs, send_sems, recv_sems, local_sems):
        c, far, sibling, chips, copy, early = parts(outs, send_sems, recv_sems)
        late = []
        for t in range(n):
            copy(t, 3, far, far).wait_recv()
            late.append(copy(t, 2, far, sibling))
            late[-1].start()
        for t in range(n):
            for j in range(3):
                copy(t, j, (*chips[j], 1 - c), sibling).wait_recv()
        for cp in early + late:
            cp.wait_send()

    outs = [jax.ShapeDtypeStruct(g.shape, g.dtype) for g in gathered]
    return _Side(gathered, outs, (4 * n, 4 * n, 0), start, finish, aliases={t: t for t in range(n)})


def _a_in(hn, w8, chip, first, count, name, into=None, sides=(), tm=1408, tn=512):
    lp, k = hn.shape
    c = w8.shape[2]
    tm, tn = _tile(lp, tm, 64), _tile(c, tn)
    per_block = c // tn
    per_part = 2 * per_block

    def where(j, s):
        return jnp.bitwise_xor(s[0], first + j // per_part), j % per_part

    def b_at(i, j, s):
        part, q = where(j, s)
        return (2 * part + q // per_block, 0, q % per_block)

    def o_at(i, j, s):
        part, q = where(j, s)
        return (part, i, q)

    def body(a_ref, b_ref, *rest):
        o_ref = rest[-1]
        o_ref[...] = lax.dot_general(a_ref[...], b_ref[...], NN, preferred_element_type=F32)

    args, in_specs, aliases = [hn, w8], [pl.BlockSpec((tm, k), lambda i, j, s: (i, 0)), pl.BlockSpec((None, k, tn), b_at)], {}
    if into is not None:
        aliases[2] = 0
        args.append(into)
        in_specs.append(_any_spec())
    (out,), side_results = _pallas(
        body, name=name, args=args, grid=(lp // tm, count * per_part), in_specs=in_specs,
        out_specs=[pl.BlockSpec((None, tm, tn), o_at)], out_shape=[jax.ShapeDtypeStruct((4, lp, 2 * c), F32)],
        semantics=("arbitrary", "arbitrary"), vmem_mib=48, sides=sides, aliases=aliases, prefetch=[chip])
    return out, side_results


def _all_gather(blocks, name):
    return _alone(_gather_side(blocks), name)


def _pair_sum(g8, got, name, tm=1024):
    _, r, c = g8.shape
    tm = tm if r % tm == 0 else r
    core = lax.axis_index("c")

    def body(core_ref, mine_ref, got_ref, o_ref):
        south_first = core_ref[0] == 0
        a, b = mine_ref[...], got_ref[...]
        o_ref[...] = (jnp.where(south_first, a, b) + jnp.where(south_first, b, a)).astype(o_ref.dtype)

    return pl.pallas_call(
        body,
        name=name,
        grid_spec=pltpu.PrefetchScalarGridSpec(
            num_scalar_prefetch=1,
            grid=(N_CHIP, r // tm),
            in_specs=[pl.BlockSpec((None, tm, c), lambda j, i, core_ref: (2 * j + core_ref[0], i, 0)),
                      pl.BlockSpec((None, tm, c), lambda j, i, core_ref: (j, i, 0))],
            out_specs=pl.BlockSpec((None, tm, c), lambda j, i, core_ref: (j, i, 0)),
        ),
        out_shape=jax.ShapeDtypeStruct((N_CHIP, r, c), PAYLOAD),
        compiler_params=_cparams(("parallel", "parallel"), 32),
    )(jnp.reshape(core, (1,)).astype(jnp.int32), g8, got)


def _all_reduce_small(s, name="small_all_reduce"):
    r, c = s.shape

    def body(s_ref, o_ref, buf_ref, send_sems, recv_sems):
        x, y, c_ = _place()
        me = 4 * x + 2 * y + c_
        buf_ref[me] = s_ref[...]

        def copy(k, slot, peer):
            return pltpu.make_async_remote_copy(
                src_ref=s_ref, dst_ref=buf_ref.at[slot],
                send_sem=send_sems.at[k - 1], recv_sem=recv_sems.at[k - 1],
                device_id=peer, device_id_type=MESH)

        peers = []
        for k in range(1, N_DEV):
            peer = (x ^ ((k >> 2) & 1), y ^ ((k >> 1) & 1), c_ ^ (k & 1))
            peers.append(peer)
            copy(k, me, peer).start()
        for k, peer in zip(range(1, N_DEV), peers):
            cp = copy(k, 4 * peer[0] + 2 * peer[1] + peer[2], peer)
            cp.wait_send()
            cp.wait_recv()
        total = buf_ref[0]
        for j in range(1, N_DEV):
            total = total + buf_ref[j]
        o_ref[...] = total

    return pl.pallas_call(
        body,
        name=name,
        out_shape=jax.ShapeDtypeStruct((r, c), F32),
        in_specs=[pl.BlockSpec(memory_space=pltpu.VMEM)],
        out_specs=pl.BlockSpec(memory_space=pltpu.VMEM),
        scratch_shapes=[pltpu.VMEM((N_DEV, r, c), F32), pltpu.SemaphoreType.DMA((N_DEV - 1,)),
                        pltpu.SemaphoreType.DMA((N_DEV - 1,))],
    )(s)


def _kv_weights(g_kv):
    d = g_kv.shape[1]
    nh = d // HEAD
    wkv = jnp.reshape(jnp.transpose(g_kv, (1, 0, 2)), (d, -1))
    return wkv[:, :2 * d], jnp.pad(wkv[:, 2 * d:], ((0, 0), (0, 128 - nh)))


def _kv_grad_blocks(dwkv2_t, dwfl_t):
    d = dwkv2_t.shape[1]
    return jnp.reshape(jnp.concatenate([dwkv2_t, dwfl_t[:d // HEAD]], axis=0), (N_DEV, -1, d))


def _local_step(x, target, meta, gamma, a_norm, wa_in, a_out_norm, wa_out, kv_norm, late, b_f, g_k, b_norm, g_q,
                dist):
    d = x.shape[1]
    nh = d // HEAD
    cols = d // N_DEV
    big = dict(tm=1408, tn=512, tk=2048)

    lb = _lb_fwd(gamma)
    if dist:
        chip = jnp.reshape(2 * lax.axis_index("x") + lax.axis_index("y"), (1,)).astype(jnp.int32)
        h0, hn_a, ((wa_in,),) = _embed(x, meta, a_norm, sides=[_gather_near_side([wa_in])])
    else:
        h0, hn_a = _embed(x, meta, a_norm)
    lp = h0.shape[0]
    if dist:
        a_out_blk, kv_blk, b_in_blk, b_out_blk = late
        u4, ((wa_in,),) = _a_in(hn_a, wa_in, chip, 0, 1, "a_in_near", sides=[_gather_far_side([wa_in])])
        u4, (first,) = _a_in(hn_a, wa_in, chip, 1, 3, "a_in_far", into=u4,
                             sides=[_gather_own_side([a_out_blk, b_in_blk])])
        o_a, og_a, ((g_a_out, g_b_in), (g_kv,)) = _hgrn2_fwd(
            u4, lb, a_out_norm, sides=[_gather_pass_side(first), _gather_own_side([kv_blk])])
        wa_out = jnp.reshape(g_a_out, (d, d))
        h1, ((g_kv,), (g_b_out,)) = _matmul(og_a, wa_out, "nn", F32, "a_out", add=h0,
                                            sides=[_gather_pass_side([g_kv]), _gather_own_side([b_out_blk])], **big)
        (wkv2, wfl), wb_in = _kv_weights(g_kv), g_b_in
    else:
        u4 = _matmul(hn_a, wa_in, "nn", F32, "a_in", out_parts=4, **big)
        o_a, og_a = _hgrn2_fwd(u4, lb, a_out_norm)
        h1 = _matmul(og_a, wa_out, "nn", F32, "a_out", add=h0, **big)
        wkv2, wfl, wb_in, wb_out = late
    hk, hb = _rms_fwd(h1, [kv_norm, b_norm], "rms_kv_b")
    if dist:
        ukv, ((g_b_out,),) = _matmul(hk, wkv2, "nn", F32, "kv_in", out_parts=2,
                                     sides=[_gather_pass_side([g_b_out])], **big)
        wb_out = jnp.reshape(g_b_out, (d, d))
    else:
        ukv = _matmul(hk, wkv2, "nn", F32, "kv_in", out_parts=2, **big)
    ufl = _matmul(hk, wfl, "nn", F32, "kv_f", **big)
    ub = _matmul(hb, wb_in, "nn", F32, "b_in", out_parts=2, **big)
    f_cum = _fgate_fwd(ufl, b_f)
    qa, ka, va = _attn_operands(ub, ukv, g_q, g_k, f_cum)
    o_b, o_lo, og_b, qb = _fox_fwd(qa, ka, va, ub)
    h2 = _matmul(og_b, wb_out, "nn", F32, "b_out", add=h1, **big)
    loss, dh2 = _loss(h2, target)

    dx_t = dict(tm=1408, tn=512, tk=2048, vmem_mib=56)
    dx_parts_t = dict(tm=704, tn=512, tk=2048, vmem_mib=58, k_whole=True)
    dw_t = dict(tm=512, tn=1024, tk=lp, vmem_mib=58)
    dw_f32_t = dict(tm=1024, tn=512, tk=lp, vmem_mib=58)
    def to_sibling(g8):
        return [_sibling_side([g8])] if dist else []

    def to_chips(partial):
        return [_chips_side([partial])] if dist else []

    def unpack(res, n_sides):
        if not dist:
            return res, [None] * n_sides
        *main, side_results = res
        return (main[0] if len(main) == 1 else tuple(main)), [r[0] for r in side_results]

    dwb_out = _matmul(og_b, dh2, "tn", F32, "b_out_dw", **dw_f32_t)
    g8_b_out = jnp.reshape(dwb_out, (N_DEV, cols, d))
    d_ogb, (got,) = unpack(_matmul(dh2, wb_out, "nt", F32, "b_out_dx", sides=to_sibling(g8_b_out), **dx_t), 1)
    p_b_out = _pair_sum(g8_b_out, got, "pair_sum_b_w_out") if dist else None
    doa = _fox_do(d_ogb, o_b, o_lo, ub)
    (dq, dg_q, dukv, dg_k, dfk), (r_b_out,) = unpack(
        _fox_bwd(qb, ka, va, doa, ukv, g_k, ub, g_q, sides=to_chips(p_b_out)), 1)
    dub = _fox_gate_bwd(ub, d_ogb, o_b, dq)
    d_f = jnp.pad(jnp.transpose(dfk[:, 0, :]), ((0, 0), (0, 128 - nh)))
    dufl, db_f = _fgate_bwd(ufl, b_f, d_f)
    dwb_in = _matmul(hb, dub, "tn", F32, "b_in_dw", out_parts=N_DEV, **dw_t)
    d_hb, (got,) = unpack(_matmul(dub, wb_in, "nt", F32, "b_in_dx", sides=to_sibling(dwb_in), **dx_parts_t), 1)
    p_b_in = _pair_sum(dwb_in, got, "pair_sum_b_w_in") if dist else None
    d_hk, (r_b_in,) = unpack(_matmul(dukv, wkv2, "nt", F32, "kv_dx", sides=to_chips(p_b_in), **dx_parts_t), 1)
    d_hk = _matmul(dufl, wfl, "nt", F32, "kv_f_dx", add=d_hk, **dx_t)
    dwkv2_t = _matmul(dukv, hk, "tn", F32, "kv_dw", **dw_t)
    dwfl_t = _matmul(dufl, hk, "tn", F32, "kv_f_dw", **dw_t)
    g8_kv = _kv_grad_blocks(dwkv2_t, dwfl_t) if dist else None
    dh1, (dg_kv, dg_b) = _rms_bwd(h1, [kv_norm, b_norm], [d_hk, d_hb], dh2, "rms_kv_b_bwd")
    d_oga, (got,) = unpack(_matmul(dh1, wa_out, "nt", F32, "a_out_dx", sides=to_sibling(g8_kv), **dx_t), 1)
    p_kv = _pair_sum(g8_kv, got, "pair_sum_kv_w") if dist else None
    dwa_out = _matmul(og_a, dh1, "tn", F32, "a_out_dw", **dw_f32_t)
    g8_a_out = jnp.reshape(dwa_out, (N_DEV, cols, d))
    (du4, dlb, dg_aout), (r_kv, got) = unpack(
        _hgrn2_bwd(u4, o_a, d_oga, lb, a_out_norm, sides=to_chips(p_kv) + to_sibling(g8_a_out)), 2)
    p_a_out = _pair_sum(g8_a_out, got, "pair_sum_a_w_out") if dist else None
    dwa_in, (r_a_out,) = unpack(
        _matmul(hn_a, du4, "tn", F32, "a_in_dw", out_parts=N_DEV, sides=to_chips(p_a_out), **dw_t), 1)
    row_tiles = lp // _tile(lp, dx_parts_t["tm"], 64)
    if dist and row_tiles >= 2:
        top = max(1, row_tiles // 6)
        d_hna, ((got,),) = _matmul(du4, wa_in, "nt", F32, "a_in_dx_top", rows=(0, top),
                                   sides=to_sibling(dwa_in), **dx_parts_t)
        p_a_in = _pair_sum(dwa_in, got, "pair_sum_a_w_in")
        d_hna, ((r_a_in,),) = _matmul(du4, wa_in, "nt", F32, "a_in_dx_bottom", rows=(top, row_tiles - top), into=d_hna,
                                      sides=to_chips(p_a_in), **dx_parts_t)
    else:
        d_hna, (got,) = unpack(_matmul(du4, wa_in, "nt", F32, "a_in_dx", sides=to_sibling(dwa_in), **dx_parts_t), 1)
        if dist:
            (r_a_in,) = _alone(_chips_side([_pair_sum(dwa_in, got, "pair_sum_a_w_in")]), "grads_to_chips_a_w_in")
    grad_x, dmeta, dg_a = _embed_bwd(h0, a_norm, d_hna, dh1)
    dgamma = _lb_bwd(gamma, dlb)

    grads = dict(meta=dmeta, gamma=dgamma, a_norm=dg_a, a_out_norm=dg_aout, kv_norm=dg_kv,
                 b_f=db_f, g_k=dg_k, b_norm=dg_b, g_q=dg_q)
    if dist:
        grads.update(wa_in=r_a_in, wa_out=r_a_out, wkv=r_kv, wb_in=r_b_in, wb_out=r_b_out)
    else:
        grads.update(wa_in=dwa_in, wa_out=dwa_out, wkv2=dwkv2_t.T, wfl=dwfl_t.T, wb_in=dwb_in, wb_out=dwb_out)
    return loss, grad_x, grads


def kernel(x, meta, gamma_lb, a_norm, a_w_in, a_out_norm, a_w_out, kv_norm, kv_w, fox_b_f, fox_k_norm, b_norm, b_w_in, b_q_norm, b_w_out, loss_target, m_meta, m_gamma_lb, m_a_norm, m_a_w_in, m_a_out_norm, m_a_w_out, m_kv_norm, m_kv_w, m_fox_b_f, m_fox_k_norm, m_b_norm, m_b_w_in, m_b_q_norm, m_b_w_out, v_meta, v_gamma_lb, v_a_norm, v_a_w_in, v_a_out_norm, v_a_w_out, v_kv_norm, v_kv_w, v_fox_b_f, v_fox_k_norm, v_b_norm, v_b_w_in, v_b_q_norm, v_b_w_out):
    d = x.shape[-1]
    nh = d // HEAD
    cols = d // N_DEV
    me = 4 * lax.axis_index("x") + 2 * lax.axis_index("y") + lax.axis_index("c")

    sharded_small = jnp.concatenate([meta, gamma_lb, a_norm, a_out_norm, jnp.zeros((4, cols), F32)], axis=0)
    (g_small,) = _all_gather([sharded_small], "gather_small")
    g_a_in, wa_out = a_w_in[0].astype(MXU), None
    late = (a_w_out[0].astype(MXU), kv_w.astype(MXU), b_w_in[0].astype(MXU), b_w_out[0].astype(MXU))
    small = jnp.reshape(jnp.transpose(g_small, (1, 0, 2)), (-1, d))
    meta_f, gamma_f, a_norm_f, a_out_norm_f = small[:16], small[16:18], small[18:19], small[19:20]
    b_f = jnp.pad(jnp.reshape(fox_b_f, (1, nh)), ((0, 0), (0, 128 - nh)))
    g_k = jnp.reshape(fox_k_norm, (1, d))
    g_q = jnp.reshape(b_q_norm, (1, d))
    kv_norm_r = jnp.reshape(kv_norm, (1, d))

    loss, grad_x, g = _local_step(x[0], loss_target[0], meta_f, gamma_f, a_norm_f, g_a_in, a_out_norm_f, wa_out,
                                  kv_norm_r, late, b_f, g_k, b_norm, g_q, dist=True)
    loss = lax.psum(loss[0, 0], AXES)

    r_a_in = _adamw(g["wa_in"], a_w_in[0], m_a_w_in[0], v_a_w_in[0], "adamw_a_w_in")
    r_a_out = _adamw(g["wa_out"], a_w_out[0], m_a_w_out[0], v_a_w_out[0], "adamw_a_w_out")
    r_kv = [jnp.transpose(r) for r in _adamw(g["wkv"], kv_w.T, m_kv_w.T, v_kv_w.T, "adamw_kv_w")]
    r_b_in = _adamw(g["wb_in"], b_w_in[0], m_b_w_in[0], v_b_w_in[0], "adamw_b_w_in")
    r_b_out = _adamw(g["wb_out"], b_w_out[0], m_b_w_out[0], v_b_w_out[0], "adamw_b_w_out")

    packed = jnp.concatenate(
        [g["meta"], g["gamma"], g["a_norm"], g["a_out_norm"], g["kv_norm"], g["b_norm"], g["g_k"], g["g_q"],
         jnp.pad(g["b_f"], ((0, 0), (0, d - 128))), jnp.zeros((7, d), F32)], axis=0)
    tot = _all_reduce_small(packed)
    mine = lax.dynamic_slice_in_dim(tot[:20], me * cols, cols, axis=1)
    gs = [mine[:16], mine[16:18], mine[18:19], mine[19:20], tot[20], tot[21:22], jnp.reshape(tot[22], (nh, HEAD)),
          jnp.reshape(tot[23], (1, nh, HEAD)), tot[24, :nh]]
    small_w = [meta, gamma_lb, a_norm, a_out_norm, kv_norm, b_norm, fox_k_norm, b_q_norm, fox_b_f]
    small_m = [m_meta, m_gamma_lb, m_a_norm, m_a_out_norm, m_kv_norm, m_b_norm, m_fox_k_norm, m_b_q_norm, m_fox_b_f]
    small_v = [v_meta, v_gamma_lb, v_a_norm, v_a_out_norm, v_kv_norm, v_b_norm, v_fox_k_norm, v_b_q_norm, v_fox_b_f]

    d_s, m_s, v_s = _adamw_small(gs, small_w, small_m, small_v)

    def ordered(s, a_in, a_out, kv, b_in, b_out):
        return [s[0], s[1], s[2], a_in[None], s[3], a_out[None], s[4], kv, s[8], s[6], s[5], b_in[None], s[7], b_out[None]]

    outs = []
    for i, s in enumerate([gs, d_s, m_s, v_s]):
        outs += ordered(s, r_a_in[i], r_a_out[i], r_kv[i], r_b_in[i], r_b_out[i])
    return (loss, grad_x[None], *outs)
```

```python
import math

import jax
import jax.numpy as jnp
from jax import lax
from jax.experimental import pallas as pl
from jax.experimental.pallas import tpu as pltpu

HEAD = 128
CHUNK = 64
N_META = 16
ROW_PAD = 128 - N_META
EPS = 1e-6
MASK_VALUE = -1e30
ADAM_LR = 0.001
ADAM_B1 = 0.9
ADAM_B2 = 0.999
ADAM_EPS = 1e-08
ADAM_WD = 0.01
ADAM_STEP = 10
N_DEV = 8
N_CHIP = 4
MIB = 1024 * 1024
AXES = ("x", "y", "c")
MESH = pl.DeviceIdType.MESH

F32 = jnp.float32
MXU = jnp.bfloat16
PAYLOAD = jnp.bfloat16
HI = lax.Precision.HIGHEST

NN = (((1,), (0,)), ((), ()))
NT = (((1,), (1,)), ((), ()))
TN = (((0,), (0,)), ((), ()))


def _dot(a, b, dims=NN):
    return lax.dot_general(a.astype(MXU), b.astype(MXU), dims, preferred_element_type=F32)


def _dot_exact(a, b):
    return lax.dot_general(a, b, NN, precision=HI, preferred_element_type=F32)


def _sigmoid(x):
    return 1.0 / (1.0 + jnp.exp(-x))


def _tile(dim, target, unit=128):
    best = None
    t = unit
    while t <= min(dim, target):
        if dim % t == 0:
            best = t
        t += unit
    return best if best is not None else dim


def _cparams(semantics, vmem_mib):
    return pltpu.CompilerParams(dimension_semantics=semantics, vmem_limit_bytes=vmem_mib * MIB)


def _mat_spec(arr, br, bc, rc_of_grid):
    if arr.ndim == 2:
        return pl.BlockSpec((br, bc), rc_of_grid)
    assert arr.shape[2] % bc == 0, (arr.shape, bc)
    per = arr.shape[2] // bc

    def idx(*g):
        r, c = rc_of_grid(*g)
        return (c // per, r, c % per)

    return pl.BlockSpec((None, br, bc), idx)


def _mat_shape(arr):
    return (arr.shape[0], arr.shape[1]) if arr.ndim == 2 else (arr.shape[1], arr.shape[0] * arr.shape[2])


def _matmul(a, b, dims, out_dtype, name, *, add=None, out_parts=1, tm=512, tn=512, tk=512, vmem_mib=48, sides=(),
            k_whole=False, rows=None, into=None):
    ar, ac = _mat_shape(a)
    br_, bc_ = _mat_shape(b)
    if dims == "nn":
        m, k, n = ar, ac, bc_
        assert br_ == k
    elif dims == "nt":
        m, k, n = ar, ac, br_
        assert bc_ == k
    else:
        m, k, n = ac, ar, bc_
        assert br_ == k
    m_unit, n_unit, k_unit = m, n, k
    if a.ndim == 3:
        if dims == "tn":
            m_unit = math.gcd(m_unit, a.shape[2])
        else:
            k_unit = math.gcd(k_unit, a.shape[2])
    if b.ndim == 3:
        if dims == "nt":
            k_unit = math.gcd(k_unit, b.shape[2])
        else:
            n_unit = math.gcd(n_unit, b.shape[2])
    if out_parts > 1:
        n_unit = math.gcd(n_unit, n // out_parts)
    tm = _tile(m_unit, tm, 128 if dims == "tn" else 64)
    tn, tk = _tile(n_unit, tn), _tile(k_unit, tk)
    whole_k = dims == "nt" and tk < k and k_whole
    if whole_k:
        k_chunk, tk = tk, k
    gm, gn, gk = m // tm, n // tn, k // tk
    assert gm * tm == m and gn * tn == n and gk * tk == k, (name, m, n, k, tm, tn, tk)

    def chunk_of(ref, arr, c):
        if arr.ndim == 2:
            return ref[:, c * k_chunk:(c + 1) * k_chunk]
        per = arr.shape[2] // k_chunk
        return ref[c // per, :, (c % per) * k_chunk:(c % per + 1) * k_chunk]

    def all_cols(arr, rows, row_of_grid):
        if arr.ndim == 2:
            return pl.BlockSpec((rows, arr.shape[1]), lambda i, j, kk: (row_of_grid(i, j), 0))
        return pl.BlockSpec((arr.shape[0], rows, arr.shape[2]), lambda i, j, kk: (0, row_of_grid(i, j), 0))

    if dims == "nn":
        a_spec = _mat_spec(a, tm, tk, lambda i, j, kk: (i, kk))
        b_spec = _mat_spec(b, tk, tn, lambda i, j, kk: (kk, j))
        dn = NN
    elif whole_k:
        a_spec = all_cols(a, tm, lambda i, j: i)
        b_spec = all_cols(b, tn, lambda i, j: j)
        dn = NT
    elif dims == "nt":
        a_spec = _mat_spec(a, tm, tk, lambda i, j, kk: (i, kk))
        b_spec = _mat_spec(b, tn, tk, lambda i, j, kk: (j, kk))
        dn = NT
    else:
        a_spec = _mat_spec(a, tk, tm, lambda i, j, kk: (kk, i))
        b_spec = _mat_spec(b, tk, tn, lambda i, j, kk: (kk, j))
        dn = TN

    if out_parts > 1:
        per = (n // out_parts) // tn
        out_shape = jax.ShapeDtypeStruct((out_parts, m, n // out_parts), out_dtype)
        o_spec = pl.BlockSpec((None, tm, tn), lambda i, j, kk: (j // per, i, j % per))
    else:
        out_shape = jax.ShapeDtypeStruct((m, n), out_dtype)
        o_spec = pl.BlockSpec((tm, tn), lambda i, j, kk: (i, j))

    in_specs = [a_spec, b_spec]
    args = [a, b]
    if add is not None:
        in_specs.append(pl.BlockSpec((tm, tn), lambda i, j, kk: (i, j)))
        args.append(add)
    if rows is not None:
        first_tile, gm = rows

        def shifted(spec):
            return pl.BlockSpec(spec.block_shape, lambda i, j, kk: spec.index_map(i + first_tile, j, kk))

        in_specs = [shifted(s) for s in in_specs]
        o_spec = shifted(o_spec)
    aliases = {}
    if into is not None:
        aliases[len(args)] = 0
        in_specs.append(_any_spec())
        args.append(into)

    def body(*refs):
        a_ref, b_ref, *others, o_ref, acc_ref = refs
        add_ref = others[0] if add is not None else None
        kk = pl.program_id(2)
        if whole_k:
            part = sum(lax.dot_general(chunk_of(a_ref, a, c).astype(MXU), chunk_of(b_ref, b, c).astype(MXU), dn,
                                       preferred_element_type=F32) for c in range(k // k_chunk))
        else:
            part = lax.dot_general(a_ref[...].astype(MXU), b_ref[...].astype(MXU), dn, preferred_element_type=F32)

        def finish(total):
            if add_ref is not None:
                total = total + add_ref[...]
            o_ref[...] = total.astype(o_ref.dtype)

        if gk == 1:
            finish(part)
        else:
            @pl.when(kk == 0)
            def _():
                acc_ref[...] = part

            @pl.when(jnp.logical_and(kk > 0, kk < gk - 1))
            def _():
                acc_ref[...] += part

            @pl.when(kk == gk - 1)
            def _():
                finish(acc_ref[...] + part)

    (out,), side_results = _pallas(
        body,
        name=name,
        args=args,
        grid=(gm, gn, gk),
        in_specs=in_specs,
        out_specs=[o_spec],
        out_shape=[out_shape],
        scratch_shapes=[pltpu.VMEM((tm, tn) if gk > 1 else (8, 128), F32)],
        semantics=("parallel", "parallel", "arbitrary"),
        vmem_mib=vmem_mib,
        sides=sides,
        aliases=aliases,
    )
    return (out, side_results) if sides else out


def _rms_fwd(h, gains, name, tm=384):
    lp, d = h.shape
    tm = _tile(lp, tm)
    n = len(gains)

    def body(*refs):
        h_ref = refs[0]
        g_refs = refs[1:1 + n]
        o_refs = refs[1 + n:]
        x = h_ref[...]
        y = x * lax.rsqrt(jnp.mean(x * x, axis=-1, keepdims=True) + EPS)
        for g_ref, o_ref in zip(g_refs, o_refs):
            o_ref[...] = (y * g_ref[...]).astype(o_ref.dtype)

    row = pl.BlockSpec((tm, d), lambda i: (i, 0))
    vec = pl.BlockSpec((1, d), lambda i: (0, 0))
    return pl.pallas_call(
        body,
        name=name,
        grid=(lp // tm,),
        in_specs=[row] + [vec] * n,
        out_specs=[row] * n,
        out_shape=[jax.ShapeDtypeStruct((lp, d), MXU)] * n,
        compiler_params=_cparams(("parallel",), 40),
    )(h, *gains)


def _embed(x, meta, gain, name="embed", sides=()):
    seq, d = x.shape
    lp = ROW_PAD + N_META + seq
    blk = ROW_PAD + N_META

    def body(x_ref, meta_ref, g_ref, h_ref, o_ref):
        i = pl.program_id(0)

        @pl.when(i == 0)
        def _():
            h_ref[0:ROW_PAD, :] = jnp.zeros((ROW_PAD, d), F32)
            h_ref[ROW_PAD:blk, :] = meta_ref[...]

        @pl.when(i > 0)
        def _():
            h_ref[...] = x_ref[...]

        h = h_ref[...]
        y = h * lax.rsqrt(jnp.mean(h * h, axis=-1, keepdims=True) + EPS)
        o_ref[...] = (y * g_ref[...]).astype(o_ref.dtype)

    row = pl.BlockSpec((blk, d), lambda i: (i, 0))
    outs, side_results = _pallas(
        body,
        name=name,
        args=(x, meta, gain),
        grid=(lp // blk,),
        in_specs=[pl.BlockSpec((blk, d), lambda i: (jnp.maximum(i - 1, 0), 0)),
                  pl.BlockSpec((N_META, d), lambda i: (0, 0)), pl.BlockSpec((1, d), lambda i: (0, 0))],
        out_specs=[row, row],
        out_shape=[jax.ShapeDtypeStruct((lp, d), F32), jax.ShapeDtypeStruct((lp, d), MXU)],
        semantics=("arbitrary",),
        vmem_mib=32,
        sides=sides,
    )
    return (*outs, side_results) if sides else tuple(outs)


def _embed_bwd(h, gain, dy, res, name="embed_bwd"):
    lp, d = h.shape
    blk = ROW_PAD + N_META

    def body(h_ref, res_ref, g_ref, dy_ref, dx_ref, dmeta_ref, dg_ref):
        i = pl.program_id(0)
        x = h_ref[...]
        rstd = lax.rsqrt(jnp.mean(x * x, axis=-1, keepdims=True) + EPS)
        xhat = x * rstd
        dy = dy_ref[...]
        gdy = dy * g_ref[...]
        dh = res_ref[...] + rstd * (gdy - xhat * jnp.mean(gdy * xhat, axis=-1, keepdims=True))
        part = jnp.sum(dy * xhat, axis=0, keepdims=True)

        @pl.when(i == 0)
        def _():
            dmeta_ref[...] = dh[ROW_PAD:blk, :]
            dg_ref[...] = part

        @pl.when(i > 0)
        def _():
            dx_ref[...] = dh
            dg_ref[...] += part

    row = pl.BlockSpec((blk, d), lambda i: (i, 0))
    vec = pl.BlockSpec((1, d), lambda i: (0, 0))
    return pl.pallas_call(
        body,
        name=name,
        grid=(lp // blk,),
        in_specs=[row, row, vec, row],
        out_specs=[pl.BlockSpec((blk, d), lambda i: (jnp.maximum(i - 1, 0), 0)),
                   pl.BlockSpec((N_META, d), lambda i: (0, 0)), vec],
        out_shape=[jax.ShapeDtypeStruct((lp - blk, d), F32), jax.ShapeDtypeStruct((N_META, d), F32),
                   jax.ShapeDtypeStruct((1, d), F32)],
        compiler_params=_cparams(("arbitrary",), 32),
    )(h, res, gain, dy)


def _rms_bwd(h, gains, dys, res, name, tm=384):
    lp, d = h.shape
    tm = _tile(lp, tm)
    n = len(gains)

    def body(*refs):
        h_ref, res_ref = refs[0], refs[1]
        g_refs = refs[2:2 + n]
        dy_refs = refs[2 + n:2 + 2 * n]
        dh_ref = refs[2 + 2 * n]
        dg_refs = refs[3 + 2 * n:]
        i = pl.program_id(0)
        x = h_ref[...]
        rstd = lax.rsqrt(jnp.mean(x * x, axis=-1, keepdims=True) + EPS)
        xhat = x * rstd
        dh = res_ref[...]
        for g_ref, dy_ref, dg_ref in zip(g_refs, dy_refs, dg_refs):
            dy = dy_ref[...]
            gdy = dy * g_ref[...]
            dh = dh + rstd * (gdy - xhat * jnp.mean(gdy * xhat, axis=-1, keepdims=True))
            part = jnp.sum(dy * xhat, axis=0, keepdims=True)

            @pl.when(i == 0)
            def _():
                dg_ref[...] = part

            @pl.when(i > 0)
            def _():
                dg_ref[...] += part

        dh_ref[...] = dh

    row = pl.BlockSpec((tm, d), lambda i: (i, 0))
    vec = pl.BlockSpec((1, d), lambda i: (0, 0))
    outs = pl.pallas_call(
        body,
        name=name,
        grid=(lp // tm,),
        in_specs=[row, row] + [vec] * n + [row] * n,
        out_specs=[row] + [vec] * n,
        out_shape=[jax.ShapeDtypeStruct((lp, d), F32)] + [jax.ShapeDtypeStruct((1, d), F32)] * n,
        compiler_params=_cparams(("arbitrary",), 56),
    )(h, res, *gains, *dys)
    return outs[0], list(outs[1:])


def _lb_fwd(gamma):
    def body(g_ref, lb_ref):
        g = g_ref[...]
        e = jnp.exp(g - jnp.max(g, axis=0, keepdims=True))
        lb_ref[...] = (e / jnp.sum(e, axis=0, keepdims=True))[0:1, :]

    return pl.pallas_call(body, name="lb_fwd", out_shape=jax.ShapeDtypeStruct((1, gamma.shape[1]), F32))(gamma)


def _lb_bwd(gamma, dlb):
    def body(g_ref, dlb_ref, dg_ref):
        g = g_ref[...]
        e = jnp.exp(g - jnp.max(g, axis=0, keepdims=True))
        s = e / jnp.sum(e, axis=0, keepdims=True)
        rows = lax.broadcasted_iota(jnp.int32, g.shape, 0)
        ds = jnp.where(rows == 0, dlb_ref[...], 0.0)
        dg_ref[...] = s * (ds - jnp.sum(s * ds, axis=0, keepdims=True))

    return pl.pallas_call(body, name="lb_bwd", out_shape=jax.ShapeDtypeStruct(gamma.shape, F32))(gamma, dlb)


def _tri(n, lower):
    r = lax.broadcasted_iota(jnp.int32, (n, n), 0)
    c = lax.broadcasted_iota(jnp.int32, (n, n), 1)
    return jnp.where((r >= c) if lower else (r <= c), 1.0, 0.0).astype(F32)


def _group(nc, most):
    return max(u for u in range(1, most + 1) if nc % u == 0)


def _running_sum(tri, x):
    hi = x.astype(MXU)
    rest = x - hi.astype(F32)
    mid = rest.astype(MXU)
    lo = (rest - mid.astype(F32)).astype(MXU)
    return _dot(tri, hi) + _dot(tri, mid) + _dot(tri, lo)


def _causal(n):
    r = lax.broadcasted_iota(jnp.int32, (n, n), 0)
    c = lax.broadcasted_iota(jnp.int32, (n, n), 1)
    return r >= c


def _chunk_gates(u_ref, lb, c):
    sl = pl.ds(pl.multiple_of(c * CHUNK, CHUNK), CHUNK)
    valid = (c * CHUNK + lax.broadcasted_iota(jnp.int32, (CHUNK, HEAD), 0)) >= ROW_PAD
    uq = u_ref[0, sl, :]
    uf = u_ref[1, sl, :]
    sq = _sigmoid(uq)
    sf = _sigmoid(uf)
    fg = lb + (1.0 - lb) * sf
    return dict(sl=sl, valid=valid, uq=uq, sq=sq, sf=sf, fg=fg, q=jnp.where(valid, uq * sq, 0.0),
                logf=jnp.where(valid, jnp.log(fg), 0.0), k=jnp.where(valid, 1.0 - fg, 0.0),
                v=jnp.where(valid, u_ref[2, sl, :], 0.0))


def _chunk_decays(x, b):
    b_last = b[CHUNK - 1:CHUNK, :]
    b_mid = b[CHUNK // 2 - 1:CHUNK // 2, :]
    e_qi = jnp.exp(b - b_mid)
    e_ki = jnp.exp(b_mid - b)
    e_kd = jnp.exp(b_last - b)
    e_qe = jnp.exp(b)
    q, k = x["q"], x["k"]
    return dict(x, e_qi=e_qi, e_ki=e_ki, e_kd=e_kd, e_qe=e_qe, qi=q * e_qi, ki=k * e_ki, kd=k * e_kd, qe=q * e_qe,
                decay=jnp.exp(b_last))


def _chunks(u_ref, lb, tri_lower, cs):
    gates = [_chunk_gates(u_ref, lb, c) for c in cs]
    sums = [_running_sum(tri_lower, x["logf"]) for x in gates]
    return [_chunk_decays(x, b) for x, b in zip(gates, sums)]


def _hgrn2_fwd(u4, lb, g_out, name="hgrn2_fwd", sides=()):
    _, lp, d = u4.shape
    nh, nc = d // HEAD, lp // CHUNK
    per = _group(nc, 22)

    def body(u_ref, lb_ref, g_ref, o_ref, og_ref):
        lb_v = lb_ref[...]
        g = g_ref[...]

        tri_lower = _tri(CHUNK, True).astype(MXU)
        causal = _causal(CHUNK)

        def step(i, st):
            xs = _chunks(u_ref, lb_v, tri_lower, [i * per + u for u in range(per)])
            scores = [_dot(x["qi"], x["ki"], NT) for x in xs]
            updates = [_dot(x["v"], x["kd"], TN) for x in xs]
            states = []
            for x, upd in zip(xs, updates):
                states.append(st)
                st = x["decay"] * st + upd
            outs = [_dot(jnp.where(causal, a, 0.0), x["v"]) + _dot(x["qe"], s, NT)
                    for x, a, s in zip(xs, scores, states)]
            for x, o in zip(xs, outs):
                o_ref[x["sl"], :] = o
                on = o * lax.rsqrt(jnp.mean(o * o, axis=-1, keepdims=True) + EPS) * g
                z = u_ref[3, x["sl"], :]
                og_ref[x["sl"], :] = (on * (z * _sigmoid(z))).astype(og_ref.dtype)
            return st

        lax.fori_loop(0, nc // per, step, jnp.zeros((HEAD, HEAD), F32))

    slab = pl.BlockSpec((lp, HEAD), lambda h: (0, h))
    vec = pl.BlockSpec((1, HEAD), lambda h: (0, h))
    outs, side_results = _pallas(
        body,
        name=name,
        args=(u4, lb, g_out),
        grid=(nh,),
        in_specs=[pl.BlockSpec((4, lp, HEAD), lambda h: (0, 0, h)), vec, vec],
        out_specs=[slab, slab],
        out_shape=[jax.ShapeDtypeStruct((lp, d), F32), jax.ShapeDtypeStruct((lp, d), MXU)],
        semantics=("parallel",),
        vmem_mib=48,
        sides=sides,
    )
    return (*outs, side_results) if sides else tuple(outs)


def _hgrn2_bwd(u4, o, d_og, lb, g_out, name="hgrn2_bwd", sides=()):
    _, lp, d = u4.shape
    nh, nc = d // HEAD, lp // CHUNK
    per_f, per = _group(nc, 6), _group(nc, 11)

    def body(u_ref, o_ref, dog_ref, lb_ref, g_ref, du_ref, dlb_ref, dg_ref, st_ref, do_ref):
        lb_v = lb_ref[...]
        g = g_ref[...]

        tri_lower = _tri(CHUNK, True).astype(MXU)
        tri_upper = _tri(CHUNK, False).astype(MXU)
        causal = _causal(CHUNK)

        def fwd_step(i, carry):
            st, dg_acc = carry
            cs = [i * per_f + u for u in range(per_f)]
            xs = _chunks(u_ref, lb_v, tri_lower, cs)
            updates = [_dot(x["v"], x["kd"], TN) for x in xs]
            for c, x, upd in zip(cs, xs, updates):
                st_ref[c] = st
                st = x["decay"] * st + upd
            for x in xs:
                sl = x["sl"]
                ov = o_ref[sl, :]
                rstd = lax.rsqrt(jnp.mean(ov * ov, axis=-1, keepdims=True) + EPS)
                on = ov * rstd
                z = u_ref[3, sl, :]
                sz = _sigmoid(z)
                dog = dog_ref[sl, :]
                dy = dog * (z * sz)
                dz = dog * (on * g) * (sz * (1.0 + z * (1.0 - sz)))
                du_ref[3, sl, :] = dz.astype(du_ref.dtype)
                gdy = dy * g
                do = rstd * (gdy - on * jnp.mean(gdy * on, axis=-1, keepdims=True))
                do_ref[sl, :] = jnp.where(x["valid"], do, 0.0)
                dg_acc = dg_acc + jnp.sum(dy * on, axis=0, keepdims=True)
            return st, dg_acc

        _, dg_tot = lax.fori_loop(0, nc // per_f, fwd_step, (jnp.zeros((HEAD, HEAD), F32), jnp.zeros((1, HEAD), F32)))
        dg_ref[...] = dg_tot

        def bwd_step(i, carry):
            gt, dlb_acc = carry
            cs = [nc - 1 - (i * per + u) for u in range(per)]
            xs = _chunks(u_ref, lb_v, tri_lower, cs)
            dos = [do_ref[x["sl"], :] for x in xs]
            sts = [st_ref[c] for c in cs]
            scores = [jnp.where(causal, _dot(x["qi"], x["ki"], NT), 0.0) for x in xs]
            d_scores = [jnp.where(causal, _dot(do, x["v"], NT), 0.0) for x, do in zip(xs, dos)]
            d_qes = [_dot(do, st) for do, st in zip(dos, sts)]
            g_updates = [_dot(do, x["qe"], TN) for x, do in zip(xs, dos)]
            gts = []
            for x, upd in zip(xs, g_updates):
                gts.append(gt)
                gt = x["decay"] * gt + upd
            d_kds = [_dot(x["v"], g_) for x, g_ in zip(xs, gts)]
            dvs = [_dot(x["kd"], g_, NT) + _dot(a, do, TN) for x, g_, a, do in zip(xs, gts, scores, dos)]
            d_qis = [_dot(d_a, x["ki"]) for x, d_a in zip(xs, d_scores)]
            d_kis = [_dot(d_a, x["qi"], TN) for x, d_a in zip(xs, d_scores)]
            rows = lax.broadcasted_iota(jnp.int32, (CHUNK, HEAD), 0)
            dbs = []
            for x, g_, st, d_qi, d_ki, d_qe, d_kd in zip(xs, gts, sts, d_qis, d_kis, d_qes, d_kds):
                t_qi, t_ki, t_qe, t_kd = d_qi * x["qi"], d_ki * x["ki"], d_qe * x["qe"], d_kd * x["kd"]
                d_decay = jnp.sum(g_ * st, axis=0, keepdims=True)
                d_mid = jnp.sum(t_ki - t_qi, axis=0, keepdims=True)
                d_last = jnp.sum(t_kd, axis=0, keepdims=True) + d_decay * x["decay"]
                dbs.append(t_qi - t_ki + t_qe - t_kd + jnp.where(rows == CHUNK // 2 - 1, d_mid, 0.0)
                           + jnp.where(rows == CHUNK - 1, d_last, 0.0))
            dlogfs = [_running_sum(tri_upper, db) for db in dbs]
            for x, dlogf, dv, d_qi, d_ki, d_qe, d_kd in zip(xs, dlogfs, dvs, d_qis, d_kis, d_qes, d_kds):
                sl = x["sl"]
                dq = d_qi * x["e_qi"] + d_qe * x["e_qe"]
                dk = d_ki * x["e_ki"] + d_kd * x["e_kd"]
                valid, sq, sf, uq = x["valid"], x["sq"], x["sf"], x["uq"]
                dfg = jnp.where(valid, dlogf / x["fg"] - dk, 0.0)
                du_ref[0, sl, :] = jnp.where(valid, dq * (sq * (1.0 + uq * (1.0 - sq))), 0.0).astype(du_ref.dtype)
                du_ref[1, sl, :] = (dfg * (1.0 - lb_v) * (sf * (1.0 - sf))).astype(du_ref.dtype)
                du_ref[2, sl, :] = jnp.where(valid, dv, 0.0).astype(du_ref.dtype)
                dlb_acc = dlb_acc + jnp.sum(dfg * (1.0 - sf), axis=0, keepdims=True)
            return gt, dlb_acc

        _, dlb_tot = lax.fori_loop(0, nc // per, bwd_step, (jnp.zeros((HEAD, HEAD), F32), jnp.zeros((1, HEAD), F32)))
        dlb_ref[...] = dlb_tot

    slab = pl.BlockSpec((lp, HEAD), lambda h: (0, h))
    vec = pl.BlockSpec((1, HEAD), lambda h: (0, h))
    quad = pl.BlockSpec((4, lp, HEAD), lambda h: (0, 0, h))
    outs, side_results = _pallas(
        body,
        name=name,
        args=(u4, o, d_og, lb, g_out),
        grid=(nh,),
        in_specs=[quad, slab, slab, vec, vec],
        out_specs=[quad, vec, vec],
        out_shape=[jax.ShapeDtypeStruct((4, lp, d), MXU), jax.ShapeDtypeStruct((1, d), F32),
                   jax.ShapeDtypeStruct((1, d), F32)],
        scratch_shapes=[pltpu.VMEM((nc, HEAD, HEAD), F32), pltpu.VMEM((lp, HEAD), F32)],
        semantics=("parallel",),
        vmem_mib=58,
        sides=sides,
    )
    return (*outs, side_results) if sides else tuple(outs)


WIDE = 2 * HEAD
INV_SCALE = HEAD ** 0.5


def _split3(x):
    hi = x.astype(MXU).astype(F32)
    rest = x - hi
    mid = rest.astype(MXU).astype(F32)
    return hi, mid, (rest - mid).astype(MXU).astype(F32)


def _extra_cols(rows, first, second):
    lane = lax.broadcasted_iota(jnp.int32, (rows, HEAD), 1)
    out = jnp.where(lane < 6, 1.0, 0.0).astype(F32)
    for base, terms in ((0, first), (3, second)):
        if terms is not None:
            for j, term in enumerate(terms):
                out = jnp.where(lane == base + j, term, out)
    return out


def _head_col(a, h):
    lane = lax.broadcasted_iota(jnp.int32, a.shape, 1)
    return jnp.sum(jnp.where(lane == h, a, 0.0), axis=-1, keepdims=True)


def _attn_operands(ub, ukv, g_q, g_k, f_cum, name="attn_operands", tm=384):
    _, lp, d = ub.shape
    tm = _tile(lp, tm)
    nh = d // HEAD

    def body(q_ref, k_ref, v_ref, gq_ref, gk_ref, f_ref, qa_ref, ka_ref, va_ref):
        i = pl.program_id(0)
        f = f_ref[...]
        is_pad = (i * tm + lax.broadcasted_iota(jnp.int32, (tm, 1), 0)) < ROW_PAD
        ones_only = _extra_cols(tm, None, (0.0, 0.0, 0.0)).astype(MXU)
        for h in range(nh):
            hs = slice(h * HEAD, (h + 1) * HEAD)
            lo, hi = h * WIDE, h * WIDE + HEAD
            f_h = _head_col(f, h)
            for x_ref, g_ref, o_ref in ((q_ref, gq_ref, qa_ref), (k_ref, gk_ref, ka_ref)):
                x = x_ref[:, hs]
                y = x * lax.rsqrt(jnp.mean(x * x, axis=-1, keepdims=True) + EPS)
                o_ref[:, lo:hi] = (y * g_ref[:, hs]).astype(o_ref.dtype)
            qa_ref[:, hi:hi + HEAD] = _extra_cols(tm, _split3(f_h * INV_SCALE), None).astype(MXU)
            f_key = jnp.where(is_pad, -MASK_VALUE, f_h)
            ka_ref[:, hi:hi + HEAD] = _extra_cols(tm, None, _split3(-f_key * INV_SCALE)).astype(MXU)
            va_ref[:, lo:hi] = v_ref[:, hs].astype(MXU)
            va_ref[:, hi:hi + HEAD] = ones_only

    wide = pl.BlockSpec((tm, nh * WIDE), lambda i: (i, 0))
    vec = pl.BlockSpec((1, d), lambda i: (0, 0))
    return pl.pallas_call(
        body,
        name=name,
        grid=(lp // tm,),
        in_specs=[pl.BlockSpec((None, tm, d), lambda i: (0, i, 0)), pl.BlockSpec((None, tm, d), lambda i: (0, i, 0)),
                  pl.BlockSpec((None, tm, d), lambda i: (1, i, 0)), vec, vec, pl.BlockSpec((tm, 128), lambda i: (i, 0))],
        out_specs=[wide, wide, wide],
        out_shape=[jax.ShapeDtypeStruct((lp, nh * WIDE), MXU)] * 3,
        compiler_params=_cparams(("parallel",), 56),
    )(ub, ukv, ukv, g_q, g_k, f_cum)


def _head_rms_bwd_tile(x, g, dy):
    rstd = lax.rsqrt(jnp.mean(x * x, axis=-1, keepdims=True) + EPS)
    xhat = x * rstd
    gdy = dy * g
    dx = rstd * (gdy - xhat * jnp.mean(gdy * xhat, axis=-1, keepdims=True))
    return dx, jnp.sum(dy * xhat, axis=0, keepdims=True)


def _fgate_fwd(ufl, b_f):
    lp = ufl.shape[0]
    nb = lp // 128

    def body(u_ref, b_ref, f_ref):
        def step(i, carry):
            sl = pl.ds(pl.multiple_of(i * 128, 128), 128)
            valid = (i * 128 + lax.broadcasted_iota(jnp.int32, (128, 128), 0)) >= ROW_PAD
            x = u_ref[sl, :] + b_ref[...]
            logf = jnp.where(valid, jnp.minimum(x, 0.0) - jnp.log(1.0 + jnp.exp(-jnp.abs(x))), 0.0)
            f = _dot_exact(_tri(128, True), logf) + carry
            f_ref[sl, :] = f
            return f[127:128, :]

        lax.fori_loop(0, nb, step, jnp.zeros((1, 128), F32))

    return pl.pallas_call(body, name="fgate_fwd", out_shape=jax.ShapeDtypeStruct((lp, 128), F32))(ufl, b_f)


def _fgate_bwd(ufl, b_f, d_f):
    lp = ufl.shape[0]
    nb = lp // 128

    def body(u_ref, b_ref, df_ref, du_ref, db_ref):
        def step(j, carry):
            later, db_acc = carry
            i = nb - 1 - j
            sl = pl.ds(pl.multiple_of(i * 128, 128), 128)
            valid = (i * 128 + lax.broadcasted_iota(jnp.int32, (128, 128), 0)) >= ROW_PAD
            x = u_ref[sl, :] + b_ref[...]
            df = df_ref[sl, :]
            dlogf = _dot_exact(_tri(128, False), df) + later
            dx = jnp.where(valid, dlogf * _sigmoid(-x), 0.0)
            du_ref[sl, :] = dx.astype(du_ref.dtype)
            return later + jnp.sum(df, axis=0, keepdims=True), db_acc + jnp.sum(dx, axis=0, keepdims=True)

        _, db_tot = lax.fori_loop(0, nb, step, (jnp.zeros((1, 128), F32), jnp.zeros((1, 128), F32)))
        db_ref[...] = db_tot

    return pl.pallas_call(
        body, name="fgate_bwd",
        out_shape=[jax.ShapeDtypeStruct((lp, 128), MXU), jax.ShapeDtypeStruct((1, 128), F32)],
    )(ufl, b_f, d_f)


STRIP = 32
PAIR = 2


def _strip_causal(r, t):
    row = r + lax.broadcasted_iota(jnp.int32, (STRIP, t), 0)
    col = lax.broadcasted_iota(jnp.int32, (STRIP, t), 1)
    return col <= row


def _fox_fwd(qa, ka, va, ub, name="fox_fwd", t=384):
    lp = qa.shape[0]
    d = ub.shape[2]
    t = _tile(lp, t)
    nh, nq = d // HEAD, lp // t
    scale = HEAD ** -0.5

    assert nh % PAIR == 0
    heads = range(PAIR)

    def body(q_ref, k_ref, v_ref, z_ref, o_ref, olo_ref, og_ref, qb_ref,
             s_ref, p_ref, m_ref, a_ref, l_ref, acc_ref):
        qb = pl.program_id(1)
        m_ref[...] = jnp.full((PAIR, t, 1), MASK_VALUE, F32)
        l_ref[...] = jnp.zeros((PAIR, t, 128), F32)
        acc_ref[...] = jnp.zeros((PAIR, 2 * t, HEAD), F32)
        p_ref[1] = jnp.zeros((PAIR, 2 * t, t), MXU)
        a_ref[1] = jnp.ones((PAIR, t, 1), F32)

        def scores(kb, buf):
            ks = pl.ds(pl.multiple_of(kb * t, t), t)
            for j in heads:
                ws = slice(j * WIDE, (j + 1) * WIDE)
                s_ref[buf, j] = _dot(q_ref[:, ws], k_ref[ks, ws], NT)

        def weighted_sum(kb, buf):
            ks = pl.ds(pl.multiple_of(kb * t, t), t)
            for j in heads:
                pv = _dot(p_ref[buf, j], v_ref[ks, j * WIDE:j * WIDE + HEAD])
                alpha = a_ref[buf, j]
                acc_ref[j, 0:t, :] = alpha * acc_ref[j, 0:t, :] + pv[0:t]
                acc_ref[j, t:2 * t, :] = alpha * acc_ref[j, t:2 * t, :] + pv[t:2 * t]

        def softmax_update(buf, diagonal):
            for j in heads:
                for r in range(0, t, STRIP):
                    rs = slice(r, r + STRIP)
                    x = s_ref[buf, j, rs, :] * scale
                    if diagonal:
                        x = jnp.where(_strip_causal(r, t), x, MASK_VALUE)
                    m_old = m_ref[j, rs, :]
                    m_new = jnp.maximum(m_old, jnp.max(x, axis=-1, keepdims=True))
                    alpha = jnp.exp(m_old - m_new)
                    p = jnp.exp(x - m_new)
                    m_ref[j, rs, :] = m_new
                    a_ref[buf, j, rs, :] = alpha
                    l_ref[j, rs, :] = alpha * l_ref[j, rs, :] + sum(p[:, c:c + 128] for c in range(0, t, 128))
                    p_hi = p.astype(MXU)
                    p_ref[buf, j, rs, :] = p_hi
                    p_ref[buf, j, t + r:t + r + STRIP, :] = (p - p_hi.astype(F32)).astype(MXU)

        def off_diagonal(kb, cur):
            weighted_sum(jnp.maximum(kb - 1, 0), 1 - cur)
            scores(kb + 1, 1 - cur)
            softmax_update(cur, False)

        def diagonal(cur):
            weighted_sum(jnp.maximum(qb - 1, 0), 1 - cur)
            softmax_update(cur, True)
            weighted_sum(qb, cur)

        def two_blocks(i, carry):
            off_diagonal(2 * i, 0)
            off_diagonal(2 * i + 1, 1)
            return carry

        scores(0, 0)
        lax.fori_loop(0, qb // 2, two_blocks, 0)

        @pl.when(lax.rem(qb, 2) == 0)
        def _():
            diagonal(0)

        @pl.when(lax.rem(qb, 2) == 1)
        def _():
            off_diagonal(qb - 1, 0)
            diagonal(1)

        is_pad = (qb * t + lax.broadcasted_iota(jnp.int32, (t, 1), 0)) < ROW_PAD
        for j in heads:
            hs = slice(j * HEAD, (j + 1) * HEAD)
            l = jnp.sum(l_ref[j], axis=-1, keepdims=True)
            o = acc_ref[j, 0:t, :] / l
            o_ref[:, hs] = o
            olo_ref[:, hs] = acc_ref[j, t:2 * t, :] / l
            z = z_ref[:, hs]
            og_ref[:, hs] = (o * (z * _sigmoid(z))).astype(og_ref.dtype)
            extra = q_ref[:, j * WIDE + HEAD:(j + 1) * WIDE].astype(F32)
            f_scaled = extra[:, 0:1] + extra[:, 1:2] + extra[:, 2:3]
            log_term = jnp.where(is_pad, MASK_VALUE * INV_SCALE, f_scaled - (m_ref[j] + jnp.log(l)) * INV_SCALE)
            qb_ref[:, j * WIDE:j * WIDE + HEAD] = q_ref[:, j * WIDE:j * WIDE + HEAD]
            qb_ref[:, j * WIDE + HEAD:(j + 1) * WIDE] = _extra_cols(t, _split3(log_term), None).astype(qb_ref.dtype)

    scratch = [pltpu.VMEM((2, PAIR, t, t), F32), pltpu.VMEM((2, PAIR, 2 * t, t), MXU), pltpu.VMEM((PAIR, t, 1), F32),
               pltpu.VMEM((2, PAIR, t, 1), F32), pltpu.VMEM((PAIR, t, 128), F32), pltpu.VMEM((PAIR, 2 * t, HEAD), F32)]
    tile = pl.BlockSpec((t, PAIR * HEAD), lambda g, i: (i, g))
    wide_tile = pl.BlockSpec((t, PAIR * WIDE), lambda g, i: (i, g))
    wide_all = pl.BlockSpec((lp, PAIR * WIDE), lambda g, i: (0, g))
    return pl.pallas_call(
        body,
        name=name,
        grid=(nh // PAIR, nq),
        in_specs=[wide_tile, wide_all, wide_all, pl.BlockSpec((None, t, PAIR * HEAD), lambda g, i: (1, i, g))],
        out_specs=[tile, tile, tile, wide_tile],
        out_shape=[jax.ShapeDtypeStruct((lp, d), F32), jax.ShapeDtypeStruct((lp, d), F32),
                   jax.ShapeDtypeStruct((lp, d), MXU), jax.ShapeDtypeStruct((lp, nh * WIDE), MXU)],
        scratch_shapes=scratch,
        compiler_params=_cparams(("parallel", "arbitrary"), 48),
    )(qa, ka, va, ub)


def _fox_bwd(qb, ka, va, doa, ukv, g_k, ub, g_q, name="fox_bwd", t=384, sides=()):
    lp = qb.shape[0]
    d = ukv.shape[2]
    t = _tile(lp, t)
    nh, nk = d // HEAD, lp // t
    scale = HEAD ** -0.5

    assert nh % PAIR == 0
    heads = range(PAIR)

    def body(q_ref, do_ref, k_ref, v_ref, kraw_ref, gk_ref, qraw_ref, gq_ref, dqraw_ref, dgq_ref, dukv_ref, dgk_ref,
             dfk_ref, s_ref, dp_ref, p_ref, ds_ref, col_ref, dk_ref, dv_ref, dq_ref):
        kb = pl.program_id(1)

        @pl.when(kb == 0)
        def _():
            dq_ref[...] = jnp.zeros_like(dq_ref)
            dgk_ref[...] = jnp.zeros_like(dgk_ref)

        dk_ref[...] = jnp.zeros_like(dk_ref)
        dv_ref[...] = jnp.zeros_like(dv_ref)
        col_ref[...] = jnp.zeros_like(col_ref)

        def step(qb, diagonal):
            qs = pl.ds(pl.multiple_of(qb * t, t), t)
            for j in heads:
                ws = slice(j * WIDE, (j + 1) * WIDE)
                s_ref[j] = _dot(q_ref[qs, ws], k_ref[:, ws], NT)
                dp_ref[j] = _dot(do_ref[qs, ws], v_ref[:, ws], NT)
            for j in heads:
                for r in range(0, t, STRIP):
                    rs = slice(r, r + STRIP)
                    x = s_ref[j, rs, :] * scale
                    if diagonal:
                        x = jnp.where(_strip_causal(r, t), x, MASK_VALUE)
                    p = jnp.exp(x)
                    ds = p * dp_ref[j, rs, :]
                    p_ref[j, rs, :] = p.astype(MXU)
                    ds_ref[j, rs, :] = (ds * scale).astype(MXU)
                    col_ref[j] += ds
            for j in heads:
                hs = slice(j * HEAD, (j + 1) * HEAD)
                narrow = slice(j * WIDE, j * WIDE + HEAD)
                dsb = ds_ref[j]
                dv_ref[:, hs] += _dot(p_ref[j], do_ref[qs, narrow], TN)
                dq_ref[qs, hs] += _dot(dsb, k_ref[:, narrow])
                dk_ref[:, hs] += _dot(dsb, q_ref[qs, narrow], TN)

        def off_diagonal(qb, carry):
            step(qb, False)
            return carry

        step(kb, True)
        lax.fori_loop(kb + 1, nk, off_diagonal, 0)
        for j in heads:
            hs = slice(j * HEAD, (j + 1) * HEAD)
            dfk_ref[j] = -jnp.sum(col_ref[j], axis=0, keepdims=True)
            dx, dg = _head_rms_bwd_tile(kraw_ref[:, hs], gk_ref[:, hs], dk_ref[:, hs])
            dukv_ref[0, :, hs] = dx.astype(dukv_ref.dtype)
            dukv_ref[1, :, hs] = dv_ref[:, hs].astype(dukv_ref.dtype)
            dgk_ref[:, hs] += dg

        @pl.when(kb == nk - 1)
        def _():
            dgq_ref[...] = jnp.zeros_like(dgq_ref)

            def rows(c, carry):
                rs = pl.ds(pl.multiple_of(c * t, t), t)
                for j in heads:
                    hs = slice(j * HEAD, (j + 1) * HEAD)
                    dx, dg = _head_rms_bwd_tile(qraw_ref[rs, hs], gq_ref[:, hs], dq_ref[rs, hs])
                    dqraw_ref[rs, hs] = dx.astype(dqraw_ref.dtype)
                    dgq_ref[:, hs] += dg
                return carry

            lax.fori_loop(0, nk, rows, 0)

    scratch = [pltpu.VMEM((PAIR, t, t), F32), pltpu.VMEM((PAIR, t, t), F32), pltpu.VMEM((PAIR, t, t), MXU),
               pltpu.VMEM((PAIR, t, t), MXU), pltpu.VMEM((PAIR, STRIP, t), F32),
               pltpu.VMEM((t, PAIR * HEAD), F32), pltpu.VMEM((t, PAIR * HEAD), F32), pltpu.VMEM((lp, PAIR * HEAD), F32)]
    whole = pl.BlockSpec((lp, PAIR * HEAD), lambda g, j: (0, g))
    wide_all = pl.BlockSpec((lp, PAIR * WIDE), lambda g, j: (0, g))
    wide_tile = pl.BlockSpec((t, PAIR * WIDE), lambda g, j: (j, g))
    vec = pl.BlockSpec((1, PAIR * HEAD), lambda g, j: (0, g))
    outs, side_results = _pallas(
        body,
        name=name,
        args=(qb, doa, ka, va, ukv, g_k, ub, g_q),
        grid=(nh // PAIR, nk),
        in_specs=[wide_all, wide_all, wide_tile, wide_tile,
                  pl.BlockSpec((None, t, PAIR * HEAD), lambda g, j: (0, j, g)), vec,
                  pl.BlockSpec((None, lp, PAIR * HEAD), lambda g, j: (0, 0, g)), vec],
        out_specs=[whole, vec, pl.BlockSpec((2, t, PAIR * HEAD), lambda g, j: (0, j, g)), vec,
                   pl.BlockSpec((PAIR, 1, t), lambda g, j: (g, 0, j))],
        out_shape=[jax.ShapeDtypeStruct((lp, d), MXU), jax.ShapeDtypeStruct((1, d), F32),
                   jax.ShapeDtypeStruct((2, lp, d), MXU), jax.ShapeDtypeStruct((1, d), F32),
                   jax.ShapeDtypeStruct((nh, 1, lp), F32)],
        scratch_shapes=scratch,
        semantics=("parallel", "arbitrary"),
        vmem_mib=48,
        sides=sides,
    )
    return (*outs, side_results) if sides else tuple(outs)


def _fox_do(d_og, o, o_lo, ub, name="fox_do", tm=384):
    lp, d = o.shape
    tm = _tile(lp, tm)
    nh = d // HEAD

    def body(dog_ref, o_ref, olo_ref, z_ref, doa_ref):
        for h in range(nh):
            hs = slice(h * HEAD, (h + 1) * HEAD)
            z = z_ref[:, hs]
            do = (dog_ref[:, hs] * (z * _sigmoid(z))).astype(doa_ref.dtype)
            delta = jnp.sum(do.astype(F32) * (o_ref[:, hs] + olo_ref[:, hs]), axis=-1, keepdims=True)
            doa_ref[:, h * WIDE:h * WIDE + HEAD] = do
            doa_ref[:, h * WIDE + HEAD:(h + 1) * WIDE] = _extra_cols(tm, _split3(-delta), (0.0, 0.0, 0.0)).astype(
                doa_ref.dtype)

    row = pl.BlockSpec((tm, d), lambda i: (i, 0))
    return pl.pallas_call(
        body,
        name=name,
        grid=(lp // tm,),
        in_specs=[row, row, row, pl.BlockSpec((None, tm, d), lambda i: (1, i, 0))],
        out_specs=pl.BlockSpec((tm, nh * WIDE), lambda i: (i, 0)),
        out_shape=jax.ShapeDtypeStruct((lp, nh * WIDE), MXU),
        compiler_params=_cparams(("parallel",), 56),
    )(d_og, o, o_lo, ub)


def _fox_gate_bwd(ub, d_og, o, dq, name="fox_gate_bwd", tm=384):
    _, lp, d = ub.shape
    tm = _tile(lp, tm)

    def body(z_ref, dog_ref, o_ref, dq_ref, dub_ref):
        z = z_ref[...]
        sz = _sigmoid(z)
        dub_ref[0] = dq_ref[...]
        dub_ref[1] = (dog_ref[...] * o_ref[...] * (sz * (1.0 + z * (1.0 - sz)))).astype(dub_ref.dtype)

    row = pl.BlockSpec((tm, d), lambda i: (i, 0))
    return pl.pallas_call(
        body,
        name=name,
        grid=(lp // tm,),
        in_specs=[pl.BlockSpec((None, tm, d), lambda i: (1, i, 0)), row, row, row],
        out_specs=pl.BlockSpec((2, tm, d), lambda i: (0, i, 0)),
        out_shape=jax.ShapeDtypeStruct((2, lp, d), MXU),
        compiler_params=_cparams(("parallel",), 48),
    )(ub, d_og, o, dq)


def _loss(h, target, name="loss_head"):
    lp, d = h.shape
    nb = lp // 128

    def body(h_ref, t_ref, loss_ref, dh_ref, acc_ref):
        i = pl.program_id(0)

        @pl.when(i == 0)
        def _():
            acc_ref[...] = jnp.zeros_like(acc_ref)
            dh_ref[...] = jnp.zeros_like(dh_ref)

        @pl.when(i > 0)
        def _():
            err = h_ref[...] - t_ref[...]
            dh_ref[...] = err * (1.0 / d)
            acc_ref[...] += jnp.sum(jnp.sum(err * err, axis=-1, keepdims=True) * (1.0 / d), axis=0, keepdims=True)

        @pl.when(i == nb - 1)
        def _():
            loss_ref[...] = 0.5 * acc_ref[...]

    return pl.pallas_call(
        body,
        name=name,
        grid=(nb,),
        in_specs=[pl.BlockSpec((128, d), lambda i: (i, 0)),
                  pl.BlockSpec((128, d), lambda i: (jnp.maximum(i - 1, 0), 0))],
        out_specs=[pl.BlockSpec((1, 1), lambda i: (0, 0)), pl.BlockSpec((128, d), lambda i: (i, 0))],
        out_shape=[jax.ShapeDtypeStruct((1, 1), F32), jax.ShapeDtypeStruct((lp, d), F32)],
        scratch_shapes=[pltpu.VMEM((1, 1), F32)],
        compiler_params=_cparams(("arbitrary",), 32),
    )(h, target)


def _adam_math(w, g, m, v):
    m = ADAM_B1 * m + (1.0 - ADAM_B1) * g
    v = ADAM_B2 * v + (1.0 - ADAM_B2) * (g * g)
    m_hat = m / (1.0 - ADAM_B1 ** ADAM_STEP)
    v_hat = v / (1.0 - ADAM_B2 ** ADAM_STEP)
    delta = -ADAM_LR * (m_hat / (jnp.sqrt(v_hat) + ADAM_EPS) + ADAM_WD * w)
    return delta, m, v


def _adamw(parts, w, m, v, name, tm=512):
    n, r, c = parts.shape
    if r % tm == 0:
        grid, blk, at = r // tm, (tm, c), lambda i: (i, 0)
    else:
        tc = _tile(c, tm)
        grid, blk, at = c // tc, (r, tc), lambda i: (0, i)

    def body(p_ref, w_ref, m_ref, v_ref, g_ref, d_ref, nm_ref, nv_ref):
        g = p_ref[0].astype(F32)
        for j in range(1, n):
            g = g + p_ref[j].astype(F32)
        g_ref[...] = g
        d_ref[...], nm_ref[...], nv_ref[...] = _adam_math(w_ref[...], g, m_ref[...], v_ref[...])

    row = pl.BlockSpec(blk, at)
    return pl.pallas_call(
        body,
        name=name,
        grid=(grid,),
        in_specs=[pl.BlockSpec((n,) + blk, lambda i: (0,) + at(i)), row, row, row],
        out_specs=[row] * 4,
        out_shape=[jax.ShapeDtypeStruct((r, c), F32)] * 4,
        compiler_params=_cparams(("parallel",), 48),
    )(parts, w, m, v)


def _adamw_small(gs, ws, ms, vs, name="adamw_small"):
    n = len(ws)
    flat = [jnp.reshape(a, (-1, a.shape[-1])) for group in (gs, ws, ms, vs) for a in group]

    def body(*refs):
        g_refs, w_refs, m_refs, v_refs = (refs[k * n:(k + 1) * n] for k in range(4))
        outs = refs[4 * n:]
        for i in range(n):
            d_new, m_new, v_new = _adam_math(w_refs[i][...], g_refs[i][...], m_refs[i][...], v_refs[i][...])
            outs[i][...], outs[n + i][...], outs[2 * n + i][...] = d_new, m_new, v_new

    res = pl.pallas_call(
        body, name=name, out_shape=[jax.ShapeDtypeStruct(a.shape, F32) for a in flat[n:2 * n]] * 3)(*flat)
    return [[jnp.reshape(r, w.shape) for r, w in zip(res[k * n:(k + 1) * n], ws)] for k in range(3)]


def _place():
    x, y, c = lax.axis_index("x"), lax.axis_index("y"), lax.axis_index("c")
    return x, y, c


def _other_chips(x, y):
    return [(1 - x, y), (x, 1 - y), (1 - x, 1 - y)]


def _any_spec():
    return pl.BlockSpec(memory_space=pl.ANY)


class _Side:
    def __init__(self, ins, outs, sems, start, finish, aliases=None):
        self.ins, self.outs, self.sems = list(ins), list(outs), sems
        self.start, self.finish = start, finish
        self.aliases = dict(aliases or {})


def _pallas(body, *, name, out_shape, args=(), grid=(), in_specs=(), out_specs=(), scratch_shapes=(),
            semantics=(), vmem_mib=None, sides=(), aliases=None, prefetch=()):
    n_in, n_out, n_scr, n_pre = len(args), len(out_shape), len(scratch_shapes), len(prefetch)
    side_ins = [a for s in sides for a in s.ins]
    side_outs = [o for s in sides for o in s.outs]
    side_sems = [pltpu.SemaphoreType.DMA((max(k, 1),)) for s in sides for k in s.sems]
    aliases, in_at, out_at = {n_pre + i: o for i, o in (aliases or {}).items()}, n_pre + n_in, n_out
    for s in sides:
        aliases.update({in_at + i: out_at + o for i, o in s.aliases.items()})
        in_at, out_at = in_at + len(s.ins), out_at + len(s.outs)

    def wrapped(*refs):
        at = [n_pre]

        def take(k):
            got = refs[at[0]:at[0] + k]
            at[0] += k
            return got

        main_in = take(n_in)
        s_in = [take(len(s.ins)) for s in sides]
        main_out = take(n_out)
        s_out = [take(len(s.outs)) for s in sides]
        main_scr = take(n_scr)
        s_sem = [take(3) for s in sides]

        def run(stage):
            for s, i_, o_, m_ in zip(sides, s_in, s_out, s_sem):
                getattr(s, stage)(i_, o_, *m_)

        first = last = None
        for k, g in enumerate(grid):
            i = pl.program_id(k)
            first = (i == 0) if first is None else jnp.logical_and(first, i == 0)
            last = (i == g - 1) if last is None else jnp.logical_and(last, i == g - 1)
        if sides:
            run("start") if first is None else pl.when(first)(lambda: run("start"))
        body(*main_in, *main_out, *main_scr)
        if sides:
            run("finish") if last is None else pl.when(last)(lambda: run("finish"))

    kw = {}
    specs = dict(in_specs=list(in_specs) + [_any_spec()] * len(side_ins),
                 out_specs=list(out_specs) + [_any_spec()] * len(side_outs),
                 scratch_shapes=list(scratch_shapes) + side_sems)
    if prefetch:
        kw["grid_spec"] = pltpu.PrefetchScalarGridSpec(num_scalar_prefetch=n_pre, grid=grid, **specs)
    else:
        kw.update(specs)
        if grid:
            kw["grid"] = grid
    if semantics or vmem_mib:
        sem = tuple("arbitrary" for _ in grid) if sides else tuple(semantics)
        kw["compiler_params"] = pltpu.CompilerParams(
            dimension_semantics=sem or None, vmem_limit_bytes=vmem_mib * MIB if vmem_mib else None)
    res = pl.pallas_call(
        wrapped,
        name=name,
        out_shape=list(out_shape) + side_outs,
        input_output_aliases=aliases,
        **kw,
    )(*prefetch, *args, *side_ins)
    main, rest, per_side = list(res[:n_out]), list(res[n_out:]), []
    for s in sides:
        per_side.append(rest[:len(s.outs)])
        rest = rest[len(s.outs):]
    return main, per_side


def _slot(p):
    return 4 * p[0] + 2 * p[1] + p[2]


def _sibling_side(grads):
    n = len(grads)

    def copies(ins, outs, send_sems, recv_sems):
        x, y, c = _place()
        return [pltpu.make_async_remote_copy(
            src_ref=ins[t].at[2 * chip + (1 - c)], dst_ref=outs[t].at[chip],
            send_sem=send_sems.at[4 * t + chip], recv_sem=recv_sems.at[4 * t + chip],
            device_id=(x, y, 1 - c), device_id_type=MESH) for t in range(n) for chip in range(N_CHIP)]

    def start(ins, outs, send_sems, recv_sems, local_sems):
        for cp in copies(ins, outs, send_sems, recv_sems):
            cp.start()

    def finish(ins, outs, send_sems, recv_sems, local_sems):
        for cp in copies(ins, outs, send_sems, recv_sems):
            cp.wait()

    outs = [jax.ShapeDtypeStruct((N_CHIP,) + g.shape[1:], g.dtype) for g in grads]
    return _Side(grads, outs, (4 * n, 4 * n, 0), start, finish)


def _chips_side(partials):
    n = len(partials)

    def copies(ins, outs, send_sems, recv_sems, local_sems):
        x, y, c = _place()
        my_chip = 2 * x + y
        local = [pltpu.make_async_copy(ins[t].at[my_chip], outs[t].at[my_chip], local_sems.at[t]) for t in range(n)]
        sends, recvs = [], []
        for t in range(n):
            for j, chip in enumerate(_other_chips(x, y)):
                their = 2 * chip[0] + chip[1]
                sems = dict(send_sem=send_sems.at[3 * t + j], recv_sem=recv_sems.at[3 * t + j],
                            device_id=(*chip, c), device_id_type=MESH)
                sends.append(pltpu.make_async_remote_copy(src_ref=ins[t].at[their], dst_ref=outs[t].at[my_chip], **sems))
                recvs.append(pltpu.make_async_remote_copy(src_ref=ins[t].at[my_chip], dst_ref=outs[t].at[their], **sems))
        return local, sends, recvs

    def start(ins, outs, send_sems, recv_sems, local_sems):
        local, sends, _ = copies(ins, outs, send_sems, recv_sems, local_sems)
        for cp in local + sends:
            cp.start()

    def finish(ins, outs, send_sems, recv_sems, local_sems):
        local, sends, recvs = copies(ins, outs, send_sems, recv_sems, local_sems)
        for cp in sends:
            cp.wait_send()
        for cp in recvs:
            cp.wait_recv()
        for cp in local:
            cp.wait()

    outs = [jax.ShapeDtypeStruct(p.shape, p.dtype) for p in partials]
    return _Side(partials, outs, (3 * n, 3 * n, n), start, finish)


def _gather_own_side(blocks):
    n = len(blocks)

    def copies(ins, outs, send_sems, recv_sems, local_sems):
        x, y, c = _place()
        me = (x, y, c)
        peers = [(x, y, 1 - c)] + [(*chip, c) for chip in _other_chips(x, y)]
        local = [pltpu.make_async_copy(ins[t], outs[t].at[_slot(me)], local_sems.at[t]) for t in range(n)]
        sends, recvs = [], []
        for t in range(n):
            for k, peer in enumerate(peers):
                sems = dict(send_sem=send_sems.at[4 * t + k], recv_sem=recv_sems.at[4 * t + k],
                            device_id=peer, device_id_type=MESH)
                sends.append(pltpu.make_async_remote_copy(src_ref=ins[t], dst_ref=outs[t].at[_slot(me)], **sems))
                recvs.append(pltpu.make_async_remote_copy(src_ref=ins[t], dst_ref=outs[t].at[_slot(peer)], **sems))
        return local, sends, recvs

    def start(ins, outs, send_sems, recv_sems, local_sems):
        local, sends, _ = copies(ins, outs, send_sems, recv_sems, local_sems)
        for cp in local + sends:
            cp.start()

    def finish(ins, outs, send_sems, recv_sems, local_sems):
        local, sends, recvs = copies(ins, outs, send_sems, recv_sems, local_sems)
        for cp in sends:
            cp.wait_send()
        for cp in recvs:
            cp.wait_recv()
        for cp in local:
            cp.wait()

    outs = [jax.ShapeDtypeStruct((N_DEV,) + b.shape, b.dtype) for b in blocks]
    return _Side(blocks, outs, (4 * n, 4 * n, n), start, finish)


def _gather_pass_side(gathered):
    n = len(gathered)

    def copies(outs, send_sems, recv_sems):
        x, y, c = _place()
        sends, recvs = [], []
        for t in range(n):
            for j, chip in enumerate(_other_chips(x, y)):
                sems = dict(send_sem=send_sems.at[3 * t + j], recv_sem=recv_sems.at[3 * t + j],
                            device_id=(x, y, 1 - c), device_id_type=MESH)
                mine, theirs = outs[t].at[_slot((*chip, c))], outs[t].at[_slot((*chip, 1 - c))]
                sends.append(pltpu.make_async_remote_copy(src_ref=mine, dst_ref=mine, **sems))
                recvs.append(pltpu.make_async_remote_copy(src_ref=mine, dst_ref=theirs, **sems))
        return sends, recvs

    def start(ins, outs, send_sems, recv_sems, local_sems):
        for cp in copies(outs, send_sems, recv_sems)[0]:
            cp.start()

    def finish(ins, outs, send_sems, recv_sems, local_sems):
        sends, recvs = copies(outs, send_sems, recv_sems)
        for cp in sends:
            cp.wait_send()
        for cp in recvs:
            cp.wait_recv()

    outs = [jax.ShapeDtypeStruct(g.shape, g.dtype) for g in gathered]
    return _Side(gathered, outs, (3 * n, 3 * n, 0), start, finish, aliases={t: t for t in range(n)})


def _alone(side, name):
    return _pallas(lambda: None, name=name, out_shape=[], sides=[side])[1][0]


def _gather_side(blocks):
    n = len(blocks)

    def parts(ins, outs, send_sems, recv_sems, local_sems):
        x, y, c = _place()
        me, sibling = (x, y, c), (x, y, 1 - c)

        def copy(t, k, block, to, src=None):
            dst = outs[t].at[_slot(block)]
            return pltpu.make_async_remote_copy(
                src_ref=dst if src is None else src, dst_ref=dst,
                send_sem=send_sems.at[7 * t + k], recv_sem=recv_sems.at[7 * t + k],
                device_id=to, device_id_type=MESH)

        local = [pltpu.make_async_copy(ins[t], outs[t].at[_slot(me)], local_sems.at[t]) for t in range(n)]
        own = [copy(t, 0, me, sibling, src=ins[t]) for t in range(n)]
        own += [copy(t, 1 + j, me, (*chip, c), src=ins[t]) for t in range(n) for j, chip in enumerate(_other_chips(x, y))]
        return me, sibling, c, _other_chips(x, y), copy, local, own

    def start(ins, outs, send_sems, recv_sems, local_sems):
        *_, local, own = parts(ins, outs, send_sems, recv_sems, local_sems)
        for cp in local + own:
            cp.start()

    def finish(ins, outs, send_sems, recv_sems, local_sems):
        me, sibling, c, chips, copy, local, own = parts(ins, outs, send_sems, recv_sems, local_sems)
        passed = []
        for t in range(n):
            for j, chip in enumerate(chips):
                copy(t, 1 + j, (*chip, c), me).wait_recv()
                passed.append(copy(t, 4 + j, (*chip, c), sibling))
                passed[-1].start()
        for t in range(n):
            copy(t, 0, sibling, me).wait_recv()
            for j, chip in enumerate(chips):
                copy(t, 4 + j, (*chip, 1 - c), me).wait_recv()
        for cp in own + passed:
            cp.wait_send()
        for cp in local:
            cp.wait()

    outs = [jax.ShapeDtypeStruct((N_DEV,) + b.shape, b.dtype) for b in blocks]
    return _Side(blocks, outs, (7 * n, 7 * n, n), start, finish)


def _gather_near_side(blocks):
    n = len(blocks)

    def copies(ins, outs, send_sems, recv_sems, local_sems):
        x, y, c = _place()
        me = (x, y, c)
        peers = [(x, y, 1 - c), (1 - x, y, c), (x, 1 - y, c)]
        local = [pltpu.make_async_copy(ins[t], outs[t].at[_slot(me)], local_sems.at[2 * t]) for t in range(n)]
        local += [pltpu.make_async_copy(ins[t], outs[n + t].at[c], local_sems.at[2 * t + 1]) for t in range(n)]
        sends, recvs = [], []
        for t in range(n):
            for k, peer in enumerate(peers):
                sems = dict(send_sem=send_sems.at[4 * t + k], recv_sem=recv_sems.at[4 * t + k],
                            device_id=peer, device_id_type=MESH)
                sends.append(pltpu.make_async_remote_copy(src_ref=ins[t], dst_ref=outs[t].at[_slot(me)], **sems))
                recvs.append(pltpu.make_async_remote_copy(src_ref=ins[t], dst_ref=outs[t].at[_slot(peer)], **sems))
            sems = dict(send_sem=send_sems.at[4 * t + 3], recv_sem=recv_sems.at[4 * t + 3],
                        device_id=peers[0], device_id_type=MESH)
            sends.append(pltpu.make_async_remote_copy(src_ref=ins[t], dst_ref=outs[n + t].at[c], **sems))
            recvs.append(pltpu.make_async_remote_copy(src_ref=ins[t], dst_ref=outs[n + t].at[1 - c], **sems))
        return local, sends, recvs

    def start(ins, outs, send_sems, recv_sems, local_sems):
        local, sends, _ = copies(ins, outs, send_sems, recv_sems, local_sems)
        for cp in local + sends:
            cp.start()

    def finish(ins, outs, send_sems, recv_sems, local_sems):
        local, sends, recvs = copies(ins, outs, send_sems, recv_sems, local_sems)
        for cp in sends:
            cp.wait_send()
        for cp in recvs:
            cp.wait_recv()
        for cp in local:
            cp.wait()

    outs = [jax.ShapeDtypeStruct((N_DEV,) + b.shape, b.dtype) for b in blocks]
    outs += [jax.ShapeDtypeStruct((2,) + b.shape, b.dtype) for b in blocks]
    return _Side(blocks, outs, (4 * n, 4 * n, 2 * n), start, finish)


def _gather_far_side(gathered):
    n = len(gathered)

    def parts(outs, send_sems, recv_sems):
        x, y, c = _place()
        me, sibling, far = (x, y, c), (x, y, 1 - c), (1 - x, 1 - y, c)
        chips = [(1 - x, y), (x, 1 - y), (1 - x, 1 - y)]

        def copy(t, k, block, to):
            ref = outs[t].at[_slot(block)]
            return pltpu.make_async_remote_copy(
                src_ref=ref, dst_ref=ref, send_sem=send_sems.at[4 * t + k], recv_sem=recv_sems.at[4 * t + k],
                device_id=to, device_id_type=MESH)

        early = [copy(t, j, (*chips[j], c), sibling) for t in range(n) for j in range(2)]
        early += [copy(t, 3, me, far) for t in range(n)]
        return c, far, sibling, chips, copy, early

    def start(ins, outs, send_sems, recv_sems, local_sems):
        for cp in parts(outs, send_sems, recv_sems)[-1]:
            cp.start()

    def finish(ins, outs, send_sems, recv_sems, local_sems):
        c, far, sibling, chips, copy, early = parts(outs, send_sems, recv_sems)
        late = []
        for t in range(n):
            copy(t, 3, far, far).wait_recv()
            late.append(copy(t, 2, far, sibling))
            late[-1].start()
        for t in range(n):
            for j in range(3):
                copy(t, j, (*chips[j], 1 - c), sibling).wait_recv()
        for cp in early + late:
            cp.wait_send()

    outs = [jax.ShapeDtypeStruct(g.shape, g.dtype) for g in gathered]
    return _Side(gathered, outs, (4 * n, 4 * n, 0), start, finish, aliases={t: t for t in range(n)})


def _a_in(hn, w8, chip, first, count, name, into=None, sides=(), tm=1408, tn=512):
    lp, k = hn.shape
    c = w8.shape[2]
    tm, tn = _tile(lp, tm, 64), _tile(c, tn)
    per_block = c // tn
    per_part = 2 * per_block

    def where(j, s):
        return jnp.bitwise_xor(s[0], first + j // per_part), j % per_part

    def b_at(i, j, s):
        part, q = where(j, s)
        block = q // per_block
        return (block if w8.shape[0] == 2 else 2 * part + block, 0, q % per_block)

    def o_at(i, j, s):
        part, q = where(j, s)
        return (part, i, q)

    def body(a_ref, b_ref, *rest):
        o_ref = rest[-1]
        o_ref[...] = lax.dot_general(a_ref[...], b_ref[...], NN, preferred_element_type=F32)

    args, in_specs, aliases = [hn, w8], [pl.BlockSpec((tm, k), lambda i, j, s: (i, 0)), pl.BlockSpec((None, k, tn), b_at)], {}
    if into is not None:
        aliases[2] = 0
        args.append(into)
        in_specs.append(_any_spec())
    (out,), side_results = _pallas(
        body, name=name, args=args, grid=(lp // tm, count * per_part), in_specs=in_specs,
        out_specs=[pl.BlockSpec((None, tm, tn), o_at)], out_shape=[jax.ShapeDtypeStruct((4, lp, 2 * c), F32)],
        semantics=("arbitrary", "arbitrary"), vmem_mib=48, sides=sides, aliases=aliases, prefetch=[chip])
    return out, side_results


def _all_gather(blocks, name):
    return _alone(_gather_side(blocks), name)


def _pair_sum(g8, got, name, tm=1024):
    _, r, c = g8.shape
    tm = tm if r % tm == 0 else r
    core = lax.axis_index("c")

    def body(core_ref, mine_ref, got_ref, o_ref):
        south_first = core_ref[0] == 0
        a, b = mine_ref[...], got_ref[...]
        o_ref[...] = (jnp.where(south_first, a, b) + jnp.where(south_first, b, a)).astype(o_ref.dtype)

    return pl.pallas_call(
        body,
        name=name,
        grid_spec=pltpu.PrefetchScalarGridSpec(
            num_scalar_prefetch=1,
            grid=(N_CHIP, r // tm),
            in_specs=[pl.BlockSpec((None, tm, c), lambda j, i, core_ref: (2 * j + core_ref[0], i, 0)),
                      pl.BlockSpec((None, tm, c), lambda j, i, core_ref: (j, i, 0))],
            out_specs=pl.BlockSpec((None, tm, c), lambda j, i, core_ref: (j, i, 0)),
        ),
        out_shape=jax.ShapeDtypeStruct((N_CHIP, r, c), PAYLOAD),
        compiler_params=_cparams(("parallel", "parallel"), 32),
    )(jnp.reshape(core, (1,)).astype(jnp.int32), g8, got)


def _all_reduce_small(s, name="small_all_reduce"):
    r, c = s.shape

    def body(s_ref, o_ref, buf_ref, send_sems, recv_sems):
        x, y, c_ = _place()
        me = 4 * x + 2 * y + c_
        buf_ref[me] = s_ref[...]

        def copy(k, slot, peer):
            return pltpu.make_async_remote_copy(
                src_ref=s_ref, dst_ref=buf_ref.at[slot],
                send_sem=send_sems.at[k - 1], recv_sem=recv_sems.at[k - 1],
                device_id=peer, device_id_type=MESH)

        peers = []
        for k in range(1, N_DEV):
            peer = (x ^ ((k >> 2) & 1), y ^ ((k >> 1) & 1), c_ ^ (k & 1))
            peers.append(peer)
            copy(k, me, peer).start()
        for k, peer in zip(range(1, N_DEV), peers):
            cp = copy(k, 4 * peer[0] + 2 * peer[1] + peer[2], peer)
            cp.wait_send()
            cp.wait_recv()
        total = buf_ref[0]
        for j in range(1, N_DEV):
            total = total + buf_ref[j]
        o_ref[...] = total

    return pl.pallas_call(
        body,
        name=name,
        out_shape=jax.ShapeDtypeStruct((r, c), F32),
        in_specs=[pl.BlockSpec(memory_space=pltpu.VMEM)],
        out_specs=pl.BlockSpec(memory_space=pltpu.VMEM),
        scratch_shapes=[pltpu.VMEM((N_DEV, r, c), F32), pltpu.SemaphoreType.DMA((N_DEV - 1,)),
                        pltpu.SemaphoreType.DMA((N_DEV - 1,))],
    )(s)


def _kv_weights(g_kv):
    d = g_kv.shape[1]
    nh = d // HEAD
    wkv = jnp.reshape(jnp.transpose(g_kv, (1, 0, 2)), (d, -1))
    return wkv[:, :2 * d], jnp.pad(wkv[:, 2 * d:], ((0, 0), (0, 128 - nh)))


def _kv_grad_blocks(dwkv2_t, dwfl_t):
    d = dwkv2_t.shape[1]
    return jnp.reshape(jnp.concatenate([dwkv2_t, dwfl_t[:d // HEAD]], axis=0), (N_DEV, -1, d))


def _local_step(x, target, meta, gamma, a_norm, wa_in, a_out_norm, wa_out, kv_norm, late, b_f, g_k, b_norm, g_q,
                dist):
    d = x.shape[1]
    nh = d // HEAD
    cols = d // N_DEV
    big = dict(tm=1408, tn=512, tk=2048)

    lb = _lb_fwd(gamma)
    if dist:
        chip = jnp.reshape(2 * lax.axis_index("x") + lax.axis_index("y"), (1,)).astype(jnp.int32)
        h0, hn_a, ((wa_in, wa_pair),) = _embed(x, meta, a_norm, sides=[_gather_near_side([wa_in])])
    else:
        h0, hn_a = _embed(x, meta, a_norm)
    lp = h0.shape[0]
    if dist:
        a_out_blk, kv_blk, b_in_blk, b_out_blk = late
        u4, ((wa_in,),) = _a_in(hn_a, wa_pair, chip, 0, 1, "a_in_near", sides=[_gather_far_side([wa_in])])
        u4, (first,) = _a_in(hn_a, wa_in, chip, 1, 3, "a_in_far", into=u4,
                             sides=[_gather_own_side([a_out_blk, b_in_blk])])
        o_a, og_a, ((g_a_out, g_b_in), (g_kv,)) = _hgrn2_fwd(
            u4, lb, a_out_norm, sides=[_gather_pass_side(first), _gather_own_side([kv_blk])])
        wa_out = jnp.reshape(g_a_out, (d, d))
        h1, ((g_kv,), (g_b_out,)) = _matmul(og_a, wa_out, "nn", F32, "a_out", add=h0,
                                            sides=[_gather_pass_side([g_kv]), _gather_own_side([b_out_blk])], **big)
        (wkv2, wfl), wb_in = _kv_weights(g_kv), g_b_in
    else:
        u4 = _matmul(hn_a, wa_in, "nn", F32, "a_in", out_parts=4, **big)
        o_a, og_a = _hgrn2_fwd(u4, lb, a_out_norm)
        h1 = _matmul(og_a, wa_out, "nn", F32, "a_out", add=h0, **big)
        wkv2, wfl, wb_in, wb_out = late
    hk, hb = _rms_fwd(h1, [kv_norm, b_norm], "rms_kv_b")
    if dist:
        ukv, ((g_b_out,),) = _matmul(hk, wkv2, "nn", F32, "kv_in", out_parts=2,
                                     sides=[_gather_pass_side([g_b_out])], **big)
        wb_out = jnp.reshape(g_b_out, (d, d))
    else:
        ukv = _matmul(hk, wkv2, "nn", F32, "kv_in", out_parts=2, **big)
    ufl = _matmul(hk, wfl, "nn", F32, "kv_f", **big)
    ub = _matmul(hb, wb_in, "nn", F32, "b_in", out_parts=2, **big)
    f_cum = _fgate_fwd(ufl, b_f)
    qa, ka, va = _attn_operands(ub, ukv, g_q, g_k, f_cum)
    o_b, o_lo, og_b, qb = _fox_fwd(qa, ka, va, ub)
    h2 = _matmul(og_b, wb_out, "nn", F32, "b_out", add=h1, **big)
    loss, dh2 = _loss(h2, target)

    dx_t = dict(tm=1408, tn=512, tk=2048, vmem_mib=56)
    dx_parts_t = dict(tm=704, tn=512, tk=2048, vmem_mib=58, k_whole=True)
    dw_t = dict(tm=512, tn=1024, tk=lp, vmem_mib=58)
    dw_f32_t = dict(tm=1024, tn=512, tk=lp, vmem_mib=58)
    def to_sibling(g8):
        return [_sibling_side([g8])] if dist else []

    def to_chips(partial):
        return [_chips_side([partial])] if dist else []

    def unpack(res, n_sides):
        if not dist:
            return res, [None] * n_sides
        *main, side_results = res
        return (main[0] if len(main) == 1 else tuple(main)), [r[0] for r in side_results]

    dwb_out = _matmul(og_b, dh2, "tn", F32, "b_out_dw", **dw_f32_t)
    g8_b_out = jnp.reshape(dwb_out, (N_DEV, cols, d))
    d_ogb, (got,) = unpack(_matmul(dh2, wb_out, "nt", F32, "b_out_dx", sides=to_sibling(g8_b_out), **dx_t), 1)
    p_b_out = _pair_sum(g8_b_out, got, "pair_sum_b_w_out") if dist else None
    doa = _fox_do(d_ogb, o_b, o_lo, ub)
    (dq, dg_q, dukv, dg_k, dfk), (r_b_out,) = unpack(
        _fox_bwd(qb, ka, va, doa, ukv, g_k, ub, g_q, sides=to_chips(p_b_out)), 1)
    dub = _fox_gate_bwd(ub, d_ogb, o_b, dq)
    d_f = jnp.pad(jnp.transpose(dfk[:, 0, :]), ((0, 0), (0, 128 - nh)))
    dufl, db_f = _fgate_bwd(ufl, b_f, d_f)
    dwb_in = _matmul(hb, dub, "tn", F32, "b_in_dw", out_parts=N_DEV, **dw_t)
    d_hb, (got,) = unpack(_matmul(dub, wb_in, "nt", F32, "b_in_dx", sides=to_sibling(dwb_in), **dx_parts_t), 1)
    p_b_in = _pair_sum(dwb_in, got, "pair_sum_b_w_in") if dist else None
    d_hk, (r_b_in,) = unpack(_matmul(dukv, wkv2, "nt", F32, "kv_dx", sides=to_chips(p_b_in), **dx_parts_t), 1)
    d_hk = _matmul(dufl, wfl, "nt", F32, "kv_f_dx", add=d_hk, **dx_t)
    dwkv2_t = _matmul(dukv, hk, "tn", F32, "kv_dw", **dw_t)
    dwfl_t = _matmul(dufl, hk, "tn", F32, "kv_f_dw", **dw_t)
    g8_kv = _kv_grad_blocks(dwkv2_t, dwfl_t) if dist else None
    dh1, (dg_kv, dg_b) = _rms_bwd(h1, [kv_norm, b_norm], [d_hk, d_hb], dh2, "rms_kv_b_bwd")
    d_oga, (got,) = unpack(_matmul(dh1, wa_out, "nt", F32, "a_out_dx", sides=to_sibling(g8_kv), **dx_t), 1)
    p_kv = _pair_sum(g8_kv, got, "pair_sum_kv_w") if dist else None
    dwa_out = _matmul(og_a, dh1, "tn", F32, "a_out_dw", **dw_f32_t)
    g8_a_out = jnp.reshape(dwa_out, (N_DEV, cols, d))
    (du4, dlb, dg_aout), (r_kv, got) = unpack(
        _hgrn2_bwd(u4, o_a, d_oga, lb, a_out_norm, sides=to_chips(p_kv) + to_sibling(g8_a_out)), 2)
    p_a_out = _pair_sum(g8_a_out, got, "pair_sum_a_w_out") if dist else None
    dwa_in, (r_a_out,) = unpack(
        _matmul(hn_a, du4, "tn", F32, "a_in_dw", out_parts=N_DEV, sides=to_chips(p_a_out), **dw_t), 1)
    row_tiles = lp // _tile(lp, dx_parts_t["tm"], 64)
    if dist and row_tiles >= 2:
        top = max(1, row_tiles // 6)
        d_hna, ((got,),) = _matmul(du4, wa_in, "nt", F32, "a_in_dx_top", rows=(0, top),
                                   sides=to_sibling(dwa_in), **dx_parts_t)
        p_a_in = _pair_sum(dwa_in, got, "pair_sum_a_w_in")
        d_hna, ((r_a_in,),) = _matmul(du4, wa_in, "nt", F32, "a_in_dx_bottom", rows=(top, row_tiles - top), into=d_hna,
                                      sides=to_chips(p_a_in), **dx_parts_t)
    else:
        d_hna, (got,) = unpack(_matmul(du4, wa_in, "nt", F32, "a_in_dx", sides=to_sibling(dwa_in), **dx_parts_t), 1)
        if dist:
            (r_a_in,) = _alone(_chips_side([_pair_sum(dwa_in, got, "pair_sum_a_w_in")]), "grads_to_chips_a_w_in")
    grad_x, dmeta, dg_a = _embed_bwd(h0, a_norm, d_hna, dh1)
    dgamma = _lb_bwd(gamma, dlb)

    grads = dict(meta=dmeta, gamma=dgamma, a_norm=dg_a, a_out_norm=dg_aout, kv_norm=dg_kv,
                 b_f=db_f, g_k=dg_k, b_norm=dg_b, g_q=dg_q)
    if dist:
        grads.update(wa_in=r_a_in, wa_out=r_a_out, wkv=r_kv, wb_in=r_b_in, wb_out=r_b_out)
    else:
        grads.update(wa_in=dwa_in, wa_out=dwa_out, wkv2=dwkv2_t.T, wfl=dwfl_t.T, wb_in=dwb_in, wb_out=dwb_out)
    return loss, grad_x, grads


def kernel(x, meta, gamma_lb, a_norm, a_w_in, a_out_norm, a_w_out, kv_norm, kv_w, fox_b_f, fox_k_norm, b_norm, b_w_in, b_q_norm, b_w_out, loss_target, m_meta, m_gamma_lb, m_a_norm, m_a_w_in, m_a_out_norm, m_a_w_out, m_kv_norm, m_kv_w, m_fox_b_f, m_fox_k_norm, m_b_norm, m_b_w_in, m_b_q_norm, m_b_w_out, v_meta, v_gamma_lb, v_a_norm, v_a_w_in, v_a_out_norm, v_a_w_out, v_kv_norm, v_kv_w, v_fox_b_f, v_fox_k_norm, v_b_norm, v_b_w_in, v_b_q_norm, v_b_w_out):
    d = x.shape[-1]
    nh = d // HEAD
    cols = d // N_DEV
    me = 4 * lax.axis_index("x") + 2 * lax.axis_index("y") + lax.axis_index("c")

    sharded_small = jnp.concatenate([meta, gamma_lb, a_norm, a_out_norm, jnp.zeros((4, cols), F32)], axis=0)
    (g_small,) = _all_gather([sharded_small], "gather_small")
    g_a_in, wa_out = a_w_in[0].astype(MXU), None
    late = (a_w_out[0].astype(MXU), kv_w.astype(MXU), b_w_in[0].astype(MXU), b_w_out[0].astype(MXU))
    small = jnp.reshape(jnp.transpose(g_small, (1, 0, 2)), (-1, d))
    meta_f, gamma_f, a_norm_f, a_out_norm_f = small[:16], small[16:18], small[18:19], small[19:20]
    b_f = jnp.pad(jnp.reshape(fox_b_f, (1, nh)), ((0, 0), (0, 128 - nh)))
    g_k = jnp.reshape(fox_k_norm, (1, d))
    g_q = jnp.reshape(b_q_norm, (1, d))
    kv_norm_r = jnp.reshape(kv_norm, (1, d))

    loss, grad_x, g = _local_step(x[0], loss_target[0], meta_f, gamma_f, a_norm_f, g_a_in, a_out_norm_f, wa_out,
                                  kv_norm_r, late, b_f, g_k, b_norm, g_q, dist=True)
    loss = lax.psum(loss[0, 0], AXES)

    r_a_in = _adamw(g["wa_in"], a_w_in[0], m_a_w_in[0], v_a_w_in[0], "adamw_a_w_in")
    r_a_out = _adamw(g["wa_out"], a_w_out[0], m_a_w_out[0], v_a_w_out[0], "adamw_a_w_out")
    r_kv = [jnp.transpose(r) for r in _adamw(g["wkv"], kv_w.T, m_kv_w.T, v_kv_w.T, "adamw_kv_w")]
    r_b_in = _adamw(g["wb_in"], b_w_in[0], m_b_w_in[0], v_b_w_in[0], "adamw_b_w_in")
    r_b_out = _adamw(g["wb_out"], b_w_out[0], m_b_w_out[0], v_b_w_out[0], "adamw_b_w_out")

    packed = jnp.concatenate(
        [g["meta"], g["gamma"], g["a_norm"], g["a_out_norm"], g["kv_norm"], g["b_norm"], g["g_k"], g["g_q"],
         jnp.pad(g["b_f"], ((0, 0), (0, d - 128))), jnp.zeros((7, d), F32)], axis=0)
    tot = _all_reduce_small(packed)
    mine = lax.dynamic_slice_in_dim(tot[:20], me * cols, cols, axis=1)
    gs = [mine[:16], mine[16:18], mine[18:19], mine[19:20], tot[20], tot[21:22], jnp.reshape(tot[22], (nh, HEAD)),
          jnp.reshape(tot[23], (1, nh, HEAD)), tot[24, :nh]]
    small_w = [meta, gamma_lb, a_norm, a_out_norm, kv_norm, b_norm, fox_k_norm, b_q_norm, fox_b_f]
    small_m = [m_meta, m_gamma_lb, m_a_norm, m_a_out_norm, m_kv_norm, m_b_norm, m_fox_k_norm, m_b_q_norm, m_fox_b_f]
    small_v = [v_meta, v_gamma_lb, v_a_norm, v_a_out_norm, v_kv_norm, v_b_norm, v_fox_k_norm, v_b_q_norm, v_fox_b_f]

    d_s, m_s, v_s = _adamw_small(gs, small_w, small_m, small_v)

    def ordered(s, a_in, a_out, kv, b_in, b_out):
        return [s[0], s[1], s[2], a_in[None], s[3], a_out[None], s[4], kv, s[8], s[6], s[5], b_in[None], s[7], b_out[None]]

    outs = []
    for i, s in enumerate([gs, d_s, m_s, v_s]):
        outs += ordered(s, r_a_in[i], r_a_out[i], r_kv[i], r_b_in[i], r_b_out[i])
    return (loss, grad_x[None], *outs)
```
